```python
import math
import jax
import jax.numpy as jnp
from jax import lax
import numpy as np

D_MODEL = 1024
BATCH = 2
SEQ = 16384
DEPTH = 2

GRID_W = 64
CTX_LEN = 256
N_MOD = 9
D_FF = 2816
RMS_EPS = 1e-6
LN_EPS = 1e-5
F32 = jnp.float32
NEG_INF = -1e30

GLA_HEADS = 4
GLA_DK = 32
GLA_DV = 64
GLA_RANK = 16
GLA_GATE_TAU = 16.0
GLA_CHUNK = 64

NA_HEADS = 4
NA_DIM = 64
NA_ROWS = 8
NA_COLS = 16

DIFF_HEADS = 4
DIFF_DK = 32
DIFF_DV = 64
Q_BLOCK = 128
ROPE_BASE = 10000.0

CONV_CH = 256
CONV_K = 31

D_MIX = GLA_HEADS * GLA_DV + NA_HEADS * NA_DIM + DIFF_HEADS * DIFF_DV + CONV_CH
PROJ_SIZES = (GLA_HEADS * GLA_DK, GLA_HEADS * GLA_DK, GLA_HEADS * GLA_DV, GLA_HEADS * GLA_DV, GLA_RANK, GLA_RANK,
              NA_HEADS * NA_DIM, NA_HEADS * NA_DIM, NA_HEADS * NA_DIM,
              DIFF_HEADS * 2 * DIFF_DK, DIFF_HEADS * 2 * DIFF_DK, DIFF_HEADS * DIFF_DV,
              2 * CONV_CH)
D_PROJ = sum(PROJ_SIZES)

kernel_name = 'hymba_style_hybrid_dit_trunk'


def rms_norm(x, w):
    xf = x.astype(F32)
    y = xf * lax.rsqrt(jnp.mean(xf * xf, axis=-1, keepdims=True) + RMS_EPS)
    return (y * w.astype(F32)).astype(x.dtype)


def layer_norm(x, g, b):
    xf = x.astype(F32)
    mu = jnp.mean(xf, axis=-1, keepdims=True)
    xc = xf - mu
    var = jnp.mean(xc * xc, axis=-1, keepdims=True)
    return (xc * lax.rsqrt(var + LN_EPS) * g.astype(F32) + b.astype(F32)).astype(x.dtype)


def modulate(x, shift, scale):
    return x * (1.0 + scale) + shift


def swiglu(h, w13, w2):
    a, u = jnp.split(h @ w13, 2, axis=-1)
    return (jax.nn.silu(a) * u) @ w2


def split_projection(z):
    offsets, acc = [], 0
    for s in PROJ_SIZES[:-1]:
        acc += s
        offsets.append(acc)
    return jnp.split(z, offsets, axis=-1)


def axial_rope_tables(seq, dim):
    t = jnp.arange(seq)
    row = (t // GRID_W).astype(F32)
    col = (t % GRID_W).astype(F32)
    half = dim // 2
    inv = 1.0 / (ROPE_BASE ** (jnp.arange(0, half, 2, dtype=F32) / half))
    ang_r = row[:, None] * inv
    ang_c = col[:, None] * inv
    ang = jnp.concatenate([ang_r, ang_r, ang_c, ang_c], axis=-1)
    return jnp.cos(ang), jnp.sin(ang)


def apply_rope(x, cos, sin):
    half = x.shape[-1] // 2

    def rot(u):
        u1, u2 = jnp.split(u, 2, axis=-1)
        return jnp.concatenate([-u2, u1], axis=-1)

    rotated = jnp.concatenate([rot(x[..., :half]), rot(x[..., half:])], axis=-1)
    cos = cos[None, :, None, :].astype(x.dtype)
    sin = sin[None, :, None, :].astype(x.dtype)
    return x * cos + rotated * sin


def softmax_attention(q, k, v):
    s = jnp.einsum('bqhd,bkhd->bhqk', q, k).astype(F32) * (q.shape[-1] ** -0.5)
    p = jax.nn.softmax(s, axis=-1).astype(v.dtype)
    return jnp.einsum('bhqk,bkhd->bqhd', p, v)


def gla_chunked(q, k, v, lg, s0):
    b_, l_, h_, dk = q.shape
    dv = v.shape[-1]
    n = l_ // GLA_CHUNK
    q, k, lg = [t.reshape(b_, n, GLA_CHUNK, h_, dk) for t in (q, k, lg)]
    v = v.reshape(b_, n, GLA_CHUNK, h_, dv)
    bcum = jnp.cumsum(lg, axis=2)
    blast = bcum[:, :, -1]
    q_in = q * jnp.exp(bcum)
    k_in = k * jnp.exp(-bcum)
    causal = jnp.tril(jnp.ones((GLA_CHUNK, GLA_CHUNK), dtype=bool))
    a = jnp.where(causal, jnp.einsum('bnihd,bnjhd->bnhij', q_in, k_in), 0.0)
    o_intra = jnp.einsum('bnhij,bnjhv->bnihv', a, v)
    k_out = k * jnp.exp(blast[:, :, None] - bcum)
    u = jnp.einsum('bnjhd,bnjhv->bnhdv', k_out, v)
    decay = jnp.exp(blast)

    def step(s, inp):
        dc, uc = inp
        return dc[..., None] * s + uc, s

    s_fin, s_prev = lax.scan(step, s0, (jnp.moveaxis(decay, 1, 0), jnp.moveaxis(u, 1, 0)))
    s_prev = jnp.moveaxis(s_prev, 0, 1)
    o_inter = jnp.einsum('bnihd,bnhdv->bnihv', q_in, s_prev)
    return (o_intra + o_inter).reshape(b_, l_, h_, dv), s_fin


def gla_final_state(k, v, lg):
    bcum = jnp.cumsum(lg, axis=1)
    kk = k * jnp.exp(bcum[:, -1:] - bcum)
    return jnp.einsum('blhd,blhv->bhdv', kk, v)


def gla_output(o, g, norm_w):
    b_, l_ = o.shape[:2]
    return rms_norm(o, norm_w).reshape(b_, l_, GLA_HEADS * GLA_DV) * jax.nn.silu(g)


def gla_group(px, pc, wa_f, ba_f, wa_b, ba_b, norm_w, ctx_out):
    def prep(p):
        q, k, v, g, a_f, a_b = p
        b_, l_ = q.shape[:2]
        q = q.reshape(b_, l_, GLA_HEADS, GLA_DK) * (GLA_DK ** -0.5)
        k = k.reshape(b_, l_, GLA_HEADS, GLA_DK)
        v = v.reshape(b_, l_, GLA_HEADS, GLA_DV)
        lg_f = (jax.nn.log_sigmoid(a_f @ wa_f + ba_f) / GLA_GATE_TAU).reshape(b_, l_, GLA_HEADS, GLA_DK)
        lg_b = (jax.nn.log_sigmoid(a_b @ wa_b + ba_b) / GLA_GATE_TAU).reshape(b_, l_, GLA_HEADS, GLA_DK)
        return q, k, v, g, lg_f, lg_b

    def flip(t):
        return jnp.flip(t, axis=1)

    qx, kx, vx, gx, fx, bx = prep(px)
    qc, kc, vc, gc, fc, bc = prep(pc)
    if ctx_out:
        zero = jnp.zeros((qc.shape[0], GLA_HEADS, GLA_DK, GLA_DV), qc.dtype)
        oc_f, sc_f = gla_chunked(qc, kc, vc, fc, zero)
        oc_b, sc_b = gla_chunked(flip(qc), flip(kc), flip(vc), flip(bc), zero)
        oc = gla_output(oc_f + flip(oc_b), gc, norm_w)
    else:
        sc_f = gla_final_state(kc, vc, fc)
        sc_b = gla_final_state(flip(kc), flip(vc), flip(bc))
        oc = None
    ox_f, _ = gla_chunked(qx, kx, vx, fx, sc_f)
    ox_b, _ = gla_chunked(flip(qx), flip(kx), flip(vx), flip(bx), sc_b)
    return gla_output(ox_f + flip(ox_b), gx, norm_w), oc


def neighborhood_attention(q, k, v, kc, vc, rpb, rows):
    b_, l_, h_, d = q.shape
    wr = min(NA_ROWS, rows)
    r = jnp.arange(rows)
    row_start = jnp.clip(r - wr // 2, 0, rows - wr)
    row_idx = row_start[:, None] + jnp.arange(wr)
    qg = q.reshape(b_, rows, GRID_W, h_, d) * (d ** -0.5)
    kg = k.reshape(b_, rows, GRID_W, h_, d)[:, row_idx]
    vg = v.reshape(b_, rows, GRID_W, h_, d)[:, row_idx]
    col = jnp.arange(GRID_W)
    col_start = jnp.clip(col - NA_COLS // 2, 0, GRID_W - NA_COLS)
    in_win = (col[None, :] >= col_start[:, None]) & (col[None, :] < col_start[:, None] + NA_COLS)
    row_off = row_idx - r[:, None] + (NA_ROWS - 1)
    col_off = jnp.clip(col[None, :] - col[:, None] + (NA_COLS - 1), 0, 2 * NA_COLS - 2)
    bias = rpb[:, row_off][..., col_off]
    bias = jnp.transpose(bias, (1, 0, 3, 2, 4)).astype(F32)
    s_win = jnp.einsum('brqhd,brikhd->brhqik', qg, kg).astype(F32) + bias
    s_win = jnp.where(in_win[:, None, :], s_win, NEG_INF)
    s_ctx = jnp.einsum('brqhd,bchd->brhqc', qg, kc).astype(F32)
    nwin = wr * GRID_W
    s = jnp.concatenate([s_win.reshape(b_, rows, h_, GRID_W, nwin), s_ctx], axis=-1)
    p = jax.nn.softmax(s, axis=-1).astype(v.dtype)
    p_win = p[..., :nwin].reshape(b_, rows, h_, GRID_W, wr, GRID_W)
    out = (jnp.einsum('brhqik,brikhd->brqhd', p_win, vg)
           + jnp.einsum('brhqc,bchd->brqhd', p[..., nwin:], vc))
    return out.reshape(b_, l_, h_ * d)


def na_group(px, pc, rpb, rows, ctx_out):
    def heads(p):
        b_, l_ = p[0].shape[:2]
        return [t.reshape(b_, l_, NA_HEADS, NA_DIM) for t in p]

    q, k, v = heads(px)
    qc, kc, vc = heads(pc)
    ox = neighborhood_attention(q, k, v, kc, vc, rpb, rows)
    if not ctx_out:
        return ox, None
    oc = softmax_attention(qc, kc, vc)
    return ox, oc.reshape(oc.shape[0], oc.shape[1], NA_HEADS * NA_DIM)


def diff_softmax_pair(q1, q2, k1, k2, v, lam):
    scale = DIFF_DK ** -0.5
    s1 = jnp.einsum('bqhd,bkhd->bhqk', q1, k1).astype(F32) * scale
    s2 = jnp.einsum('bqhd,bkhd->bhqk', q2, k2).astype(F32) * scale
    p = jax.nn.softmax(s1, axis=-1) - lam * jax.nn.softmax(s2, axis=-1)
    return jnp.einsum('bhqk,bkhd->bqhd', p.astype(v.dtype), v)


def diff_output(o, norm_w, lambda_init):
    b_, l_ = o.shape[:2]
    return (rms_norm(o, norm_w) * (1.0 - lambda_init)).reshape(b_, l_, DIFF_HEADS * DIFF_DV)


def diff_group(px, pc, lq1, lk1, lq2, lk2, norm_w, lambda_init, cos, sin, ctx_out):
    def heads(p):
        q, k, v = p
        b_, l_ = q.shape[:2]
        q = q.reshape(b_, l_, DIFF_HEADS, 2, DIFF_DK)
        k = k.reshape(b_, l_, DIFF_HEADS, 2, DIFF_DK)
        return q[..., 0, :], q[..., 1, :], k[..., 0, :], k[..., 1, :], v.reshape(b_, l_, DIFF_HEADS, DIFF_DV)

    q1, q2, k1, k2, v = heads(px)
    q1, q2, k1, k2 = [apply_rope(t, cos, sin) for t in (q1, q2, k1, k2)]
    q1c, q2c, k1c, k2c, vc = heads(pc)
    lam = (jnp.exp(jnp.sum(lq1.astype(F32) * lk1.astype(F32)))
           - jnp.exp(jnp.sum(lq2.astype(F32) * lk2.astype(F32))) + lambda_init)
    k1a = jnp.concatenate([k1c, k1], axis=1)
    k2a = jnp.concatenate([k2c, k2], axis=1)
    va = jnp.concatenate([vc, v], axis=1)
    b_, l_ = q1.shape[:2]
    nb = l_ // Q_BLOCK

    def blocks(t):
        return jnp.swapaxes(t.reshape(b_, nb, Q_BLOCK, DIFF_HEADS, DIFF_DK), 0, 1)

    o = lax.map(lambda qs: diff_softmax_pair(qs[0], qs[1], k1a, k2a, va, lam), (blocks(q1), blocks(q2)))
    o = jnp.swapaxes(o, 0, 1).reshape(b_, l_, DIFF_HEADS, DIFF_DV)
    ox = diff_output(o, norm_w, lambda_init)
    if not ctx_out:
        return ox, None
    return ox, diff_output(diff_softmax_pair(q1c, q2c, k1c, k2c, vc, lam), norm_w, lambda_init)


def conformer_conv(z, dw, dw_b, ln_g, ln_b, pw, pw_b):
    a, gte = jnp.split(z, 2, axis=-1)
    u = a * jax.nn.sigmoid(gte)
    u = lax.conv_general_dilated(u, dw[:, None, :], window_strides=(1,),
                                 padding=[(CONV_K // 2, CONV_K // 2)],
                                 dimension_numbers=('NWC', 'WIO', 'NWC'),
                                 feature_group_count=CONV_CH) + dw_b
    u = jax.nn.silu(layer_norm(u, ln_g, ln_b))
    return u @ pw + pw_b


def token_mixing(hx, hc, w_in, w_out, gla_wa_f, gla_ba_f, gla_wa_b, gla_ba_b, gla_norm, na_rpb,
                 diff_lq1, diff_lk1, diff_lq2, diff_lk2, diff_norm, conv_dw, conv_dw_b, conv_ln_g,
                 conv_ln_b, conv_pw, conv_pw_b, lambda_init, cos, sin, rows, ctx_out):
    px = split_projection(hx @ w_in)
    pc = split_projection(hc @ w_in)
    gx, gc = gla_group(px[0:6], pc[0:6], gla_wa_f, gla_ba_f, gla_wa_b, gla_ba_b, gla_norm, ctx_out)
    nx, nc = na_group(px[6:9], pc[6:9], na_rpb, rows, ctx_out)
    dx, dc = diff_group(px[9:12], pc[9:12], diff_lq1, diff_lk1, diff_lq2, diff_lk2, diff_norm,
                        lambda_init, cos, sin, ctx_out)
    cx = conformer_conv(px[12], conv_dw, conv_dw_b, conv_ln_g, conv_ln_b, conv_pw, conv_pw_b)
    out_x = jnp.concatenate([gx, nx, dx, cx], axis=-1) @ w_out
    if not ctx_out:
        return out_x, None
    cc = conformer_conv(pc[12], conv_dw, conv_dw_b, conv_ln_g, conv_ln_b, conv_pw, conv_pw_b)
    out_c = jnp.concatenate([gc, nc, dc, cc], axis=-1) @ w_out
    return out_x, out_c


def setup_inputs(seed: int = 0) -> dict:
    key = jax.random.key(seed)
    ks = iter(jax.random.split(key, 40))

    def nrm(shape, s):
        return jax.random.normal(next(ks), shape, F32) * s

    L, D = DEPTH, D_MODEL
    return {
        'x': nrm((BATCH, SEQ, D), 1.0),
        'c': nrm((BATCH, D), 1.0),
        'ctx': nrm((BATCH, CTX_LEN, D), 1.0),
        'c_ctx': nrm((D,), 1.0),
        'ada_w': nrm((L, D, N_MOD * D), 0.5 * D ** -0.5),
        'ada_b': nrm((L, N_MOD * D), 0.02),
        'norm_ffn1': 1.0 + nrm((L, D), 0.02),
        'ffn1_w13': nrm((L, D, 2 * D_FF), D ** -0.5),
        'ffn1_w2': nrm((L, D_FF, D), D_FF ** -0.5),
        'norm_mix': 1.0 + nrm((L, D), 0.02),
        'w_in': nrm((L, D, D_PROJ), D ** -0.5),
        'gla_wa_f': nrm((L, GLA_RANK, GLA_HEADS * GLA_DK), GLA_RANK ** -0.5),
        'gla_ba_f': nrm((L, GLA_HEADS * GLA_DK), 0.1),
        'gla_wa_b': nrm((L, GLA_RANK, GLA_HEADS * GLA_DK), GLA_RANK ** -0.5),
        'gla_ba_b': nrm((L, GLA_HEADS * GLA_DK), 0.1),
        'gla_norm': 1.0 + nrm((L, GLA_DV), 0.02),
        'na_rpb': nrm((L, NA_HEADS, 2 * NA_ROWS - 1, 2 * NA_COLS - 1), 0.02),
        'diff_lq1': nrm((L, DIFF_DK), 0.1),
        'diff_lk1': nrm((L, DIFF_DK), 0.1),
        'diff_lq2': nrm((L, DIFF_DK), 0.1),
        'diff_lk2': nrm((L, DIFF_DK), 0.1),
        'diff_norm': 1.0 + nrm((L, DIFF_DV), 0.02),
        'conv_dw': nrm((L, CONV_K, CONV_CH), CONV_K ** -0.5),
        'conv_dw_b': nrm((L, CONV_CH), 0.02),
        'conv_ln_g': 1.0 + nrm((L, CONV_CH), 0.02),
        'conv_ln_b': nrm((L, CONV_CH), 0.02),
        'conv_pw': nrm((L, CONV_CH, CONV_CH), CONV_CH ** -0.5),
        'conv_pw_b': nrm((L, CONV_CH), 0.02),
        'w_out': nrm((L, D_MIX, D), D_MIX ** -0.5),
        'norm_ffn2': 1.0 + nrm((L, D), 0.02),
        'ffn2_w13': nrm((L, D, 2 * D_FF), D ** -0.5),
        'ffn2_w2': nrm((L, D_FF, D), D_FF ** -0.5),
        'final_norm': 1.0 + nrm((D,), 0.02),
    }


def reference(x, c, ctx, c_ctx, ada_w, ada_b, norm_ffn1, ffn1_w13, ffn1_w2, norm_mix, w_in,
              gla_wa_f, gla_ba_f, gla_wa_b, gla_ba_b, gla_norm, na_rpb, diff_lq1, diff_lk1,
              diff_lq2, diff_lk2, diff_norm, conv_dw, conv_dw_b, conv_ln_g, conv_ln_b, conv_pw,
              conv_pw_b, w_out, norm_ffn2, ffn2_w13, ffn2_w2, final_norm):
    seq = x.shape[1]
    rows = seq // GRID_W
    cos, sin = axial_rope_tables(seq, DIFF_DK)
    h, hc = x, ctx
    for i in range(DEPTH):
        last = i == DEPTH - 1
        lambda_init = 0.8 - 0.6 * math.exp(-0.3 * i)
        mx = jnp.split((jax.nn.silu(c) @ ada_w[i] + ada_b[i])[:, None, :], N_MOD, axis=-1)
        mc = jnp.split(jax.nn.silu(c_ctx) @ ada_w[i] + ada_b[i], N_MOD, axis=-1)
        h = h + 0.5 * mx[2] * swiglu(modulate(rms_norm(h, norm_ffn1[i]), mx[0], mx[1]), ffn1_w13[i], ffn1_w2[i])
        hc = hc + 0.5 * mc[2] * swiglu(modulate(rms_norm(hc, norm_ffn1[i]), mc[0], mc[1]), ffn1_w13[i], ffn1_w2[i])
        ox, oc = token_mixing(modulate(rms_norm(h, norm_mix[i]), mx[3], mx[4]),
                              modulate(rms_norm(hc, norm_mix[i]), mc[3], mc[4]),
                              w_in[i], w_out[i], gla_wa_f[i], gla_ba_f[i], gla_wa_b[i], gla_ba_b[i],
                              gla_norm[i], na_rpb[i], diff_lq1[i], diff_lk1[i], diff_lq2[i], diff_lk2[i],
                              diff_norm[i], conv_dw[i], conv_dw_b[i], conv_ln_g[i], conv_ln_b[i],
                              conv_pw[i], conv_pw_b[i], lambda_init, cos, sin, rows, not last)
        h = h + mx[5] * ox
        if not last:
            hc = hc + mc[5] * oc
            hc = hc + 0.5 * mc[8] * swiglu(modulate(rms_norm(hc, norm_ffn2[i]), mc[6], mc[7]), ffn2_w13[i], ffn2_w2[i])
        h = h + 0.5 * mx[8] * swiglu(modulate(rms_norm(h, norm_ffn2[i]), mx[6], mx[7]), ffn2_w13[i], ffn2_w2[i])
    return rms_norm(h, final_norm)
```

```python
import functools
import math

import jax
import jax.numpy as jnp
from jax import lax
from jax.experimental import pallas as pl
from jax.experimental.pallas import tpu as pltpu

F32 = jnp.float32
BF16 = jnp.bfloat16

GRID_W = 64
N_MOD = 9
RMS_EPS = 1e-6
LN_EPS = 1e-5
NEG_INF = -1e30
GLA_HEADS, GLA_DK, GLA_DV, GLA_RANK, GLA_TAU, GLA_CHUNK = 4, 32, 64, 16, 16.0, 64
NA_HEADS, NA_DIM, NA_ROWS, NA_COLS = 4, 64, 8, 16
DIFF_HEADS, DIFF_DK, DIFF_DV = 4, 32, 64
CONV_CH, CONV_K = 256, 31
ROPE_BASE = 10000.0

LANES = 128
VMEM_LIMIT = 56 * 1024 * 1024

ROW_TILE = 512
FF_CHUNK = 256
GLA_BLOCK = 256
NA_QROWS = 8
NA_KBLK = 256
CONV_TILE = 256
DIFF_TQ = 256
DIFF_TK = 512
HALO = 16

C_GLA, C_NA, C_DIFF, C_CONV, C_AUX, C_END = 0, 768, 1536, 2304, 2816, 2944


def _dot(a, b):
    return jnp.dot(a, b, preferred_element_type=F32)


def _dot_nt(a, b):
    return lax.dot_general(a, b, (((1,), (1,)), ((), ())), preferred_element_type=F32)


def _params(*sem):
    return pltpu.CompilerParams(dimension_semantics=sem, vmem_limit_bytes=VMEM_LIMIT)


def _resident(shape):
    nd = len(shape)
    return pl.BlockSpec(shape, lambda *_: (0,) * nd, pipeline_mode=pl.Buffered(1))


def _silu(x):
    return x * jax.nn.sigmoid(x)


def _rms(x, w):
    return x * lax.rsqrt(jnp.mean(x * x, axis=-1, keepdims=True) + RMS_EPS) * w


def _ada_body(c_ref, w_ref, b_ref, o_ref):
    s = _silu(c_ref[...])
    o_ref[...] = jnp.dot(s, w_ref[...], precision=lax.Precision.HIGHEST,
                         preferred_element_type=F32) + b_ref[...]


def _ada(c_rows, ada_w, ada_b):
    depth, d, _ = ada_w.shape
    return pl.pallas_call(
        _ada_body,
        grid=(depth, N_MOD),
        in_specs=[pl.BlockSpec((8, d), lambda l, n: (0, 0)),
                  pl.BlockSpec((None, d, d), lambda l, n: (l, 0, n)),
                  pl.BlockSpec((None, 1, d), lambda l, n: (l, 0, n))],
        out_specs=pl.BlockSpec((None, 8, d), lambda l, n: (l, 0, n)),
        out_shape=jax.ShapeDtypeStruct((depth, 8, N_MOD * d), F32),
        compiler_params=_params("arbitrary", "arbitrary"),
        name="ada",
    )(c_rows, ada_w, ada_b.reshape(depth, 1, N_MOD * d))


def _ffn_body(h_ref, mod_ref, nw_ref, w13_ref, w2_ref, fw_ref, o_ref, xb_ref, acc_ref, *, k0, final):
    n_chunks, _, two_tf = w13_ref.shape
    tf = two_tf // 2
    x = h_ref[...]
    xm = _rms(x, nw_ref[...]) * (1.0 + mod_ref[k0 + 1:k0 + 2, :]) + mod_ref[k0:k0 + 1, :]
    xb_ref[...] = xm.astype(BF16)
    acc_ref[...] = jnp.zeros_like(acc_ref)

    def chunk(c, carry):
        h13 = _dot(xb_ref[...], w13_ref[c])
        a = h13[:, :tf]
        u = h13[:, tf:]
        acc_ref[...] += _dot((_silu(a) * u).astype(BF16), w2_ref[c])
        return carry

    lax.fori_loop(0, n_chunks, chunk, 0)
    out = x + (0.5 * mod_ref[k0 + 2:k0 + 3, :]) * acc_ref[...]
    if final:
        out = _rms(out, fw_ref[...])
    o_ref[...] = out


def _ffn(h, mods, norm_w, w13c, w2c, final_w, *, k0, n_tiles, group_of, final):
    r, d = h.shape
    tm = ROW_TILE
    return pl.pallas_call(
        functools.partial(_ffn_body, k0=k0, final=final),
        grid=(n_tiles,),
        in_specs=[pl.BlockSpec((tm, d), lambda i: (i, 0)),
                  pl.BlockSpec((None, N_MOD, d), lambda i: (group_of(i), 0, 0)),
                  _resident((1, d)),
                  _resident(w13c.shape),
                  _resident(w2c.shape),
                  _resident((1, d))],
        out_specs=pl.BlockSpec((tm, d), lambda i: (i, 0)),
        out_shape=jax.ShapeDtypeStruct((n_tiles * tm, d), F32),
        scratch_shapes=[pltpu.VMEM((tm, d), BF16), pltpu.VMEM((tm, d), F32)],
        compiler_params=_params("parallel"),
        name="ffn",
    )(h, mods, norm_w.reshape(1, d), w13c, w2c, final_w.reshape(1, d))


def _chunk_ffn_weights(w13, w2):
    d, two_ff = w13.shape
    ff = two_ff // 2
    nc = ff // FF_CHUNK
    w1 = w13[:, :ff].reshape(d, nc, FF_CHUNK)
    w3 = w13[:, ff:].reshape(d, nc, FF_CHUNK)
    w13c = jnp.transpose(jnp.concatenate([w1, w3], axis=-1), (1, 0, 2)).astype(BF16)
    w2c = w2.reshape(nc, FF_CHUNK, d).astype(BF16)
    return w13c, w2c


def _log_sigmoid(x):
    return jnp.minimum(x, 0.0) - jnp.log1p(jnp.exp(-jnp.abs(x)))


def _rope_rotate(x):
    n = x.shape[-1]
    lane = lax.broadcasted_iota(jnp.int32, x.shape, 1)
    up = pltpu.roll(x, n - 8, 1)
    dn = pltpu.roll(x, 8, 1)
    return jnp.where((lane & 15) < 8, -up, dn)


def _proj_body(h_ref, mod_ref, nw_ref, w_ref, wvt_ref, wa_ref, ba_ref, cos_ref, sin_ref,
               gqk_ref, gv_ref, gvt_ref, gg_ref, glg_ref, nq_ref, nk_ref, nv_ref,
               dq_ref, dk_ref, dv_ref, cu_ref, xb_ref):
    x = h_ref[...]
    tm = x.shape[0]
    xm = _rms(x, nw_ref[...]) * (1.0 + mod_ref[4:5, :]) + mod_ref[3:4, :]
    xb_ref[...] = xm.astype(BF16)

    z = _dot(xb_ref[...], w_ref[:, C_GLA:C_GLA + 256])
    lane = lax.broadcasted_iota(jnp.int32, (1, 256), 1)
    gqk_ref[...] = z * jnp.where(lane < 128, GLA_DK ** -0.5, 1.0)
    gv_ref[...] = _dot(xb_ref[...], w_ref[:, C_GLA + 256:C_GLA + 512])
    gvt_ref[...] = _dot_nt(wvt_ref[...], xb_ref[...])
    gg_ref[...] = _dot(xb_ref[...], w_ref[:, C_GLA + 512:C_GLA + 768])
    aux = _dot(xb_ref[...], w_ref[:, C_AUX:C_END])
    pre = _dot(aux.astype(BF16), wa_ref[...]) + ba_ref[...]
    glg_ref[...] = _log_sigmoid(pre) * (1.0 / GLA_TAU)

    nq_ref[...] = (_dot(xb_ref[...], w_ref[:, C_NA:C_NA + 256]) * (NA_DIM ** -0.5)).astype(BF16)
    nk_ref[...] = _dot(xb_ref[...], w_ref[:, C_NA + 256:C_NA + 512]).astype(BF16)
    nv_ref[...] = _dot(xb_ref[...], w_ref[:, C_NA + 512:C_NA + 768]).astype(BF16)

    cos = cos_ref[...]
    sin = sin_ref[...]
    cos2 = jnp.concatenate([cos, cos], axis=1)
    sin2 = jnp.concatenate([sin, sin], axis=1)
    zq = _dot(xb_ref[...], w_ref[:, C_DIFF:C_DIFF + 256])
    zq = (zq * cos2 + _rope_rotate(zq) * sin2) * (DIFF_DK ** -0.5)
    zk = _dot(xb_ref[...], w_ref[:, C_DIFF + 256:C_DIFF + 512])
    zk = zk * cos2 + _rope_rotate(zk) * sin2
    zv = _dot(xb_ref[...], w_ref[:, C_DIFF + 512:C_DIFF + 768])
    one_col = jnp.where(lax.broadcasted_iota(jnp.int32, (tm, 64), 1) == 0, 1.0, 0.0).astype(BF16)
    for hh in range(DIFF_HEADS):
        sl = slice(64 * hh, 64 * hh + 64)
        dq_ref[hh] = zq[:, sl].astype(BF16)
        dk_ref[hh] = zk[:, sl].astype(BF16)
        dv_ref[hh, :, 0:64] = zv[:, sl].astype(BF16)
        dv_ref[hh, :, 64:128] = one_col

    za = _dot(xb_ref[...], w_ref[:, C_CONV:C_CONV + 256])
    zg = _dot(xb_ref[...], w_ref[:, C_CONV + 256:C_CONV + 512])
    cu_ref[...] = za * jax.nn.sigmoid(zg)


def _proj(h, mods, norm_w, w_p, wa, ba, cos_t, sin_t, *, n_tiles, group_of, pos_of):
    r, d = h.shape
    tm = ROW_TILE
    row = lambda i: (i, 0)
    hrow = lambda i: (0, i, 0)
    f32_256 = jax.ShapeDtypeStruct((r, 256), F32)
    bf_256 = jax.ShapeDtypeStruct((r, 256), BF16)
    wvt = w_p[:, C_GLA + 256:C_GLA + 512].T
    out_shape = [f32_256, f32_256, jax.ShapeDtypeStruct((256, r), F32), f32_256, f32_256,
                 bf_256, bf_256, bf_256,
                 jax.ShapeDtypeStruct((DIFF_HEADS, r, 64), BF16),
                 jax.ShapeDtypeStruct((DIFF_HEADS, r, 64), BF16),
                 jax.ShapeDtypeStruct((DIFF_HEADS, r, 128), BF16),
                 f32_256]
    out_specs = [pl.BlockSpec((tm, 256), row)] * 2 + [pl.BlockSpec((256, tm), lambda i: (0, i))] + [
        pl.BlockSpec((tm, 256), row)] * 5 + [
        pl.BlockSpec((DIFF_HEADS, tm, 64), hrow),
        pl.BlockSpec((DIFF_HEADS, tm, 64), hrow),
        pl.BlockSpec((DIFF_HEADS, tm, 128), hrow),
        pl.BlockSpec((tm, 256), row)]
    return pl.pallas_call(
        _proj_body,
        grid=(n_tiles,),
        in_specs=[pl.BlockSpec((tm, d), row),
                  pl.BlockSpec((None, N_MOD, d), lambda i: (group_of(i), 0, 0)),
                  _resident((1, d)),
                  _resident(w_p.shape),
                  _resident(wvt.shape),
                  _resident(wa.shape),
                  _resident(ba.shape),
                  pl.BlockSpec((tm, LANES), lambda i: (pos_of(i), 0)),
                  pl.BlockSpec((tm, LANES), lambda i: (pos_of(i), 0))],
        out_specs=out_specs,
        out_shape=out_shape,
        scratch_shapes=[pltpu.VMEM((tm, d), BF16)],
        compiler_params=_params("parallel"),
        name="proj",
    )(h, mods, norm_w.reshape(1, d), w_p, wvt, wa, ba, cos_t, sin_t)


def _permute_w_in(w_in):
    d = w_in.shape[0]
    g0 = 2 * GLA_HEADS * GLA_DK + 2 * GLA_HEADS * GLA_DV
    aux = w_in[:, g0:g0 + 2 * GLA_RANK]
    rest = w_in[:, g0 + 2 * GLA_RANK:]
    pad = jnp.zeros((d, C_END - C_AUX - 2 * GLA_RANK), w_in.dtype)
    return jnp.concatenate([w_in[:, :g0], rest, aux, pad], axis=1).astype(BF16)


def _gate_weights(wa_f, ba_f, wa_b, ba_b):
    n = GLA_HEADS * GLA_DK
    wa = jnp.zeros((C_END - C_AUX, 2 * n), F32)
    wa = wa.at[:GLA_RANK, :n].set(wa_f).at[GLA_RANK:2 * GLA_RANK, n:].set(wa_b)
    return wa.astype(BF16), jnp.concatenate([ba_f, ba_b]).reshape(1, 2 * n)


def _rope_tables(seq, tile):
    t = jnp.arange(seq)
    row = (t // GRID_W).astype(F32)
    col = (t % GRID_W).astype(F32)
    half = DIFF_DK // 2
    inv = 1.0 / (ROPE_BASE ** (jnp.arange(0, half, 2, dtype=F32) / half))
    ang_r = row[:, None] * inv
    ang_c = col[:, None] * inv
    ang = jnp.concatenate([ang_r, ang_r, ang_c, ang_c], axis=-1)
    ang = jnp.tile(ang, (1, LANES // DIFF_DK))
    cos = jnp.concatenate([jnp.cos(ang), jnp.ones((tile, LANES), F32)], axis=0)
    sin = jnp.concatenate([jnp.sin(ang), jnp.zeros((tile, LANES), F32)], axis=0)
    return cos, sin


def _split3(x):
    hi = x.astype(BF16)
    r1 = x - hi.astype(F32)
    mid = r1.astype(BF16)
    lo = (r1 - mid.astype(F32)).astype(BF16)
    return hi, mid, lo


def _gla_body(qk_ref, v_ref, vt_ref, lg_ref, g_ref, of_ref, nw_ref, o_ref, st_ref, *, reverse):
    blk = qk_ref.shape[0]
    n_chunks = blk // GLA_CHUNK
    nqk = GLA_HEADS * GLA_DK
    nv = GLA_HEADS * GLA_DV

    @pl.when(pl.program_id(1) == 0)
    def _():
        st_ref[...] = jnp.zeros_like(st_ref)

    ri = lax.broadcasted_iota(jnp.int32, (blk, blk), 0)
    ci = lax.broadcasted_iota(jnp.int32, (blk, blk), 1)
    same = (ri // GLA_CHUNK) == (ci // GLA_CHUNK)
    tri = jnp.where(same & ((ci >= ri) if reverse else (ci <= ri)), 1.0, 0.0).astype(BF16)
    lg = lg_ref[:, nqk:2 * nqk] if reverse else lg_ref[:, 0:nqk]
    hi, mid, lo = _split3(lg)
    bcum_all = _dot(tri, hi) + _dot(tri, mid) + _dot(tri, lo)

    c = GLA_CHUNK
    rk = lax.broadcasted_iota(jnp.int32, (GLA_HEADS * c, nqk), 0) // c
    ck = lax.broadcasted_iota(jnp.int32, (GLA_HEADS * c, nqk), 1) // GLA_DK
    mask_k = rk == ck
    rv = lax.broadcasted_iota(jnp.int32, (GLA_HEADS * c, nv), 0) // c
    cv = lax.broadcasted_iota(jnp.int32, (GLA_HEADS * c, nv), 1) // GLA_DV
    mask_v = rv == cv
    rs = lax.broadcasted_iota(jnp.int32, (nv, nqk), 0) // GLA_DV
    cs = lax.broadcasted_iota(jnp.int32, (nv, nqk), 1) // GLA_DK
    mask_s = rs == cs
    ai = lax.broadcasted_iota(jnp.int32, (c, GLA_HEADS * c), 0)
    aj = lax.broadcasted_iota(jnp.int32, (c, GLA_HEADS * c), 1) % c
    mask_a = (aj >= ai) if reverse else (aj <= ai)

    vt = vt_ref[...].astype(BF16)
    order = range(n_chunks - 1, -1, -1) if reverse else range(n_chunks)
    for ch in order:
        rows = slice(ch * c, (ch + 1) * c)
        q = qk_ref[rows, 0:nqk]
        k = qk_ref[rows, nqk:2 * nqk]
        v = v_ref[rows, :]
        bcum = bcum_all[rows, :]
        btot = bcum[0:1, :] if reverse else bcum[c - 1:c, :]
        q_in = (q * jnp.exp(bcum)).astype(BF16)
        k_in = k * jnp.exp(-bcum)
        k_out = (k * jnp.exp(btot - bcum)).astype(BF16)
        k_bd = jnp.where(mask_k, jnp.concatenate([k_in] * GLA_HEADS, axis=0), 0.0).astype(BF16)
        a = jnp.where(mask_a, _dot_nt(q_in, k_bd), 0.0).astype(BF16)
        v_bd = jnp.where(mask_v, jnp.concatenate([v] * GLA_HEADS, axis=0), 0.0).astype(BF16)
        st = st_ref[...]
        o = _dot(a, v_bd) + _dot_nt(q_in, st.astype(BF16))
        k_pad = jnp.concatenate([jnp.zeros((n * c, nqk), BF16) for n in (ch,) if n] + [k_out]
                                + [jnp.zeros((n * c, nqk), BF16) for n in (n_chunks - 1 - ch,) if n], axis=0)
        u_t = _dot(vt, k_pad)
        st_ref[...] = st * jnp.exp(btot) + jnp.where(mask_s, u_t, 0.0)
        if reverse:
            o = o + of_ref[rows, :]
            hi2, lo2, _ = _split3(o * o)
            hr = lax.broadcasted_iota(jnp.int32, (nv, nv), 0) // GLA_DV
            hc = lax.broadcasted_iota(jnp.int32, (nv, nv), 1) // GLA_DV
            seg = jnp.where(hr == hc, 1.0, 0.0).astype(BF16)
            ms = (_dot(hi2, seg) + _dot(lo2, seg)) * (1.0 / GLA_DV)
            y = o * lax.rsqrt(ms + RMS_EPS) * nw_ref[...]
            o_ref[rows, :] = y * _silu(g_ref[rows, :])
        else:
            o_ref[rows, :] = o


def _gla(gqk, gv, gvt, glg, gg, o_f, norm_w4, *, reverse, batch, seq, ctx):
    r = gqk.shape[0]
    blk = GLA_BLOCK
    nc, nl = ctx // blk, seq // blk
    ctx_base = batch * seq // blk

    def blk_of(b, s):
        if reverse:
            return jnp.where(s < nc, ctx_base + b * nc + (nc - 1 - s), b * nl + (nl - 1 - (s - nc)))
        return jnp.where(s < nc, ctx_base + b * nc + s, b * nl + (s - nc))

    spec = pl.BlockSpec((blk, 256), lambda b, s: (blk_of(b, s), 0))
    spec_t = pl.BlockSpec((256, blk), lambda b, s: (0, blk_of(b, s)))
    return pl.pallas_call(
        functools.partial(_gla_body, reverse=reverse),
        grid=(batch, nc + nl),
        in_specs=[spec, spec, spec_t, spec, spec, spec, _resident((1, 256))],
        out_specs=spec,
        out_shape=jax.ShapeDtypeStruct((r, 256), F32),
        scratch_shapes=[pltpu.VMEM((GLA_HEADS * GLA_DV, GLA_HEADS * GLA_DK), F32)],
        compiler_params=_params("arbitrary", "arbitrary"),
        name="gla_bwd" if reverse else "gla_fwd",
    )(gqk, gv, gvt, glg, gg, o_f, norm_w4)


def _na_bias_tables(rpb, rows):
    nq = NA_QROWS
    nkr = 4 * NA_KBLK // GRID_W
    n_blocks = rows // nq
    kb_total = rows * GRID_W // NA_KBLK
    tabs = []
    for j in (0, 1, n_blocks - 1):
        kr0 = min(max(2 * j - 1, 0), kb_total - 4) * (NA_KBLK // GRID_W)
        r = j * nq + jnp.arange(nq)
        rk = kr0 + jnp.arange(nkr)
        start = jnp.clip(r - NA_ROWS // 2, 0, rows - NA_ROWS)
        row_ok = (rk[None, :] >= start[:, None]) & (rk[None, :] < start[:, None] + NA_ROWS)
        row_off = jnp.clip(rk[None, :] - r[:, None] + (NA_ROWS - 1), 0, 2 * NA_ROWS - 2)
        col = jnp.arange(GRID_W)
        cstart = jnp.clip(col - NA_COLS // 2, 0, GRID_W - NA_COLS)
        col_ok = (col[None, :] >= cstart[:, None]) & (col[None, :] < cstart[:, None] + NA_COLS)
        col_off = jnp.clip(col[None, :] - col[:, None] + (NA_COLS - 1), 0, 2 * NA_COLS - 2)
        b = rpb[:, row_off][..., col_off]
        ok = row_ok[:, :, None, None] & col_ok[None, None, :, :]
        b = jnp.where(ok[None], b, NEG_INF)
        b = jnp.transpose(b, (0, 1, 3, 2, 4)).reshape(NA_HEADS, nq * GRID_W, nkr * GRID_W)
        tabs.append(b)
    return jnp.stack(tabs).astype(F32)


def _na_body(q_ref, k0, k1, k2, k3, v0, v1, v2, v3, kc_ref, vc_ref, bias_ref, o_ref):
    q = q_ref[...]
    kw = jnp.concatenate([k0[...], k1[...], k2[...], k3[...]], axis=0)
    vw = jnp.concatenate([v0[...], v1[...], v2[...], v3[...]], axis=0)
    kc = kc_ref[...]
    vc = vc_ref[...]
    lane = lax.broadcasted_iota(jnp.int32, (1, NA_HEADS * NA_DIM), 1) // NA_DIM
    out = jnp.zeros(o_ref.shape, F32)
    for hh in range(NA_HEADS):
        mh = lane == hh
        qh = jnp.where(mh, q, jnp.zeros_like(q))
        s_w = _dot_nt(qh, kw) + bias_ref[hh]
        s_c = _dot_nt(qh, kc)
        m = jnp.maximum(jnp.max(s_w, axis=-1, keepdims=True), jnp.max(s_c, axis=-1, keepdims=True))
        p_w = jnp.exp(s_w - m)
        p_c = jnp.exp(s_c - m)
        l = jnp.sum(p_w, axis=-1, keepdims=True) + jnp.sum(p_c, axis=-1, keepdims=True)
        o = _dot(p_w.astype(BF16), vw) + _dot(p_c.astype(BF16), vc)
        out = out + jnp.where(mh, o / l, 0.0)
    o_ref[...] = out


def _na(nq, nk, nv, bias, *, batch, seq, ctx):
    r = nq.shape[0]
    tq = NA_QROWS * GRID_W
    nj = seq // tq
    kb = seq // NA_KBLK
    assert NA_KBLK % ctx == 0 or ctx % NA_KBLK == 0
    ctx_blk0 = batch * seq // ctx

    def kspec(i):
        return pl.BlockSpec((NA_KBLK, 256),
                            lambda b, j: (b * kb + jnp.clip(2 * j - 1, 0, kb - 4) + i, 0))

    cspec = pl.BlockSpec((ctx, 256), lambda b, j: (ctx_blk0 + b, 0))
    variant = lambda j: jnp.where(j == 0, 0, jnp.where(j == nj - 1, 2, 1))
    return pl.pallas_call(
        _na_body,
        grid=(batch, nj),
        in_specs=[pl.BlockSpec((tq, 256), lambda b, j: (b * nj + j, 0))]
                 + [kspec(i) for i in range(4)] + [kspec(i) for i in range(4)]
                 + [cspec, cspec,
                    pl.BlockSpec((None, NA_HEADS, tq, 4 * NA_KBLK), lambda b, j: (variant(j), 0, 0, 0))],
        out_specs=pl.BlockSpec((tq, 256), lambda b, j: (b * nj + j, 0)),
        out_shape=jax.ShapeDtypeStruct((r, 256), F32),
        compiler_params=_params("parallel", "arbitrary"),
        name="na",
    )(nq, nk, nk, nk, nk, nv, nv, nv, nv, nk, nv, bias)


def _diff_lambda(lq_ref, lambda_init):
    lq = lq_ref[...]
    s1 = jnp.sum(lq[0:1, :] * lq[1:2, :], axis=-1, keepdims=True)
    s2 = jnp.sum(lq[2:3, :] * lq[3:4, :], axis=-1, keepdims=True)
    return jnp.exp(s1) - jnp.exp(s2) + lambda_init


def _stack_q(q):
    lane = lax.broadcasted_iota(jnp.int32, (1, 2 * DIFF_DK), 1)
    zero = jnp.zeros_like(q)
    return jnp.concatenate([jnp.where(lane < DIFF_DK, q, zero), jnp.where(lane >= DIFF_DK, q, zero)], axis=0)


def _diff_finish(acc, tq, lam, nw, lambda_init):
    o = acc[:, 0:DIFF_DV] / acc[:, DIFF_DV:DIFF_DV + 1]
    od = o[:tq] - lam * o[tq:]
    return _rms(od, nw) * (1.0 - lambda_init)


def _diff_body(q_ref, kc_ref, vc_ref, k_ref, v_ref, lq_ref, nw_ref, o_ref, *, tk, lambda_init):
    tq = q_ref.shape[0]
    qs = _stack_q(q_ref[...])
    s = _dot_nt(qs, kc_ref[...])
    m0 = jnp.max(s, axis=-1, keepdims=True)
    acc0 = _dot(jnp.exp(s - m0).astype(BF16), vc_ref[...])

    def body(j, carry):
        m, acc = carry
        off = pl.multiple_of(j * tk, tk)
        s = _dot_nt(qs, k_ref[pl.ds(off, tk), :])
        m_new = jnp.maximum(m, jnp.max(s, axis=-1, keepdims=True))
        p = jnp.exp(s - m_new).astype(BF16)
        acc = jnp.exp(m - m_new) * acc + _dot(p, v_ref[pl.ds(off, tk), :])
        return m_new, acc

    _, acc = lax.fori_loop(0, k_ref.shape[0] // tk, body, (m0, acc0))
    o_ref[...] = _diff_finish(acc, tq, _diff_lambda(lq_ref, lambda_init), nw_ref[...], lambda_init)


def _diff(dq, dk, dv, lq, norm_w, *, batch, seq, ctx, lambda_init):
    h, r, _ = dq.shape
    tq = DIFF_TQ
    nq = seq // tq
    ctx_blk0 = batch * seq // ctx
    return pl.pallas_call(
        functools.partial(_diff_body, tk=min(DIFF_TK, seq), lambda_init=lambda_init),
        grid=(batch, h, nq),
        in_specs=[pl.BlockSpec((None, tq, 64), lambda b, hh, i: (hh, b * nq + i, 0)),
                  pl.BlockSpec((None, ctx, 64), lambda b, hh, i: (hh, ctx_blk0 + b, 0)),
                  pl.BlockSpec((None, ctx, 128), lambda b, hh, i: (hh, ctx_blk0 + b, 0)),
                  pl.BlockSpec((None, seq, 64), lambda b, hh, i: (hh, b, 0)),
                  pl.BlockSpec((None, seq, 128), lambda b, hh, i: (hh, b, 0)),
                  _resident(lq.shape),
                  _resident((1, DIFF_DV))],
        out_specs=pl.BlockSpec((None, tq, 64), lambda b, hh, i: (hh, b * nq + i, 0)),
        out_shape=jax.ShapeDtypeStruct((h, r, 64), F32),
        compiler_params=_params("parallel", "parallel", "arbitrary"),
        name="diff",
    )(dq, dk, dv, dk, dv, lq, norm_w.reshape(1, DIFF_DV))


def _ctx_attn_body(nq_ref, nk_ref, nv_ref, dq_ref, dk_ref, dv_ref, lq_ref, nw_ref, na_in, df_in,
                   na_out, df_out, *, lambda_init):
    del na_in, df_in
    q = nq_ref[...]
    k = nk_ref[...]
    v = nv_ref[...]
    lane = lax.broadcasted_iota(jnp.int32, (1, NA_HEADS * NA_DIM), 1) // NA_DIM
    out = jnp.zeros(na_out.shape, F32)
    for hh in range(NA_HEADS):
        mh = lane == hh
        s = _dot_nt(jnp.where(mh, q, jnp.zeros_like(q)), k)
        p = jnp.exp(s - jnp.max(s, axis=-1, keepdims=True))
        o = _dot(p.astype(BF16), v) / jnp.sum(p, axis=-1, keepdims=True)
        out = out + jnp.where(mh, o, 0.0)
    na_out[...] = out

    lam = _diff_lambda(lq_ref, lambda_init)
    tq = dq_ref.shape[1]
    for hh in range(DIFF_HEADS):
        s = _dot_nt(_stack_q(dq_ref[hh]), dk_ref[hh])
        p = jnp.exp(s - jnp.max(s, axis=-1, keepdims=True))
        acc = _dot(p.astype(BF16), dv_ref[hh])
        df_out[hh] = _diff_finish(acc, tq, lam, nw_ref[...], lambda_init)


def _ctx_attn(nq, nk, nv, dq, dk, dv, lq, norm_w, na_o, df_o, *, batch, seq, ctx, lambda_init):
    blk0 = batch * seq // ctx
    s256 = pl.BlockSpec((ctx, 256), lambda b: (blk0 + b, 0))
    sh64 = pl.BlockSpec((DIFF_HEADS, ctx, 64), lambda b: (0, blk0 + b, 0))
    sh128 = pl.BlockSpec((DIFF_HEADS, ctx, 128), lambda b: (0, blk0 + b, 0))
    return pl.pallas_call(
        functools.partial(_ctx_attn_body, lambda_init=lambda_init),
        grid=(batch,),
        in_specs=[s256, s256, s256, sh64, sh64, sh128, _resident(lq.shape), _resident((1, DIFF_DV)),
                  pl.BlockSpec(memory_space=pl.ANY), pl.BlockSpec(memory_space=pl.ANY)],
        out_specs=[s256, sh64],
        out_shape=[jax.ShapeDtypeStruct(na_o.shape, F32), jax.ShapeDtypeStruct(df_o.shape, F32)],
        input_output_aliases={8: 0, 9: 1},
        compiler_params=_params("arbitrary"),
        name="ctx_attn",
    )(nq, nk, nv, dq, dk, dv, lq, norm_w.reshape(1, DIFF_DV), na_o, df_o)


def _conv_body(u_ref, prev_ref, next_ref, dw_ref, dwb_ref, lng_ref, lnb_ref, pw_ref, pwb_ref,
               o_ref, pad_ref, *, tiles_per_seq, n_latent_tiles):
    t = u_ref.shape[0]
    i = pl.program_id(0)
    in_ctx = i >= n_latent_tiles
    first = in_ctx | (i % tiles_per_seq == 0)
    last = in_ctx | (i % tiles_per_seq == tiles_per_seq - 1)
    pad_ref[0:HALO, :] = jnp.where(first, 0.0, prev_ref[...])
    pad_ref[HALO:HALO + t, :] = u_ref[...]
    pad_ref[HALO + t:HALO + t + HALO, :] = jnp.where(last, 0.0, next_ref[...])
    acc = jnp.zeros((t, CONV_CH), F32)
    base = HALO - CONV_K // 2
    for k in range(CONV_K):
        acc = acc + dw_ref[k:k + 1, :] * pad_ref[base + k:base + k + t, :]
    y = acc + dwb_ref[...]
    mu = jnp.mean(y, axis=-1, keepdims=True)
    yc = y - mu
    var = jnp.mean(yc * yc, axis=-1, keepdims=True)
    y = _silu(yc * lax.rsqrt(var + LN_EPS) * lng_ref[...] + lnb_ref[...])
    o_ref[...] = _dot(y.astype(BF16), pw_ref[...]) + pwb_ref[...]


def _conv(cu, dw, dw_b, ln_g, ln_b, pw, pw_b, *, batch, seq, ctx):
    r = cu.shape[0]
    t = CONV_TILE
    assert ctx == t and seq % t == 0
    n_tiles = r // t
    hb = t // HALO
    vec = lambda a: a.reshape(1, CONV_CH)
    return pl.pallas_call(
        functools.partial(_conv_body, tiles_per_seq=seq // t, n_latent_tiles=batch * seq // t),
        grid=(n_tiles,),
        in_specs=[pl.BlockSpec((t, CONV_CH), lambda i: (i, 0)),
                  pl.BlockSpec((HALO, CONV_CH), lambda i: (jnp.maximum(i * hb - 1, 0), 0)),
                  pl.BlockSpec((HALO, CONV_CH), lambda i: (jnp.minimum((i + 1) * hb, n_tiles * hb - 1), 0)),
                  _resident((CONV_K, CONV_CH))] + [_resident((1, CONV_CH))] * 3
                 + [_resident((CONV_CH, CONV_CH)), _resident((1, CONV_CH))],
        out_specs=pl.BlockSpec((t, CONV_CH), lambda i: (i, 0)),
        out_shape=jax.ShapeDtypeStruct((r, CONV_CH), F32),
        scratch_shapes=[pltpu.VMEM((t + 2 * HALO, CONV_CH), F32)],
        compiler_params=_params("parallel"),
        name="conv",
    )(cu, cu, cu, dw, vec(dw_b), vec(ln_g), vec(ln_b), pw.astype(BF16), vec(pw_b))


def _wout_body(h_ref, mod_ref, gx_ref, nx_ref, dx_ref, cx_ref, w_ref, o_ref):
    mix = jnp.concatenate([gx_ref[...], nx_ref[...]] + [dx_ref[hh] for hh in range(DIFF_HEADS)]
                          + [cx_ref[...]], axis=-1).astype(BF16)
    o_ref[...] = h_ref[...] + mod_ref[5:6, :] * _dot(mix, w_ref[...])


def _wout(h, mods, gx, nx, dx, cx, w_out, *, n_tiles, group_of):
    r, d = h.shape
    tm = ROW_TILE
    row = lambda i: (i, 0)
    return pl.pallas_call(
        _wout_body,
        grid=(n_tiles,),
        in_specs=[pl.BlockSpec((tm, d), row),
                  pl.BlockSpec((None, N_MOD, d), lambda i: (group_of(i), 0, 0)),
                  pl.BlockSpec((tm, 256), row), pl.BlockSpec((tm, 256), row),
                  pl.BlockSpec((DIFF_HEADS, tm, 64), lambda i: (0, i, 0)),
                  pl.BlockSpec((tm, 256), row),
                  _resident(w_out.shape)],
        out_specs=pl.BlockSpec((tm, d), row),
        out_shape=jax.ShapeDtypeStruct((n_tiles * tm, d), F32),
        compiler_params=_params("parallel"),
        name="wout",
    )(h, mods, gx, nx, dx, cx, w_out.astype(BF16))


def kernel(x, c, ctx, c_ctx, ada_w, ada_b, norm_ffn1, ffn1_w13, ffn1_w2, norm_mix, w_in, gla_wa_f, gla_ba_f, gla_wa_b, gla_ba_b, gla_norm, na_rpb, diff_lq1, diff_lk1, diff_lq2, diff_lk2, diff_norm, conv_dw, conv_dw_b, conv_ln_g, conv_ln_b, conv_pw, conv_pw_b, w_out, norm_ffn2, ffn2_w13, ffn2_w2, final_norm):
    batch, seq, d = x.shape
    n_ctx = ctx.shape[1]
    depth = ada_w.shape[0]
    rows = seq // GRID_W
    tm = ROW_TILE
    assert seq % tm == 0 and (batch * n_ctx) % tm == 0 and batch + 1 <= 8
    lat_tiles = batch * seq // tm
    all_tiles = lat_tiles + batch * n_ctx // tm
    tiles_per_batch = seq // tm

    def group_of(i):
        return jnp.minimum(i // tiles_per_batch, batch)

    def pos_of(i):
        return jnp.where(i < lat_tiles, i % tiles_per_batch, tiles_per_batch)

    c_rows = jnp.concatenate([c, c_ctx[None, :], jnp.zeros((8 - batch - 1, d), F32)], axis=0)
    mods_all = _ada(c_rows, ada_w, ada_b)[:, :batch + 1].reshape(depth, batch + 1, N_MOD, d)
    cos_t, sin_t = _rope_tables(seq, tm)
    h = jnp.concatenate([x.reshape(batch * seq, d), ctx.reshape(batch * n_ctx, d)], axis=0)

    for i in range(depth):
        last = i == depth - 1
        lambda_init = 0.8 - 0.6 * math.exp(-0.3 * i)
        mods = mods_all[i]
        tok = dict(n_tiles=all_tiles, group_of=group_of)
        geo = dict(batch=batch, seq=seq, ctx=n_ctx)

        w13c, w2c = _chunk_ffn_weights(ffn1_w13[i], ffn1_w2[i])
        h = _ffn(h, mods, norm_ffn1[i], w13c, w2c, final_norm, k0=0, final=False, **tok)

        wa, ba = _gate_weights(gla_wa_f[i], gla_ba_f[i], gla_wa_b[i], gla_ba_b[i])
        (gqk, gv, gvt, gg, glg, nq, nk, nv, dq, dk, dv, cu) = _proj(
            h, mods, norm_mix[i], _permute_w_in(w_in[i]), wa, ba, cos_t, sin_t, pos_of=pos_of, **tok)

        gnorm = jnp.tile(gla_norm[i], GLA_HEADS).reshape(1, GLA_HEADS * GLA_DV)
        o_f = _gla(gqk, gv, gvt, glg, gg, gg, gnorm, reverse=False, **geo)
        gx = _gla(gqk, gv, gvt, glg, gg, o_f, gnorm, reverse=True, **geo)

        nx = _na(nq, nk, nv, _na_bias_tables(na_rpb[i], rows), **geo)

        lq = jnp.stack([diff_lq1[i], diff_lk1[i], diff_lq2[i], diff_lk2[i]])
        dx = _diff(dq, dk, dv, lq, diff_norm[i], lambda_init=lambda_init, **geo)
        if not last:
            nx, dx = _ctx_attn(nq, nk, nv, dq, dk, dv, lq, diff_norm[i], nx, dx,
                               lambda_init=lambda_init, **geo)

        cx = _conv(cu, conv_dw[i], conv_dw_b[i], conv_ln_g[i], conv_ln_b[i], conv_pw[i], conv_pw_b[i], **geo)

        if last:
            tok = dict(n_tiles=lat_tiles, group_of=group_of)
        h = _wout(h, mods, gx, nx, dx, cx, w_out[i], **tok)

        w13c, w2c = _chunk_ffn_weights(ffn2_w13[i], ffn2_w2[i])
        h = _ffn(h, mods, norm_ffn2[i], w13c, w2c, final_norm, k0=6, final=last, **tok)

    return h[:batch * seq].reshape(batch, seq, d)
```

```python
import functools
import math

import jax
import jax.numpy as jnp
from jax import lax
from jax.experimental import pallas as pl
from jax.experimental.pallas import tpu as pltpu

F32 = jnp.float32
BF16 = jnp.bfloat16

GRID_W = 64
N_MOD = 9
RMS_EPS = 1e-6
LN_EPS = 1e-5
NEG_INF = -1e30
GLA_HEADS, GLA_DK, GLA_DV, GLA_RANK, GLA_TAU, GLA_CHUNK = 4, 32, 64, 16, 16.0, 64
NA_HEADS, NA_DIM, NA_ROWS, NA_COLS = 4, 64, 8, 16
DIFF_HEADS, DIFF_DK, DIFF_DV = 4, 32, 64
CONV_CH, CONV_K = 256, 31
ROPE_BASE = 10000.0
LOG2E = 1.4426950408889634

LANES = 128
VMEM_LIMIT = 56 * 1024 * 1024

ROW_TILE = 512
FF_CHUNK = 256
GLA_BLOCK = 256
NA_QROWS = 8
NA_KBLK = 256
CONV_TILE = 256
DIFF_TQ = 256
DIFF_TK = 512
DIFF_UNROLL = 8
HALO = 16

C_GLA, C_NA, C_DIFF, C_CONV, C_AUX, C_END = 0, 768, 1536, 2304, 2816, 2944


def _dot(a, b):
    return jnp.dot(a, b, preferred_element_type=F32)


def _dot_nt(a, b):
    return lax.dot_general(a, b, (((1,), (1,)), ((), ())), preferred_element_type=F32)


def _params(*sem):
    return pltpu.CompilerParams(dimension_semantics=sem, vmem_limit_bytes=VMEM_LIMIT)


def _resident(shape):
    nd = len(shape)
    return pl.BlockSpec(shape, lambda *_: (0,) * nd, pipeline_mode=pl.Buffered(1))


def _silu(x):
    return x * jax.nn.sigmoid(x)


def _rms(x, w):
    return x * lax.rsqrt(jnp.mean(x * x, axis=-1, keepdims=True) + RMS_EPS) * w


def _ada_body(c_ref, w_ref, b_ref, o_ref):
    s = _silu(c_ref[...])
    o_ref[...] = jnp.dot(s, w_ref[...], precision=lax.Precision.HIGHEST,
                         preferred_element_type=F32) + b_ref[...]


def _ada(c_rows, ada_w, ada_b):
    depth, d, _ = ada_w.shape
    return pl.pallas_call(
        _ada_body,
        grid=(depth, N_MOD),
        in_specs=[pl.BlockSpec((8, d), lambda l, n: (0, 0)),
                  pl.BlockSpec((None, d, d), lambda l, n: (l, 0, n)),
                  pl.BlockSpec((None, 1, d), lambda l, n: (l, 0, n))],
        out_specs=pl.BlockSpec((None, 8, d), lambda l, n: (l, 0, n)),
        out_shape=jax.ShapeDtypeStruct((depth, 8, N_MOD * d), F32),
        compiler_params=_params("arbitrary", "arbitrary"),
        name="ada",
    )(c_rows, ada_w, ada_b.reshape(depth, 1, N_MOD * d))


def _ffn_body(h_ref, mod_ref, nw_ref, w13_ref, w2_ref, fw_ref, o_ref, xb_ref, acc_ref, *, k0, final):
    n_chunks, _, two_tf = w13_ref.shape
    tf = two_tf // 2
    x = h_ref[...]
    xm = _rms(x, nw_ref[...]) * (1.0 + mod_ref[k0 + 1:k0 + 2, :]) + mod_ref[k0:k0 + 1, :]
    xb_ref[...] = xm.astype(BF16)
    acc_ref[...] = jnp.zeros_like(acc_ref)

    def chunk(c, carry):
        h13 = _dot(xb_ref[...], w13_ref[c])
        a = h13[:, :tf]
        u = h13[:, tf:]
        acc_ref[...] += _dot((_silu(a) * u).astype(BF16), w2_ref[c])
        return carry

    lax.fori_loop(0, n_chunks, chunk, 0)
    out = x + (0.5 * mod_ref[k0 + 2:k0 + 3, :]) * acc_ref[...]
    if final:
        out = _rms(out, fw_ref[...])
    o_ref[...] = out


def _ffn(h, mods, norm_w, w13c, w2c, final_w, *, k0, n_tiles, group_of, final):
    r, d = h.shape
    tm = ROW_TILE
    return pl.pallas_call(
        functools.partial(_ffn_body, k0=k0, final=final),
        grid=(n_tiles,),
        in_specs=[pl.BlockSpec((tm, d), lambda i: (i, 0)),
                  pl.BlockSpec((None, N_MOD, d), lambda i: (group_of(i), 0, 0)),
                  _resident((1, d)),
                  _resident(w13c.shape),
                  _resident(w2c.shape),
                  _resident((1, d))],
        out_specs=pl.BlockSpec((tm, d), lambda i: (i, 0)),
        out_shape=jax.ShapeDtypeStruct((n_tiles * tm, d), F32),
        scratch_shapes=[pltpu.VMEM((tm, d), BF16), pltpu.VMEM((tm, d), F32)],
        compiler_params=_params("parallel"),
        name="ffn",
    )(h, mods, norm_w.reshape(1, d), w13c, w2c, final_w.reshape(1, d))


def _chunk_ffn_weights(w13, w2):
    d, two_ff = w13.shape
    ff = two_ff // 2
    nc = ff // FF_CHUNK
    w1 = w13[:, :ff].reshape(d, nc, FF_CHUNK)
    w3 = w13[:, ff:].reshape(d, nc, FF_CHUNK)
    w13c = jnp.transpose(jnp.concatenate([w1, w3], axis=-1), (1, 0, 2)).astype(BF16)
    w2c = w2.reshape(nc, FF_CHUNK, d).astype(BF16)
    return w13c, w2c


def _log_sigmoid(x):
    return jnp.minimum(x, 0.0) - jnp.log1p(jnp.exp(-jnp.abs(x)))


def _rope_rotate(x):
    n = x.shape[-1]
    lane = lax.broadcasted_iota(jnp.int32, x.shape, 1)
    up = pltpu.roll(x, n - 8, 1)
    dn = pltpu.roll(x, 8, 1)
    return jnp.where((lane & 15) < 8, -up, dn)


def _rope_rotate_rows(x):
    n = x.shape[0]
    row = lax.broadcasted_iota(jnp.int32, x.shape, 0)
    up = pltpu.roll(x, n - 8, 0)
    dn = pltpu.roll(x, 8, 0)
    return jnp.where((row & 15) < 8, -up, dn)


def _proj_body(h_ref, mod_ref, nw_ref, w_ref, wvt_ref, wdqt_ref, wdvt_ref, wa_ref, ba_ref,
               cos_ref, sin_ref, cost_ref, sint_ref,
               gqk_ref, gv_ref, gvt_ref, gg_ref, glg_ref, nq_ref, nk_ref, nv_ref,
               dqt_ref, dk_ref, dvt_ref, cu_ref, xb_ref):
    x = h_ref[...]
    tm = x.shape[0]
    xm = _rms(x, nw_ref[...]) * (1.0 + mod_ref[4:5, :]) + mod_ref[3:4, :]
    xb_ref[...] = xm.astype(BF16)

    z = _dot(xb_ref[...], w_ref[:, C_GLA:C_GLA + 256])
    lane = lax.broadcasted_iota(jnp.int32, (1, 256), 1)
    gqk_ref[...] = z * jnp.where(lane < 128, GLA_DK ** -0.5, 1.0)
    gv_ref[...] = _dot(xb_ref[...], w_ref[:, C_GLA + 256:C_GLA + 512])
    gvt_ref[...] = _dot_nt(wvt_ref[...], xb_ref[...])
    gg_ref[...] = _dot(xb_ref[...], w_ref[:, C_GLA + 512:C_GLA + 768])
    aux = _dot(xb_ref[...], w_ref[:, C_AUX:C_END])
    pre = _dot(aux.astype(BF16), wa_ref[...]) + ba_ref[...]
    glg_ref[...] = _log_sigmoid(pre) * (1.0 / GLA_TAU)

    nq_ref[...] = (_dot(xb_ref[...], w_ref[:, C_NA:C_NA + 256]) * (NA_DIM ** -0.5)).astype(BF16)
    nk_ref[...] = _dot(xb_ref[...], w_ref[:, C_NA + 256:C_NA + 512]).astype(BF16)
    nv_ref[...] = _dot(xb_ref[...], w_ref[:, C_NA + 512:C_NA + 768]).astype(BF16)

    cos = cos_ref[...]
    sin = sin_ref[...]
    cos2 = jnp.concatenate([cos, cos], axis=1)
    sin2 = jnp.concatenate([sin, sin], axis=1)
    zk = _dot(xb_ref[...], w_ref[:, C_DIFF + 256:C_DIFF + 512])
    zk = zk * cos2 + _rope_rotate(zk) * sin2
    n_rep = 2 * DIFF_HEADS
    cos_t = jnp.concatenate([cost_ref[...]] * n_rep, axis=0)
    sin_t = jnp.concatenate([sint_ref[...]] * n_rep, axis=0)
    zqt = _dot_nt(wdqt_ref[...], xb_ref[...])
    zqt = (zqt * cos_t + _rope_rotate_rows(zqt) * sin_t) * (DIFF_DK ** -0.5 * LOG2E)
    zvt = _dot_nt(wdvt_ref[...], xb_ref[...])
    one_row = jnp.where(lax.broadcasted_iota(jnp.int32, (64, tm), 0) == 0, 1.0, 0.0).astype(BF16)
    for hh in range(DIFF_HEADS):
        sl = slice(64 * hh, 64 * hh + 64)
        dk_ref[hh] = zk[:, sl].astype(BF16)
        dqt_ref[hh] = zqt[sl, :].astype(BF16)
        dvt_ref[hh, 0:64, :] = zvt[sl, :].astype(BF16)
        dvt_ref[hh, 64:128, :] = one_row

    za = _dot(xb_ref[...], w_ref[:, C_CONV:C_CONV + 256])
    zg = _dot(xb_ref[...], w_ref[:, C_CONV + 256:C_CONV + 512])
    cu_ref[...] = za * jax.nn.sigmoid(zg)


def _proj(h, mods, norm_w, w_p, wa, ba, rope, *, n_tiles, group_of, pos_of):
    r, d = h.shape
    tm = ROW_TILE
    row = lambda i: (i, 0)
    hrow = lambda i: (0, i, 0)
    hcol = lambda i: (0, 0, i)
    f32_256 = jax.ShapeDtypeStruct((r, 256), F32)
    bf_256 = jax.ShapeDtypeStruct((r, 256), BF16)
    wvt = w_p[:, C_GLA + 256:C_GLA + 512].T
    wdqt = w_p[:, C_DIFF:C_DIFF + 256].T
    wdvt = w_p[:, C_DIFF + 512:C_DIFF + 768].T
    cos_r, sin_r, cos_c, sin_c = rope
    out_shape = [f32_256, f32_256, jax.ShapeDtypeStruct((256, r), F32), f32_256, f32_256,
                 bf_256, bf_256, bf_256,
                 jax.ShapeDtypeStruct((DIFF_HEADS, 64, r), BF16),
                 jax.ShapeDtypeStruct((DIFF_HEADS, r, 64), BF16),
                 jax.ShapeDtypeStruct((DIFF_HEADS, 128, r), BF16),
                 f32_256]
    out_specs = [pl.BlockSpec((tm, 256), row)] * 2 + [pl.BlockSpec((256, tm), lambda i: (0, i))] + [
        pl.BlockSpec((tm, 256), row)] * 5 + [
        pl.BlockSpec((DIFF_HEADS, 64, tm), hcol),
        pl.BlockSpec((DIFF_HEADS, tm, 64), hrow),
        pl.BlockSpec((DIFF_HEADS, 128, tm), hcol),
        pl.BlockSpec((tm, 256), row)]
    return pl.pallas_call(
        _proj_body,
        grid=(n_tiles,),
        in_specs=[pl.BlockSpec((tm, d), row),
                  pl.BlockSpec((None, N_MOD, d), lambda i: (group_of(i), 0, 0)),
                  _resident((1, d)),
                  _resident(w_p.shape),
                  _resident(wvt.shape),
                  _resident(wdqt.shape),
                  _resident(wdvt.shape),
                  _resident(wa.shape),
                  _resident(ba.shape),
                  pl.BlockSpec((tm, LANES), lambda i: (pos_of(i), 0)),
                  pl.BlockSpec((tm, LANES), lambda i: (pos_of(i), 0)),
                  pl.BlockSpec((DIFF_DK, tm), lambda i: (0, pos_of(i))),
                  pl.BlockSpec((DIFF_DK, tm), lambda i: (0, pos_of(i)))],
        out_specs=out_specs,
        out_shape=out_shape,
        scratch_shapes=[pltpu.VMEM((tm, d), BF16)],
        compiler_params=_params("parallel"),
        name="proj",
    )(h, mods, norm_w.reshape(1, d), w_p, wvt, wdqt, wdvt, wa, ba, cos_r, sin_r, cos_c, sin_c)


def _permute_w_in(w_in):
    d = w_in.shape[0]
    g0 = 2 * GLA_HEADS * GLA_DK + 2 * GLA_HEADS * GLA_DV
    aux = w_in[:, g0:g0 + 2 * GLA_RANK]
    rest = w_in[:, g0 + 2 * GLA_RANK:]
    pad = jnp.zeros((d, C_END - C_AUX - 2 * GLA_RANK), w_in.dtype)
    return jnp.concatenate([w_in[:, :g0], rest, aux, pad], axis=1).astype(BF16)


def _gate_weights(wa_f, ba_f, wa_b, ba_b):
    n = GLA_HEADS * GLA_DK
    wa = jnp.zeros((C_END - C_AUX, 2 * n), F32)
    wa = wa.at[:GLA_RANK, :n].set(wa_f).at[GLA_RANK:2 * GLA_RANK, n:].set(wa_b)
    return wa.astype(BF16), jnp.concatenate([ba_f, ba_b]).reshape(1, 2 * n)


def _rope_tables(seq, tile):
    t = jnp.arange(seq)
    row = (t // GRID_W).astype(F32)
    col = (t % GRID_W).astype(F32)
    half = DIFF_DK // 2
    inv = 1.0 / (ROPE_BASE ** (jnp.arange(0, half, 2, dtype=F32) / half))
    ang_r = row[:, None] * inv
    ang_c = col[:, None] * inv
    ang = jnp.concatenate([ang_r, ang_r, ang_c, ang_c], axis=-1)
    cos = jnp.concatenate([jnp.cos(ang), jnp.ones((tile, DIFF_DK), F32)], axis=0)
    sin = jnp.concatenate([jnp.sin(ang), jnp.zeros((tile, DIFF_DK), F32)], axis=0)
    rep = (1, LANES // DIFF_DK)
    return jnp.tile(cos, rep), jnp.tile(sin, rep), cos.T, sin.T


def _split3(x):
    hi = x.astype(BF16)
    r1 = x - hi.astype(F32)
    mid = r1.astype(BF16)
    lo = (r1 - mid.astype(F32)).astype(BF16)
    return hi, mid, lo


def _gla_body(qk_ref, v_ref, vt_ref, lg_ref, g_ref, of_ref, nw_ref, o_ref, st_ref, *, reverse):
    blk = qk_ref.shape[0]
    n_chunks = blk // GLA_CHUNK
    nqk = GLA_HEADS * GLA_DK
    nv = GLA_HEADS * GLA_DV

    @pl.when(pl.program_id(1) == 0)
    def _():
        st_ref[...] = jnp.zeros_like(st_ref)

    ri = lax.broadcasted_iota(jnp.int32, (blk, blk), 0)
    ci = lax.broadcasted_iota(jnp.int32, (blk, blk), 1)
    same = (ri // GLA_CHUNK) == (ci // GLA_CHUNK)
    tri = jnp.where(same & ((ci >= ri) if reverse else (ci <= ri)), 1.0, 0.0).astype(BF16)
    lg = lg_ref[:, nqk:2 * nqk] if reverse else lg_ref[:, 0:nqk]
    hi, mid, lo = _split3(lg)
    bcum_all = _dot(tri, hi) + _dot(tri, mid) + _dot(tri, lo)

    c = GLA_CHUNK
    rk = lax.broadcasted_iota(jnp.int32, (GLA_HEADS * c, nqk), 0) // c
    ck = lax.broadcasted_iota(jnp.int32, (GLA_HEADS * c, nqk), 1) // GLA_DK
    mask_k = rk == ck
    rv = lax.broadcasted_iota(jnp.int32, (GLA_HEADS * c, nv), 0) // c
    cv = lax.broadcasted_iota(jnp.int32, (GLA_HEADS * c, nv), 1) // GLA_DV
    mask_v = rv == cv
    rs = lax.broadcasted_iota(jnp.int32, (nv, nqk), 0) // GLA_DV
    cs = lax.broadcasted_iota(jnp.int32, (nv, nqk), 1) // GLA_DK
    mask_s = rs == cs
    ai = lax.broadcasted_iota(jnp.int32, (c, GLA_HEADS * c), 0)
    aj = lax.broadcasted_iota(jnp.int32, (c, GLA_HEADS * c), 1) % c
    mask_a = (aj >= ai) if reverse else (aj <= ai)

    vt = vt_ref[...].astype(BF16)
    order = range(n_chunks - 1, -1, -1) if reverse else range(n_chunks)
    for ch in order:
        rows = slice(ch * c, (ch + 1) * c)
        q = qk_ref[rows, 0:nqk]
        k = qk_ref[rows, nqk:2 * nqk]
        v = v_ref[rows, :]
        bcum = bcum_all[rows, :]
        btot = bcum[0:1, :] if reverse else bcum[c - 1:c, :]
        q_in = (q * jnp.exp(bcum)).astype(BF16)
        k_in = k * jnp.exp(-bcum)
        k_out = (k * jnp.exp(btot - bcum)).astype(BF16)
        k_bd = jnp.where(mask_k, jnp.concatenate([k_in] * GLA_HEADS, axis=0), 0.0).astype(BF16)
        a = jnp.where(mask_a, _dot_nt(q_in, k_bd), 0.0).astype(BF16)
        v_bd = jnp.where(mask_v, jnp.concatenate([v] * GLA_HEADS, axis=0), 0.0).astype(BF16)
        st = st_ref[...]
        o = _dot(a, v_bd) + _dot_nt(q_in, st.astype(BF16))
        k_pad = jnp.concatenate([jnp.zeros((n * c, nqk), BF16) for n in (ch,) if n] + [k_out]
                                + [jnp.zeros((n * c, nqk), BF16) for n in (n_chunks - 1 - ch,) if n], axis=0)
        u_t = _dot(vt, k_pad)
        st_ref[...] = st * jnp.exp(btot) + jnp.where(mask_s, u_t, 0.0)
        if reverse:
            o = o + of_ref[rows, :]
            hi2, lo2, _ = _split3(o * o)
            hr = lax.broadcasted_iota(jnp.int32, (nv, nv), 0) // GLA_DV
            hc = lax.broadcasted_iota(jnp.int32, (nv, nv), 1) // GLA_DV
            seg = jnp.where(hr == hc, 1.0, 0.0).astype(BF16)
            ms = (_dot(hi2, seg) + _dot(lo2, seg)) * (1.0 / GLA_DV)
            y = o * lax.rsqrt(ms + RMS_EPS) * nw_ref[...]
            o_ref[rows, :] = y * _silu(g_ref[rows, :])
        else:
            o_ref[rows, :] = o


def _gla(gqk, gv, gvt, glg, gg, o_f, norm_w4, *, reverse, batch, seq, ctx):
    r = gqk.shape[0]
    blk = GLA_BLOCK
    nc, nl = ctx // blk, seq // blk
    ctx_base = batch * seq // blk

    def blk_of(b, s):
        if reverse:
            return jnp.where(s < nc, ctx_base + b * nc + (nc - 1 - s), b * nl + (nl - 1 - (s - nc)))
        return jnp.where(s < nc, ctx_base + b * nc + s, b * nl + (s - nc))

    spec = pl.BlockSpec((blk, 256), lambda b, s: (blk_of(b, s), 0))
    spec_t = pl.BlockSpec((256, blk), lambda b, s: (0, blk_of(b, s)))
    return pl.pallas_call(
        functools.partial(_gla_body, reverse=reverse),
        grid=(batch, nc + nl),
        in_specs=[spec, spec, spec_t, spec, spec, spec, _resident((1, 256))],
        out_specs=spec,
        out_shape=jax.ShapeDtypeStruct((r, 256), F32),
        scratch_shapes=[pltpu.VMEM((GLA_HEADS * GLA_DV, GLA_HEADS * GLA_DK), F32)],
        compiler_params=_params("arbitrary", "arbitrary"),
        name="gla_bwd" if reverse else "gla_fwd",
    )(gqk, gv, gvt, glg, gg, o_f, norm_w4)


def _na_bias_tables(rpb, rows):
    nq = NA_QROWS
    nkr = 4 * NA_KBLK // GRID_W
    n_blocks = rows // nq
    kb_total = rows * GRID_W // NA_KBLK
    tabs = []
    for j in (0, 1, n_blocks - 1):
        kr0 = min(max(2 * j - 1, 0), kb_total - 4) * (NA_KBLK // GRID_W)
        r = j * nq + jnp.arange(nq)
        rk = kr0 + jnp.arange(nkr)
        start = jnp.clip(r - NA_ROWS // 2, 0, rows - NA_ROWS)
        row_ok = (rk[None, :] >= start[:, None]) & (rk[None, :] < start[:, None] + NA_ROWS)
        row_off = jnp.clip(rk[None, :] - r[:, None] + (NA_ROWS - 1), 0, 2 * NA_ROWS - 2)
        col = jnp.arange(GRID_W)
        cstart = jnp.clip(col - NA_COLS // 2, 0, GRID_W - NA_COLS)
        col_ok = (col[None, :] >= cstart[:, None]) & (col[None, :] < cstart[:, None] + NA_COLS)
        col_off = jnp.clip(col[None, :] - col[:, None] + (NA_COLS - 1), 0, 2 * NA_COLS - 2)
        b = rpb[:, row_off][..., col_off]
        ok = row_ok[:, :, None, None] & col_ok[None, None, :, :]
        b = jnp.where(ok[None], b, NEG_INF)
        b = jnp.transpose(b, (0, 1, 3, 2, 4)).reshape(NA_HEADS, nq * GRID_W, nkr * GRID_W)
        tabs.append(b)
    return jnp.stack(tabs).astype(F32)


def _na_body(q_ref, k0, k1, k2, k3, v0, v1, v2, v3, kc_ref, vc_ref, bias_ref, o_ref):
    q = q_ref[...]
    kw = jnp.concatenate([k0[...], k1[...], k2[...], k3[...]], axis=0)
    vw = jnp.concatenate([v0[...], v1[...], v2[...], v3[...]], axis=0)
    kc = kc_ref[...]
    vc = vc_ref[...]
    lane = lax.broadcasted_iota(jnp.int32, (1, NA_HEADS * NA_DIM), 1) // NA_DIM
    out = jnp.zeros(o_ref.shape, F32)
    for hh in range(NA_HEADS):
        mh = lane == hh
        qh = jnp.where(mh, q, jnp.zeros_like(q))
        s_w = _dot_nt(qh, kw) + bias_ref[hh]
        s_c = _dot_nt(qh, kc)
        m = jnp.maximum(jnp.max(s_w, axis=-1, keepdims=True), jnp.max(s_c, axis=-1, keepdims=True))
        p_w = jnp.exp(s_w - m)
        p_c = jnp.exp(s_c - m)
        l = jnp.sum(p_w, axis=-1, keepdims=True) + jnp.sum(p_c, axis=-1, keepdims=True)
        o = _dot(p_w.astype(BF16), vw) + _dot(p_c.astype(BF16), vc)
        out = out + jnp.where(mh, o / l, 0.0)
    o_ref[...] = out


def _na(nq, nk, nv, bias, *, batch, seq, ctx):
    r = nq.shape[0]
    tq = NA_QROWS * GRID_W
    nj = seq // tq
    kb = seq // NA_KBLK
    assert NA_KBLK % ctx == 0 or ctx % NA_KBLK == 0
    ctx_blk0 = batch * seq // ctx

    def kspec(i):
        return pl.BlockSpec((NA_KBLK, 256),
                            lambda b, j: (b * kb + jnp.clip(2 * j - 1, 0, kb - 4) + i, 0))

    cspec = pl.BlockSpec((ctx, 256), lambda b, j: (ctx_blk0 + b, 0))
    variant = lambda j: jnp.where(j == 0, 0, jnp.where(j == nj - 1, 2, 1))
    return pl.pallas_call(
        _na_body,
        grid=(batch, nj),
        in_specs=[pl.BlockSpec((tq, 256), lambda b, j: (b * nj + j, 0))]
                 + [kspec(i) for i in range(4)] + [kspec(i) for i in range(4)]
                 + [cspec, cspec,
                    pl.BlockSpec((None, NA_HEADS, tq, 4 * NA_KBLK), lambda b, j: (variant(j), 0, 0, 0))],
        out_specs=pl.BlockSpec((tq, 256), lambda b, j: (b * nj + j, 0)),
        out_shape=jax.ShapeDtypeStruct((r, 256), F32),
        compiler_params=_params("parallel", "arbitrary"),
        name="na",
    )(nq, nk, nk, nk, nk, nv, nv, nv, nv, nk, nv, bias)


def _diff_lambda(lq_ref, lambda_init):
    lq = lq_ref[...]
    s1 = jnp.sum(lq[0:1, :] * lq[1:2, :], axis=-1, keepdims=True)
    s2 = jnp.sum(lq[2:3, :] * lq[3:4, :], axis=-1, keepdims=True)
    return jnp.exp(s1) - jnp.exp(s2) + lambda_init


def _stack_qt(qt):
    row = lax.broadcasted_iota(jnp.int32, (2 * DIFF_DK, 1), 0)
    zero = jnp.zeros_like(qt)
    return jnp.concatenate([jnp.where(row < DIFF_DK, qt, zero), jnp.where(row >= DIFF_DK, qt, zero)], axis=1)


def _diff_finish_t(acc, tq, lam, nw_col, lambda_init):
    o = acc[0:DIFF_DV, :] / acc[DIFF_DV:DIFF_DV + 1, :]
    od = o[:, :tq] - lam * o[:, tq:]
    ms = jnp.mean(od * od, axis=0, keepdims=True)
    y = od * lax.rsqrt(ms + RMS_EPS) * nw_col * (1.0 - lambda_init)
    y = jnp.concatenate([y, jnp.zeros_like(y)], axis=0)
    return jnp.transpose(y)[:, 0:DIFF_DV]


def _diff_body(qt_ref, kc_ref, vtc_ref, k_ref, vt_ref, lq_ref, nw_ref, o_ref, *, tk, unroll, lambda_init):
    tq = qt_ref.shape[1]
    n_chunks = k_ref.shape[0] // tk
    qs = _stack_qt(qt_ref[...])

    def scores(j):
        off = pl.multiple_of(j * tk, tk)
        return _dot(k_ref[pl.ds(off, tk), :], qs)

    s_ctx = _dot(kc_ref[...], qs)
    s_first = scores(0)
    m0 = jnp.max(s_ctx, axis=0, keepdims=True)
    acc0 = _dot(vtc_ref[...], jnp.exp2(s_ctx - m0).astype(BF16))

    def group(g, carry):
        s_cur, m, acc = carry
        for u in range(unroll):
            j = g * unroll + u
            s_next = scores(jnp.minimum(j + 1, n_chunks - 1))
            m_new = jnp.maximum(m, jnp.max(s_cur, axis=0, keepdims=True))
            p = jnp.exp2(s_cur - m_new).astype(BF16)
            off = pl.multiple_of(j * tk, tk)
            acc = jnp.exp2(m - m_new) * acc + _dot(vt_ref[:, pl.ds(off, tk)], p)
            m, s_cur = m_new, s_next
        return s_cur, m, acc

    _, _, acc = lax.fori_loop(0, n_chunks // unroll, group, (s_first, m0, acc0))
    o_ref[...] = _diff_finish_t(acc, tq, _diff_lambda(lq_ref, lambda_init), nw_ref[...], lambda_init)


def _diff(dqt, dk, dvt, lq, norm_w, *, batch, seq, ctx, lambda_init):
    h, r, _ = dk.shape
    tq = DIFF_TQ
    nq = seq // tq
    tk = min(DIFF_TK, seq)
    ctx_blk0 = batch * seq // ctx
    return pl.pallas_call(
        functools.partial(_diff_body, tk=tk, unroll=DIFF_UNROLL, lambda_init=lambda_init),
        grid=(batch, h, nq),
        in_specs=[pl.BlockSpec((None, 64, tq), lambda b, hh, i: (hh, 0, b * nq + i)),
                  pl.BlockSpec((None, ctx, 64), lambda b, hh, i: (hh, ctx_blk0 + b, 0)),
                  pl.BlockSpec((None, 128, ctx), lambda b, hh, i: (hh, 0, ctx_blk0 + b)),
                  pl.BlockSpec((None, seq, 64), lambda b, hh, i: (hh, b, 0)),
                  pl.BlockSpec((None, 128, seq), lambda b, hh, i: (hh, 0, b)),
                  _resident(lq.shape),
                  _resident((DIFF_DV, 1))],
        out_specs=pl.BlockSpec((None, tq, 64), lambda b, hh, i: (hh, b * nq + i, 0)),
        out_shape=jax.ShapeDtypeStruct((h, r, 64), F32),
        compiler_params=_params("parallel", "parallel", "arbitrary"),
        name="diff",
    )(dqt, dk, dvt, dk, dvt, lq, norm_w.reshape(DIFF_DV, 1))


def _ctx_attn_body(nq_ref, nk_ref, nv_ref, dqt_ref, dk_ref, dvt_ref, lq_ref, nw_ref, na_in, df_in,
                   na_out, df_out, *, lambda_init):
    del na_in, df_in
    q = nq_ref[...]
    k = nk_ref[...]
    v = nv_ref[...]
    lane = lax.broadcasted_iota(jnp.int32, (1, NA_HEADS * NA_DIM), 1) // NA_DIM
    out = jnp.zeros(na_out.shape, F32)
    for hh in range(NA_HEADS):
        mh = lane == hh
        s = _dot_nt(jnp.where(mh, q, jnp.zeros_like(q)), k)
        p = jnp.exp(s - jnp.max(s, axis=-1, keepdims=True))
        o = _dot(p.astype(BF16), v) / jnp.sum(p, axis=-1, keepdims=True)
        out = out + jnp.where(mh, o, 0.0)
    na_out[...] = out

    lam = _diff_lambda(lq_ref, lambda_init)
    tq = dqt_ref.shape[2]
    for hh in range(DIFF_HEADS):
        s = _dot(dk_ref[hh], _stack_qt(dqt_ref[hh]))
        p = jnp.exp2(s - jnp.max(s, axis=0, keepdims=True))
        acc = _dot(dvt_ref[hh], p.astype(BF16))
        df_out[hh] = _diff_finish_t(acc, tq, lam, nw_ref[...], lambda_init)


def _ctx_attn(nq, nk, nv, dqt, dk, dvt, lq, norm_w, na_o, df_o, *, batch, seq, ctx, lambda_init):
    blk0 = batch * seq // ctx
    s256 = pl.BlockSpec((ctx, 256), lambda b: (blk0 + b, 0))
    sh64 = pl.BlockSpec((DIFF_HEADS, ctx, 64), lambda b: (0, blk0 + b, 0))
    sq_t = pl.BlockSpec((DIFF_HEADS, 64, ctx), lambda b: (0, 0, blk0 + b))
    sv_t = pl.BlockSpec((DIFF_HEADS, 128, ctx), lambda b: (0, 0, blk0 + b))
    return pl.pallas_call(
        functools.partial(_ctx_attn_body, lambda_init=lambda_init),
        grid=(batch,),
        in_specs=[s256, s256, s256, sq_t, sh64, sv_t, _resident(lq.shape), _resident((DIFF_DV, 1)),
                  pl.BlockSpec(memory_space=pl.ANY), pl.BlockSpec(memory_space=pl.ANY)],
        out_specs=[s256, sh64],
        out_shape=[jax.ShapeDtypeStruct(na_o.shape, F32), jax.ShapeDtypeStruct(df_o.shape, F32)],
        input_output_aliases={8: 0, 9: 1},
        compiler_params=_params("arbitrary"),
        name="ctx_attn",
    )(nq, nk, nv, dqt, dk, dvt, lq, norm_w.reshape(DIFF_DV, 1), na_o, df_o)


def _conv_body(u_ref, prev_ref, next_ref, dw_ref, dwb_ref, lng_ref, lnb_ref, pw_ref, pwb_ref,
               o_ref, pad_ref, *, tiles_per_seq, n_latent_tiles):
    t = u_ref.shape[0]
    i = pl.program_id(0)
    in_ctx = i >= n_latent_tiles
    first = in_ctx | (i % tiles_per_seq == 0)
    last = in_ctx | (i % tiles_per_seq == tiles_per_seq - 1)
    pad_ref[0:HALO, :] = jnp.where(first, 0.0, prev_ref[...])
    pad_ref[HALO:HALO + t, :] = u_ref[...]
    pad_ref[HALO + t:HALO + t + HALO, :] = jnp.where(last, 0.0, next_ref[...])
    acc = jnp.zeros((t, CONV_CH), F32)
    base = HALO - CONV_K // 2
    for k in range(CONV_K):
        acc = acc + dw_ref[k:k + 1, :] * pad_ref[base + k:base + k + t, :]
    y = acc + dwb_ref[...]
    mu = jnp.mean(y, axis=-1, keepdims=True)
    yc = y - mu
    var = jnp.mean(yc * yc, axis=-1, keepdims=True)
    y = _silu(yc * lax.rsqrt(var + LN_EPS) * lng_ref[...] + lnb_ref[...])
    o_ref[...] = _dot(y.astype(BF16), pw_ref[...]) + pwb_ref[...]


def _conv(cu, dw, dw_b, ln_g, ln_b, pw, pw_b, *, batch, seq, ctx):
    r = cu.shape[0]
    t = CONV_TILE
    assert ctx == t and seq % t == 0
    n_tiles = r // t
    hb = t // HALO
    vec = lambda a: a.reshape(1, CONV_CH)
    return pl.pallas_call(
        functools.partial(_conv_body, tiles_per_seq=seq // t, n_latent_tiles=batch * seq // t),
        grid=(n_tiles,),
        in_specs=[pl.BlockSpec((t, CONV_CH), lambda i: (i, 0)),
                  pl.BlockSpec((HALO, CONV_CH), lambda i: (jnp.maximum(i * hb - 1, 0), 0)),
                  pl.BlockSpec((HALO, CONV_CH), lambda i: (jnp.minimum((i + 1) * hb, n_tiles * hb - 1), 0)),
                  _resident((CONV_K, CONV_CH))] + [_resident((1, CONV_CH))] * 3
                 + [_resident((CONV_CH, CONV_CH)), _resident((1, CONV_CH))],
        out_specs=pl.BlockSpec((t, CONV_CH), lambda i: (i, 0)),
        out_shape=jax.ShapeDtypeStruct((r, CONV_CH), F32),
        scratch_shapes=[pltpu.VMEM((t + 2 * HALO, CONV_CH), F32)],
        compiler_params=_params("parallel"),
        name="conv",
    )(cu, cu, cu, dw, vec(dw_b), vec(ln_g), vec(ln_b), pw.astype(BF16), vec(pw_b))


def _wout_body(h_ref, mod_ref, gx_ref, nx_ref, dx_ref, cx_ref, w_ref, o_ref):
    mix = jnp.concatenate([gx_ref[...], nx_ref[...]] + [dx_ref[hh] for hh in range(DIFF_HEADS)]
                          + [cx_ref[...]], axis=-1).astype(BF16)
    o_ref[...] = h_ref[...] + mod_ref[5:6, :] * _dot(mix, w_ref[...])


def _wout(h, mods, gx, nx, dx, cx, w_out, *, n_tiles, group_of):
    r, d = h.shape
    tm = ROW_TILE
    row = lambda i: (i, 0)
    return pl.pallas_call(
        _wout_body,
        grid=(n_tiles,),
        in_specs=[pl.BlockSpec((tm, d), row),
                  pl.BlockSpec((None, N_MOD, d), lambda i: (group_of(i), 0, 0)),
                  pl.BlockSpec((tm, 256), row), pl.BlockSpec((tm, 256), row),
                  pl.BlockSpec((DIFF_HEADS, tm, 64), lambda i: (0, i, 0)),
                  pl.BlockSpec((tm, 256), row),
                  _resident(w_out.shape)],
        out_specs=pl.BlockSpec((tm, d), row),
        out_shape=jax.ShapeDtypeStruct((n_tiles * tm, d), F32),
        compiler_params=_params("parallel"),
        name="wout",
    )(h, mods, gx, nx, dx, cx, w_out.astype(BF16))


def kernel(x, c, ctx, c_ctx, ada_w, ada_b, norm_ffn1, ffn1_w13, ffn1_w2, norm_mix, w_in, gla_wa_f, gla_ba_f, gla_wa_b, gla_ba_b, gla_norm, na_rpb, diff_lq1, diff_lk1, diff_lq2, diff_lk2, diff_norm, conv_dw, conv_dw_b, conv_ln_g, conv_ln_b, conv_pw, conv_pw_b, w_out, norm_ffn2, ffn2_w13, ffn2_w2, final_norm):
    batch, seq, d = x.shape
    n_ctx = ctx.shape[1]
    depth = ada_w.shape[0]
    rows = seq // GRID_W
    tm = ROW_TILE
    assert seq % tm == 0 and (batch * n_ctx) % tm == 0 and batch + 1 <= 8
    lat_tiles = batch * seq // tm
    all_tiles = lat_tiles + batch * n_ctx // tm
    tiles_per_batch = seq // tm

    def group_of(i):
        return jnp.minimum(i // tiles_per_batch, batch)

    def pos_of(i):
        return jnp.where(i < lat_tiles, i % tiles_per_batch, tiles_per_batch)

    c_rows = jnp.concatenate([c, c_ctx[None, :], jnp.zeros((8 - batch - 1, d), F32)], axis=0)
    mods_all = _ada(c_rows, ada_w, ada_b)[:, :batch + 1].reshape(depth, batch + 1, N_MOD, d)
    rope = _rope_tables(seq, tm)
    h = jnp.concatenate([x.reshape(batch * seq, d), ctx.reshape(batch * n_ctx, d)], axis=0)

    for i in range(depth):
        last = i == depth - 1
        lambda_init = 0.8 - 0.6 * math.exp(-0.3 * i)
        mods = mods_all[i]
        tok = dict(n_tiles=all_tiles, group_of=group_of)
        geo = dict(batch=batch, seq=seq, ctx=n_ctx)

        w13c, w2c = _chunk_ffn_weights(ffn1_w13[i], ffn1_w2[i])
        h = _ffn(h, mods, norm_ffn1[i], w13c, w2c, final_norm, k0=0, final=False, **tok)

        wa, ba = _gate_weights(gla_wa_f[i], gla_ba_f[i], gla_wa_b[i], gla_ba_b[i])
        (gqk, gv, gvt, gg, glg, nq, nk, nv, dq, dk, dv, cu) = _proj(
            h, mods, norm_mix[i], _permute_w_in(w_in[i]), wa, ba, rope, pos_of=pos_of, **tok)

        gnorm = jnp.tile(gla_norm[i], GLA_HEADS).reshape(1, GLA_HEADS * GLA_DV)
        o_f = _gla(gqk, gv, gvt, glg, gg, gg, gnorm, reverse=False, **geo)
        gx = _gla(gqk, gv, gvt, glg, gg, o_f, gnorm, reverse=True, **geo)

        nx = _na(nq, nk, nv, _na_bias_tables(na_rpb[i], rows), **geo)

        lq = jnp.stack([diff_lq1[i], diff_lk1[i], diff_lq2[i], diff_lk2[i]])
        dx = _diff(dq, dk, dv, lq, diff_norm[i], lambda_init=lambda_init, **geo)
        if not last:
            nx, dx = _ctx_attn(nq, nk, nv, dq, dk, dv, lq, diff_norm[i], nx, dx,
                               lambda_init=lambda_init, **geo)

        cx = _conv(cu, conv_dw[i], conv_dw_b[i], conv_ln_g[i], conv_ln_b[i], conv_pw[i], conv_pw_b[i], **geo)

        if last:
            tok = dict(n_tiles=lat_tiles, group_of=group_of)
        h = _wout(h, mods, gx, nx, dx, cx, w_out[i], **tok)

        w13c, w2c = _chunk_ffn_weights(ffn2_w13[i], ffn2_w2[i])
        h = _ffn(h, mods, norm_ffn2[i], w13c, w2c, final_norm, k0=6, final=last, **tok)

    return h[:batch * seq].reshape(batch, seq, d)
```

```python
import functools
import math

import jax
import jax.numpy as jnp
from jax import lax
from jax.experimental import pallas as pl
from jax.experimental.pallas import tpu as pltpu

F32 = jnp.float32
BF16 = jnp.bfloat16

GRID_W = 64
N_MOD = 9
RMS_EPS = 1e-6
LN_EPS = 1e-5
NEG_INF = -1e30
GLA_HEADS, GLA_DK, GLA_DV, GLA_RANK, GLA_TAU, GLA_CHUNK = 4, 32, 64, 16, 16.0, 64
NA_HEADS, NA_DIM, NA_ROWS, NA_COLS = 4, 64, 8, 16
DIFF_HEADS, DIFF_DK, DIFF_DV = 4, 32, 64
CONV_CH, CONV_K = 256, 31
ROPE_BASE = 10000.0
LOG2E = 1.4426950408889634

LANES = 128
VMEM_LIMIT = 56 * 1024 * 1024

ROW_TILE = 512
FF_CHUNK = 256
GLA_BLOCK = 256
NA_QROWS = 8
NA_KBLK = 256
CONV_TILE = 256
DIFF_TQ = 256
DIFF_TK = 512
DIFF_UNROLL = 8
DIFF_AHEAD = 2
DIFF_RING = 4
HALO = 16

C_GLA, C_NA, C_DIFF, C_CONV, C_AUX, C_END = 0, 768, 1536, 2304, 2816, 2944


def _dot(a, b):
    return jnp.dot(a, b, preferred_element_type=F32)


def _dot_nt(a, b):
    return lax.dot_general(a, b, (((1,), (1,)), ((), ())), preferred_element_type=F32)


def _params(*sem):
    return pltpu.CompilerParams(dimension_semantics=sem, vmem_limit_bytes=VMEM_LIMIT)


def _resident(shape):
    nd = len(shape)
    return pl.BlockSpec(shape, lambda *_: (0,) * nd, pipeline_mode=pl.Buffered(1))


def _silu(x):
    return x * jax.nn.sigmoid(x)


def _rms(x, w):
    return x * lax.rsqrt(jnp.mean(x * x, axis=-1, keepdims=True) + RMS_EPS) * w


def _ada_body(c_ref, w_ref, b_ref, o_ref):
    s = _silu(c_ref[...])
    o_ref[...] = jnp.dot(s, w_ref[...], precision=lax.Precision.HIGHEST,
                         preferred_element_type=F32) + b_ref[...]


def _ada(c_rows, ada_w, ada_b):
    depth, d, _ = ada_w.shape
    return pl.pallas_call(
        _ada_body,
        grid=(depth, N_MOD),
        in_specs=[pl.BlockSpec((8, d), lambda l, n: (0, 0)),
                  pl.BlockSpec((None, d, d), lambda l, n: (l, 0, n)),
                  pl.BlockSpec((None, 1, d), lambda l, n: (l, 0, n))],
        out_specs=pl.BlockSpec((None, 8, d), lambda l, n: (l, 0, n)),
        out_shape=jax.ShapeDtypeStruct((depth, 8, N_MOD * d), F32),
        compiler_params=_params("arbitrary", "arbitrary"),
        name="ada",
    )(c_rows, ada_w, ada_b.reshape(depth, 1, N_MOD * d))


def _ffn_body(h_ref, mod_ref, nw_ref, w13_ref, w2_ref, fw_ref, o_ref, xb_ref, acc_ref, h13_ref, *, k0, final):
    n_chunks, _, two_tf = w13_ref.shape
    tf = two_tf // 2
    x = h_ref[...]
    xm = _rms(x, nw_ref[...]) * (1.0 + mod_ref[k0 + 1:k0 + 2, :]) + mod_ref[k0:k0 + 1, :]
    xb_ref[...] = xm.astype(BF16)
    acc_ref[...] = jnp.zeros_like(acc_ref)

    def up(c, slot):
        h13_ref[slot] = _dot(xb_ref[...], w13_ref[c])

    def down(c, slot):
        a = h13_ref[slot, :, :tf]
        u = h13_ref[slot, :, tf:]
        acc_ref[...] += _dot((_silu(a) * u).astype(BF16), w2_ref[c])

    def pair(t, carry):
        up(2 * t + 1, 1)
        down(2 * t, 0)
        up(2 * t + 2, 0)
        down(2 * t + 1, 1)
        return carry

    up(0, 0)
    lax.fori_loop(0, (n_chunks - 1) // 2, pair, 0)
    down(n_chunks - 1, 0)
    out = x + (0.5 * mod_ref[k0 + 2:k0 + 3, :]) * acc_ref[...]
    if final:
        out = _rms(out, fw_ref[...])
    o_ref[...] = out


def _ffn(h, mods, norm_w, w13c, w2c, final_w, *, k0, n_tiles, group_of, final):
    r, d = h.shape
    tm = ROW_TILE
    assert w13c.shape[0] % 2 == 1
    return pl.pallas_call(
        functools.partial(_ffn_body, k0=k0, final=final),
        grid=(n_tiles,),
        in_specs=[pl.BlockSpec((tm, d), lambda i: (i, 0)),
                  pl.BlockSpec((None, N_MOD, d), lambda i: (group_of(i), 0, 0)),
                  _resident((1, d)),
                  _resident(w13c.shape),
                  _resident(w2c.shape),
                  _resident((1, d))],
        out_specs=pl.BlockSpec((tm, d), lambda i: (i, 0)),
        out_shape=jax.ShapeDtypeStruct((n_tiles * tm, d), F32),
        scratch_shapes=[pltpu.VMEM((tm, d), BF16), pltpu.VMEM((tm, d), F32),
                        pltpu.VMEM((2, tm, w13c.shape[2]), F32)],
        compiler_params=_params("parallel"),
        name="ffn",
    )(h, mods, norm_w.reshape(1, d), w13c, w2c, final_w.reshape(1, d))


def _chunk_ffn_weights(w13, w2):
    d, two_ff = w13.shape
    ff = two_ff // 2
    nc = ff // FF_CHUNK
    w1 = w13[:, :ff].reshape(d, nc, FF_CHUNK)
    w3 = w13[:, ff:].reshape(d, nc, FF_CHUNK)
    w13c = jnp.transpose(jnp.concatenate([w1, w3], axis=-1), (1, 0, 2)).astype(BF16)
    w2c = w2.reshape(nc, FF_CHUNK, d).astype(BF16)
    return w13c, w2c


def _log_sigmoid(x):
    return jnp.minimum(x, 0.0) - jnp.log1p(jnp.exp(-jnp.abs(x)))


def _rope_rotate(x):
    n = x.shape[-1]
    lane = lax.broadcasted_iota(jnp.int32, x.shape, 1)
    up = pltpu.roll(x, n - 8, 1)
    dn = pltpu.roll(x, 8, 1)
    return jnp.where((lane & 15) < 8, -up, dn)


def _rope_rotate_rows(x):
    n = x.shape[0]
    row = lax.broadcasted_iota(jnp.int32, x.shape, 0)
    up = pltpu.roll(x, n - 8, 0)
    dn = pltpu.roll(x, 8, 0)
    return jnp.where((row & 15) < 8, -up, dn)


def _proj_body(h_ref, mod_ref, nw_ref, w_ref, wvt_ref, wdqt_ref, wdvt_ref, wa_ref, ba_ref,
               cos_ref, sin_ref, cost_ref, sint_ref,
               gqk_ref, gv_ref, gvt_ref, gg_ref, glg_ref, nq_ref, nk_ref, nv_ref,
               dqt_ref, dk_ref, dvt_ref, cu_ref, xb_ref):
    x = h_ref[...]
    tm = x.shape[0]
    xm = _rms(x, nw_ref[...]) * (1.0 + mod_ref[4:5, :]) + mod_ref[3:4, :]
    xb_ref[...] = xm.astype(BF16)

    z = _dot(xb_ref[...], w_ref[:, C_GLA:C_GLA + 256])
    lane = lax.broadcasted_iota(jnp.int32, (1, 256), 1)
    gqk_ref[...] = z * jnp.where(lane < 128, GLA_DK ** -0.5, 1.0)
    gv_ref[...] = _dot(xb_ref[...], w_ref[:, C_GLA + 256:C_GLA + 512])
    gvt_ref[...] = _dot_nt(wvt_ref[...], xb_ref[...])
    gg_ref[...] = _dot(xb_ref[...], w_ref[:, C_GLA + 512:C_GLA + 768])
    aux = _dot(xb_ref[...], w_ref[:, C_AUX:C_END])
    pre = _dot(aux.astype(BF16), wa_ref[...]) + ba_ref[...]
    glg_ref[...] = _log_sigmoid(pre) * (1.0 / GLA_TAU)

    nq_ref[...] = (_dot(xb_ref[...], w_ref[:, C_NA:C_NA + 256]) * (NA_DIM ** -0.5)).astype(BF16)
    nk_ref[...] = _dot(xb_ref[...], w_ref[:, C_NA + 256:C_NA + 512]).astype(BF16)
    nv_ref[...] = _dot(xb_ref[...], w_ref[:, C_NA + 512:C_NA + 768]).astype(BF16)

    cos = cos_ref[...]
    sin = sin_ref[...]
    cos2 = jnp.concatenate([cos, cos], axis=1)
    sin2 = jnp.concatenate([sin, sin], axis=1)
    zk = _dot(xb_ref[...], w_ref[:, C_DIFF + 256:C_DIFF + 512])
    zk = zk * cos2 + _rope_rotate(zk) * sin2
    n_rep = 2 * DIFF_HEADS
    cos_t = jnp.concatenate([cost_ref[...]] * n_rep, axis=0)
    sin_t = jnp.concatenate([sint_ref[...]] * n_rep, axis=0)
    zqt = _dot_nt(wdqt_ref[...], xb_ref[...])
    zqt = (zqt * cos_t + _rope_rotate_rows(zqt) * sin_t) * (DIFF_DK ** -0.5 * LOG2E)
    zvt = _dot_nt(wdvt_ref[...], xb_ref[...])
    one_row = jnp.where(lax.broadcasted_iota(jnp.int32, (64, tm), 0) == 0, 1.0, 0.0).astype(BF16)
    for hh in range(DIFF_HEADS):
        sl = slice(64 * hh, 64 * hh + 64)
        dk_ref[hh] = zk[:, sl].astype(BF16)
        dqt_ref[hh] = zqt[sl, :].astype(BF16)
        dvt_ref[hh, 0:64, :] = zvt[sl, :].astype(BF16)
        dvt_ref[hh, 64:128, :] = one_row

    za = _dot(xb_ref[...], w_ref[:, C_CONV:C_CONV + 256])
    zg = _dot(xb_ref[...], w_ref[:, C_CONV + 256:C_CONV + 512])
    cu_ref[...] = za * jax.nn.sigmoid(zg)


def _proj(h, mods, norm_w, w_p, wa, ba, rope, *, n_tiles, group_of, pos_of):
    r, d = h.shape
    tm = ROW_TILE
    row = lambda i: (i, 0)
    hrow = lambda i: (0, i, 0)
    hcol = lambda i: (0, 0, i)
    f32_256 = jax.ShapeDtypeStruct((r, 256), F32)
    bf_256 = jax.ShapeDtypeStruct((r, 256), BF16)
    wvt = w_p[:, C_GLA + 256:C_GLA + 512].T
    wdqt = w_p[:, C_DIFF:C_DIFF + 256].T
    wdvt = w_p[:, C_DIFF + 512:C_DIFF + 768].T
    cos_r, sin_r, cos_c, sin_c = rope
    out_shape = [f32_256, f32_256, jax.ShapeDtypeStruct((256, r), F32), f32_256, f32_256,
                 bf_256, bf_256, bf_256,
                 jax.ShapeDtypeStruct((DIFF_HEADS, 64, r), BF16),
                 jax.ShapeDtypeStruct((DIFF_HEADS, r, 64), BF16),
                 jax.ShapeDtypeStruct((DIFF_HEADS, 128, r), BF16),
                 f32_256]
    out_specs = [pl.BlockSpec((tm, 256), row)] * 2 + [pl.BlockSpec((256, tm), lambda i: (0, i))] + [
        pl.BlockSpec((tm, 256), row)] * 5 + [
        pl.BlockSpec((DIFF_HEADS, 64, tm), hcol),
        pl.BlockSpec((DIFF_HEADS, tm, 64), hrow),
        pl.BlockSpec((DIFF_HEADS, 128, tm), hcol),
        pl.BlockSpec((tm, 256), row)]
    return pl.pallas_call(
        _proj_body,
        grid=(n_tiles,),
        in_specs=[pl.BlockSpec((tm, d), row),
                  pl.BlockSpec((None, N_MOD, d), lambda i: (group_of(i), 0, 0)),
                  _resident((1, d)),
                  _resident(w_p.shape),
                  _resident(wvt.shape),
                  _resident(wdqt.shape),
                  _resident(wdvt.shape),
                  _resident(wa.shape),
                  _resident(ba.shape),
                  pl.BlockSpec((tm, LANES), lambda i: (pos_of(i), 0)),
                  pl.BlockSpec((tm, LANES), lambda i: (pos_of(i), 0)),
                  pl.BlockSpec((DIFF_DK, tm), lambda i: (0, pos_of(i))),
                  pl.BlockSpec((DIFF_DK, tm), lambda i: (0, pos_of(i)))],
        out_specs=out_specs,
        out_shape=out_shape,
        scratch_shapes=[pltpu.VMEM((tm, d), BF16)],
        compiler_params=_params("parallel"),
        name="proj",
    )(h, mods, norm_w.reshape(1, d), w_p, wvt, wdqt, wdvt, wa, ba, cos_r, sin_r, cos_c, sin_c)


def _permute_w_in(w_in):
    d = w_in.shape[0]
    g0 = 2 * GLA_HEADS * GLA_DK + 2 * GLA_HEADS * GLA_DV
    aux = w_in[:, g0:g0 + 2 * GLA_RANK]
    rest = w_in[:, g0 + 2 * GLA_RANK:]
    pad = jnp.zeros((d, C_END - C_AUX - 2 * GLA_RANK), w_in.dtype)
    return jnp.concatenate([w_in[:, :g0], rest, aux, pad], axis=1).astype(BF16)


def _gate_weights(wa_f, ba_f, wa_b, ba_b):
    n = GLA_HEADS * GLA_DK
    wa = jnp.zeros((C_END - C_AUX, 2 * n), F32)
    wa = wa.at[:GLA_RANK, :n].set(wa_f).at[GLA_RANK:2 * GLA_RANK, n:].set(wa_b)
    return wa.astype(BF16), jnp.concatenate([ba_f, ba_b]).reshape(1, 2 * n)


def _rope_tables(seq, tile):
    t = jnp.arange(seq)
    row = (t // GRID_W).astype(F32)
    col = (t % GRID_W).astype(F32)
    half = DIFF_DK // 2
    inv = 1.0 / (ROPE_BASE ** (jnp.arange(0, half, 2, dtype=F32) / half))
    ang_r = row[:, None] * inv
    ang_c = col[:, None] * inv
    ang = jnp.concatenate([ang_r, ang_r, ang_c, ang_c], axis=-1)
    cos = jnp.concatenate([jnp.cos(ang), jnp.ones((tile, DIFF_DK), F32)], axis=0)
    sin = jnp.concatenate([jnp.sin(ang), jnp.zeros((tile, DIFF_DK), F32)], axis=0)
    rep = (1, LANES // DIFF_DK)
    return jnp.tile(cos, rep), jnp.tile(sin, rep), cos.T, sin.T


def _split3(x):
    hi = x.astype(BF16)
    r1 = x - hi.astype(F32)
    mid = r1.astype(BF16)
    lo = (r1 - mid.astype(F32)).astype(BF16)
    return hi, mid, lo


def _gla_body(qk_ref, v_ref, vt_ref, lg_ref, g_ref, of_ref, nw_ref, o_ref, st_ref, *, reverse):
    blk = qk_ref.shape[0]
    n_chunks = blk // GLA_CHUNK
    nqk = GLA_HEADS * GLA_DK
    nv = GLA_HEADS * GLA_DV

    @pl.when(pl.program_id(1) == 0)
    def _():
        st_ref[...] = jnp.zeros_like(st_ref)

    ri = lax.broadcasted_iota(jnp.int32, (blk, blk), 0)
    ci = lax.broadcasted_iota(jnp.int32, (blk, blk), 1)
    same = (ri // GLA_CHUNK) == (ci // GLA_CHUNK)
    tri = jnp.where(same & ((ci >= ri) if reverse else (ci <= ri)), 1.0, 0.0).astype(BF16)
    lg = lg_ref[:, nqk:2 * nqk] if reverse else lg_ref[:, 0:nqk]
    hi, mid, lo = _split3(lg)
    bcum_all = _dot(tri, hi) + _dot(tri, mid) + _dot(tri, lo)

    c = GLA_CHUNK
    rk = lax.broadcasted_iota(jnp.int32, (GLA_HEADS * c, nqk), 0) // c
    ck = lax.broadcasted_iota(jnp.int32, (GLA_HEADS * c, nqk), 1) // GLA_DK
    mask_k = rk == ck
    rv = lax.broadcasted_iota(jnp.int32, (GLA_HEADS * c, nv), 0) // c
    cv = lax.broadcasted_iota(jnp.int32, (GLA_HEADS * c, nv), 1) // GLA_DV
    mask_v = rv == cv
    rs = lax.broadcasted_iota(jnp.int32, (nv, nqk), 0) // GLA_DV
    cs = lax.broadcasted_iota(jnp.int32, (nv, nqk), 1) // GLA_DK
    mask_s = rs == cs
    ai = lax.broadcasted_iota(jnp.int32, (c, GLA_HEADS * c), 0)
    aj = lax.broadcasted_iota(jnp.int32, (c, GLA_HEADS * c), 1) % c
    mask_a = (aj >= ai) if reverse else (aj <= ai)

    vt = vt_ref[...].astype(BF16)
    order = range(n_chunks - 1, -1, -1) if reverse else range(n_chunks)
    for ch in order:
        rows = slice(ch * c, (ch + 1) * c)
        q = qk_ref[rows, 0:nqk]
        k = qk_ref[rows, nqk:2 * nqk]
        v = v_ref[rows, :]
        bcum = bcum_all[rows, :]
        btot = bcum[0:1, :] if reverse else bcum[c - 1:c, :]
        q_in = (q * jnp.exp(bcum)).astype(BF16)
        k_in = k * jnp.exp(-bcum)
        k_out = (k * jnp.exp(btot - bcum)).astype(BF16)
        k_bd = jnp.where(mask_k, jnp.concatenate([k_in] * GLA_HEADS, axis=0), 0.0).astype(BF16)
        a = jnp.where(mask_a, _dot_nt(q_in, k_bd), 0.0).astype(BF16)
        v_bd = jnp.where(mask_v, jnp.concatenate([v] * GLA_HEADS, axis=0), 0.0).astype(BF16)
        st = st_ref[...]
        o = _dot(a, v_bd) + _dot_nt(q_in, st.astype(BF16))
        k_pad = jnp.concatenate([jnp.zeros((n * c, nqk), BF16) for n in (ch,) if n] + [k_out]
                                + [jnp.zeros((n * c, nqk), BF16) for n in (n_chunks - 1 - ch,) if n], axis=0)
        u_t = _dot(vt, k_pad)
        st_ref[...] = st * jnp.exp(btot) + jnp.where(mask_s, u_t, 0.0)
        if reverse:
            o = o + of_ref[rows, :]
            hi2, lo2, _ = _split3(o * o)
            hr = lax.broadcasted_iota(jnp.int32, (nv, nv), 0) // GLA_DV
            hc = lax.broadcasted_iota(jnp.int32, (nv, nv), 1) // GLA_DV
            seg = jnp.where(hr == hc, 1.0, 0.0).astype(BF16)
            ms = (_dot(hi2, seg) + _dot(lo2, seg)) * (1.0 / GLA_DV)
            y = o * lax.rsqrt(ms + RMS_EPS) * nw_ref[...]
            o_ref[rows, :] = y * _silu(g_ref[rows, :])
        else:
            o_ref[rows, :] = o


def _gla(gqk, gv, gvt, glg, gg, o_f, norm_w4, *, reverse, batch, seq, ctx):
    r = gqk.shape[0]
    blk = GLA_BLOCK
    nc, nl = ctx // blk, seq // blk
    ctx_base = batch * seq // blk

    def blk_of(b, s):
        if reverse:
            return jnp.where(s < nc, ctx_base + b * nc + (nc - 1 - s), b * nl + (nl - 1 - (s - nc)))
        return jnp.where(s < nc, ctx_base + b * nc + s, b * nl + (s - nc))

    spec = pl.BlockSpec((blk, 256), lambda b, s: (blk_of(b, s), 0))
    spec_t = pl.BlockSpec((256, blk), lambda b, s: (0, blk_of(b, s)))
    return pl.pallas_call(
        functools.partial(_gla_body, reverse=reverse),
        grid=(batch, nc + nl),
        in_specs=[spec, spec, spec_t, spec, spec, spec, _resident((1, 256))],
        out_specs=spec,
        out_shape=jax.ShapeDtypeStruct((r, 256), F32),
        scratch_shapes=[pltpu.VMEM((GLA_HEADS * GLA_DV, GLA_HEADS * GLA_DK), F32)],
        compiler_params=_params("arbitrary", "arbitrary"),
        name="gla_bwd" if reverse else "gla_fwd",
    )(gqk, gv, gvt, glg, gg, o_f, norm_w4)


def _na_bias_tables(rpb, rows):
    nq = NA_QROWS
    nkr = 4 * NA_KBLK // GRID_W
    n_blocks = rows // nq
    kb_total = rows * GRID_W // NA_KBLK
    tabs = []
    for j in (0, 1, n_blocks - 1):
        kr0 = min(max(2 * j - 1, 0), kb_total - 4) * (NA_KBLK // GRID_W)
        r = j * nq + jnp.arange(nq)
        rk = kr0 + jnp.arange(nkr)
        start = jnp.clip(r - NA_ROWS // 2, 0, rows - NA_ROWS)
        row_ok = (rk[None, :] >= start[:, None]) & (rk[None, :] < start[:, None] + NA_ROWS)
        row_off = jnp.clip(rk[None, :] - r[:, None] + (NA_ROWS - 1), 0, 2 * NA_ROWS - 2)
        col = jnp.arange(GRID_W)
        cstart = jnp.clip(col - NA_COLS // 2, 0, GRID_W - NA_COLS)
        col_ok = (col[None, :] >= cstart[:, None]) & (col[None, :] < cstart[:, None] + NA_COLS)
        col_off = jnp.clip(col[None, :] - col[:, None] + (NA_COLS - 1), 0, 2 * NA_COLS - 2)
        b = rpb[:, row_off][..., col_off]
        ok = row_ok[:, :, None, None] & col_ok[None, None, :, :]
        b = jnp.where(ok[None], b, NEG_INF)
        b = jnp.transpose(b, (0, 1, 3, 2, 4)).reshape(NA_HEADS, nq * GRID_W, nkr * GRID_W)
        tabs.append(b)
    return jnp.stack(tabs).astype(F32)


def _na_body(q_ref, k0, k1, k2, k3, v0, v1, v2, v3, kc_ref, vc_ref, bias_ref, o_ref):
    q = q_ref[...]
    kw = jnp.concatenate([k0[...], k1[...], k2[...], k3[...]], axis=0)
    vw = jnp.concatenate([v0[...], v1[...], v2[...], v3[...]], axis=0)
    kc = kc_ref[...]
    vc = vc_ref[...]
    lane = lax.broadcasted_iota(jnp.int32, (1, NA_HEADS * NA_DIM), 1) // NA_DIM
    out = jnp.zeros(o_ref.shape, F32)
    for hh in range(NA_HEADS):
        mh = lane == hh
        qh = jnp.where(mh, q, jnp.zeros_like(q))
        s_w = _dot_nt(qh, kw) + bias_ref[hh]
        s_c = _dot_nt(qh, kc)
        m = jnp.maximum(jnp.max(s_w, axis=-1, keepdims=True), jnp.max(s_c, axis=-1, keepdims=True))
        p_w = jnp.exp(s_w - m)
        p_c = jnp.exp(s_c - m)
        l = jnp.sum(p_w, axis=-1, keepdims=True) + jnp.sum(p_c, axis=-1, keepdims=True)
        o = _dot(p_w.astype(BF16), vw) + _dot(p_c.astype(BF16), vc)
        out = out + jnp.where(mh, o / l, 0.0)
    o_ref[...] = out


def _na(nq, nk, nv, bias, *, batch, seq, ctx):
    r = nq.shape[0]
    tq = NA_QROWS * GRID_W
    nj = seq // tq
    kb = seq // NA_KBLK
    assert NA_KBLK % ctx == 0 or ctx % NA_KBLK == 0
    ctx_blk0 = batch * seq // ctx

    def kspec(i):
        return pl.BlockSpec((NA_KBLK, 256),
                            lambda b, j: (b * kb + jnp.clip(2 * j - 1, 0, kb - 4) + i, 0))

    cspec = pl.BlockSpec((ctx, 256), lambda b, j: (ctx_blk0 + b, 0))
    variant = lambda j: jnp.where(j == 0, 0, jnp.where(j == nj - 1, 2, 1))
    return pl.pallas_call(
        _na_body,
        grid=(batch, nj),
        in_specs=[pl.BlockSpec((tq, 256), lambda b, j: (b * nj + j, 0))]
                 + [kspec(i) for i in range(4)] + [kspec(i) for i in range(4)]
                 + [cspec, cspec,
                    pl.BlockSpec((None, NA_HEADS, tq, 4 * NA_KBLK), lambda b, j: (variant(j), 0, 0, 0))],
        out_specs=pl.BlockSpec((tq, 256), lambda b, j: (b * nj + j, 0)),
        out_shape=jax.ShapeDtypeStruct((r, 256), F32),
        compiler_params=_params("parallel", "arbitrary"),
        name="na",
    )(nq, nk, nk, nk, nk, nv, nv, nv, nv, nk, nv, bias)


def _diff_lambda(lq_ref, lambda_init):
    lq = lq_ref[...]
    s1 = jnp.sum(lq[0:1, :] * lq[1:2, :], axis=-1, keepdims=True)
    s2 = jnp.sum(lq[2:3, :] * lq[3:4, :], axis=-1, keepdims=True)
    return jnp.exp(s1) - jnp.exp(s2) + lambda_init


def _stack_qt(qt):
    row = lax.broadcasted_iota(jnp.int32, (2 * DIFF_DK, 1), 0)
    zero = jnp.zeros_like(qt)
    return jnp.concatenate([jnp.where(row < DIFF_DK, qt, zero), jnp.where(row >= DIFF_DK, qt, zero)], axis=1)


def _diff_finish_t(acc, tq, lam, nw_col, lambda_init):
    o = acc[0:DIFF_DV, :] / acc[DIFF_DV:DIFF_DV + 1, :]
    od = o[:, :tq] - lam * o[:, tq:]
    ms = jnp.mean(od * od, axis=0, keepdims=True)
    y = od * lax.rsqrt(ms + RMS_EPS) * nw_col * (1.0 - lambda_init)
    y = jnp.concatenate([y, jnp.zeros_like(y)], axis=0)
    return jnp.transpose(y)[:, 0:DIFF_DV]


def _diff_body(qt_ref, kc_ref, vtc_ref, k_ref, vt_ref, lq_ref, nw_ref, o_ref, s_ref, *, tk, unroll, lambda_init):
    tq = qt_ref.shape[1]
    n_chunks = k_ref.shape[0] // tk
    ring = s_ref.shape[0]
    qs = _stack_qt(qt_ref[...])

    def chunk(j):
        return pl.ds(j * tk if isinstance(j, int) else pl.multiple_of(j * tk, tk), tk)

    def issue_scores(j, slot):
        s = _dot(k_ref[chunk(j), :], qs)
        s_ref[slot] = s
        return jnp.max(s, axis=0, keepdims=True)

    s_ctx = _dot(kc_ref[...], qs)
    m0 = jnp.max(s_ctx, axis=0, keepdims=True)
    cmax0 = tuple(issue_scores(min(a, n_chunks - 1), a % ring) for a in range(DIFF_AHEAD))
    acc0 = _dot(vtc_ref[...], jnp.exp2(s_ctx - m0).astype(BF16))

    def group(g, carry, tail=False):
        cmax, m, acc = carry
        for u in range(unroll):
            j = g * unroll + u
            if tail and j + DIFF_AHEAD >= n_chunks:
                c_new = cmax[0]
            else:
                c_new = issue_scores(j + DIFF_AHEAD, (u + DIFF_AHEAD) % ring)
            m_new = jnp.maximum(m, cmax[0])
            p = jnp.exp2(s_ref[u % ring] - m_new).astype(BF16)
            vt = vt_ref[:, chunk(j)]
            acc = jnp.exp2(m - m_new) * acc + _dot(vt, p)
            m, cmax = m_new, cmax[1:] + (c_new,)
        return cmax, m, acc

    n_groups = n_chunks // unroll
    carry = lax.fori_loop(0, n_groups - 1, group, (cmax0, m0, acc0))
    _, _, acc = group(n_groups - 1, carry, tail=True)
    o_ref[...] = _diff_finish_t(acc, tq, _diff_lambda(lq_ref, lambda_init), nw_ref[...], lambda_init)


def _diff(dqt, dk, dvt, lq, norm_w, *, batch, seq, ctx, lambda_init):
    h, r, _ = dk.shape
    tq = DIFF_TQ
    nq = seq // tq
    tk = min(DIFF_TK, seq)
    unroll = min(DIFF_UNROLL, seq // tk)
    assert (seq // tk) % unroll == 0 and unroll % DIFF_RING == 0 and DIFF_RING > DIFF_AHEAD
    ctx_blk0 = batch * seq // ctx
    return pl.pallas_call(
        functools.partial(_diff_body, tk=tk, unroll=unroll, lambda_init=lambda_init),
        grid=(batch, h, nq),
        in_specs=[pl.BlockSpec((None, 64, tq), lambda b, hh, i: (hh, 0, b * nq + i)),
                  pl.BlockSpec((None, ctx, 64), lambda b, hh, i: (hh, ctx_blk0 + b, 0)),
                  pl.BlockSpec((None, 128, ctx), lambda b, hh, i: (hh, 0, ctx_blk0 + b)),
                  pl.BlockSpec((None, seq, 64), lambda b, hh, i: (hh, b, 0)),
                  pl.BlockSpec((None, 128, seq), lambda b, hh, i: (hh, 0, b)),
                  _resident(lq.shape),
                  _resident((DIFF_DV, 1))],
        out_specs=pl.BlockSpec((None, tq, 64), lambda b, hh, i: (hh, b * nq + i, 0)),
        out_shape=jax.ShapeDtypeStruct((h, r, 64), F32),
        scratch_shapes=[pltpu.VMEM((DIFF_RING, tk, 2 * tq), F32)],
        compiler_params=_params("parallel", "parallel", "arbitrary"),
        name="diff",
    )(dqt, dk, dvt, dk, dvt, lq, norm_w.reshape(DIFF_DV, 1))


def _ctx_attn_body(nq_ref, nk_ref, nv_ref, dqt_ref, dk_ref, dvt_ref, lq_ref, nw_ref, na_in, df_in,
                   na_out, df_out, *, lambda_init):
    del na_in, df_in
    q = nq_ref[...]
    k = nk_ref[...]
    v = nv_ref[...]
    lane = lax.broadcasted_iota(jnp.int32, (1, NA_HEADS * NA_DIM), 1) // NA_DIM
    out = jnp.zeros(na_out.shape, F32)
    for hh in range(NA_HEADS):
        mh = lane == hh
        s = _dot_nt(jnp.where(mh, q, jnp.zeros_like(q)), k)
        p = jnp.exp(s - jnp.max(s, axis=-1, keepdims=True))
        o = _dot(p.astype(BF16), v) / jnp.sum(p, axis=-1, keepdims=True)
        out = out + jnp.where(mh, o, 0.0)
    na_out[...] = out

    lam = _diff_lambda(lq_ref, lambda_init)
    tq = dqt_ref.shape[2]
    for hh in range(DIFF_HEADS):
        s = _dot(dk_ref[hh], _stack_qt(dqt_ref[hh]))
        p = jnp.exp2(s - jnp.max(s, axis=0, keepdims=True))
        acc = _dot(dvt_ref[hh], p.astype(BF16))
        df_out[hh] = _diff_finish_t(acc, tq, lam, nw_ref[...], lambda_init)


def _ctx_attn(nq, nk, nv, dqt, dk, dvt, lq, norm_w, na_o, df_o, *, batch, seq, ctx, lambda_init):
    blk0 = batch * seq // ctx
    s256 = pl.BlockSpec((ctx, 256), lambda b: (blk0 + b, 0))
    sh64 = pl.BlockSpec((DIFF_HEADS, ctx, 64), lambda b: (0, blk0 + b, 0))
    sq_t = pl.BlockSpec((DIFF_HEADS, 64, ctx), lambda b: (0, 0, blk0 + b))
    sv_t = pl.BlockSpec((DIFF_HEADS, 128, ctx), lambda b: (0, 0, blk0 + b))
    return pl.pallas_call(
        functools.partial(_ctx_attn_body, lambda_init=lambda_init),
        grid=(batch,),
        in_specs=[s256, s256, s256, sq_t, sh64, sv_t, _resident(lq.shape), _resident((DIFF_DV, 1)),
                  pl.BlockSpec(memory_space=pl.ANY), pl.BlockSpec(memory_space=pl.ANY)],
        out_specs=[s256, sh64],
        out_shape=[jax.ShapeDtypeStruct(na_o.shape, F32), jax.ShapeDtypeStruct(df_o.shape, F32)],
        input_output_aliases={8: 0, 9: 1},
        compiler_params=_params("arbitrary"),
        name="ctx_attn",
    )(nq, nk, nv, dqt, dk, dvt, lq, norm_w.reshape(DIFF_DV, 1), na_o, df_o)


def _conv_body(u_ref, prev_ref, next_ref, dw_ref, dwb_ref, lng_ref, lnb_ref, pw_ref, pwb_ref,
               o_ref, pad_ref, *, tiles_per_seq, n_latent_tiles):
    t = u_ref.shape[0]
    i = pl.program_id(0)
    in_ctx = i >= n_latent_tiles
    first = in_ctx | (i % tiles_per_seq == 0)
    last = in_ctx | (i % tiles_per_seq == tiles_per_seq - 1)
    pad_ref[0:HALO, :] = jnp.where(first, 0.0, prev_ref[...])
    pad_ref[HALO:HALO + t, :] = u_ref[...]
    pad_ref[HALO + t:HALO + t + HALO, :] = jnp.where(last, 0.0, next_ref[...])
    acc = jnp.zeros((t, CONV_CH), F32)
    base = HALO - CONV_K // 2
    for k in range(CONV_K):
        acc = acc + dw_ref[k:k + 1, :] * pad_ref[base + k:base + k + t, :]
    y = acc + dwb_ref[...]
    mu = jnp.mean(y, axis=-1, keepdims=True)
    yc = y - mu
    var = jnp.mean(yc * yc, axis=-1, keepdims=True)
    y = _silu(yc * lax.rsqrt(var + LN_EPS) * lng_ref[...] + lnb_ref[...])
    o_ref[...] = _dot(y.astype(BF16), pw_ref[...]) + pwb_ref[...]


def _conv(cu, dw, dw_b, ln_g, ln_b, pw, pw_b, *, batch, seq, ctx):
    r = cu.shape[0]
    t = CONV_TILE
    assert ctx == t and seq % t == 0
    n_tiles = r // t
    hb = t // HALO
    vec = lambda a: a.reshape(1, CONV_CH)
    return pl.pallas_call(
        functools.partial(_conv_body, tiles_per_seq=seq // t, n_latent_tiles=batch * seq // t),
        grid=(n_tiles,),
        in_specs=[pl.BlockSpec((t, CONV_CH), lambda i: (i, 0)),
                  pl.BlockSpec((HALO, CONV_CH), lambda i: (jnp.maximum(i * hb - 1, 0), 0)),
                  pl.BlockSpec((HALO, CONV_CH), lambda i: (jnp.minimum((i + 1) * hb, n_tiles * hb - 1), 0)),
                  _resident((CONV_K, CONV_CH))] + [_resident((1, CONV_CH))] * 3
                 + [_resident((CONV_CH, CONV_CH)), _resident((1, CONV_CH))],
        out_specs=pl.BlockSpec((t, CONV_CH), lambda i: (i, 0)),
        out_shape=jax.ShapeDtypeStruct((r, CONV_CH), F32),
        scratch_shapes=[pltpu.VMEM((t + 2 * HALO, CONV_CH), F32)],
        compiler_params=_params("parallel"),
        name="conv",
    )(cu, cu, cu, dw, vec(dw_b), vec(ln_g), vec(ln_b), pw.astype(BF16), vec(pw_b))


def _wout_body(h_ref, mod_ref, gx_ref, nx_ref, dx_ref, cx_ref, w_ref, o_ref):
    mix = jnp.concatenate([gx_ref[...], nx_ref[...]] + [dx_ref[hh] for hh in range(DIFF_HEADS)]
                          + [cx_ref[...]], axis=-1).astype(BF16)
    o_ref[...] = h_ref[...] + mod_ref[5:6, :] * _dot(mix, w_ref[...])


def _wout(h, mods, gx, nx, dx, cx, w_out, *, n_tiles, group_of):
    r, d = h.shape
    tm = ROW_TILE
    row = lambda i: (i, 0)
    return pl.pallas_call(
        _wout_body,
        grid=(n_tiles,),
        in_specs=[pl.BlockSpec((tm, d), row),
                  pl.BlockSpec((None, N_MOD, d), lambda i: (group_of(i), 0, 0)),
                  pl.BlockSpec((tm, 256), row), pl.BlockSpec((tm, 256), row),
                  pl.BlockSpec((DIFF_HEADS, tm, 64), lambda i: (0, i, 0)),
                  pl.BlockSpec((tm, 256), row),
                  _resident(w_out.shape)],
        out_specs=pl.BlockSpec((tm, d), row),
        out_shape=jax.ShapeDtypeStruct((n_tiles * tm, d), F32),
        compiler_params=_params("parallel"),
        name="wout",
    )(h, mods, gx, nx, dx, cx, w_out.astype(BF16))


def kernel(x, c, ctx, c_ctx, ada_w, ada_b, norm_ffn1, ffn1_w13, ffn1_w2, norm_mix, w_in, gla_wa_f, gla_ba_f, gla_wa_b, gla_ba_b, gla_norm, na_rpb, diff_lq1, diff_lk1, diff_lq2, diff_lk2, diff_norm, conv_dw, conv_dw_b, conv_ln_g, conv_ln_b, conv_pw, conv_pw_b, w_out, norm_ffn2, ffn2_w13, ffn2_w2, final_norm):
    batch, seq, d = x.shape
    n_ctx = ctx.shape[1]
    depth = ada_w.shape[0]
    rows = seq // GRID_W
    tm = ROW_TILE
    assert seq % tm == 0 and (batch * n_ctx) % tm == 0 and batch + 1 <= 8
    lat_tiles = batch * seq // tm
    all_tiles = lat_tiles + batch * n_ctx // tm
    tiles_per_batch = seq // tm

    def group_of(i):
        return jnp.minimum(i // tiles_per_batch, batch)

    def pos_of(i):
        return jnp.where(i < lat_tiles, i % tiles_per_batch, tiles_per_batch)

    c_rows = jnp.concatenate([c, c_ctx[None, :], jnp.zeros((8 - batch - 1, d), F32)], axis=0)
    mods_all = _ada(c_rows, ada_w, ada_b)[:, :batch + 1].reshape(depth, batch + 1, N_MOD, d)
    rope = _rope_tables(seq, tm)
    h = jnp.concatenate([x.reshape(batch * seq, d), ctx.reshape(batch * n_ctx, d)], axis=0)

    for i in range(depth):
        last = i == depth - 1
        lambda_init = 0.8 - 0.6 * math.exp(-0.3 * i)
        mods = mods_all[i]
        tok = dict(n_tiles=all_tiles, group_of=group_of)
        geo = dict(batch=batch, seq=seq, ctx=n_ctx)

        w13c, w2c = _chunk_ffn_weights(ffn1_w13[i], ffn1_w2[i])
        h = _ffn(h, mods, norm_ffn1[i], w13c, w2c, final_norm, k0=0, final=False, **tok)

        wa, ba = _gate_weights(gla_wa_f[i], gla_ba_f[i], gla_wa_b[i], gla_ba_b[i])
        (gqk, gv, gvt, gg, glg, nq, nk, nv, dq, dk, dv, cu) = _proj(
            h, mods, norm_mix[i], _permute_w_in(w_in[i]), wa, ba, rope, pos_of=pos_of, **tok)

        gnorm = jnp.tile(gla_norm[i], GLA_HEADS).reshape(1, GLA_HEADS * GLA_DV)
        o_f = _gla(gqk, gv, gvt, glg, gg, gg, gnorm, reverse=False, **geo)
        gx = _gla(gqk, gv, gvt, glg, gg, o_f, gnorm, reverse=True, **geo)

        nx = _na(nq, nk, nv, _na_bias_tables(na_rpb[i], rows), **geo)

        lq = jnp.stack([diff_lq1[i], diff_lk1[i], diff_lq2[i], diff_lk2[i]])
        dx = _diff(dq, dk, dv, lq, diff_norm[i], lambda_init=lambda_init, **geo)
        if not last:
            nx, dx = _ctx_attn(nq, nk, nv, dq, dk, dv, lq, diff_norm[i], nx, dx,
                               lambda_init=lambda_init, **geo)

        cx = _conv(cu, conv_dw[i], conv_dw_b[i], conv_ln_g[i], conv_ln_b[i], conv_pw[i], conv_pw_b[i], **geo)

        if last:
            tok = dict(n_tiles=lat_tiles, group_of=group_of)
        h = _wout(h, mods, gx, nx, dx, cx, w_out[i], **tok)

        w13c, w2c = _chunk_ffn_weights(ffn2_w13[i], ffn2_w2[i])
        h = _ffn(h, mods, norm_ffn2[i], w13c, w2c, final_norm, k0=6, final=last, **tok)

    return h[:batch * seq].reshape(batch, seq, d)
```

```python
import functools
import math

import jax
import jax.numpy as jnp
from jax import lax
from jax.experimental import pallas as pl
from jax.experimental.pallas import tpu as pltpu

F32 = jnp.float32
BF16 = jnp.bfloat16

GRID_W = 64
N_MOD = 9
RMS_EPS = 1e-6
LN_EPS = 1e-5
NEG_INF = -1e30
GLA_HEADS, GLA_DK, GLA_DV, GLA_RANK, GLA_TAU, GLA_CHUNK = 4, 32, 64, 16, 16.0, 64
NA_HEADS, NA_DIM, NA_ROWS, NA_COLS = 4, 64, 8, 16
DIFF_HEADS, DIFF_DK, DIFF_DV = 4, 32, 64
DIFF_VROWS = 80
CONV_CH, CONV_K = 256, 31
ROPE_BASE = 10000.0
LOG2E = 1.4426950408889634

LANES = 128
VMEM_LIMIT = 56 * 1024 * 1024

ROW_TILE = 512
FF_CHUNK = 256
GLA_BLOCK = 256
NA_QROWS = 8
NA_KBLK = 256
CONV_TILE = 256
DIFF_TQ = 256
DIFF_TK = 512
DIFF_UNROLL = 8
DIFF_AHEAD = 3
DIFF_RING = 4
HALO = 16

C_GLA, C_NA, C_DIFF, C_CONV, C_AUX, C_END = 0, 768, 1536, 2304, 2816, 2944


def _dot(a, b):
    return jnp.dot(a, b, preferred_element_type=F32)


def _dot_nt(a, b):
    return lax.dot_general(a, b, (((1,), (1,)), ((), ())), preferred_element_type=F32)


def _params(*sem):
    return pltpu.CompilerParams(dimension_semantics=sem, vmem_limit_bytes=VMEM_LIMIT)


def _resident(shape):
    nd = len(shape)
    return pl.BlockSpec(shape, lambda *_: (0,) * nd, pipeline_mode=pl.Buffered(1))


def _silu(x):
    return x * jax.nn.sigmoid(x)


def _rms(x, w):
    return x * lax.rsqrt(jnp.mean(x * x, axis=-1, keepdims=True) + RMS_EPS) * w


def _ada_body(c_ref, w_ref, b_ref, o_ref):
    s = _silu(c_ref[...])
    o_ref[...] = jnp.dot(s, w_ref[...], precision=lax.Precision.HIGHEST,
                         preferred_element_type=F32) + b_ref[...]


def _ada(c_rows, ada_w, ada_b):
    depth, d, _ = ada_w.shape
    return pl.pallas_call(
        _ada_body,
        grid=(depth, N_MOD),
        in_specs=[pl.BlockSpec((8, d), lambda l, n: (0, 0)),
                  pl.BlockSpec((None, d, d), lambda l, n: (l, 0, n)),
                  pl.BlockSpec((None, 1, d), lambda l, n: (l, 0, n))],
        out_specs=pl.BlockSpec((None, 8, d), lambda l, n: (l, 0, n)),
        out_shape=jax.ShapeDtypeStruct((depth, 8, N_MOD * d), F32),
        compiler_params=_params("arbitrary", "arbitrary"),
        name="ada",
    )(c_rows, ada_w, ada_b.reshape(depth, 1, N_MOD * d))


def _ffn_body(h_ref, hc_ref, mod_ref, nw_ref, w13_ref, w2_ref, fw_ref, o_ref, xb_ref, acc_ref, h13_ref,
              *, k0, final, n_first):
    n_chunks, tf, _ = w2_ref.shape
    ff = n_chunks * tf
    x = h_ref[...]
    if n_first is not None:
        x = jnp.where(pl.program_id(0) < n_first, x, hc_ref[...])
    xm = _rms(x, nw_ref[...]) * (1.0 + mod_ref[k0 + 1:k0 + 2, :]) + mod_ref[k0:k0 + 1, :]
    xb_ref[...] = xm.astype(BF16)
    acc_ref[...] = jnp.zeros_like(acc_ref)

    def up(c, slot):
        col = c * tf if isinstance(c, int) else pl.multiple_of(c * tf, tf)
        h13_ref[slot, :, :tf] = _dot(xb_ref[...], w13_ref[:, pl.ds(col, tf)])
        h13_ref[slot, :, tf:] = _dot(xb_ref[...], w13_ref[:, pl.ds(ff + col, tf)])

    def down(c, slot):
        a = h13_ref[slot, :, :tf]
        u = h13_ref[slot, :, tf:]
        acc_ref[...] += _dot((_silu(a) * u).astype(BF16), w2_ref[c])

    def pair(t, carry):
        up(2 * t + 1, 1)
        down(2 * t, 0)
        up(2 * t + 2, 0)
        down(2 * t + 1, 1)
        return carry

    up(0, 0)
    lax.fori_loop(0, (n_chunks - 1) // 2, pair, 0)
    down(n_chunks - 1, 0)
    out = x + (0.5 * mod_ref[k0 + 2:k0 + 3, :]) * acc_ref[...]
    if final:
        out = _rms(out, fw_ref[...])
    o_ref[...] = out


def _ffn(h, h_ctx, mods, norm_w, w13, w2, final_w, *, k0, n_tiles, group_of, final):
    d = h.shape[1]
    tm = ROW_TILE
    ff = w2.shape[0]
    n_chunks = ff // FF_CHUNK
    assert ff % FF_CHUNK == 0 and n_chunks % 2 == 1
    if h_ctx is None:
        n_first, h_ctx = None, norm_w.reshape(1, d)
        row0 = lambda i: (i, 0)
        ctx_spec = _resident((1, d))
    else:
        n_first = h.shape[0] // tm
        row0 = lambda i: (jnp.minimum(i, n_first - 1), 0)
        ctx_spec = pl.BlockSpec((tm, d), lambda i: (jnp.maximum(i - n_first, 0), 0))
    return pl.pallas_call(
        functools.partial(_ffn_body, k0=k0, final=final, n_first=n_first),
        grid=(n_tiles,),
        in_specs=[pl.BlockSpec((tm, d), row0),
                  ctx_spec,
                  pl.BlockSpec((None, N_MOD, d), lambda i: (group_of(i), 0, 0)),
                  _resident((1, d)),
                  _resident(w13.shape),
                  _resident((n_chunks, FF_CHUNK, d)),
                  _resident((1, d))],
        out_specs=pl.BlockSpec((tm, d), lambda i: (i, 0)),
        out_shape=jax.ShapeDtypeStruct((n_tiles * tm, d), F32),
        scratch_shapes=[pltpu.VMEM((tm, d), BF16), pltpu.VMEM((tm, d), F32),
                        pltpu.VMEM((2, tm, 2 * FF_CHUNK), F32)],
        compiler_params=_params("parallel"),
        name="ffn",
    )(h, h_ctx, mods, norm_w.reshape(1, d), w13.astype(BF16),
      w2.astype(BF16).reshape(n_chunks, FF_CHUNK, d), final_w.reshape(1, d))


def _log_sigmoid(x):
    return jnp.minimum(x, 0.0) - jnp.log1p(jnp.exp(-jnp.abs(x)))


def _rope_rotate(x):
    n = x.shape[-1]
    lane = lax.broadcasted_iota(jnp.int32, x.shape, 1)
    up = pltpu.roll(x, n - 8, 1)
    dn = pltpu.roll(x, 8, 1)
    return jnp.where((lane & 15) < 8, -up, dn)


def _rope_rotate_rows(x):
    n = x.shape[0]
    row = lax.broadcasted_iota(jnp.int32, x.shape, 0)
    up = pltpu.roll(x, n - 8, 0)
    dn = pltpu.roll(x, 8, 0)
    return jnp.where((row & 15) < 8, -up, dn)


def _proj_body(h_ref, mod_ref, nw_ref, w_ref, wvt_ref, wdqt_ref, wdvt_ref, wa_ref, ba_ref,
               cos_ref, sin_ref, cost_ref, sint_ref,
               gqk_ref, gv_ref, gvt_ref, gg_ref, glg_ref, nq_ref, nk_ref, nv_ref,
               dqt_ref, dk_ref, dvt_ref, cu_ref, xb_ref):
    x = h_ref[...]
    tm = x.shape[0]
    xm = _rms(x, nw_ref[...]) * (1.0 + mod_ref[4:5, :]) + mod_ref[3:4, :]
    xb_ref[...] = xm.astype(BF16)

    z = _dot(xb_ref[...], w_ref[:, C_GLA:C_GLA + 256])
    lane = lax.broadcasted_iota(jnp.int32, (1, 256), 1)
    gqk_ref[...] = z * jnp.where(lane < 128, GLA_DK ** -0.5, 1.0)
    gv_ref[...] = _dot(xb_ref[...], w_ref[:, C_GLA + 256:C_GLA + 512])
    gvt_ref[...] = _dot_nt(wvt_ref[...], xb_ref[...])
    gg_ref[...] = _dot(xb_ref[...], w_ref[:, C_GLA + 512:C_GLA + 768])
    aux = _dot(xb_ref[...], w_ref[:, C_AUX:C_END])
    pre = _dot(aux.astype(BF16), wa_ref[...]) + ba_ref[...]
    glg_ref[...] = _log_sigmoid(pre) * (1.0 / GLA_TAU)

    nq_ref[...] = (_dot(xb_ref[...], w_ref[:, C_NA:C_NA + 256]) * (NA_DIM ** -0.5)).astype(BF16)
    nk_ref[...] = _dot(xb_ref[...], w_ref[:, C_NA + 256:C_NA + 512]).astype(BF16)
    nv_ref[...] = _dot(xb_ref[...], w_ref[:, C_NA + 512:C_NA + 768]).astype(BF16)

    cos = cos_ref[...]
    sin = sin_ref[...]
    cos2 = jnp.concatenate([cos, cos], axis=1)
    sin2 = jnp.concatenate([sin, sin], axis=1)
    zk = _dot(xb_ref[...], w_ref[:, C_DIFF + 256:C_DIFF + 512])
    zk = zk * cos2 + _rope_rotate(zk) * sin2
    n_rep = 2 * DIFF_HEADS
    cos_t = jnp.concatenate([cost_ref[...]] * n_rep, axis=0)
    sin_t = jnp.concatenate([sint_ref[...]] * n_rep, axis=0)
    zqt = _dot_nt(wdqt_ref[...], xb_ref[...])
    zqt = (zqt * cos_t + _rope_rotate_rows(zqt) * sin_t) * (DIFF_DK ** -0.5 * LOG2E)
    zvt = _dot_nt(wdvt_ref[...], xb_ref[...])
    pad_rows = DIFF_VROWS - DIFF_DV
    one_row = jnp.where(lax.broadcasted_iota(jnp.int32, (pad_rows, tm), 0) == 0, 1.0, 0.0).astype(BF16)
    for hh in range(DIFF_HEADS):
        sl = slice(64 * hh, 64 * hh + 64)
        dk_ref[hh] = zk[:, sl].astype(BF16)
        dqt_ref[hh] = zqt[sl, :].astype(BF16)
        dvt_ref[hh, 0:64, :] = zvt[sl, :].astype(BF16)
        dvt_ref[hh, DIFF_DV:DIFF_VROWS, :] = one_row

    za = _dot(xb_ref[...], w_ref[:, C_CONV:C_CONV + 256])
    zg = _dot(xb_ref[...], w_ref[:, C_CONV + 256:C_CONV + 512])
    cu_ref[...] = za * jax.nn.sigmoid(zg)


def _proj(h, mods, norm_w, w_p, wa, ba, rope, *, n_tiles, group_of, pos_of):
    r, d = h.shape
    tm = ROW_TILE
    row = lambda i: (i, 0)
    hrow = lambda i: (0, i, 0)
    hcol = lambda i: (0, 0, i)
    f32_256 = jax.ShapeDtypeStruct((r, 256), F32)
    bf_256 = jax.ShapeDtypeStruct((r, 256), BF16)
    wvt = w_p[:, C_GLA + 256:C_GLA + 512].T
    wdqt = w_p[:, C_DIFF:C_DIFF + 256].T
    wdvt = w_p[:, C_DIFF + 512:C_DIFF + 768].T
    cos_r, sin_r, cos_c, sin_c = rope
    out_shape = [f32_256, f32_256, jax.ShapeDtypeStruct((256, r), F32), f32_256, f32_256,
                 bf_256, bf_256, bf_256,
                 jax.ShapeDtypeStruct((DIFF_HEADS, 64, r), BF16),
                 jax.ShapeDtypeStruct((DIFF_HEADS, r, 64), BF16),
                 jax.ShapeDtypeStruct((DIFF_HEADS, DIFF_VROWS, r), BF16),
                 f32_256]
    out_specs = [pl.BlockSpec((tm, 256), row)] * 2 + [pl.BlockSpec((256, tm), lambda i: (0, i))] + [
        pl.BlockSpec((tm, 256), row)] * 5 + [
        pl.BlockSpec((DIFF_HEADS, 64, tm), hcol),
        pl.BlockSpec((DIFF_HEADS, tm, 64), hrow),
        pl.BlockSpec((DIFF_HEADS, DIFF_VROWS, tm), hcol),
        pl.BlockSpec((tm, 256), row)]
    return pl.pallas_call(
        _proj_body,
        grid=(n_tiles,),
        in_specs=[pl.BlockSpec((tm, d), row),
                  pl.BlockSpec((None, N_MOD, d), lambda i: (group_of(i), 0, 0)),
                  _resident((1, d)),
                  _resident(w_p.shape),
                  _resident(wvt.shape),
                  _resident(wdqt.shape),
                  _resident(wdvt.shape),
                  _resident(wa.shape),
                  _resident(ba.shape),
                  pl.BlockSpec((tm, LANES), lambda i: (pos_of(i), 0)),
                  pl.BlockSpec((tm, LANES), lambda i: (pos_of(i), 0)),
                  pl.BlockSpec((DIFF_DK, tm), lambda i: (0, pos_of(i))),
                  pl.BlockSpec((DIFF_DK, tm), lambda i: (0, pos_of(i)))],
        out_specs=out_specs,
        out_shape=out_shape,
        scratch_shapes=[pltpu.VMEM((tm, d), BF16)],
        compiler_params=_params("parallel"),
        name="proj",
    )(h, mods, norm_w.reshape(1, d), w_p, wvt, wdqt, wdvt, wa, ba, cos_r, sin_r, cos_c, sin_c)


def _permute_w_in(w_in):
    d = w_in.shape[0]
    g0 = 2 * GLA_HEADS * GLA_DK + 2 * GLA_HEADS * GLA_DV
    aux = w_in[:, g0:g0 + 2 * GLA_RANK]
    rest = w_in[:, g0 + 2 * GLA_RANK:]
    pad = jnp.zeros((d, C_END - C_AUX - 2 * GLA_RANK), w_in.dtype)
    return jnp.concatenate([w_in[:, :g0], rest, aux, pad], axis=1).astype(BF16)


def _gate_weights(wa_f, ba_f, wa_b, ba_b):
    n = GLA_HEADS * GLA_DK
    wa = jnp.zeros((C_END - C_AUX, 2 * n), F32)
    wa = wa.at[:GLA_RANK, :n].set(wa_f).at[GLA_RANK:2 * GLA_RANK, n:].set(wa_b)
    return wa.astype(BF16), jnp.concatenate([ba_f, ba_b]).reshape(1, 2 * n)


def _rope_tables(seq, tile):
    t = jnp.arange(seq)
    row = (t // GRID_W).astype(F32)
    col = (t % GRID_W).astype(F32)
    half = DIFF_DK // 2
    inv = 1.0 / (ROPE_BASE ** (jnp.arange(0, half, 2, dtype=F32) / half))
    ang_r = row[:, None] * inv
    ang_c = col[:, None] * inv
    ang = jnp.concatenate([ang_r, ang_r, ang_c, ang_c], axis=-1)
    cos = jnp.concatenate([jnp.cos(ang), jnp.ones((tile, DIFF_DK), F32)], axis=0)
    sin = jnp.concatenate([jnp.sin(ang), jnp.zeros((tile, DIFF_DK), F32)], axis=0)
    rep = (1, LANES // DIFF_DK)
    return jnp.tile(cos, rep), jnp.tile(sin, rep), cos.T, sin.T


def _split3(x):
    hi = x.astype(BF16)
    r1 = x - hi.astype(F32)
    mid = r1.astype(BF16)
    lo = (r1 - mid.astype(F32)).astype(BF16)
    return hi, mid, lo


def _gla_body(qk_ref, v_ref, vt_ref, lg_ref, g_ref, of_ref, nw_ref, o_ref, st_ref, *, reverse):
    blk = qk_ref.shape[0]
    n_chunks = blk // GLA_CHUNK
    nqk = GLA_HEADS * GLA_DK
    nv = GLA_HEADS * GLA_DV

    @pl.when(pl.program_id(1) == 0)
    def _():
        st_ref[...] = jnp.zeros_like(st_ref)

    ri = lax.broadcasted_iota(jnp.int32, (blk, blk), 0)
    ci = lax.broadcasted_iota(jnp.int32, (blk, blk), 1)
    same = (ri // GLA_CHUNK) == (ci // GLA_CHUNK)
    tri = jnp.where(same & ((ci >= ri) if reverse else (ci <= ri)), 1.0, 0.0).astype(BF16)
    lg = lg_ref[:, nqk:2 * nqk] if reverse else lg_ref[:, 0:nqk]
    hi, mid, lo = _split3(lg)
    bcum_all = _dot(tri, hi) + _dot(tri, mid) + _dot(tri, lo)

    c = GLA_CHUNK
    rk = lax.broadcasted_iota(jnp.int32, (GLA_HEADS * c, nqk), 0) // c
    ck = lax.broadcasted_iota(jnp.int32, (GLA_HEADS * c, nqk), 1) // GLA_DK
    mask_k = rk == ck
    rv = lax.broadcasted_iota(jnp.int32, (GLA_HEADS * c, nv), 0) // c
    cv = lax.broadcasted_iota(jnp.int32, (GLA_HEADS * c, nv), 1) // GLA_DV
    mask_v = rv == cv
    rs = lax.broadcasted_iota(jnp.int32, (nv, nqk), 0) // GLA_DV
    cs = lax.broadcasted_iota(jnp.int32, (nv, nqk), 1) // GLA_DK
    mask_s = rs == cs
    ai = lax.broadcasted_iota(jnp.int32, (c, GLA_HEADS * c), 0)
    aj = lax.broadcasted_iota(jnp.int32, (c, GLA_HEADS * c), 1) % c
    mask_a = (aj >= ai) if reverse else (aj <= ai)

    vt = vt_ref[...].astype(BF16)
    order = range(n_chunks - 1, -1, -1) if reverse else range(n_chunks)
    for ch in order:
        rows = slice(ch * c, (ch + 1) * c)
        q = qk_ref[rows, 0:nqk]
        k = qk_ref[rows, nqk:2 * nqk]
        v = v_ref[rows, :]
        bcum = bcum_all[rows, :]
        btot = bcum[0:1, :] if reverse else bcum[c - 1:c, :]
        q_in = (q * jnp.exp(bcum)).astype(BF16)
        k_in = k * jnp.exp(-bcum)
        k_out = (k * jnp.exp(btot - bcum)).astype(BF16)
        k_bd = jnp.where(mask_k, jnp.concatenate([k_in] * GLA_HEADS, axis=0), 0.0).astype(BF16)
        a = jnp.where(mask_a, _dot_nt(q_in, k_bd), 0.0).astype(BF16)
        v_bd = jnp.where(mask_v, jnp.concatenate([v] * GLA_HEADS, axis=0), 0.0).astype(BF16)
        st = st_ref[...]
        o = _dot(a, v_bd) + _dot_nt(q_in, st.astype(BF16))
        k_pad = jnp.concatenate([jnp.zeros((n * c, nqk), BF16) for n in (ch,) if n] + [k_out]
                                + [jnp.zeros((n * c, nqk), BF16) for n in (n_chunks - 1 - ch,) if n], axis=0)
        u_t = _dot(vt, k_pad)
        st_ref[...] = st * jnp.exp(btot) + jnp.where(mask_s, u_t, 0.0)
        if reverse:
            o = o + of_ref[rows, :]
            hi2, lo2, _ = _split3(o * o)
            hr = lax.broadcasted_iota(jnp.int32, (nv, nv), 0) // GLA_DV
            hc = lax.broadcasted_iota(jnp.int32, (nv, nv), 1) // GLA_DV
            seg = jnp.where(hr == hc, 1.0, 0.0).astype(BF16)
            ms = (_dot(hi2, seg) + _dot(lo2, seg)) * (1.0 / GLA_DV)
            y = o * lax.rsqrt(ms + RMS_EPS) * nw_ref[...]
            o_ref[rows, :] = y * _silu(g_ref[rows, :])
        else:
            o_ref[rows, :] = o


def _gla(gqk, gv, gvt, glg, gg, o_f, norm_w4, *, reverse, batch, seq, ctx):
    r = gqk.shape[0]
    blk = GLA_BLOCK
    nc, nl = ctx // blk, seq // blk
    ctx_base = batch * seq // blk

    def blk_of(b, s):
        if reverse:
            return jnp.where(s < nc, ctx_base + b * nc + (nc - 1 - s), b * nl + (nl - 1 - (s - nc)))
        return jnp.where(s < nc, ctx_base + b * nc + s, b * nl + (s - nc))

    spec = pl.BlockSpec((blk, 256), lambda b, s: (blk_of(b, s), 0))
    spec_t = pl.BlockSpec((256, blk), lambda b, s: (0, blk_of(b, s)))
    return pl.pallas_call(
        functools.partial(_gla_body, reverse=reverse),
        grid=(batch, nc + nl),
        in_specs=[spec, spec, spec_t, spec, spec, spec, _resident((1, 256))],
        out_specs=spec,
        out_shape=jax.ShapeDtypeStruct((r, 256), F32),
        scratch_shapes=[pltpu.VMEM((GLA_HEADS * GLA_DV, GLA_HEADS * GLA_DK), F32)],
        compiler_params=_params("arbitrary", "arbitrary"),
        name="gla_bwd" if reverse else "gla_fwd",
    )(gqk, gv, gvt, glg, gg, o_f, norm_w4)


def _na_bias_tables(rpb, rows):
    nq = NA_QROWS
    nkr = 4 * NA_KBLK // GRID_W
    n_blocks = rows // nq
    kb_total = rows * GRID_W // NA_KBLK
    tabs = []
    for j in (0, 1, n_blocks - 1):
        kr0 = min(max(2 * j - 1, 0), kb_total - 4) * (NA_KBLK // GRID_W)
        r = j * nq + jnp.arange(nq)
        rk = kr0 + jnp.arange(nkr)
        start = jnp.clip(r - NA_ROWS // 2, 0, rows - NA_ROWS)
        row_ok = (rk[None, :] >= start[:, None]) & (rk[None, :] < start[:, None] + NA_ROWS)
        row_off = jnp.clip(rk[None, :] - r[:, None] + (NA_ROWS - 1), 0, 2 * NA_ROWS - 2)
        col = jnp.arange(GRID_W)
        cstart = jnp.clip(col - NA_COLS // 2, 0, GRID_W - NA_COLS)
        col_ok = (col[None, :] >= cstart[:, None]) & (col[None, :] < cstart[:, None] + NA_COLS)
        col_off = jnp.clip(col[None, :] - col[:, None] + (NA_COLS - 1), 0, 2 * NA_COLS - 2)
        b = rpb[:, row_off][..., col_off]
        ok = row_ok[:, :, None, None] & col_ok[None, None, :, :]
        b = jnp.where(ok[None], b, NEG_INF)
        b = jnp.transpose(b, (0, 1, 3, 2, 4)).reshape(NA_HEADS, nq * GRID_W, nkr * GRID_W)
        tabs.append(b)
    return jnp.stack(tabs).astype(F32)


def _na_body(q_ref, k0, k1, k2, k3, v0, v1, v2, v3, kc_ref, vc_ref, bias_ref, o_ref):
    q = q_ref[...]
    kw = jnp.concatenate([k0[...], k1[...], k2[...], k3[...]], axis=0)
    vw = jnp.concatenate([v0[...], v1[...], v2[...], v3[...]], axis=0)
    kc = kc_ref[...]
    vc = vc_ref[...]
    lane = lax.broadcasted_iota(jnp.int32, (1, NA_HEADS * NA_DIM), 1) // NA_DIM
    out = jnp.zeros(o_ref.shape, F32)
    for hh in range(NA_HEADS):
        mh = lane == hh
        qh = jnp.where(mh, q, jnp.zeros_like(q))
        s_w = _dot_nt(qh, kw) + bias_ref[hh]
        s_c = _dot_nt(qh, kc)
        m = jnp.maximum(jnp.max(s_w, axis=-1, keepdims=True), jnp.max(s_c, axis=-1, keepdims=True))
        p_w = jnp.exp(s_w - m)
        p_c = jnp.exp(s_c - m)
        l = jnp.sum(p_w, axis=-1, keepdims=True) + jnp.sum(p_c, axis=-1, keepdims=True)
        o = _dot(p_w.astype(BF16), vw) + _dot(p_c.astype(BF16), vc)
        out = out + jnp.where(mh, o / l, 0.0)
    o_ref[...] = out


def _na(nq, nk, nv, bias, *, batch, seq, ctx):
    r = nq.shape[0]
    tq = NA_QROWS * GRID_W
    nj = seq // tq
    kb = seq // NA_KBLK
    assert NA_KBLK % ctx == 0 or ctx % NA_KBLK == 0
    ctx_blk0 = batch * seq // ctx

    def kspec(i):
        return pl.BlockSpec((NA_KBLK, 256),
                            lambda b, j: (b * kb + jnp.clip(2 * j - 1, 0, kb - 4) + i, 0))

    cspec = pl.BlockSpec((ctx, 256), lambda b, j: (ctx_blk0 + b, 0))
    variant = lambda j: jnp.where(j == 0, 0, jnp.where(j == nj - 1, 2, 1))
    return pl.pallas_call(
        _na_body,
        grid=(batch, nj),
        in_specs=[pl.BlockSpec((tq, 256), lambda b, j: (b * nj + j, 0))]
                 + [kspec(i) for i in range(4)] + [kspec(i) for i in range(4)]
                 + [cspec, cspec,
                    pl.BlockSpec((None, NA_HEADS, tq, 4 * NA_KBLK), lambda b, j: (variant(j), 0, 0, 0))],
        out_specs=pl.BlockSpec((tq, 256), lambda b, j: (b * nj + j, 0)),
        out_shape=jax.ShapeDtypeStruct((r, 256), F32),
        compiler_params=_params("parallel", "arbitrary"),
        name="na",
    )(nq, nk, nk, nk, nk, nv, nv, nv, nv, nk, nv, bias)


def _diff_lambda(lq_ref, lambda_init):
    lq = lq_ref[...]
    s1 = jnp.sum(lq[0:1, :] * lq[1:2, :], axis=-1, keepdims=True)
    s2 = jnp.sum(lq[2:3, :] * lq[3:4, :], axis=-1, keepdims=True)
    return jnp.exp(s1) - jnp.exp(s2) + lambda_init


def _stack_qt(qt):
    row = lax.broadcasted_iota(jnp.int32, (2 * DIFF_DK, 1), 0)
    zero = jnp.zeros_like(qt)
    return jnp.concatenate([jnp.where(row < DIFF_DK, qt, zero), jnp.where(row >= DIFF_DK, qt, zero)], axis=1)


def _diff_finish_t(acc, tq, lam, nw_col, lambda_init):
    o = acc[0:DIFF_DV, :] / acc[DIFF_DV:DIFF_DV + 1, :]
    od = o[:, :tq] - lam * o[:, tq:]
    ms = jnp.mean(od * od, axis=0, keepdims=True)
    y = od * lax.rsqrt(ms + RMS_EPS) * nw_col * (1.0 - lambda_init)
    y = jnp.concatenate([y, jnp.zeros_like(y)], axis=0)
    return jnp.transpose(y)[:, 0:DIFF_DV]


def _diff_body(qt_ref, kc_ref, vtc_ref, k_ref, vt_ref, lq_ref, nw_ref, o_ref, s_ref, *, tk, unroll, lambda_init):
    tq = qt_ref.shape[1]
    n_chunks = k_ref.shape[0] // tk
    ring = s_ref.shape[0]
    qs = _stack_qt(qt_ref[...])

    def chunk(j):
        return pl.ds(j * tk if isinstance(j, int) else pl.multiple_of(j * tk, tk), tk)

    def issue_scores(j, slot):
        s = _dot(k_ref[chunk(j), :], qs)
        s_ref[slot] = s
        return jnp.max(s, axis=0, keepdims=True)

    s_ctx = _dot(kc_ref[...], qs)
    m0 = jnp.max(s_ctx, axis=0, keepdims=True)
    cmax0 = tuple(issue_scores(min(a, n_chunks - 1), a % ring) for a in range(DIFF_AHEAD))
    acc0 = _dot(vtc_ref[...], jnp.exp2(s_ctx - m0).astype(BF16))

    def group(g, carry, tail=False):
        cmax, m, acc = carry
        for u in range(unroll):
            j = g * unroll + u
            if tail and j + DIFF_AHEAD >= n_chunks:
                c_new = cmax[0]
            else:
                c_new = issue_scores(j + DIFF_AHEAD, (u + DIFF_AHEAD) % ring)
            m_new = jnp.maximum(m, cmax[0])
            p = jnp.exp2(s_ref[u % ring] - m_new).astype(BF16)
            vt = vt_ref[:, chunk(j)]
            acc = jnp.exp2(m - m_new) * acc + _dot(vt, p)
            m, cmax = m_new, cmax[1:] + (c_new,)
        return cmax, m, acc

    n_groups = n_chunks // unroll
    carry = lax.fori_loop(0, n_groups - 1, group, (cmax0, m0, acc0))
    _, _, acc = group(n_groups - 1, carry, tail=True)
    o_ref[...] = _diff_finish_t(acc, tq, _diff_lambda(lq_ref, lambda_init), nw_ref[...], lambda_init)


def _diff(dqt, dk, dvt, lq, norm_w, *, batch, seq, ctx, lambda_init):
    h, r, _ = dk.shape
    tq = DIFF_TQ
    nq = seq // tq
    tk = min(DIFF_TK, seq)
    unroll = min(DIFF_UNROLL, seq // tk)
    assert (seq // tk) % unroll == 0 and unroll % DIFF_RING == 0 and DIFF_RING > DIFF_AHEAD
    ctx_blk0 = batch * seq // ctx
    return pl.pallas_call(
        functools.partial(_diff_body, tk=tk, unroll=unroll, lambda_init=lambda_init),
        grid=(batch, h, nq),
        in_specs=[pl.BlockSpec((None, 64, tq), lambda b, hh, i: (hh, 0, b * nq + i)),
                  pl.BlockSpec((None, ctx, 64), lambda b, hh, i: (hh, ctx_blk0 + b, 0)),
                  pl.BlockSpec((None, DIFF_VROWS, ctx), lambda b, hh, i: (hh, 0, ctx_blk0 + b)),
                  pl.BlockSpec((None, seq, 64), lambda b, hh, i: (hh, b, 0)),
                  pl.BlockSpec((None, DIFF_VROWS, seq), lambda b, hh, i: (hh, 0, b)),
                  _resident(lq.shape),
                  _resident((DIFF_DV, 1))],
        out_specs=pl.BlockSpec((None, tq, 64), lambda b, hh, i: (hh, b * nq + i, 0)),
        out_shape=jax.ShapeDtypeStruct((h, r, 64), F32),
        scratch_shapes=[pltpu.VMEM((DIFF_RING, tk, 2 * tq), F32)],
        compiler_params=_params("parallel", "parallel", "arbitrary"),
        name="diff",
    )(dqt, dk, dvt, dk, dvt, lq, norm_w.reshape(DIFF_DV, 1))


def _ctx_attn_body(nq_ref, nk_ref, nv_ref, dqt_ref, dk_ref, dvt_ref, lq_ref, nw_ref, na_in, df_in,
                   na_out, df_out, *, lambda_init):
    del na_in, df_in
    q = nq_ref[...]
    k = nk_ref[...]
    v = nv_ref[...]
    lane = lax.broadcasted_iota(jnp.int32, (1, NA_HEADS * NA_DIM), 1) // NA_DIM
    out = jnp.zeros(na_out.shape, F32)
    for hh in range(NA_HEADS):
        mh = lane == hh
        s = _dot_nt(jnp.where(mh, q, jnp.zeros_like(q)), k)
        p = jnp.exp(s - jnp.max(s, axis=-1, keepdims=True))
        o = _dot(p.astype(BF16), v) / jnp.sum(p, axis=-1, keepdims=True)
        out = out + jnp.where(mh, o, 0.0)
    na_out[...] = out

    lam = _diff_lambda(lq_ref, lambda_init)
    tq = dqt_ref.shape[2]
    for hh in range(DIFF_HEADS):
        s = _dot(dk_ref[hh], _stack_qt(dqt_ref[hh]))
        p = jnp.exp2(s - jnp.max(s, axis=0, keepdims=True))
        acc = _dot(dvt_ref[hh], p.astype(BF16))
        df_out[hh] = _diff_finish_t(acc, tq, lam, nw_ref[...], lambda_init)


def _ctx_attn(nq, nk, nv, dqt, dk, dvt, lq, norm_w, na_o, df_o, *, batch, seq, ctx, lambda_init):
    blk0 = batch * seq // ctx
    s256 = pl.BlockSpec((ctx, 256), lambda b: (blk0 + b, 0))
    sh64 = pl.BlockSpec((DIFF_HEADS, ctx, 64), lambda b: (0, blk0 + b, 0))
    sq_t = pl.BlockSpec((DIFF_HEADS, 64, ctx), lambda b: (0, 0, blk0 + b))
    sv_t = pl.BlockSpec((DIFF_HEADS, DIFF_VROWS, ctx), lambda b: (0, 0, blk0 + b))
    return pl.pallas_call(
        functools.partial(_ctx_attn_body, lambda_init=lambda_init),
        grid=(batch,),
        in_specs=[s256, s256, s256, sq_t, sh64, sv_t, _resident(lq.shape), _resident((DIFF_DV, 1)),
                  pl.BlockSpec(memory_space=pl.ANY), pl.BlockSpec(memory_space=pl.ANY)],
        out_specs=[s256, sh64],
        out_shape=[jax.ShapeDtypeStruct(na_o.shape, F32), jax.ShapeDtypeStruct(df_o.shape, F32)],
        input_output_aliases={8: 0, 9: 1},
        compiler_params=_params("arbitrary"),
        name="ctx_attn",
    )(nq, nk, nv, dqt, dk, dvt, lq, norm_w.reshape(DIFF_DV, 1), na_o, df_o)


def _conv_body(u_ref, prev_ref, next_ref, dw_ref, dwb_ref, lng_ref, lnb_ref, pw_ref, pwb_ref,
               o_ref, pad_ref, *, tiles_per_seq, n_latent_tiles):
    t = u_ref.shape[0]
    i = pl.program_id(0)
    in_ctx = i >= n_latent_tiles
    first = in_ctx | (i % tiles_per_seq == 0)
    last = in_ctx | (i % tiles_per_seq == tiles_per_seq - 1)
    pad_ref[0:HALO, :] = jnp.where(first, 0.0, prev_ref[...])
    pad_ref[HALO:HALO + t, :] = u_ref[...]
    pad_ref[HALO + t:HALO + t + HALO, :] = jnp.where(last, 0.0, next_ref[...])
    acc = jnp.zeros((t, CONV_CH), F32)
    base = HALO - CONV_K // 2
    for k in range(CONV_K):
        acc = acc + dw_ref[k:k + 1, :] * pad_ref[base + k:base + k + t, :]
    y = acc + dwb_ref[...]
    mu = jnp.mean(y, axis=-1, keepdims=True)
    yc = y - mu
    var = jnp.mean(yc * yc, axis=-1, keepdims=True)
    y = _silu(yc * lax.rsqrt(var + LN_EPS) * lng_ref[...] + lnb_ref[...])
    o_ref[...] = _dot(y.astype(BF16), pw_ref[...]) + pwb_ref[...]


def _conv(cu, dw, dw_b, ln_g, ln_b, pw, pw_b, *, batch, seq, ctx):
    r = cu.shape[0]
    t = CONV_TILE
    assert ctx == t and seq % t == 0
    n_tiles = r // t
    hb = t // HALO
    vec = lambda a: a.reshape(1, CONV_CH)
    return pl.pallas_call(
        functools.partial(_conv_body, tiles_per_seq=seq // t, n_latent_tiles=batch * seq // t),
        grid=(n_tiles,),
        in_specs=[pl.BlockSpec((t, CONV_CH), lambda i: (i, 0)),
                  pl.BlockSpec((HALO, CONV_CH), lambda i: (jnp.maximum(i * hb - 1, 0), 0)),
                  pl.BlockSpec((HALO, CONV_CH), lambda i: (jnp.minimum((i + 1) * hb, n_tiles * hb - 1), 0)),
                  _resident((CONV_K, CONV_CH))] + [_resident((1, CONV_CH))] * 3
                 + [_resident((CONV_CH, CONV_CH)), _resident((1, CONV_CH))],
        out_specs=pl.BlockSpec((t, CONV_CH), lambda i: (i, 0)),
        out_shape=jax.ShapeDtypeStruct((r, CONV_CH), F32),
        scratch_shapes=[pltpu.VMEM((t + 2 * HALO, CONV_CH), F32)],
        compiler_params=_params("parallel"),
        name="conv",
    )(cu, cu, cu, dw, vec(dw_b), vec(ln_g), vec(ln_b), pw.astype(BF16), vec(pw_b))


def _wout_body(h_ref, mod_ref, gx_ref, nx_ref, dx_ref, cx_ref, w_ref, o_ref):
    mix = jnp.concatenate([gx_ref[...], nx_ref[...]] + [dx_ref[hh] for hh in range(DIFF_HEADS)]
                          + [cx_ref[...]], axis=-1).astype(BF16)
    o_ref[...] = h_ref[...] + mod_ref[5:6, :] * _dot(mix, w_ref[...])


def _wout(h, mods, gx, nx, dx, cx, w_out, *, n_tiles, group_of):
    r, d = h.shape
    tm = ROW_TILE
    row = lambda i: (i, 0)
    return pl.pallas_call(
        _wout_body,
        grid=(n_tiles,),
        in_specs=[pl.BlockSpec((tm, d), row),
                  pl.BlockSpec((None, N_MOD, d), lambda i: (group_of(i), 0, 0)),
                  pl.BlockSpec((tm, 256), row), pl.BlockSpec((tm, 256), row),
                  pl.BlockSpec((DIFF_HEADS, tm, 64), lambda i: (0, i, 0)),
                  pl.BlockSpec((tm, 256), row),
                  _resident(w_out.shape)],
        out_specs=pl.BlockSpec((tm, d), row),
        out_shape=jax.ShapeDtypeStruct((n_tiles * tm, d), F32),
        compiler_params=_params("parallel"),
        name="wout",
    )(h, mods, gx, nx, dx, cx, w_out.astype(BF16))


def kernel(x, c, ctx, c_ctx, ada_w, ada_b, norm_ffn1, ffn1_w13, ffn1_w2, norm_mix, w_in, gla_wa_f, gla_ba_f, gla_wa_b, gla_ba_b, gla_norm, na_rpb, diff_lq1, diff_lk1, diff_lq2, diff_lk2, diff_norm, conv_dw, conv_dw_b, conv_ln_g, conv_ln_b, conv_pw, conv_pw_b, w_out, norm_ffn2, ffn2_w13, ffn2_w2, final_norm):
    batch, seq, d = x.shape
    n_ctx = ctx.shape[1]
    depth = ada_w.shape[0]
    rows = seq // GRID_W
    tm = ROW_TILE
    assert seq % tm == 0 and (batch * n_ctx) % tm == 0 and batch + 1 <= 8
    lat_tiles = batch * seq // tm
    all_tiles = lat_tiles + batch * n_ctx // tm
    tiles_per_batch = seq // tm

    def group_of(i):
        return jnp.minimum(i // tiles_per_batch, batch)

    def pos_of(i):
        return jnp.where(i < lat_tiles, i % tiles_per_batch, tiles_per_batch)

    c_rows = jnp.concatenate([c, c_ctx[None, :], jnp.zeros((8 - batch - 1, d), F32)], axis=0)
    mods_all = _ada(c_rows, ada_w, ada_b)[:, :batch + 1].reshape(depth, batch + 1, N_MOD, d)
    rope = _rope_tables(seq, tm)
    h = x.reshape(batch * seq, d)
    h_ctx = ctx.reshape(batch * n_ctx, d)

    for i in range(depth):
        last = i == depth - 1
        lambda_init = 0.8 - 0.6 * math.exp(-0.3 * i)
        mods = mods_all[i]
        tok = dict(n_tiles=all_tiles, group_of=group_of)
        geo = dict(batch=batch, seq=seq, ctx=n_ctx)

        h = _ffn(h, h_ctx if i == 0 else None, mods, norm_ffn1[i], ffn1_w13[i], ffn1_w2[i], final_norm,
                 k0=0, final=False, **tok)

        wa, ba = _gate_weights(gla_wa_f[i], gla_ba_f[i], gla_wa_b[i], gla_ba_b[i])
        (gqk, gv, gvt, gg, glg, nq, nk, nv, dq, dk, dv, cu) = _proj(
            h, mods, norm_mix[i], _permute_w_in(w_in[i]), wa, ba, rope, pos_of=pos_of, **tok)

        gnorm = jnp.tile(gla_norm[i], GLA_HEADS).reshape(1, GLA_HEADS * GLA_DV)
        o_f = _gla(gqk, gv, gvt, glg, gg, gg, gnorm, reverse=False, **geo)
        gx = _gla(gqk, gv, gvt, glg, gg, o_f, gnorm, reverse=True, **geo)

        nx = _na(nq, nk, nv, _na_bias_tables(na_rpb[i], rows), **geo)

        lq = jnp.stack([diff_lq1[i], diff_lk1[i], diff_lq2[i], diff_lk2[i]])
        dx = _diff(dq, dk, dv, lq, diff_norm[i], lambda_init=lambda_init, **geo)
        if not last:
            nx, dx = _ctx_attn(nq, nk, nv, dq, dk, dv, lq, diff_norm[i], nx, dx,
                               lambda_init=lambda_init, **geo)

        cx = _conv(cu, conv_dw[i], conv_dw_b[i], conv_ln_g[i], conv_ln_b[i], conv_pw[i], conv_pw_b[i], **geo)

        if last:
            tok = dict(n_tiles=lat_tiles, group_of=group_of)
        h = _wout(h, mods, gx, nx, dx, cx, w_out[i], **tok)

        h = _ffn(h, None, mods, norm_ffn2[i], ffn2_w13[i], ffn2_w2[i], final_norm, k0=6, final=last, **tok)

    return h.reshape(batch, seq, d)
```

```python
import functools
import math

import jax
import jax.numpy as jnp
from jax import lax
from jax.experimental import pallas as pl
from jax.experimental.pallas import tpu as pltpu

F32 = jnp.float32
BF16 = jnp.bfloat16

GRID_W = 64
N_MOD = 9
RMS_EPS = 1e-6
LN_EPS = 1e-5
NEG_INF = -1e30
GLA_HEADS, GLA_DK, GLA_DV, GLA_RANK, GLA_TAU, GLA_CHUNK = 4, 32, 64, 16, 16.0, 64
NA_HEADS, NA_DIM, NA_ROWS, NA_COLS = 4, 64, 8, 16
DIFF_HEADS, DIFF_DK, DIFF_DV = 4, 32, 64
DIFF_VROWS = 128
CONV_CH, CONV_K = 256, 31
ROPE_BASE = 10000.0
LOG2E = 1.4426950408889634

LANES = 128
VMEM_LIMIT = 56 * 1024 * 1024

ROW_TILE = 512
FF_CHUNK = 256
GLA_BLOCK = 256
NA_QROWS = 8
NA_KBLK = 256
CONV_TILE = 256
DIFF_TQ = 256
DIFF_TK = 512
DIFF_UNROLL = 8
DIFF_AHEAD = 3
DIFF_RING = 4
HALO = 16

C_GLA, C_NA, C_DIFF, C_CONV, C_AUX, C_END = 0, 768, 1536, 2304, 2816, 2944


def _dot(a, b):
    return jnp.dot(a, b, preferred_element_type=F32)


def _dot_nt(a, b):
    return lax.dot_general(a, b, (((1,), (1,)), ((), ())), preferred_element_type=F32)


def _params(*sem):
    return pltpu.CompilerParams(dimension_semantics=sem, vmem_limit_bytes=VMEM_LIMIT)


def _resident(shape):
    nd = len(shape)
    return pl.BlockSpec(shape, lambda *_: (0,) * nd, pipeline_mode=pl.Buffered(1))


def _silu(x):
    return x * jax.nn.sigmoid(x)


def _rms(x, w):
    return x * lax.rsqrt(jnp.mean(x * x, axis=-1, keepdims=True) + RMS_EPS) * w


def _ada_body(c_ref, w_ref, b_ref, o_ref):
    s = _silu(c_ref[...])
    o_ref[...] = jnp.dot(s, w_ref[...], precision=lax.Precision.HIGHEST,
                         preferred_element_type=F32) + b_ref[...]


def _ada(c_rows, ada_w, ada_b):
    depth, d, _ = ada_w.shape
    return pl.pallas_call(
        _ada_body,
        grid=(depth, N_MOD),
        in_specs=[pl.BlockSpec((8, d), lambda l, n: (0, 0)),
                  pl.BlockSpec((None, d, d), lambda l, n: (l, 0, n)),
                  pl.BlockSpec((None, 1, d), lambda l, n: (l, 0, n))],
        out_specs=pl.BlockSpec((None, 8, d), lambda l, n: (l, 0, n)),
        out_shape=jax.ShapeDtypeStruct((depth, 8, N_MOD * d), F32),
        compiler_params=_params("arbitrary", "arbitrary"),
        name="ada",
    )(c_rows, ada_w, ada_b.reshape(depth, 1, N_MOD * d))


def _ffn_body(h_ref, hc_ref, mod_ref, nw_ref, w13_ref, w2_ref, fw_ref, o_ref, xb_ref, acc_ref, h13_ref,
              *, k0, final, n_first):
    n_chunks, tf, _ = w2_ref.shape
    ff = n_chunks * tf
    x = h_ref[...]
    if n_first is not None:
        x = jnp.where(pl.program_id(0) < n_first, x, hc_ref[...])
    xm = _rms(x, nw_ref[...]) * (1.0 + mod_ref[k0 + 1:k0 + 2, :]) + mod_ref[k0:k0 + 1, :]
    xb_ref[...] = xm.astype(BF16)
    acc_ref[...] = jnp.zeros_like(acc_ref)

    def up(c, slot):
        col = c * tf if isinstance(c, int) else pl.multiple_of(c * tf, tf)
        h13_ref[slot, :, :tf] = _dot(xb_ref[...], w13_ref[:, pl.ds(col, tf)])
        h13_ref[slot, :, tf:] = _dot(xb_ref[...], w13_ref[:, pl.ds(ff + col, tf)])

    def down(c, slot):
        a = h13_ref[slot, :, :tf]
        u = h13_ref[slot, :, tf:]
        acc_ref[...] += _dot((_silu(a) * u).astype(BF16), w2_ref[c])

    def pair(t, carry):
        up(2 * t + 1, 1)
        down(2 * t, 0)
        up(2 * t + 2, 0)
        down(2 * t + 1, 1)
        return carry

    up(0, 0)
    lax.fori_loop(0, (n_chunks - 1) // 2, pair, 0)
    down(n_chunks - 1, 0)
    out = x + (0.5 * mod_ref[k0 + 2:k0 + 3, :]) * acc_ref[...]
    if final:
        out = _rms(out, fw_ref[...])
    o_ref[...] = out


def _ffn(h, h_ctx, mods, norm_w, w13, w2, final_w, *, k0, n_tiles, group_of, final):
    d = h.shape[1]
    tm = ROW_TILE
    ff = w2.shape[0]
    n_chunks = ff // FF_CHUNK
    assert ff % FF_CHUNK == 0 and n_chunks % 2 == 1
    if h_ctx is None:
        n_first, h_ctx = None, norm_w.reshape(1, d)
        row0 = lambda i: (i, 0)
        ctx_spec = _resident((1, d))
    else:
        n_first = h.shape[0] // tm
        row0 = lambda i: (jnp.minimum(i, n_first - 1), 0)
        ctx_spec = pl.BlockSpec((tm, d), lambda i: (jnp.maximum(i - n_first, 0), 0))
    return pl.pallas_call(
        functools.partial(_ffn_body, k0=k0, final=final, n_first=n_first),
        grid=(n_tiles,),
        in_specs=[pl.BlockSpec((tm, d), row0),
                  ctx_spec,
                  pl.BlockSpec((None, N_MOD, d), lambda i: (group_of(i), 0, 0)),
                  _resident((1, d)),
                  _resident(w13.shape),
                  _resident((n_chunks, FF_CHUNK, d)),
                  _resident((1, d))],
        out_specs=pl.BlockSpec((tm, d), lambda i: (i, 0)),
        out_shape=jax.ShapeDtypeStruct((n_tiles * tm, d), F32),
        scratch_shapes=[pltpu.VMEM((tm, d), BF16), pltpu.VMEM((tm, d), F32),
                        pltpu.VMEM((2, tm, 2 * FF_CHUNK), F32)],
        compiler_params=_params("parallel"),
        name="ffn",
    )(h, h_ctx, mods, norm_w.reshape(1, d), w13.astype(BF16),
      w2.astype(BF16).reshape(n_chunks, FF_CHUNK, d), final_w.reshape(1, d))


def _log_sigmoid(x):
    return jnp.minimum(x, 0.0) - jnp.log1p(jnp.exp(-jnp.abs(x)))


def _rope_rotate(x):
    n = x.shape[-1]
    lane = lax.broadcasted_iota(jnp.int32, x.shape, 1)
    up = pltpu.roll(x, n - 8, 1)
    dn = pltpu.roll(x, 8, 1)
    return jnp.where((lane & 15) < 8, -up, dn)


def _rope_rotate_rows(x):
    n = x.shape[0]
    row = lax.broadcasted_iota(jnp.int32, x.shape, 0)
    up = pltpu.roll(x, n - 8, 0)
    dn = pltpu.roll(x, 8, 0)
    return jnp.where((row & 15) < 8, -up, dn)


def _proj_body(h_ref, mod_ref, nw_ref, w_ref, wvt_ref, wdqt_ref, wdvt_ref, wa_ref, ba_ref,
               cos_ref, sin_ref, cost_ref, sint_ref,
               gqk_ref, gv_ref, gvt_ref, gg_ref, glg_ref, nq_ref, nk_ref, nv_ref,
               dqt_ref, dk_ref, dvt_ref, cu_ref, xb_ref):
    x = h_ref[...]
    tm = x.shape[0]
    xm = _rms(x, nw_ref[...]) * (1.0 + mod_ref[4:5, :]) + mod_ref[3:4, :]
    xb_ref[...] = xm.astype(BF16)

    z = _dot(xb_ref[...], w_ref[:, C_GLA:C_GLA + 256])
    lane = lax.broadcasted_iota(jnp.int32, (1, 256), 1)
    gqk_ref[...] = z * jnp.where(lane < 128, GLA_DK ** -0.5, 1.0)
    gv_ref[...] = _dot(xb_ref[...], w_ref[:, C_GLA + 256:C_GLA + 512])
    gvt_ref[...] = _dot_nt(wvt_ref[...], xb_ref[...])
    gg_ref[...] = _dot(xb_ref[...], w_ref[:, C_GLA + 512:C_GLA + 768])
    aux = _dot(xb_ref[...], w_ref[:, C_AUX:C_END])
    pre = _dot(aux.astype(BF16), wa_ref[...]) + ba_ref[...]
    glg_ref[...] = _log_sigmoid(pre) * (1.0 / GLA_TAU)

    nq_ref[...] = (_dot(xb_ref[...], w_ref[:, C_NA:C_NA + 256]) * (NA_DIM ** -0.5)).astype(BF16)
    nk_ref[...] = _dot(xb_ref[...], w_ref[:, C_NA + 256:C_NA + 512]).astype(BF16)
    nv_ref[...] = _dot(xb_ref[...], w_ref[:, C_NA + 512:C_NA + 768]).astype(BF16)

    cos = cos_ref[...]
    sin = sin_ref[...]
    cos2 = jnp.concatenate([cos, cos], axis=1)
    sin2 = jnp.concatenate([sin, sin], axis=1)
    zk = _dot(xb_ref[...], w_ref[:, C_DIFF + 256:C_DIFF + 512])
    zk = zk * cos2 + _rope_rotate(zk) * sin2
    n_rep = 2 * DIFF_HEADS
    cos_t = jnp.concatenate([cost_ref[...]] * n_rep, axis=0)
    sin_t = jnp.concatenate([sint_ref[...]] * n_rep, axis=0)
    zqt = _dot_nt(wdqt_ref[...], xb_ref[...])
    zqt = (zqt * cos_t + _rope_rotate_rows(zqt) * sin_t) * (DIFF_DK ** -0.5 * LOG2E)
    zvt = _dot_nt(wdvt_ref[...], xb_ref[...])
    pad_rows = DIFF_VROWS - DIFF_DV
    one_row = jnp.where(lax.broadcasted_iota(jnp.int32, (pad_rows, tm), 0) == 0, 1.0, 0.0).astype(BF16)
    for hh in range(DIFF_HEADS):
        sl = slice(64 * hh, 64 * hh + 64)
        dk_ref[hh] = zk[:, sl].astype(BF16)
        dqt_ref[hh] = zqt[sl, :].astype(BF16)
        dvt_ref[hh, 0:64, :] = zvt[sl, :].astype(BF16)
        dvt_ref[hh, DIFF_DV:DIFF_VROWS, :] = one_row

    za = _dot(xb_ref[...], w_ref[:, C_CONV:C_CONV + 256])
    zg = _dot(xb_ref[...], w_ref[:, C_CONV + 256:C_CONV + 512])
    cu_ref[...] = za * jax.nn.sigmoid(zg)


def _proj(h, mods, norm_w, w_p, wa, ba, rope, *, n_tiles, group_of, pos_of):
    r, d = h.shape
    tm = ROW_TILE
    row = lambda i: (i, 0)
    hrow = lambda i: (0, i, 0)
    hcol = lambda i: (0, 0, i)
    f32_256 = jax.ShapeDtypeStruct((r, 256), F32)
    bf_256 = jax.ShapeDtypeStruct((r, 256), BF16)
    wvt = w_p[:, C_GLA + 256:C_GLA + 512].T
    wdqt = w_p[:, C_DIFF:C_DIFF + 256].T
    wdvt = w_p[:, C_DIFF + 512:C_DIFF + 768].T
    cos_r, sin_r, cos_c, sin_c = rope
    out_shape = [f32_256, f32_256, jax.ShapeDtypeStruct((256, r), F32), f32_256, f32_256,
                 bf_256, bf_256, bf_256,
                 jax.ShapeDtypeStruct((DIFF_HEADS, 64, r), BF16),
                 jax.ShapeDtypeStruct((DIFF_HEADS, r, 64), BF16),
                 jax.ShapeDtypeStruct((DIFF_HEADS, DIFF_VROWS, r), BF16),
                 f32_256]
    out_specs = [pl.BlockSpec((tm, 256), row)] * 2 + [pl.BlockSpec((256, tm), lambda i: (0, i))] + [
        pl.BlockSpec((tm, 256), row)] * 5 + [
        pl.BlockSpec((DIFF_HEADS, 64, tm), hcol),
        pl.BlockSpec((DIFF_HEADS, tm, 64), hrow),
        pl.BlockSpec((DIFF_HEADS, DIFF_VROWS, tm), hcol),
        pl.BlockSpec((tm, 256), row)]
    return pl.pallas_call(
        _proj_body,
        grid=(n_tiles,),
        in_specs=[pl.BlockSpec((tm, d), row),
                  pl.BlockSpec((None, N_MOD, d), lambda i: (group_of(i), 0, 0)),
                  _resident((1, d)),
                  _resident(w_p.shape),
                  _resident(wvt.shape),
                  _resident(wdqt.shape),
                  _resident(wdvt.shape),
                  _resident(wa.shape),
                  _resident(ba.shape),
                  pl.BlockSpec((tm, LANES), lambda i: (pos_of(i), 0)),
                  pl.BlockSpec((tm, LANES), lambda i: (pos_of(i), 0)),
                  pl.BlockSpec((DIFF_DK, tm), lambda i: (0, pos_of(i))),
                  pl.BlockSpec((DIFF_DK, tm), lambda i: (0, pos_of(i)))],
        out_specs=out_specs,
        out_shape=out_shape,
        scratch_shapes=[pltpu.VMEM((tm, d), BF16)],
        compiler_params=_params("parallel"),
        name="proj",
    )(h, mods, norm_w.reshape(1, d), w_p, wvt, wdqt, wdvt, wa, ba, cos_r, sin_r, cos_c, sin_c)


def _permute_w_in(w_in):
    d = w_in.shape[0]
    g0 = 2 * GLA_HEADS * GLA_DK + 2 * GLA_HEADS * GLA_DV
    aux = w_in[:, g0:g0 + 2 * GLA_RANK]
    rest = w_in[:, g0 + 2 * GLA_RANK:]
    pad = jnp.zeros((d, C_END - C_AUX - 2 * GLA_RANK), w_in.dtype)
    return jnp.concatenate([w_in[:, :g0], rest, aux, pad], axis=1).astype(BF16)


def _gate_weights(wa_f, ba_f, wa_b, ba_b):
    n = GLA_HEADS * GLA_DK
    wa = jnp.zeros((C_END - C_AUX, 2 * n), F32)
    wa = wa.at[:GLA_RANK, :n].set(wa_f).at[GLA_RANK:2 * GLA_RANK, n:].set(wa_b)
    return wa.astype(BF16), jnp.concatenate([ba_f, ba_b]).reshape(1, 2 * n)


def _rope_tables(seq, tile):
    t = jnp.arange(seq)
    row = (t // GRID_W).astype(F32)
    col = (t % GRID_W).astype(F32)
    half = DIFF_DK // 2
    inv = 1.0 / (ROPE_BASE ** (jnp.arange(0, half, 2, dtype=F32) / half))
    ang_r = row[:, None] * inv
    ang_c = col[:, None] * inv
    ang = jnp.concatenate([ang_r, ang_r, ang_c, ang_c], axis=-1)
    cos = jnp.concatenate([jnp.cos(ang), jnp.ones((tile, DIFF_DK), F32)], axis=0)
    sin = jnp.concatenate([jnp.sin(ang), jnp.zeros((tile, DIFF_DK), F32)], axis=0)
    rep = (1, LANES // DIFF_DK)
    return jnp.tile(cos, rep), jnp.tile(sin, rep), cos.T, sin.T


def _split3(x):
    hi = x.astype(BF16)
    r1 = x - hi.astype(F32)
    mid = r1.astype(BF16)
    lo = (r1 - mid.astype(F32)).astype(BF16)
    return hi, mid, lo


def _gla_body(qk_ref, v_ref, vt_ref, lg_ref, g_ref, of_ref, nw_ref, o_ref, st_ref, *, reverse):
    blk = qk_ref.shape[0]
    n_chunks = blk // GLA_CHUNK
    nqk = GLA_HEADS * GLA_DK
    nv = GLA_HEADS * GLA_DV

    @pl.when(pl.program_id(1) == 0)
    def _():
        st_ref[...] = jnp.zeros_like(st_ref)

    ri = lax.broadcasted_iota(jnp.int32, (blk, blk), 0)
    ci = lax.broadcasted_iota(jnp.int32, (blk, blk), 1)
    same = (ri // GLA_CHUNK) == (ci // GLA_CHUNK)
    tri = jnp.where(same & ((ci >= ri) if reverse else (ci <= ri)), 1.0, 0.0).astype(BF16)
    lg = lg_ref[:, nqk:2 * nqk] if reverse else lg_ref[:, 0:nqk]
    hi, mid, lo = _split3(lg)
    bcum_all = _dot(tri, hi) + _dot(tri, mid) + _dot(tri, lo)

    c = GLA_CHUNK
    rk = lax.broadcasted_iota(jnp.int32, (GLA_HEADS * c, nqk), 0) // c
    ck = lax.broadcasted_iota(jnp.int32, (GLA_HEADS * c, nqk), 1) // GLA_DK
    mask_k = rk == ck
    rv = lax.broadcasted_iota(jnp.int32, (GLA_HEADS * c, nv), 0) // c
    cv = lax.broadcasted_iota(jnp.int32, (GLA_HEADS * c, nv), 1) // GLA_DV
    mask_v = rv == cv
    rs = lax.broadcasted_iota(jnp.int32, (nv, nqk), 0) // GLA_DV
    cs = lax.broadcasted_iota(jnp.int32, (nv, nqk), 1) // GLA_DK
    mask_s = rs == cs
    ai = lax.broadcasted_iota(jnp.int32, (c, GLA_HEADS * c), 0)
    aj = lax.broadcasted_iota(jnp.int32, (c, GLA_HEADS * c), 1) % c
    mask_a = (aj >= ai) if reverse else (aj <= ai)

    vt = vt_ref[...].astype(BF16)
    order = range(n_chunks - 1, -1, -1) if reverse else range(n_chunks)
    q_in, a_raw, u_t, decay = {}, {}, {}, {}
    for ch in order:
        rows = slice(ch * c, (ch + 1) * c)
        k = qk_ref[rows, nqk:2 * nqk]
        bcum = bcum_all[rows, :]
        btot = bcum[0:1, :] if reverse else bcum[c - 1:c, :]
        q_in[ch] = (qk_ref[rows, 0:nqk] * jnp.exp(bcum)).astype(BF16)
        k_in = k * jnp.exp(-bcum)
        k_out = (k * jnp.exp(btot - bcum)).astype(BF16)
        k_bd = jnp.where(mask_k, jnp.concatenate([k_in] * GLA_HEADS, axis=0), 0.0).astype(BF16)
        a_raw[ch] = _dot_nt(q_in[ch], k_bd)
        k_pad = jnp.concatenate([jnp.zeros((n * c, nqk), BF16) for n in (ch,) if n] + [k_out]
                                + [jnp.zeros((n * c, nqk), BF16) for n in (n_chunks - 1 - ch,) if n], axis=0)
        u_t[ch] = jnp.where(mask_s, _dot(vt, k_pad), 0.0)
        decay[ch] = jnp.exp(btot)
    o = {}
    for ch in order:
        v = v_ref[ch * c:(ch + 1) * c, :]
        v_bd = jnp.where(mask_v, jnp.concatenate([v] * GLA_HEADS, axis=0), 0.0).astype(BF16)
        o[ch] = _dot(jnp.where(mask_a, a_raw[ch], 0.0).astype(BF16), v_bd)
    st = st_ref[...]
    for ch in order:
        o[ch] = o[ch] + _dot_nt(q_in[ch], st.astype(BF16))
        st = st * decay[ch] + u_t[ch]
    st_ref[...] = st
    o_blk = jnp.concatenate([o[ch] for ch in range(n_chunks)], axis=0)
    if reverse:
        o_blk = o_blk + of_ref[...]
        hi2, lo2, _ = _split3(o_blk * o_blk)
        hr = lax.broadcasted_iota(jnp.int32, (nv, nv), 0) // GLA_DV
        hc = lax.broadcasted_iota(jnp.int32, (nv, nv), 1) // GLA_DV
        seg = jnp.where(hr == hc, 1.0, 0.0).astype(BF16)
        ms = (_dot(hi2, seg) + _dot(lo2, seg)) * (1.0 / GLA_DV)
        o_blk = o_blk * lax.rsqrt(ms + RMS_EPS) * nw_ref[...] * _silu(g_ref[...])
    o_ref[...] = o_blk


def _gla(gqk, gv, gvt, glg, gg, o_f, norm_w4, *, reverse, batch, seq, ctx):
    r = gqk.shape[0]
    blk = GLA_BLOCK
    nc, nl = ctx // blk, seq // blk
    ctx_base = batch * seq // blk

    def blk_of(b, s):
        if reverse:
            return jnp.where(s < nc, ctx_base + b * nc + (nc - 1 - s), b * nl + (nl - 1 - (s - nc)))
        return jnp.where(s < nc, ctx_base + b * nc + s, b * nl + (s - nc))

    spec = pl.BlockSpec((blk, 256), lambda b, s: (blk_of(b, s), 0))
    spec_t = pl.BlockSpec((256, blk), lambda b, s: (0, blk_of(b, s)))
    return pl.pallas_call(
        functools.partial(_gla_body, reverse=reverse),
        grid=(batch, nc + nl),
        in_specs=[spec, spec, spec_t, spec, spec, spec, _resident((1, 256))],
        out_specs=spec,
        out_shape=jax.ShapeDtypeStruct((r, 256), F32),
        scratch_shapes=[pltpu.VMEM((GLA_HEADS * GLA_DV, GLA_HEADS * GLA_DK), F32)],
        compiler_params=_params("arbitrary", "arbitrary"),
        name="gla_bwd" if reverse else "gla_fwd",
    )(gqk, gv, gvt, glg, gg, o_f, norm_w4)


def _na_col_tables(rpb):
    col = jnp.arange(GRID_W)
    cstart = jnp.clip(col - NA_COLS // 2, 0, GRID_W - NA_COLS)
    col_ok = (col[None, :] >= cstart[:, None]) & (col[None, :] < cstart[:, None] + NA_COLS)
    col_off = jnp.clip(col[None, :] - col[:, None] + (NA_COLS - 1), 0, 2 * NA_COLS - 2)
    t = jnp.where(col_ok[None, None], rpb[:, :, col_off], NEG_INF)
    dead = jnp.full((NA_HEADS, 1, GRID_W, GRID_W), NEG_INF, F32)
    t = jnp.concatenate([dead, t.astype(F32), dead], axis=1)
    return jnp.concatenate([t[:, :-1], t[:, 1:]], axis=-1)


def _na_body(q_ref, k0, k1, k2, k3, v0, v1, v2, v3, kc_ref, vc_ref, tab_ref, o_ref, *, rows, key_blocks):
    q = q_ref[...]
    kw = jnp.concatenate([k0[...], k1[...], k2[...], k3[...]], axis=0)
    vw = jnp.concatenate([v0[...], v1[...], v2[...], v3[...]], axis=0)
    kc = kc_ref[...]
    vc = vc_ref[...]

    j = pl.program_id(1)
    rows_per_blk = NA_KBLK // GRID_W
    r0 = j * NA_QROWS
    kr0 = jnp.clip(2 * j - 1, 0, key_blocks - 4) * rows_per_blk
    lane_lo = lax.broadcasted_iota(jnp.int32, (1, 2 * GRID_W), 1) < GRID_W
    n_pairs = 4 * rows_per_blk // 2
    entry, ok = [], []
    for a in range(NA_QROWS):
        r = r0 + a
        start = jnp.clip(r - NA_ROWS // 2, 0, rows - NA_ROWS)
        for bp in range(n_pairs):
            rk = kr0 + 2 * bp
            entry.append(jnp.clip(rk - r + NA_ROWS, 0, 2 * NA_ROWS - 1))
            in0 = ((rk >= start) & (rk < start + NA_ROWS)).astype(jnp.int32)
            in1 = ((rk + 1 >= start) & (rk + 1 < start + NA_ROWS)).astype(jnp.int32)
            ok.append(jnp.where(lane_lo, in0, in1) != 0)

    def bias_of(hh):
        rows_ = []
        for a in range(NA_QROWS):
            tiles = [jnp.where(ok[a * n_pairs + bp], tab_ref[hh, entry[a * n_pairs + bp]], NEG_INF)
                     for bp in range(n_pairs)]
            rows_.append(jnp.concatenate(tiles, axis=1))
        return jnp.concatenate(rows_, axis=0)

    lane = lax.broadcasted_iota(jnp.int32, (1, NA_HEADS * NA_DIM), 1) // NA_DIM
    out = jnp.zeros(o_ref.shape, F32)
    for hh in range(NA_HEADS):
        mh = lane == hh
        qh = jnp.where(mh, q, jnp.zeros_like(q))
        s_w = _dot_nt(qh, kw) + bias_of(hh)
        s_c = _dot_nt(qh, kc)
        m = jnp.maximum(jnp.max(s_w, axis=-1, keepdims=True), jnp.max(s_c, axis=-1, keepdims=True))
        p_w = jnp.exp(s_w - m)
        p_c = jnp.exp(s_c - m)
        l = jnp.sum(p_w, axis=-1, keepdims=True) + jnp.sum(p_c, axis=-1, keepdims=True)
        o = _dot(p_w.astype(BF16), vw) + _dot(p_c.astype(BF16), vc)
        out = out + jnp.where(mh, o / l, 0.0)
    o_ref[...] = out


def _na(nq, nk, nv, tab, *, batch, seq, ctx):
    r = nq.shape[0]
    tq = NA_QROWS * GRID_W
    nj = seq // tq
    kb = seq // NA_KBLK
    assert kb >= 4 and seq % tq == 0 and seq % ctx == 0
    ctx_blk0 = batch * seq // ctx

    def kspec(i):
        return pl.BlockSpec((NA_KBLK, 256),
                            lambda b, j: (b * kb + jnp.clip(2 * j - 1, 0, kb - 4) + i, 0))

    cspec = pl.BlockSpec((ctx, 256), lambda b, j: (ctx_blk0 + b, 0))
    return pl.pallas_call(
        functools.partial(_na_body, rows=seq // GRID_W, key_blocks=kb),
        grid=(batch, nj),
        in_specs=[pl.BlockSpec((tq, 256), lambda b, j: (b * nj + j, 0))]
                 + [kspec(i) for i in range(4)] + [kspec(i) for i in range(4)]
                 + [cspec, cspec, _resident(tab.shape)],
        out_specs=pl.BlockSpec((tq, 256), lambda b, j: (b * nj + j, 0)),
        out_shape=jax.ShapeDtypeStruct((r, 256), F32),
        compiler_params=_params("parallel", "arbitrary"),
        name="na",
    )(nq, nk, nk, nk, nk, nv, nv, nv, nv, nk, nv, tab)


def _diff_lambda(lq_ref, lambda_init):
    lq = lq_ref[...]
    s1 = jnp.sum(lq[0:1, :] * lq[1:2, :], axis=-1, keepdims=True)
    s2 = jnp.sum(lq[2:3, :] * lq[3:4, :], axis=-1, keepdims=True)
    return jnp.exp(s1) - jnp.exp(s2) + lambda_init


def _stack_qt(qt):
    row = lax.broadcasted_iota(jnp.int32, (2 * DIFF_DK, 1), 0)
    zero = jnp.zeros_like(qt)
    return jnp.concatenate([jnp.where(row < DIFF_DK, qt, zero), jnp.where(row >= DIFF_DK, qt, zero)], axis=1)


def _diff_finish_t(acc, tq, lam, nw_col, lambda_init):
    o = acc[0:DIFF_DV, :] / acc[DIFF_DV:DIFF_DV + 1, :]
    od = o[:, :tq] - lam * o[:, tq:]
    ms = jnp.mean(od * od, axis=0, keepdims=True)
    y = od * lax.rsqrt(ms + RMS_EPS) * nw_col * (1.0 - lambda_init)
    y = jnp.concatenate([y, jnp.zeros_like(y)], axis=0)
    return jnp.transpose(y)[:, 0:DIFF_DV]


def _diff_body(qt_ref, kc_ref, vtc_ref, k_ref, vt_ref, lq_ref, nw_ref, o_ref, s_ref, *, tk, unroll, lambda_init):
    tq = qt_ref.shape[1]
    n_chunks = k_ref.shape[0] // tk
    ring = s_ref.shape[0]
    qs = _stack_qt(qt_ref[...])

    def chunk(j):
        return pl.ds(j * tk if isinstance(j, int) else pl.multiple_of(j * tk, tk), tk)

    def issue_scores(j, slot):
        s = _dot(k_ref[chunk(j), :], qs)
        s_ref[slot] = s
        return jnp.max(s, axis=0, keepdims=True)

    s_ctx = _dot(kc_ref[...], qs)
    m0 = jnp.max(s_ctx, axis=0, keepdims=True)
    cmax0 = tuple(issue_scores(min(a, n_chunks - 1), a % ring) for a in range(DIFF_AHEAD))
    acc0 = _dot(vtc_ref[...], jnp.exp2(s_ctx - m0).astype(BF16))

    def group(g, carry, tail=False):
        cmax, m, acc = carry
        for u in range(unroll):
            j = g * unroll + u
            if tail and j + DIFF_AHEAD >= n_chunks:
                c_new = cmax[0]
            else:
                c_new = issue_scores(j + DIFF_AHEAD, (u + DIFF_AHEAD) % ring)
            m_new = jnp.maximum(m, cmax[0])
            p = jnp.exp2(s_ref[u % ring] - m_new).astype(BF16)
            vt = vt_ref[:, chunk(j)]
            acc = jnp.exp2(m - m_new) * acc + _dot(vt, p)
            m, cmax = m_new, cmax[1:] + (c_new,)
        return cmax, m, acc

    n_groups = n_chunks // unroll
    carry = lax.fori_loop(0, n_groups - 1, group, (cmax0, m0, acc0))
    _, _, acc = group(n_groups - 1, carry, tail=True)
    o_ref[...] = _diff_finish_t(acc, tq, _diff_lambda(lq_ref, lambda_init), nw_ref[...], lambda_init)


def _diff(dqt, dk, dvt, lq, norm_w, *, batch, seq, ctx, lambda_init):
    h, r, _ = dk.shape
    tq = DIFF_TQ
    nq = seq // tq
    tk = min(DIFF_TK, seq)
    unroll = min(DIFF_UNROLL, seq // tk)
    assert (seq // tk) % unroll == 0 and unroll % DIFF_RING == 0 and DIFF_RING > DIFF_AHEAD
    ctx_blk0 = batch * seq // ctx
    return pl.pallas_call(
        functools.partial(_diff_body, tk=tk, unroll=unroll, lambda_init=lambda_init),
        grid=(batch, h, nq),
        in_specs=[pl.BlockSpec((None, 64, tq), lambda b, hh, i: (hh, 0, b * nq + i)),
                  pl.BlockSpec((None, ctx, 64), lambda b, hh, i: (hh, ctx_blk0 + b, 0)),
                  pl.BlockSpec((None, DIFF_VROWS, ctx), lambda b, hh, i: (hh, 0, ctx_blk0 + b)),
                  pl.BlockSpec((None, seq, 64), lambda b, hh, i: (hh, b, 0)),
                  pl.BlockSpec((None, DIFF_VROWS, seq), lambda b, hh, i: (hh, 0, b)),
                  _resident(lq.shape),
                  _resident((DIFF_DV, 1))],
        out_specs=pl.BlockSpec((None, tq, 64), lambda b, hh, i: (hh, b * nq + i, 0)),
        out_shape=jax.ShapeDtypeStruct((h, r, 64), F32),
        scratch_shapes=[pltpu.VMEM((DIFF_RING, tk, 2 * tq), F32)],
        compiler_params=_params("parallel", "parallel", "arbitrary"),
        name="diff",
    )(dqt, dk, dvt, dk, dvt, lq, norm_w.reshape(DIFF_DV, 1))


def _ctx_attn_body(nq_ref, nk_ref, nv_ref, dqt_ref, dk_ref, dvt_ref, lq_ref, nw_ref, na_in, df_in,
                   na_out, df_out, *, lambda_init):
    del na_in, df_in
    q = nq_ref[...]
    k = nk_ref[...]
    v = nv_ref[...]
    lane = lax.broadcasted_iota(jnp.int32, (1, NA_HEADS * NA_DIM), 1) // NA_DIM
    out = jnp.zeros(na_out.shape, F32)
    for hh in range(NA_HEADS):
        mh = lane == hh
        s = _dot_nt(jnp.where(mh, q, jnp.zeros_like(q)), k)
        p = jnp.exp(s - jnp.max(s, axis=-1, keepdims=True))
        o = _dot(p.astype(BF16), v) / jnp.sum(p, axis=-1, keepdims=True)
        out = out + jnp.where(mh, o, 0.0)
    na_out[...] = out

    lam = _diff_lambda(lq_ref, lambda_init)
    tq = dqt_ref.shape[2]
    for hh in range(DIFF_HEADS):
        s = _dot(dk_ref[hh], _stack_qt(dqt_ref[hh]))
        p = jnp.exp2(s - jnp.max(s, axis=0, keepdims=True))
        acc = _dot(dvt_ref[hh], p.astype(BF16))
        df_out[hh] = _diff_finish_t(acc, tq, lam, nw_ref[...], lambda_init)


def _ctx_attn(nq, nk, nv, dqt, dk, dvt, lq, norm_w, na_o, df_o, *, batch, seq, ctx, lambda_init):
    blk0 = batch * seq // ctx
    s256 = pl.BlockSpec((ctx, 256), lambda b: (blk0 + b, 0))
    sh64 = pl.BlockSpec((DIFF_HEADS, ctx, 64), lambda b: (0, blk0 + b, 0))
    sq_t = pl.BlockSpec((DIFF_HEADS, 64, ctx), lambda b: (0, 0, blk0 + b))
    sv_t = pl.BlockSpec((DIFF_HEADS, DIFF_VROWS, ctx), lambda b: (0, 0, blk0 + b))
    return pl.pallas_call(
        functools.partial(_ctx_attn_body, lambda_init=lambda_init),
        grid=(batch,),
        in_specs=[s256, s256, s256, sq_t, sh64, sv_t, _resident(lq.shape), _resident((DIFF_DV, 1)),
                  pl.BlockSpec(memory_space=pl.ANY), pl.BlockSpec(memory_space=pl.ANY)],
        out_specs=[s256, sh64],
        out_shape=[jax.ShapeDtypeStruct(na_o.shape, F32), jax.ShapeDtypeStruct(df_o.shape, F32)],
        input_output_aliases={8: 0, 9: 1},
        compiler_params=_params("arbitrary"),
        name="ctx_attn",
    )(nq, nk, nv, dqt, dk, dvt, lq, norm_w.reshape(DIFF_DV, 1), na_o, df_o)


def _conv_body(u_ref, prev_ref, next_ref, dw_ref, dwb_ref, lng_ref, lnb_ref, pw_ref, pwb_ref,
               o_ref, pad_ref, sh_ref, *, tiles_per_seq, n_latent_tiles):
    t = u_ref.shape[0]
    i = pl.program_id(0)
    in_ctx = i >= n_latent_tiles
    first = in_ctx | (i % tiles_per_seq == 0)
    last = in_ctx | (i % tiles_per_seq == tiles_per_seq - 1)
    pad_ref[0:HALO, :] = jnp.where(first, 0.0, prev_ref[...])
    pad_ref[HALO:HALO + t, :] = u_ref[...]
    pad_ref[HALO + t:HALO + t + HALO, :] = jnp.where(last, 0.0, next_ref[...])
    base = HALO - CONV_K // 2
    first = {}
    for phase in range(8):
        taps = [k for k in range(CONV_K) if (base + k) % 8 == phase]
        first[phase] = base + taps[0]
        span = base + taps[-1] + t - first[phase]
        sh_ref[phase, 0:span, :] = pad_ref[first[phase]:first[phase] + span, :]
    sub = 64
    parts = []
    for r0 in range(0, t, sub):
        acc = jnp.zeros((sub, CONV_CH), F32)
        for k in range(CONV_K):
            phase = (base + k) % 8
            off = base + k - first[phase] + r0
            acc = acc + dw_ref[k:k + 1, :] * sh_ref[phase, off:off + sub, :]
        parts.append(acc)
    y = jnp.concatenate(parts, axis=0) + dwb_ref[...]
    mu = jnp.mean(y, axis=-1, keepdims=True)
    yc = y - mu
    var = jnp.mean(yc * yc, axis=-1, keepdims=True)
    y = _silu(yc * lax.rsqrt(var + LN_EPS) * lng_ref[...] + lnb_ref[...])
    o_ref[...] = _dot(y.astype(BF16), pw_ref[...]) + pwb_ref[...]


def _conv(cu, dw, dw_b, ln_g, ln_b, pw, pw_b, *, batch, seq, ctx):
    r = cu.shape[0]
    t = CONV_TILE
    assert ctx == t and seq % t == 0
    n_tiles = r // t
    hb = t // HALO
    vec = lambda a: a.reshape(1, CONV_CH)
    return pl.pallas_call(
        functools.partial(_conv_body, tiles_per_seq=seq // t, n_latent_tiles=batch * seq // t),
        grid=(n_tiles,),
        in_specs=[pl.BlockSpec((t, CONV_CH), lambda i: (i, 0)),
                  pl.BlockSpec((HALO, CONV_CH), lambda i: (jnp.maximum(i * hb - 1, 0), 0)),
                  pl.BlockSpec((HALO, CONV_CH), lambda i: (jnp.minimum((i + 1) * hb, n_tiles * hb - 1), 0)),
                  _resident((CONV_K, CONV_CH))] + [_resident((1, CONV_CH))] * 3
                 + [_resident((CONV_CH, CONV_CH)), _resident((1, CONV_CH))],
        out_specs=pl.BlockSpec((t, CONV_CH), lambda i: (i, 0)),
        out_shape=jax.ShapeDtypeStruct((r, CONV_CH), F32),
        scratch_shapes=[pltpu.VMEM((t + 2 * HALO, CONV_CH), F32),
                        pltpu.VMEM((8, t + 2 * HALO, CONV_CH), F32)],
        compiler_params=_params("parallel"),
        name="conv",
    )(cu, cu, cu, dw, vec(dw_b), vec(ln_g), vec(ln_b), pw.astype(BF16), vec(pw_b))


def _wout_body(h_ref, mod_ref, gx_ref, nx_ref, dx_ref, cx_ref, w_ref, o_ref):
    mix = jnp.concatenate([gx_ref[...], nx_ref[...]] + [dx_ref[hh] for hh in range(DIFF_HEADS)]
                          + [cx_ref[...]], axis=-1).astype(BF16)
    o_ref[...] = h_ref[...] + mod_ref[5:6, :] * _dot(mix, w_ref[...])


def _wout(h, mods, gx, nx, dx, cx, w_out, *, n_tiles, group_of):
    r, d = h.shape
    tm = ROW_TILE
    row = lambda i: (i, 0)
    return pl.pallas_call(
        _wout_body,
        grid=(n_tiles,),
        in_specs=[pl.BlockSpec((tm, d), row),
                  pl.BlockSpec((None, N_MOD, d), lambda i: (group_of(i), 0, 0)),
                  pl.BlockSpec((tm, 256), row), pl.BlockSpec((tm, 256), row),
                  pl.BlockSpec((DIFF_HEADS, tm, 64), lambda i: (0, i, 0)),
                  pl.BlockSpec((tm, 256), row),
                  _resident(w_out.shape)],
        out_specs=pl.BlockSpec((tm, d), row),
        out_shape=jax.ShapeDtypeStruct((n_tiles * tm, d), F32),
        compiler_params=_params("parallel"),
        name="wout",
    )(h, mods, gx, nx, dx, cx, w_out.astype(BF16))


def kernel(x, c, ctx, c_ctx, ada_w, ada_b, norm_ffn1, ffn1_w13, ffn1_w2, norm_mix, w_in, gla_wa_f, gla_ba_f, gla_wa_b, gla_ba_b, gla_norm, na_rpb, diff_lq1, diff_lk1, diff_lq2, diff_lk2, diff_norm, conv_dw, conv_dw_b, conv_ln_g, conv_ln_b, conv_pw, conv_pw_b, w_out, norm_ffn2, ffn2_w13, ffn2_w2, final_norm):
    batch, seq, d = x.shape
    n_ctx = ctx.shape[1]
    depth = ada_w.shape[0]
    rows = seq // GRID_W
    tm = ROW_TILE
    assert seq % tm == 0 and (batch * n_ctx) % tm == 0 and batch + 1 <= 8
    lat_tiles = batch * seq // tm
    all_tiles = lat_tiles + batch * n_ctx // tm
    tiles_per_batch = seq // tm

    def group_of(i):
        return jnp.minimum(i // tiles_per_batch, batch)

    def pos_of(i):
        return jnp.where(i < lat_tiles, i % tiles_per_batch, tiles_per_batch)

    c_rows = jnp.concatenate([c, c_ctx[None, :], jnp.zeros((8 - batch - 1, d), F32)], axis=0)
    mods_all = _ada(c_rows, ada_w, ada_b)[:, :batch + 1].reshape(depth, batch + 1, N_MOD, d)
    rope = _rope_tables(seq, tm)
    h = x.reshape(batch * seq, d)
    h_ctx = ctx.reshape(batch * n_ctx, d)

    for i in range(depth):
        last = i == depth - 1
        lambda_init = 0.8 - 0.6 * math.exp(-0.3 * i)
        mods = mods_all[i]
        tok = dict(n_tiles=all_tiles, group_of=group_of)
        geo = dict(batch=batch, seq=seq, ctx=n_ctx)

        h = _ffn(h, h_ctx if i == 0 else None, mods, norm_ffn1[i], ffn1_w13[i], ffn1_w2[i], final_norm,
                 k0=0, final=False, **tok)

        wa, ba = _gate_weights(gla_wa_f[i], gla_ba_f[i], gla_wa_b[i], gla_ba_b[i])
        (gqk, gv, gvt, gg, glg, nq, nk, nv, dq, dk, dv, cu) = _proj(
            h, mods, norm_mix[i], _permute_w_in(w_in[i]), wa, ba, rope, pos_of=pos_of, **tok)

        gnorm = jnp.tile(gla_norm[i], GLA_HEADS).reshape(1, GLA_HEADS * GLA_DV)
        o_f = _gla(gqk, gv, gvt, glg, gg, gg, gnorm, reverse=False, **geo)
        gx = _gla(gqk, gv, gvt, glg, gg, o_f, gnorm, reverse=True, **geo)

        nx = _na(nq, nk, nv, _na_col_tables(na_rpb[i]), **geo)

        lq = jnp.stack([diff_lq1[i], diff_lk1[i], diff_lq2[i], diff_lk2[i]])
        dx = _diff(dq, dk, dv, lq, diff_norm[i], lambda_init=lambda_init, **geo)
        if not last:
            nx, dx = _ctx_attn(nq, nk, nv, dq, dk, dv, lq, diff_norm[i], nx, dx,
                               lambda_init=lambda_init, **geo)

        cx = _conv(cu, conv_dw[i], conv_dw_b[i], conv_ln_g[i], conv_ln_b[i], conv_pw[i], conv_pw_b[i], **geo)

        if last:
            tok = dict(n_tiles=lat_tiles, group_of=group_of)
        h = _wout(h, mods, gx, nx, dx, cx, w_out[i], **tok)

        h = _ffn(h, None, mods, norm_ffn2[i], ffn2_w13[i], ffn2_w2[i], final_norm, k0=6, final=last, **tok)

    return h.reshape(batch, seq, d)
```

```python
import functools
import math

import jax
import jax.numpy as jnp
from jax import lax
from jax.experimental import pallas as pl
from jax.experimental.pallas import tpu as pltpu

F32 = jnp.float32
BF16 = jnp.bfloat16
F8 = jnp.float8_e4m3fn

GRID_W = 64
N_MOD = 9
RMS_EPS = 1e-6
LN_EPS = 1e-5
NEG_INF = -1e30
GLA_HEADS, GLA_DK, GLA_DV, GLA_RANK, GLA_TAU, GLA_CHUNK = 4, 32, 64, 16, 16.0, 64
NA_HEADS, NA_DIM, NA_ROWS, NA_COLS = 4, 64, 8, 16
DIFF_HEADS, DIFF_DK, DIFF_DV = 4, 32, 64
DIFF_KW = 256
DIFF_VROWS = 128
CONV_CH, CONV_K = 256, 31
ROPE_BASE = 10000.0
LOG2E = 1.4426950408889634

LANES = 128
VMEM_LIMIT = 56 * 1024 * 1024

ROW_TILE = 512
FF_CHUNK = 256
GLA_BLOCK = 256
NA_QROWS = 8
NA_KBLK = 256
CONV_TILE = 256
DIFF_TQ = 256
DIFF_TK = 512
DIFF_UNROLL = 8
DIFF_AHEAD = 3
DIFF_RING = 4
HALO = 16

C_GLA, C_NA, C_DIFF, C_CONV, C_AUX, C_END = 0, 768, 1536, 2304, 2816, 2944


def _dot(a, b):
    return jnp.dot(a, b, preferred_element_type=F32)


def _dot_nt(a, b):
    return lax.dot_general(a, b, (((1,), (1,)), ((), ())), preferred_element_type=F32)


def _params(*sem):
    return pltpu.CompilerParams(dimension_semantics=sem, vmem_limit_bytes=VMEM_LIMIT)


def _resident(shape):
    nd = len(shape)
    return pl.BlockSpec(shape, lambda *_: (0,) * nd, pipeline_mode=pl.Buffered(1))


def _silu(x):
    return x * jax.nn.sigmoid(x)


def _rms(x, w):
    return x * lax.rsqrt(jnp.mean(x * x, axis=-1, keepdims=True) + RMS_EPS) * w


def _ada_body(c_ref, w_ref, b_ref, o_ref):
    s = _silu(c_ref[...])
    o_ref[...] = jnp.dot(s, w_ref[...], precision=lax.Precision.HIGHEST,
                         preferred_element_type=F32) + b_ref[...]


def _ada(c_rows, ada_w, ada_b):
    depth, d, _ = ada_w.shape
    return pl.pallas_call(
        _ada_body,
        grid=(depth, N_MOD),
        in_specs=[pl.BlockSpec((8, d), lambda l, n: (0, 0)),
                  pl.BlockSpec((None, d, d), lambda l, n: (l, 0, n)),
                  pl.BlockSpec((None, 1, d), lambda l, n: (l, 0, n))],
        out_specs=pl.BlockSpec((None, 8, d), lambda l, n: (l, 0, n)),
        out_shape=jax.ShapeDtypeStruct((depth, 8, N_MOD * d), F32),
        compiler_params=_params("arbitrary", "arbitrary"),
        name="ada",
    )(c_rows, ada_w, ada_b.reshape(depth, 1, N_MOD * d))


def _ffn_body(h_ref, hc_ref, mod_ref, nw_ref, w13_ref, w2_ref, fw_ref, o_ref, xb_ref, acc_ref, h13_ref,
              *, k0, final, n_first):
    n_chunks, tf, _ = w2_ref.shape
    ff = n_chunks * tf
    x = h_ref[...]
    if n_first is not None:
        x = jnp.where(pl.program_id(0) < n_first, x, hc_ref[...])
    xm = _rms(x, nw_ref[...]) * (1.0 + mod_ref[k0 + 1:k0 + 2, :]) + mod_ref[k0:k0 + 1, :]
    xb_ref[...] = xm.astype(BF16)
    acc_ref[...] = jnp.zeros_like(acc_ref)

    def up(c, slot):
        col = c * tf if isinstance(c, int) else pl.multiple_of(c * tf, tf)
        h13_ref[slot, :, :tf] = _dot(xb_ref[...], w13_ref[:, pl.ds(col, tf)])
        h13_ref[slot, :, tf:] = _dot(xb_ref[...], w13_ref[:, pl.ds(ff + col, tf)])

    def down(c, slot):
        a = h13_ref[slot, :, :tf]
        u = h13_ref[slot, :, tf:]
        acc_ref[...] += _dot((_silu(a) * u).astype(BF16), w2_ref[c])

    def pair(t, carry):
        up(2 * t + 1, 1)
        down(2 * t, 0)
        up(2 * t + 2, 0)
        down(2 * t + 1, 1)
        return carry

    up(0, 0)
    lax.fori_loop(0, (n_chunks - 1) // 2, pair, 0)
    down(n_chunks - 1, 0)
    out = x + (0.5 * mod_ref[k0 + 2:k0 + 3, :]) * acc_ref[...]
    if final:
        out = _rms(out, fw_ref[...])
    o_ref[...] = out


def _ffn(h, h_ctx, mods, norm_w, w13, w2, final_w, *, k0, n_tiles, group_of, final):
    d = h.shape[1]
    tm = ROW_TILE
    ff = w2.shape[0]
    n_chunks = ff // FF_CHUNK
    assert ff % FF_CHUNK == 0 and n_chunks % 2 == 1
    if h_ctx is None:
        n_first, h_ctx = None, norm_w.reshape(1, d)
        row0 = lambda i: (i, 0)
        ctx_spec = _resident((1, d))
    else:
        n_first = h.shape[0] // tm
        row0 = lambda i: (jnp.minimum(i, n_first - 1), 0)
        ctx_spec = pl.BlockSpec((tm, d), lambda i: (jnp.maximum(i - n_first, 0), 0))
    return pl.pallas_call(
        functools.partial(_ffn_body, k0=k0, final=final, n_first=n_first),
        grid=(n_tiles,),
        in_specs=[pl.BlockSpec((tm, d), row0),
                  ctx_spec,
                  pl.BlockSpec((None, N_MOD, d), lambda i: (group_of(i), 0, 0)),
                  _resident((1, d)),
                  _resident(w13.shape),
                  _resident((n_chunks, FF_CHUNK, d)),
                  _resident((1, d))],
        out_specs=pl.BlockSpec((tm, d), lambda i: (i, 0)),
        out_shape=jax.ShapeDtypeStruct((n_tiles * tm, d), F32),
        scratch_shapes=[pltpu.VMEM((tm, d), BF16), pltpu.VMEM((tm, d), F32),
                        pltpu.VMEM((2, tm, 2 * FF_CHUNK), F32)],
        compiler_params=_params("parallel"),
        name="ffn",
    )(h, h_ctx, mods, norm_w.reshape(1, d), w13.astype(BF16),
      w2.astype(BF16).reshape(n_chunks, FF_CHUNK, d), final_w.reshape(1, d))


def _log_sigmoid(x):
    return jnp.minimum(x, 0.0) - jnp.log1p(jnp.exp(-jnp.abs(x)))


def _rope_rotate(x):
    n = x.shape[-1]
    lane = lax.broadcasted_iota(jnp.int32, x.shape, 1)
    up = pltpu.roll(x, n - 8, 1)
    dn = pltpu.roll(x, 8, 1)
    return jnp.where((lane & 15) < 8, -up, dn)


def _rope_rotate_rows(x):
    n = x.shape[0]
    row = lax.broadcasted_iota(jnp.int32, x.shape, 0)
    up = pltpu.roll(x, n - 8, 0)
    dn = pltpu.roll(x, 8, 0)
    return jnp.where((row & 15) < 8, -up, dn)


def _split_f8(x):
    hi = x.astype(F8)
    return hi, (x - hi.astype(F32)).astype(F8)


def _proj_body(h_ref, mod_ref, nw_ref, w_ref, wvt_ref, wdqt_ref, wdvt_ref, wa_ref, ba_ref,
               cos_ref, sin_ref, cost_ref, sint_ref, pk_ref,
               gqk_ref, gv_ref, gvt_ref, gg_ref, glg_ref, nq_ref, nk_ref, nv_ref,
               dqt_ref, dk_ref, dvt_ref, cu_ref, xb_ref):
    x = h_ref[...]
    tm = x.shape[0]
    xm = _rms(x, nw_ref[...]) * (1.0 + mod_ref[4:5, :]) + mod_ref[3:4, :]
    xb_ref[...] = xm.astype(BF16)

    z = _dot(xb_ref[...], w_ref[:, C_GLA:C_GLA + 256])
    lane = lax.broadcasted_iota(jnp.int32, (1, 256), 1)
    gqk_ref[...] = z * jnp.where(lane < 128, GLA_DK ** -0.5, 1.0)
    gv_ref[...] = _dot(xb_ref[...], w_ref[:, C_GLA + 256:C_GLA + 512])
    gvt_ref[...] = _dot_nt(wvt_ref[...], xb_ref[...])
    gg_ref[...] = _dot(xb_ref[...], w_ref[:, C_GLA + 512:C_GLA + 768])
    aux = _dot(xb_ref[...], w_ref[:, C_AUX:C_END])
    pre = _dot(aux.astype(BF16), wa_ref[...]) + ba_ref[...]
    glg_ref[...] = _log_sigmoid(pre) * (1.0 / GLA_TAU)

    nq_ref[...] = (_dot(xb_ref[...], w_ref[:, C_NA:C_NA + 256]) * (NA_DIM ** -0.5)).astype(BF16)
    nk_ref[...] = _dot(xb_ref[...], w_ref[:, C_NA + 256:C_NA + 512]).astype(BF16)
    nv_ref[...] = _dot(xb_ref[...], w_ref[:, C_NA + 512:C_NA + 768]).astype(BF16)

    cos = cos_ref[...]
    sin = sin_ref[...]
    cos2 = jnp.concatenate([cos, cos], axis=1)
    sin2 = jnp.concatenate([sin, sin], axis=1)
    zk = _dot(xb_ref[...], w_ref[:, C_DIFF + 256:C_DIFF + 512])
    zk = zk * cos2 + _rope_rotate(zk) * sin2
    k_hi, k_lo = _split_f8(zk.astype(BF16).astype(F32))
    k_parts = jnp.concatenate([k_hi.astype(BF16), k_lo.astype(BF16)], axis=1)
    n_rep = 2 * DIFF_HEADS
    cos_t = jnp.concatenate([cost_ref[...]] * n_rep, axis=0)
    sin_t = jnp.concatenate([sint_ref[...]] * n_rep, axis=0)
    zqt = _dot_nt(wdqt_ref[...], xb_ref[...])
    zqt = (zqt * cos_t + _rope_rotate_rows(zqt) * sin_t) * (DIFF_DK ** -0.5 * LOG2E)
    zvt = _dot_nt(wdvt_ref[...], xb_ref[...])
    pad_rows = DIFF_VROWS - DIFF_DV
    one_row = jnp.where(lax.broadcasted_iota(jnp.int32, (pad_rows, tm), 0) == 0, 1.0, 0.0).astype(BF16)
    for hh in range(DIFF_HEADS):
        sl = slice(64 * hh, 64 * hh + 64)
        dk_ref[hh] = _dot(k_parts, pk_ref[hh]).astype(F8)
        dqt_ref[hh] = zqt[sl, :].astype(BF16)
        dvt_ref[hh, 0:64, :] = zvt[sl, :].astype(BF16)
        dvt_ref[hh, DIFF_DV:DIFF_VROWS, :] = one_row

    za = _dot(xb_ref[...], w_ref[:, C_CONV:C_CONV + 256])
    zg = _dot(xb_ref[...], w_ref[:, C_CONV + 256:C_CONV + 512])
    cu_ref[...] = za * jax.nn.sigmoid(zg)


def _proj(h, mods, norm_w, w_p, wa, ba, rope, *, n_tiles, group_of, pos_of):
    r, d = h.shape
    tm = ROW_TILE
    row = lambda i: (i, 0)
    hrow = lambda i: (0, i, 0)
    hcol = lambda i: (0, 0, i)
    f32_256 = jax.ShapeDtypeStruct((r, 256), F32)
    bf_256 = jax.ShapeDtypeStruct((r, 256), BF16)
    wvt = w_p[:, C_GLA + 256:C_GLA + 512].T
    wdqt = w_p[:, C_DIFF:C_DIFF + 256].T
    wdvt = w_p[:, C_DIFF + 512:C_DIFF + 768].T
    cos_r, sin_r, cos_c, sin_c = rope
    pk = _key_placement()
    out_shape = [f32_256, f32_256, jax.ShapeDtypeStruct((256, r), F32), f32_256, f32_256,
                 bf_256, bf_256, bf_256,
                 jax.ShapeDtypeStruct((DIFF_HEADS, 64, r), BF16),
                 jax.ShapeDtypeStruct((DIFF_HEADS, r, DIFF_KW), F8),
                 jax.ShapeDtypeStruct((DIFF_HEADS, DIFF_VROWS, r), BF16),
                 f32_256]
    out_specs = [pl.BlockSpec((tm, 256), row)] * 2 + [pl.BlockSpec((256, tm), lambda i: (0, i))] + [
        pl.BlockSpec((tm, 256), row)] * 5 + [
        pl.BlockSpec((DIFF_HEADS, 64, tm), hcol),
        pl.BlockSpec((DIFF_HEADS, tm, DIFF_KW), hrow),
        pl.BlockSpec((DIFF_HEADS, DIFF_VROWS, tm), hcol),
        pl.BlockSpec((tm, 256), row)]
    return pl.pallas_call(
        _proj_body,
        grid=(n_tiles,),
        in_specs=[pl.BlockSpec((tm, d), row),
                  pl.BlockSpec((None, N_MOD, d), lambda i: (group_of(i), 0, 0)),
                  _resident((1, d)),
                  _resident(w_p.shape),
                  _resident(wvt.shape),
                  _resident(wdqt.shape),
                  _resident(wdvt.shape),
                  _resident(wa.shape),
                  _resident(ba.shape),
                  pl.BlockSpec((tm, LANES), lambda i: (pos_of(i), 0)),
                  pl.BlockSpec((tm, LANES), lambda i: (pos_of(i), 0)),
                  pl.BlockSpec((DIFF_DK, tm), lambda i: (0, pos_of(i))),
                  pl.BlockSpec((DIFF_DK, tm), lambda i: (0, pos_of(i))),
                  _resident(pk.shape)],
        out_specs=out_specs,
        out_shape=out_shape,
        scratch_shapes=[pltpu.VMEM((tm, d), BF16)],
        compiler_params=_params("parallel"),
        name="proj",
    )(h, mods, norm_w.reshape(1, d), w_p, wvt, wdqt, wdvt, wa, ba, cos_r, sin_r, cos_c, sin_c, pk)


def _key_placement():
    n = 2 * DIFF_DK
    src = jnp.arange(2 * DIFF_HEADS * n)[None, :, None]
    dst = jnp.arange(DIFF_KW)[None, None, :]
    head = jnp.arange(DIFF_HEADS)[:, None, None]
    want = ((dst // n) % 2) * (DIFF_HEADS * n) + head * n + dst % n
    return (src == want).astype(BF16)


def _permute_w_in(w_in):
    d = w_in.shape[0]
    g0 = 2 * GLA_HEADS * GLA_DK + 2 * GLA_HEADS * GLA_DV
    aux = w_in[:, g0:g0 + 2 * GLA_RANK]
    rest = w_in[:, g0 + 2 * GLA_RANK:]
    pad = jnp.zeros((d, C_END - C_AUX - 2 * GLA_RANK), w_in.dtype)
    return jnp.concatenate([w_in[:, :g0], rest, aux, pad], axis=1).astype(BF16)


def _gate_weights(wa_f, ba_f, wa_b, ba_b):
    n = GLA_HEADS * GLA_DK
    wa = jnp.zeros((C_END - C_AUX, 2 * n), F32)
    wa = wa.at[:GLA_RANK, :n].set(wa_f).at[GLA_RANK:2 * GLA_RANK, n:].set(wa_b)
    return wa.astype(BF16), jnp.concatenate([ba_f, ba_b]).reshape(1, 2 * n)


def _rope_tables(seq, tile):
    t = jnp.arange(seq)
    row = (t // GRID_W).astype(F32)
    col = (t % GRID_W).astype(F32)
    half = DIFF_DK // 2
    inv = 1.0 / (ROPE_BASE ** (jnp.arange(0, half, 2, dtype=F32) / half))
    ang_r = row[:, None] * inv
    ang_c = col[:, None] * inv
    ang = jnp.concatenate([ang_r, ang_r, ang_c, ang_c], axis=-1)
    cos = jnp.concatenate([jnp.cos(ang), jnp.ones((tile, DIFF_DK), F32)], axis=0)
    sin = jnp.concatenate([jnp.sin(ang), jnp.zeros((tile, DIFF_DK), F32)], axis=0)
    rep = (1, LANES // DIFF_DK)
    return jnp.tile(cos, rep), jnp.tile(sin, rep), cos.T, sin.T


def _split3(x):
    hi = x.astype(BF16)
    r1 = x - hi.astype(F32)
    mid = r1.astype(BF16)
    lo = (r1 - mid.astype(F32)).astype(BF16)
    return hi, mid, lo


def _gla_body(qk_ref, v_ref, vt_ref, lg_ref, g_ref, of_ref, nw_ref, o_ref, st_ref, *, reverse):
    blk = qk_ref.shape[0]
    n_chunks = blk // GLA_CHUNK
    nqk = GLA_HEADS * GLA_DK
    nv = GLA_HEADS * GLA_DV

    @pl.when(pl.program_id(1) == 0)
    def _():
        st_ref[...] = jnp.zeros_like(st_ref)

    ri = lax.broadcasted_iota(jnp.int32, (blk, blk), 0)
    ci = lax.broadcasted_iota(jnp.int32, (blk, blk), 1)
    same = (ri // GLA_CHUNK) == (ci // GLA_CHUNK)
    tri = jnp.where(same & ((ci >= ri) if reverse else (ci <= ri)), 1.0, 0.0).astype(BF16)
    lg = lg_ref[:, nqk:2 * nqk] if reverse else lg_ref[:, 0:nqk]
    hi, mid, lo = _split3(lg)
    bcum_all = _dot(tri, hi) + _dot(tri, mid) + _dot(tri, lo)

    c = GLA_CHUNK
    rk = lax.broadcasted_iota(jnp.int32, (GLA_HEADS * c, nqk), 0) // c
    ck = lax.broadcasted_iota(jnp.int32, (GLA_HEADS * c, nqk), 1) // GLA_DK
    mask_k = rk == ck
    rv = lax.broadcasted_iota(jnp.int32, (GLA_HEADS * c, nv), 0) // c
    cv = lax.broadcasted_iota(jnp.int32, (GLA_HEADS * c, nv), 1) // GLA_DV
    mask_v = rv == cv
    rs = lax.broadcasted_iota(jnp.int32, (nv, nqk), 0) // GLA_DV
    cs = lax.broadcasted_iota(jnp.int32, (nv, nqk), 1) // GLA_DK
    mask_s = rs == cs
    ai = lax.broadcasted_iota(jnp.int32, (c, GLA_HEADS * c), 0)
    aj = lax.broadcasted_iota(jnp.int32, (c, GLA_HEADS * c), 1) % c
    mask_a = (aj >= ai) if reverse else (aj <= ai)

    vt = vt_ref[...].astype(BF16)
    order = range(n_chunks - 1, -1, -1) if reverse else range(n_chunks)
    q_in, a_raw, u_t, decay = {}, {}, {}, {}
    for ch in order:
        rows = slice(ch * c, (ch + 1) * c)
        k = qk_ref[rows, nqk:2 * nqk]
        bcum = bcum_all[rows, :]
        btot = bcum[0:1, :] if reverse else bcum[c - 1:c, :]
        q_in[ch] = (qk_ref[rows, 0:nqk] * jnp.exp(bcum)).astype(BF16)
        k_in = k * jnp.exp(-bcum)
        k_out = (k * jnp.exp(btot - bcum)).astype(BF16)
        k_bd = jnp.where(mask_k, jnp.concatenate([k_in] * GLA_HEADS, axis=0), 0.0).astype(BF16)
        a_raw[ch] = _dot_nt(q_in[ch], k_bd)
        k_pad = jnp.concatenate([jnp.zeros((n * c, nqk), BF16) for n in (ch,) if n] + [k_out]
                                + [jnp.zeros((n * c, nqk), BF16) for n in (n_chunks - 1 - ch,) if n], axis=0)
        u_t[ch] = jnp.where(mask_s, _dot(vt, k_pad), 0.0)
        decay[ch] = jnp.exp(btot)
    o = {}
    for ch in order:
        v = v_ref[ch * c:(ch + 1) * c, :]
        v_bd = jnp.where(mask_v, jnp.concatenate([v] * GLA_HEADS, axis=0), 0.0).astype(BF16)
        o[ch] = _dot(jnp.where(mask_a, a_raw[ch], 0.0).astype(BF16), v_bd)
    st = st_ref[...]
    for ch in order:
        o[ch] = o[ch] + _dot_nt(q_in[ch], st.astype(BF16))
        st = st * decay[ch] + u_t[ch]
    st_ref[...] = st
    o_blk = jnp.concatenate([o[ch] for ch in range(n_chunks)], axis=0)
    if reverse:
        o_blk = o_blk + of_ref[...]
        hi2, lo2, _ = _split3(o_blk * o_blk)
        hr = lax.broadcasted_iota(jnp.int32, (nv, nv), 0) // GLA_DV
        hc = lax.broadcasted_iota(jnp.int32, (nv, nv), 1) // GLA_DV
        seg = jnp.where(hr == hc, 1.0, 0.0).astype(BF16)
        ms = (_dot(hi2, seg) + _dot(lo2, seg)) * (1.0 / GLA_DV)
        o_blk = o_blk * lax.rsqrt(ms + RMS_EPS) * nw_ref[...] * _silu(g_ref[...])
    o_ref[...] = o_blk


def _gla(gqk, gv, gvt, glg, gg, o_f, norm_w4, *, reverse, batch, seq, ctx):
    r = gqk.shape[0]
    blk = GLA_BLOCK
    nc, nl = ctx // blk, seq // blk
    ctx_base = batch * seq // blk

    def blk_of(b, s):
        if reverse:
            return jnp.where(s < nc, ctx_base + b * nc + (nc - 1 - s), b * nl + (nl - 1 - (s - nc)))
        return jnp.where(s < nc, ctx_base + b * nc + s, b * nl + (s - nc))

    spec = pl.BlockSpec((blk, 256), lambda b, s: (blk_of(b, s), 0))
    spec_t = pl.BlockSpec((256, blk), lambda b, s: (0, blk_of(b, s)))
    return pl.pallas_call(
        functools.partial(_gla_body, reverse=reverse),
        grid=(batch, nc + nl),
        in_specs=[spec, spec, spec_t, spec, spec, spec, _resident((1, 256))],
        out_specs=spec,
        out_shape=jax.ShapeDtypeStruct((r, 256), F32),
        scratch_shapes=[pltpu.VMEM((GLA_HEADS * GLA_DV, GLA_HEADS * GLA_DK), F32)],
        compiler_params=_params("arbitrary", "arbitrary"),
        name="gla_bwd" if reverse else "gla_fwd",
    )(gqk, gv, gvt, glg, gg, o_f, norm_w4)


def _na_col_tables(rpb):
    col = jnp.arange(GRID_W)
    cstart = jnp.clip(col - NA_COLS // 2, 0, GRID_W - NA_COLS)
    col_ok = (col[None, :] >= cstart[:, None]) & (col[None, :] < cstart[:, None] + NA_COLS)
    col_off = jnp.clip(col[None, :] - col[:, None] + (NA_COLS - 1), 0, 2 * NA_COLS - 2)
    t = jnp.where(col_ok[None, None], rpb[:, :, col_off], NEG_INF)
    dead = jnp.full((NA_HEADS, 1, GRID_W, GRID_W), NEG_INF, F32)
    t = jnp.concatenate([dead, t.astype(F32), dead], axis=1)
    return jnp.concatenate([t[:, :-1], t[:, 1:]], axis=-1)


def _na_body(q_ref, k0, k1, k2, k3, v0, v1, v2, v3, kc_ref, vc_ref, tab_ref, o_ref, *, rows, key_blocks):
    q = q_ref[...]
    kw = jnp.concatenate([k0[...], k1[...], k2[...], k3[...]], axis=0)
    vw = jnp.concatenate([v0[...], v1[...], v2[...], v3[...]], axis=0)
    kc = kc_ref[...]
    vc = vc_ref[...]

    j = pl.program_id(1)
    rows_per_blk = NA_KBLK // GRID_W
    r0 = j * NA_QROWS
    kr0 = jnp.clip(2 * j - 1, 0, key_blocks - 4) * rows_per_blk
    lane_lo = lax.broadcasted_iota(jnp.int32, (1, 2 * GRID_W), 1) < GRID_W
    n_pairs = 4 * rows_per_blk // 2
    entry, ok = [], []
    for a in range(NA_QROWS):
        r = r0 + a
        start = jnp.clip(r - NA_ROWS // 2, 0, rows - NA_ROWS)
        for bp in range(n_pairs):
            rk = kr0 + 2 * bp
            entry.append(jnp.clip(rk - r + NA_ROWS, 0, 2 * NA_ROWS - 1))
            in0 = ((rk >= start) & (rk < start + NA_ROWS)).astype(jnp.int32)
            in1 = ((rk + 1 >= start) & (rk + 1 < start + NA_ROWS)).astype(jnp.int32)
            ok.append(jnp.where(lane_lo, in0, in1) != 0)

    def bias_of(hh):
        rows_ = []
        for a in range(NA_QROWS):
            tiles = [jnp.where(ok[a * n_pairs + bp], tab_ref[hh, entry[a * n_pairs + bp]], NEG_INF)
                     for bp in range(n_pairs)]
            rows_.append(jnp.concatenate(tiles, axis=1))
        return jnp.concatenate(rows_, axis=0)

    lane = lax.broadcasted_iota(jnp.int32, (1, NA_HEADS * NA_DIM), 1) // NA_DIM
    out = jnp.zeros(o_ref.shape, F32)
    for hh in range(NA_HEADS):
        mh = lane == hh
        qh = jnp.where(mh, q, jnp.zeros_like(q))
        s_w = _dot_nt(qh, kw) + bias_of(hh)
        s_c = _dot_nt(qh, kc)
        m = jnp.maximum(jnp.max(s_w, axis=-1, keepdims=True), jnp.max(s_c, axis=-1, keepdims=True))
        p_w = jnp.exp(s_w - m)
        p_c = jnp.exp(s_c - m)
        l = jnp.sum(p_w, axis=-1, keepdims=True) + jnp.sum(p_c, axis=-1, keepdims=True)
        o = _dot(p_w.astype(BF16), vw) + _dot(p_c.astype(BF16), vc)
        out = out + jnp.where(mh, o / l, 0.0)
    o_ref[...] = out


def _na(nq, nk, nv, tab, *, batch, seq, ctx):
    r = nq.shape[0]
    tq = NA_QROWS * GRID_W
    nj = seq // tq
    kb = seq // NA_KBLK
    assert kb >= 4 and seq % tq == 0 and seq % ctx == 0
    ctx_blk0 = batch * seq // ctx

    def kspec(i):
        return pl.BlockSpec((NA_KBLK, 256),
                            lambda b, j: (b * kb + jnp.clip(2 * j - 1, 0, kb - 4) + i, 0))

    cspec = pl.BlockSpec((ctx, 256), lambda b, j: (ctx_blk0 + b, 0))
    return pl.pallas_call(
        functools.partial(_na_body, rows=seq // GRID_W, key_blocks=kb),
        grid=(batch, nj),
        in_specs=[pl.BlockSpec((tq, 256), lambda b, j: (b * nj + j, 0))]
                 + [kspec(i) for i in range(4)] + [kspec(i) for i in range(4)]
                 + [cspec, cspec, _resident(tab.shape)],
        out_specs=pl.BlockSpec((tq, 256), lambda b, j: (b * nj + j, 0)),
        out_shape=jax.ShapeDtypeStruct((r, 256), F32),
        compiler_params=_params("parallel", "arbitrary"),
        name="na",
    )(nq, nk, nk, nk, nk, nv, nv, nv, nv, nk, nv, tab)


def _diff_lambda(lq_ref, lambda_init):
    lq = lq_ref[...]
    s1 = jnp.sum(lq[0:1, :] * lq[1:2, :], axis=-1, keepdims=True)
    s2 = jnp.sum(lq[2:3, :] * lq[3:4, :], axis=-1, keepdims=True)
    return jnp.exp(s1) - jnp.exp(s2) + lambda_init


def _stack_qt(qt):
    row = lax.broadcasted_iota(jnp.int32, (2 * DIFF_DK, 1), 0)
    zero = jnp.zeros_like(qt)
    return jnp.concatenate([jnp.where(row < DIFF_DK, qt, zero), jnp.where(row >= DIFF_DK, qt, zero)], axis=1)


def _q8_stack(qt):
    hi, lo = _split_f8(qt.astype(F32))
    hi, lo = _stack_qt(hi.astype(F32)), _stack_qt(lo.astype(F32))
    return jnp.concatenate([hi, hi, lo, lo], axis=0).astype(F8)


def _diff_finish_t(acc, tq, lam, nw_col, lambda_init):
    o = acc[0:DIFF_DV, :] / acc[DIFF_DV:DIFF_DV + 1, :]
    od = o[:, :tq] - lam * o[:, tq:]
    ms = jnp.mean(od * od, axis=0, keepdims=True)
    y = od * lax.rsqrt(ms + RMS_EPS) * nw_col * (1.0 - lambda_init)
    y = jnp.concatenate([y, jnp.zeros_like(y)], axis=0)
    return jnp.transpose(y)[:, 0:DIFF_DV]


def _diff_body(qt_ref, kc_ref, vtc_ref, k_ref, vt_ref, lq_ref, nw_ref, o_ref, s_ref, *, tk, unroll, lambda_init):
    tq = qt_ref.shape[1]
    n_chunks = k_ref.shape[0] // tk
    ring = s_ref.shape[0]
    qs = _q8_stack(qt_ref[...])

    def chunk(j):
        return pl.ds(j * tk if isinstance(j, int) else pl.multiple_of(j * tk, tk), tk)

    def issue_scores(j, slot):
        s = _dot(k_ref[chunk(j), :], qs)
        s_ref[slot] = s
        return jnp.max(s, axis=0, keepdims=True)

    s_ctx = _dot(kc_ref[...], qs)
    m0 = jnp.max(s_ctx, axis=0, keepdims=True)
    cmax0 = tuple(issue_scores(min(a, n_chunks - 1), a % ring) for a in range(DIFF_AHEAD))
    acc0 = _dot(vtc_ref[...], jnp.exp2(s_ctx - m0).astype(BF16))

    def group(g, carry, tail=False):
        cmax, m, acc = carry
        for u in range(unroll):
            j = g * unroll + u
            if tail and j + DIFF_AHEAD >= n_chunks:
                c_new = cmax[0]
            else:
                c_new = issue_scores(j + DIFF_AHEAD, (u + DIFF_AHEAD) % ring)
            m_new = jnp.maximum(m, cmax[0])
            p = jnp.exp2(s_ref[u % ring] - m_new).astype(BF16)
            vt = vt_ref[:, chunk(j)]
            acc = jnp.exp2(m - m_new) * acc + _dot(vt, p)
            m, cmax = m_new, cmax[1:] + (c_new,)
        return cmax, m, acc

    n_groups = n_chunks // unroll
    carry = lax.fori_loop(0, n_groups - 1, group, (cmax0, m0, acc0))
    _, _, acc = group(n_groups - 1, carry, tail=True)
    o_ref[...] = _diff_finish_t(acc, tq, _diff_lambda(lq_ref, lambda_init), nw_ref[...], lambda_init)


def _diff(dqt, dk, dvt, lq, norm_w, *, batch, seq, ctx, lambda_init):
    h, r, _ = dk.shape
    tq = DIFF_TQ
    nq = seq // tq
    tk = min(DIFF_TK, seq)
    unroll = min(DIFF_UNROLL, seq // tk)
    assert (seq // tk) % unroll == 0 and unroll % DIFF_RING == 0 and DIFF_RING > DIFF_AHEAD
    ctx_blk0 = batch * seq // ctx
    return pl.pallas_call(
        functools.partial(_diff_body, tk=tk, unroll=unroll, lambda_init=lambda_init),
        grid=(batch, h, nq),
        in_specs=[pl.BlockSpec((None, 64, tq), lambda b, hh, i: (hh, 0, b * nq + i)),
                  pl.BlockSpec((None, ctx, DIFF_KW), lambda b, hh, i: (hh, ctx_blk0 + b, 0)),
                  pl.BlockSpec((None, DIFF_VROWS, ctx), lambda b, hh, i: (hh, 0, ctx_blk0 + b)),
                  pl.BlockSpec((None, seq, DIFF_KW), lambda b, hh, i: (hh, b, 0)),
                  pl.BlockSpec((None, DIFF_VROWS, seq), lambda b, hh, i: (hh, 0, b)),
                  _resident(lq.shape),
                  _resident((DIFF_DV, 1))],
        out_specs=pl.BlockSpec((None, tq, 64), lambda b, hh, i: (hh, b * nq + i, 0)),
        out_shape=jax.ShapeDtypeStruct((h, r, 64), F32),
        scratch_shapes=[pltpu.VMEM((DIFF_RING, tk, 2 * tq), F32)],
        compiler_params=_params("parallel", "parallel", "arbitrary"),
        name="diff",
    )(dqt, dk, dvt, dk, dvt, lq, norm_w.reshape(DIFF_DV, 1))


def _ctx_attn_body(nq_ref, nk_ref, nv_ref, dqt_ref, dk_ref, dvt_ref, lq_ref, nw_ref, na_in, df_in,
                   na_out, df_out, *, lambda_init):
    del na_in, df_in
    q = nq_ref[...]
    k = nk_ref[...]
    v = nv_ref[...]
    lane = lax.broadcasted_iota(jnp.int32, (1, NA_HEADS * NA_DIM), 1) // NA_DIM
    out = jnp.zeros(na_out.shape, F32)
    for hh in range(NA_HEADS):
        mh = lane == hh
        s = _dot_nt(jnp.where(mh, q, jnp.zeros_like(q)), k)
        p = jnp.exp(s - jnp.max(s, axis=-1, keepdims=True))
        o = _dot(p.astype(BF16), v) / jnp.sum(p, axis=-1, keepdims=True)
        out = out + jnp.where(mh, o, 0.0)
    na_out[...] = out

    lam = _diff_lambda(lq_ref, lambda_init)
    tq = dqt_ref.shape[2]
    for hh in range(DIFF_HEADS):
        s = _dot(dk_ref[hh], _q8_stack(dqt_ref[hh]))
        p = jnp.exp2(s - jnp.max(s, axis=0, keepdims=True))
        acc = _dot(dvt_ref[hh], p.astype(BF16))
        df_out[hh] = _diff_finish_t(acc, tq, lam, nw_ref[...], lambda_init)


def _ctx_attn(nq, nk, nv, dqt, dk, dvt, lq, norm_w, na_o, df_o, *, batch, seq, ctx, lambda_init):
    blk0 = batch * seq // ctx
    s256 = pl.BlockSpec((ctx, 256), lambda b: (blk0 + b, 0))
    sh64 = pl.BlockSpec((DIFF_HEADS, ctx, 64), lambda b: (0, blk0 + b, 0))
    sk8 = pl.BlockSpec((DIFF_HEADS, ctx, DIFF_KW), lambda b: (0, blk0 + b, 0))
    sq_t = pl.BlockSpec((DIFF_HEADS, 64, ctx), lambda b: (0, 0, blk0 + b))
    sv_t = pl.BlockSpec((DIFF_HEADS, DIFF_VROWS, ctx), lambda b: (0, 0, blk0 + b))
    return pl.pallas_call(
        functools.partial(_ctx_attn_body, lambda_init=lambda_init),
        grid=(batch,),
        in_specs=[s256, s256, s256, sq_t, sk8, sv_t, _resident(lq.shape), _resident((DIFF_DV, 1)),
                  pl.BlockSpec(memory_space=pl.ANY), pl.BlockSpec(memory_space=pl.ANY)],
        out_specs=[s256, sh64],
        out_shape=[jax.ShapeDtypeStruct(na_o.shape, F32), jax.ShapeDtypeStruct(df_o.shape, F32)],
        input_output_aliases={8: 0, 9: 1},
        compiler_params=_params("arbitrary"),
        name="ctx_attn",
    )(nq, nk, nv, dqt, dk, dvt, lq, norm_w.reshape(DIFF_DV, 1), na_o, df_o)


def _conv_body(u_ref, prev_ref, next_ref, dw_ref, dwb_ref, lng_ref, lnb_ref, pw_ref, pwb_ref,
               o_ref, pad_ref, sh_ref, *, tiles_per_seq, n_latent_tiles):
    t = u_ref.shape[0]
    i = pl.program_id(0)
    in_ctx = i >= n_latent_tiles
    first = in_ctx | (i % tiles_per_seq == 0)
    last = in_ctx | (i % tiles_per_seq == tiles_per_seq - 1)
    pad_ref[0:HALO, :] = jnp.where(first, 0.0, prev_ref[...])
    pad_ref[HALO:HALO + t, :] = u_ref[...]
    pad_ref[HALO + t:HALO + t + HALO, :] = jnp.where(last, 0.0, next_ref[...])
    base = HALO - CONV_K // 2
    first = {}
    for phase in range(8):
        taps = [k for k in range(CONV_K) if (base + k) % 8 == phase]
        first[phase] = base + taps[0]
        span = base + taps[-1] + t - first[phase]
        sh_ref[phase, 0:span, :] = pad_ref[first[phase]:first[phase] + span, :]
    sub = 64
    parts = []
    for r0 in range(0, t, sub):
        acc = jnp.zeros((sub, CONV_CH), F32)
        for k in range(CONV_K):
            phase = (base + k) % 8
            off = base + k - first[phase] + r0
            acc = acc + dw_ref[k:k + 1, :] * sh_ref[phase, off:off + sub, :]
        parts.append(acc)
    y = jnp.concatenate(parts, axis=0) + dwb_ref[...]
    mu = jnp.mean(y, axis=-1, keepdims=True)
    yc = y - mu
    var = jnp.mean(yc * yc, axis=-1, keepdims=True)
    y = _silu(yc * lax.rsqrt(var + LN_EPS) * lng_ref[...] + lnb_ref[...])
    o_ref[...] = _dot(y.astype(BF16), pw_ref[...]) + pwb_ref[...]


def _conv(cu, dw, dw_b, ln_g, ln_b, pw, pw_b, *, batch, seq, ctx):
    r = cu.shape[0]
    t = CONV_TILE
    assert ctx == t and seq % t == 0
    n_tiles = r // t
    hb = t // HALO
    vec = lambda a: a.reshape(1, CONV_CH)
    return pl.pallas_call(
        functools.partial(_conv_body, tiles_per_seq=seq // t, n_latent_tiles=batch * seq // t),
        grid=(n_tiles,),
        in_specs=[pl.BlockSpec((t, CONV_CH), lambda i: (i, 0)),
                  pl.BlockSpec((HALO, CONV_CH), lambda i: (jnp.maximum(i * hb - 1, 0), 0)),
                  pl.BlockSpec((HALO, CONV_CH), lambda i: (jnp.minimum((i + 1) * hb, n_tiles * hb - 1), 0)),
                  _resident((CONV_K, CONV_CH))] + [_resident((1, CONV_CH))] * 3
                 + [_resident((CONV_CH, CONV_CH)), _resident((1, CONV_CH))],
        out_specs=pl.BlockSpec((t, CONV_CH), lambda i: (i, 0)),
        out_shape=jax.ShapeDtypeStruct((r, CONV_CH), F32),
        scratch_shapes=[pltpu.VMEM((t + 2 * HALO, CONV_CH), F32),
                        pltpu.VMEM((8, t + 2 * HALO, CONV_CH), F32)],
        compiler_params=_params("parallel"),
        name="conv",
    )(cu, cu, cu, dw, vec(dw_b), vec(ln_g), vec(ln_b), pw.astype(BF16), vec(pw_b))


def _wout_body(h_ref, mod_ref, gx_ref, nx_ref, dx_ref, cx_ref, w_ref, o_ref):
    mix = jnp.concatenate([gx_ref[...], nx_ref[...]] + [dx_ref[hh] for hh in range(DIFF_HEADS)]
                          + [cx_ref[...]], axis=-1).astype(BF16)
    o_ref[...] = h_ref[...] + mod_ref[5:6, :] * _dot(mix, w_ref[...])


def _wout(h, mods, gx, nx, dx, cx, w_out, *, n_tiles, group_of):
    r, d = h.shape
    tm = ROW_TILE
    row = lambda i: (i, 0)
    return pl.pallas_call(
        _wout_body,
        grid=(n_tiles,),
        in_specs=[pl.BlockSpec((tm, d), row),
                  pl.BlockSpec((None, N_MOD, d), lambda i: (group_of(i), 0, 0)),
                  pl.BlockSpec((tm, 256), row), pl.BlockSpec((tm, 256), row),
                  pl.BlockSpec((DIFF_HEADS, tm, 64), lambda i: (0, i, 0)),
                  pl.BlockSpec((tm, 256), row),
                  _resident(w_out.shape)],
        out_specs=pl.BlockSpec((tm, d), row),
        out_shape=jax.ShapeDtypeStruct((n_tiles * tm, d), F32),
        compiler_params=_params("parallel"),
        name="wout",
    )(h, mods, gx, nx, dx, cx, w_out.astype(BF16))


def kernel(x, c, ctx, c_ctx, ada_w, ada_b, norm_ffn1, ffn1_w13, ffn1_w2, norm_mix, w_in, gla_wa_f, gla_ba_f, gla_wa_b, gla_ba_b, gla_norm, na_rpb, diff_lq1, diff_lk1, diff_lq2, diff_lk2, diff_norm, conv_dw, conv_dw_b, conv_ln_g, conv_ln_b, conv_pw, conv_pw_b, w_out, norm_ffn2, ffn2_w13, ffn2_w2, final_norm):
    batch, seq, d = x.shape
    n_ctx = ctx.shape[1]
    depth = ada_w.shape[0]
    rows = seq // GRID_W
    tm = ROW_TILE
    assert seq % tm == 0 and (batch * n_ctx) % tm == 0 and batch + 1 <= 8
    lat_tiles = batch * seq // tm
    all_tiles = lat_tiles + batch * n_ctx // tm
    tiles_per_batch = seq // tm

    def group_of(i):
        return jnp.minimum(i // tiles_per_batch, batch)

    def pos_of(i):
        return jnp.where(i < lat_tiles, i % tiles_per_batch, tiles_per_batch)

    c_rows = jnp.concatenate([c, c_ctx[None, :], jnp.zeros((8 - batch - 1, d), F32)], axis=0)
    mods_all = _ada(c_rows, ada_w, ada_b)[:, :batch + 1].reshape(depth, batch + 1, N_MOD, d)
    rope = _rope_tables(seq, tm)
    h = x.reshape(batch * seq, d)
    h_ctx = ctx.reshape(batch * n_ctx, d)

    for i in range(depth):
        last = i == depth - 1
        lambda_init = 0.8 - 0.6 * math.exp(-0.3 * i)
        mods = mods_all[i]
        tok = dict(n_tiles=all_tiles, group_of=group_of)
        geo = dict(batch=batch, seq=seq, ctx=n_ctx)

        h = _ffn(h, h_ctx if i == 0 else None, mods, norm_ffn1[i], ffn1_w13[i], ffn1_w2[i], final_norm,
                 k0=0, final=False, **tok)

        wa, ba = _gate_weights(gla_wa_f[i], gla_ba_f[i], gla_wa_b[i], gla_ba_b[i])
        (gqk, gv, gvt, gg, glg, nq, nk, nv, dq, dk, dv, cu) = _proj(
            h, mods, norm_mix[i], _permute_w_in(w_in[i]), wa, ba, rope, pos_of=pos_of, **tok)

        gnorm = jnp.tile(gla_norm[i], GLA_HEADS).reshape(1, GLA_HEADS * GLA_DV)
        o_f = _gla(gqk, gv, gvt, glg, gg, gg, gnorm, reverse=False, **geo)
        gx = _gla(gqk, gv, gvt, glg, gg, o_f, gnorm, reverse=True, **geo)

        nx = _na(nq, nk, nv, _na_col_tables(na_rpb[i]), **geo)

        lq = jnp.stack([diff_lq1[i], diff_lk1[i], diff_lq2[i], diff_lk2[i]])
        dx = _diff(dq, dk, dv, lq, diff_norm[i], lambda_init=lambda_init, **geo)
        if not last:
            nx, dx = _ctx_attn(nq, nk, nv, dq, dk, dv, lq, diff_norm[i], nx, dx,
                               lambda_init=lambda_init, **geo)

        cx = _conv(cu, conv_dw[i], conv_dw_b[i], conv_ln_g[i], conv_ln_b[i], conv_pw[i], conv_pw_b[i], **geo)

        if last:
            tok = dict(n_tiles=lat_tiles, group_of=group_of)
        h = _wout(h, mods, gx, nx, dx, cx, w_out[i], **tok)

        h = _ffn(h, None, mods, norm_ffn2[i], ffn2_w13[i], ffn2_w2[i], final_norm, k0=6, final=last, **tok)

    return h.reshape(batch, seq, d)
```

```python
import functools
import math

import jax
import jax.numpy as jnp
from jax import lax
from jax.experimental import pallas as pl
from jax.experimental.pallas import tpu as pltpu

F32 = jnp.float32
BF16 = jnp.bfloat16

GRID_W = 64
N_MOD = 9
RMS_EPS = 1e-6
LN_EPS = 1e-5
NEG_INF = -1e30
GLA_HEADS, GLA_DK, GLA_DV, GLA_RANK, GLA_TAU, GLA_CHUNK = 4, 32, 64, 16, 16.0, 64
NA_HEADS, NA_DIM, NA_ROWS, NA_COLS = 4, 64, 8, 16
DIFF_HEADS, DIFF_DK, DIFF_DV = 4, 32, 64
DIFF_VROWS = 128
CONV_CH, CONV_K = 256, 31
ROPE_BASE = 10000.0
LOG2E = 1.4426950408889634

LANES = 128
VMEM_LIMIT = 56 * 1024 * 1024

ROW_TILE = 512
FF_CHUNK = 256
GLA_BLOCK = 256
NA_QROWS = 8
NA_KBLK = 256
CONV_TILE = 256
DIFF_TQ = 256
DIFF_TK = 512
DIFF_UNROLL = 8
DIFF_AHEAD = 3
DIFF_RING = 4
HALO = 16

C_GLA, C_NA, C_DIFF, C_CONV, C_AUX, C_END = 0, 768, 1536, 2304, 2816, 2944


def _dot(a, b):
    return jnp.dot(a, b, preferred_element_type=F32)


def _dot_nt(a, b):
    return lax.dot_general(a, b, (((1,), (1,)), ((), ())), preferred_element_type=F32)


def _params(*sem):
    return pltpu.CompilerParams(dimension_semantics=sem, vmem_limit_bytes=VMEM_LIMIT)


def _resident(shape):
    nd = len(shape)
    return pl.BlockSpec(shape, lambda *_: (0,) * nd, pipeline_mode=pl.Buffered(1))


def _silu(x):
    return x * jax.nn.sigmoid(x)


def _rms(x, w):
    return x * lax.rsqrt(jnp.mean(x * x, axis=-1, keepdims=True) + RMS_EPS) * w


def _ada_body(c_ref, w_ref, b_ref, o_ref):
    s = _silu(c_ref[...])
    o_ref[...] = jnp.dot(s, w_ref[...], precision=lax.Precision.HIGHEST,
                         preferred_element_type=F32) + b_ref[...]


def _ada(c_rows, ada_w, ada_b):
    depth, d, _ = ada_w.shape
    return pl.pallas_call(
        _ada_body,
        grid=(depth, N_MOD),
        in_specs=[pl.BlockSpec((8, d), lambda l, n: (0, 0)),
                  pl.BlockSpec((None, d, d), lambda l, n: (l, 0, n)),
                  pl.BlockSpec((None, 1, d), lambda l, n: (l, 0, n))],
        out_specs=pl.BlockSpec((None, 8, d), lambda l, n: (l, 0, n)),
        out_shape=jax.ShapeDtypeStruct((depth, 8, N_MOD * d), F32),
        compiler_params=_params("arbitrary", "arbitrary"),
        name="ada",
    )(c_rows, ada_w, ada_b.reshape(depth, 1, N_MOD * d))


def _ffn_body(*refs, k0, final, source, n_first):
    h_ref = refs[0]
    n_extra = {"plain": 0, "two_arrays": 1, "mixers": 6}[source]
    extra = refs[1:1 + n_extra]
    mod_ref, nw_ref, w13_ref, w2_ref, fw_ref, o_ref, xb_ref, acc_ref, h13_ref = refs[1 + n_extra:]
    n_chunks, tf, _ = w2_ref.shape
    ff = n_chunks * tf
    x = h_ref[...]
    if source == "two_arrays":
        x = jnp.where(pl.program_id(0) < n_first, x, extra[0][...])
    elif source == "mixers":
        gx_ref, gxc_ref, nx_ref, dx_ref, cx_ref, wo_ref = extra
        gx = jnp.where(pl.program_id(0) < n_first, gx_ref[...], gxc_ref[...].reshape(gx_ref.shape))
        mix = jnp.concatenate([gx, nx_ref[...]] + [dx_ref[hh] for hh in range(DIFF_HEADS)]
                              + [cx_ref[...]], axis=-1).astype(BF16)
        x = x + mod_ref[5:6, :] * _dot(mix, wo_ref[...])
    xm = _rms(x, nw_ref[...]) * (1.0 + mod_ref[k0 + 1:k0 + 2, :]) + mod_ref[k0:k0 + 1, :]
    xb_ref[...] = xm.astype(BF16)
    acc_ref[...] = jnp.zeros_like(acc_ref)

    def up(c, slot):
        col = c * tf if isinstance(c, int) else pl.multiple_of(c * tf, tf)
        h13_ref[slot, :, :tf] = _dot(xb_ref[...], w13_ref[:, pl.ds(col, tf)])
        h13_ref[slot, :, tf:] = _dot(xb_ref[...], w13_ref[:, pl.ds(ff + col, tf)])

    def down(c, slot):
        a = h13_ref[slot, :, :tf]
        u = h13_ref[slot, :, tf:]
        acc_ref[...] += _dot((_silu(a) * u).astype(BF16), w2_ref[c])

    def pair(t, carry):
        up(2 * t + 1, 1)
        down(2 * t, 0)
        up(2 * t + 2, 0)
        down(2 * t + 1, 1)
        return carry

    up(0, 0)
    lax.fori_loop(0, (n_chunks - 1) // 2, pair, 0)
    down(n_chunks - 1, 0)
    out = x + (0.5 * mod_ref[k0 + 2:k0 + 3, :]) * acc_ref[...]
    if final:
        out = _rms(out, fw_ref[...])
    o_ref[...] = out


def _ffn(h, mods, norm_w, w13, w2, final_w, *, k0, n_tiles, group_of, final, h_ctx=None, mixers=None):
    d = h.shape[1]
    tm = ROW_TILE
    ff = w2.shape[0]
    n_chunks = ff // FF_CHUNK
    assert ff % FF_CHUNK == 0 and n_chunks % 2 == 1
    row = lambda i: (i, 0)
    source, n_first, extra, extra_specs, h_spec = "plain", None, [], [], pl.BlockSpec((tm, d), row)
    if h_ctx is not None:
        source, n_first, extra = "two_arrays", h.shape[0] // tm, [h_ctx]
        h_spec = pl.BlockSpec((tm, d), lambda i: (jnp.minimum(i, n_first - 1), 0))
        extra_specs = [pl.BlockSpec((tm, d), lambda i: (jnp.maximum(i - n_first, 0), 0))]
    elif mixers is not None:
        gx, nx, dx, cx, w_out, (batch, seq, n_ctx) = mixers
        assert batch * n_ctx == tm and seq % tm == 0 and seq % n_ctx == 0
        source, n_first, extra = "mixers", batch * seq // tm, [gx, gx, nx, dx, cx, w_out.astype(BF16)]
        per_seq = seq // tm
        extra_specs = [pl.BlockSpec((None, tm, 256),
                                    lambda i: (jnp.minimum(i // per_seq, batch - 1), i % per_seq, 0)),
                       pl.BlockSpec((batch, n_ctx, 256), lambda i: (0, seq // n_ctx, 0)),
                       pl.BlockSpec((tm, 256), row),
                       pl.BlockSpec((DIFF_HEADS, tm, 64), lambda i: (0, i, 0)),
                       pl.BlockSpec((tm, 256), row), _resident(w_out.shape)]
    return pl.pallas_call(
        functools.partial(_ffn_body, k0=k0, final=final, source=source, n_first=n_first),
        grid=(n_tiles,),
        in_specs=[h_spec] + extra_specs + [
            pl.BlockSpec((None, N_MOD, d), lambda i: (group_of(i), 0, 0)),
            _resident((1, d)),
            _resident(w13.shape),
            _resident((n_chunks, FF_CHUNK, d)),
            _resident((1, d))],
        out_specs=pl.BlockSpec((tm, d), row),
        out_shape=jax.ShapeDtypeStruct((n_tiles * tm, d), F32),
        scratch_shapes=[pltpu.VMEM((tm, d), BF16), pltpu.VMEM((tm, d), F32),
                        pltpu.VMEM((2, tm, 2 * FF_CHUNK), F32)],
        compiler_params=_params("parallel"),
        name="ffn",
    )(h, *extra, mods, norm_w.reshape(1, d), w13.astype(BF16),
      w2.astype(BF16).reshape(n_chunks, FF_CHUNK, d), final_w.reshape(1, d))


def _log_sigmoid(x):
    return jnp.minimum(x, 0.0) - jnp.log1p(jnp.exp(-jnp.abs(x)))


def _rope_rotate(x):
    n = x.shape[-1]
    lane = lax.broadcasted_iota(jnp.int32, x.shape, 1)
    up = pltpu.roll(x, n - 8, 1)
    dn = pltpu.roll(x, 8, 1)
    return jnp.where((lane & 15) < 8, -up, dn)


def _rope_rotate_rows(x):
    n = x.shape[0]
    row = lax.broadcasted_iota(jnp.int32, x.shape, 0)
    up = pltpu.roll(x, n - 8, 0)
    dn = pltpu.roll(x, 8, 0)
    return jnp.where((row & 15) < 8, -up, dn)


def _proj_body(h_ref, mod_ref, nw_ref, w_ref, wvt_ref, wdqt_ref, wdvt_ref, wa_ref, ba_ref,
               cos_ref, sin_ref, cost_ref, sint_ref,
               gqk_ref, gv_ref, gvt_ref, gg_ref, glg_ref, nq_ref, nk_ref, nv_ref,
               dqt_ref, dk_ref, dvt_ref, cu_ref, xb_ref):
    x = h_ref[...]
    tm = x.shape[0]
    xm = _rms(x, nw_ref[...]) * (1.0 + mod_ref[4:5, :]) + mod_ref[3:4, :]
    xb_ref[...] = xm.astype(BF16)

    z = _dot(xb_ref[...], w_ref[:, C_GLA:C_GLA + 256])
    lane = lax.broadcasted_iota(jnp.int32, (1, 256), 1)
    gqk_ref[...] = z * jnp.where(lane < 128, GLA_DK ** -0.5, 1.0)
    gv_ref[...] = _dot(xb_ref[...], w_ref[:, C_GLA + 256:C_GLA + 512])
    gvt_ref[...] = _dot_nt(wvt_ref[...], xb_ref[...])
    gg_ref[...] = _dot(xb_ref[...], w_ref[:, C_GLA + 512:C_GLA + 768])
    aux = _dot(xb_ref[...], w_ref[:, C_AUX:C_END])
    pre = _dot(aux.astype(BF16), wa_ref[...]) + ba_ref[...]
    glg_ref[...] = _log_sigmoid(pre) * (1.0 / GLA_TAU)

    nq_ref[...] = (_dot(xb_ref[...], w_ref[:, C_NA:C_NA + 256]) * (NA_DIM ** -0.5)).astype(BF16)
    nk_ref[...] = _dot(xb_ref[...], w_ref[:, C_NA + 256:C_NA + 512]).astype(BF16)
    nv_ref[...] = _dot(xb_ref[...], w_ref[:, C_NA + 512:C_NA + 768]).astype(BF16)

    cos = cos_ref[...]
    sin = sin_ref[...]
    cos2 = jnp.concatenate([cos, cos], axis=1)
    sin2 = jnp.concatenate([sin, sin], axis=1)
    zk = _dot(xb_ref[...], w_ref[:, C_DIFF + 256:C_DIFF + 512])
    zk = zk * cos2 + _rope_rotate(zk) * sin2
    n_rep = 2 * DIFF_HEADS
    cos_t = jnp.concatenate([cost_ref[...]] * n_rep, axis=0)
    sin_t = jnp.concatenate([sint_ref[...]] * n_rep, axis=0)
    zqt = _dot_nt(wdqt_ref[...], xb_ref[...])
    zqt = (zqt * cos_t + _rope_rotate_rows(zqt) * sin_t) * (DIFF_DK ** -0.5 * LOG2E)
    zvt = _dot_nt(wdvt_ref[...], xb_ref[...])
    pad_rows = DIFF_VROWS - DIFF_DV
    one_row = jnp.where(lax.broadcasted_iota(jnp.int32, (pad_rows, tm), 0) == 0, 1.0, 0.0).astype(BF16)
    for hh in range(DIFF_HEADS):
        sl = slice(64 * hh, 64 * hh + 64)
        dk_ref[hh] = zk[:, sl].astype(BF16)
        dqt_ref[hh] = zqt[sl, :].astype(BF16)
        dvt_ref[hh, 0:64, :] = zvt[sl, :].astype(BF16)
        dvt_ref[hh, DIFF_DV:DIFF_VROWS, :] = one_row

    za = _dot(xb_ref[...], w_ref[:, C_CONV:C_CONV + 256])
    zg = _dot(xb_ref[...], w_ref[:, C_CONV + 256:C_CONV + 512])
    cu_ref[...] = za * jax.nn.sigmoid(zg)


def _proj(h, mods, norm_w, w_p, wa, ba, rope, *, n_tiles, group_of, pos_of):
    r, d = h.shape
    tm = ROW_TILE
    row = lambda i: (i, 0)
    hrow = lambda i: (0, i, 0)
    hcol = lambda i: (0, 0, i)
    f32_256 = jax.ShapeDtypeStruct((r, 256), F32)
    bf_256 = jax.ShapeDtypeStruct((r, 256), BF16)
    wvt = w_p[:, C_GLA + 256:C_GLA + 512].T
    wdqt = w_p[:, C_DIFF:C_DIFF + 256].T
    wdvt = w_p[:, C_DIFF + 512:C_DIFF + 768].T
    cos_r, sin_r, cos_c, sin_c = rope
    out_shape = [f32_256, f32_256, jax.ShapeDtypeStruct((256, r), F32), f32_256, f32_256,
                 bf_256, bf_256, bf_256,
                 jax.ShapeDtypeStruct((DIFF_HEADS, 64, r), BF16),
                 jax.ShapeDtypeStruct((DIFF_HEADS, r, 64), BF16),
                 jax.ShapeDtypeStruct((DIFF_HEADS, DIFF_VROWS, r), BF16),
                 f32_256]
    out_specs = [pl.BlockSpec((tm, 256), row)] * 2 + [pl.BlockSpec((256, tm), lambda i: (0, i))] + [
        pl.BlockSpec((tm, 256), row)] * 5 + [
        pl.BlockSpec((DIFF_HEADS, 64, tm), hcol),
        pl.BlockSpec((DIFF_HEADS, tm, 64), hrow),
        pl.BlockSpec((DIFF_HEADS, DIFF_VROWS, tm), hcol),
        pl.BlockSpec((tm, 256), row)]
    return pl.pallas_call(
        _proj_body,
        grid=(n_tiles,),
        in_specs=[pl.BlockSpec((tm, d), row),
                  pl.BlockSpec((None, N_MOD, d), lambda i: (group_of(i), 0, 0)),
                  _resident((1, d)),
                  _resident(w_p.shape),
                  _resident(wvt.shape),
                  _resident(wdqt.shape),
                  _resident(wdvt.shape),
                  _resident(wa.shape),
                  _resident(ba.shape),
                  pl.BlockSpec((tm, LANES), lambda i: (pos_of(i), 0)),
                  pl.BlockSpec((tm, LANES), lambda i: (pos_of(i), 0)),
                  pl.BlockSpec((DIFF_DK, tm), lambda i: (0, pos_of(i))),
                  pl.BlockSpec((DIFF_DK, tm), lambda i: (0, pos_of(i)))],
        out_specs=out_specs,
        out_shape=out_shape,
        scratch_shapes=[pltpu.VMEM((tm, d), BF16)],
        compiler_params=_params("parallel"),
        name="proj",
    )(h, mods, norm_w.reshape(1, d), w_p, wvt, wdqt, wdvt, wa, ba, cos_r, sin_r, cos_c, sin_c)


def _permute_w_in(w_in):
    d = w_in.shape[0]
    g0 = 2 * GLA_HEADS * GLA_DK + 2 * GLA_HEADS * GLA_DV
    aux = w_in[:, g0:g0 + 2 * GLA_RANK]
    rest = w_in[:, g0 + 2 * GLA_RANK:]
    pad = jnp.zeros((d, C_END - C_AUX - 2 * GLA_RANK), w_in.dtype)
    return jnp.concatenate([w_in[:, :g0], rest, aux, pad], axis=1).astype(BF16)


def _gate_weights(wa_f, ba_f, wa_b, ba_b):
    n = GLA_HEADS * GLA_DK
    wa = jnp.zeros((C_END - C_AUX, 2 * n), F32)
    wa = wa.at[:GLA_RANK, :n].set(wa_f).at[GLA_RANK:2 * GLA_RANK, n:].set(wa_b)
    return wa.astype(BF16), jnp.concatenate([ba_f, ba_b]).reshape(1, 2 * n)


def _rope_tables(seq, tile):
    t = jnp.arange(seq)
    row = (t // GRID_W).astype(F32)
    col = (t % GRID_W).astype(F32)
    half = DIFF_DK // 2
    inv = 1.0 / (ROPE_BASE ** (jnp.arange(0, half, 2, dtype=F32) / half))
    ang_r = row[:, None] * inv
    ang_c = col[:, None] * inv
    ang = jnp.concatenate([ang_r, ang_r, ang_c, ang_c], axis=-1)
    cos = jnp.concatenate([jnp.cos(ang), jnp.ones((tile, DIFF_DK), F32)], axis=0)
    sin = jnp.concatenate([jnp.sin(ang), jnp.zeros((tile, DIFF_DK), F32)], axis=0)
    rep = (1, LANES // DIFF_DK)
    return jnp.tile(cos, rep), jnp.tile(sin, rep), cos.T, sin.T


def _split3(x):
    hi = x.astype(BF16)
    r1 = x - hi.astype(F32)
    mid = r1.astype(BF16)
    lo = (r1 - mid.astype(F32)).astype(BF16)
    return hi, mid, lo


def _gla_body(*refs, reverse, batch):
    per_b = [refs[5 * b:5 * b + 5] for b in range(batch)]
    of_ref, nw_ref, o_ref, st_ref = refs[5 * batch:]
    blk = per_b[0][0].shape[0]
    n_chunks = blk // GLA_CHUNK
    nqk = GLA_HEADS * GLA_DK
    nv = GLA_HEADS * GLA_DV

    @pl.when(pl.program_id(0) == 0)
    def _():
        st_ref[...] = jnp.zeros_like(st_ref)

    ri = lax.broadcasted_iota(jnp.int32, (blk, blk), 0)
    ci = lax.broadcasted_iota(jnp.int32, (blk, blk), 1)
    same = (ri // GLA_CHUNK) == (ci // GLA_CHUNK)
    tri = jnp.where(same & ((ci >= ri) if reverse else (ci <= ri)), 1.0, 0.0).astype(BF16)
    bcum_of = []
    for qk_ref, v_ref, vt_ref, lg_ref, g_ref in per_b:
        lg = lg_ref[:, nqk:2 * nqk] if reverse else lg_ref[:, 0:nqk]
        hi, mid, lo = _split3(lg)
        bcum_of.append(_dot(tri, hi) + _dot(tri, mid) + _dot(tri, lo))

    c = GLA_CHUNK
    rk = lax.broadcasted_iota(jnp.int32, (GLA_HEADS * c, nqk), 0) // c
    ck = lax.broadcasted_iota(jnp.int32, (GLA_HEADS * c, nqk), 1) // GLA_DK
    mask_k = rk == ck
    rv = lax.broadcasted_iota(jnp.int32, (GLA_HEADS * c, nv), 0) // c
    cv = lax.broadcasted_iota(jnp.int32, (GLA_HEADS * c, nv), 1) // GLA_DV
    mask_v = rv == cv
    rs = lax.broadcasted_iota(jnp.int32, (nv, nqk), 0) // GLA_DV
    cs = lax.broadcasted_iota(jnp.int32, (nv, nqk), 1) // GLA_DK
    mask_s = rs == cs
    ai = lax.broadcasted_iota(jnp.int32, (c, GLA_HEADS * c), 0)
    aj = lax.broadcasted_iota(jnp.int32, (c, GLA_HEADS * c), 1) % c
    mask_a = (aj >= ai) if reverse else (aj <= ai)

    order = range(n_chunks - 1, -1, -1) if reverse else range(n_chunks)
    units = [(b, ch) for ch in order for b in range(batch)]
    q_in, a_raw, u_t, decay = {}, {}, {}, {}
    vts = [refs_b[2][...].astype(BF16) for refs_b in per_b]
    for b, ch in units:
        qk_ref = per_b[b][0]
        rows = slice(ch * c, (ch + 1) * c)
        k = qk_ref[rows, nqk:2 * nqk]
        bcum = bcum_of[b][rows, :]
        btot = bcum[0:1, :] if reverse else bcum[c - 1:c, :]
        q_in[b, ch] = (qk_ref[rows, 0:nqk] * jnp.exp(bcum)).astype(BF16)
        k_in = k * jnp.exp(-bcum)
        k_out = (k * jnp.exp(btot - bcum)).astype(BF16)
        k_bd = jnp.where(mask_k, jnp.concatenate([k_in] * GLA_HEADS, axis=0), 0.0).astype(BF16)
        a_raw[b, ch] = _dot_nt(q_in[b, ch], k_bd)
        k_pad = jnp.concatenate([jnp.zeros((n * c, nqk), BF16) for n in (ch,) if n] + [k_out]
                                + [jnp.zeros((n * c, nqk), BF16) for n in (n_chunks - 1 - ch,) if n], axis=0)
        u_t[b, ch] = jnp.where(mask_s, _dot(vts[b], k_pad), 0.0)
        decay[b, ch] = jnp.exp(btot)
    o = {}
    for b, ch in units:
        v = per_b[b][1][ch * c:(ch + 1) * c, :]
        v_bd = jnp.where(mask_v, jnp.concatenate([v] * GLA_HEADS, axis=0), 0.0).astype(BF16)
        o[b, ch] = _dot(jnp.where(mask_a, a_raw[b, ch], 0.0).astype(BF16), v_bd)
    st = [st_ref[b] for b in range(batch)]
    for b, ch in units:
        o[b, ch] = o[b, ch] + _dot_nt(q_in[b, ch], st[b].astype(BF16))
        st[b] = st[b] * decay[b, ch] + u_t[b, ch]
    for b in range(batch):
        st_ref[b] = st[b]
        o_blk = jnp.concatenate([o[b, ch] for ch in range(n_chunks)], axis=0)
        if reverse:
            o_blk = o_blk + of_ref[b]
            hi2, lo2, _ = _split3(o_blk * o_blk)
            hr = lax.broadcasted_iota(jnp.int32, (nv, nv), 0) // GLA_DV
            hc = lax.broadcasted_iota(jnp.int32, (nv, nv), 1) // GLA_DV
            seg = jnp.where(hr == hc, 1.0, 0.0).astype(BF16)
            ms = (_dot(hi2, seg) + _dot(lo2, seg)) * (1.0 / GLA_DV)
            o_blk = o_blk * lax.rsqrt(ms + RMS_EPS) * nw_ref[...] * _silu(per_b[b][4][...])
        o_ref[b] = o_blk


def _gla(gqk, gv, gvt, glg, gg, o_f, norm_w4, *, reverse, batch, seq, ctx):
    blk = GLA_BLOCK
    nc, nl = ctx // blk, seq // blk
    ctx_base = batch * seq // blk

    def step_blk(s):
        if reverse:
            return jnp.where(s < nc, nl + (nc - 1 - s), nl - 1 - (s - nc))
        return jnp.where(s < nc, nl + s, s - nc)

    def row_blk(b):
        return lambda s: jnp.where(s < nc, ctx_base + b * nc - nl, b * nl) + step_blk(s)

    specs, args = [], []
    for b in range(batch):
        spec = pl.BlockSpec((blk, 256), lambda s, f=row_blk(b): (f(s), 0))
        spec_t = pl.BlockSpec((256, blk), lambda s, f=row_blk(b): (0, f(s)))
        specs += [spec, spec, spec_t, spec, spec]
        args += [gqk, gv, gvt, glg, gg]
    seq_spec = pl.BlockSpec((batch, blk, 256), lambda s: (0, step_blk(s), 0))
    if o_f is None:
        o_f, of_spec = norm_w4, _resident((1, 256))
    else:
        of_spec = seq_spec
    return pl.pallas_call(
        functools.partial(_gla_body, reverse=reverse, batch=batch),
        grid=(nc + nl,),
        in_specs=specs + [of_spec, _resident((1, 256))],
        out_specs=seq_spec,
        out_shape=jax.ShapeDtypeStruct((batch, seq + ctx, 256), F32),
        scratch_shapes=[pltpu.VMEM((batch, GLA_HEADS * GLA_DV, GLA_HEADS * GLA_DK), F32)],
        compiler_params=_params("arbitrary"),
        name="gla_bwd" if reverse else "gla_fwd",
    )(*args, o_f, norm_w4)


def _na_col_tables(rpb):
    col = jnp.arange(GRID_W)
    cstart = jnp.clip(col - NA_COLS // 2, 0, GRID_W - NA_COLS)
    col_ok = (col[None, :] >= cstart[:, None]) & (col[None, :] < cstart[:, None] + NA_COLS)
    col_off = jnp.clip(col[None, :] - col[:, None] + (NA_COLS - 1), 0, 2 * NA_COLS - 2)
    t = jnp.where(col_ok[None, None], rpb[:, :, col_off], NEG_INF)
    dead = jnp.full((NA_HEADS, 1, GRID_W, GRID_W), NEG_INF, F32)
    t = jnp.concatenate([dead, t.astype(F32), dead], axis=1)
    return jnp.concatenate([t[:, :-1], t[:, 1:]], axis=-1)


def _na_body(q_ref, k0, k1, k2, k3, v0, v1, v2, v3, kc_ref, vc_ref, tab_ref, o_ref, *, rows, key_blocks):
    q = q_ref[...]
    kw = jnp.concatenate([k0[...], k1[...], k2[...], k3[...]], axis=0)
    vw = jnp.concatenate([v0[...], v1[...], v2[...], v3[...]], axis=0)
    kc = kc_ref[...]
    vc = vc_ref[...]

    j = pl.program_id(1)
    rows_per_blk = NA_KBLK // GRID_W
    r0 = j * NA_QROWS
    kr0 = jnp.clip(2 * j - 1, 0, key_blocks - 4) * rows_per_blk
    lane_lo = lax.broadcasted_iota(jnp.int32, (1, 2 * GRID_W), 1) < GRID_W
    n_pairs = 4 * rows_per_blk // 2
    entry, ok = [], []
    for a in range(NA_QROWS):
        r = r0 + a
        start = jnp.clip(r - NA_ROWS // 2, 0, rows - NA_ROWS)
        for bp in range(n_pairs):
            rk = kr0 + 2 * bp
            entry.append(jnp.clip(rk - r + NA_ROWS, 0, 2 * NA_ROWS - 1))
            in0 = ((rk >= start) & (rk < start + NA_ROWS)).astype(jnp.int32)
            in1 = ((rk + 1 >= start) & (rk + 1 < start + NA_ROWS)).astype(jnp.int32)
            ok.append(jnp.where(lane_lo, in0, in1) != 0)

    def bias_of(hh):
        rows_ = []
        for a in range(NA_QROWS):
            tiles = [jnp.where(ok[a * n_pairs + bp], tab_ref[hh, entry[a * n_pairs + bp]], NEG_INF)
                     for bp in range(n_pairs)]
            rows_.append(jnp.concatenate(tiles, axis=1))
        return jnp.concatenate(rows_, axis=0)

    lane = lax.broadcasted_iota(jnp.int32, (1, NA_HEADS * NA_DIM), 1) // NA_DIM
    out = jnp.zeros(o_ref.shape, F32)
    for hh in range(NA_HEADS):
        mh = lane == hh
        qh = jnp.where(mh, q, jnp.zeros_like(q))
        s_w = _dot_nt(qh, kw) + bias_of(hh)
        s_c = _dot_nt(qh, kc)
        m = jnp.maximum(jnp.max(s_w, axis=-1, keepdims=True), jnp.max(s_c, axis=-1, keepdims=True))
        p_w = jnp.exp(s_w - m)
        p_c = jnp.exp(s_c - m)
        l = jnp.sum(p_w, axis=-1, keepdims=True) + jnp.sum(p_c, axis=-1, keepdims=True)
        o = _dot(p_w.astype(BF16), vw) + _dot(p_c.astype(BF16), vc)
        out = out + jnp.where(mh, o / l, 0.0)
    o_ref[...] = out


def _na(nq, nk, nv, tab, *, batch, seq, ctx):
    r = nq.shape[0]
    tq = NA_QROWS * GRID_W
    nj = seq // tq
    kb = seq // NA_KBLK
    assert kb >= 4 and seq % tq == 0 and seq % ctx == 0
    ctx_blk0 = batch * seq // ctx

    def kspec(i):
        return pl.BlockSpec((NA_KBLK, 256),
                            lambda b, j: (b * kb + jnp.clip(2 * j - 1, 0, kb - 4) + i, 0))

    cspec = pl.BlockSpec((ctx, 256), lambda b, j: (ctx_blk0 + b, 0))
    return pl.pallas_call(
        functools.partial(_na_body, rows=seq // GRID_W, key_blocks=kb),
        grid=(batch, nj),
        in_specs=[pl.BlockSpec((tq, 256), lambda b, j: (b * nj + j, 0))]
                 + [kspec(i) for i in range(4)] + [kspec(i) for i in range(4)]
                 + [cspec, cspec, _resident(tab.shape)],
        out_specs=pl.BlockSpec((tq, 256), lambda b, j: (b * nj + j, 0)),
        out_shape=jax.ShapeDtypeStruct((r, 256), F32),
        compiler_params=_params("parallel", "arbitrary"),
        name="na",
    )(nq, nk, nk, nk, nk, nv, nv, nv, nv, nk, nv, tab)


def _diff_lambda(lq_ref, lambda_init):
    lq = lq_ref[...]
    s1 = jnp.sum(lq[0:1, :] * lq[1:2, :], axis=-1, keepdims=True)
    s2 = jnp.sum(lq[2:3, :] * lq[3:4, :], axis=-1, keepdims=True)
    return jnp.exp(s1) - jnp.exp(s2) + lambda_init


def _stack_qt(qt):
    row = lax.broadcasted_iota(jnp.int32, (2 * DIFF_DK, 1), 0)
    zero = jnp.zeros_like(qt)
    return jnp.concatenate([jnp.where(row < DIFF_DK, qt, zero), jnp.where(row >= DIFF_DK, qt, zero)], axis=1)


def _diff_finish_t(acc, tq, lam, nw_col, lambda_init):
    o = acc[0:DIFF_DV, :] / acc[DIFF_DV:DIFF_DV + 1, :]
    od = o[:, :tq] - lam * o[:, tq:]
    ms = jnp.mean(od * od, axis=0, keepdims=True)
    y = od * lax.rsqrt(ms + RMS_EPS) * nw_col * (1.0 - lambda_init)
    y = jnp.concatenate([y, jnp.zeros_like(y)], axis=0)
    return jnp.transpose(y)[:, 0:DIFF_DV]


def _diff_body(qt_ref, kc_ref, vtc_ref, k_ref, vt_ref, lq_ref, nw_ref, o_ref, s_ref, *, tk, unroll, lambda_init):
    tq = qt_ref.shape[1]
    n_chunks = k_ref.shape[0] // tk
    ring = s_ref.shape[0]
    qs = _stack_qt(qt_ref[...])

    def chunk(j):
        return pl.ds(j * tk if isinstance(j, int) else pl.multiple_of(j * tk, tk), tk)

    def issue_scores(j, slot):
        s = _dot(k_ref[chunk(j), :], qs)
        s_ref[slot] = s
        return jnp.max(s, axis=0, keepdims=True)

    s_ctx = _dot(kc_ref[...], qs)
    m0 = jnp.max(s_ctx, axis=0, keepdims=True)
    cmax0 = tuple(issue_scores(min(a, n_chunks - 1), a % ring) for a in range(DIFF_AHEAD))
    acc0 = _dot(vtc_ref[...], jnp.exp2(s_ctx - m0).astype(BF16))

    def group(g, carry, tail=False):
        cmax, m, acc = carry
        for u in range(unroll):
            j = g * unroll + u
            if tail and j + DIFF_AHEAD >= n_chunks:
                c_new = cmax[0]
            else:
                c_new = issue_scores(j + DIFF_AHEAD, (u + DIFF_AHEAD) % ring)
            m_new = jnp.maximum(m, cmax[0])
            p = jnp.exp2(s_ref[u % ring] - m_new).astype(BF16)
            vt = vt_ref[:, chunk(j)]
            acc = jnp.exp2(m - m_new) * acc + _dot(vt, p)
            m, cmax = m_new, cmax[1:] + (c_new,)
        return cmax, m, acc

    n_groups = n_chunks // unroll
    carry = lax.fori_loop(0, n_groups - 1, group, (cmax0, m0, acc0))
    _, _, acc = group(n_groups - 1, carry, tail=True)
    o_ref[...] = _diff_finish_t(acc, tq, _diff_lambda(lq_ref, lambda_init), nw_ref[...], lambda_init)


def _diff(dqt, dk, dvt, lq, norm_w, *, batch, seq, ctx, lambda_init):
    h, r, _ = dk.shape
    tq = DIFF_TQ
    nq = seq // tq
    tk = min(DIFF_TK, seq)
    unroll = min(DIFF_UNROLL, seq // tk)
    assert (seq // tk) % unroll == 0 and unroll % DIFF_RING == 0 and DIFF_RING > DIFF_AHEAD
    ctx_blk0 = batch * seq // ctx
    return pl.pallas_call(
        functools.partial(_diff_body, tk=tk, unroll=unroll, lambda_init=lambda_init),
        grid=(batch, h, nq),
        in_specs=[pl.BlockSpec((None, 64, tq), lambda b, hh, i: (hh, 0, b * nq + i)),
                  pl.BlockSpec((None, ctx, 64), lambda b, hh, i: (hh, ctx_blk0 + b, 0)),
                  pl.BlockSpec((None, DIFF_VROWS, ctx), lambda b, hh, i: (hh, 0, ctx_blk0 + b)),
                  pl.BlockSpec((None, seq, 64), lambda b, hh, i: (hh, b, 0)),
                  pl.BlockSpec((None, DIFF_VROWS, seq), lambda b, hh, i: (hh, 0, b)),
                  _resident(lq.shape),
                  _resident((DIFF_DV, 1))],
        out_specs=pl.BlockSpec((None, tq, 64), lambda b, hh, i: (hh, b * nq + i, 0)),
        out_shape=jax.ShapeDtypeStruct((h, r, 64), F32),
        scratch_shapes=[pltpu.VMEM((DIFF_RING, tk, 2 * tq), F32)],
        compiler_params=_params("parallel", "parallel", "arbitrary"),
        name="diff",
    )(dqt, dk, dvt, dk, dvt, lq, norm_w.reshape(DIFF_DV, 1))


def _ctx_attn_body(nq_ref, nk_ref, nv_ref, dqt_ref, dk_ref, dvt_ref, lq_ref, nw_ref, na_in, df_in,
                   na_out, df_out, *, lambda_init):
    del na_in, df_in
    q = nq_ref[...]
    k = nk_ref[...]
    v = nv_ref[...]
    lane = lax.broadcasted_iota(jnp.int32, (1, NA_HEADS * NA_DIM), 1) // NA_DIM
    out = jnp.zeros(na_out.shape, F32)
    for hh in range(NA_HEADS):
        mh = lane == hh
        s = _dot_nt(jnp.where(mh, q, jnp.zeros_like(q)), k)
        p = jnp.exp(s - jnp.max(s, axis=-1, keepdims=True))
        o = _dot(p.astype(BF16), v) / jnp.sum(p, axis=-1, keepdims=True)
        out = out + jnp.where(mh, o, 0.0)
    na_out[...] = out

    lam = _diff_lambda(lq_ref, lambda_init)
    tq = dqt_ref.shape[2]
    for hh in range(DIFF_HEADS):
        s = _dot(dk_ref[hh], _stack_qt(dqt_ref[hh]))
        p = jnp.exp2(s - jnp.max(s, axis=0, keepdims=True))
        acc = _dot(dvt_ref[hh], p.astype(BF16))
        df_out[hh] = _diff_finish_t(acc, tq, lam, nw_ref[...], lambda_init)


def _ctx_attn(nq, nk, nv, dqt, dk, dvt, lq, norm_w, na_o, df_o, *, batch, seq, ctx, lambda_init):
    blk0 = batch * seq // ctx
    s256 = pl.BlockSpec((ctx, 256), lambda b: (blk0 + b, 0))
    sh64 = pl.BlockSpec((DIFF_HEADS, ctx, 64), lambda b: (0, blk0 + b, 0))
    sq_t = pl.BlockSpec((DIFF_HEADS, 64, ctx), lambda b: (0, 0, blk0 + b))
    sv_t = pl.BlockSpec((DIFF_HEADS, DIFF_VROWS, ctx), lambda b: (0, 0, blk0 + b))
    return pl.pallas_call(
        functools.partial(_ctx_attn_body, lambda_init=lambda_init),
        grid=(batch,),
        in_specs=[s256, s256, s256, sq_t, sh64, sv_t, _resident(lq.shape), _resident((DIFF_DV, 1)),
                  pl.BlockSpec(memory_space=pl.ANY), pl.BlockSpec(memory_space=pl.ANY)],
        out_specs=[s256, sh64],
        out_shape=[jax.ShapeDtypeStruct(na_o.shape, F32), jax.ShapeDtypeStruct(df_o.shape, F32)],
        input_output_aliases={8: 0, 9: 1},
        compiler_params=_params("arbitrary"),
        name="ctx_attn",
    )(nq, nk, nv, dqt, dk, dvt, lq, norm_w.reshape(DIFF_DV, 1), na_o, df_o)


def _conv_body(u_ref, prev_ref, next_ref, dw_ref, dwb_ref, lng_ref, lnb_ref, pw_ref, pwb_ref,
               o_ref, pad_ref, sh_ref, *, tiles_per_seq, n_latent_tiles):
    t = u_ref.shape[0]
    i = pl.program_id(0)
    in_ctx = i >= n_latent_tiles
    first = in_ctx | (i % tiles_per_seq == 0)
    last = in_ctx | (i % tiles_per_seq == tiles_per_seq - 1)
    pad_ref[0:HALO, :] = jnp.where(first, 0.0, prev_ref[...])
    pad_ref[HALO:HALO + t, :] = u_ref[...]
    pad_ref[HALO + t:HALO + t + HALO, :] = jnp.where(last, 0.0, next_ref[...])
    base = HALO - CONV_K // 2
    first = {}
    for phase in range(8):
        taps = [k for k in range(CONV_K) if (base + k) % 8 == phase]
        first[phase] = base + taps[0]
        span = base + taps[-1] + t - first[phase]
        sh_ref[phase, 0:span, :] = pad_ref[first[phase]:first[phase] + span, :]
    sub = 64
    parts = []
    for r0 in range(0, t, sub):
        acc = jnp.zeros((sub, CONV_CH), F32)
        for k in range(CONV_K):
            phase = (base + k) % 8
            off = base + k - first[phase] + r0
            acc = acc + dw_ref[k:k + 1, :] * sh_ref[phase, off:off + sub, :]
        parts.append(acc)
    y = jnp.concatenate(parts, axis=0) + dwb_ref[...]
    mu = jnp.mean(y, axis=-1, keepdims=True)
    yc = y - mu
    var = jnp.mean(yc * yc, axis=-1, keepdims=True)
    y = _silu(yc * lax.rsqrt(var + LN_EPS) * lng_ref[...] + lnb_ref[...])
    o_ref[...] = _dot(y.astype(BF16), pw_ref[...]) + pwb_ref[...]


def _conv(cu, dw, dw_b, ln_g, ln_b, pw, pw_b, *, batch, seq, ctx):
    r = cu.shape[0]
    t = CONV_TILE
    assert ctx == t and seq % t == 0
    n_tiles = r // t
    hb = t // HALO
    vec = lambda a: a.reshape(1, CONV_CH)
    return pl.pallas_call(
        functools.partial(_conv_body, tiles_per_seq=seq // t, n_latent_tiles=batch * seq // t),
        grid=(n_tiles,),
        in_specs=[pl.BlockSpec((t, CONV_CH), lambda i: (i, 0)),
                  pl.BlockSpec((HALO, CONV_CH), lambda i: (jnp.maximum(i * hb - 1, 0), 0)),
                  pl.BlockSpec((HALO, CONV_CH), lambda i: (jnp.minimum((i + 1) * hb, n_tiles * hb - 1), 0)),
                  _resident((CONV_K, CONV_CH))] + [_resident((1, CONV_CH))] * 3
                 + [_resident((CONV_CH, CONV_CH)), _resident((1, CONV_CH))],
        out_specs=pl.BlockSpec((t, CONV_CH), lambda i: (i, 0)),
        out_shape=jax.ShapeDtypeStruct((r, CONV_CH), F32),
        scratch_shapes=[pltpu.VMEM((t + 2 * HALO, CONV_CH), F32),
                        pltpu.VMEM((8, t + 2 * HALO, CONV_CH), F32)],
        compiler_params=_params("parallel"),
        name="conv",
    )(cu, cu, cu, dw, vec(dw_b), vec(ln_g), vec(ln_b), pw.astype(BF16), vec(pw_b))


def kernel(x, c, ctx, c_ctx, ada_w, ada_b, norm_ffn1, ffn1_w13, ffn1_w2, norm_mix, w_in, gla_wa_f, gla_ba_f, gla_wa_b, gla_ba_b, gla_norm, na_rpb, diff_lq1, diff_lk1, diff_lq2, diff_lk2, diff_norm, conv_dw, conv_dw_b, conv_ln_g, conv_ln_b, conv_pw, conv_pw_b, w_out, norm_ffn2, ffn2_w13, ffn2_w2, final_norm):
    batch, seq, d = x.shape
    n_ctx = ctx.shape[1]
    depth = ada_w.shape[0]
    rows = seq // GRID_W
    tm = ROW_TILE
    assert seq % tm == 0 and (batch * n_ctx) % tm == 0 and batch + 1 <= 8
    lat_tiles = batch * seq // tm
    all_tiles = lat_tiles + batch * n_ctx // tm
    tiles_per_batch = seq // tm

    def group_of(i):
        return jnp.minimum(i // tiles_per_batch, batch)

    def pos_of(i):
        return jnp.where(i < lat_tiles, i % tiles_per_batch, tiles_per_batch)

    c_rows = jnp.concatenate([c, c_ctx[None, :], jnp.zeros((8 - batch - 1, d), F32)], axis=0)
    mods_all = _ada(c_rows, ada_w, ada_b)[:, :batch + 1].reshape(depth, batch + 1, N_MOD, d)
    rope = _rope_tables(seq, tm)
    h = x.reshape(batch * seq, d)
    h_ctx = ctx.reshape(batch * n_ctx, d)

    for i in range(depth):
        last = i == depth - 1
        lambda_init = 0.8 - 0.6 * math.exp(-0.3 * i)
        mods = mods_all[i]
        tok = dict(n_tiles=all_tiles, group_of=group_of)
        geo = dict(batch=batch, seq=seq, ctx=n_ctx)

        h = _ffn(h, mods, norm_ffn1[i], ffn1_w13[i], ffn1_w2[i], final_norm, k0=0, final=False,
                 h_ctx=h_ctx if i == 0 else None, **tok)

        wa, ba = _gate_weights(gla_wa_f[i], gla_ba_f[i], gla_wa_b[i], gla_ba_b[i])
        (gqk, gv, gvt, gg, glg, nq, nk, nv, dq, dk, dv, cu) = _proj(
            h, mods, norm_mix[i], _permute_w_in(w_in[i]), wa, ba, rope, pos_of=pos_of, **tok)

        gnorm = jnp.tile(gla_norm[i], GLA_HEADS).reshape(1, GLA_HEADS * GLA_DV)
        o_f = _gla(gqk, gv, gvt, glg, gg, None, gnorm, reverse=False, **geo)
        gx = _gla(gqk, gv, gvt, glg, gg, o_f, gnorm, reverse=True, **geo)

        nx = _na(nq, nk, nv, _na_col_tables(na_rpb[i]), **geo)

        lq = jnp.stack([diff_lq1[i], diff_lk1[i], diff_lq2[i], diff_lk2[i]])
        dx = _diff(dq, dk, dv, lq, diff_norm[i], lambda_init=lambda_init, **geo)
        if not last:
            nx, dx = _ctx_attn(nq, nk, nv, dq, dk, dv, lq, diff_norm[i], nx, dx,
                               lambda_init=lambda_init, **geo)

        cx = _conv(cu, conv_dw[i], conv_dw_b[i], conv_ln_g[i], conv_ln_b[i], conv_pw[i], conv_pw_b[i], **geo)

        if last:
            tok = dict(n_tiles=lat_tiles, group_of=group_of)
        h = _ffn(h, mods, norm_ffn2[i], ffn2_w13[i], ffn2_w2[i], final_norm, k0=6, final=last,
                 mixers=(gx, nx, dx, cx, w_out[i], (batch, seq, n_ctx)), **tok)

    return h.reshape(batch, seq, d)
```

```python
import functools
import math

import jax
import jax.numpy as jnp
from jax import lax
from jax.experimental import pallas as pl
from jax.experimental.pallas import tpu as pltpu

F32 = jnp.float32
BF16 = jnp.bfloat16

GRID_W = 64
N_MOD = 9
RMS_EPS = 1e-6
LN_EPS = 1e-5
NEG_INF = -1e30
GLA_HEADS, GLA_DK, GLA_DV, GLA_RANK, GLA_TAU, GLA_CHUNK = 4, 32, 64, 16, 16.0, 64
NA_HEADS, NA_DIM, NA_ROWS, NA_COLS = 4, 64, 8, 16
DIFF_HEADS, DIFF_DK, DIFF_DV = 4, 32, 64
DIFF_VROWS = 128
CONV_CH, CONV_K = 256, 31
ROPE_BASE = 10000.0
LOG2E = 1.4426950408889634

LANES = 128
VMEM_LIMIT = 56 * 1024 * 1024

ROW_TILE = 512
FF_CHUNK = 256
GLA_BLOCK = 256
NA_QROWS = 8
NA_KBLK = 256
CONV_TILE = 256
DIFF_TQ = 256
DIFF_TK = 512
DIFF_UNROLL = 8
DIFF_AHEAD = 3
DIFF_RING = 4
HALO = 16

C_GLA, C_NA, C_DIFF, C_CONV, C_AUX, C_END = 0, 768, 1536, 2304, 2816, 2944


def _dot(a, b):
    return jnp.dot(a, b, preferred_element_type=F32)


def _dot_nt(a, b):
    return lax.dot_general(a, b, (((1,), (1,)), ((), ())), preferred_element_type=F32)


def _params(*sem):
    return pltpu.CompilerParams(dimension_semantics=sem, vmem_limit_bytes=VMEM_LIMIT)


def _resident(shape):
    nd = len(shape)
    return pl.BlockSpec(shape, lambda *_: (0,) * nd, pipeline_mode=pl.Buffered(1))


def _silu(x):
    return x * jax.nn.sigmoid(x)


def _rms(x, w):
    return x * lax.rsqrt(jnp.mean(x * x, axis=-1, keepdims=True) + RMS_EPS) * w


def _ada_body(c_ref, w_ref, b_ref, o_ref):
    s = _silu(c_ref[...])
    o_ref[...] = jnp.dot(s, w_ref[...], precision=lax.Precision.HIGHEST,
                         preferred_element_type=F32) + b_ref[...]


def _ada(c_rows, ada_w, ada_b):
    depth, d, _ = ada_w.shape
    return pl.pallas_call(
        _ada_body,
        grid=(depth, N_MOD),
        in_specs=[pl.BlockSpec((8, d), lambda l, n: (0, 0)),
                  pl.BlockSpec((None, d, d), lambda l, n: (l, 0, n)),
                  pl.BlockSpec((None, 1, d), lambda l, n: (l, 0, n))],
        out_specs=pl.BlockSpec((None, 8, d), lambda l, n: (l, 0, n)),
        out_shape=jax.ShapeDtypeStruct((depth, 8, N_MOD * d), F32),
        compiler_params=_params("arbitrary", "arbitrary"),
        name="ada",
    )(c_rows, ada_w, ada_b.reshape(depth, 1, N_MOD * d))


def _ffn_body(*refs, k0, final, source, n_first):
    h_ref = refs[0]
    n_extra = {"plain": 0, "two_arrays": 1, "mixers": 6}[source]
    extra = refs[1:1 + n_extra]
    mod_ref, nw_ref, w13_ref, w2_ref, fw_ref, o_ref, xb_ref, g_ref, h13_ref = refs[1 + n_extra:]
    ff = w2_ref.shape[0]
    tf = h13_ref.shape[2] // 2
    n_chunks = ff // tf
    x = h_ref[...]
    if source == "two_arrays":
        x = jnp.where(pl.program_id(0) < n_first, x, extra[0][...])
    elif source == "mixers":
        gx_ref, gxc_ref, nx_ref, dx_ref, cx_ref, wo_ref = extra
        gx = jnp.where(pl.program_id(0) < n_first, gx_ref[...], gxc_ref[...].reshape(gx_ref.shape))
        mix = jnp.concatenate([gx, nx_ref[...]] + [dx_ref[hh] for hh in range(DIFF_HEADS)]
                              + [cx_ref[...]], axis=-1).astype(BF16)
        x = x + mod_ref[5:6, :] * _dot(mix, wo_ref[...])
    xm = _rms(x, nw_ref[...]) * (1.0 + mod_ref[k0 + 1:k0 + 2, :]) + mod_ref[k0:k0 + 1, :]
    xb_ref[...] = xm.astype(BF16)

    def cols(c):
        return c * tf if isinstance(c, int) else pl.multiple_of(c * tf, tf)

    def up(c, slot):
        h13_ref[slot, :, :tf] = _dot(xb_ref[...], w13_ref[:, pl.ds(cols(c), tf)])
        h13_ref[slot, :, tf:] = _dot(xb_ref[...], w13_ref[:, pl.ds(ff + cols(c), tf)])

    def gate(c, slot):
        a = h13_ref[slot, :, :tf]
        u = h13_ref[slot, :, tf:]
        g_ref[:, pl.ds(cols(c), tf)] = (_silu(a) * u).astype(BF16)

    def pair(t, carry):
        up(2 * t + 1, 1)
        gate(2 * t, 0)
        up(2 * t + 2, 0)
        gate(2 * t + 1, 1)
        return carry

    up(0, 0)
    lax.fori_loop(0, (n_chunks - 1) // 2, pair, 0)
    gate(n_chunks - 1, 0)
    out = x + (0.5 * mod_ref[k0 + 2:k0 + 3, :]) * _dot(g_ref[...], w2_ref[...])
    if final:
        out = _rms(out, fw_ref[...])
    o_ref[...] = out


def _ffn(h, mods, norm_w, w13, w2, final_w, *, k0, n_tiles, group_of, final, h_ctx=None, mixers=None):
    d = h.shape[1]
    tm = ROW_TILE
    ff = w2.shape[0]
    n_chunks = ff // FF_CHUNK
    assert ff % FF_CHUNK == 0 and n_chunks % 2 == 1
    row = lambda i: (i, 0)
    source, n_first, extra, extra_specs, h_spec = "plain", None, [], [], pl.BlockSpec((tm, d), row)
    if h_ctx is not None:
        source, n_first, extra = "two_arrays", h.shape[0] // tm, [h_ctx]
        h_spec = pl.BlockSpec((tm, d), lambda i: (jnp.minimum(i, n_first - 1), 0))
        extra_specs = [pl.BlockSpec((tm, d), lambda i: (jnp.maximum(i - n_first, 0), 0))]
    elif mixers is not None:
        gx, nx, dx, cx, w_out, (batch, seq, n_ctx) = mixers
        assert batch * n_ctx == tm and seq % tm == 0 and seq % n_ctx == 0
        source, n_first, extra = "mixers", batch * seq // tm, [gx, gx, nx, dx, cx, w_out.astype(BF16)]
        per_seq = seq // tm
        extra_specs = [pl.BlockSpec((None, tm, 256),
                                    lambda i: (jnp.minimum(i // per_seq, batch - 1), i % per_seq, 0)),
                       pl.BlockSpec((batch, n_ctx, 256), lambda i: (0, seq // n_ctx, 0)),
                       pl.BlockSpec((tm, 256), row),
                       pl.BlockSpec((DIFF_HEADS, tm, 64), lambda i: (0, i, 0)),
                       pl.BlockSpec((tm, 256), row), _resident(w_out.shape)]
    return pl.pallas_call(
        functools.partial(_ffn_body, k0=k0, final=final, source=source, n_first=n_first),
        grid=(n_tiles,),
        in_specs=[h_spec] + extra_specs + [
            pl.BlockSpec((None, N_MOD, d), lambda i: (group_of(i), 0, 0)),
            _resident((1, d)),
            _resident(w13.shape),
            _resident(w2.shape),
            _resident((1, d))],
        out_specs=pl.BlockSpec((tm, d), row),
        out_shape=jax.ShapeDtypeStruct((n_tiles * tm, d), F32),
        scratch_shapes=[pltpu.VMEM((tm, d), BF16), pltpu.VMEM((tm, ff), BF16),
                        pltpu.VMEM((2, tm, 2 * FF_CHUNK), F32)],
        compiler_params=_params("parallel"),
        name="ffn",
    )(h, *extra, mods, norm_w.reshape(1, d), w13.astype(BF16), w2.astype(BF16), final_w.reshape(1, d))


def _log_sigmoid(x):
    return jnp.minimum(x, 0.0) - jnp.log1p(jnp.exp(-jnp.abs(x)))


def _rope_rotate(x):
    n = x.shape[-1]
    lane = lax.broadcasted_iota(jnp.int32, x.shape, 1)
    up = pltpu.roll(x, n - 8, 1)
    dn = pltpu.roll(x, 8, 1)
    return jnp.where((lane & 15) < 8, -up, dn)


def _rope_rotate_rows(x):
    n = x.shape[0]
    row = lax.broadcasted_iota(jnp.int32, x.shape, 0)
    up = pltpu.roll(x, n - 8, 0)
    dn = pltpu.roll(x, 8, 0)
    return jnp.where((row & 15) < 8, -up, dn)


def _proj_body(h_ref, mod_ref, nw_ref, w_ref, wvt_ref, wdqt_ref, wdvt_ref, wa_ref, ba_ref,
               cos_ref, sin_ref, cost_ref, sint_ref,
               gqk_ref, gv_ref, gvt_ref, gg_ref, glg_ref, nq_ref, nk_ref, nv_ref,
               dqt_ref, dk_ref, dvt_ref, cu_ref, xb_ref):
    x = h_ref[...]
    tm = x.shape[0]
    xm = _rms(x, nw_ref[...]) * (1.0 + mod_ref[4:5, :]) + mod_ref[3:4, :]
    xb_ref[...] = xm.astype(BF16)

    z = _dot(xb_ref[...], w_ref[:, C_GLA:C_GLA + 256])
    lane = lax.broadcasted_iota(jnp.int32, (1, 256), 1)
    gqk_ref[...] = z * jnp.where(lane < 128, GLA_DK ** -0.5, 1.0)
    gv_ref[...] = _dot(xb_ref[...], w_ref[:, C_GLA + 256:C_GLA + 512])
    gvt_ref[...] = _dot_nt(wvt_ref[...], xb_ref[...])
    gg_ref[...] = _dot(xb_ref[...], w_ref[:, C_GLA + 512:C_GLA + 768])
    aux = _dot(xb_ref[...], w_ref[:, C_AUX:C_END])
    pre = _dot(aux.astype(BF16), wa_ref[...]) + ba_ref[...]
    glg_ref[...] = _log_sigmoid(pre) * (1.0 / GLA_TAU)

    nq_ref[...] = (_dot(xb_ref[...], w_ref[:, C_NA:C_NA + 256]) * (NA_DIM ** -0.5)).astype(BF16)
    nk_ref[...] = _dot(xb_ref[...], w_ref[:, C_NA + 256:C_NA + 512]).astype(BF16)
    nv_ref[...] = _dot(xb_ref[...], w_ref[:, C_NA + 512:C_NA + 768]).astype(BF16)

    cos = cos_ref[...]
    sin = sin_ref[...]
    cos2 = jnp.concatenate([cos, cos], axis=1)
    sin2 = jnp.concatenate([sin, sin], axis=1)
    zk = _dot(xb_ref[...], w_ref[:, C_DIFF + 256:C_DIFF + 512])
    zk = zk * cos2 + _rope_rotate(zk) * sin2
    n_rep = 2 * DIFF_HEADS
    cos_t = jnp.concatenate([cost_ref[...]] * n_rep, axis=0)
    sin_t = jnp.concatenate([sint_ref[...]] * n_rep, axis=0)
    zqt = _dot_nt(wdqt_ref[...], xb_ref[...])
    zqt = (zqt * cos_t + _rope_rotate_rows(zqt) * sin_t) * (DIFF_DK ** -0.5 * LOG2E)
    zvt = _dot_nt(wdvt_ref[...], xb_ref[...])
    pad_rows = DIFF_VROWS - DIFF_DV
    one_row = jnp.where(lax.broadcasted_iota(jnp.int32, (pad_rows, tm), 0) == 0, 1.0, 0.0).astype(BF16)
    for hh in range(DIFF_HEADS):
        sl = slice(64 * hh, 64 * hh + 64)
        dk_ref[hh] = zk[:, sl].astype(BF16)
        dqt_ref[hh] = zqt[sl, :].astype(BF16)
        dvt_ref[hh, 0:64, :] = zvt[sl, :].astype(BF16)
        dvt_ref[hh, DIFF_DV:DIFF_VROWS, :] = one_row

    za = _dot(xb_ref[...], w_ref[:, C_CONV:C_CONV + 256])
    zg = _dot(xb_ref[...], w_ref[:, C_CONV + 256:C_CONV + 512])
    cu_ref[...] = za * jax.nn.sigmoid(zg)


def _proj(h, mods, norm_w, w_p, wa, ba, rope, *, n_tiles, group_of, pos_of):
    r, d = h.shape
    tm = ROW_TILE
    row = lambda i: (i, 0)
    hrow = lambda i: (0, i, 0)
    hcol = lambda i: (0, 0, i)
    f32_256 = jax.ShapeDtypeStruct((r, 256), F32)
    bf_256 = jax.ShapeDtypeStruct((r, 256), BF16)
    wvt = w_p[:, C_GLA + 256:C_GLA + 512].T
    wdqt = w_p[:, C_DIFF:C_DIFF + 256].T
    wdvt = w_p[:, C_DIFF + 512:C_DIFF + 768].T
    cos_r, sin_r, cos_c, sin_c = rope
    out_shape = [f32_256, f32_256, jax.ShapeDtypeStruct((256, r), F32), f32_256, f32_256,
                 bf_256, bf_256, bf_256,
                 jax.ShapeDtypeStruct((DIFF_HEADS, 64, r), BF16),
                 jax.ShapeDtypeStruct((DIFF_HEADS, r, 64), BF16),
                 jax.ShapeDtypeStruct((DIFF_HEADS, DIFF_VROWS, r), BF16),
                 f32_256]
    out_specs = [pl.BlockSpec((tm, 256), row)] * 2 + [pl.BlockSpec((256, tm), lambda i: (0, i))] + [
        pl.BlockSpec((tm, 256), row)] * 5 + [
        pl.BlockSpec((DIFF_HEADS, 64, tm), hcol),
        pl.BlockSpec((DIFF_HEADS, tm, 64), hrow),
        pl.BlockSpec((DIFF_HEADS, DIFF_VROWS, tm), hcol),
        pl.BlockSpec((tm, 256), row)]
    return pl.pallas_call(
        _proj_body,
        grid=(n_tiles,),
        in_specs=[pl.BlockSpec((tm, d), row),
                  pl.BlockSpec((None, N_MOD, d), lambda i: (group_of(i), 0, 0)),
                  _resident((1, d)),
                  _resident(w_p.shape),
                  _resident(wvt.shape),
                  _resident(wdqt.shape),
                  _resident(wdvt.shape),
                  _resident(wa.shape),
                  _resident(ba.shape),
                  pl.BlockSpec((tm, LANES), lambda i: (pos_of(i), 0)),
                  pl.BlockSpec((tm, LANES), lambda i: (pos_of(i), 0)),
                  pl.BlockSpec((DIFF_DK, tm), lambda i: (0, pos_of(i))),
                  pl.BlockSpec((DIFF_DK, tm), lambda i: (0, pos_of(i)))],
        out_specs=out_specs,
        out_shape=out_shape,
        scratch_shapes=[pltpu.VMEM((tm, d), BF16)],
        compiler_params=_params("parallel"),
        name="proj",
    )(h, mods, norm_w.reshape(1, d), w_p, wvt, wdqt, wdvt, wa, ba, cos_r, sin_r, cos_c, sin_c)


def _permute_w_in(w_in):
    d = w_in.shape[0]
    g0 = 2 * GLA_HEADS * GLA_DK + 2 * GLA_HEADS * GLA_DV
    aux = w_in[:, g0:g0 + 2 * GLA_RANK]
    rest = w_in[:, g0 + 2 * GLA_RANK:]
    pad = jnp.zeros((d, C_END - C_AUX - 2 * GLA_RANK), w_in.dtype)
    return jnp.concatenate([w_in[:, :g0], rest, aux, pad], axis=1).astype(BF16)


def _gate_weights(wa_f, ba_f, wa_b, ba_b):
    n = GLA_HEADS * GLA_DK
    wa = jnp.zeros((C_END - C_AUX, 2 * n), F32)
    wa = wa.at[:GLA_RANK, :n].set(wa_f).at[GLA_RANK:2 * GLA_RANK, n:].set(wa_b)
    return wa.astype(BF16), jnp.concatenate([ba_f, ba_b]).reshape(1, 2 * n)


def _rope_tables(seq, tile):
    t = jnp.arange(seq)
    row = (t // GRID_W).astype(F32)
    col = (t % GRID_W).astype(F32)
    half = DIFF_DK // 2
    inv = 1.0 / (ROPE_BASE ** (jnp.arange(0, half, 2, dtype=F32) / half))
    ang_r = row[:, None] * inv
    ang_c = col[:, None] * inv
    ang = jnp.concatenate([ang_r, ang_r, ang_c, ang_c], axis=-1)
    cos = jnp.concatenate([jnp.cos(ang), jnp.ones((tile, DIFF_DK), F32)], axis=0)
    sin = jnp.concatenate([jnp.sin(ang), jnp.zeros((tile, DIFF_DK), F32)], axis=0)
    rep = (1, LANES // DIFF_DK)
    return jnp.tile(cos, rep), jnp.tile(sin, rep), cos.T, sin.T


def _split3(x):
    hi = x.astype(BF16)
    r1 = x - hi.astype(F32)
    mid = r1.astype(BF16)
    lo = (r1 - mid.astype(F32)).astype(BF16)
    return hi, mid, lo


def _gla_body(*refs, reverse, batch):
    per_b = [refs[5 * b:5 * b + 5] for b in range(batch)]
    of_ref, nw_ref, o_ref, st_ref = refs[5 * batch:]
    blk = per_b[0][0].shape[0]
    n_chunks = blk // GLA_CHUNK
    nqk = GLA_HEADS * GLA_DK
    nv = GLA_HEADS * GLA_DV

    @pl.when(pl.program_id(0) == 0)
    def _():
        st_ref[...] = jnp.zeros_like(st_ref)

    ri = lax.broadcasted_iota(jnp.int32, (blk, blk), 0)
    ci = lax.broadcasted_iota(jnp.int32, (blk, blk), 1)
    same = (ri // GLA_CHUNK) == (ci // GLA_CHUNK)
    tri = jnp.where(same & ((ci >= ri) if reverse else (ci <= ri)), 1.0, 0.0).astype(BF16)
    bcum_of = []
    for qk_ref, v_ref, vt_ref, lg_ref, g_ref in per_b:
        lg = lg_ref[:, nqk:2 * nqk] if reverse else lg_ref[:, 0:nqk]
        hi, mid, lo = _split3(lg)
        bcum_of.append(_dot(tri, hi) + _dot(tri, mid) + _dot(tri, lo))

    c = GLA_CHUNK
    rk = lax.broadcasted_iota(jnp.int32, (GLA_HEADS * c, nqk), 0) // c
    ck = lax.broadcasted_iota(jnp.int32, (GLA_HEADS * c, nqk), 1) // GLA_DK
    mask_k = rk == ck
    rv = lax.broadcasted_iota(jnp.int32, (GLA_HEADS * c, nv), 0) // c
    cv = lax.broadcasted_iota(jnp.int32, (GLA_HEADS * c, nv), 1) // GLA_DV
    mask_v = rv == cv
    rs = lax.broadcasted_iota(jnp.int32, (nv, nqk), 0) // GLA_DV
    cs = lax.broadcasted_iota(jnp.int32, (nv, nqk), 1) // GLA_DK
    mask_s = rs == cs
    ai = lax.broadcasted_iota(jnp.int32, (c, GLA_HEADS * c), 0)
    aj = lax.broadcasted_iota(jnp.int32, (c, GLA_HEADS * c), 1) % c
    mask_a = (aj >= ai) if reverse else (aj <= ai)

    order = range(n_chunks - 1, -1, -1) if reverse else range(n_chunks)
    units = [(b, ch) for ch in order for b in range(batch)]
    q_in, a_raw, u_t, decay = {}, {}, {}, {}
    vts = [refs_b[2][...].astype(BF16) for refs_b in per_b]
    for b, ch in units:
        qk_ref = per_b[b][0]
        rows = slice(ch * c, (ch + 1) * c)
        k = qk_ref[rows, nqk:2 * nqk]
        bcum = bcum_of[b][rows, :]
        btot = bcum[0:1, :] if reverse else bcum[c - 1:c, :]
        q_in[b, ch] = (qk_ref[rows, 0:nqk] * jnp.exp(bcum)).astype(BF16)
        k_in = k * jnp.exp(-bcum)
        k_out = (k * jnp.exp(btot - bcum)).astype(BF16)
        k_bd = jnp.where(mask_k, jnp.concatenate([k_in] * GLA_HEADS, axis=0), 0.0).astype(BF16)
        a_raw[b, ch] = _dot_nt(q_in[b, ch], k_bd)
        k_pad = jnp.concatenate([jnp.zeros((n * c, nqk), BF16) for n in (ch,) if n] + [k_out]
                                + [jnp.zeros((n * c, nqk), BF16) for n in (n_chunks - 1 - ch,) if n], axis=0)
        u_t[b, ch] = jnp.where(mask_s, _dot(vts[b], k_pad), 0.0)
        decay[b, ch] = jnp.exp(btot)
    o = {}
    for b, ch in units:
        v = per_b[b][1][ch * c:(ch + 1) * c, :]
        v_bd = jnp.where(mask_v, jnp.concatenate([v] * GLA_HEADS, axis=0), 0.0).astype(BF16)
        o[b, ch] = _dot(jnp.where(mask_a, a_raw[b, ch], 0.0).astype(BF16), v_bd)
    st = [st_ref[b] for b in range(batch)]
    for b, ch in units:
        o[b, ch] = o[b, ch] + _dot_nt(q_in[b, ch], st[b].astype(BF16))
        st[b] = st[b] * decay[b, ch] + u_t[b, ch]
    for b in range(batch):
        st_ref[b] = st[b]
        o_blk = jnp.concatenate([o[b, ch] for ch in range(n_chunks)], axis=0)
        if reverse:
            o_blk = o_blk + of_ref[b]
            hi2, lo2, _ = _split3(o_blk * o_blk)
            hr = lax.broadcasted_iota(jnp.int32, (nv, nv), 0) // GLA_DV
            hc = lax.broadcasted_iota(jnp.int32, (nv, nv), 1) // GLA_DV
            seg = jnp.where(hr == hc, 1.0, 0.0).astype(BF16)
            ms = (_dot(hi2, seg) + _dot(lo2, seg)) * (1.0 / GLA_DV)
            o_blk = o_blk * lax.rsqrt(ms + RMS_EPS) * nw_ref[...] * _silu(per_b[b][4][...])
        o_ref[b] = o_blk


def _gla(gqk, gv, gvt, glg, gg, o_f, norm_w4, *, reverse, batch, seq, ctx):
    blk = GLA_BLOCK
    nc, nl = ctx // blk, seq // blk
    ctx_base = batch * seq // blk

    def step_blk(s):
        if reverse:
            return jnp.where(s < nc, nl + (nc - 1 - s), nl - 1 - (s - nc))
        return jnp.where(s < nc, nl + s, s - nc)

    def row_blk(b):
        return lambda s: jnp.where(s < nc, ctx_base + b * nc - nl, b * nl) + step_blk(s)

    specs, args = [], []
    for b in range(batch):
        spec = pl.BlockSpec((blk, 256), lambda s, f=row_blk(b): (f(s), 0))
        spec_t = pl.BlockSpec((256, blk), lambda s, f=row_blk(b): (0, f(s)))
        specs += [spec, spec, spec_t, spec, spec]
        args += [gqk, gv, gvt, glg, gg]
    seq_spec = pl.BlockSpec((batch, blk, 256), lambda s: (0, step_blk(s), 0))
    if o_f is None:
        o_f, of_spec = norm_w4, _resident((1, 256))
    else:
        of_spec = seq_spec
    return pl.pallas_call(
        functools.partial(_gla_body, reverse=reverse, batch=batch),
        grid=(nc + nl,),
        in_specs=specs + [of_spec, _resident((1, 256))],
        out_specs=seq_spec,
        out_shape=jax.ShapeDtypeStruct((batch, seq + ctx, 256), F32),
        scratch_shapes=[pltpu.VMEM((batch, GLA_HEADS * GLA_DV, GLA_HEADS * GLA_DK), F32)],
        compiler_params=_params("arbitrary"),
        name="gla_bwd" if reverse else "gla_fwd",
    )(*args, o_f, norm_w4)


def _na_col_tables(rpb):
    col = jnp.arange(GRID_W)
    cstart = jnp.clip(col - NA_COLS // 2, 0, GRID_W - NA_COLS)
    col_ok = (col[None, :] >= cstart[:, None]) & (col[None, :] < cstart[:, None] + NA_COLS)
    col_off = jnp.clip(col[None, :] - col[:, None] + (NA_COLS - 1), 0, 2 * NA_COLS - 2)
    t = jnp.where(col_ok[None, None], rpb[:, :, col_off], NEG_INF)
    dead = jnp.full((NA_HEADS, 1, GRID_W, GRID_W), NEG_INF, F32)
    t = jnp.concatenate([dead, t.astype(F32), dead], axis=1)
    return jnp.concatenate([t[:, :-1], t[:, 1:]], axis=-1)


def _na_body(q_ref, k0, k1, k2, k3, v0, v1, v2, v3, kc_ref, vc_ref, tab_ref, o_ref, *, rows, key_blocks):
    q = q_ref[...]
    kw = jnp.concatenate([k0[...], k1[...], k2[...], k3[...]], axis=0)
    vw = jnp.concatenate([v0[...], v1[...], v2[...], v3[...]], axis=0)
    kc = kc_ref[...]
    vc = vc_ref[...]

    j = pl.program_id(1)
    rows_per_blk = NA_KBLK // GRID_W
    r0 = j * NA_QROWS
    kr0 = jnp.clip(2 * j - 1, 0, key_blocks - 4) * rows_per_blk
    lane_lo = lax.broadcasted_iota(jnp.int32, (1, 2 * GRID_W), 1) < GRID_W
    n_pairs = 4 * rows_per_blk // 2
    entry, ok = [], []
    for a in range(NA_QROWS):
        r = r0 + a
        start = jnp.clip(r - NA_ROWS // 2, 0, rows - NA_ROWS)
        for bp in range(n_pairs):
            rk = kr0 + 2 * bp
            entry.append(jnp.clip(rk - r + NA_ROWS, 0, 2 * NA_ROWS - 1))
            in0 = ((rk >= start) & (rk < start + NA_ROWS)).astype(jnp.int32)
            in1 = ((rk + 1 >= start) & (rk + 1 < start + NA_ROWS)).astype(jnp.int32)
            ok.append(jnp.where(lane_lo, in0, in1) != 0)

    def bias_of(hh):
        rows_ = []
        for a in range(NA_QROWS):
            tiles = [jnp.where(ok[a * n_pairs + bp], tab_ref[hh, entry[a * n_pairs + bp]], NEG_INF)
                     for bp in range(n_pairs)]
            rows_.append(jnp.concatenate(tiles, axis=1))
        return jnp.concatenate(rows_, axis=0)

    lane = lax.broadcasted_iota(jnp.int32, (1, NA_HEADS * NA_DIM), 1) // NA_DIM
    out = jnp.zeros(o_ref.shape, F32)
    for hh in range(NA_HEADS):
        mh = lane == hh
        qh = jnp.where(mh, q, jnp.zeros_like(q))
        s_w = _dot_nt(qh, kw) + bias_of(hh)
        s_c = _dot_nt(qh, kc)
        m = jnp.maximum(jnp.max(s_w, axis=-1, keepdims=True), jnp.max(s_c, axis=-1, keepdims=True))
        p_w = jnp.exp(s_w - m)
        p_c = jnp.exp(s_c - m)
        l = jnp.sum(p_w, axis=-1, keepdims=True) + jnp.sum(p_c, axis=-1, keepdims=True)
        o = _dot(p_w.astype(BF16), vw) + _dot(p_c.astype(BF16), vc)
        out = out + jnp.where(mh, o / l, 0.0)
    o_ref[...] = out


def _na(nq, nk, nv, tab, *, batch, seq, ctx):
    r = nq.shape[0]
    tq = NA_QROWS * GRID_W
    nj = seq // tq
    kb = seq // NA_KBLK
    assert kb >= 4 and seq % tq == 0 and seq % ctx == 0
    ctx_blk0 = batch * seq // ctx

    def kspec(i):
        return pl.BlockSpec((NA_KBLK, 256),
                            lambda b, j: (b * kb + jnp.clip(2 * j - 1, 0, kb - 4) + i, 0))

    cspec = pl.BlockSpec((ctx, 256), lambda b, j: (ctx_blk0 + b, 0))
    return pl.pallas_call(
        functools.partial(_na_body, rows=seq // GRID_W, key_blocks=kb),
        grid=(batch, nj),
        in_specs=[pl.BlockSpec((tq, 256), lambda b, j: (b * nj + j, 0))]
                 + [kspec(i) for i in range(4)] + [kspec(i) for i in range(4)]
                 + [cspec, cspec, _resident(tab.shape)],
        out_specs=pl.BlockSpec((tq, 256), lambda b, j: (b * nj + j, 0)),
        out_shape=jax.ShapeDtypeStruct((r, 256), F32),
        compiler_params=_params("parallel", "arbitrary"),
        name="na",
    )(nq, nk, nk, nk, nk, nv, nv, nv, nv, nk, nv, tab)


def _diff_lambda(lq_ref, lambda_init):
    lq = lq_ref[...]
    s1 = jnp.sum(lq[0:1, :] * lq[1:2, :], axis=-1, keepdims=True)
    s2 = jnp.sum(lq[2:3, :] * lq[3:4, :], axis=-1, keepdims=True)
    return jnp.exp(s1) - jnp.exp(s2) + lambda_init


def _stack_qt(qt):
    row = lax.broadcasted_iota(jnp.int32, (2 * DIFF_DK, 1), 0)
    zero = jnp.zeros_like(qt)
    return jnp.concatenate([jnp.where(row < DIFF_DK, qt, zero), jnp.where(row >= DIFF_DK, qt, zero)], axis=1)


def _diff_finish_t(acc, tq, lam, nw_col, lambda_init):
    o = acc[0:DIFF_DV, :] / acc[DIFF_DV:DIFF_DV + 1, :]
    od = o[:, :tq] - lam * o[:, tq:]
    ms = jnp.mean(od * od, axis=0, keepdims=True)
    y = od * lax.rsqrt(ms + RMS_EPS) * nw_col * (1.0 - lambda_init)
    y = jnp.concatenate([y, jnp.zeros_like(y)], axis=0)
    return jnp.transpose(y)[:, 0:DIFF_DV]


def _diff_body(qt_ref, kc_ref, vtc_ref, k_ref, vt_ref, lq_ref, nw_ref, o_ref, s_ref, *, tk, unroll, lambda_init):
    tq = qt_ref.shape[1]
    n_chunks = k_ref.shape[0] // tk
    ring = s_ref.shape[0]
    qs = _stack_qt(qt_ref[...])

    def chunk(j):
        return pl.ds(j * tk if isinstance(j, int) else pl.multiple_of(j * tk, tk), tk)

    def issue_scores(j, slot):
        s = _dot(k_ref[chunk(j), :], qs)
        s_ref[slot] = s
        return jnp.max(s, axis=0, keepdims=True)

    s_ctx = _dot(kc_ref[...], qs)
    m0 = jnp.max(s_ctx, axis=0, keepdims=True)
    cmax0 = tuple(issue_scores(min(a, n_chunks - 1), a % ring) for a in range(DIFF_AHEAD))
    acc0 = _dot(vtc_ref[...], jnp.exp2(s_ctx - m0).astype(BF16))

    def group(g, carry, tail=False):
        cmax, m, acc = carry
        for u in range(unroll):
            j = g * unroll + u
            if tail and j + DIFF_AHEAD >= n_chunks:
                c_new = cmax[0]
            else:
                c_new = issue_scores(j + DIFF_AHEAD, (u + DIFF_AHEAD) % ring)
            m_new = jnp.maximum(m, cmax[0])
            p = jnp.exp2(s_ref[u % ring] - m_new).astype(BF16)
            vt = vt_ref[:, chunk(j)]
            acc = jnp.exp2(m - m_new) * acc + _dot(vt, p)
            m, cmax = m_new, cmax[1:] + (c_new,)
        return cmax, m, acc

    n_groups = n_chunks // unroll
    carry = lax.fori_loop(0, n_groups - 1, group, (cmax0, m0, acc0))
    _, _, acc = group(n_groups - 1, carry, tail=True)
    o_ref[...] = _diff_finish_t(acc, tq, _diff_lambda(lq_ref, lambda_init), nw_ref[...], lambda_init)


def _diff(dqt, dk, dvt, lq, norm_w, *, batch, seq, ctx, lambda_init):
    h, r, _ = dk.shape
    tq = DIFF_TQ
    nq = seq // tq
    tk = min(DIFF_TK, seq)
    unroll = min(DIFF_UNROLL, seq // tk)
    assert (seq // tk) % unroll == 0 and unroll % DIFF_RING == 0 and DIFF_RING > DIFF_AHEAD
    ctx_blk0 = batch * seq // ctx
    return pl.pallas_call(
        functools.partial(_diff_body, tk=tk, unroll=unroll, lambda_init=lambda_init),
        grid=(batch, h, nq),
        in_specs=[pl.BlockSpec((None, 64, tq), lambda b, hh, i: (hh, 0, b * nq + i)),
                  pl.BlockSpec((None, ctx, 64), lambda b, hh, i: (hh, ctx_blk0 + b, 0)),
                  pl.BlockSpec((None, DIFF_VROWS, ctx), lambda b, hh, i: (hh, 0, ctx_blk0 + b)),
                  pl.BlockSpec((None, seq, 64), lambda b, hh, i: (hh, b, 0)),
                  pl.BlockSpec((None, DIFF_VROWS, seq), lambda b, hh, i: (hh, 0, b)),
                  _resident(lq.shape),
                  _resident((DIFF_DV, 1))],
        out_specs=pl.BlockSpec((None, tq, 64), lambda b, hh, i: (hh, b * nq + i, 0)),
        out_shape=jax.ShapeDtypeStruct((h, r, 64), F32),
        scratch_shapes=[pltpu.VMEM((DIFF_RING, tk, 2 * tq), F32)],
        compiler_params=_params("parallel", "parallel", "arbitrary"),
        name="diff",
    )(dqt, dk, dvt, dk, dvt, lq, norm_w.reshape(DIFF_DV, 1))


def _ctx_attn_body(nq_ref, nk_ref, nv_ref, dqt_ref, dk_ref, dvt_ref, lq_ref, nw_ref, na_in, df_in,
                   na_out, df_out, *, lambda_init):
    del na_in, df_in
    q = nq_ref[...]
    k = nk_ref[...]
    v = nv_ref[...]
    lane = lax.broadcasted_iota(jnp.int32, (1, NA_HEADS * NA_DIM), 1) // NA_DIM
    out = jnp.zeros(na_out.shape, F32)
    for hh in range(NA_HEADS):
        mh = lane == hh
        s = _dot_nt(jnp.where(mh, q, jnp.zeros_like(q)), k)
        p = jnp.exp(s - jnp.max(s, axis=-1, keepdims=True))
        o = _dot(p.astype(BF16), v) / jnp.sum(p, axis=-1, keepdims=True)
        out = out + jnp.where(mh, o, 0.0)
    na_out[...] = out

    lam = _diff_lambda(lq_ref, lambda_init)
    tq = dqt_ref.shape[2]
    for hh in range(DIFF_HEADS):
        s = _dot(dk_ref[hh], _stack_qt(dqt_ref[hh]))
        p = jnp.exp2(s - jnp.max(s, axis=0, keepdims=True))
        acc = _dot(dvt_ref[hh], p.astype(BF16))
        df_out[hh] = _diff_finish_t(acc, tq, lam, nw_ref[...], lambda_init)


def _ctx_attn(nq, nk, nv, dqt, dk, dvt, lq, norm_w, na_o, df_o, *, batch, seq, ctx, lambda_init):
    blk0 = batch * seq // ctx
    s256 = pl.BlockSpec((ctx, 256), lambda b: (blk0 + b, 0))
    sh64 = pl.BlockSpec((DIFF_HEADS, ctx, 64), lambda b: (0, blk0 + b, 0))
    sq_t = pl.BlockSpec((DIFF_HEADS, 64, ctx), lambda b: (0, 0, blk0 + b))
    sv_t = pl.BlockSpec((DIFF_HEADS, DIFF_VROWS, ctx), lambda b: (0, 0, blk0 + b))
    return pl.pallas_call(
        functools.partial(_ctx_attn_body, lambda_init=lambda_init),
        grid=(batch,),
        in_specs=[s256, s256, s256, sq_t, sh64, sv_t, _resident(lq.shape), _resident((DIFF_DV, 1)),
                  pl.BlockSpec(memory_space=pl.ANY), pl.BlockSpec(memory_space=pl.ANY)],
        out_specs=[s256, sh64],
        out_shape=[jax.ShapeDtypeStruct(na_o.shape, F32), jax.ShapeDtypeStruct(df_o.shape, F32)],
        input_output_aliases={8: 0, 9: 1},
        compiler_params=_params("arbitrary"),
        name="ctx_attn",
    )(nq, nk, nv, dqt, dk, dvt, lq, norm_w.reshape(DIFF_DV, 1), na_o, df_o)


def _conv_body(u_ref, prev_ref, next_ref, dw_ref, dwb_ref, lng_ref, lnb_ref, pw_ref, pwb_ref,
               o_ref, pad_ref, sh_ref, *, tiles_per_seq, n_latent_tiles):
    t = u_ref.shape[0]
    i = pl.program_id(0)
    in_ctx = i >= n_latent_tiles
    first = in_ctx | (i % tiles_per_seq == 0)
    last = in_ctx | (i % tiles_per_seq == tiles_per_seq - 1)
    pad_ref[0:HALO, :] = jnp.where(first, 0.0, prev_ref[...])
    pad_ref[HALO:HALO + t, :] = u_ref[...]
    pad_ref[HALO + t:HALO + t + HALO, :] = jnp.where(last, 0.0, next_ref[...])
    base = HALO - CONV_K // 2
    first = {}
    for phase in range(8):
        taps = [k for k in range(CONV_K) if (base + k) % 8 == phase]
        first[phase] = base + taps[0]
        span = base + taps[-1] + t - first[phase]
        sh_ref[phase, 0:span, :] = pad_ref[first[phase]:first[phase] + span, :]
    sub = 64
    parts = []
    for r0 in range(0, t, sub):
        acc = jnp.zeros((sub, CONV_CH), F32)
        for k in range(CONV_K):
            phase = (base + k) % 8
            off = base + k - first[phase] + r0
            acc = acc + dw_ref[k:k + 1, :] * sh_ref[phase, off:off + sub, :]
        parts.append(acc)
    y = jnp.concatenate(parts, axis=0) + dwb_ref[...]
    mu = jnp.mean(y, axis=-1, keepdims=True)
    yc = y - mu
    var = jnp.mean(yc * yc, axis=-1, keepdims=True)
    y = _silu(yc * lax.rsqrt(var + LN_EPS) * lng_ref[...] + lnb_ref[...])
    o_ref[...] = _dot(y.astype(BF16), pw_ref[...]) + pwb_ref[...]


def _conv(cu, dw, dw_b, ln_g, ln_b, pw, pw_b, *, batch, seq, ctx):
    r = cu.shape[0]
    t = CONV_TILE
    assert ctx == t and seq % t == 0
    n_tiles = r // t
    hb = t // HALO
    vec = lambda a: a.reshape(1, CONV_CH)
    return pl.pallas_call(
        functools.partial(_conv_body, tiles_per_seq=seq // t, n_latent_tiles=batch * seq // t),
        grid=(n_tiles,),
        in_specs=[pl.BlockSpec((t, CONV_CH), lambda i: (i, 0)),
                  pl.BlockSpec((HALO, CONV_CH), lambda i: (jnp.maximum(i * hb - 1, 0), 0)),
                  pl.BlockSpec((HALO, CONV_CH), lambda i: (jnp.minimum((i + 1) * hb, n_tiles * hb - 1), 0)),
                  _resident((CONV_K, CONV_CH))] + [_resident((1, CONV_CH))] * 3
                 + [_resident((CONV_CH, CONV_CH)), _resident((1, CONV_CH))],
        out_specs=pl.BlockSpec((t, CONV_CH), lambda i: (i, 0)),
        out_shape=jax.ShapeDtypeStruct((r, CONV_CH), F32),
        scratch_shapes=[pltpu.VMEM((t + 2 * HALO, CONV_CH), F32),
                        pltpu.VMEM((8, t + 2 * HALO, CONV_CH), F32)],
        compiler_params=_params("parallel"),
        name="conv",
    )(cu, cu, cu, dw, vec(dw_b), vec(ln_g), vec(ln_b), pw.astype(BF16), vec(pw_b))


def kernel(x, c, ctx, c_ctx, ada_w, ada_b, norm_ffn1, ffn1_w13, ffn1_w2, norm_mix, w_in, gla_wa_f, gla_ba_f, gla_wa_b, gla_ba_b, gla_norm, na_rpb, diff_lq1, diff_lk1, diff_lq2, diff_lk2, diff_norm, conv_dw, conv_dw_b, conv_ln_g, conv_ln_b, conv_pw, conv_pw_b, w_out, norm_ffn2, ffn2_w13, ffn2_w2, final_norm):
    batch, seq, d = x.shape
    n_ctx = ctx.shape[1]
    depth = ada_w.shape[0]
    rows = seq // GRID_W
    tm = ROW_TILE
    assert seq % tm == 0 and (batch * n_ctx) % tm == 0 and batch + 1 <= 8
    lat_tiles = batch * seq // tm
    all_tiles = lat_tiles + batch * n_ctx // tm
    tiles_per_batch = seq // tm

    def group_of(i):
        return jnp.minimum(i // tiles_per_batch, batch)

    def pos_of(i):
        return jnp.where(i < lat_tiles, i % tiles_per_batch, tiles_per_batch)

    c_rows = jnp.concatenate([c, c_ctx[None, :], jnp.zeros((8 - batch - 1, d), F32)], axis=0)
    mods_all = _ada(c_rows, ada_w, ada_b)[:, :batch + 1].reshape(depth, batch + 1, N_MOD, d)
    rope = _rope_tables(seq, tm)
    h = x.reshape(batch * seq, d)
    h_ctx = ctx.reshape(batch * n_ctx, d)

    for i in range(depth):
        last = i == depth - 1
        lambda_init = 0.8 - 0.6 * math.exp(-0.3 * i)
        mods = mods_all[i]
        tok = dict(n_tiles=all_tiles, group_of=group_of)
        geo = dict(batch=batch, seq=seq, ctx=n_ctx)

        h = _ffn(h, mods, norm_ffn1[i], ffn1_w13[i], ffn1_w2[i], final_norm, k0=0, final=False,
                 h_ctx=h_ctx if i == 0 else None, **tok)

        wa, ba = _gate_weights(gla_wa_f[i], gla_ba_f[i], gla_wa_b[i], gla_ba_b[i])
        (gqk, gv, gvt, gg, glg, nq, nk, nv, dq, dk, dv, cu) = _proj(
            h, mods, norm_mix[i], _permute_w_in(w_in[i]), wa, ba, rope, pos_of=pos_of, **tok)

        gnorm = jnp.tile(gla_norm[i], GLA_HEADS).reshape(1, GLA_HEADS * GLA_DV)
        o_f = _gla(gqk, gv, gvt, glg, gg, None, gnorm, reverse=False, **geo)
        gx = _gla(gqk, gv, gvt, glg, gg, o_f, gnorm, reverse=True, **geo)

        nx = _na(nq, nk, nv, _na_col_tables(na_rpb[i]), **geo)

        lq = jnp.stack([diff_lq1[i], diff_lk1[i], diff_lq2[i], diff_lk2[i]])
        dx = _diff(dq, dk, dv, lq, diff_norm[i], lambda_init=lambda_init, **geo)
        if not last:
            nx, dx = _ctx_attn(nq, nk, nv, dq, dk, dv, lq, diff_norm[i], nx, dx,
                               lambda_init=lambda_init, **geo)

        cx = _conv(cu, conv_dw[i], conv_dw_b[i], conv_ln_g[i], conv_ln_b[i], conv_pw[i], conv_pw_b[i], **geo)

        if last:
            tok = dict(n_tiles=lat_tiles, group_of=group_of)
        h = _ffn(h, mods, norm_ffn2[i], ffn2_w13[i], ffn2_w2[i], final_norm, k0=6, final=last,
                 mixers=(gx, nx, dx, cx, w_out[i], (batch, seq, n_ctx)), **tok)

    return h.reshape(batch, seq, d)
```

```python
import functools
import math

import jax
import jax.numpy as jnp
from jax import lax
from jax.experimental import pallas as pl
from jax.experimental.pallas import tpu as pltpu

F32 = jnp.float32
BF16 = jnp.bfloat16

GRID_W = 64
N_MOD = 9
RMS_EPS = 1e-6
LN_EPS = 1e-5
NEG_INF = -1e30
GLA_HEADS, GLA_DK, GLA_DV, GLA_RANK, GLA_TAU, GLA_CHUNK = 4, 32, 64, 16, 16.0, 64
NA_HEADS, NA_DIM, NA_ROWS, NA_COLS = 4, 64, 8, 16
DIFF_HEADS, DIFF_DK, DIFF_DV = 4, 32, 64
DIFF_VROWS = 128
CONV_CH, CONV_K = 256, 31
ROPE_BASE = 10000.0
LOG2E = 1.4426950408889634

LANES = 128
VMEM_LIMIT = 56 * 1024 * 1024

ROW_TILE = 512
FF_CHUNK = 256
GLA_BLOCK = 256
NA_QROWS = 8
NA_KBLK = 256
CONV_TILE = 256
DIFF_TQ = 512
DIFF_TK = 512
DIFF_UNROLL = 8
DIFF_AHEAD = 3
DIFF_RING = 4
HALO = 16

C_GLA, C_NA, C_DIFF, C_CONV, C_AUX, C_END = 0, 768, 1536, 2304, 2816, 2944


def _dot(a, b):
    return jnp.dot(a, b, preferred_element_type=F32)


def _dot_nt(a, b):
    return lax.dot_general(a, b, (((1,), (1,)), ((), ())), preferred_element_type=F32)


def _params(*sem):
    return pltpu.CompilerParams(dimension_semantics=sem, vmem_limit_bytes=VMEM_LIMIT)


def _resident(shape):
    nd = len(shape)
    return pl.BlockSpec(shape, lambda *_: (0,) * nd, pipeline_mode=pl.Buffered(1))


def _silu(x):
    return x * jax.nn.sigmoid(x)


def _rms(x, w):
    return x * lax.rsqrt(jnp.mean(x * x, axis=-1, keepdims=True) + RMS_EPS) * w


def _ada_body(c_ref, w_ref, b_ref, o_ref):
    s = _silu(c_ref[...])
    o_ref[...] = jnp.dot(s, w_ref[...], precision=lax.Precision.HIGHEST,
                         preferred_element_type=F32) + b_ref[...]


def _ada(c_rows, ada_w, ada_b):
    depth, d, _ = ada_w.shape
    return pl.pallas_call(
        _ada_body,
        grid=(depth, N_MOD),
        in_specs=[pl.BlockSpec((8, d), lambda l, n: (0, 0)),
                  pl.BlockSpec((None, d, d), lambda l, n: (l, 0, n)),
                  pl.BlockSpec((None, 1, d), lambda l, n: (l, 0, n))],
        out_specs=pl.BlockSpec((None, 8, d), lambda l, n: (l, 0, n)),
        out_shape=jax.ShapeDtypeStruct((depth, 8, N_MOD * d), F32),
        compiler_params=_params("arbitrary", "arbitrary"),
        name="ada",
    )(c_rows, ada_w, ada_b.reshape(depth, 1, N_MOD * d))


def _ffn_body(*refs, k0, final, source, n_first):
    h_ref = refs[0]
    n_extra = {"plain": 0, "two_arrays": 1, "mixers": 6}[source]
    extra = refs[1:1 + n_extra]
    mod_ref, nw_ref, w13_ref, w2_ref, fw_ref, o_ref, xb_ref, g_ref, h13_ref = refs[1 + n_extra:]
    ff = w2_ref.shape[0]
    tf = h13_ref.shape[2] // 2
    n_chunks = ff // tf
    x = h_ref[...]
    if source == "two_arrays":
        x = jnp.where(pl.program_id(0) < n_first, x, extra[0][...])
    elif source == "mixers":
        gx_ref, gxc_ref, nx_ref, dx_ref, cx_ref, wo_ref = extra
        gx = jnp.where(pl.program_id(0) < n_first, gx_ref[...], gxc_ref[...].reshape(gx_ref.shape))
        mix = jnp.concatenate([gx, nx_ref[...]] + [dx_ref[hh] for hh in range(DIFF_HEADS)]
                              + [cx_ref[...]], axis=-1).astype(BF16)
        x = x + mod_ref[5:6, :] * _dot(mix, wo_ref[...])
    xm = _rms(x, nw_ref[...]) * (1.0 + mod_ref[k0 + 1:k0 + 2, :]) + mod_ref[k0:k0 + 1, :]
    xb_ref[...] = xm.astype(BF16)

    def cols(c):
        return c * tf if isinstance(c, int) else pl.multiple_of(c * tf, tf)

    def up(c, slot):
        h13_ref[slot, :, :tf] = _dot(xb_ref[...], w13_ref[:, pl.ds(cols(c), tf)])
        h13_ref[slot, :, tf:] = _dot(xb_ref[...], w13_ref[:, pl.ds(ff + cols(c), tf)])

    def gate(c, slot):
        a = h13_ref[slot, :, :tf]
        u = h13_ref[slot, :, tf:]
        g_ref[:, pl.ds(cols(c), tf)] = (_silu(a) * u).astype(BF16)

    def pair(t, carry):
        up(2 * t + 1, 1)
        gate(2 * t, 0)
        up(2 * t + 2, 0)
        gate(2 * t + 1, 1)
        return carry

    up(0, 0)
    lax.fori_loop(0, (n_chunks - 1) // 2, pair, 0)
    gate(n_chunks - 1, 0)
    out = x + (0.5 * mod_ref[k0 + 2:k0 + 3, :]) * _dot(g_ref[...], w2_ref[...])
    if final:
        out = _rms(out, fw_ref[...])
    o_ref[...] = out


def _ffn(h, mods, norm_w, w13, w2, final_w, *, k0, n_tiles, group_of, final, h_ctx=None, mixers=None):
    d = h.shape[1]
    tm = ROW_TILE
    ff = w2.shape[0]
    n_chunks = ff // FF_CHUNK
    assert ff % FF_CHUNK == 0 and n_chunks % 2 == 1
    row = lambda i: (i, 0)
    source, n_first, extra, extra_specs, h_spec = "plain", None, [], [], pl.BlockSpec((tm, d), row)
    if h_ctx is not None:
        source, n_first, extra = "two_arrays", h.shape[0] // tm, [h_ctx]
        h_spec = pl.BlockSpec((tm, d), lambda i: (jnp.minimum(i, n_first - 1), 0))
        extra_specs = [pl.BlockSpec((tm, d), lambda i: (jnp.maximum(i - n_first, 0), 0))]
    elif mixers is not None:
        gx, nx, dx, cx, w_out, (batch, seq, n_ctx) = mixers
        assert batch * n_ctx == tm and seq % tm == 0 and seq % n_ctx == 0
        source, n_first, extra = "mixers", batch * seq // tm, [gx, gx, nx, dx, cx, w_out.astype(BF16)]
        per_seq = seq // tm
        extra_specs = [pl.BlockSpec((None, tm, 256),
                                    lambda i: (jnp.minimum(i // per_seq, batch - 1), i % per_seq, 0)),
                       pl.BlockSpec((batch, n_ctx, 256), lambda i: (0, seq // n_ctx, 0)),
                       pl.BlockSpec((tm, 256), row),
                       pl.BlockSpec((DIFF_HEADS, tm, 64), lambda i: (0, i, 0)),
                       pl.BlockSpec((tm, 256), row), _resident(w_out.shape)]
    return pl.pallas_call(
        functools.partial(_ffn_body, k0=k0, final=final, source=source, n_first=n_first),
        grid=(n_tiles,),
        in_specs=[h_spec] + extra_specs + [
            pl.BlockSpec((None, N_MOD, d), lambda i: (group_of(i), 0, 0)),
            _resident((1, d)),
            _resident(w13.shape),
            _resident(w2.shape),
            _resident((1, d))],
        out_specs=pl.BlockSpec((tm, d), row),
        out_shape=jax.ShapeDtypeStruct((n_tiles * tm, d), F32),
        scratch_shapes=[pltpu.VMEM((tm, d), BF16), pltpu.VMEM((tm, ff), BF16),
                        pltpu.VMEM((2, tm, 2 * FF_CHUNK), F32)],
        compiler_params=_params("parallel"),
        name="ffn",
    )(h, *extra, mods, norm_w.reshape(1, d), w13.astype(BF16), w2.astype(BF16), final_w.reshape(1, d))


def _log_sigmoid(x):
    return jnp.minimum(x, 0.0) - jnp.log1p(jnp.exp(-jnp.abs(x)))


def _rope_rotate(x):
    n = x.shape[-1]
    lane = lax.broadcasted_iota(jnp.int32, x.shape, 1)
    up = pltpu.roll(x, n - 8, 1)
    dn = pltpu.roll(x, 8, 1)
    return jnp.where((lane & 15) < 8, -up, dn)


def _rope_rotate_rows(x):
    n = x.shape[0]
    row = lax.broadcasted_iota(jnp.int32, x.shape, 0)
    up = pltpu.roll(x, n - 8, 0)
    dn = pltpu.roll(x, 8, 0)
    return jnp.where((row & 15) < 8, -up, dn)


def _proj_body(h_ref, mod_ref, nw_ref, w_ref, wvt_ref, wdqt_ref, wdvt_ref, wa_ref, ba_ref,
               cos_ref, sin_ref, cost_ref, sint_ref,
               gqk_ref, gv_ref, gvt_ref, gg_ref, glg_ref, nq_ref, nk_ref, nv_ref,
               dqt_ref, dk_ref, dvt_ref, cu_ref, xb_ref):
    x = h_ref[...]
    tm = x.shape[0]
    xm = _rms(x, nw_ref[...]) * (1.0 + mod_ref[4:5, :]) + mod_ref[3:4, :]
    xb_ref[...] = xm.astype(BF16)

    z = _dot(xb_ref[...], w_ref[:, C_GLA:C_GLA + 256])
    lane = lax.broadcasted_iota(jnp.int32, (1, 256), 1)
    gqk_ref[...] = z * jnp.where(lane < 128, GLA_DK ** -0.5, 1.0)
    gv_ref[...] = _dot(xb_ref[...], w_ref[:, C_GLA + 256:C_GLA + 512])
    gvt_ref[...] = _dot_nt(wvt_ref[...], xb_ref[...])
    gg_ref[...] = _dot(xb_ref[...], w_ref[:, C_GLA + 512:C_GLA + 768])
    aux = _dot(xb_ref[...], w_ref[:, C_AUX:C_END])
    pre = _dot(aux.astype(BF16), wa_ref[...]) + ba_ref[...]
    glg_ref[...] = _log_sigmoid(pre) * (1.0 / GLA_TAU)

    nq_ref[...] = (_dot(xb_ref[...], w_ref[:, C_NA:C_NA + 256]) * (NA_DIM ** -0.5)).astype(BF16)
    nk_ref[...] = _dot(xb_ref[...], w_ref[:, C_NA + 256:C_NA + 512]).astype(BF16)
    nv_ref[...] = _dot(xb_ref[...], w_ref[:, C_NA + 512:C_NA + 768]).astype(BF16)

    cos = cos_ref[...]
    sin = sin_ref[...]
    cos2 = jnp.concatenate([cos, cos], axis=1)
    sin2 = jnp.concatenate([sin, sin], axis=1)
    zk = _dot(xb_ref[...], w_ref[:, C_DIFF + 256:C_DIFF + 512])
    zk = zk * cos2 + _rope_rotate(zk) * sin2
    n_rep = 2 * DIFF_HEADS
    cos_t = jnp.concatenate([cost_ref[...]] * n_rep, axis=0)
    sin_t = jnp.concatenate([sint_ref[...]] * n_rep, axis=0)
    zqt = _dot_nt(wdqt_ref[...], xb_ref[...])
    zqt = (zqt * cos_t + _rope_rotate_rows(zqt) * sin_t) * (DIFF_DK ** -0.5 * LOG2E)
    zvt = _dot_nt(wdvt_ref[...], xb_ref[...])
    pad_rows = DIFF_VROWS - DIFF_DV
    one_row = jnp.where(lax.broadcasted_iota(jnp.int32, (pad_rows, tm), 0) == 0, 1.0, 0.0).astype(BF16)
    for hh in range(DIFF_HEADS):
        sl = slice(64 * hh, 64 * hh + 64)
        dk_ref[hh] = zk[:, sl].astype(BF16)
        dqt_ref[hh] = zqt[sl, :].astype(BF16)
        dvt_ref[hh, 0:64, :] = zvt[sl, :].astype(BF16)
        dvt_ref[hh, DIFF_DV:DIFF_VROWS, :] = one_row

    za = _dot(xb_ref[...], w_ref[:, C_CONV:C_CONV + 256])
    zg = _dot(xb_ref[...], w_ref[:, C_CONV + 256:C_CONV + 512])
    cu_ref[...] = za * jax.nn.sigmoid(zg)


def _proj(h, mods, norm_w, w_p, wa, ba, rope, *, n_tiles, group_of, pos_of):
    r, d = h.shape
    tm = ROW_TILE
    row = lambda i: (i, 0)
    hrow = lambda i: (0, i, 0)
    hcol = lambda i: (0, 0, i)
    f32_256 = jax.ShapeDtypeStruct((r, 256), F32)
    bf_256 = jax.ShapeDtypeStruct((r, 256), BF16)
    wvt = w_p[:, C_GLA + 256:C_GLA + 512].T
    wdqt = w_p[:, C_DIFF:C_DIFF + 256].T
    wdvt = w_p[:, C_DIFF + 512:C_DIFF + 768].T
    cos_r, sin_r, cos_c, sin_c = rope
    out_shape = [f32_256, f32_256, jax.ShapeDtypeStruct((256, r), F32), f32_256, f32_256,
                 bf_256, bf_256, bf_256,
                 jax.ShapeDtypeStruct((DIFF_HEADS, 64, r), BF16),
                 jax.ShapeDtypeStruct((DIFF_HEADS, r, 64), BF16),
                 jax.ShapeDtypeStruct((DIFF_HEADS, DIFF_VROWS, r), BF16),
                 f32_256]
    out_specs = [pl.BlockSpec((tm, 256), row)] * 2 + [pl.BlockSpec((256, tm), lambda i: (0, i))] + [
        pl.BlockSpec((tm, 256), row)] * 5 + [
        pl.BlockSpec((DIFF_HEADS, 64, tm), hcol),
        pl.BlockSpec((DIFF_HEADS, tm, 64), hrow),
        pl.BlockSpec((DIFF_HEADS, DIFF_VROWS, tm), hcol),
        pl.BlockSpec((tm, 256), row)]
    return pl.pallas_call(
        _proj_body,
        grid=(n_tiles,),
        in_specs=[pl.BlockSpec((tm, d), row),
                  pl.BlockSpec((None, N_MOD, d), lambda i: (group_of(i), 0, 0)),
                  _resident((1, d)),
                  _resident(w_p.shape),
                  _resident(wvt.shape),
                  _resident(wdqt.shape),
                  _resident(wdvt.shape),
                  _resident(wa.shape),
                  _resident(ba.shape),
                  pl.BlockSpec((tm, LANES), lambda i: (pos_of(i), 0)),
                  pl.BlockSpec((tm, LANES), lambda i: (pos_of(i), 0)),
                  pl.BlockSpec((DIFF_DK, tm), lambda i: (0, pos_of(i))),
                  pl.BlockSpec((DIFF_DK, tm), lambda i: (0, pos_of(i)))],
        out_specs=out_specs,
        out_shape=out_shape,
        scratch_shapes=[pltpu.VMEM((tm, d), BF16)],
        compiler_params=_params("parallel"),
        name="proj",
    )(h, mods, norm_w.reshape(1, d), w_p, wvt, wdqt, wdvt, wa, ba, cos_r, sin_r, cos_c, sin_c)


def _permute_w_in(w_in):
    d = w_in.shape[0]
    g0 = 2 * GLA_HEADS * GLA_DK + 2 * GLA_HEADS * GLA_DV
    aux = w_in[:, g0:g0 + 2 * GLA_RANK]
    rest = w_in[:, g0 + 2 * GLA_RANK:]
    pad = jnp.zeros((d, C_END - C_AUX - 2 * GLA_RANK), w_in.dtype)
    return jnp.concatenate([w_in[:, :g0], rest, aux, pad], axis=1).astype(BF16)


def _gate_weights(wa_f, ba_f, wa_b, ba_b):
    n = GLA_HEADS * GLA_DK
    wa = jnp.zeros((C_END - C_AUX, 2 * n), F32)
    wa = wa.at[:GLA_RANK, :n].set(wa_f).at[GLA_RANK:2 * GLA_RANK, n:].set(wa_b)
    return wa.astype(BF16), jnp.concatenate([ba_f, ba_b]).reshape(1, 2 * n)


def _rope_tables(seq, tile):
    t = jnp.arange(seq)
    row = (t // GRID_W).astype(F32)
    col = (t % GRID_W).astype(F32)
    half = DIFF_DK // 2
    inv = 1.0 / (ROPE_BASE ** (jnp.arange(0, half, 2, dtype=F32) / half))
    ang_r = row[:, None] * inv
    ang_c = col[:, None] * inv
    ang = jnp.concatenate([ang_r, ang_r, ang_c, ang_c], axis=-1)
    cos = jnp.concatenate([jnp.cos(ang), jnp.ones((tile, DIFF_DK), F32)], axis=0)
    sin = jnp.concatenate([jnp.sin(ang), jnp.zeros((tile, DIFF_DK), F32)], axis=0)
    rep = (1, LANES // DIFF_DK)
    return jnp.tile(cos, rep), jnp.tile(sin, rep), cos.T, sin.T


def _split3(x):
    hi = x.astype(BF16)
    r1 = x - hi.astype(F32)
    mid = r1.astype(BF16)
    lo = (r1 - mid.astype(F32)).astype(BF16)
    return hi, mid, lo


def _gla_body(*refs, reverse, batch):
    per_b = [refs[5 * b:5 * b + 5] for b in range(batch)]
    of_ref, nw_ref, o_ref, st_ref = refs[5 * batch:]
    blk = per_b[0][0].shape[0]
    n_chunks = blk // GLA_CHUNK
    nqk = GLA_HEADS * GLA_DK
    nv = GLA_HEADS * GLA_DV

    @pl.when(pl.program_id(0) == 0)
    def _():
        st_ref[...] = jnp.zeros_like(st_ref)

    ri = lax.broadcasted_iota(jnp.int32, (blk, blk), 0)
    ci = lax.broadcasted_iota(jnp.int32, (blk, blk), 1)
    same = (ri // GLA_CHUNK) == (ci // GLA_CHUNK)
    tri = jnp.where(same & ((ci >= ri) if reverse else (ci <= ri)), 1.0, 0.0).astype(BF16)
    bcum_of = []
    for qk_ref, v_ref, vt_ref, lg_ref, g_ref in per_b:
        lg = lg_ref[:, nqk:2 * nqk] if reverse else lg_ref[:, 0:nqk]
        hi, mid, lo = _split3(lg)
        bcum_of.append(_dot(tri, hi) + _dot(tri, mid) + _dot(tri, lo))

    c = GLA_CHUNK
    rk = lax.broadcasted_iota(jnp.int32, (GLA_HEADS * c, nqk), 0) // c
    ck = lax.broadcasted_iota(jnp.int32, (GLA_HEADS * c, nqk), 1) // GLA_DK
    mask_k = rk == ck
    rv = lax.broadcasted_iota(jnp.int32, (GLA_HEADS * c, nv), 0) // c
    cv = lax.broadcasted_iota(jnp.int32, (GLA_HEADS * c, nv), 1) // GLA_DV
    mask_v = rv == cv
    rs = lax.broadcasted_iota(jnp.int32, (nv, nqk), 0) // GLA_DV
    cs = lax.broadcasted_iota(jnp.int32, (nv, nqk), 1) // GLA_DK
    mask_s = rs == cs
    ai = lax.broadcasted_iota(jnp.int32, (c, GLA_HEADS * c), 0)
    aj = lax.broadcasted_iota(jnp.int32, (c, GLA_HEADS * c), 1) % c
    mask_a = (aj >= ai) if reverse else (aj <= ai)

    order = range(n_chunks - 1, -1, -1) if reverse else range(n_chunks)
    units = [(b, ch) for ch in order for b in range(batch)]
    q_in, a_raw, u_t, decay = {}, {}, {}, {}
    vts = [refs_b[2][...].astype(BF16) for refs_b in per_b]
    for b, ch in units:
        qk_ref = per_b[b][0]
        rows = slice(ch * c, (ch + 1) * c)
        k = qk_ref[rows, nqk:2 * nqk]
        bcum = bcum_of[b][rows, :]
        btot = bcum[0:1, :] if reverse else bcum[c - 1:c, :]
        q_in[b, ch] = (qk_ref[rows, 0:nqk] * jnp.exp(bcum)).astype(BF16)
        k_in = k * jnp.exp(-bcum)
        k_out = (k * jnp.exp(btot - bcum)).astype(BF16)
        k_bd = jnp.where(mask_k, jnp.concatenate([k_in] * GLA_HEADS, axis=0), 0.0).astype(BF16)
        a_raw[b, ch] = _dot_nt(q_in[b, ch], k_bd)
        k_pad = jnp.concatenate([jnp.zeros((n * c, nqk), BF16) for n in (ch,) if n] + [k_out]
                                + [jnp.zeros((n * c, nqk), BF16) for n in (n_chunks - 1 - ch,) if n], axis=0)
        u_t[b, ch] = jnp.where(mask_s, _dot(vts[b], k_pad), 0.0)
        decay[b, ch] = jnp.exp(btot)
    o = {}
    for b, ch in units:
        v = per_b[b][1][ch * c:(ch + 1) * c, :]
        v_bd = jnp.where(mask_v, jnp.concatenate([v] * GLA_HEADS, axis=0), 0.0).astype(BF16)
        o[b, ch] = _dot(jnp.where(mask_a, a_raw[b, ch], 0.0).astype(BF16), v_bd)
    st = [st_ref[b] for b in range(batch)]
    for b, ch in units:
        o[b, ch] = o[b, ch] + _dot_nt(q_in[b, ch], st[b].astype(BF16))
        st[b] = st[b] * decay[b, ch] + u_t[b, ch]
    for b in range(batch):
        st_ref[b] = st[b]
        o_blk = jnp.concatenate([o[b, ch] for ch in range(n_chunks)], axis=0)
        if reverse:
            o_blk = o_blk + of_ref[b]
            hi2, lo2, _ = _split3(o_blk * o_blk)
            hr = lax.broadcasted_iota(jnp.int32, (nv, nv), 0) // GLA_DV
            hc = lax.broadcasted_iota(jnp.int32, (nv, nv), 1) // GLA_DV
            seg = jnp.where(hr == hc, 1.0, 0.0).astype(BF16)
            ms = (_dot(hi2, seg) + _dot(lo2, seg)) * (1.0 / GLA_DV)
            o_blk = o_blk * lax.rsqrt(ms + RMS_EPS) * nw_ref[...] * _silu(per_b[b][4][...])
        o_ref[b] = o_blk


def _gla(gqk, gv, gvt, glg, gg, o_f, norm_w4, *, reverse, batch, seq, ctx):
    blk = GLA_BLOCK
    nc, nl = ctx // blk, seq // blk
    ctx_base = batch * seq // blk

    def step_blk(s):
        if reverse:
            return jnp.where(s < nc, nl + (nc - 1 - s), nl - 1 - (s - nc))
        return jnp.where(s < nc, nl + s, s - nc)

    def row_blk(b):
        return lambda s: jnp.where(s < nc, ctx_base + b * nc - nl, b * nl) + step_blk(s)

    specs, args = [], []
    for b in range(batch):
        spec = pl.BlockSpec((blk, 256), lambda s, f=row_blk(b): (f(s), 0))
        spec_t = pl.BlockSpec((256, blk), lambda s, f=row_blk(b): (0, f(s)))
        specs += [spec, spec, spec_t, spec, spec]
        args += [gqk, gv, gvt, glg, gg]
    seq_spec = pl.BlockSpec((batch, blk, 256), lambda s: (0, step_blk(s), 0))
    if o_f is None:
        o_f, of_spec = norm_w4, _resident((1, 256))
    else:
        of_spec = seq_spec
    return pl.pallas_call(
        functools.partial(_gla_body, reverse=reverse, batch=batch),
        grid=(nc + nl,),
        in_specs=specs + [of_spec, _resident((1, 256))],
        out_specs=seq_spec,
        out_shape=jax.ShapeDtypeStruct((batch, seq + ctx, 256), F32),
        scratch_shapes=[pltpu.VMEM((batch, GLA_HEADS * GLA_DV, GLA_HEADS * GLA_DK), F32)],
        compiler_params=_params("arbitrary"),
        name="gla_bwd" if reverse else "gla_fwd",
    )(*args, o_f, norm_w4)


def _na_col_tables(rpb):
    col = jnp.arange(GRID_W)
    cstart = jnp.clip(col - NA_COLS // 2, 0, GRID_W - NA_COLS)
    col_ok = (col[None, :] >= cstart[:, None]) & (col[None, :] < cstart[:, None] + NA_COLS)
    col_off = jnp.clip(col[None, :] - col[:, None] + (NA_COLS - 1), 0, 2 * NA_COLS - 2)
    t = jnp.where(col_ok[None, None], rpb[:, :, col_off], NEG_INF)
    dead = jnp.full((NA_HEADS, 1, GRID_W, GRID_W), NEG_INF, F32)
    t = jnp.concatenate([dead, t.astype(F32), dead], axis=1)
    return jnp.concatenate([t[:, :-1], t[:, 1:]], axis=-1)


def _na_body(q_ref, k0, k1, k2, k3, v0, v1, v2, v3, kc_ref, vc_ref, tab_ref, o_ref, *, rows, key_blocks):
    q = q_ref[...]
    kw = jnp.concatenate([k0[...], k1[...], k2[...], k3[...]], axis=0)
    vw = jnp.concatenate([v0[...], v1[...], v2[...], v3[...]], axis=0)
    kc = kc_ref[...]
    vc = vc_ref[...]

    j = pl.program_id(1)
    rows_per_blk = NA_KBLK // GRID_W
    r0 = j * NA_QROWS
    kr0 = jnp.clip(2 * j - 1, 0, key_blocks - 4) * rows_per_blk
    lane_lo = lax.broadcasted_iota(jnp.int32, (1, 2 * GRID_W), 1) < GRID_W
    n_pairs = 4 * rows_per_blk // 2
    entry, ok = [], []
    for a in range(NA_QROWS):
        r = r0 + a
        start = jnp.clip(r - NA_ROWS // 2, 0, rows - NA_ROWS)
        for bp in range(n_pairs):
            rk = kr0 + 2 * bp
            entry.append(jnp.clip(rk - r + NA_ROWS, 0, 2 * NA_ROWS - 1))
            in0 = ((rk >= start) & (rk < start + NA_ROWS)).astype(jnp.int32)
            in1 = ((rk + 1 >= start) & (rk + 1 < start + NA_ROWS)).astype(jnp.int32)
            ok.append(jnp.where(lane_lo, in0, in1) != 0)

    def bias_of(hh):
        rows_ = []
        for a in range(NA_QROWS):
            tiles = [jnp.where(ok[a * n_pairs + bp], tab_ref[hh, entry[a * n_pairs + bp]], NEG_INF)
                     for bp in range(n_pairs)]
            rows_.append(jnp.concatenate(tiles, axis=1))
        return jnp.concatenate(rows_, axis=0)

    lane = lax.broadcasted_iota(jnp.int32, (1, NA_HEADS * NA_DIM), 1) // NA_DIM
    out = jnp.zeros(o_ref.shape, F32)
    for hh in range(NA_HEADS):
        mh = lane == hh
        qh = jnp.where(mh, q, jnp.zeros_like(q))
        s_w = _dot_nt(qh, kw) + bias_of(hh)
        s_c = _dot_nt(qh, kc)
        m = jnp.maximum(jnp.max(s_w, axis=-1, keepdims=True), jnp.max(s_c, axis=-1, keepdims=True))
        p_w = jnp.exp(s_w - m)
        p_c = jnp.exp(s_c - m)
        l = jnp.sum(p_w, axis=-1, keepdims=True) + jnp.sum(p_c, axis=-1, keepdims=True)
        o = _dot(p_w.astype(BF16), vw) + _dot(p_c.astype(BF16), vc)
        out = out + jnp.where(mh, o / l, 0.0)
    o_ref[...] = out


def _na(nq, nk, nv, tab, *, batch, seq, ctx):
    r = nq.shape[0]
    tq = NA_QROWS * GRID_W
    nj = seq // tq
    kb = seq // NA_KBLK
    assert kb >= 4 and seq % tq == 0 and seq % ctx == 0
    ctx_blk0 = batch * seq // ctx

    def kspec(i):
        return pl.BlockSpec((NA_KBLK, 256),
                            lambda b, j: (b * kb + jnp.clip(2 * j - 1, 0, kb - 4) + i, 0))

    cspec = pl.BlockSpec((ctx, 256), lambda b, j: (ctx_blk0 + b, 0))
    return pl.pallas_call(
        functools.partial(_na_body, rows=seq // GRID_W, key_blocks=kb),
        grid=(batch, nj),
        in_specs=[pl.BlockSpec((tq, 256), lambda b, j: (b * nj + j, 0))]
                 + [kspec(i) for i in range(4)] + [kspec(i) for i in range(4)]
                 + [cspec, cspec, _resident(tab.shape)],
        out_specs=pl.BlockSpec((tq, 256), lambda b, j: (b * nj + j, 0)),
        out_shape=jax.ShapeDtypeStruct((r, 256), F32),
        compiler_params=_params("parallel", "arbitrary"),
        name="na",
    )(nq, nk, nk, nk, nk, nv, nv, nv, nv, nk, nv, tab)


def _diff_lambda(lq_ref, lambda_init):
    lq = lq_ref[...]
    s1 = jnp.sum(lq[0:1, :] * lq[1:2, :], axis=-1, keepdims=True)
    s2 = jnp.sum(lq[2:3, :] * lq[3:4, :], axis=-1, keepdims=True)
    return jnp.exp(s1) - jnp.exp(s2) + lambda_init


def _stack_qt(qt):
    row = lax.broadcasted_iota(jnp.int32, (2 * DIFF_DK, 1), 0)
    zero = jnp.zeros_like(qt)
    return jnp.concatenate([jnp.where(row < DIFF_DK, qt, zero), jnp.where(row >= DIFF_DK, qt, zero)], axis=1)


def _diff_finish_t(acc, tq, lam, nw_col, lambda_init):
    o = acc[0:DIFF_DV, :] / acc[DIFF_DV:DIFF_DV + 1, :]
    od = o[:, :tq] - lam * o[:, tq:]
    ms = jnp.mean(od * od, axis=0, keepdims=True)
    y = od * lax.rsqrt(ms + RMS_EPS) * nw_col * (1.0 - lambda_init)
    y = jnp.concatenate([y, jnp.zeros_like(y)], axis=0)
    return jnp.transpose(y)[:, 0:DIFF_DV]


def _diff_body(qt_ref, kc_ref, vtc_ref, k_ref, vt_ref, lq_ref, nw_ref, o_ref, s_ref, *, tk, unroll, lambda_init):
    tq = qt_ref.shape[1]
    n_chunks = k_ref.shape[0] // tk
    ring = s_ref.shape[0]
    qs = _stack_qt(qt_ref[...])

    def chunk(j):
        return pl.ds(j * tk if isinstance(j, int) else pl.multiple_of(j * tk, tk), tk)

    def issue_scores(j, slot):
        s = _dot(k_ref[chunk(j), :], qs)
        s_ref[slot] = s
        return jnp.max(s, axis=0, keepdims=True)

    s_ctx = _dot(kc_ref[...], qs)
    m0 = jnp.max(s_ctx, axis=0, keepdims=True)
    cmax0 = tuple(issue_scores(min(a, n_chunks - 1), a % ring) for a in range(DIFF_AHEAD))
    acc0 = _dot(vtc_ref[...], jnp.exp2(s_ctx - m0).astype(BF16))

    def group(g, carry, tail=False):
        cmax, m, acc = carry
        for u in range(unroll):
            j = g * unroll + u
            if tail and j + DIFF_AHEAD >= n_chunks:
                c_new = cmax[0]
            else:
                c_new = issue_scores(j + DIFF_AHEAD, (u + DIFF_AHEAD) % ring)
            m_new = jnp.maximum(m, cmax[0])
            p = jnp.exp2(s_ref[u % ring] - m_new).astype(BF16)
            vt = vt_ref[:, chunk(j)]
            acc = jnp.exp2(m - m_new) * acc + _dot(vt, p)
            m, cmax = m_new, cmax[1:] + (c_new,)
        return cmax, m, acc

    n_groups = n_chunks // unroll
    carry = lax.fori_loop(0, n_groups - 1, group, (cmax0, m0, acc0))
    _, _, acc = group(n_groups - 1, carry, tail=True)
    o_ref[...] = _diff_finish_t(acc, tq, _diff_lambda(lq_ref, lambda_init), nw_ref[...], lambda_init)


def _diff(dqt, dk, dvt, lq, norm_w, *, batch, seq, ctx, lambda_init):
    h, r, _ = dk.shape
    tq = DIFF_TQ
    nq = seq // tq
    tk = min(DIFF_TK, seq)
    unroll = min(DIFF_UNROLL, seq // tk)
    assert (seq // tk) % unroll == 0 and unroll % DIFF_RING == 0 and DIFF_RING > DIFF_AHEAD
    ctx_blk0 = batch * seq // ctx
    return pl.pallas_call(
        functools.partial(_diff_body, tk=tk, unroll=unroll, lambda_init=lambda_init),
        grid=(batch, h, nq),
        in_specs=[pl.BlockSpec((None, 64, tq), lambda b, hh, i: (hh, 0, b * nq + i)),
                  pl.BlockSpec((None, ctx, 64), lambda b, hh, i: (hh, ctx_blk0 + b, 0)),
                  pl.BlockSpec((None, DIFF_VROWS, ctx), lambda b, hh, i: (hh, 0, ctx_blk0 + b)),
                  pl.BlockSpec((None, seq, 64), lambda b, hh, i: (hh, b, 0)),
                  pl.BlockSpec((None, DIFF_VROWS, seq), lambda b, hh, i: (hh, 0, b)),
                  _resident(lq.shape),
                  _resident((DIFF_DV, 1))],
        out_specs=pl.BlockSpec((None, tq, 64), lambda b, hh, i: (hh, b * nq + i, 0)),
        out_shape=jax.ShapeDtypeStruct((h, r, 64), F32),
        scratch_shapes=[pltpu.VMEM((DIFF_RING, tk, 2 * tq), F32)],
        compiler_params=_params("parallel", "parallel", "arbitrary"),
        name="diff",
    )(dqt, dk, dvt, dk, dvt, lq, norm_w.reshape(DIFF_DV, 1))


def _ctx_attn_body(nq_ref, nk_ref, nv_ref, dqt_ref, dk_ref, dvt_ref, lq_ref, nw_ref, na_in, df_in,
                   na_out, df_out, *, lambda_init):
    del na_in, df_in
    q = nq_ref[...]
    k = nk_ref[...]
    v = nv_ref[...]
    lane = lax.broadcasted_iota(jnp.int32, (1, NA_HEADS * NA_DIM), 1) // NA_DIM
    out = jnp.zeros(na_out.shape, F32)
    for hh in range(NA_HEADS):
        mh = lane == hh
        s = _dot_nt(jnp.where(mh, q, jnp.zeros_like(q)), k)
        p = jnp.exp(s - jnp.max(s, axis=-1, keepdims=True))
        o = _dot(p.astype(BF16), v) / jnp.sum(p, axis=-1, keepdims=True)
        out = out + jnp.where(mh, o, 0.0)
    na_out[...] = out

    lam = _diff_lambda(lq_ref, lambda_init)
    tq = dqt_ref.shape[2]
    for hh in range(DIFF_HEADS):
        s = _dot(dk_ref[hh], _stack_qt(dqt_ref[hh]))
        p = jnp.exp2(s - jnp.max(s, axis=0, keepdims=True))
        acc = _dot(dvt_ref[hh], p.astype(BF16))
        df_out[hh] = _diff_finish_t(acc, tq, lam, nw_ref[...], lambda_init)


def _ctx_attn(nq, nk, nv, dqt, dk, dvt, lq, norm_w, na_o, df_o, *, batch, seq, ctx, lambda_init):
    blk0 = batch * seq // ctx
    s256 = pl.BlockSpec((ctx, 256), lambda b: (blk0 + b, 0))
    sh64 = pl.BlockSpec((DIFF_HEADS, ctx, 64), lambda b: (0, blk0 + b, 0))
    sq_t = pl.BlockSpec((DIFF_HEADS, 64, ctx), lambda b: (0, 0, blk0 + b))
    sv_t = pl.BlockSpec((DIFF_HEADS, DIFF_VROWS, ctx), lambda b: (0, 0, blk0 + b))
    return pl.pallas_call(
        functools.partial(_ctx_attn_body, lambda_init=lambda_init),
        grid=(batch,),
        in_specs=[s256, s256, s256, sq_t, sh64, sv_t, _resident(lq.shape), _resident((DIFF_DV, 1)),
                  pl.BlockSpec(memory_space=pl.ANY), pl.BlockSpec(memory_space=pl.ANY)],
        out_specs=[s256, sh64],
        out_shape=[jax.ShapeDtypeStruct(na_o.shape, F32), jax.ShapeDtypeStruct(df_o.shape, F32)],
        input_output_aliases={8: 0, 9: 1},
        compiler_params=_params("arbitrary"),
        name="ctx_attn",
    )(nq, nk, nv, dqt, dk, dvt, lq, norm_w.reshape(DIFF_DV, 1), na_o, df_o)


def _conv_body(u_ref, prev_ref, next_ref, dw_ref, dwb_ref, lng_ref, lnb_ref, pw_ref, pwb_ref,
               o_ref, pad_ref, sh_ref, *, tiles_per_seq, n_latent_tiles):
    t = u_ref.shape[0]
    i = pl.program_id(0)
    in_ctx = i >= n_latent_tiles
    first = in_ctx | (i % tiles_per_seq == 0)
    last = in_ctx | (i % tiles_per_seq == tiles_per_seq - 1)
    pad_ref[0:HALO, :] = jnp.where(first, 0.0, prev_ref[...])
    pad_ref[HALO:HALO + t, :] = u_ref[...]
    pad_ref[HALO + t:HALO + t + HALO, :] = jnp.where(last, 0.0, next_ref[...])
    base = HALO - CONV_K // 2
    first = {}
    for phase in range(8):
        taps = [k for k in range(CONV_K) if (base + k) % 8 == phase]
        first[phase] = base + taps[0]
        span = base + taps[-1] + t - first[phase]
        sh_ref[phase, 0:span, :] = pad_ref[first[phase]:first[phase] + span, :]
    sub = 64
    parts = []
    for r0 in range(0, t, sub):
        acc = jnp.zeros((sub, CONV_CH), F32)
        for k in range(CONV_K):
            phase = (base + k) % 8
            off = base + k - first[phase] + r0
            acc = acc + dw_ref[k:k + 1, :] * sh_ref[phase, off:off + sub, :]
        parts.append(acc)
    y = jnp.concatenate(parts, axis=0) + dwb_ref[...]
    mu = jnp.mean(y, axis=-1, keepdims=True)
    yc = y - mu
    var = jnp.mean(yc * yc, axis=-1, keepdims=True)
    y = _silu(yc * lax.rsqrt(var + LN_EPS) * lng_ref[...] + lnb_ref[...])
    o_ref[...] = _dot(y.astype(BF16), pw_ref[...]) + pwb_ref[...]


def _conv(cu, dw, dw_b, ln_g, ln_b, pw, pw_b, *, batch, seq, ctx):
    r = cu.shape[0]
    t = CONV_TILE
    assert ctx == t and seq % t == 0
    n_tiles = r // t
    hb = t // HALO
    vec = lambda a: a.reshape(1, CONV_CH)
    return pl.pallas_call(
        functools.partial(_conv_body, tiles_per_seq=seq // t, n_latent_tiles=batch * seq // t),
        grid=(n_tiles,),
        in_specs=[pl.BlockSpec((t, CONV_CH), lambda i: (i, 0)),
                  pl.BlockSpec((HALO, CONV_CH), lambda i: (jnp.maximum(i * hb - 1, 0), 0)),
                  pl.BlockSpec((HALO, CONV_CH), lambda i: (jnp.minimum((i + 1) * hb, n_tiles * hb - 1), 0)),
                  _resident((CONV_K, CONV_CH))] + [_resident((1, CONV_CH))] * 3
                 + [_resident((CONV_CH, CONV_CH)), _resident((1, CONV_CH))],
        out_specs=pl.BlockSpec((t, CONV_CH), lambda i: (i, 0)),
        out_shape=jax.ShapeDtypeStruct((r, CONV_CH), F32),
        scratch_shapes=[pltpu.VMEM((t + 2 * HALO, CONV_CH), F32),
                        pltpu.VMEM((8, t + 2 * HALO, CONV_CH), F32)],
        compiler_params=_params("parallel"),
        name="conv",
    )(cu, cu, cu, dw, vec(dw_b), vec(ln_g), vec(ln_b), pw.astype(BF16), vec(pw_b))


def kernel(x, c, ctx, c_ctx, ada_w, ada_b, norm_ffn1, ffn1_w13, ffn1_w2, norm_mix, w_in, gla_wa_f, gla_ba_f, gla_wa_b, gla_ba_b, gla_norm, na_rpb, diff_lq1, diff_lk1, diff_lq2, diff_lk2, diff_norm, conv_dw, conv_dw_b, conv_ln_g, conv_ln_b, conv_pw, conv_pw_b, w_out, norm_ffn2, ffn2_w13, ffn2_w2, final_norm):
    batch, seq, d = x.shape
    n_ctx = ctx.shape[1]
    depth = ada_w.shape[0]
    rows = seq // GRID_W
    tm = ROW_TILE
    assert seq % tm == 0 and (batch * n_ctx) % tm == 0 and batch + 1 <= 8
    lat_tiles = batch * seq // tm
    all_tiles = lat_tiles + batch * n_ctx // tm
    tiles_per_batch = seq // tm

    def group_of(i):
        return jnp.minimum(i // tiles_per_batch, batch)

    def pos_of(i):
        return jnp.where(i < lat_tiles, i % tiles_per_batch, tiles_per_batch)

    c_rows = jnp.concatenate([c, c_ctx[None, :], jnp.zeros((8 - batch - 1, d), F32)], axis=0)
    mods_all = _ada(c_rows, ada_w, ada_b)[:, :batch + 1].reshape(depth, batch + 1, N_MOD, d)
    rope = _rope_tables(seq, tm)
    h = x.reshape(batch * seq, d)
    h_ctx = ctx.reshape(batch * n_ctx, d)

    for i in range(depth):
        last = i == depth - 1
        lambda_init = 0.8 - 0.6 * math.exp(-0.3 * i)
        mods = mods_all[i]
        tok = dict(n_tiles=all_tiles, group_of=group_of)
        geo = dict(batch=batch, seq=seq, ctx=n_ctx)

        h = _ffn(h, mods, norm_ffn1[i], ffn1_w13[i], ffn1_w2[i], final_norm, k0=0, final=False,
                 h_ctx=h_ctx if i == 0 else None, **tok)

        wa, ba = _gate_weights(gla_wa_f[i], gla_ba_f[i], gla_wa_b[i], gla_ba_b[i])
        (gqk, gv, gvt, gg, glg, nq, nk, nv, dq, dk, dv, cu) = _proj(
            h, mods, norm_mix[i], _permute_w_in(w_in[i]), wa, ba, rope, pos_of=pos_of, **tok)

        gnorm = jnp.tile(gla_norm[i], GLA_HEADS).reshape(1, GLA_HEADS * GLA_DV)
        o_f = _gla(gqk, gv, gvt, glg, gg, None, gnorm, reverse=False, **geo)
        gx = _gla(gqk, gv, gvt, glg, gg, o_f, gnorm, reverse=True, **geo)

        nx = _na(nq, nk, nv, _na_col_tables(na_rpb[i]), **geo)

        lq = jnp.stack([diff_lq1[i], diff_lk1[i], diff_lq2[i], diff_lk2[i]])
        dx = _diff(dq, dk, dv, lq, diff_norm[i], lambda_init=lambda_init, **geo)
        if not last:
            nx, dx = _ctx_attn(nq, nk, nv, dq, dk, dv, lq, diff_norm[i], nx, dx,
                               lambda_init=lambda_init, **geo)

        cx = _conv(cu, conv_dw[i], conv_dw_b[i], conv_ln_g[i], conv_ln_b[i], conv_pw[i], conv_pw_b[i], **geo)

        if last:
            tok = dict(n_tiles=lat_tiles, group_of=group_of)
        h = _ffn(h, mods, norm_ffn2[i], ffn2_w13[i], ffn2_w2[i], final_norm, k0=6, final=last,
                 mixers=(gx, nx, dx, cx, w_out[i], (batch, seq, n_ctx)), **tok)

    return h.reshape(batch, seq, d)
```

```python
import functools
import math

import jax
import jax.numpy as jnp
from jax import lax
from jax.experimental import pallas as pl
from jax.experimental.pallas import tpu as pltpu

F32 = jnp.float32
BF16 = jnp.bfloat16

GRID_W = 64
N_MOD = 9
RMS_EPS = 1e-6
LN_EPS = 1e-5
NEG_INF = -1e30
GLA_HEADS, GLA_DK, GLA_DV, GLA_RANK, GLA_TAU, GLA_CHUNK = 4, 32, 64, 16, 16.0, 64
NA_HEADS, NA_DIM, NA_ROWS, NA_COLS = 4, 64, 8, 16
DIFF_HEADS, DIFF_DK, DIFF_DV = 4, 32, 64
DIFF_VROWS = 128
CONV_CH, CONV_K = 256, 31
ROPE_BASE = 10000.0
LOG2E = 1.4426950408889634

LANES = 128
VMEM_LIMIT = 56 * 1024 * 1024

ROW_TILE = 512
FF_CHUNK = 256
GLA_BLOCK = 256
NA_QROWS = 8
NA_KBLK = 256
CONV_TILE = 256
DIFF_TQ = 256
DIFF_TK = 256
DIFF_UNROLL = 16
DIFF_AHEAD = 3
DIFF_RING = 4
HALO = 16

C_GLA, C_NA, C_DIFF, C_CONV, C_AUX, C_END = 0, 768, 1536, 2304, 2816, 2944


def _dot(a, b):
    return jnp.dot(a, b, preferred_element_type=F32)


def _dot_nt(a, b):
    return lax.dot_general(a, b, (((1,), (1,)), ((), ())), preferred_element_type=F32)


def _params(*sem):
    return pltpu.CompilerParams(dimension_semantics=sem, vmem_limit_bytes=VMEM_LIMIT)


def _resident(shape):
    nd = len(shape)
    return pl.BlockSpec(shape, lambda *_: (0,) * nd, pipeline_mode=pl.Buffered(1))


def _silu(x):
    return x * jax.nn.sigmoid(x)


def _rms(x, w):
    return x * lax.rsqrt(jnp.mean(x * x, axis=-1, keepdims=True) + RMS_EPS) * w


def _ada_body(c_ref, w_ref, b_ref, o_ref):
    s = _silu(c_ref[...])
    o_ref[...] = jnp.dot(s, w_ref[...], precision=lax.Precision.HIGHEST,
                         preferred_element_type=F32) + b_ref[...]


def _ada(c_rows, ada_w, ada_b):
    depth, d, _ = ada_w.shape
    return pl.pallas_call(
        _ada_body,
        grid=(depth, N_MOD),
        in_specs=[pl.BlockSpec((8, d), lambda l, n: (0, 0)),
                  pl.BlockSpec((None, d, d), lambda l, n: (l, 0, n)),
                  pl.BlockSpec((None, 1, d), lambda l, n: (l, 0, n))],
        out_specs=pl.BlockSpec((None, 8, d), lambda l, n: (l, 0, n)),
        out_shape=jax.ShapeDtypeStruct((depth, 8, N_MOD * d), F32),
        compiler_params=_params("arbitrary", "arbitrary"),
        name="ada",
    )(c_rows, ada_w, ada_b.reshape(depth, 1, N_MOD * d))


def _ffn_body(*refs, k0, final, source, n_first):
    h_ref = refs[0]
    n_extra = {"plain": 0, "two_arrays": 1, "mixers": 6}[source]
    extra = refs[1:1 + n_extra]
    mod_ref, nw_ref, w13_ref, w2_ref, fw_ref, o_ref, xb_ref, g_ref, h13_ref = refs[1 + n_extra:]
    ff = w2_ref.shape[0]
    tf = h13_ref.shape[2] // 2
    n_chunks = ff // tf
    x = h_ref[...]
    if source == "two_arrays":
        x = jnp.where(pl.program_id(0) < n_first, x, extra[0][...])
    elif source == "mixers":
        gx_ref, gxc_ref, nx_ref, dx_ref, cx_ref, wo_ref = extra
        gx = jnp.where(pl.program_id(0) < n_first, gx_ref[...], gxc_ref[...].reshape(gx_ref.shape))
        mix = jnp.concatenate([gx, nx_ref[...]] + [dx_ref[hh] for hh in range(DIFF_HEADS)]
                              + [cx_ref[...]], axis=-1).astype(BF16)
        x = x + mod_ref[5:6, :] * _dot(mix, wo_ref[...])
    xm = _rms(x, nw_ref[...]) * (1.0 + mod_ref[k0 + 1:k0 + 2, :]) + mod_ref[k0:k0 + 1, :]
    xb_ref[...] = xm.astype(BF16)

    def cols(c):
        return c * tf if isinstance(c, int) else pl.multiple_of(c * tf, tf)

    def up(c, slot):
        h13_ref[slot, :, :tf] = _dot(xb_ref[...], w13_ref[:, pl.ds(cols(c), tf)])
        h13_ref[slot, :, tf:] = _dot(xb_ref[...], w13_ref[:, pl.ds(ff + cols(c), tf)])

    def gate(c, slot):
        a = h13_ref[slot, :, :tf]
        u = h13_ref[slot, :, tf:]
        g_ref[:, pl.ds(cols(c), tf)] = (_silu(a) * u).astype(BF16)

    def pair(t, carry):
        up(2 * t + 1, 1)
        gate(2 * t, 0)
        up(2 * t + 2, 0)
        gate(2 * t + 1, 1)
        return carry

    up(0, 0)
    lax.fori_loop(0, (n_chunks - 1) // 2, pair, 0)
    gate(n_chunks - 1, 0)
    out = x + (0.5 * mod_ref[k0 + 2:k0 + 3, :]) * _dot(g_ref[...], w2_ref[...])
    if final:
        out = _rms(out, fw_ref[...])
    o_ref[...] = out


def _ffn(h, mods, norm_w, w13, w2, final_w, *, k0, n_tiles, group_of, final, h_ctx=None, mixers=None):
    d = h.shape[1]
    tm = ROW_TILE
    ff = w2.shape[0]
    n_chunks = ff // FF_CHUNK
    assert ff % FF_CHUNK == 0 and n_chunks % 2 == 1
    row = lambda i: (i, 0)
    source, n_first, extra, extra_specs, h_spec = "plain", None, [], [], pl.BlockSpec((tm, d), row)
    if h_ctx is not None:
        source, n_first, extra = "two_arrays", h.shape[0] // tm, [h_ctx]
        h_spec = pl.BlockSpec((tm, d), lambda i: (jnp.minimum(i, n_first - 1), 0))
        extra_specs = [pl.BlockSpec((tm, d), lambda i: (jnp.maximum(i - n_first, 0), 0))]
    elif mixers is not None:
        gx, nx, dx, cx, w_out, (batch, seq, n_ctx) = mixers
        assert batch * n_ctx == tm and seq % tm == 0 and seq % n_ctx == 0
        source, n_first, extra = "mixers", batch * seq // tm, [gx, gx, nx, dx, cx, w_out.astype(BF16)]
        per_seq = seq // tm
        extra_specs = [pl.BlockSpec((None, tm, 256),
                                    lambda i: (jnp.minimum(i // per_seq, batch - 1), i % per_seq, 0)),
                       pl.BlockSpec((batch, n_ctx, 256), lambda i: (0, seq // n_ctx, 0)),
                       pl.BlockSpec((tm, 256), row),
                       pl.BlockSpec((DIFF_HEADS, tm, 64), lambda i: (0, i, 0)),
                       pl.BlockSpec((tm, 256), row), _resident(w_out.shape)]
    return pl.pallas_call(
        functools.partial(_ffn_body, k0=k0, final=final, source=source, n_first=n_first),
        grid=(n_tiles,),
        in_specs=[h_spec] + extra_specs + [
            pl.BlockSpec((None, N_MOD, d), lambda i: (group_of(i), 0, 0)),
            _resident((1, d)),
            _resident(w13.shape),
            _resident(w2.shape),
            _resident((1, d))],
        out_specs=pl.BlockSpec((tm, d), row),
        out_shape=jax.ShapeDtypeStruct((n_tiles * tm, d), F32),
        scratch_shapes=[pltpu.VMEM((tm, d), BF16), pltpu.VMEM((tm, ff), BF16),
                        pltpu.VMEM((2, tm, 2 * FF_CHUNK), F32)],
        compiler_params=_params("parallel"),
        name="ffn",
    )(h, *extra, mods, norm_w.reshape(1, d), w13.astype(BF16), w2.astype(BF16), final_w.reshape(1, d))


def _log_sigmoid(x):
    return jnp.minimum(x, 0.0) - jnp.log1p(jnp.exp(-jnp.abs(x)))


def _rope_rotate(x):
    n = x.shape[-1]
    lane = lax.broadcasted_iota(jnp.int32, x.shape, 1)
    up = pltpu.roll(x, n - 8, 1)
    dn = pltpu.roll(x, 8, 1)
    return jnp.where((lane & 15) < 8, -up, dn)


def _rope_rotate_rows(x):
    n = x.shape[0]
    row = lax.broadcasted_iota(jnp.int32, x.shape, 0)
    up = pltpu.roll(x, n - 8, 0)
    dn = pltpu.roll(x, 8, 0)
    return jnp.where((row & 15) < 8, -up, dn)


def _proj_body(h_ref, mod_ref, nw_ref, w_ref, wvt_ref, wdqt_ref, wdvt_ref, wa_ref, ba_ref,
               cos_ref, sin_ref, cost_ref, sint_ref,
               gqk_ref, gv_ref, gvt_ref, gg_ref, glg_ref, nq_ref, nk_ref, nv_ref,
               dqt_ref, dk_ref, dvt_ref, cu_ref, xb_ref):
    x = h_ref[...]
    tm = x.shape[0]
    xm = _rms(x, nw_ref[...]) * (1.0 + mod_ref[4:5, :]) + mod_ref[3:4, :]
    xb_ref[...] = xm.astype(BF16)

    z = _dot(xb_ref[...], w_ref[:, C_GLA:C_GLA + 256])
    lane = lax.broadcasted_iota(jnp.int32, (1, 256), 1)
    gqk_ref[...] = z * jnp.where(lane < 128, GLA_DK ** -0.5, 1.0)
    gv_ref[...] = _dot(xb_ref[...], w_ref[:, C_GLA + 256:C_GLA + 512])
    gvt_ref[...] = _dot_nt(wvt_ref[...], xb_ref[...])
    gg_ref[...] = _dot(xb_ref[...], w_ref[:, C_GLA + 512:C_GLA + 768])
    aux = _dot(xb_ref[...], w_ref[:, C_AUX:C_END])
    pre = _dot(aux.astype(BF16), wa_ref[...]) + ba_ref[...]
    glg_ref[...] = _log_sigmoid(pre) * (1.0 / GLA_TAU)

    nq_ref[...] = (_dot(xb_ref[...], w_ref[:, C_NA:C_NA + 256]) * (NA_DIM ** -0.5 * LOG2E)).astype(BF16)
    nk_ref[...] = _dot(xb_ref[...], w_ref[:, C_NA + 256:C_NA + 512]).astype(BF16)
    nv_ref[...] = _dot(xb_ref[...], w_ref[:, C_NA + 512:C_NA + 768]).astype(BF16)

    cos = cos_ref[...]
    sin = sin_ref[...]
    cos2 = jnp.concatenate([cos, cos], axis=1)
    sin2 = jnp.concatenate([sin, sin], axis=1)
    zk = _dot(xb_ref[...], w_ref[:, C_DIFF + 256:C_DIFF + 512])
    zk = zk * cos2 + _rope_rotate(zk) * sin2
    n_rep = 2 * DIFF_HEADS
    cos_t = jnp.concatenate([cost_ref[...]] * n_rep, axis=0)
    sin_t = jnp.concatenate([sint_ref[...]] * n_rep, axis=0)
    zqt = _dot_nt(wdqt_ref[...], xb_ref[...])
    zqt = (zqt * cos_t + _rope_rotate_rows(zqt) * sin_t) * (DIFF_DK ** -0.5 * LOG2E)
    zvt = _dot_nt(wdvt_ref[...], xb_ref[...])
    pad_rows = DIFF_VROWS - DIFF_DV
    one_row = jnp.where(lax.broadcasted_iota(jnp.int32, (pad_rows, tm), 0) == 0, 1.0, 0.0).astype(BF16)
    for hh in range(DIFF_HEADS):
        sl = slice(64 * hh, 64 * hh + 64)
        dk_ref[hh] = zk[:, sl].astype(BF16)
        dqt_ref[hh] = zqt[sl, :].astype(BF16)
        dvt_ref[hh, 0:64, :] = zvt[sl, :].astype(BF16)
        dvt_ref[hh, DIFF_DV:DIFF_VROWS, :] = one_row

    za = _dot(xb_ref[...], w_ref[:, C_CONV:C_CONV + 256])
    zg = _dot(xb_ref[...], w_ref[:, C_CONV + 256:C_CONV + 512])
    cu_ref[...] = za * jax.nn.sigmoid(zg)


def _proj(h, mods, norm_w, w_p, wa, ba, rope, *, n_tiles, group_of, pos_of):
    r, d = h.shape
    tm = ROW_TILE
    row = lambda i: (i, 0)
    hrow = lambda i: (0, i, 0)
    hcol = lambda i: (0, 0, i)
    f32_256 = jax.ShapeDtypeStruct((r, 256), F32)
    bf_256 = jax.ShapeDtypeStruct((r, 256), BF16)
    wvt = w_p[:, C_GLA + 256:C_GLA + 512].T
    wdqt = w_p[:, C_DIFF:C_DIFF + 256].T
    wdvt = w_p[:, C_DIFF + 512:C_DIFF + 768].T
    cos_r, sin_r, cos_c, sin_c = rope
    out_shape = [f32_256, f32_256, jax.ShapeDtypeStruct((256, r), F32), f32_256, f32_256,
                 bf_256, bf_256, bf_256,
                 jax.ShapeDtypeStruct((DIFF_HEADS, 64, r), BF16),
                 jax.ShapeDtypeStruct((DIFF_HEADS, r, 64), BF16),
                 jax.ShapeDtypeStruct((DIFF_HEADS, DIFF_VROWS, r), BF16),
                 f32_256]
    out_specs = [pl.BlockSpec((tm, 256), row)] * 2 + [pl.BlockSpec((256, tm), lambda i: (0, i))] + [
        pl.BlockSpec((tm, 256), row)] * 5 + [
        pl.BlockSpec((DIFF_HEADS, 64, tm), hcol),
        pl.BlockSpec((DIFF_HEADS, tm, 64), hrow),
        pl.BlockSpec((DIFF_HEADS, DIFF_VROWS, tm), hcol),
        pl.BlockSpec((tm, 256), row)]
    return pl.pallas_call(
        _proj_body,
        grid=(n_tiles,),
        in_specs=[pl.BlockSpec((tm, d), row),
                  pl.BlockSpec((None, N_MOD, d), lambda i: (group_of(i), 0, 0)),
                  _resident((1, d)),
                  _resident(w_p.shape),
                  _resident(wvt.shape),
                  _resident(wdqt.shape),
                  _resident(wdvt.shape),
                  _resident(wa.shape),
                  _resident(ba.shape),
                  pl.BlockSpec((tm, LANES), lambda i: (pos_of(i), 0)),
                  pl.BlockSpec((tm, LANES), lambda i: (pos_of(i), 0)),
                  pl.BlockSpec((DIFF_DK, tm), lambda i: (0, pos_of(i))),
                  pl.BlockSpec((DIFF_DK, tm), lambda i: (0, pos_of(i)))],
        out_specs=out_specs,
        out_shape=out_shape,
        scratch_shapes=[pltpu.VMEM((tm, d), BF16)],
        compiler_params=_params("parallel"),
        name="proj",
    )(h, mods, norm_w.reshape(1, d), w_p, wvt, wdqt, wdvt, wa, ba, cos_r, sin_r, cos_c, sin_c)


def _permute_w_in(w_in):
    d = w_in.shape[0]
    g0 = 2 * GLA_HEADS * GLA_DK + 2 * GLA_HEADS * GLA_DV
    aux = w_in[:, g0:g0 + 2 * GLA_RANK]
    rest = w_in[:, g0 + 2 * GLA_RANK:]
    pad = jnp.zeros((d, C_END - C_AUX - 2 * GLA_RANK), w_in.dtype)
    return jnp.concatenate([w_in[:, :g0], rest, aux, pad], axis=1).astype(BF16)


def _gate_weights(wa_f, ba_f, wa_b, ba_b):
    n = GLA_HEADS * GLA_DK
    wa = jnp.zeros((C_END - C_AUX, 2 * n), F32)
    wa = wa.at[:GLA_RANK, :n].set(wa_f).at[GLA_RANK:2 * GLA_RANK, n:].set(wa_b)
    return wa.astype(BF16), jnp.concatenate([ba_f, ba_b]).reshape(1, 2 * n)


def _rope_tables(seq, tile):
    t = jnp.arange(seq)
    row = (t // GRID_W).astype(F32)
    col = (t % GRID_W).astype(F32)
    half = DIFF_DK // 2
    inv = 1.0 / (ROPE_BASE ** (jnp.arange(0, half, 2, dtype=F32) / half))
    ang_r = row[:, None] * inv
    ang_c = col[:, None] * inv
    ang = jnp.concatenate([ang_r, ang_r, ang_c, ang_c], axis=-1)
    cos = jnp.concatenate([jnp.cos(ang), jnp.ones((tile, DIFF_DK), F32)], axis=0)
    sin = jnp.concatenate([jnp.sin(ang), jnp.zeros((tile, DIFF_DK), F32)], axis=0)
    rep = (1, LANES // DIFF_DK)
    return jnp.tile(cos, rep), jnp.tile(sin, rep), cos.T, sin.T


def _split3(x):
    hi = x.astype(BF16)
    r1 = x - hi.astype(F32)
    mid = r1.astype(BF16)
    lo = (r1 - mid.astype(F32)).astype(BF16)
    return hi, mid, lo


def _gla_body(*refs, reverse, batch):
    per_b = [refs[5 * b:5 * b + 5] for b in range(batch)]
    of_ref, nw_ref, o_ref, st_ref = refs[5 * batch:]
    blk = per_b[0][0].shape[0]
    n_chunks = blk // GLA_CHUNK
    nqk = GLA_HEADS * GLA_DK
    nv = GLA_HEADS * GLA_DV

    @pl.when(pl.program_id(0) == 0)
    def _():
        st_ref[...] = jnp.zeros_like(st_ref)

    ri = lax.broadcasted_iota(jnp.int32, (blk, blk), 0)
    ci = lax.broadcasted_iota(jnp.int32, (blk, blk), 1)
    same = (ri // GLA_CHUNK) == (ci // GLA_CHUNK)
    tri = jnp.where(same & ((ci >= ri) if reverse else (ci <= ri)), 1.0, 0.0).astype(BF16)
    bcum_of = []
    for qk_ref, v_ref, vt_ref, lg_ref, g_ref in per_b:
        lg = lg_ref[:, nqk:2 * nqk] if reverse else lg_ref[:, 0:nqk]
        hi, mid, lo = _split3(lg)
        bcum_of.append(_dot(tri, hi) + _dot(tri, mid) + _dot(tri, lo))

    c = GLA_CHUNK
    rk = lax.broadcasted_iota(jnp.int32, (GLA_HEADS * c, nqk), 0) // c
    ck = lax.broadcasted_iota(jnp.int32, (GLA_HEADS * c, nqk), 1) // GLA_DK
    mask_k = rk == ck
    rv = lax.broadcasted_iota(jnp.int32, (GLA_HEADS * c, nv), 0) // c
    cv = lax.broadcasted_iota(jnp.int32, (GLA_HEADS * c, nv), 1) // GLA_DV
    mask_v = rv == cv
    rs = lax.broadcasted_iota(jnp.int32, (nv, nqk), 0) // GLA_DV
    cs = lax.broadcasted_iota(jnp.int32, (nv, nqk), 1) // GLA_DK
    mask_s = rs == cs
    ai = lax.broadcasted_iota(jnp.int32, (c, GLA_HEADS * c), 0)
    aj = lax.broadcasted_iota(jnp.int32, (c, GLA_HEADS * c), 1) % c
    mask_a = (aj >= ai) if reverse else (aj <= ai)

    order = range(n_chunks - 1, -1, -1) if reverse else range(n_chunks)
    units = [(b, ch) for ch in order for b in range(batch)]
    q_in, a_raw, u_t, decay = {}, {}, {}, {}
    vts = [refs_b[2][...].astype(BF16) for refs_b in per_b]
    for b, ch in units:
        qk_ref = per_b[b][0]
        rows = slice(ch * c, (ch + 1) * c)
        k = qk_ref[rows, nqk:2 * nqk]
        bcum = bcum_of[b][rows, :]
        btot = bcum[0:1, :] if reverse else bcum[c - 1:c, :]
        q_in[b, ch] = (qk_ref[rows, 0:nqk] * jnp.exp(bcum)).astype(BF16)
        k_in = k * jnp.exp(-bcum)
        k_out = (k * jnp.exp(btot - bcum)).astype(BF16)
        k_bd = jnp.where(mask_k, jnp.concatenate([k_in] * GLA_HEADS, axis=0), 0.0).astype(BF16)
        a_raw[b, ch] = _dot_nt(q_in[b, ch], k_bd)
        k_pad = jnp.concatenate([jnp.zeros((n * c, nqk), BF16) for n in (ch,) if n] + [k_out]
                                + [jnp.zeros((n * c, nqk), BF16) for n in (n_chunks - 1 - ch,) if n], axis=0)
        u_t[b, ch] = jnp.where(mask_s, _dot(vts[b], k_pad), 0.0)
        decay[b, ch] = jnp.exp(btot)
    o = {}
    for b, ch in units:
        v = per_b[b][1][ch * c:(ch + 1) * c, :]
        v_bd = jnp.where(mask_v, jnp.concatenate([v] * GLA_HEADS, axis=0), 0.0).astype(BF16)
        o[b, ch] = _dot(jnp.where(mask_a, a_raw[b, ch], 0.0).astype(BF16), v_bd)
    st = [st_ref[b] for b in range(batch)]
    for b, ch in units:
        o[b, ch] = o[b, ch] + _dot_nt(q_in[b, ch], st[b].astype(BF16))
        st[b] = st[b] * decay[b, ch] + u_t[b, ch]
    for b in range(batch):
        st_ref[b] = st[b]
        o_blk = jnp.concatenate([o[b, ch] for ch in range(n_chunks)], axis=0)
        if reverse:
            o_blk = o_blk + of_ref[b]
            hi2, lo2, _ = _split3(o_blk * o_blk)
            hr = lax.broadcasted_iota(jnp.int32, (nv, nv), 0) // GLA_DV
            hc = lax.broadcasted_iota(jnp.int32, (nv, nv), 1) // GLA_DV
            seg = jnp.where(hr == hc, 1.0, 0.0).astype(BF16)
            ms = (_dot(hi2, seg) + _dot(lo2, seg)) * (1.0 / GLA_DV)
            o_blk = o_blk * lax.rsqrt(ms + RMS_EPS) * nw_ref[...] * _silu(per_b[b][4][...])
        o_ref[b] = o_blk


def _gla(gqk, gv, gvt, glg, gg, o_f, norm_w4, *, reverse, batch, seq, ctx):
    blk = GLA_BLOCK
    nc, nl = ctx // blk, seq // blk
    ctx_base = batch * seq // blk

    def step_blk(s):
        if reverse:
            return jnp.where(s < nc, nl + (nc - 1 - s), nl - 1 - (s - nc))
        return jnp.where(s < nc, nl + s, s - nc)

    def row_blk(b):
        return lambda s: jnp.where(s < nc, ctx_base + b * nc - nl, b * nl) + step_blk(s)

    specs, args = [], []
    for b in range(batch):
        spec = pl.BlockSpec((blk, 256), lambda s, f=row_blk(b): (f(s), 0))
        spec_t = pl.BlockSpec((256, blk), lambda s, f=row_blk(b): (0, f(s)))
        specs += [spec, spec, spec_t, spec, spec]
        args += [gqk, gv, gvt, glg, gg]
    seq_spec = pl.BlockSpec((batch, blk, 256), lambda s: (0, step_blk(s), 0))
    if o_f is None:
        o_f, of_spec = norm_w4, _resident((1, 256))
    else:
        of_spec = seq_spec
    return pl.pallas_call(
        functools.partial(_gla_body, reverse=reverse, batch=batch),
        grid=(nc + nl,),
        in_specs=specs + [of_spec, _resident((1, 256))],
        out_specs=seq_spec,
        out_shape=jax.ShapeDtypeStruct((batch, seq + ctx, 256), F32),
        scratch_shapes=[pltpu.VMEM((batch, GLA_HEADS * GLA_DV, GLA_HEADS * GLA_DK), F32)],
        compiler_params=_params("arbitrary"),
        name="gla_bwd" if reverse else "gla_fwd",
    )(*args, o_f, norm_w4)


def _na_col_tables(rpb):
    col = jnp.arange(GRID_W)
    cstart = jnp.clip(col - NA_COLS // 2, 0, GRID_W - NA_COLS)
    col_ok = (col[None, :] >= cstart[:, None]) & (col[None, :] < cstart[:, None] + NA_COLS)
    col_off = jnp.clip(col[None, :] - col[:, None] + (NA_COLS - 1), 0, 2 * NA_COLS - 2)
    t = jnp.where(col_ok[None, None], rpb[:, :, col_off] * LOG2E, NEG_INF)
    dead = jnp.full((NA_HEADS, 1, GRID_W, GRID_W), NEG_INF, F32)
    t = jnp.concatenate([dead, t.astype(F32), dead], axis=1)
    return jnp.concatenate([t[:, :-1], t[:, 1:]], axis=-1)


def _na_body(q_ref, k0, k1, k2, k3, v0, v1, v2, v3, kc_ref, vc_ref, tab_ref, o_ref, *, rows, key_blocks):
    q = q_ref[...]
    kw = jnp.concatenate([k0[...], k1[...], k2[...], k3[...]], axis=0)
    vw = jnp.concatenate([v0[...], v1[...], v2[...], v3[...]], axis=0)
    kc = kc_ref[...]
    vc = vc_ref[...]

    j = pl.program_id(1)
    rows_per_blk = NA_KBLK // GRID_W
    r0 = j * NA_QROWS
    kr0 = jnp.clip(2 * j - 1, 0, key_blocks - 4) * rows_per_blk
    lane_lo = lax.broadcasted_iota(jnp.int32, (1, 2 * GRID_W), 1) < GRID_W
    n_pairs = 4 * rows_per_blk // 2
    entry, ok = [], []
    for a in range(NA_QROWS):
        r = r0 + a
        start = jnp.clip(r - NA_ROWS // 2, 0, rows - NA_ROWS)
        for bp in range(n_pairs):
            rk = kr0 + 2 * bp
            entry.append(jnp.clip(rk - r + NA_ROWS, 0, 2 * NA_ROWS - 1))
            in0 = ((rk >= start) & (rk < start + NA_ROWS)).astype(jnp.int32)
            in1 = ((rk + 1 >= start) & (rk + 1 < start + NA_ROWS)).astype(jnp.int32)
            ok.append(jnp.where(lane_lo, in0, in1) != 0)

    def bias_of(hh):
        rows_ = []
        for a in range(NA_QROWS):
            tiles = [jnp.where(ok[a * n_pairs + bp], tab_ref[hh, entry[a * n_pairs + bp]], NEG_INF)
                     for bp in range(n_pairs)]
            rows_.append(jnp.concatenate(tiles, axis=1))
        return jnp.concatenate(rows_, axis=0)

    lane = lax.broadcasted_iota(jnp.int32, (1, NA_HEADS * NA_DIM), 1) // NA_DIM
    out = jnp.zeros(o_ref.shape, F32)
    for hh in range(NA_HEADS):
        mh = lane == hh
        qh = jnp.where(mh, q, jnp.zeros_like(q))
        s_w = _dot_nt(qh, kw) + bias_of(hh)
        s_c = _dot_nt(qh, kc)
        m = jnp.maximum(jnp.max(s_w, axis=-1, keepdims=True), jnp.max(s_c, axis=-1, keepdims=True))
        p_w = jnp.exp2(s_w - m)
        p_c = jnp.exp2(s_c - m)
        l = jnp.sum(p_w, axis=-1, keepdims=True) + jnp.sum(p_c, axis=-1, keepdims=True)
        o = _dot(p_w.astype(BF16), vw) + _dot(p_c.astype(BF16), vc)
        out = out + jnp.where(mh, o / l, 0.0)
    o_ref[...] = out


def _na(nq, nk, nv, tab, *, batch, seq, ctx):
    r = nq.shape[0]
    tq = NA_QROWS * GRID_W
    nj = seq // tq
    kb = seq // NA_KBLK
    assert kb >= 4 and seq % tq == 0 and seq % ctx == 0
    ctx_blk0 = batch * seq // ctx

    def kspec(i):
        return pl.BlockSpec((NA_KBLK, 256),
                            lambda b, j: (b * kb + jnp.clip(2 * j - 1, 0, kb - 4) + i, 0))

    cspec = pl.BlockSpec((ctx, 256), lambda b, j: (ctx_blk0 + b, 0))
    return pl.pallas_call(
        functools.partial(_na_body, rows=seq // GRID_W, key_blocks=kb),
        grid=(batch, nj),
        in_specs=[pl.BlockSpec((tq, 256), lambda b, j: (b * nj + j, 0))]
                 + [kspec(i) for i in range(4)] + [kspec(i) for i in range(4)]
                 + [cspec, cspec, _resident(tab.shape)],
        out_specs=pl.BlockSpec((tq, 256), lambda b, j: (b * nj + j, 0)),
        out_shape=jax.ShapeDtypeStruct((r, 256), F32),
        compiler_params=_params("parallel", "arbitrary"),
        name="na",
    )(nq, nk, nk, nk, nk, nv, nv, nv, nv, nk, nv, tab)


def _diff_lambda(lq_ref, lambda_init):
    lq = lq_ref[...]
    s1 = jnp.sum(lq[0:1, :] * lq[1:2, :], axis=-1, keepdims=True)
    s2 = jnp.sum(lq[2:3, :] * lq[3:4, :], axis=-1, keepdims=True)
    return jnp.exp(s1) - jnp.exp(s2) + lambda_init


def _stack_qt(qt):
    row = lax.broadcasted_iota(jnp.int32, (2 * DIFF_DK, 1), 0)
    zero = jnp.zeros_like(qt)
    return jnp.concatenate([jnp.where(row < DIFF_DK, qt, zero), jnp.where(row >= DIFF_DK, qt, zero)], axis=1)


def _diff_finish_t(acc, tq, lam, nw_col, lambda_init):
    o = acc[0:DIFF_DV, :] / acc[DIFF_DV:DIFF_DV + 1, :]
    od = o[:, :tq] - lam * o[:, tq:]
    ms = jnp.mean(od * od, axis=0, keepdims=True)
    y = od * lax.rsqrt(ms + RMS_EPS) * nw_col * (1.0 - lambda_init)
    y = jnp.concatenate([y, jnp.zeros_like(y)], axis=0)
    return jnp.transpose(y)[:, 0:DIFF_DV]


def _diff_body(qt_ref, kc_ref, vtc_ref, k_ref, vt_ref, lq_ref, nw_ref, o_ref, s_ref, *, tk, unroll, lambda_init):
    tq = qt_ref.shape[1]
    n_chunks = k_ref.shape[0] // tk
    ring = s_ref.shape[0]
    qs = _stack_qt(qt_ref[...])

    def chunk(j):
        return pl.ds(j * tk if isinstance(j, int) else pl.multiple_of(j * tk, tk), tk)

    def issue_scores(j, slot):
        s = _dot(k_ref[chunk(j), :], qs)
        s_ref[slot] = s
        return jnp.max(s, axis=0, keepdims=True)

    s_ctx = _dot(kc_ref[...], qs)
    m0 = jnp.max(s_ctx, axis=0, keepdims=True)
    cmax0 = tuple(issue_scores(min(a, n_chunks - 1), a % ring) for a in range(DIFF_AHEAD))
    acc0 = _dot(vtc_ref[...], jnp.exp2(s_ctx - m0).astype(BF16))

    def group(g, carry, tail=False):
        cmax, m, acc = carry
        for u in range(unroll):
            j = g * unroll + u
            if tail and j + DIFF_AHEAD >= n_chunks:
                c_new = cmax[0]
            else:
                c_new = issue_scores(j + DIFF_AHEAD, (u + DIFF_AHEAD) % ring)
            m_new = jnp.maximum(m, cmax[0])
            p = jnp.exp2(s_ref[u % ring] - m_new).astype(BF16)
            vt = vt_ref[:, chunk(j)]
            acc = jnp.exp2(m - m_new) * acc + _dot(vt, p)
            m, cmax = m_new, cmax[1:] + (c_new,)
        return cmax, m, acc

    n_groups = n_chunks // unroll
    carry = lax.fori_loop(0, n_groups - 1, group, (cmax0, m0, acc0))
    _, _, acc = group(n_groups - 1, carry, tail=True)
    o_ref[...] = _diff_finish_t(acc, tq, _diff_lambda(lq_ref, lambda_init), nw_ref[...], lambda_init)


def _diff(dqt, dk, dvt, lq, norm_w, *, batch, seq, ctx, lambda_init):
    h, r, _ = dk.shape
    tq = DIFF_TQ
    nq = seq // tq
    tk = min(DIFF_TK, seq)
    unroll = min(DIFF_UNROLL, seq // tk)
    assert (seq // tk) % unroll == 0 and unroll % DIFF_RING == 0 and DIFF_RING > DIFF_AHEAD
    ctx_blk0 = batch * seq // ctx
    return pl.pallas_call(
        functools.partial(_diff_body, tk=tk, unroll=unroll, lambda_init=lambda_init),
        grid=(batch, h, nq),
        in_specs=[pl.BlockSpec((None, 64, tq), lambda b, hh, i: (hh, 0, b * nq + i)),
                  pl.BlockSpec((None, ctx, 64), lambda b, hh, i: (hh, ctx_blk0 + b, 0)),
                  pl.BlockSpec((None, DIFF_VROWS, ctx), lambda b, hh, i: (hh, 0, ctx_blk0 + b)),
                  pl.BlockSpec((None, seq, 64), lambda b, hh, i: (hh, b, 0)),
                  pl.BlockSpec((None, DIFF_VROWS, seq), lambda b, hh, i: (hh, 0, b)),
                  _resident(lq.shape),
                  _resident((DIFF_DV, 1))],
        out_specs=pl.BlockSpec((None, tq, 64), lambda b, hh, i: (hh, b * nq + i, 0)),
        out_shape=jax.ShapeDtypeStruct((h, r, 64), F32),
        scratch_shapes=[pltpu.VMEM((DIFF_RING, tk, 2 * tq), F32)],
        compiler_params=_params("parallel", "parallel", "arbitrary"),
        name="diff",
    )(dqt, dk, dvt, dk, dvt, lq, norm_w.reshape(DIFF_DV, 1))


def _ctx_attn_body(nq_ref, nk_ref, nv_ref, dqt_ref, dk_ref, dvt_ref, lq_ref, nw_ref, na_in, df_in,
                   na_out, df_out, *, lambda_init):
    del na_in, df_in
    q = nq_ref[...]
    k = nk_ref[...]
    v = nv_ref[...]
    lane = lax.broadcasted_iota(jnp.int32, (1, NA_HEADS * NA_DIM), 1) // NA_DIM
    out = jnp.zeros(na_out.shape, F32)
    for hh in range(NA_HEADS):
        mh = lane == hh
        s = _dot_nt(jnp.where(mh, q, jnp.zeros_like(q)), k)
        p = jnp.exp2(s - jnp.max(s, axis=-1, keepdims=True))
        o = _dot(p.astype(BF16), v) / jnp.sum(p, axis=-1, keepdims=True)
        out = out + jnp.where(mh, o, 0.0)
    na_out[...] = out

    lam = _diff_lambda(lq_ref, lambda_init)
    tq = dqt_ref.shape[2]
    for hh in range(DIFF_HEADS):
        s = _dot(dk_ref[hh], _stack_qt(dqt_ref[hh]))
        p = jnp.exp2(s - jnp.max(s, axis=0, keepdims=True))
        acc = _dot(dvt_ref[hh], p.astype(BF16))
        df_out[hh] = _diff_finish_t(acc, tq, lam, nw_ref[...], lambda_init)


def _ctx_attn(nq, nk, nv, dqt, dk, dvt, lq, norm_w, na_o, df_o, *, batch, seq, ctx, lambda_init):
    blk0 = batch * seq // ctx
    s256 = pl.BlockSpec((ctx, 256), lambda b: (blk0 + b, 0))
    sh64 = pl.BlockSpec((DIFF_HEADS, ctx, 64), lambda b: (0, blk0 + b, 0))
    sq_t = pl.BlockSpec((DIFF_HEADS, 64, ctx), lambda b: (0, 0, blk0 + b))
    sv_t = pl.BlockSpec((DIFF_HEADS, DIFF_VROWS, ctx), lambda b: (0, 0, blk0 + b))
    return pl.pallas_call(
        functools.partial(_ctx_attn_body, lambda_init=lambda_init),
        grid=(batch,),
        in_specs=[s256, s256, s256, sq_t, sh64, sv_t, _resident(lq.shape), _resident((DIFF_DV, 1)),
                  pl.BlockSpec(memory_space=pl.ANY), pl.BlockSpec(memory_space=pl.ANY)],
        out_specs=[s256, sh64],
        out_shape=[jax.ShapeDtypeStruct(na_o.shape, F32), jax.ShapeDtypeStruct(df_o.shape, F32)],
        input_output_aliases={8: 0, 9: 1},
        compiler_params=_params("arbitrary"),
        name="ctx_attn",
    )(nq, nk, nv, dqt, dk, dvt, lq, norm_w.reshape(DIFF_DV, 1), na_o, df_o)


def _conv_body(u_ref, prev_ref, next_ref, dw_ref, dwb_ref, lng_ref, lnb_ref, pw_ref, pwb_ref,
               o_ref, pad_ref, sh_ref, *, tiles_per_seq, n_latent_tiles):
    t = u_ref.shape[0]
    i = pl.program_id(0)
    in_ctx = i >= n_latent_tiles
    first = in_ctx | (i % tiles_per_seq == 0)
    last = in_ctx | (i % tiles_per_seq == tiles_per_seq - 1)
    pad_ref[0:HALO, :] = jnp.where(first, 0.0, prev_ref[...])
    pad_ref[HALO:HALO + t, :] = u_ref[...]
    pad_ref[HALO + t:HALO + t + HALO, :] = jnp.where(last, 0.0, next_ref[...])
    base = HALO - CONV_K // 2
    first = {}
    for phase in range(8):
        taps = [k for k in range(CONV_K) if (base + k) % 8 == phase]
        first[phase] = base + taps[0]
        span = base + taps[-1] + t - first[phase]
        sh_ref[phase, 0:span, :] = pad_ref[first[phase]:first[phase] + span, :]
    sub = 64
    parts = []
    for r0 in range(0, t, sub):
        acc = jnp.zeros((sub, CONV_CH), F32)
        for k in range(CONV_K):
            phase = (base + k) % 8
            off = base + k - first[phase] + r0
            acc = acc + dw_ref[k:k + 1, :] * sh_ref[phase, off:off + sub, :]
        parts.append(acc)
    y = jnp.concatenate(parts, axis=0) + dwb_ref[...]
    mu = jnp.mean(y, axis=-1, keepdims=True)
    yc = y - mu
    var = jnp.mean(yc * yc, axis=-1, keepdims=True)
    y = _silu(yc * lax.rsqrt(var + LN_EPS) * lng_ref[...] + lnb_ref[...])
    o_ref[...] = _dot(y.astype(BF16), pw_ref[...]) + pwb_ref[...]


def _conv(cu, dw, dw_b, ln_g, ln_b, pw, pw_b, *, batch, seq, ctx):
    r = cu.shape[0]
    t = CONV_TILE
    assert ctx == t and seq % t == 0
    n_tiles = r // t
    hb = t // HALO
    vec = lambda a: a.reshape(1, CONV_CH)
    return pl.pallas_call(
        functools.partial(_conv_body, tiles_per_seq=seq // t, n_latent_tiles=batch * seq // t),
        grid=(n_tiles,),
        in_specs=[pl.BlockSpec((t, CONV_CH), lambda i: (i, 0)),
                  pl.BlockSpec((HALO, CONV_CH), lambda i: (jnp.maximum(i * hb - 1, 0), 0)),
                  pl.BlockSpec((HALO, CONV_CH), lambda i: (jnp.minimum((i + 1) * hb, n_tiles * hb - 1), 0)),
                  _resident((CONV_K, CONV_CH))] + [_resident((1, CONV_CH))] * 3
                 + [_resident((CONV_CH, CONV_CH)), _resident((1, CONV_CH))],
        out_specs=pl.BlockSpec((t, CONV_CH), lambda i: (i, 0)),
        out_shape=jax.ShapeDtypeStruct((r, CONV_CH), F32),
        scratch_shapes=[pltpu.VMEM((t + 2 * HALO, CONV_CH), F32),
                        pltpu.VMEM((8, t + 2 * HALO, CONV_CH), F32)],
        compiler_params=_params("parallel"),
        name="conv",
    )(cu, cu, cu, dw, vec(dw_b), vec(ln_g), vec(ln_b), pw.astype(BF16), vec(pw_b))


def kernel(x, c, ctx, c_ctx, ada_w, ada_b, norm_ffn1, ffn1_w13, ffn1_w2, norm_mix, w_in, gla_wa_f, gla_ba_f, gla_wa_b, gla_ba_b, gla_norm, na_rpb, diff_lq1, diff_lk1, diff_lq2, diff_lk2, diff_norm, conv_dw, conv_dw_b, conv_ln_g, conv_ln_b, conv_pw, conv_pw_b, w_out, norm_ffn2, ffn2_w13, ffn2_w2, final_norm):
    batch, seq, d = x.shape
    n_ctx = ctx.shape[1]
    depth = ada_w.shape[0]
    rows = seq // GRID_W
    tm = ROW_TILE
    assert seq % tm == 0 and (batch * n_ctx) % tm == 0 and batch + 1 <= 8
    lat_tiles = batch * seq // tm
    all_tiles = lat_tiles + batch * n_ctx // tm
    tiles_per_batch = seq // tm

    def group_of(i):
        return jnp.minimum(i // tiles_per_batch, batch)

    def pos_of(i):
        return jnp.where(i < lat_tiles, i % tiles_per_batch, tiles_per_batch)

    c_rows = jnp.concatenate([c, c_ctx[None, :], jnp.zeros((8 - batch - 1, d), F32)], axis=0)
    mods_all = _ada(c_rows, ada_w, ada_b)[:, :batch + 1].reshape(depth, batch + 1, N_MOD, d)
    rope = _rope_tables(seq, tm)
    h = x.reshape(batch * seq, d)
    h_ctx = ctx.reshape(batch * n_ctx, d)

    for i in range(depth):
        last = i == depth - 1
        lambda_init = 0.8 - 0.6 * math.exp(-0.3 * i)
        mods = mods_all[i]
        tok = dict(n_tiles=all_tiles, group_of=group_of)
        geo = dict(batch=batch, seq=seq, ctx=n_ctx)

        h = _ffn(h, mods, norm_ffn1[i], ffn1_w13[i], ffn1_w2[i], final_norm, k0=0, final=False,
                 h_ctx=h_ctx if i == 0 else None, **tok)

        wa, ba = _gate_weights(gla_wa_f[i], gla_ba_f[i], gla_wa_b[i], gla_ba_b[i])
        (gqk, gv, gvt, gg, glg, nq, nk, nv, dq, dk, dv, cu) = _proj(
            h, mods, norm_mix[i], _permute_w_in(w_in[i]), wa, ba, rope, pos_of=pos_of, **tok)

        gnorm = jnp.tile(gla_norm[i], GLA_HEADS).reshape(1, GLA_HEADS * GLA_DV)
        o_f = _gla(gqk, gv, gvt, glg, gg, None, gnorm, reverse=False, **geo)
        gx = _gla(gqk, gv, gvt, glg, gg, o_f, gnorm, reverse=True, **geo)

        nx = _na(nq, nk, nv, _na_col_tables(na_rpb[i]), **geo)

        lq = jnp.stack([diff_lq1[i], diff_lk1[i], diff_lq2[i], diff_lk2[i]])
        dx = _diff(dq, dk, dv, lq, diff_norm[i], lambda_init=lambda_init, **geo)
        if not last:
            nx, dx = _ctx_attn(nq, nk, nv, dq, dk, dv, lq, diff_norm[i], nx, dx,
                               lambda_init=lambda_init, **geo)

        cx = _conv(cu, conv_dw[i], conv_dw_b[i], conv_ln_g[i], conv_ln_b[i], conv_pw[i], conv_pw_b[i], **geo)

        if last:
            tok = dict(n_tiles=lat_tiles, group_of=group_of)
        h = _ffn(h, mods, norm_ffn2[i], ffn2_w13[i], ffn2_w2[i], final_norm, k0=6, final=last,
                 mixers=(gx, nx, dx, cx, w_out[i], (batch, seq, n_ctx)), **tok)

    return h.reshape(batch, seq, d)
```

```python
import functools
import math

import jax
import jax.numpy as jnp
from jax import lax
from jax.experimental import pallas as pl
from jax.experimental.pallas import tpu as pltpu

F32 = jnp.float32
BF16 = jnp.bfloat16

GRID_W = 64
N_MOD = 9
RMS_EPS = 1e-6
LN_EPS = 1e-5
NEG_INF = -1e30
GLA_HEADS, GLA_DK, GLA_DV, GLA_RANK, GLA_TAU, GLA_CHUNK = 4, 32, 64, 16, 16.0, 64
NA_HEADS, NA_DIM, NA_ROWS, NA_COLS = 4, 64, 8, 16
DIFF_HEADS, DIFF_DK, DIFF_DV = 4, 32, 64
DIFF_VROWS = 128
CONV_CH, CONV_K = 256, 31
ROPE_BASE = 10000.0
LOG2E = 1.4426950408889634

LANES = 128
VMEM_LIMIT = 56 * 1024 * 1024

ROW_TILE = 512
FF_CHUNK = 256
GLA_BLOCK = 256
NA_QROWS = 8
NA_KBLK = 256
CONV_TILE = 256
DIFF_TQ = 256
DIFF_TK = 256
DIFF_UNROLL = 16
DIFF_AHEAD = 3
DIFF_RING = 4
HALO = 16

C_GLA, C_NA, C_DIFF, C_CONV, C_AUX, C_END = 0, 768, 1536, 2304, 2816, 2944


def _dot(a, b):
    return jnp.dot(a, b, preferred_element_type=F32)


def _dot_nt(a, b):
    return lax.dot_general(a, b, (((1,), (1,)), ((), ())), preferred_element_type=F32)


def _params(*sem):
    return pltpu.CompilerParams(dimension_semantics=sem, vmem_limit_bytes=VMEM_LIMIT)


def _resident(shape):
    nd = len(shape)
    return pl.BlockSpec(shape, lambda *_: (0,) * nd, pipeline_mode=pl.Buffered(1))


def _silu(x):
    return x * jax.nn.sigmoid(x)


def _rms(x, w):
    return x * lax.rsqrt(jnp.mean(x * x, axis=-1, keepdims=True) + RMS_EPS) * w


def _ada_body(c_ref, w_ref, b_ref, o_ref):
    s = _silu(c_ref[...])
    o_ref[...] = jnp.dot(s, w_ref[...], precision=lax.Precision.HIGHEST,
                         preferred_element_type=F32) + b_ref[...]


def _ada(c_rows, ada_w, ada_b):
    depth, d, _ = ada_w.shape
    return pl.pallas_call(
        _ada_body,
        grid=(depth, N_MOD),
        in_specs=[pl.BlockSpec((8, d), lambda l, n: (0, 0)),
                  pl.BlockSpec((None, d, d), lambda l, n: (l, 0, n)),
                  pl.BlockSpec((None, 1, d), lambda l, n: (l, 0, n))],
        out_specs=pl.BlockSpec((None, 8, d), lambda l, n: (l, 0, n)),
        out_shape=jax.ShapeDtypeStruct((depth, 8, N_MOD * d), F32),
        compiler_params=_params("arbitrary", "arbitrary"),
        name="ada",
    )(c_rows, ada_w, ada_b.reshape(depth, 1, N_MOD * d))


def _ffn_body(*refs, k0, final, source, n_first):
    h_ref = refs[0]
    n_extra = {"plain": 0, "two_arrays": 1, "mixers": 6}[source]
    extra = refs[1:1 + n_extra]
    mod_ref, nw_ref, w13_ref, w2_ref, fw_ref, o_ref, xb_ref, g_ref, h13_ref = refs[1 + n_extra:]
    ff = w2_ref.shape[0]
    tf = h13_ref.shape[2] // 2
    n_chunks = ff // tf
    x = h_ref[...]
    if source == "two_arrays":
        x = jnp.where(pl.program_id(0) < n_first, x, extra[0][...])
    elif source == "mixers":
        gx_ref, gxc_ref, nx_ref, dx_ref, cx_ref, wo_ref = extra
        gx = jnp.where(pl.program_id(0) < n_first, gx_ref[...], gxc_ref[...].reshape(gx_ref.shape))
        mix = jnp.concatenate([gx, nx_ref[...]] + [dx_ref[hh] for hh in range(DIFF_HEADS)]
                              + [cx_ref[...]], axis=-1).astype(BF16)
        x = x + mod_ref[5:6, :] * _dot(mix, wo_ref[...])
    xm = _rms(x, nw_ref[...]) * (1.0 + mod_ref[k0 + 1:k0 + 2, :]) + mod_ref[k0:k0 + 1, :]
    xb_ref[...] = xm.astype(BF16)

    def cols(c):
        return c * tf if isinstance(c, int) else pl.multiple_of(c * tf, tf)

    def up(c, slot):
        h13_ref[slot, :, :tf] = _dot(xb_ref[...], w13_ref[:, pl.ds(cols(c), tf)])
        h13_ref[slot, :, tf:] = _dot(xb_ref[...], w13_ref[:, pl.ds(ff + cols(c), tf)])

    def gate(c, slot):
        a = h13_ref[slot, :, :tf]
        u = h13_ref[slot, :, tf:]
        g_ref[:, pl.ds(cols(c), tf)] = (_silu(a) * u).astype(BF16)

    def pair(t, carry):
        up(2 * t + 1, 1)
        gate(2 * t, 0)
        up(2 * t + 2, 0)
        gate(2 * t + 1, 1)
        return carry

    up(0, 0)
    lax.fori_loop(0, (n_chunks - 1) // 2, pair, 0)
    gate(n_chunks - 1, 0)
    out = x + (0.5 * mod_ref[k0 + 2:k0 + 3, :]) * _dot(g_ref[...], w2_ref[...])
    if final:
        out = _rms(out, fw_ref[...])
    o_ref[...] = out


def _ffn(h, mods, norm_w, w13, w2, final_w, *, k0, n_tiles, group_of, final, h_ctx=None, mixers=None):
    d = h.shape[1]
    tm = ROW_TILE
    ff = w2.shape[0]
    n_chunks = ff // FF_CHUNK
    assert ff % FF_CHUNK == 0 and n_chunks % 2 == 1
    row = lambda i: (i, 0)
    source, n_first, extra, extra_specs, h_spec = "plain", None, [], [], pl.BlockSpec((tm, d), row)
    if h_ctx is not None:
        source, n_first, extra = "two_arrays", h.shape[0] // tm, [h_ctx]
        h_spec = pl.BlockSpec((tm, d), lambda i: (jnp.minimum(i, n_first - 1), 0))
        extra_specs = [pl.BlockSpec((tm, d), lambda i: (jnp.maximum(i - n_first, 0), 0))]
    elif mixers is not None:
        gx, nx, dx, cx, w_out, (batch, seq, n_ctx) = mixers
        assert batch * n_ctx == tm and seq % tm == 0 and seq % n_ctx == 0
        source, n_first, extra = "mixers", batch * seq // tm, [gx, gx, nx, dx, cx, w_out.astype(BF16)]
        per_seq = seq // tm
        extra_specs = [pl.BlockSpec((None, tm, 256),
                                    lambda i: (jnp.minimum(i // per_seq, batch - 1), i % per_seq, 0)),
                       pl.BlockSpec((batch, n_ctx, 256), lambda i: (0, seq // n_ctx, 0)),
                       pl.BlockSpec((tm, 256), row),
                       pl.BlockSpec((DIFF_HEADS, tm, 64), lambda i: (0, i, 0)),
                       pl.BlockSpec((tm, 256), row), _resident(w_out.shape)]
    return pl.pallas_call(
        functools.partial(_ffn_body, k0=k0, final=final, source=source, n_first=n_first),
        grid=(n_tiles,),
        in_specs=[h_spec] + extra_specs + [
            pl.BlockSpec((None, N_MOD, d), lambda i: (group_of(i), 0, 0)),
            _resident((1, d)),
            _resident(w13.shape),
            _resident(w2.shape),
            _resident((1, d))],
        out_specs=pl.BlockSpec((tm, d), row),
        out_shape=jax.ShapeDtypeStruct((n_tiles * tm, d), F32),
        scratch_shapes=[pltpu.VMEM((tm, d), BF16), pltpu.VMEM((tm, ff), BF16),
                        pltpu.VMEM((2, tm, 2 * FF_CHUNK), F32)],
        compiler_params=_params("parallel"),
        name="ffn",
    )(h, *extra, mods, norm_w.reshape(1, d), w13.astype(BF16), w2.astype(BF16), final_w.reshape(1, d))


def _log_sigmoid(x):
    return jnp.minimum(x, 0.0) - jnp.log1p(jnp.exp(-jnp.abs(x)))


def _rope_rotate(x):
    n = x.shape[-1]
    lane = lax.broadcasted_iota(jnp.int32, x.shape, 1)
    up = pltpu.roll(x, n - 8, 1)
    dn = pltpu.roll(x, 8, 1)
    return jnp.where((lane & 15) < 8, -up, dn)


def _rope_rotate_rows(x):
    n = x.shape[0]
    row = lax.broadcasted_iota(jnp.int32, x.shape, 0)
    up = pltpu.roll(x, n - 8, 0)
    dn = pltpu.roll(x, 8, 0)
    return jnp.where((row & 15) < 8, -up, dn)


def _proj_body(h_ref, mod_ref, nw_ref, w_ref, wvt_ref, wdqt_ref, wdvt_ref, wa_ref, ba_ref,
               cos_ref, sin_ref, cost_ref, sint_ref,
               gqk_ref, gv_ref, gvt_ref, gg_ref, glg_ref, nq_ref, nk_ref, nv_ref,
               dqt_ref, dk_ref, dvt_ref, cu_ref, xb_ref):
    x = h_ref[...]
    tm = x.shape[0]
    xm = _rms(x, nw_ref[...]) * (1.0 + mod_ref[4:5, :]) + mod_ref[3:4, :]
    xb_ref[...] = xm.astype(BF16)

    z = _dot(xb_ref[...], w_ref[:, C_GLA:C_GLA + 256])
    lane = lax.broadcasted_iota(jnp.int32, (1, 256), 1)
    gqk_ref[...] = z * jnp.where(lane < 128, GLA_DK ** -0.5, 1.0)
    gv_ref[...] = _dot(xb_ref[...], w_ref[:, C_GLA + 256:C_GLA + 512])
    gvt_ref[...] = _dot_nt(wvt_ref[...], xb_ref[...])
    gg_ref[...] = _dot(xb_ref[...], w_ref[:, C_GLA + 512:C_GLA + 768])
    aux = _dot(xb_ref[...], w_ref[:, C_AUX:C_END])
    pre = _dot(aux.astype(BF16), wa_ref[...]) + ba_ref[...]
    glg_ref[...] = _log_sigmoid(pre) * (1.0 / GLA_TAU)

    nq_ref[...] = (_dot(xb_ref[...], w_ref[:, C_NA:C_NA + 256]) * (NA_DIM ** -0.5 * LOG2E)).astype(BF16)
    nk_ref[...] = _dot(xb_ref[...], w_ref[:, C_NA + 256:C_NA + 512]).astype(BF16)
    nv_ref[...] = _dot(xb_ref[...], w_ref[:, C_NA + 512:C_NA + 768]).astype(BF16)

    cos = cos_ref[...]
    sin = sin_ref[...]
    cos2 = jnp.concatenate([cos, cos], axis=1)
    sin2 = jnp.concatenate([sin, sin], axis=1)
    zk = _dot(xb_ref[...], w_ref[:, C_DIFF + 256:C_DIFF + 512])
    zk = zk * cos2 + _rope_rotate(zk) * sin2
    n_rep = 2 * DIFF_HEADS
    cos_t = jnp.concatenate([cost_ref[...]] * n_rep, axis=0)
    sin_t = jnp.concatenate([sint_ref[...]] * n_rep, axis=0)
    zqt = _dot_nt(wdqt_ref[...], xb_ref[...])
    zqt = (zqt * cos_t + _rope_rotate_rows(zqt) * sin_t) * (DIFF_DK ** -0.5 * LOG2E)
    zvt = _dot_nt(wdvt_ref[...], xb_ref[...])
    pad_rows = DIFF_VROWS - DIFF_DV
    one_row = jnp.where(lax.broadcasted_iota(jnp.int32, (pad_rows, tm), 0) == 0, 1.0, 0.0).astype(BF16)
    for hh in range(DIFF_HEADS):
        sl = slice(64 * hh, 64 * hh + 64)
        dk_ref[hh] = zk[:, sl].astype(BF16)
        dqt_ref[hh] = zqt[sl, :].astype(BF16)
        dvt_ref[hh, 0:64, :] = zvt[sl, :].astype(BF16)
        dvt_ref[hh, DIFF_DV:DIFF_VROWS, :] = one_row

    za = _dot(xb_ref[...], w_ref[:, C_CONV:C_CONV + 256])
    zg = _dot(xb_ref[...], w_ref[:, C_CONV + 256:C_CONV + 512])
    cu_ref[...] = za * jax.nn.sigmoid(zg)


def _proj(h, mods, norm_w, w_p, wa, ba, rope, *, n_tiles, group_of, pos_of):
    r, d = h.shape
    tm = ROW_TILE
    row = lambda i: (i, 0)
    hrow = lambda i: (0, i, 0)
    hcol = lambda i: (0, 0, i)
    f32_256 = jax.ShapeDtypeStruct((r, 256), F32)
    bf_256 = jax.ShapeDtypeStruct((r, 256), BF16)
    wvt = w_p[:, C_GLA + 256:C_GLA + 512].T
    wdqt = w_p[:, C_DIFF:C_DIFF + 256].T
    wdvt = w_p[:, C_DIFF + 512:C_DIFF + 768].T
    cos_r, sin_r, cos_c, sin_c = rope
    out_shape = [f32_256, f32_256, jax.ShapeDtypeStruct((256, r), F32), f32_256, f32_256,
                 bf_256, bf_256, bf_256,
                 jax.ShapeDtypeStruct((DIFF_HEADS, 64, r), BF16),
                 jax.ShapeDtypeStruct((DIFF_HEADS, r, 64), BF16),
                 jax.ShapeDtypeStruct((DIFF_HEADS, DIFF_VROWS, r), BF16),
                 f32_256]
    out_specs = [pl.BlockSpec((tm, 256), row)] * 2 + [pl.BlockSpec((256, tm), lambda i: (0, i))] + [
        pl.BlockSpec((tm, 256), row)] * 5 + [
        pl.BlockSpec((DIFF_HEADS, 64, tm), hcol),
        pl.BlockSpec((DIFF_HEADS, tm, 64), hrow),
        pl.BlockSpec((DIFF_HEADS, DIFF_VROWS, tm), hcol),
        pl.BlockSpec((tm, 256), row)]
    return pl.pallas_call(
        _proj_body,
        grid=(n_tiles,),
        in_specs=[pl.BlockSpec((tm, d), row),
                  pl.BlockSpec((None, N_MOD, d), lambda i: (group_of(i), 0, 0)),
                  _resident((1, d)),
                  _resident(w_p.shape),
                  _resident(wvt.shape),
                  _resident(wdqt.shape),
                  _resident(wdvt.shape),
                  _resident(wa.shape),
                  _resident(ba.shape),
                  pl.BlockSpec((tm, LANES), lambda i: (pos_of(i), 0)),
                  pl.BlockSpec((tm, LANES), lambda i: (pos_of(i), 0)),
                  pl.BlockSpec((DIFF_DK, tm), lambda i: (0, pos_of(i))),
                  pl.BlockSpec((DIFF_DK, tm), lambda i: (0, pos_of(i)))],
        out_specs=out_specs,
        out_shape=out_shape,
        scratch_shapes=[pltpu.VMEM((tm, d), BF16)],
        compiler_params=_params("parallel"),
        name="proj",
    )(h, mods, norm_w.reshape(1, d), w_p, wvt, wdqt, wdvt, wa, ba, cos_r, sin_r, cos_c, sin_c)


def _permute_w_in(w_in):
    d = w_in.shape[0]
    g0 = 2 * GLA_HEADS * GLA_DK + 2 * GLA_HEADS * GLA_DV
    aux = w_in[:, g0:g0 + 2 * GLA_RANK]
    rest = w_in[:, g0 + 2 * GLA_RANK:]
    pad = jnp.zeros((d, C_END - C_AUX - 2 * GLA_RANK), w_in.dtype)
    return jnp.concatenate([w_in[:, :g0], rest, aux, pad], axis=1).astype(BF16)


def _gate_weights(wa_f, ba_f, wa_b, ba_b):
    n = GLA_HEADS * GLA_DK
    wa = jnp.zeros((C_END - C_AUX, 2 * n), F32)
    wa = wa.at[:GLA_RANK, :n].set(wa_f).at[GLA_RANK:2 * GLA_RANK, n:].set(wa_b)
    return wa.astype(BF16), jnp.concatenate([ba_f, ba_b]).reshape(1, 2 * n)


def _rope_tables(seq, tile):
    t = jnp.arange(seq)
    row = (t // GRID_W).astype(F32)
    col = (t % GRID_W).astype(F32)
    half = DIFF_DK // 2
    inv = 1.0 / (ROPE_BASE ** (jnp.arange(0, half, 2, dtype=F32) / half))
    ang_r = row[:, None] * inv
    ang_c = col[:, None] * inv
    ang = jnp.concatenate([ang_r, ang_r, ang_c, ang_c], axis=-1)
    cos = jnp.concatenate([jnp.cos(ang), jnp.ones((tile, DIFF_DK), F32)], axis=0)
    sin = jnp.concatenate([jnp.sin(ang), jnp.zeros((tile, DIFF_DK), F32)], axis=0)
    rep = (1, LANES // DIFF_DK)
    return jnp.tile(cos, rep), jnp.tile(sin, rep), cos.T, sin.T


def _split3(x):
    hi = x.astype(BF16)
    r1 = x - hi.astype(F32)
    mid = r1.astype(BF16)
    lo = (r1 - mid.astype(F32)).astype(BF16)
    return hi, mid, lo


def _gla_body(*refs, reverse, batch):
    per_b = [refs[5 * b:5 * b + 5] for b in range(batch)]
    of_ref, nw_ref, o_ref, st_ref = refs[5 * batch:]
    blk = per_b[0][0].shape[0]
    n_chunks = blk // GLA_CHUNK
    nqk = GLA_HEADS * GLA_DK
    nv = GLA_HEADS * GLA_DV

    @pl.when(pl.program_id(0) == 0)
    def _():
        st_ref[...] = jnp.zeros_like(st_ref)

    ri = lax.broadcasted_iota(jnp.int32, (blk, blk), 0)
    ci = lax.broadcasted_iota(jnp.int32, (blk, blk), 1)
    same = (ri // GLA_CHUNK) == (ci // GLA_CHUNK)
    tri = jnp.where(same & ((ci >= ri) if reverse else (ci <= ri)), 1.0, 0.0).astype(BF16)
    bcum_of = []
    for qk_ref, v_ref, vt_ref, lg_ref, g_ref in per_b:
        lg = lg_ref[:, nqk:2 * nqk] if reverse else lg_ref[:, 0:nqk]
        hi, mid, lo = _split3(lg)
        bcum_of.append(_dot(tri, hi) + _dot(tri, mid) + _dot(tri, lo))

    c = GLA_CHUNK
    rk = lax.broadcasted_iota(jnp.int32, (GLA_HEADS * c, nqk), 0) // c
    ck = lax.broadcasted_iota(jnp.int32, (GLA_HEADS * c, nqk), 1) // GLA_DK
    mask_k = rk == ck
    rv = lax.broadcasted_iota(jnp.int32, (GLA_HEADS * c, nv), 0) // c
    cv = lax.broadcasted_iota(jnp.int32, (GLA_HEADS * c, nv), 1) // GLA_DV
    mask_v = rv == cv
    rs = lax.broadcasted_iota(jnp.int32, (nv, nqk), 0) // GLA_DV
    cs = lax.broadcasted_iota(jnp.int32, (nv, nqk), 1) // GLA_DK
    mask_s = rs == cs
    ai = lax.broadcasted_iota(jnp.int32, (c, GLA_HEADS * c), 0)
    aj = lax.broadcasted_iota(jnp.int32, (c, GLA_HEADS * c), 1) % c
    mask_a = (aj >= ai) if reverse else (aj <= ai)

    order = range(n_chunks - 1, -1, -1) if reverse else range(n_chunks)
    units = [(b, ch) for ch in order for b in range(batch)]
    q_in, a_raw, u_t, decay = {}, {}, {}, {}
    vts = [refs_b[2][...].astype(BF16) for refs_b in per_b]
    for b, ch in units:
        qk_ref = per_b[b][0]
        rows = slice(ch * c, (ch + 1) * c)
        k = qk_ref[rows, nqk:2 * nqk]
        bcum = bcum_of[b][rows, :]
        btot = bcum[0:1, :] if reverse else bcum[c - 1:c, :]
        q_in[b, ch] = (qk_ref[rows, 0:nqk] * jnp.exp(bcum)).astype(BF16)
        k_in = k * jnp.exp(-bcum)
        k_out = (k * jnp.exp(btot - bcum)).astype(BF16)
        k_bd = jnp.where(mask_k, jnp.concatenate([k_in] * GLA_HEADS, axis=0), 0.0).astype(BF16)
        a_raw[b, ch] = _dot_nt(q_in[b, ch], k_bd)
        k_pad = jnp.concatenate([jnp.zeros((n * c, nqk), BF16) for n in (ch,) if n] + [k_out]
                                + [jnp.zeros((n * c, nqk), BF16) for n in (n_chunks - 1 - ch,) if n], axis=0)
        u_t[b, ch] = jnp.where(mask_s, _dot(vts[b], k_pad), 0.0)
        decay[b, ch] = jnp.exp(btot)
    o = {}
    for b, ch in units:
        v = per_b[b][1][ch * c:(ch + 1) * c, :]
        v_bd = jnp.where(mask_v, jnp.concatenate([v] * GLA_HEADS, axis=0), 0.0).astype(BF16)
        o[b, ch] = _dot(jnp.where(mask_a, a_raw[b, ch], 0.0).astype(BF16), v_bd)
    st = [st_ref[b] for b in range(batch)]
    for b, ch in units:
        o[b, ch] = o[b, ch] + _dot_nt(q_in[b, ch], st[b].astype(BF16))
        st[b] = st[b] * decay[b, ch] + u_t[b, ch]
    for b in range(batch):
        st_ref[b] = st[b]
        o_blk = jnp.concatenate([o[b, ch] for ch in range(n_chunks)], axis=0)
        if reverse:
            o_blk = o_blk + of_ref[b]
            hi2, lo2, _ = _split3(o_blk * o_blk)
            hr = lax.broadcasted_iota(jnp.int32, (nv, nv), 0) // GLA_DV
            hc = lax.broadcasted_iota(jnp.int32, (nv, nv), 1) // GLA_DV
            seg = jnp.where(hr == hc, 1.0, 0.0).astype(BF16)
            ms = (_dot(hi2, seg) + _dot(lo2, seg)) * (1.0 / GLA_DV)
            o_blk = o_blk * lax.rsqrt(ms + RMS_EPS) * nw_ref[...] * _silu(per_b[b][4][...])
        o_ref[b] = o_blk


def _gla(gqk, gv, gvt, glg, gg, o_f, norm_w4, *, reverse, batch, seq, ctx):
    blk = GLA_BLOCK
    nc, nl = ctx // blk, seq // blk
    ctx_base = batch * seq // blk

    def step_blk(s):
        if reverse:
            return jnp.where(s < nc, nl + (nc - 1 - s), nl - 1 - (s - nc))
        return jnp.where(s < nc, nl + s, s - nc)

    def row_blk(b):
        return lambda s: jnp.where(s < nc, ctx_base + b * nc - nl, b * nl) + step_blk(s)

    specs, args = [], []
    for b in range(batch):
        spec = pl.BlockSpec((blk, 256), lambda s, f=row_blk(b): (f(s), 0))
        spec_t = pl.BlockSpec((256, blk), lambda s, f=row_blk(b): (0, f(s)))
        specs += [spec, spec, spec_t, spec, spec]
        args += [gqk, gv, gvt, glg, gg]
    seq_spec = pl.BlockSpec((batch, blk, 256), lambda s: (0, step_blk(s), 0))
    if o_f is None:
        o_f, of_spec = norm_w4, _resident((1, 256))
    else:
        of_spec = seq_spec
    return pl.pallas_call(
        functools.partial(_gla_body, reverse=reverse, batch=batch),
        grid=(nc + nl,),
        in_specs=specs + [of_spec, _resident((1, 256))],
        out_specs=seq_spec,
        out_shape=jax.ShapeDtypeStruct((batch, seq + ctx, 256), F32),
        scratch_shapes=[pltpu.VMEM((batch, GLA_HEADS * GLA_DV, GLA_HEADS * GLA_DK), F32)],
        compiler_params=_params("arbitrary"),
        name="gla_bwd" if reverse else "gla_fwd",
    )(*args, o_f, norm_w4)


def _na_col_tables(rpb):
    col = jnp.arange(GRID_W)
    cstart = jnp.clip(col - NA_COLS // 2, 0, GRID_W - NA_COLS)
    col_ok = (col[None, :] >= cstart[:, None]) & (col[None, :] < cstart[:, None] + NA_COLS)
    col_off = jnp.clip(col[None, :] - col[:, None] + (NA_COLS - 1), 0, 2 * NA_COLS - 2)
    t = jnp.where(col_ok[None, None], rpb[:, :, col_off] * LOG2E, NEG_INF)
    dead = jnp.full((NA_HEADS, 1, GRID_W, GRID_W), NEG_INF, F32)
    t = jnp.concatenate([dead, t.astype(F32), dead], axis=1)
    return jnp.concatenate([t[:, :-1], t[:, 1:]], axis=-1)


def _na_body(q_ref, k0, k1, k2, k3, v0, v1, v2, v3, kc_ref, vc_ref, tab_ref, o_ref, *, rows, key_blocks):
    k_blk = [k0[...], k1[...], k2[...], k3[...]]
    v_blk = [v0[...], v1[...], v2[...], v3[...]]
    kc = kc_ref[...]
    vc = vc_ref[...]

    j = pl.program_id(1)
    rows_per_blk = NA_KBLK // GRID_W
    r0 = j * NA_QROWS
    kr0 = jnp.clip(2 * j - 1, 0, key_blocks - 4) * rows_per_blk
    lane_lo = lax.broadcasted_iota(jnp.int32, (1, 2 * GRID_W), 1) < GRID_W
    lane = lax.broadcasted_iota(jnp.int32, (1, NA_HEADS * NA_DIM), 1) // NA_DIM
    half = NA_QROWS // 2
    n_pairs = 3 * rows_per_blk // 2
    gq = half * GRID_W
    for g in range(2):
        first = jnp.clip(r0 + g * half - NA_ROWS // 2, 0, rows - NA_ROWS)
        shift = jnp.clip((first - kr0) // rows_per_blk, 0, 1)
        kw = jnp.concatenate([jnp.where(shift == 0, k_blk[i], k_blk[i + 1]) for i in range(3)], axis=0)
        vw = jnp.concatenate([jnp.where(shift == 0, v_blk[i], v_blk[i + 1]) for i in range(3)], axis=0)
        kg0 = kr0 + shift * rows_per_blk
        entry, ok = [], []
        for a in range(half):
            r = r0 + g * half + a
            start = jnp.clip(r - NA_ROWS // 2, 0, rows - NA_ROWS)
            for bp in range(n_pairs):
                rk = kg0 + 2 * bp
                entry.append(jnp.clip(rk - r + NA_ROWS, 0, 2 * NA_ROWS - 1))
                in0 = ((rk >= start) & (rk < start + NA_ROWS)).astype(jnp.int32)
                in1 = ((rk + 1 >= start) & (rk + 1 < start + NA_ROWS)).astype(jnp.int32)
                ok.append(jnp.where(lane_lo, in0, in1) != 0)

        def bias_of(hh):
            rows_ = []
            for a in range(half):
                tiles = [jnp.where(ok[a * n_pairs + bp], tab_ref[hh, entry[a * n_pairs + bp]], NEG_INF)
                         for bp in range(n_pairs)]
                rows_.append(jnp.concatenate(tiles, axis=1))
            return jnp.concatenate(rows_, axis=0)

        q = q_ref[g * gq:(g + 1) * gq, :]
        out = jnp.zeros((gq, NA_HEADS * NA_DIM), F32)
        for hh in range(NA_HEADS):
            mh = lane == hh
            qh = jnp.where(mh, q, jnp.zeros_like(q))
            s_w = _dot_nt(qh, kw) + bias_of(hh)
            s_c = _dot_nt(qh, kc)
            m = jnp.maximum(jnp.max(s_w, axis=-1, keepdims=True), jnp.max(s_c, axis=-1, keepdims=True))
            p_w = jnp.exp2(s_w - m)
            p_c = jnp.exp2(s_c - m)
            l = jnp.sum(p_w, axis=-1, keepdims=True) + jnp.sum(p_c, axis=-1, keepdims=True)
            o = _dot(p_w.astype(BF16), vw) + _dot(p_c.astype(BF16), vc)
            out = out + jnp.where(mh, o / l, 0.0)
        o_ref[g * gq:(g + 1) * gq, :] = out


def _na(nq, nk, nv, tab, *, batch, seq, ctx):
    r = nq.shape[0]
    tq = NA_QROWS * GRID_W
    nj = seq // tq
    kb = seq // NA_KBLK
    assert kb >= 4 and seq % tq == 0 and seq % ctx == 0
    ctx_blk0 = batch * seq // ctx

    def kspec(i):
        return pl.BlockSpec((NA_KBLK, 256),
                            lambda b, j: (b * kb + jnp.clip(2 * j - 1, 0, kb - 4) + i, 0))

    cspec = pl.BlockSpec((ctx, 256), lambda b, j: (ctx_blk0 + b, 0))
    return pl.pallas_call(
        functools.partial(_na_body, rows=seq // GRID_W, key_blocks=kb),
        grid=(batch, nj),
        in_specs=[pl.BlockSpec((tq, 256), lambda b, j: (b * nj + j, 0))]
                 + [kspec(i) for i in range(4)] + [kspec(i) for i in range(4)]
                 + [cspec, cspec, _resident(tab.shape)],
        out_specs=pl.BlockSpec((tq, 256), lambda b, j: (b * nj + j, 0)),
        out_shape=jax.ShapeDtypeStruct((r, 256), F32),
        compiler_params=_params("parallel", "arbitrary"),
        name="na",
    )(nq, nk, nk, nk, nk, nv, nv, nv, nv, nk, nv, tab)


def _diff_lambda(lq_ref, lambda_init):
    lq = lq_ref[...]
    s1 = jnp.sum(lq[0:1, :] * lq[1:2, :], axis=-1, keepdims=True)
    s2 = jnp.sum(lq[2:3, :] * lq[3:4, :], axis=-1, keepdims=True)
    return jnp.exp(s1) - jnp.exp(s2) + lambda_init


def _stack_qt(qt):
    row = lax.broadcasted_iota(jnp.int32, (2 * DIFF_DK, 1), 0)
    zero = jnp.zeros_like(qt)
    return jnp.concatenate([jnp.where(row < DIFF_DK, qt, zero), jnp.where(row >= DIFF_DK, qt, zero)], axis=1)


def _diff_finish_t(acc, tq, lam, nw_col, lambda_init):
    o = acc[0:DIFF_DV, :] / acc[DIFF_DV:DIFF_DV + 1, :]
    od = o[:, :tq] - lam * o[:, tq:]
    ms = jnp.mean(od * od, axis=0, keepdims=True)
    y = od * lax.rsqrt(ms + RMS_EPS) * nw_col * (1.0 - lambda_init)
    y = jnp.concatenate([y, jnp.zeros_like(y)], axis=0)
    return jnp.transpose(y)[:, 0:DIFF_DV]


def _diff_body(qt_ref, kc_ref, vtc_ref, k_ref, vt_ref, lq_ref, nw_ref, o_ref, s_ref, *, tk, unroll, lambda_init):
    tq = qt_ref.shape[1]
    n_chunks = k_ref.shape[0] // tk
    ring = s_ref.shape[0]
    qs = _stack_qt(qt_ref[...])

    def chunk(j):
        return pl.ds(j * tk if isinstance(j, int) else pl.multiple_of(j * tk, tk), tk)

    def issue_scores(j, slot):
        s = _dot(k_ref[chunk(j), :], qs)
        s_ref[slot] = s
        return jnp.max(s, axis=0, keepdims=True)

    s_ctx = _dot(kc_ref[...], qs)
    m0 = jnp.max(s_ctx, axis=0, keepdims=True)
    cmax0 = tuple(issue_scores(min(a, n_chunks - 1), a % ring) for a in range(DIFF_AHEAD))
    acc0 = _dot(vtc_ref[...], jnp.exp2(s_ctx - m0).astype(BF16))

    def group(g, carry, tail=False):
        cmax, m, acc = carry
        for u in range(unroll):
            j = g * unroll + u
            if tail and j + DIFF_AHEAD >= n_chunks:
                c_new = cmax[0]
            else:
                c_new = issue_scores(j + DIFF_AHEAD, (u + DIFF_AHEAD) % ring)
            m_new = jnp.maximum(m, cmax[0])
            p = jnp.exp2(s_ref[u % ring] - m_new).astype(BF16)
            vt = vt_ref[:, chunk(j)]
            acc = jnp.exp2(m - m_new) * acc + _dot(vt, p)
            m, cmax = m_new, cmax[1:] + (c_new,)
        return cmax, m, acc

    n_groups = n_chunks // unroll
    carry = lax.fori_loop(0, n_groups - 1, group, (cmax0, m0, acc0))
    _, _, acc = group(n_groups - 1, carry, tail=True)
    o_ref[...] = _diff_finish_t(acc, tq, _diff_lambda(lq_ref, lambda_init), nw_ref[...], lambda_init)


def _diff(dqt, dk, dvt, lq, norm_w, *, batch, seq, ctx, lambda_init):
    h, r, _ = dk.shape
    tq = DIFF_TQ
    nq = seq // tq
    tk = min(DIFF_TK, seq)
    unroll = min(DIFF_UNROLL, seq // tk)
    assert (seq // tk) % unroll == 0 and unroll % DIFF_RING == 0 and DIFF_RING > DIFF_AHEAD
    ctx_blk0 = batch * seq // ctx
    return pl.pallas_call(
        functools.partial(_diff_body, tk=tk, unroll=unroll, lambda_init=lambda_init),
        grid=(batch, h, nq),
        in_specs=[pl.BlockSpec((None, 64, tq), lambda b, hh, i: (hh, 0, b * nq + i)),
                  pl.BlockSpec((None, ctx, 64), lambda b, hh, i: (hh, ctx_blk0 + b, 0)),
                  pl.BlockSpec((None, DIFF_VROWS, ctx), lambda b, hh, i: (hh, 0, ctx_blk0 + b)),
                  pl.BlockSpec((None, seq, 64), lambda b, hh, i: (hh, b, 0)),
                  pl.BlockSpec((None, DIFF_VROWS, seq), lambda b, hh, i: (hh, 0, b)),
                  _resident(lq.shape),
                  _resident((DIFF_DV, 1))],
        out_specs=pl.BlockSpec((None, tq, 64), lambda b, hh, i: (hh, b * nq + i, 0)),
        out_shape=jax.ShapeDtypeStruct((h, r, 64), F32),
        scratch_shapes=[pltpu.VMEM((DIFF_RING, tk, 2 * tq), F32)],
        compiler_params=_params("parallel", "parallel", "arbitrary"),
        name="diff",
    )(dqt, dk, dvt, dk, dvt, lq, norm_w.reshape(DIFF_DV, 1))


def _ctx_attn_body(nq_ref, nk_ref, nv_ref, dqt_ref, dk_ref, dvt_ref, lq_ref, nw_ref, na_in, df_in,
                   na_out, df_out, *, lambda_init):
    del na_in, df_in
    q = nq_ref[...]
    k = nk_ref[...]
    v = nv_ref[...]
    lane = lax.broadcasted_iota(jnp.int32, (1, NA_HEADS * NA_DIM), 1) // NA_DIM
    out = jnp.zeros(na_out.shape, F32)
    for hh in range(NA_HEADS):
        mh = lane == hh
        s = _dot_nt(jnp.where(mh, q, jnp.zeros_like(q)), k)
        p = jnp.exp2(s - jnp.max(s, axis=-1, keepdims=True))
        o = _dot(p.astype(BF16), v) / jnp.sum(p, axis=-1, keepdims=True)
        out = out + jnp.where(mh, o, 0.0)
    na_out[...] = out

    lam = _diff_lambda(lq_ref, lambda_init)
    tq = dqt_ref.shape[2]
    for hh in range(DIFF_HEADS):
        s = _dot(dk_ref[hh], _stack_qt(dqt_ref[hh]))
        p = jnp.exp2(s - jnp.max(s, axis=0, keepdims=True))
        acc = _dot(dvt_ref[hh], p.astype(BF16))
        df_out[hh] = _diff_finish_t(acc, tq, lam, nw_ref[...], lambda_init)


def _ctx_attn(nq, nk, nv, dqt, dk, dvt, lq, norm_w, na_o, df_o, *, batch, seq, ctx, lambda_init):
    blk0 = batch * seq // ctx
    s256 = pl.BlockSpec((ctx, 256), lambda b: (blk0 + b, 0))
    sh64 = pl.BlockSpec((DIFF_HEADS, ctx, 64), lambda b: (0, blk0 + b, 0))
    sq_t = pl.BlockSpec((DIFF_HEADS, 64, ctx), lambda b: (0, 0, blk0 + b))
    sv_t = pl.BlockSpec((DIFF_HEADS, DIFF_VROWS, ctx), lambda b: (0, 0, blk0 + b))
    return pl.pallas_call(
        functools.partial(_ctx_attn_body, lambda_init=lambda_init),
        grid=(batch,),
        in_specs=[s256, s256, s256, sq_t, sh64, sv_t, _resident(lq.shape), _resident((DIFF_DV, 1)),
                  pl.BlockSpec(memory_space=pl.ANY), pl.BlockSpec(memory_space=pl.ANY)],
        out_specs=[s256, sh64],
        out_shape=[jax.ShapeDtypeStruct(na_o.shape, F32), jax.ShapeDtypeStruct(df_o.shape, F32)],
        input_output_aliases={8: 0, 9: 1},
        compiler_params=_params("arbitrary"),
        name="ctx_attn",
    )(nq, nk, nv, dqt, dk, dvt, lq, norm_w.reshape(DIFF_DV, 1), na_o, df_o)


def _conv_body(u_ref, prev_ref, next_ref, dw_ref, dwb_ref, lng_ref, lnb_ref, pw_ref, pwb_ref,
               o_ref, pad_ref, sh_ref, *, tiles_per_seq, n_latent_tiles):
    t = u_ref.shape[0]
    i = pl.program_id(0)
    in_ctx = i >= n_latent_tiles
    first = in_ctx | (i % tiles_per_seq == 0)
    last = in_ctx | (i % tiles_per_seq == tiles_per_seq - 1)
    pad_ref[0:HALO, :] = jnp.where(first, 0.0, prev_ref[...])
    pad_ref[HALO:HALO + t, :] = u_ref[...]
    pad_ref[HALO + t:HALO + t + HALO, :] = jnp.where(last, 0.0, next_ref[...])
    base = HALO - CONV_K // 2
    first = {}
    for phase in range(8):
        taps = [k for k in range(CONV_K) if (base + k) % 8 == phase]
        first[phase] = base + taps[0]
        span = base + taps[-1] + t - first[phase]
        sh_ref[phase, 0:span, :] = pad_ref[first[phase]:first[phase] + span, :]
    sub = 64
    parts = []
    for r0 in range(0, t, sub):
        acc = jnp.zeros((sub, CONV_CH), F32)
        for k in range(CONV_K):
            phase = (base + k) % 8
            off = base + k - first[phase] + r0
            acc = acc + dw_ref[k:k + 1, :] * sh_ref[phase, off:off + sub, :]
        parts.append(acc)
    y = jnp.concatenate(parts, axis=0) + dwb_ref[...]
    mu = jnp.mean(y, axis=-1, keepdims=True)
    yc = y - mu
    var = jnp.mean(yc * yc, axis=-1, keepdims=True)
    y = _silu(yc * lax.rsqrt(var + LN_EPS) * lng_ref[...] + lnb_ref[...])
    o_ref[...] = _dot(y.astype(BF16), pw_ref[...]) + pwb_ref[...]


def _conv(cu, dw, dw_b, ln_g, ln_b, pw, pw_b, *, batch, seq, ctx):
    r = cu.shape[0]
    t = CONV_TILE
    assert ctx == t and seq % t == 0
    n_tiles = r // t
    hb = t // HALO
    vec = lambda a: a.reshape(1, CONV_CH)
    return pl.pallas_call(
        functools.partial(_conv_body, tiles_per_seq=seq // t, n_latent_tiles=batch * seq // t),
        grid=(n_tiles,),
        in_specs=[pl.BlockSpec((t, CONV_CH), lambda i: (i, 0)),
                  pl.BlockSpec((HALO, CONV_CH), lambda i: (jnp.maximum(i * hb - 1, 0), 0)),
                  pl.BlockSpec((HALO, CONV_CH), lambda i: (jnp.minimum((i + 1) * hb, n_tiles * hb - 1), 0)),
                  _resident((CONV_K, CONV_CH))] + [_resident((1, CONV_CH))] * 3
                 + [_resident((CONV_CH, CONV_CH)), _resident((1, CONV_CH))],
        out_specs=pl.BlockSpec((t, CONV_CH), lambda i: (i, 0)),
        out_shape=jax.ShapeDtypeStruct((r, CONV_CH), F32),
        scratch_shapes=[pltpu.VMEM((t + 2 * HALO, CONV_CH), F32),
                        pltpu.VMEM((8, t + 2 * HALO, CONV_CH), F32)],
        compiler_params=_params("parallel"),
        name="conv",
    )(cu, cu, cu, dw, vec(dw_b), vec(ln_g), vec(ln_b), pw.astype(BF16), vec(pw_b))


def kernel(x, c, ctx, c_ctx, ada_w, ada_b, norm_ffn1, ffn1_w13, ffn1_w2, norm_mix, w_in, gla_wa_f, gla_ba_f, gla_wa_b, gla_ba_b, gla_norm, na_rpb, diff_lq1, diff_lk1, diff_lq2, diff_lk2, diff_norm, conv_dw, conv_dw_b, conv_ln_g, conv_ln_b, conv_pw, conv_pw_b, w_out, norm_ffn2, ffn2_w13, ffn2_w2, final_norm):
    batch, seq, d = x.shape
    n_ctx = ctx.shape[1]
    depth = ada_w.shape[0]
    tm = ROW_TILE
    assert seq % tm == 0 and (batch * n_ctx) % tm == 0 and batch + 1 <= 8
    lat_tiles = batch * seq // tm
    all_tiles = lat_tiles + batch * n_ctx // tm
    tiles_per_batch = seq // tm

    def group_of(i):
        return jnp.minimum(i // tiles_per_batch, batch)

    def pos_of(i):
        return jnp.where(i < lat_tiles, i % tiles_per_batch, tiles_per_batch)

    c_rows = jnp.concatenate([c, c_ctx[None, :], jnp.zeros((8 - batch - 1, d), F32)], axis=0)
    mods_all = _ada(c_rows, ada_w, ada_b)[:, :batch + 1].reshape(depth, batch + 1, N_MOD, d)
    rope = _rope_tables(seq, tm)
    h = x.reshape(batch * seq, d)
    h_ctx = ctx.reshape(batch * n_ctx, d)

    for i in range(depth):
        last = i == depth - 1
        lambda_init = 0.8 - 0.6 * math.exp(-0.3 * i)
        mods = mods_all[i]
        tok = dict(n_tiles=all_tiles, group_of=group_of)
        geo = dict(batch=batch, seq=seq, ctx=n_ctx)

        h = _ffn(h, mods, norm_ffn1[i], ffn1_w13[i], ffn1_w2[i], final_norm, k0=0, final=False,
                 h_ctx=h_ctx if i == 0 else None, **tok)

        wa, ba = _gate_weights(gla_wa_f[i], gla_ba_f[i], gla_wa_b[i], gla_ba_b[i])
        (gqk, gv, gvt, gg, glg, nq, nk, nv, dq, dk, dv, cu) = _proj(
            h, mods, norm_mix[i], _permute_w_in(w_in[i]), wa, ba, rope, pos_of=pos_of, **tok)

        gnorm = jnp.tile(gla_norm[i], GLA_HEADS).reshape(1, GLA_HEADS * GLA_DV)
        o_f = _gla(gqk, gv, gvt, glg, gg, None, gnorm, reverse=False, **geo)
        gx = _gla(gqk, gv, gvt, glg, gg, o_f, gnorm, reverse=True, **geo)

        nx = _na(nq, nk, nv, _na_col_tables(na_rpb[i]), **geo)

        lq = jnp.stack([diff_lq1[i], diff_lk1[i], diff_lq2[i], diff_lk2[i]])
        dx = _diff(dq, dk, dv, lq, diff_norm[i], lambda_init=lambda_init, **geo)
        if not last:
            nx, dx = _ctx_attn(nq, nk, nv, dq, dk, dv, lq, diff_norm[i], nx, dx,
                               lambda_init=lambda_init, **geo)

        cx = _conv(cu, conv_dw[i], conv_dw_b[i], conv_ln_g[i], conv_ln_b[i], conv_pw[i], conv_pw_b[i], **geo)

        if last:
            tok = dict(n_tiles=lat_tiles, group_of=group_of)
        h = _ffn(h, mods, norm_ffn2[i], ffn2_w13[i], ffn2_w2[i], final_norm, k0=6, final=last,
                 mixers=(gx, nx, dx, cx, w_out[i], (batch, seq, n_ctx)), **tok)

    return h.reshape(batch, seq, d)
```

```python
import functools
import math

import jax
import jax.numpy as jnp
from jax import lax
from jax.experimental import pallas as pl
from jax.experimental.pallas import tpu as pltpu

F32 = jnp.float32
BF16 = jnp.bfloat16

GRID_W = 64
N_MOD = 9
RMS_EPS = 1e-6
LN_EPS = 1e-5
NEG_INF = -1e30
GLA_HEADS, GLA_DK, GLA_DV, GLA_RANK, GLA_TAU, GLA_CHUNK = 4, 32, 64, 16, 16.0, 64
NA_HEADS, NA_DIM, NA_ROWS, NA_COLS = 4, 64, 8, 16
DIFF_HEADS, DIFF_DK, DIFF_DV = 4, 32, 64
DIFF_VROWS = 128
CONV_CH, CONV_K = 256, 31
ROPE_BASE = 10000.0
LOG2E = 1.4426950408889634

LANES = 128
VMEM_LIMIT = 56 * 1024 * 1024

ROW_TILE = 512
FF_CHUNK = 256
GLA_BLOCK = 256
NA_QROWS = 8
NA_KBLK = 256
CONV_TILE = 256
DIFF_TQ = 256
DIFF_QTILES = 8
DIFF_TK = 256
DIFF_UNROLL = 16
DIFF_AHEAD = 3
DIFF_RING = 4
HALO = 16

C_GLA, C_NA, C_DIFF, C_CONV, C_AUX, C_END = 0, 768, 1536, 2304, 2816, 2944


def _dot(a, b):
    return jnp.dot(a, b, preferred_element_type=F32)


def _dot_nt(a, b):
    return lax.dot_general(a, b, (((1,), (1,)), ((), ())), preferred_element_type=F32)


def _params(*sem):
    return pltpu.CompilerParams(dimension_semantics=sem, vmem_limit_bytes=VMEM_LIMIT)


def _resident(shape):
    nd = len(shape)
    return pl.BlockSpec(shape, lambda *_: (0,) * nd, pipeline_mode=pl.Buffered(1))


def _silu(x):
    return x * jax.nn.sigmoid(x)


def _rms(x, w):
    return x * lax.rsqrt(jnp.mean(x * x, axis=-1, keepdims=True) + RMS_EPS) * w


def _ada_body(c_ref, w_ref, b_ref, o_ref):
    s = _silu(c_ref[...])
    o_ref[...] = jnp.dot(s, w_ref[...], precision=lax.Precision.HIGHEST,
                         preferred_element_type=F32) + b_ref[...]


def _ada(c_rows, ada_w, ada_b):
    depth, d, _ = ada_w.shape
    return pl.pallas_call(
        _ada_body,
        grid=(depth, N_MOD),
        in_specs=[pl.BlockSpec((8, d), lambda l, n: (0, 0)),
                  pl.BlockSpec((None, d, d), lambda l, n: (l, 0, n)),
                  pl.BlockSpec((None, 1, d), lambda l, n: (l, 0, n))],
        out_specs=pl.BlockSpec((None, 8, d), lambda l, n: (l, 0, n)),
        out_shape=jax.ShapeDtypeStruct((depth, 8, N_MOD * d), F32),
        compiler_params=_params("arbitrary", "arbitrary"),
        name="ada",
    )(c_rows, ada_w, ada_b.reshape(depth, 1, N_MOD * d))


def _ffn_body(*refs, k0, final, source, n_first):
    h_ref = refs[0]
    n_extra = {"plain": 0, "two_arrays": 1, "mixers": 6}[source]
    extra = refs[1:1 + n_extra]
    mod_ref, nw_ref, w13_ref, w2_ref, fw_ref, o_ref, xb_ref, g_ref, h13_ref = refs[1 + n_extra:]
    ff = w2_ref.shape[0]
    tf = h13_ref.shape[2] // 2
    n_chunks = ff // tf
    x = h_ref[...]
    if source == "two_arrays":
        x = jnp.where(pl.program_id(0) < n_first, x, extra[0][...])
    elif source == "mixers":
        gx_ref, gxc_ref, nx_ref, dx_ref, cx_ref, wo_ref = extra
        gx = jnp.where(pl.program_id(0) < n_first, gx_ref[...], gxc_ref[...].reshape(gx_ref.shape))
        mix = jnp.concatenate([gx, nx_ref[...]] + [dx_ref[hh] for hh in range(DIFF_HEADS)]
                              + [cx_ref[...]], axis=-1).astype(BF16)
        x = x + mod_ref[5:6, :] * _dot(mix, wo_ref[...])
    xm = _rms(x, nw_ref[...]) * (1.0 + mod_ref[k0 + 1:k0 + 2, :]) + mod_ref[k0:k0 + 1, :]
    xb_ref[...] = xm.astype(BF16)

    def cols(c):
        return c * tf if isinstance(c, int) else pl.multiple_of(c * tf, tf)

    def up(c, slot):
        h13_ref[slot, :, :tf] = _dot(xb_ref[...], w13_ref[:, pl.ds(cols(c), tf)])
        h13_ref[slot, :, tf:] = _dot(xb_ref[...], w13_ref[:, pl.ds(ff + cols(c), tf)])

    def gate(c, slot):
        a = h13_ref[slot, :, :tf]
        u = h13_ref[slot, :, tf:]
        g_ref[:, pl.ds(cols(c), tf)] = (_silu(a) * u).astype(BF16)

    def pair(t, carry):
        up(2 * t + 1, 1)
        gate(2 * t, 0)
        up(2 * t + 2, 0)
        gate(2 * t + 1, 1)
        return carry

    up(0, 0)
    lax.fori_loop(0, (n_chunks - 1) // 2, pair, 0)
    gate(n_chunks - 1, 0)
    out = x + (0.5 * mod_ref[k0 + 2:k0 + 3, :]) * _dot(g_ref[...], w2_ref[...])
    if final:
        out = _rms(out, fw_ref[...])
    o_ref[...] = out


def _ffn(h, mods, norm_w, w13, w2, final_w, *, k0, n_tiles, group_of, final, h_ctx=None, mixers=None):
    d = h.shape[1]
    tm = ROW_TILE
    ff = w2.shape[0]
    n_chunks = ff // FF_CHUNK
    assert ff % FF_CHUNK == 0 and n_chunks % 2 == 1
    row = lambda i: (i, 0)
    source, n_first, extra, extra_specs, h_spec = "plain", None, [], [], pl.BlockSpec((tm, d), row)
    if h_ctx is not None:
        source, n_first, extra = "two_arrays", h.shape[0] // tm, [h_ctx]
        h_spec = pl.BlockSpec((tm, d), lambda i: (jnp.minimum(i, n_first - 1), 0))
        extra_specs = [pl.BlockSpec((tm, d), lambda i: (jnp.maximum(i - n_first, 0), 0))]
    elif mixers is not None:
        gx, nx, dx, cx, w_out, (batch, seq, n_ctx) = mixers
        assert batch * n_ctx == tm and seq % tm == 0 and seq % n_ctx == 0
        source, n_first, extra = "mixers", batch * seq // tm, [gx, gx, nx, dx, cx, w_out.astype(BF16)]
        per_seq = seq // tm
        extra_specs = [pl.BlockSpec((None, tm, 256),
                                    lambda i: (jnp.minimum(i // per_seq, batch - 1), i % per_seq, 0)),
                       pl.BlockSpec((batch, n_ctx, 256), lambda i: (0, seq // n_ctx, 0)),
                       pl.BlockSpec((tm, 256), row),
                       pl.BlockSpec((DIFF_HEADS, tm, 64), lambda i: (0, i, 0)),
                       pl.BlockSpec((tm, 256), row), _resident(w_out.shape)]
    return pl.pallas_call(
        functools.partial(_ffn_body, k0=k0, final=final, source=source, n_first=n_first),
        grid=(n_tiles,),
        in_specs=[h_spec] + extra_specs + [
            pl.BlockSpec((None, N_MOD, d), lambda i: (group_of(i), 0, 0)),
            _resident((1, d)),
            _resident(w13.shape),
            _resident(w2.shape),
            _resident((1, d))],
        out_specs=pl.BlockSpec((tm, d), row),
        out_shape=jax.ShapeDtypeStruct((n_tiles * tm, d), F32),
        scratch_shapes=[pltpu.VMEM((tm, d), BF16), pltpu.VMEM((tm, ff), BF16),
                        pltpu.VMEM((2, tm, 2 * FF_CHUNK), F32)],
        compiler_params=_params("parallel"),
        name="ffn",
    )(h, *extra, mods, norm_w.reshape(1, d), w13.astype(BF16), w2.astype(BF16), final_w.reshape(1, d))


def _log_sigmoid(x):
    return jnp.minimum(x, 0.0) - jnp.log1p(jnp.exp(-jnp.abs(x)))


def _rope_rotate(x):
    n = x.shape[-1]
    lane = lax.broadcasted_iota(jnp.int32, x.shape, 1)
    up = pltpu.roll(x, n - 8, 1)
    dn = pltpu.roll(x, 8, 1)
    return jnp.where((lane & 15) < 8, -up, dn)


def _rope_rotate_rows(x):
    n = x.shape[0]
    row = lax.broadcasted_iota(jnp.int32, x.shape, 0)
    up = pltpu.roll(x, n - 8, 0)
    dn = pltpu.roll(x, 8, 0)
    return jnp.where((row & 15) < 8, -up, dn)


def _proj_body(h_ref, mod_ref, nw_ref, w_ref, wvt_ref, wdqt_ref, wdvt_ref, wa_ref, ba_ref,
               cos_ref, sin_ref, cost_ref, sint_ref,
               gqk_ref, gv_ref, gvt_ref, gg_ref, glg_ref, nq_ref, nk_ref, nv_ref,
               dqt_ref, dk_ref, dvt_ref, cu_ref, xb_ref):
    x = h_ref[...]
    tm = x.shape[0]
    xm = _rms(x, nw_ref[...]) * (1.0 + mod_ref[4:5, :]) + mod_ref[3:4, :]
    xb_ref[...] = xm.astype(BF16)

    z = _dot(xb_ref[...], w_ref[:, C_GLA:C_GLA + 256])
    lane = lax.broadcasted_iota(jnp.int32, (1, 256), 1)
    gqk_ref[...] = z * jnp.where(lane < 128, GLA_DK ** -0.5, 1.0)
    gv_ref[...] = _dot(xb_ref[...], w_ref[:, C_GLA + 256:C_GLA + 512])
    gvt_ref[...] = _dot_nt(wvt_ref[...], xb_ref[...])
    gg_ref[...] = _dot(xb_ref[...], w_ref[:, C_GLA + 512:C_GLA + 768])
    aux = _dot(xb_ref[...], w_ref[:, C_AUX:C_END])
    pre = _dot(aux.astype(BF16), wa_ref[...]) + ba_ref[...]
    glg_ref[...] = _log_sigmoid(pre) * (1.0 / GLA_TAU)

    nq_ref[...] = (_dot(xb_ref[...], w_ref[:, C_NA:C_NA + 256]) * (NA_DIM ** -0.5 * LOG2E)).astype(BF16)
    nk_ref[...] = _dot(xb_ref[...], w_ref[:, C_NA + 256:C_NA + 512]).astype(BF16)
    nv_ref[...] = _dot(xb_ref[...], w_ref[:, C_NA + 512:C_NA + 768]).astype(BF16)

    cos = cos_ref[...]
    sin = sin_ref[...]
    cos2 = jnp.concatenate([cos, cos], axis=1)
    sin2 = jnp.concatenate([sin, sin], axis=1)
    zk = _dot(xb_ref[...], w_ref[:, C_DIFF + 256:C_DIFF + 512])
    zk = zk * cos2 + _rope_rotate(zk) * sin2
    n_rep = 2 * DIFF_HEADS
    cos_t = jnp.concatenate([cost_ref[...]] * n_rep, axis=0)
    sin_t = jnp.concatenate([sint_ref[...]] * n_rep, axis=0)
    zqt = _dot_nt(wdqt_ref[...], xb_ref[...])
    zqt = (zqt * cos_t + _rope_rotate_rows(zqt) * sin_t) * (DIFF_DK ** -0.5 * LOG2E)
    zvt = _dot_nt(wdvt_ref[...], xb_ref[...])
    pad_rows = DIFF_VROWS - DIFF_DV
    one_row = jnp.where(lax.broadcasted_iota(jnp.int32, (pad_rows, tm), 0) == 0, 1.0, 0.0).astype(BF16)
    for hh in range(DIFF_HEADS):
        sl = slice(64 * hh, 64 * hh + 64)
        dk_ref[hh] = zk[:, sl].astype(BF16)
        dqt_ref[hh] = zqt[sl, :].astype(BF16)
        dvt_ref[hh, 0:64, :] = zvt[sl, :].astype(BF16)
        dvt_ref[hh, DIFF_DV:DIFF_VROWS, :] = one_row

    za = _dot(xb_ref[...], w_ref[:, C_CONV:C_CONV + 256])
    zg = _dot(xb_ref[...], w_ref[:, C_CONV + 256:C_CONV + 512])
    cu_ref[...] = za * jax.nn.sigmoid(zg)


def _proj(h, mods, norm_w, w_p, wa, ba, rope, *, n_tiles, group_of, pos_of):
    r, d = h.shape
    tm = ROW_TILE
    row = lambda i: (i, 0)
    hrow = lambda i: (0, i, 0)
    hcol = lambda i: (0, 0, i)
    f32_256 = jax.ShapeDtypeStruct((r, 256), F32)
    bf_256 = jax.ShapeDtypeStruct((r, 256), BF16)
    wvt = w_p[:, C_GLA + 256:C_GLA + 512].T
    wdqt = w_p[:, C_DIFF:C_DIFF + 256].T
    wdvt = w_p[:, C_DIFF + 512:C_DIFF + 768].T
    cos_r, sin_r, cos_c, sin_c = rope
    out_shape = [f32_256, f32_256, jax.ShapeDtypeStruct((256, r), F32), f32_256, f32_256,
                 bf_256, bf_256, bf_256,
                 jax.ShapeDtypeStruct((DIFF_HEADS, 64, r), BF16),
                 jax.ShapeDtypeStruct((DIFF_HEADS, r, 64), BF16),
                 jax.ShapeDtypeStruct((DIFF_HEADS, DIFF_VROWS, r), BF16),
                 f32_256]
    out_specs = [pl.BlockSpec((tm, 256), row)] * 2 + [pl.BlockSpec((256, tm), lambda i: (0, i))] + [
        pl.BlockSpec((tm, 256), row)] * 5 + [
        pl.BlockSpec((DIFF_HEADS, 64, tm), hcol),
        pl.BlockSpec((DIFF_HEADS, tm, 64), hrow),
        pl.BlockSpec((DIFF_HEADS, DIFF_VROWS, tm), hcol),
        pl.BlockSpec((tm, 256), row)]
    return pl.pallas_call(
        _proj_body,
        grid=(n_tiles,),
        in_specs=[pl.BlockSpec((tm, d), row),
                  pl.BlockSpec((None, N_MOD, d), lambda i: (group_of(i), 0, 0)),
                  _resident((1, d)),
                  _resident(w_p.shape),
                  _resident(wvt.shape),
                  _resident(wdqt.shape),
                  _resident(wdvt.shape),
                  _resident(wa.shape),
                  _resident(ba.shape),
                  pl.BlockSpec((tm, LANES), lambda i: (pos_of(i), 0)),
                  pl.BlockSpec((tm, LANES), lambda i: (pos_of(i), 0)),
                  pl.BlockSpec((DIFF_DK, tm), lambda i: (0, pos_of(i))),
                  pl.BlockSpec((DIFF_DK, tm), lambda i: (0, pos_of(i)))],
        out_specs=out_specs,
        out_shape=out_shape,
        scratch_shapes=[pltpu.VMEM((tm, d), BF16)],
        compiler_params=_params("parallel"),
        name="proj",
    )(h, mods, norm_w.reshape(1, d), w_p, wvt, wdqt, wdvt, wa, ba, cos_r, sin_r, cos_c, sin_c)


def _permute_w_in(w_in):
    d = w_in.shape[0]
    g0 = 2 * GLA_HEADS * GLA_DK + 2 * GLA_HEADS * GLA_DV
    aux = w_in[:, g0:g0 + 2 * GLA_RANK]
    rest = w_in[:, g0 + 2 * GLA_RANK:]
    pad = jnp.zeros((d, C_END - C_AUX - 2 * GLA_RANK), w_in.dtype)
    return jnp.concatenate([w_in[:, :g0], rest, aux, pad], axis=1).astype(BF16)


def _gate_weights(wa_f, ba_f, wa_b, ba_b):
    n = GLA_HEADS * GLA_DK
    wa = jnp.zeros((C_END - C_AUX, 2 * n), F32)
    wa = wa.at[:GLA_RANK, :n].set(wa_f).at[GLA_RANK:2 * GLA_RANK, n:].set(wa_b)
    return wa.astype(BF16), jnp.concatenate([ba_f, ba_b]).reshape(1, 2 * n)


def _rope_tables(seq, tile):
    t = jnp.arange(seq)
    row = (t // GRID_W).astype(F32)
    col = (t % GRID_W).astype(F32)
    half = DIFF_DK // 2
    inv = 1.0 / (ROPE_BASE ** (jnp.arange(0, half, 2, dtype=F32) / half))
    ang_r = row[:, None] * inv
    ang_c = col[:, None] * inv
    ang = jnp.concatenate([ang_r, ang_r, ang_c, ang_c], axis=-1)
    cos = jnp.concatenate([jnp.cos(ang), jnp.ones((tile, DIFF_DK), F32)], axis=0)
    sin = jnp.concatenate([jnp.sin(ang), jnp.zeros((tile, DIFF_DK), F32)], axis=0)
    rep = (1, LANES // DIFF_DK)
    return jnp.tile(cos, rep), jnp.tile(sin, rep), cos.T, sin.T


def _split3(x):
    hi = x.astype(BF16)
    r1 = x - hi.astype(F32)
    mid = r1.astype(BF16)
    lo = (r1 - mid.astype(F32)).astype(BF16)
    return hi, mid, lo


def _gla_body(*refs, reverse, batch):
    per_b = [refs[5 * b:5 * b + 5] for b in range(batch)]
    of_ref, nw_ref, o_ref, st_ref = refs[5 * batch:]
    blk = per_b[0][0].shape[0]
    n_chunks = blk // GLA_CHUNK
    nqk = GLA_HEADS * GLA_DK
    nv = GLA_HEADS * GLA_DV

    @pl.when(pl.program_id(0) == 0)
    def _():
        st_ref[...] = jnp.zeros_like(st_ref)

    ri = lax.broadcasted_iota(jnp.int32, (blk, blk), 0)
    ci = lax.broadcasted_iota(jnp.int32, (blk, blk), 1)
    same = (ri // GLA_CHUNK) == (ci // GLA_CHUNK)
    tri = jnp.where(same & ((ci >= ri) if reverse else (ci <= ri)), 1.0, 0.0).astype(BF16)
    bcum_of = []
    for qk_ref, v_ref, vt_ref, lg_ref, g_ref in per_b:
        lg = lg_ref[:, nqk:2 * nqk] if reverse else lg_ref[:, 0:nqk]
        hi, mid, lo = _split3(lg)
        bcum_of.append(_dot(tri, hi) + _dot(tri, mid) + _dot(tri, lo))

    c = GLA_CHUNK
    rk = lax.broadcasted_iota(jnp.int32, (GLA_HEADS * c, nqk), 0) // c
    ck = lax.broadcasted_iota(jnp.int32, (GLA_HEADS * c, nqk), 1) // GLA_DK
    mask_k = rk == ck
    rv = lax.broadcasted_iota(jnp.int32, (GLA_HEADS * c, nv), 0) // c
    cv = lax.broadcasted_iota(jnp.int32, (GLA_HEADS * c, nv), 1) // GLA_DV
    mask_v = rv == cv
    rs = lax.broadcasted_iota(jnp.int32, (nv, nqk), 0) // GLA_DV
    cs = lax.broadcasted_iota(jnp.int32, (nv, nqk), 1) // GLA_DK
    mask_s = rs == cs
    ai = lax.broadcasted_iota(jnp.int32, (c, GLA_HEADS * c), 0)
    aj = lax.broadcasted_iota(jnp.int32, (c, GLA_HEADS * c), 1) % c
    mask_a = (aj >= ai) if reverse else (aj <= ai)

    order = range(n_chunks - 1, -1, -1) if reverse else range(n_chunks)
    units = [(b, ch) for ch in order for b in range(batch)]
    q_in, a_raw, u_t, decay = {}, {}, {}, {}
    vts = [refs_b[2][...].astype(BF16) for refs_b in per_b]
    for b, ch in units:
        qk_ref = per_b[b][0]
        rows = slice(ch * c, (ch + 1) * c)
        k = qk_ref[rows, nqk:2 * nqk]
        bcum = bcum_of[b][rows, :]
        btot = bcum[0:1, :] if reverse else bcum[c - 1:c, :]
        q_in[b, ch] = (qk_ref[rows, 0:nqk] * jnp.exp(bcum)).astype(BF16)
        k_in = k * jnp.exp(-bcum)
        k_out = (k * jnp.exp(btot - bcum)).astype(BF16)
        k_bd = jnp.where(mask_k, jnp.concatenate([k_in] * GLA_HEADS, axis=0), 0.0).astype(BF16)
        a_raw[b, ch] = _dot_nt(q_in[b, ch], k_bd)
        k_pad = jnp.concatenate([jnp.zeros((n * c, nqk), BF16) for n in (ch,) if n] + [k_out]
                                + [jnp.zeros((n * c, nqk), BF16) for n in (n_chunks - 1 - ch,) if n], axis=0)
        u_t[b, ch] = jnp.where(mask_s, _dot(vts[b], k_pad), 0.0)
        decay[b, ch] = jnp.exp(btot)
    o = {}
    for b, ch in units:
        v = per_b[b][1][ch * c:(ch + 1) * c, :]
        v_bd = jnp.where(mask_v, jnp.concatenate([v] * GLA_HEADS, axis=0), 0.0).astype(BF16)
        o[b, ch] = _dot(jnp.where(mask_a, a_raw[b, ch], 0.0).astype(BF16), v_bd)
    st = [st_ref[b] for b in range(batch)]
    for b, ch in units:
        o[b, ch] = o[b, ch] + _dot_nt(q_in[b, ch], st[b].astype(BF16))
        st[b] = st[b] * decay[b, ch] + u_t[b, ch]
    for b in range(batch):
        st_ref[b] = st[b]
        o_blk = jnp.concatenate([o[b, ch] for ch in range(n_chunks)], axis=0)
        if reverse:
            o_blk = o_blk + of_ref[b]
            hi2, lo2, _ = _split3(o_blk * o_blk)
            hr = lax.broadcasted_iota(jnp.int32, (nv, nv), 0) // GLA_DV
            hc = lax.broadcasted_iota(jnp.int32, (nv, nv), 1) // GLA_DV
            seg = jnp.where(hr == hc, 1.0, 0.0).astype(BF16)
            ms = (_dot(hi2, seg) + _dot(lo2, seg)) * (1.0 / GLA_DV)
            o_blk = o_blk * lax.rsqrt(ms + RMS_EPS) * nw_ref[...] * _silu(per_b[b][4][...])
        o_ref[b] = o_blk


def _gla(gqk, gv, gvt, glg, gg, o_f, norm_w4, *, reverse, batch, seq, ctx):
    blk = GLA_BLOCK
    nc, nl = ctx // blk, seq // blk
    ctx_base = batch * seq // blk

    def step_blk(s):
        if reverse:
            return jnp.where(s < nc, nl + (nc - 1 - s), nl - 1 - (s - nc))
        return jnp.where(s < nc, nl + s, s - nc)

    def row_blk(b):
        return lambda s: jnp.where(s < nc, ctx_base + b * nc - nl, b * nl) + step_blk(s)

    specs, args = [], []
    for b in range(batch):
        spec = pl.BlockSpec((blk, 256), lambda s, f=row_blk(b): (f(s), 0))
        spec_t = pl.BlockSpec((256, blk), lambda s, f=row_blk(b): (0, f(s)))
        specs += [spec, spec, spec_t, spec, spec]
        args += [gqk, gv, gvt, glg, gg]
    seq_spec = pl.BlockSpec((batch, blk, 256), lambda s: (0, step_blk(s), 0))
    if o_f is None:
        o_f, of_spec = norm_w4, _resident((1, 256))
    else:
        of_spec = seq_spec
    return pl.pallas_call(
        functools.partial(_gla_body, reverse=reverse, batch=batch),
        grid=(nc + nl,),
        in_specs=specs + [of_spec, _resident((1, 256))],
        out_specs=seq_spec,
        out_shape=jax.ShapeDtypeStruct((batch, seq + ctx, 256), F32),
        scratch_shapes=[pltpu.VMEM((batch, GLA_HEADS * GLA_DV, GLA_HEADS * GLA_DK), F32)],
        compiler_params=_params("arbitrary"),
        name="gla_bwd" if reverse else "gla_fwd",
    )(*args, o_f, norm_w4)


def _na_col_tables(rpb):
    col = jnp.arange(GRID_W)
    cstart = jnp.clip(col - NA_COLS // 2, 0, GRID_W - NA_COLS)
    col_ok = (col[None, :] >= cstart[:, None]) & (col[None, :] < cstart[:, None] + NA_COLS)
    col_off = jnp.clip(col[None, :] - col[:, None] + (NA_COLS - 1), 0, 2 * NA_COLS - 2)
    t = jnp.where(col_ok[None, None], rpb[:, :, col_off] * LOG2E, NEG_INF)
    dead = jnp.full((NA_HEADS, 1, GRID_W, GRID_W), NEG_INF, F32)
    t = jnp.concatenate([dead, t.astype(F32), dead], axis=1)
    return jnp.concatenate([t[:, :-1], t[:, 1:]], axis=-1)


def _na_body(q_ref, k0, k1, k2, k3, v0, v1, v2, v3, kc_ref, vc_ref, tab_ref, o_ref, *, rows, key_blocks):
    k_blk = [k0[...], k1[...], k2[...], k3[...]]
    v_blk = [v0[...], v1[...], v2[...], v3[...]]
    kc = kc_ref[...]
    vc = vc_ref[...]

    j = pl.program_id(1)
    rows_per_blk = NA_KBLK // GRID_W
    r0 = j * NA_QROWS
    kr0 = jnp.clip(2 * j - 1, 0, key_blocks - 4) * rows_per_blk
    lane_lo = lax.broadcasted_iota(jnp.int32, (1, 2 * GRID_W), 1) < GRID_W
    lane = lax.broadcasted_iota(jnp.int32, (1, NA_HEADS * NA_DIM), 1) // NA_DIM
    half = NA_QROWS // 2
    n_pairs = 3 * rows_per_blk // 2
    gq = half * GRID_W
    for g in range(2):
        first = jnp.clip(r0 + g * half - NA_ROWS // 2, 0, rows - NA_ROWS)
        shift = jnp.clip((first - kr0) // rows_per_blk, 0, 1)
        kw = jnp.concatenate([jnp.where(shift == 0, k_blk[i], k_blk[i + 1]) for i in range(3)], axis=0)
        vw = jnp.concatenate([jnp.where(shift == 0, v_blk[i], v_blk[i + 1]) for i in range(3)], axis=0)
        kg0 = kr0 + shift * rows_per_blk
        entry, ok = [], []
        for a in range(half):
            r = r0 + g * half + a
            start = jnp.clip(r - NA_ROWS // 2, 0, rows - NA_ROWS)
            for bp in range(n_pairs):
                rk = kg0 + 2 * bp
                entry.append(jnp.clip(rk - r + NA_ROWS, 0, 2 * NA_ROWS - 1))
                in0 = ((rk >= start) & (rk < start + NA_ROWS)).astype(jnp.int32)
                in1 = ((rk + 1 >= start) & (rk + 1 < start + NA_ROWS)).astype(jnp.int32)
                ok.append(jnp.where(lane_lo, in0, in1) != 0)

        def bias_of(hh):
            rows_ = []
            for a in range(half):
                tiles = [jnp.where(ok[a * n_pairs + bp], tab_ref[hh, entry[a * n_pairs + bp]], NEG_INF)
                         for bp in range(n_pairs)]
                rows_.append(jnp.concatenate(tiles, axis=1))
            return jnp.concatenate(rows_, axis=0)

        q = q_ref[g * gq:(g + 1) * gq, :]
        out = jnp.zeros((gq, NA_HEADS * NA_DIM), F32)
        for hh in range(NA_HEADS):
            mh = lane == hh
            qh = jnp.where(mh, q, jnp.zeros_like(q))
            s_w = _dot_nt(qh, kw) + bias_of(hh)
            s_c = _dot_nt(qh, kc)
            m = jnp.maximum(jnp.max(s_w, axis=-1, keepdims=True), jnp.max(s_c, axis=-1, keepdims=True))
            p_w = jnp.exp2(s_w - m)
            p_c = jnp.exp2(s_c - m)
            l = jnp.sum(p_w, axis=-1, keepdims=True) + jnp.sum(p_c, axis=-1, keepdims=True)
            o = _dot(p_w.astype(BF16), vw) + _dot(p_c.astype(BF16), vc)
            out = out + jnp.where(mh, o / l, 0.0)
        o_ref[g * gq:(g + 1) * gq, :] = out


def _na(nq, nk, nv, tab, *, batch, seq, ctx):
    r = nq.shape[0]
    tq = NA_QROWS * GRID_W
    nj = seq // tq
    kb = seq // NA_KBLK
    assert kb >= 4 and seq % tq == 0 and seq % ctx == 0
    ctx_blk0 = batch * seq // ctx

    def kspec(i):
        return pl.BlockSpec((NA_KBLK, 256),
                            lambda b, j: (b * kb + jnp.clip(2 * j - 1, 0, kb - 4) + i, 0))

    cspec = pl.BlockSpec((ctx, 256), lambda b, j: (ctx_blk0 + b, 0))
    return pl.pallas_call(
        functools.partial(_na_body, rows=seq // GRID_W, key_blocks=kb),
        grid=(batch, nj),
        in_specs=[pl.BlockSpec((tq, 256), lambda b, j: (b * nj + j, 0))]
                 + [kspec(i) for i in range(4)] + [kspec(i) for i in range(4)]
                 + [cspec, cspec, _resident(tab.shape)],
        out_specs=pl.BlockSpec((tq, 256), lambda b, j: (b * nj + j, 0)),
        out_shape=jax.ShapeDtypeStruct((r, 256), F32),
        compiler_params=_params("parallel", "arbitrary"),
        name="na",
    )(nq, nk, nk, nk, nk, nv, nv, nv, nv, nk, nv, tab)


def _diff_lambda(lq_ref, lambda_init):
    lq = lq_ref[...]
    s1 = jnp.sum(lq[0:1, :] * lq[1:2, :], axis=-1, keepdims=True)
    s2 = jnp.sum(lq[2:3, :] * lq[3:4, :], axis=-1, keepdims=True)
    return jnp.exp(s1) - jnp.exp(s2) + lambda_init


def _stack_qt(qt):
    row = lax.broadcasted_iota(jnp.int32, (2 * DIFF_DK, 1), 0)
    zero = jnp.zeros_like(qt)
    return jnp.concatenate([jnp.where(row < DIFF_DK, qt, zero), jnp.where(row >= DIFF_DK, qt, zero)], axis=1)


def _diff_finish_t(acc, tq, lam, nw_col, lambda_init):
    o = acc[0:DIFF_DV, :] / acc[DIFF_DV:DIFF_DV + 1, :]
    od = o[:, :tq] - lam * o[:, tq:]
    ms = jnp.mean(od * od, axis=0, keepdims=True)
    y = od * lax.rsqrt(ms + RMS_EPS) * nw_col * (1.0 - lambda_init)
    y = jnp.concatenate([y, jnp.zeros_like(y)], axis=0)
    return jnp.transpose(y)[:, 0:DIFF_DV]


def _diff_body(qt_ref, kc_ref, vtc_ref, k_ref, vt_ref, lq_ref, nw_ref, o_ref, s_ref, *, tq, tk, unroll, lambda_init):
    def tile(t, carry):
        q0 = pl.multiple_of(t * tq, tq)
        out = _diff_tile(qt_ref[:, pl.ds(q0, tq)], kc_ref, vtc_ref, k_ref, vt_ref, lq_ref, nw_ref, s_ref,
                         tk=tk, unroll=unroll, lambda_init=lambda_init)
        o_ref[pl.ds(q0, tq), :] = out
        return carry

    lax.fori_loop(0, qt_ref.shape[1] // tq, tile, 0)


def _diff_tile(qt, kc_ref, vtc_ref, k_ref, vt_ref, lq_ref, nw_ref, s_ref, *, tk, unroll, lambda_init):
    tq = qt.shape[1]
    n_chunks = k_ref.shape[0] // tk
    ring = s_ref.shape[0]
    qs = _stack_qt(qt)

    def chunk(j):
        return pl.ds(j * tk if isinstance(j, int) else pl.multiple_of(j * tk, tk), tk)

    def issue_scores(j, slot):
        s = _dot(k_ref[chunk(j), :], qs)
        s_ref[slot] = s
        return jnp.max(s, axis=0, keepdims=True)

    s_ctx = _dot(kc_ref[...], qs)
    m0 = jnp.max(s_ctx, axis=0, keepdims=True)
    cmax0 = tuple(issue_scores(min(a, n_chunks - 1), a % ring) for a in range(DIFF_AHEAD))
    acc0 = _dot(vtc_ref[...], jnp.exp2(s_ctx - m0).astype(BF16))

    def group(g, carry, tail=False):
        cmax, m, acc = carry
        for u in range(unroll):
            j = g * unroll + u
            if tail and j + DIFF_AHEAD >= n_chunks:
                c_new = cmax[0]
            else:
                c_new = issue_scores(j + DIFF_AHEAD, (u + DIFF_AHEAD) % ring)
            m_new = jnp.maximum(m, cmax[0])
            p = jnp.exp2(s_ref[u % ring] - m_new).astype(BF16)
            vt = vt_ref[:, chunk(j)]
            acc = jnp.exp2(m - m_new) * acc + _dot(vt, p)
            m, cmax = m_new, cmax[1:] + (c_new,)
        return cmax, m, acc

    n_groups = n_chunks // unroll
    carry = lax.fori_loop(0, n_groups - 1, group, (cmax0, m0, acc0))
    _, _, acc = group(n_groups - 1, carry, tail=True)
    return _diff_finish_t(acc, tq, _diff_lambda(lq_ref, lambda_init), nw_ref[...], lambda_init)


def _diff(dqt, dk, dvt, lq, norm_w, *, batch, seq, ctx, lambda_init):
    h, r, _ = dk.shape
    tq = DIFF_TQ
    tqs = tq * min(DIFF_QTILES, seq // tq)
    nq = seq // tqs
    tk = min(DIFF_TK, seq)
    unroll = min(DIFF_UNROLL, seq // tk)
    assert (seq // tk) % unroll == 0 and unroll % DIFF_RING == 0 and DIFF_RING > DIFF_AHEAD and seq % tqs == 0
    ctx_blk0 = batch * seq // ctx
    return pl.pallas_call(
        functools.partial(_diff_body, tq=tq, tk=tk, unroll=unroll, lambda_init=lambda_init),
        grid=(batch, h, nq),
        in_specs=[pl.BlockSpec((None, 64, tqs), lambda b, hh, i: (hh, 0, b * nq + i)),
                  pl.BlockSpec((None, ctx, 64), lambda b, hh, i: (hh, ctx_blk0 + b, 0)),
                  pl.BlockSpec((None, DIFF_VROWS, ctx), lambda b, hh, i: (hh, 0, ctx_blk0 + b)),
                  pl.BlockSpec((None, seq, 64), lambda b, hh, i: (hh, b, 0)),
                  pl.BlockSpec((None, DIFF_VROWS, seq), lambda b, hh, i: (hh, 0, b)),
                  _resident(lq.shape),
                  _resident((DIFF_DV, 1))],
        out_specs=pl.BlockSpec((None, tqs, 64), lambda b, hh, i: (hh, b * nq + i, 0)),
        out_shape=jax.ShapeDtypeStruct((h, r, 64), F32),
        scratch_shapes=[pltpu.VMEM((DIFF_RING, tk, 2 * tq), F32)],
        compiler_params=_params("parallel", "parallel", "arbitrary"),
        name="diff",
    )(dqt, dk, dvt, dk, dvt, lq, norm_w.reshape(DIFF_DV, 1))


def _ctx_attn_body(nq_ref, nk_ref, nv_ref, dqt_ref, dk_ref, dvt_ref, lq_ref, nw_ref, na_in, df_in,
                   na_out, df_out, *, lambda_init):
    del na_in, df_in
    q = nq_ref[...]
    k = nk_ref[...]
    v = nv_ref[...]
    lane = lax.broadcasted_iota(jnp.int32, (1, NA_HEADS * NA_DIM), 1) // NA_DIM
    out = jnp.zeros(na_out.shape, F32)
    for hh in range(NA_HEADS):
        mh = lane == hh
        s = _dot_nt(jnp.where(mh, q, jnp.zeros_like(q)), k)
        p = jnp.exp2(s - jnp.max(s, axis=-1, keepdims=True))
        o = _dot(p.astype(BF16), v) / jnp.sum(p, axis=-1, keepdims=True)
        out = out + jnp.where(mh, o, 0.0)
    na_out[...] = out

    lam = _diff_lambda(lq_ref, lambda_init)
    tq = dqt_ref.shape[2]
    for hh in range(DIFF_HEADS):
        s = _dot(dk_ref[hh], _stack_qt(dqt_ref[hh]))
        p = jnp.exp2(s - jnp.max(s, axis=0, keepdims=True))
        acc = _dot(dvt_ref[hh], p.astype(BF16))
        df_out[hh] = _diff_finish_t(acc, tq, lam, nw_ref[...], lambda_init)


def _ctx_attn(nq, nk, nv, dqt, dk, dvt, lq, norm_w, na_o, df_o, *, batch, seq, ctx, lambda_init):
    blk0 = batch * seq // ctx
    s256 = pl.BlockSpec((ctx, 256), lambda b: (blk0 + b, 0))
    sh64 = pl.BlockSpec((DIFF_HEADS, ctx, 64), lambda b: (0, blk0 + b, 0))
    sq_t = pl.BlockSpec((DIFF_HEADS, 64, ctx), lambda b: (0, 0, blk0 + b))
    sv_t = pl.BlockSpec((DIFF_HEADS, DIFF_VROWS, ctx), lambda b: (0, 0, blk0 + b))
    return pl.pallas_call(
        functools.partial(_ctx_attn_body, lambda_init=lambda_init),
        grid=(batch,),
        in_specs=[s256, s256, s256, sq_t, sh64, sv_t, _resident(lq.shape), _resident((DIFF_DV, 1)),
                  pl.BlockSpec(memory_space=pl.ANY), pl.BlockSpec(memory_space=pl.ANY)],
        out_specs=[s256, sh64],
        out_shape=[jax.ShapeDtypeStruct(na_o.shape, F32), jax.ShapeDtypeStruct(df_o.shape, F32)],
        input_output_aliases={8: 0, 9: 1},
        compiler_params=_params("arbitrary"),
        name="ctx_attn",
    )(nq, nk, nv, dqt, dk, dvt, lq, norm_w.reshape(DIFF_DV, 1), na_o, df_o)


def _conv_body(u_ref, prev_ref, next_ref, dw_ref, dwb_ref, lng_ref, lnb_ref, pw_ref, pwb_ref,
               o_ref, pad_ref, sh_ref, *, tiles_per_seq, n_latent_tiles):
    t = u_ref.shape[0]
    i = pl.program_id(0)
    in_ctx = i >= n_latent_tiles
    first = in_ctx | (i % tiles_per_seq == 0)
    last = in_ctx | (i % tiles_per_seq == tiles_per_seq - 1)
    pad_ref[0:HALO, :] = jnp.where(first, 0.0, prev_ref[...])
    pad_ref[HALO:HALO + t, :] = u_ref[...]
    pad_ref[HALO + t:HALO + t + HALO, :] = jnp.where(last, 0.0, next_ref[...])
    base = HALO - CONV_K // 2
    first = {}
    for phase in range(8):
        taps = [k for k in range(CONV_K) if (base + k) % 8 == phase]
        first[phase] = base + taps[0]
        span = base + taps[-1] + t - first[phase]
        sh_ref[phase, 0:span, :] = pad_ref[first[phase]:first[phase] + span, :]
    sub = 64
    parts = []
    for r0 in range(0, t, sub):
        acc = jnp.zeros((sub, CONV_CH), F32)
        for k in range(CONV_K):
            phase = (base + k) % 8
            off = base + k - first[phase] + r0
            acc = acc + dw_ref[k:k + 1, :] * sh_ref[phase, off:off + sub, :]
        parts.append(acc)
    y = jnp.concatenate(parts, axis=0) + dwb_ref[...]
    mu = jnp.mean(y, axis=-1, keepdims=True)
    yc = y - mu
    var = jnp.mean(yc * yc, axis=-1, keepdims=True)
    y = _silu(yc * lax.rsqrt(var + LN_EPS) * lng_ref[...] + lnb_ref[...])
    o_ref[...] = _dot(y.astype(BF16), pw_ref[...]) + pwb_ref[...]


def _conv(cu, dw, dw_b, ln_g, ln_b, pw, pw_b, *, batch, seq, ctx):
    r = cu.shape[0]
    t = CONV_TILE
    assert ctx == t and seq % t == 0
    n_tiles = r // t
    hb = t // HALO
    vec = lambda a: a.reshape(1, CONV_CH)
    return pl.pallas_call(
        functools.partial(_conv_body, tiles_per_seq=seq // t, n_latent_tiles=batch * seq // t),
        grid=(n_tiles,),
        in_specs=[pl.BlockSpec((t, CONV_CH), lambda i: (i, 0)),
                  pl.BlockSpec((HALO, CONV_CH), lambda i: (jnp.maximum(i * hb - 1, 0), 0)),
                  pl.BlockSpec((HALO, CONV_CH), lambda i: (jnp.minimum((i + 1) * hb, n_tiles * hb - 1), 0)),
                  _resident((CONV_K, CONV_CH))] + [_resident((1, CONV_CH))] * 3
                 + [_resident((CONV_CH, CONV_CH)), _resident((1, CONV_CH))],
        out_specs=pl.BlockSpec((t, CONV_CH), lambda i: (i, 0)),
        out_shape=jax.ShapeDtypeStruct((r, CONV_CH), F32),
        scratch_shapes=[pltpu.VMEM((t + 2 * HALO, CONV_CH), F32),
                        pltpu.VMEM((8, t + 2 * HALO, CONV_CH), F32)],
        compiler_params=_params("parallel"),
        name="conv",
    )(cu, cu, cu, dw, vec(dw_b), vec(ln_g), vec(ln_b), pw.astype(BF16), vec(pw_b))


def kernel(x, c, ctx, c_ctx, ada_w, ada_b, norm_ffn1, ffn1_w13, ffn1_w2, norm_mix, w_in, gla_wa_f, gla_ba_f, gla_wa_b, gla_ba_b, gla_norm, na_rpb, diff_lq1, diff_lk1, diff_lq2, diff_lk2, diff_norm, conv_dw, conv_dw_b, conv_ln_g, conv_ln_b, conv_pw, conv_pw_b, w_out, norm_ffn2, ffn2_w13, ffn2_w2, final_norm):
    batch, seq, d = x.shape
    n_ctx = ctx.shape[1]
    depth = ada_w.shape[0]
    tm = ROW_TILE
    assert seq % tm == 0 and (batch * n_ctx) % tm == 0 and batch + 1 <= 8
    lat_tiles = batch * seq // tm
    all_tiles = lat_tiles + batch * n_ctx // tm
    tiles_per_batch = seq // tm

    def group_of(i):
        return jnp.minimum(i // tiles_per_batch, batch)

    def pos_of(i):
        return jnp.where(i < lat_tiles, i % tiles_per_batch, tiles_per_batch)

    c_rows = jnp.concatenate([c, c_ctx[None, :], jnp.zeros((8 - batch - 1, d), F32)], axis=0)
    mods_all = _ada(c_rows, ada_w, ada_b)[:, :batch + 1].reshape(depth, batch + 1, N_MOD, d)
    rope = _rope_tables(seq, tm)
    h = x.reshape(batch * seq, d)
    h_ctx = ctx.reshape(batch * n_ctx, d)

    for i in range(depth):
        last = i == depth - 1
        lambda_init = 0.8 - 0.6 * math.exp(-0.3 * i)
        mods = mods_all[i]
        tok = dict(n_tiles=all_tiles, group_of=group_of)
        geo = dict(batch=batch, seq=seq, ctx=n_ctx)

        h = _ffn(h, mods, norm_ffn1[i], ffn1_w13[i], ffn1_w2[i], final_norm, k0=0, final=False,
                 h_ctx=h_ctx if i == 0 else None, **tok)

        wa, ba = _gate_weights(gla_wa_f[i], gla_ba_f[i], gla_wa_b[i], gla_ba_b[i])
        (gqk, gv, gvt, gg, glg, nq, nk, nv, dq, dk, dv, cu) = _proj(
            h, mods, norm_mix[i], _permute_w_in(w_in[i]), wa, ba, rope, pos_of=pos_of, **tok)

        gnorm = jnp.tile(gla_norm[i], GLA_HEADS).reshape(1, GLA_HEADS * GLA_DV)
        o_f = _gla(gqk, gv, gvt, glg, gg, None, gnorm, reverse=False, **geo)
        gx = _gla(gqk, gv, gvt, glg, gg, o_f, gnorm, reverse=True, **geo)

        nx = _na(nq, nk, nv, _na_col_tables(na_rpb[i]), **geo)

        lq = jnp.stack([diff_lq1[i], diff_lk1[i], diff_lq2[i], diff_lk2[i]])
        dx = _diff(dq, dk, dv, lq, diff_norm[i], lambda_init=lambda_init, **geo)
        if not last:
            nx, dx = _ctx_attn(nq, nk, nv, dq, dk, dv, lq, diff_norm[i], nx, dx,
                               lambda_init=lambda_init, **geo)

        cx = _conv(cu, conv_dw[i], conv_dw_b[i], conv_ln_g[i], conv_ln_b[i], conv_pw[i], conv_pw_b[i], **geo)

        if last:
            tok = dict(n_tiles=lat_tiles, group_of=group_of)
        h = _ffn(h, mods, norm_ffn2[i], ffn2_w13[i], ffn2_w2[i], final_norm, k0=6, final=last,
                 mixers=(gx, nx, dx, cx, w_out[i], (batch, seq, n_ctx)), **tok)

    return h.reshape(batch, seq, d)
```

```python
import functools
import math

import jax
import jax.numpy as jnp
from jax import lax
from jax.experimental import pallas as pl
from jax.experimental.pallas import tpu as pltpu

F32 = jnp.float32
BF16 = jnp.bfloat16

GRID_W = 64
N_MOD = 9
RMS_EPS = 1e-6
LN_EPS = 1e-5
NEG_INF = -1e30
GLA_HEADS, GLA_DK, GLA_DV, GLA_RANK, GLA_TAU, GLA_CHUNK = 4, 32, 64, 16, 16.0, 64
NA_HEADS, NA_DIM, NA_ROWS, NA_COLS = 4, 64, 8, 16
DIFF_HEADS, DIFF_DK, DIFF_DV = 4, 32, 64
DIFF_VROWS = 128
CONV_CH, CONV_K = 256, 31
ROPE_BASE = 10000.0
LOG2E = 1.4426950408889634

LANES = 128
VMEM_LIMIT = 56 * 1024 * 1024

ROW_TILE = 512
FF_CHUNK = 256
GLA_BLOCK = 256
NA_QROWS = 8
NA_KBLK = 256
CONV_TILE = 256
DIFF_TQ = 256
DIFF_QTILES = 8
DIFF_TK = 256
DIFF_UNROLL = 16
DIFF_AHEAD = 3
DIFF_RING = 4
HALO = 16

C_GLA, C_NA, C_DIFF, C_CONV, C_AUX, C_END = 0, 768, 1536, 2304, 2816, 2944


def _dot(a, b):
    return jnp.dot(a, b, preferred_element_type=F32)


def _dot_nt(a, b):
    return lax.dot_general(a, b, (((1,), (1,)), ((), ())), preferred_element_type=F32)


def _params(*sem):
    return pltpu.CompilerParams(dimension_semantics=sem, vmem_limit_bytes=VMEM_LIMIT)


def _resident(shape):
    nd = len(shape)
    return pl.BlockSpec(shape, lambda *_: (0,) * nd, pipeline_mode=pl.Buffered(1))


def _silu(x):
    return x * jax.nn.sigmoid(x)


def _rms(x, w):
    return x * lax.rsqrt(jnp.mean(x * x, axis=-1, keepdims=True) + RMS_EPS) * w


def _ada_body(c_ref, w_ref, b_ref, o_ref):
    s = _silu(c_ref[...])
    o_ref[...] = jnp.dot(s, w_ref[...], precision=lax.Precision.HIGHEST,
                         preferred_element_type=F32) + b_ref[...]


def _ada(c_rows, ada_w, ada_b):
    depth, d, _ = ada_w.shape
    return pl.pallas_call(
        _ada_body,
        grid=(depth, N_MOD),
        in_specs=[pl.BlockSpec((8, d), lambda l, n: (0, 0)),
                  pl.BlockSpec((None, d, d), lambda l, n: (l, 0, n)),
                  pl.BlockSpec((None, 1, d), lambda l, n: (l, 0, n))],
        out_specs=pl.BlockSpec((None, 8, d), lambda l, n: (l, 0, n)),
        out_shape=jax.ShapeDtypeStruct((depth, 8, N_MOD * d), F32),
        compiler_params=_params("arbitrary", "arbitrary"),
        name="ada",
    )(c_rows, ada_w, ada_b.reshape(depth, 1, N_MOD * d))


def _ffn_body(*refs, k0, final, source, n_first):
    h_ref = refs[0]
    n_extra = {"plain": 0, "two_arrays": 1, "mixers": 6}[source]
    extra = refs[1:1 + n_extra]
    mod_ref, nw_ref, w13_ref, w2_ref, fw_ref, o_ref, xb_ref, g_ref, h13_ref = refs[1 + n_extra:]
    ff = w2_ref.shape[0]
    tf = h13_ref.shape[2] // 2
    n_chunks = ff // tf
    x = h_ref[...]
    if source == "two_arrays":
        x = jnp.where(pl.program_id(0) < n_first, x, extra[0][...])
    elif source == "mixers":
        gx_ref, gxc_ref, nx_ref, dx_ref, cx_ref, wo_ref = extra
        gx = jnp.where(pl.program_id(0) < n_first, gx_ref[...], gxc_ref[...].reshape(gx_ref.shape))
        mix = jnp.concatenate([gx, nx_ref[...]] + [dx_ref[hh] for hh in range(DIFF_HEADS)]
                              + [cx_ref[...]], axis=-1).astype(BF16)
        x = x + mod_ref[5:6, :] * _dot(mix, wo_ref[...])
    xm = _rms(x, nw_ref[...]) * (1.0 + mod_ref[k0 + 1:k0 + 2, :]) + mod_ref[k0:k0 + 1, :]
    xb_ref[...] = xm.astype(BF16)

    def cols(c):
        return c * tf if isinstance(c, int) else pl.multiple_of(c * tf, tf)

    def up(c, slot):
        h13_ref[slot, :, :tf] = _dot(xb_ref[...], w13_ref[:, pl.ds(cols(c), tf)])
        h13_ref[slot, :, tf:] = _dot(xb_ref[...], w13_ref[:, pl.ds(ff + cols(c), tf)])

    def gate(c, slot):
        a = h13_ref[slot, :, :tf]
        u = h13_ref[slot, :, tf:]
        g_ref[:, pl.ds(cols(c), tf)] = (_silu(a) * u).astype(BF16)

    def pair(t, carry):
        up(2 * t + 1, 1)
        gate(2 * t, 0)
        up(2 * t + 2, 0)
        gate(2 * t + 1, 1)
        return carry

    up(0, 0)
    for t in range((n_chunks - 1) // 2):
        pair(t, 0)
    gate(n_chunks - 1, 0)
    out = x + (0.5 * mod_ref[k0 + 2:k0 + 3, :]) * _dot(g_ref[...], w2_ref[...])
    if final:
        out = _rms(out, fw_ref[...])
    o_ref[...] = out


def _ffn(h, mods, norm_w, w13, w2, final_w, *, k0, n_tiles, group_of, final, h_ctx=None, mixers=None):
    d = h.shape[1]
    tm = ROW_TILE
    ff = w2.shape[0]
    n_chunks = ff // FF_CHUNK
    assert ff % FF_CHUNK == 0 and n_chunks % 2 == 1
    row = lambda i: (i, 0)
    source, n_first, extra, extra_specs, h_spec = "plain", None, [], [], pl.BlockSpec((tm, d), row)
    if h_ctx is not None:
        source, n_first, extra = "two_arrays", h.shape[0] // tm, [h_ctx]
        h_spec = pl.BlockSpec((tm, d), lambda i: (jnp.minimum(i, n_first - 1), 0))
        extra_specs = [pl.BlockSpec((tm, d), lambda i: (jnp.maximum(i - n_first, 0), 0))]
    elif mixers is not None:
        gx, nx, dx, cx, w_out, (batch, seq, n_ctx) = mixers
        assert batch * n_ctx == tm and seq % tm == 0 and seq % n_ctx == 0
        source, n_first, extra = "mixers", batch * seq // tm, [gx, gx, nx, dx, cx, w_out.astype(BF16)]
        per_seq = seq // tm
        extra_specs = [pl.BlockSpec((None, tm, 256),
                                    lambda i: (jnp.minimum(i // per_seq, batch - 1), i % per_seq, 0)),
                       pl.BlockSpec((batch, n_ctx, 256), lambda i: (0, seq // n_ctx, 0)),
                       pl.BlockSpec((tm, 256), row),
                       pl.BlockSpec((DIFF_HEADS, tm, 64), lambda i: (0, i, 0)),
                       pl.BlockSpec((tm, 256), row), _resident(w_out.shape)]
    return pl.pallas_call(
        functools.partial(_ffn_body, k0=k0, final=final, source=source, n_first=n_first),
        grid=(n_tiles,),
        in_specs=[h_spec] + extra_specs + [
            pl.BlockSpec((None, N_MOD, d), lambda i: (group_of(i), 0, 0)),
            _resident((1, d)),
            _resident(w13.shape),
            _resident(w2.shape),
            _resident((1, d))],
        out_specs=pl.BlockSpec((tm, d), row),
        out_shape=jax.ShapeDtypeStruct((n_tiles * tm, d), F32),
        scratch_shapes=[pltpu.VMEM((tm, d), BF16), pltpu.VMEM((tm, ff), BF16),
                        pltpu.VMEM((2, tm, 2 * FF_CHUNK), F32)],
        compiler_params=_params("parallel"),
        name="ffn",
    )(h, *extra, mods, norm_w.reshape(1, d), w13.astype(BF16), w2.astype(BF16), final_w.reshape(1, d))


def _log_sigmoid(x):
    return jnp.minimum(x, 0.0) - jnp.log1p(jnp.exp(-jnp.abs(x)))


def _rope_rotate(x):
    n = x.shape[-1]
    lane = lax.broadcasted_iota(jnp.int32, x.shape, 1)
    up = pltpu.roll(x, n - 8, 1)
    dn = pltpu.roll(x, 8, 1)
    return jnp.where((lane & 15) < 8, -up, dn)


def _rope_rotate_rows(x):
    n = x.shape[0]
    row = lax.broadcasted_iota(jnp.int32, x.shape, 0)
    up = pltpu.roll(x, n - 8, 0)
    dn = pltpu.roll(x, 8, 0)
    return jnp.where((row & 15) < 8, -up, dn)


def _proj_body(h_ref, mod_ref, nw_ref, w_ref, wvt_ref, wdqt_ref, wdvt_ref, wa_ref, ba_ref,
               cos_ref, sin_ref, cost_ref, sint_ref,
               gqk_ref, gv_ref, gvt_ref, gg_ref, glg_ref, nq_ref, nk_ref, nv_ref,
               dqt_ref, dk_ref, dvt_ref, cu_ref, xb_ref):
    x = h_ref[...]
    tm = x.shape[0]
    xm = _rms(x, nw_ref[...]) * (1.0 + mod_ref[4:5, :]) + mod_ref[3:4, :]
    xb_ref[...] = xm.astype(BF16)

    z = _dot(xb_ref[...], w_ref[:, C_GLA:C_GLA + 256])
    lane = lax.broadcasted_iota(jnp.int32, (1, 256), 1)
    gqk_ref[...] = z * jnp.where(lane < 128, GLA_DK ** -0.5, 1.0)
    gv_ref[...] = _dot(xb_ref[...], w_ref[:, C_GLA + 256:C_GLA + 512])
    gvt_ref[...] = _dot_nt(wvt_ref[...], xb_ref[...])
    gg_ref[...] = _dot(xb_ref[...], w_ref[:, C_GLA + 512:C_GLA + 768])
    aux = _dot(xb_ref[...], w_ref[:, C_AUX:C_END])
    pre = _dot(aux.astype(BF16), wa_ref[...]) + ba_ref[...]
    glg_ref[...] = _log_sigmoid(pre) * (1.0 / GLA_TAU)

    nq_ref[...] = (_dot(xb_ref[...], w_ref[:, C_NA:C_NA + 256]) * (NA_DIM ** -0.5 * LOG2E)).astype(BF16)
    nk_ref[...] = _dot(xb_ref[...], w_ref[:, C_NA + 256:C_NA + 512]).astype(BF16)
    nv_ref[...] = _dot(xb_ref[...], w_ref[:, C_NA + 512:C_NA + 768]).astype(BF16)

    cos = cos_ref[...]
    sin = sin_ref[...]
    cos2 = jnp.concatenate([cos, cos], axis=1)
    sin2 = jnp.concatenate([sin, sin], axis=1)
    zk = _dot(xb_ref[...], w_ref[:, C_DIFF + 256:C_DIFF + 512])
    zk = zk * cos2 + _rope_rotate(zk) * sin2
    n_rep = 2 * DIFF_HEADS
    cos_t = jnp.concatenate([cost_ref[...]] * n_rep, axis=0)
    sin_t = jnp.concatenate([sint_ref[...]] * n_rep, axis=0)
    zqt = _dot_nt(wdqt_ref[...], xb_ref[...])
    zqt = (zqt * cos_t + _rope_rotate_rows(zqt) * sin_t) * (DIFF_DK ** -0.5 * LOG2E)
    zvt = _dot_nt(wdvt_ref[...], xb_ref[...])
    pad_rows = DIFF_VROWS - DIFF_DV
    one_row = jnp.where(lax.broadcasted_iota(jnp.int32, (pad_rows, tm), 0) == 0, 1.0, 0.0).astype(BF16)
    for hh in range(DIFF_HEADS):
        sl = slice(64 * hh, 64 * hh + 64)
        dk_ref[hh] = zk[:, sl].astype(BF16)
        dqt_ref[hh] = zqt[sl, :].astype(BF16)
        dvt_ref[hh, 0:64, :] = zvt[sl, :].astype(BF16)
        dvt_ref[hh, DIFF_DV:DIFF_VROWS, :] = one_row

    za = _dot(xb_ref[...], w_ref[:, C_CONV:C_CONV + 256])
    zg = _dot(xb_ref[...], w_ref[:, C_CONV + 256:C_CONV + 512])
    cu_ref[...] = za * jax.nn.sigmoid(zg)


def _proj(h, mods, norm_w, w_p, wa, ba, rope, *, n_tiles, group_of, pos_of):
    r, d = h.shape
    tm = ROW_TILE
    row = lambda i: (i, 0)
    hrow = lambda i: (0, i, 0)
    hcol = lambda i: (0, 0, i)
    f32_256 = jax.ShapeDtypeStruct((r, 256), F32)
    bf_256 = jax.ShapeDtypeStruct((r, 256), BF16)
    wvt = w_p[:, C_GLA + 256:C_GLA + 512].T
    wdqt = w_p[:, C_DIFF:C_DIFF + 256].T
    wdvt = w_p[:, C_DIFF + 512:C_DIFF + 768].T
    cos_r, sin_r, cos_c, sin_c = rope
    out_shape = [f32_256, f32_256, jax.ShapeDtypeStruct((256, r), F32), f32_256, f32_256,
                 bf_256, bf_256, bf_256,
                 jax.ShapeDtypeStruct((DIFF_HEADS, 64, r), BF16),
                 jax.ShapeDtypeStruct((DIFF_HEADS, r, 64), BF16),
                 jax.ShapeDtypeStruct((DIFF_HEADS, DIFF_VROWS, r), BF16),
                 f32_256]
    out_specs = [pl.BlockSpec((tm, 256), row)] * 2 + [pl.BlockSpec((256, tm), lambda i: (0, i))] + [
        pl.BlockSpec((tm, 256), row)] * 5 + [
        pl.BlockSpec((DIFF_HEADS, 64, tm), hcol),
        pl.BlockSpec((DIFF_HEADS, tm, 64), hrow),
        pl.BlockSpec((DIFF_HEADS, DIFF_VROWS, tm), hcol),
        pl.BlockSpec((tm, 256), row)]
    return pl.pallas_call(
        _proj_body,
        grid=(n_tiles,),
        in_specs=[pl.BlockSpec((tm, d), row),
                  pl.BlockSpec((None, N_MOD, d), lambda i: (group_of(i), 0, 0)),
                  _resident((1, d)),
                  _resident(w_p.shape),
                  _resident(wvt.shape),
                  _resident(wdqt.shape),
                  _resident(wdvt.shape),
                  _resident(wa.shape),
                  _resident(ba.shape),
                  pl.BlockSpec((tm, LANES), lambda i: (pos_of(i), 0)),
                  pl.BlockSpec((tm, LANES), lambda i: (pos_of(i), 0)),
                  pl.BlockSpec((DIFF_DK, tm), lambda i: (0, pos_of(i))),
                  pl.BlockSpec((DIFF_DK, tm), lambda i: (0, pos_of(i)))],
        out_specs=out_specs,
        out_shape=out_shape,
        scratch_shapes=[pltpu.VMEM((tm, d), BF16)],
        compiler_params=_params("parallel"),
        name="proj",
    )(h, mods, norm_w.reshape(1, d), w_p, wvt, wdqt, wdvt, wa, ba, cos_r, sin_r, cos_c, sin_c)


def _permute_w_in(w_in):
    d = w_in.shape[0]
    g0 = 2 * GLA_HEADS * GLA_DK + 2 * GLA_HEADS * GLA_DV
    aux = w_in[:, g0:g0 + 2 * GLA_RANK]
    rest = w_in[:, g0 + 2 * GLA_RANK:]
    pad = jnp.zeros((d, C_END - C_AUX - 2 * GLA_RANK), w_in.dtype)
    return jnp.concatenate([w_in[:, :g0], rest, aux, pad], axis=1).astype(BF16)


def _gate_weights(wa_f, ba_f, wa_b, ba_b):
    n = GLA_HEADS * GLA_DK
    wa = jnp.zeros((C_END - C_AUX, 2 * n), F32)
    wa = wa.at[:GLA_RANK, :n].set(wa_f).at[GLA_RANK:2 * GLA_RANK, n:].set(wa_b)
    return wa.astype(BF16), jnp.concatenate([ba_f, ba_b]).reshape(1, 2 * n)


def _rope_tables(seq, tile):
    t = jnp.arange(seq)
    row = (t // GRID_W).astype(F32)
    col = (t % GRID_W).astype(F32)
    half = DIFF_DK // 2
    inv = 1.0 / (ROPE_BASE ** (jnp.arange(0, half, 2, dtype=F32) / half))
    ang_r = row[:, None] * inv
    ang_c = col[:, None] * inv
    ang = jnp.concatenate([ang_r, ang_r, ang_c, ang_c], axis=-1)
    cos = jnp.concatenate([jnp.cos(ang), jnp.ones((tile, DIFF_DK), F32)], axis=0)
    sin = jnp.concatenate([jnp.sin(ang), jnp.zeros((tile, DIFF_DK), F32)], axis=0)
    rep = (1, LANES // DIFF_DK)
    return jnp.tile(cos, rep), jnp.tile(sin, rep), cos.T, sin.T


def _split3(x):
    hi = x.astype(BF16)
    r1 = x - hi.astype(F32)
    mid = r1.astype(BF16)
    lo = (r1 - mid.astype(F32)).astype(BF16)
    return hi, mid, lo


def _gla_body(*refs, reverse, batch):
    per_b = [refs[5 * b:5 * b + 5] for b in range(batch)]
    of_ref, nw_ref, o_ref, st_ref = refs[5 * batch:]
    blk = per_b[0][0].shape[0]
    n_chunks = blk // GLA_CHUNK
    nqk = GLA_HEADS * GLA_DK
    nv = GLA_HEADS * GLA_DV

    @pl.when(pl.program_id(0) == 0)
    def _():
        st_ref[...] = jnp.zeros_like(st_ref)

    ri = lax.broadcasted_iota(jnp.int32, (blk, blk), 0)
    ci = lax.broadcasted_iota(jnp.int32, (blk, blk), 1)
    same = (ri // GLA_CHUNK) == (ci // GLA_CHUNK)
    tri = jnp.where(same & ((ci >= ri) if reverse else (ci <= ri)), 1.0, 0.0).astype(BF16)
    bcum_of = []
    for qk_ref, v_ref, vt_ref, lg_ref, g_ref in per_b:
        lg = lg_ref[:, nqk:2 * nqk] if reverse else lg_ref[:, 0:nqk]
        hi, mid, lo = _split3(lg)
        bcum_of.append(_dot(tri, hi) + _dot(tri, mid) + _dot(tri, lo))

    c = GLA_CHUNK
    rk = lax.broadcasted_iota(jnp.int32, (GLA_HEADS * c, nqk), 0) // c
    ck = lax.broadcasted_iota(jnp.int32, (GLA_HEADS * c, nqk), 1) // GLA_DK
    mask_k = rk == ck
    rv = lax.broadcasted_iota(jnp.int32, (GLA_HEADS * c, nv), 0) // c
    cv = lax.broadcasted_iota(jnp.int32, (GLA_HEADS * c, nv), 1) // GLA_DV
    mask_v = rv == cv
    rs = lax.broadcasted_iota(jnp.int32, (nv, nqk), 0) // GLA_DV
    cs = lax.broadcasted_iota(jnp.int32, (nv, nqk), 1) // GLA_DK
    mask_s = rs == cs
    ai = lax.broadcasted_iota(jnp.int32, (c, GLA_HEADS * c), 0)
    aj = lax.broadcasted_iota(jnp.int32, (c, GLA_HEADS * c), 1) % c
    mask_a = (aj >= ai) if reverse else (aj <= ai)

    order = range(n_chunks - 1, -1, -1) if reverse else range(n_chunks)
    units = [(b, ch) for ch in order for b in range(batch)]
    q_in, a_raw, u_t, decay = {}, {}, {}, {}
    vts = [refs_b[2][...].astype(BF16) for refs_b in per_b]
    for b, ch in units:
        qk_ref = per_b[b][0]
        rows = slice(ch * c, (ch + 1) * c)
        k = qk_ref[rows, nqk:2 * nqk]
        bcum = bcum_of[b][rows, :]
        btot = bcum[0:1, :] if reverse else bcum[c - 1:c, :]
        q_in[b, ch] = (qk_ref[rows, 0:nqk] * jnp.exp(bcum)).astype(BF16)
        k_in = k * jnp.exp(-bcum)
        k_out = (k * jnp.exp(btot - bcum)).astype(BF16)
        k_bd = jnp.where(mask_k, jnp.concatenate([k_in] * GLA_HEADS, axis=0), 0.0).astype(BF16)
        a_raw[b, ch] = _dot_nt(q_in[b, ch], k_bd)
        k_pad = jnp.concatenate([jnp.zeros((n * c, nqk), BF16) for n in (ch,) if n] + [k_out]
                                + [jnp.zeros((n * c, nqk), BF16) for n in (n_chunks - 1 - ch,) if n], axis=0)
        u_t[b, ch] = jnp.where(mask_s, _dot(vts[b], k_pad), 0.0)
        decay[b, ch] = jnp.exp(btot)
    o = {}
    for b, ch in units:
        v = per_b[b][1][ch * c:(ch + 1) * c, :]
        v_bd = jnp.where(mask_v, jnp.concatenate([v] * GLA_HEADS, axis=0), 0.0).astype(BF16)
        o[b, ch] = _dot(jnp.where(mask_a, a_raw[b, ch], 0.0).astype(BF16), v_bd)
    st = [st_ref[b] for b in range(batch)]
    for b, ch in units:
        o[b, ch] = o[b, ch] + _dot_nt(q_in[b, ch], st[b].astype(BF16))
        st[b] = st[b] * decay[b, ch] + u_t[b, ch]
    for b in range(batch):
        st_ref[b] = st[b]
        o_blk = jnp.concatenate([o[b, ch] for ch in range(n_chunks)], axis=0)
        if reverse:
            o_blk = o_blk + of_ref[b]
            hi2, lo2, _ = _split3(o_blk * o_blk)
            hr = lax.broadcasted_iota(jnp.int32, (nv, nv), 0) // GLA_DV
            hc = lax.broadcasted_iota(jnp.int32, (nv, nv), 1) // GLA_DV
            seg = jnp.where(hr == hc, 1.0, 0.0).astype(BF16)
            ms = (_dot(hi2, seg) + _dot(lo2, seg)) * (1.0 / GLA_DV)
            o_blk = o_blk * lax.rsqrt(ms + RMS_EPS) * nw_ref[...] * _silu(per_b[b][4][...])
        o_ref[b] = o_blk


def _gla(gqk, gv, gvt, glg, gg, o_f, norm_w4, *, reverse, batch, seq, ctx):
    blk = GLA_BLOCK
    nc, nl = ctx // blk, seq // blk
    ctx_base = batch * seq // blk

    def step_blk(s):
        if reverse:
            return jnp.where(s < nc, nl + (nc - 1 - s), nl - 1 - (s - nc))
        return jnp.where(s < nc, nl + s, s - nc)

    def row_blk(b):
        return lambda s: jnp.where(s < nc, ctx_base + b * nc - nl, b * nl) + step_blk(s)

    specs, args = [], []
    for b in range(batch):
        spec = pl.BlockSpec((blk, 256), lambda s, f=row_blk(b): (f(s), 0))
        spec_t = pl.BlockSpec((256, blk), lambda s, f=row_blk(b): (0, f(s)))
        specs += [spec, spec, spec_t, spec, spec]
        args += [gqk, gv, gvt, glg, gg]
    seq_spec = pl.BlockSpec((batch, blk, 256), lambda s: (0, step_blk(s), 0))
    if o_f is None:
        o_f, of_spec = norm_w4, _resident((1, 256))
    else:
        of_spec = seq_spec
    return pl.pallas_call(
        functools.partial(_gla_body, reverse=reverse, batch=batch),
        grid=(nc + nl,),
        in_specs=specs + [of_spec, _resident((1, 256))],
        out_specs=seq_spec,
        out_shape=jax.ShapeDtypeStruct((batch, seq + ctx, 256), F32),
        scratch_shapes=[pltpu.VMEM((batch, GLA_HEADS * GLA_DV, GLA_HEADS * GLA_DK), F32)],
        compiler_params=_params("arbitrary"),
        name="gla_bwd" if reverse else "gla_fwd",
    )(*args, o_f, norm_w4)


def _na_col_tables(rpb):
    col = jnp.arange(GRID_W)
    cstart = jnp.clip(col - NA_COLS // 2, 0, GRID_W - NA_COLS)
    col_ok = (col[None, :] >= cstart[:, None]) & (col[None, :] < cstart[:, None] + NA_COLS)
    col_off = jnp.clip(col[None, :] - col[:, None] + (NA_COLS - 1), 0, 2 * NA_COLS - 2)
    t = jnp.where(col_ok[None, None], rpb[:, :, col_off] * LOG2E, NEG_INF)
    dead = jnp.full((NA_HEADS, 1, GRID_W, GRID_W), NEG_INF, F32)
    t = jnp.concatenate([dead, t.astype(F32), dead], axis=1)
    return jnp.concatenate([t[:, :-1], t[:, 1:]], axis=-1)


def _na_body(q_ref, k0, k1, k2, k3, v0, v1, v2, v3, kc_ref, vc_ref, tab_ref, o_ref, *, rows, key_blocks):
    k_blk = [k0[...], k1[...], k2[...], k3[...]]
    v_blk = [v0[...], v1[...], v2[...], v3[...]]
    kc = kc_ref[...]
    vc = vc_ref[...]

    j = pl.program_id(1)
    rows_per_blk = NA_KBLK // GRID_W
    r0 = j * NA_QROWS
    kr0 = jnp.clip(2 * j - 1, 0, key_blocks - 4) * rows_per_blk
    lane_lo = lax.broadcasted_iota(jnp.int32, (1, 2 * GRID_W), 1) < GRID_W
    lane = lax.broadcasted_iota(jnp.int32, (1, NA_HEADS * NA_DIM), 1) // NA_DIM
    half = NA_QROWS // 2
    n_pairs = 3 * rows_per_blk // 2
    gq = half * GRID_W
    for g in range(2):
        first = jnp.clip(r0 + g * half - NA_ROWS // 2, 0, rows - NA_ROWS)
        shift = jnp.clip((first - kr0) // rows_per_blk, 0, 1)
        kw = jnp.concatenate([jnp.where(shift == 0, k_blk[i], k_blk[i + 1]) for i in range(3)], axis=0)
        vw = jnp.concatenate([jnp.where(shift == 0, v_blk[i], v_blk[i + 1]) for i in range(3)], axis=0)
        kg0 = kr0 + shift * rows_per_blk
        entry, ok = [], []
        for a in range(half):
            r = r0 + g * half + a
            start = jnp.clip(r - NA_ROWS // 2, 0, rows - NA_ROWS)
            for bp in range(n_pairs):
                rk = kg0 + 2 * bp
                entry.append(jnp.clip(rk - r + NA_ROWS, 0, 2 * NA_ROWS - 1))
                in0 = ((rk >= start) & (rk < start + NA_ROWS)).astype(jnp.int32)
                in1 = ((rk + 1 >= start) & (rk + 1 < start + NA_ROWS)).astype(jnp.int32)
                ok.append(jnp.where(lane_lo, in0, in1) != 0)

        def bias_of(hh):
            rows_ = []
            for a in range(half):
                tiles = [jnp.where(ok[a * n_pairs + bp], tab_ref[hh, entry[a * n_pairs + bp]], NEG_INF)
                         for bp in range(n_pairs)]
                rows_.append(jnp.concatenate(tiles, axis=1))
            return jnp.concatenate(rows_, axis=0)

        q = q_ref[g * gq:(g + 1) * gq, :]
        out = jnp.zeros((gq, NA_HEADS * NA_DIM), F32)
        for hh in range(NA_HEADS):
            mh = lane == hh
            qh = jnp.where(mh, q, jnp.zeros_like(q))
            s_w = _dot_nt(qh, kw) + bias_of(hh)
            s_c = _dot_nt(qh, kc)
            m = jnp.maximum(jnp.max(s_w, axis=-1, keepdims=True), jnp.max(s_c, axis=-1, keepdims=True))
            p_w = jnp.exp2(s_w - m)
            p_c = jnp.exp2(s_c - m)
            l = jnp.sum(p_w, axis=-1, keepdims=True) + jnp.sum(p_c, axis=-1, keepdims=True)
            o = _dot(p_w.astype(BF16), vw) + _dot(p_c.astype(BF16), vc)
            out = out + jnp.where(mh, o / l, 0.0)
        o_ref[g * gq:(g + 1) * gq, :] = out


def _na(nq, nk, nv, tab, *, batch, seq, ctx):
    r = nq.shape[0]
    tq = NA_QROWS * GRID_W
    nj = seq // tq
    kb = seq // NA_KBLK
    assert kb >= 4 and seq % tq == 0 and seq % ctx == 0
    ctx_blk0 = batch * seq // ctx

    def kspec(i):
        return pl.BlockSpec((NA_KBLK, 256),
                            lambda b, j: (b * kb + jnp.clip(2 * j - 1, 0, kb - 4) + i, 0))

    cspec = pl.BlockSpec((ctx, 256), lambda b, j: (ctx_blk0 + b, 0))
    return pl.pallas_call(
        functools.partial(_na_body, rows=seq // GRID_W, key_blocks=kb),
        grid=(batch, nj),
        in_specs=[pl.BlockSpec((tq, 256), lambda b, j: (b * nj + j, 0))]
                 + [kspec(i) for i in range(4)] + [kspec(i) for i in range(4)]
                 + [cspec, cspec, _resident(tab.shape)],
        out_specs=pl.BlockSpec((tq, 256), lambda b, j: (b * nj + j, 0)),
        out_shape=jax.ShapeDtypeStruct((r, 256), F32),
        compiler_params=_params("parallel", "arbitrary"),
        name="na",
    )(nq, nk, nk, nk, nk, nv, nv, nv, nv, nk, nv, tab)


def _diff_lambda(lq_ref, lambda_init):
    lq = lq_ref[...]
    s1 = jnp.sum(lq[0:1, :] * lq[1:2, :], axis=-1, keepdims=True)
    s2 = jnp.sum(lq[2:3, :] * lq[3:4, :], axis=-1, keepdims=True)
    return jnp.exp(s1) - jnp.exp(s2) + lambda_init


def _stack_qt(qt):
    row = lax.broadcasted_iota(jnp.int32, (2 * DIFF_DK, 1), 0)
    zero = jnp.zeros_like(qt)
    return jnp.concatenate([jnp.where(row < DIFF_DK, qt, zero), jnp.where(row >= DIFF_DK, qt, zero)], axis=1)


def _diff_finish_t(acc, tq, lam, nw_col, lambda_init):
    o = acc[0:DIFF_DV, :] / acc[DIFF_DV:DIFF_DV + 1, :]
    od = o[:, :tq] - lam * o[:, tq:]
    ms = jnp.mean(od * od, axis=0, keepdims=True)
    y = od * lax.rsqrt(ms + RMS_EPS) * nw_col * (1.0 - lambda_init)
    y = jnp.concatenate([y, jnp.zeros_like(y)], axis=0)
    return jnp.transpose(y)[:, 0:DIFF_DV]


def _diff_body(qt_ref, kc_ref, vtc_ref, k_ref, vt_ref, lq_ref, nw_ref, o_ref, s_ref, *, tq, tk, unroll, lambda_init):
    def tile(t, carry):
        q0 = pl.multiple_of(t * tq, tq)
        out = _diff_tile(qt_ref[:, pl.ds(q0, tq)], kc_ref, vtc_ref, k_ref, vt_ref, lq_ref, nw_ref, s_ref,
                         tk=tk, unroll=unroll, lambda_init=lambda_init)
        o_ref[pl.ds(q0, tq), :] = out
        return carry

    lax.fori_loop(0, qt_ref.shape[1] // tq, tile, 0)


def _diff_tile(qt, kc_ref, vtc_ref, k_ref, vt_ref, lq_ref, nw_ref, s_ref, *, tk, unroll, lambda_init):
    tq = qt.shape[1]
    n_chunks = k_ref.shape[0] // tk
    ring = s_ref.shape[0]
    qs = _stack_qt(qt)

    def chunk(j):
        return pl.ds(j * tk if isinstance(j, int) else pl.multiple_of(j * tk, tk), tk)

    def issue_scores(j, slot):
        s = _dot(k_ref[chunk(j), :], qs)
        s_ref[slot] = s
        return jnp.max(s, axis=0, keepdims=True)

    s_ctx = _dot(kc_ref[...], qs)
    m0 = jnp.max(s_ctx, axis=0, keepdims=True)
    cmax0 = tuple(issue_scores(min(a, n_chunks - 1), a % ring) for a in range(DIFF_AHEAD))
    acc0 = _dot(vtc_ref[...], jnp.exp2(s_ctx - m0).astype(BF16))

    def group(g, carry, tail=False):
        cmax, m, acc = carry
        for u in range(unroll):
            j = g * unroll + u
            if tail and j + DIFF_AHEAD >= n_chunks:
                c_new = cmax[0]
            else:
                c_new = issue_scores(j + DIFF_AHEAD, (u + DIFF_AHEAD) % ring)
            m_new = jnp.maximum(m, cmax[0])
            p = jnp.exp2(s_ref[u % ring] - m_new).astype(BF16)
            vt = vt_ref[:, chunk(j)]
            acc = jnp.exp2(m - m_new) * acc + _dot(vt, p)
            m, cmax = m_new, cmax[1:] + (c_new,)
        return cmax, m, acc

    n_groups = n_chunks // unroll
    carry = lax.fori_loop(0, n_groups - 1, group, (cmax0, m0, acc0))
    _, _, acc = group(n_groups - 1, carry, tail=True)
    return _diff_finish_t(acc, tq, _diff_lambda(lq_ref, lambda_init), nw_ref[...], lambda_init)


def _diff(dqt, dk, dvt, lq, norm_w, *, batch, seq, ctx, lambda_init):
    h, r, _ = dk.shape
    tq = DIFF_TQ
    tqs = tq * min(DIFF_QTILES, seq // tq)
    nq = seq // tqs
    tk = min(DIFF_TK, seq)
    unroll = min(DIFF_UNROLL, seq // tk)
    assert (seq // tk) % unroll == 0 and unroll % DIFF_RING == 0 and DIFF_RING > DIFF_AHEAD and seq % tqs == 0
    ctx_blk0 = batch * seq // ctx
    return pl.pallas_call(
        functools.partial(_diff_body, tq=tq, tk=tk, unroll=unroll, lambda_init=lambda_init),
        grid=(batch, h, nq),
        in_specs=[pl.BlockSpec((None, 64, tqs), lambda b, hh, i: (hh, 0, b * nq + i)),
                  pl.BlockSpec((None, ctx, 64), lambda b, hh, i: (hh, ctx_blk0 + b, 0)),
                  pl.BlockSpec((None, DIFF_VROWS, ctx), lambda b, hh, i: (hh, 0, ctx_blk0 + b)),
                  pl.BlockSpec((None, seq, 64), lambda b, hh, i: (hh, b, 0)),
                  pl.BlockSpec((None, DIFF_VROWS, seq), lambda b, hh, i: (hh, 0, b)),
                  _resident(lq.shape),
                  _resident((DIFF_DV, 1))],
        out_specs=pl.BlockSpec((None, tqs, 64), lambda b, hh, i: (hh, b * nq + i, 0)),
        out_shape=jax.ShapeDtypeStruct((h, r, 64), F32),
        scratch_shapes=[pltpu.VMEM((DIFF_RING, tk, 2 * tq), F32)],
        compiler_params=_params("parallel", "parallel", "arbitrary"),
        name="diff",
    )(dqt, dk, dvt, dk, dvt, lq, norm_w.reshape(DIFF_DV, 1))


def _ctx_attn_body(nq_ref, nk_ref, nv_ref, dqt_ref, dk_ref, dvt_ref, lq_ref, nw_ref, na_in, df_in,
                   na_out, df_out, *, lambda_init):
    del na_in, df_in
    q = nq_ref[...]
    k = nk_ref[...]
    v = nv_ref[...]
    lane = lax.broadcasted_iota(jnp.int32, (1, NA_HEADS * NA_DIM), 1) // NA_DIM
    out = jnp.zeros(na_out.shape, F32)
    for hh in range(NA_HEADS):
        mh = lane == hh
        s = _dot_nt(jnp.where(mh, q, jnp.zeros_like(q)), k)
        p = jnp.exp2(s - jnp.max(s, axis=-1, keepdims=True))
        o = _dot(p.astype(BF16), v) / jnp.sum(p, axis=-1, keepdims=True)
        out = out + jnp.where(mh, o, 0.0)
    na_out[...] = out

    lam = _diff_lambda(lq_ref, lambda_init)
    tq = dqt_ref.shape[2]
    for hh in range(DIFF_HEADS):
        s = _dot(dk_ref[hh], _stack_qt(dqt_ref[hh]))
        p = jnp.exp2(s - jnp.max(s, axis=0, keepdims=True))
        acc = _dot(dvt_ref[hh], p.astype(BF16))
        df_out[hh] = _diff_finish_t(acc, tq, lam, nw_ref[...], lambda_init)


def _ctx_attn(nq, nk, nv, dqt, dk, dvt, lq, norm_w, na_o, df_o, *, batch, seq, ctx, lambda_init):
    blk0 = batch * seq // ctx
    s256 = pl.BlockSpec((ctx, 256), lambda b: (blk0 + b, 0))
    sh64 = pl.BlockSpec((DIFF_HEADS, ctx, 64), lambda b: (0, blk0 + b, 0))
    sq_t = pl.BlockSpec((DIFF_HEADS, 64, ctx), lambda b: (0, 0, blk0 + b))
    sv_t = pl.BlockSpec((DIFF_HEADS, DIFF_VROWS, ctx), lambda b: (0, 0, blk0 + b))
    return pl.pallas_call(
        functools.partial(_ctx_attn_body, lambda_init=lambda_init),
        grid=(batch,),
        in_specs=[s256, s256, s256, sq_t, sh64, sv_t, _resident(lq.shape), _resident((DIFF_DV, 1)),
                  pl.BlockSpec(memory_space=pl.ANY), pl.BlockSpec(memory_space=pl.ANY)],
        out_specs=[s256, sh64],
        out_shape=[jax.ShapeDtypeStruct(na_o.shape, F32), jax.ShapeDtypeStruct(df_o.shape, F32)],
        input_output_aliases={8: 0, 9: 1},
        compiler_params=_params("arbitrary"),
        name="ctx_attn",
    )(nq, nk, nv, dqt, dk, dvt, lq, norm_w.reshape(DIFF_DV, 1), na_o, df_o)


def _conv_body(u_ref, prev_ref, next_ref, dw_ref, dwb_ref, lng_ref, lnb_ref, pw_ref, pwb_ref,
               o_ref, pad_ref, sh_ref, *, tiles_per_seq, n_latent_tiles):
    t = u_ref.shape[0]
    i = pl.program_id(0)
    in_ctx = i >= n_latent_tiles
    first = in_ctx | (i % tiles_per_seq == 0)
    last = in_ctx | (i % tiles_per_seq == tiles_per_seq - 1)
    pad_ref[0:HALO, :] = jnp.where(first, 0.0, prev_ref[...])
    pad_ref[HALO:HALO + t, :] = u_ref[...]
    pad_ref[HALO + t:HALO + t + HALO, :] = jnp.where(last, 0.0, next_ref[...])
    base = HALO - CONV_K // 2
    first = {}
    for phase in range(8):
        taps = [k for k in range(CONV_K) if (base + k) % 8 == phase]
        first[phase] = base + taps[0]
        span = base + taps[-1] + t - first[phase]
        sh_ref[phase, 0:span, :] = pad_ref[first[phase]:first[phase] + span, :]
    sub = 64
    parts = []
    for r0 in range(0, t, sub):
        acc = jnp.zeros((sub, CONV_CH), F32)
        for k in range(CONV_K):
            phase = (base + k) % 8
            off = base + k - first[phase] + r0
            acc = acc + dw_ref[k:k + 1, :] * sh_ref[phase, off:off + sub, :]
        parts.append(acc)
    y = jnp.concatenate(parts, axis=0) + dwb_ref[...]
    mu = jnp.mean(y, axis=-1, keepdims=True)
    yc = y - mu
    var = jnp.mean(yc * yc, axis=-1, keepdims=True)
    y = _silu(yc * lax.rsqrt(var + LN_EPS) * lng_ref[...] + lnb_ref[...])
    o_ref[...] = _dot(y.astype(BF16), pw_ref[...]) + pwb_ref[...]


def _conv(cu, dw, dw_b, ln_g, ln_b, pw, pw_b, *, batch, seq, ctx):
    r = cu.shape[0]
    t = CONV_TILE
    assert ctx == t and seq % t == 0
    n_tiles = r // t
    hb = t // HALO
    vec = lambda a: a.reshape(1, CONV_CH)
    return pl.pallas_call(
        functools.partial(_conv_body, tiles_per_seq=seq // t, n_latent_tiles=batch * seq // t),
        grid=(n_tiles,),
        in_specs=[pl.BlockSpec((t, CONV_CH), lambda i: (i, 0)),
                  pl.BlockSpec((HALO, CONV_CH), lambda i: (jnp.maximum(i * hb - 1, 0), 0)),
                  pl.BlockSpec((HALO, CONV_CH), lambda i: (jnp.minimum((i + 1) * hb, n_tiles * hb - 1), 0)),
                  _resident((CONV_K, CONV_CH))] + [_resident((1, CONV_CH))] * 3
                 + [_resident((CONV_CH, CONV_CH)), _resident((1, CONV_CH))],
        out_specs=pl.BlockSpec((t, CONV_CH), lambda i: (i, 0)),
        out_shape=jax.ShapeDtypeStruct((r, CONV_CH), F32),
        scratch_shapes=[pltpu.VMEM((t + 2 * HALO, CONV_CH), F32),
                        pltpu.VMEM((8, t + 2 * HALO, CONV_CH), F32)],
        compiler_params=_params("parallel"),
        name="conv",
    )(cu, cu, cu, dw, vec(dw_b), vec(ln_g), vec(ln_b), pw.astype(BF16), vec(pw_b))


def kernel(x, c, ctx, c_ctx, ada_w, ada_b, norm_ffn1, ffn1_w13, ffn1_w2, norm_mix, w_in, gla_wa_f, gla_ba_f, gla_wa_b, gla_ba_b, gla_norm, na_rpb, diff_lq1, diff_lk1, diff_lq2, diff_lk2, diff_norm, conv_dw, conv_dw_b, conv_ln_g, conv_ln_b, conv_pw, conv_pw_b, w_out, norm_ffn2, ffn2_w13, ffn2_w2, final_norm):
    batch, seq, d = x.shape
    n_ctx = ctx.shape[1]
    depth = ada_w.shape[0]
    tm = ROW_TILE
    assert seq % tm == 0 and (batch * n_ctx) % tm == 0 and batch + 1 <= 8
    lat_tiles = batch * seq // tm
    all_tiles = lat_tiles + batch * n_ctx // tm
    tiles_per_batch = seq // tm

    def group_of(i):
        return jnp.minimum(i // tiles_per_batch, batch)

    def pos_of(i):
        return jnp.where(i < lat_tiles, i % tiles_per_batch, tiles_per_batch)

    c_rows = jnp.concatenate([c, c_ctx[None, :], jnp.zeros((8 - batch - 1, d), F32)], axis=0)
    mods_all = _ada(c_rows, ada_w, ada_b)[:, :batch + 1].reshape(depth, batch + 1, N_MOD, d)
    rope = _rope_tables(seq, tm)
    h = x.reshape(batch * seq, d)
    h_ctx = ctx.reshape(batch * n_ctx, d)

    for i in range(depth):
        last = i == depth - 1
        lambda_init = 0.8 - 0.6 * math.exp(-0.3 * i)
        mods = mods_all[i]
        tok = dict(n_tiles=all_tiles, group_of=group_of)
        geo = dict(batch=batch, seq=seq, ctx=n_ctx)

        h = _ffn(h, mods, norm_ffn1[i], ffn1_w13[i], ffn1_w2[i], final_norm, k0=0, final=False,
                 h_ctx=h_ctx if i == 0 else None, **tok)

        wa, ba = _gate_weights(gla_wa_f[i], gla_ba_f[i], gla_wa_b[i], gla_ba_b[i])
        (gqk, gv, gvt, gg, glg, nq, nk, nv, dq, dk, dv, cu) = _proj(
            h, mods, norm_mix[i], _permute_w_in(w_in[i]), wa, ba, rope, pos_of=pos_of, **tok)

        gnorm = jnp.tile(gla_norm[i], GLA_HEADS).reshape(1, GLA_HEADS * GLA_DV)
        o_f = _gla(gqk, gv, gvt, glg, gg, None, gnorm, reverse=False, **geo)
        gx = _gla(gqk, gv, gvt, glg, gg, o_f, gnorm, reverse=True, **geo)

        nx = _na(nq, nk, nv, _na_col_tables(na_rpb[i]), **geo)

        lq = jnp.stack([diff_lq1[i], diff_lk1[i], diff_lq2[i], diff_lk2[i]])
        dx = _diff(dq, dk, dv, lq, diff_norm[i], lambda_init=lambda_init, **geo)
        if not last:
            nx, dx = _ctx_attn(nq, nk, nv, dq, dk, dv, lq, diff_norm[i], nx, dx,
                               lambda_init=lambda_init, **geo)

        cx = _conv(cu, conv_dw[i], conv_dw_b[i], conv_ln_g[i], conv_ln_b[i], conv_pw[i], conv_pw_b[i], **geo)

        if last:
            tok = dict(n_tiles=lat_tiles, group_of=group_of)
        h = _ffn(h, mods, norm_ffn2[i], ffn2_w13[i], ffn2_w2[i], final_norm, k0=6, final=last,
                 mixers=(gx, nx, dx, cx, w_out[i], (batch, seq, n_ctx)), **tok)

    return h.reshape(batch, seq, d)
```

```python
import functools
import math

import jax
import jax.numpy as jnp
from jax import lax
from jax.experimental import pallas as pl
from jax.experimental.pallas import tpu as pltpu

F32 = jnp.float32
BF16 = jnp.bfloat16

GRID_W = 64
N_MOD = 9
RMS_EPS = 1e-6
LN_EPS = 1e-5
NEG_INF = -1e30
GLA_HEADS, GLA_DK, GLA_DV, GLA_RANK, GLA_TAU, GLA_CHUNK = 4, 32, 64, 16, 16.0, 64
NA_HEADS, NA_DIM, NA_ROWS, NA_COLS = 4, 64, 8, 16
DIFF_HEADS, DIFF_DK, DIFF_DV = 4, 32, 64
DIFF_VROWS = 128
CONV_CH, CONV_K = 256, 31
ROPE_BASE = 10000.0
LOG2E = 1.4426950408889634

LANES = 128
VMEM_LIMIT = 56 * 1024 * 1024

ROW_TILE = 512
FF_CHUNK = 256
GLA_BLOCK = 256
NA_QROWS = 8
NA_KBLK = 256
CONV_TILE = 256
DIFF_TQ = 256
DIFF_QTILES = 8
DIFF_TK = 256
DIFF_UNROLL = 32
DIFF_AHEAD = 3
DIFF_RING = 4
HALO = 16

C_GLA, C_NA, C_DIFF, C_CONV, C_AUX, C_END = 0, 768, 1536, 2304, 2816, 2944


def _dot(a, b):
    return jnp.dot(a, b, preferred_element_type=F32)


def _dot_nt(a, b):
    return lax.dot_general(a, b, (((1,), (1,)), ((), ())), preferred_element_type=F32)


def _params(*sem):
    return pltpu.CompilerParams(dimension_semantics=sem, vmem_limit_bytes=VMEM_LIMIT)


def _resident(shape):
    nd = len(shape)
    return pl.BlockSpec(shape, lambda *_: (0,) * nd, pipeline_mode=pl.Buffered(1))


def _silu(x):
    return x * jax.nn.sigmoid(x)


def _rms(x, w):
    return x * lax.rsqrt(jnp.mean(x * x, axis=-1, keepdims=True) + RMS_EPS) * w


def _ada_body(c_ref, w_ref, b_ref, o_ref):
    s = _silu(c_ref[...])
    o_ref[...] = jnp.dot(s, w_ref[...], precision=lax.Precision.HIGHEST,
                         preferred_element_type=F32) + b_ref[...]


def _ada(c_rows, ada_w, ada_b):
    depth, d, _ = ada_w.shape
    return pl.pallas_call(
        _ada_body,
        grid=(depth, N_MOD),
        in_specs=[pl.BlockSpec((8, d), lambda l, n: (0, 0)),
                  pl.BlockSpec((None, d, d), lambda l, n: (l, 0, n)),
                  pl.BlockSpec((None, 1, d), lambda l, n: (l, 0, n))],
        out_specs=pl.BlockSpec((None, 8, d), lambda l, n: (l, 0, n)),
        out_shape=jax.ShapeDtypeStruct((depth, 8, N_MOD * d), F32),
        compiler_params=_params("arbitrary", "arbitrary"),
        name="ada",
    )(c_rows, ada_w, ada_b.reshape(depth, 1, N_MOD * d))


def _ffn_body(*refs, k0, final, source, n_first):
    h_ref = refs[0]
    n_extra = {"plain": 0, "two_arrays": 1, "mixers": 6}[source]
    extra = refs[1:1 + n_extra]
    mod_ref, nw_ref, w13_ref, w2_ref, fw_ref, o_ref, xb_ref, g_ref, h13_ref = refs[1 + n_extra:]
    ff = w2_ref.shape[0]
    tf = h13_ref.shape[2] // 2
    n_chunks = ff // tf
    x = h_ref[...]
    if source == "two_arrays":
        x = jnp.where(pl.program_id(0) < n_first, x, extra[0][...])
    elif source == "mixers":
        gx_ref, gxc_ref, nx_ref, dx_ref, cx_ref, wo_ref = extra
        gx = jnp.where(pl.program_id(0) < n_first, gx_ref[...], gxc_ref[...].reshape(gx_ref.shape))
        mix = jnp.concatenate([gx, nx_ref[...]] + [dx_ref[hh] for hh in range(DIFF_HEADS)]
                              + [cx_ref[...]], axis=-1).astype(BF16)
        x = x + mod_ref[5:6, :] * _dot(mix, wo_ref[...])
    xm = _rms(x, nw_ref[...]) * (1.0 + mod_ref[k0 + 1:k0 + 2, :]) + mod_ref[k0:k0 + 1, :]
    xb_ref[...] = xm.astype(BF16)

    def cols(c):
        return c * tf if isinstance(c, int) else pl.multiple_of(c * tf, tf)

    def up(c, slot):
        h13_ref[slot, :, :tf] = _dot(xb_ref[...], w13_ref[:, pl.ds(cols(c), tf)])
        h13_ref[slot, :, tf:] = _dot(xb_ref[...], w13_ref[:, pl.ds(ff + cols(c), tf)])

    def gate(c, slot):
        a = h13_ref[slot, :, :tf]
        u = h13_ref[slot, :, tf:]
        g_ref[:, pl.ds(cols(c), tf)] = (_silu(a) * u).astype(BF16)

    def pair(t, carry):
        up(2 * t + 1, 1)
        gate(2 * t, 0)
        up(2 * t + 2, 0)
        gate(2 * t + 1, 1)
        return carry

    up(0, 0)
    for t in range((n_chunks - 1) // 2):
        pair(t, 0)
    gate(n_chunks - 1, 0)
    out = x + (0.5 * mod_ref[k0 + 2:k0 + 3, :]) * _dot(g_ref[...], w2_ref[...])
    if final:
        out = _rms(out, fw_ref[...])
    o_ref[...] = out


def _ffn(h, mods, norm_w, w13, w2, final_w, *, k0, n_tiles, group_of, final, h_ctx=None, mixers=None):
    d = h.shape[1]
    tm = ROW_TILE
    ff = w2.shape[0]
    n_chunks = ff // FF_CHUNK
    assert ff % FF_CHUNK == 0 and n_chunks % 2 == 1
    row = lambda i: (i, 0)
    source, n_first, extra, extra_specs, h_spec = "plain", None, [], [], pl.BlockSpec((tm, d), row)
    if h_ctx is not None:
        source, n_first, extra = "two_arrays", h.shape[0] // tm, [h_ctx]
        h_spec = pl.BlockSpec((tm, d), lambda i: (jnp.minimum(i, n_first - 1), 0))
        extra_specs = [pl.BlockSpec((tm, d), lambda i: (jnp.maximum(i - n_first, 0), 0))]
    elif mixers is not None:
        gx, nx, dx, cx, w_out, (batch, seq, n_ctx) = mixers
        assert batch * n_ctx == tm and seq % tm == 0 and seq % n_ctx == 0
        source, n_first, extra = "mixers", batch * seq // tm, [gx, gx, nx, dx, cx, w_out.astype(BF16)]
        per_seq = seq // tm
        extra_specs = [pl.BlockSpec((None, tm, 256),
                                    lambda i: (jnp.minimum(i // per_seq, batch - 1), i % per_seq, 0)),
                       pl.BlockSpec((batch, n_ctx, 256), lambda i: (0, seq // n_ctx, 0)),
                       pl.BlockSpec((tm, 256), row),
                       pl.BlockSpec((DIFF_HEADS, tm, 64), lambda i: (0, i, 0)),
                       pl.BlockSpec((tm, 256), row), _resident(w_out.shape)]
    return pl.pallas_call(
        functools.partial(_ffn_body, k0=k0, final=final, source=source, n_first=n_first),
        grid=(n_tiles,),
        in_specs=[h_spec] + extra_specs + [
            pl.BlockSpec((None, N_MOD, d), lambda i: (group_of(i), 0, 0)),
            _resident((1, d)),
            _resident(w13.shape),
            _resident(w2.shape),
            _resident((1, d))],
        out_specs=pl.BlockSpec((tm, d), row),
        out_shape=jax.ShapeDtypeStruct((n_tiles * tm, d), F32),
        scratch_shapes=[pltpu.VMEM((tm, d), BF16), pltpu.VMEM((tm, ff), BF16),
                        pltpu.VMEM((2, tm, 2 * FF_CHUNK), F32)],
        compiler_params=_params("parallel"),
        name="ffn",
    )(h, *extra, mods, norm_w.reshape(1, d), w13.astype(BF16), w2.astype(BF16), final_w.reshape(1, d))


def _log_sigmoid(x):
    return jnp.minimum(x, 0.0) - jnp.log1p(jnp.exp(-jnp.abs(x)))


def _rope_rotate(x):
    n = x.shape[-1]
    lane = lax.broadcasted_iota(jnp.int32, x.shape, 1)
    up = pltpu.roll(x, n - 8, 1)
    dn = pltpu.roll(x, 8, 1)
    return jnp.where((lane & 15) < 8, -up, dn)


def _rope_rotate_rows(x):
    n = x.shape[0]
    row = lax.broadcasted_iota(jnp.int32, x.shape, 0)
    up = pltpu.roll(x, n - 8, 0)
    dn = pltpu.roll(x, 8, 0)
    return jnp.where((row & 15) < 8, -up, dn)


def _proj_body(h_ref, mod_ref, nw_ref, w_ref, wvt_ref, wdqt_ref, wdvt_ref, wa_ref, ba_ref,
               cos_ref, sin_ref, cost_ref, sint_ref,
               gqk_ref, gv_ref, gvt_ref, gg_ref, glg_ref, nq_ref, nk_ref, nv_ref,
               dqt_ref, dk_ref, dvt_ref, cu_ref, xb_ref):
    x = h_ref[...]
    tm = x.shape[0]
    xm = _rms(x, nw_ref[...]) * (1.0 + mod_ref[4:5, :]) + mod_ref[3:4, :]
    xb_ref[...] = xm.astype(BF16)

    z = _dot(xb_ref[...], w_ref[:, C_GLA:C_GLA + 256])
    lane = lax.broadcasted_iota(jnp.int32, (1, 256), 1)
    gqk_ref[...] = z * jnp.where(lane < 128, GLA_DK ** -0.5, 1.0)
    gv_ref[...] = _dot(xb_ref[...], w_ref[:, C_GLA + 256:C_GLA + 512])
    gvt_ref[...] = _dot_nt(wvt_ref[...], xb_ref[...])
    gg_ref[...] = _dot(xb_ref[...], w_ref[:, C_GLA + 512:C_GLA + 768])
    aux = _dot(xb_ref[...], w_ref[:, C_AUX:C_END])
    pre = _dot(aux.astype(BF16), wa_ref[...]) + ba_ref[...]
    glg_ref[...] = _log_sigmoid(pre) * (1.0 / GLA_TAU)

    nq_ref[...] = (_dot(xb_ref[...], w_ref[:, C_NA:C_NA + 256]) * (NA_DIM ** -0.5 * LOG2E)).astype(BF16)
    nk_ref[...] = _dot(xb_ref[...], w_ref[:, C_NA + 256:C_NA + 512]).astype(BF16)
    nv_ref[...] = _dot(xb_ref[...], w_ref[:, C_NA + 512:C_NA + 768]).astype(BF16)

    cos = cos_ref[...]
    sin = sin_ref[...]
    cos2 = jnp.concatenate([cos, cos], axis=1)
    sin2 = jnp.concatenate([sin, sin], axis=1)
    zk = _dot(xb_ref[...], w_ref[:, C_DIFF + 256:C_DIFF + 512])
    zk = zk * cos2 + _rope_rotate(zk) * sin2
    n_rep = 2 * DIFF_HEADS
    cos_t = jnp.concatenate([cost_ref[...]] * n_rep, axis=0)
    sin_t = jnp.concatenate([sint_ref[...]] * n_rep, axis=0)
    zqt = _dot_nt(wdqt_ref[...], xb_ref[...])
    zqt = (zqt * cos_t + _rope_rotate_rows(zqt) * sin_t) * (DIFF_DK ** -0.5 * LOG2E)
    zvt = _dot_nt(wdvt_ref[...], xb_ref[...])
    pad_rows = DIFF_VROWS - DIFF_DV
    one_row = jnp.where(lax.broadcasted_iota(jnp.int32, (pad_rows, tm), 0) == 0, 1.0, 0.0).astype(BF16)
    for hh in range(DIFF_HEADS):
        sl = slice(64 * hh, 64 * hh + 64)
        dk_ref[hh] = zk[:, sl].astype(BF16)
        dqt_ref[hh] = zqt[sl, :].astype(BF16)
        dvt_ref[hh, 0:64, :] = zvt[sl, :].astype(BF16)
        dvt_ref[hh, DIFF_DV:DIFF_VROWS, :] = one_row

    za = _dot(xb_ref[...], w_ref[:, C_CONV:C_CONV + 256])
    zg = _dot(xb_ref[...], w_ref[:, C_CONV + 256:C_CONV + 512])
    cu_ref[...] = za * jax.nn.sigmoid(zg)


def _proj(h, mods, norm_w, w_p, wa, ba, rope, *, n_tiles, group_of, pos_of):
    r, d = h.shape
    tm = ROW_TILE
    row = lambda i: (i, 0)
    hrow = lambda i: (0, i, 0)
    hcol = lambda i: (0, 0, i)
    f32_256 = jax.ShapeDtypeStruct((r, 256), F32)
    bf_256 = jax.ShapeDtypeStruct((r, 256), BF16)
    wvt = w_p[:, C_GLA + 256:C_GLA + 512].T
    wdqt = w_p[:, C_DIFF:C_DIFF + 256].T
    wdvt = w_p[:, C_DIFF + 512:C_DIFF + 768].T
    cos_r, sin_r, cos_c, sin_c = rope
    out_shape = [f32_256, f32_256, jax.ShapeDtypeStruct((256, r), F32), f32_256, f32_256,
                 bf_256, bf_256, bf_256,
                 jax.ShapeDtypeStruct((DIFF_HEADS, 64, r), BF16),
                 jax.ShapeDtypeStruct((DIFF_HEADS, r, 64), BF16),
                 jax.ShapeDtypeStruct((DIFF_HEADS, DIFF_VROWS, r), BF16),
                 f32_256]
    out_specs = [pl.BlockSpec((tm, 256), row)] * 2 + [pl.BlockSpec((256, tm), lambda i: (0, i))] + [
        pl.BlockSpec((tm, 256), row)] * 5 + [
        pl.BlockSpec((DIFF_HEADS, 64, tm), hcol),
        pl.BlockSpec((DIFF_HEADS, tm, 64), hrow),
        pl.BlockSpec((DIFF_HEADS, DIFF_VROWS, tm), hcol),
        pl.BlockSpec((tm, 256), row)]
    return pl.pallas_call(
        _proj_body,
        grid=(n_tiles,),
        in_specs=[pl.BlockSpec((tm, d), row),
                  pl.BlockSpec((None, N_MOD, d), lambda i: (group_of(i), 0, 0)),
                  _resident((1, d)),
                  _resident(w_p.shape),
                  _resident(wvt.shape),
                  _resident(wdqt.shape),
                  _resident(wdvt.shape),
                  _resident(wa.shape),
                  _resident(ba.shape),
                  pl.BlockSpec((tm, LANES), lambda i: (pos_of(i), 0)),
                  pl.BlockSpec((tm, LANES), lambda i: (pos_of(i), 0)),
                  pl.BlockSpec((DIFF_DK, tm), lambda i: (0, pos_of(i))),
                  pl.BlockSpec((DIFF_DK, tm), lambda i: (0, pos_of(i)))],
        out_specs=out_specs,
        out_shape=out_shape,
        scratch_shapes=[pltpu.VMEM((tm, d), BF16)],
        compiler_params=_params("parallel"),
        name="proj",
    )(h, mods, norm_w.reshape(1, d), w_p, wvt, wdqt, wdvt, wa, ba, cos_r, sin_r, cos_c, sin_c)


def _permute_w_in(w_in):
    d = w_in.shape[0]
    g0 = 2 * GLA_HEADS * GLA_DK + 2 * GLA_HEADS * GLA_DV
    aux = w_in[:, g0:g0 + 2 * GLA_RANK]
    rest = w_in[:, g0 + 2 * GLA_RANK:]
    pad = jnp.zeros((d, C_END - C_AUX - 2 * GLA_RANK), w_in.dtype)
    return jnp.concatenate([w_in[:, :g0], rest, aux, pad], axis=1).astype(BF16)


def _gate_weights(wa_f, ba_f, wa_b, ba_b):
    n = GLA_HEADS * GLA_DK
    wa = jnp.zeros((C_END - C_AUX, 2 * n), F32)
    wa = wa.at[:GLA_RANK, :n].set(wa_f).at[GLA_RANK:2 * GLA_RANK, n:].set(wa_b)
    return wa.astype(BF16), jnp.concatenate([ba_f, ba_b]).reshape(1, 2 * n)


def _rope_tables(seq, tile):
    t = jnp.arange(seq)
    row = (t // GRID_W).astype(F32)
    col = (t % GRID_W).astype(F32)
    half = DIFF_DK // 2
    inv = 1.0 / (ROPE_BASE ** (jnp.arange(0, half, 2, dtype=F32) / half))
    ang_r = row[:, None] * inv
    ang_c = col[:, None] * inv
    ang = jnp.concatenate([ang_r, ang_r, ang_c, ang_c], axis=-1)
    cos = jnp.concatenate([jnp.cos(ang), jnp.ones((tile, DIFF_DK), F32)], axis=0)
    sin = jnp.concatenate([jnp.sin(ang), jnp.zeros((tile, DIFF_DK), F32)], axis=0)
    rep = (1, LANES // DIFF_DK)
    return jnp.tile(cos, rep), jnp.tile(sin, rep), cos.T, sin.T


def _split3(x):
    hi = x.astype(BF16)
    r1 = x - hi.astype(F32)
    mid = r1.astype(BF16)
    lo = (r1 - mid.astype(F32)).astype(BF16)
    return hi, mid, lo


def _gla_body(*refs, reverse, batch):
    per_b = [refs[5 * b:5 * b + 5] for b in range(batch)]
    of_ref, nw_ref, o_ref, st_ref = refs[5 * batch:]
    blk = per_b[0][0].shape[0]
    n_chunks = blk // GLA_CHUNK
    nqk = GLA_HEADS * GLA_DK
    nv = GLA_HEADS * GLA_DV

    @pl.when(pl.program_id(0) == 0)
    def _():
        st_ref[...] = jnp.zeros_like(st_ref)

    ri = lax.broadcasted_iota(jnp.int32, (blk, blk), 0)
    ci = lax.broadcasted_iota(jnp.int32, (blk, blk), 1)
    same = (ri // GLA_CHUNK) == (ci // GLA_CHUNK)
    tri = jnp.where(same & ((ci >= ri) if reverse else (ci <= ri)), 1.0, 0.0).astype(BF16)
    bcum_of = []
    for qk_ref, v_ref, vt_ref, lg_ref, g_ref in per_b:
        lg = lg_ref[:, nqk:2 * nqk] if reverse else lg_ref[:, 0:nqk]
        hi, mid, lo = _split3(lg)
        bcum_of.append(_dot(tri, hi) + _dot(tri, mid) + _dot(tri, lo))

    c = GLA_CHUNK
    rk = lax.broadcasted_iota(jnp.int32, (GLA_HEADS * c, nqk), 0) // c
    ck = lax.broadcasted_iota(jnp.int32, (GLA_HEADS * c, nqk), 1) // GLA_DK
    mask_k = rk == ck
    rv = lax.broadcasted_iota(jnp.int32, (GLA_HEADS * c, nv), 0) // c
    cv = lax.broadcasted_iota(jnp.int32, (GLA_HEADS * c, nv), 1) // GLA_DV
    mask_v = rv == cv
    rs = lax.broadcasted_iota(jnp.int32, (nv, nqk), 0) // GLA_DV
    cs = lax.broadcasted_iota(jnp.int32, (nv, nqk), 1) // GLA_DK
    mask_s = rs == cs
    ai = lax.broadcasted_iota(jnp.int32, (c, GLA_HEADS * c), 0)
    aj = lax.broadcasted_iota(jnp.int32, (c, GLA_HEADS * c), 1) % c
    mask_a = (aj >= ai) if reverse else (aj <= ai)

    order = range(n_chunks - 1, -1, -1) if reverse else range(n_chunks)
    units = [(b, ch) for ch in order for b in range(batch)]
    q_in, a_raw, u_t, decay = {}, {}, {}, {}
    vts = [refs_b[2][...].astype(BF16) for refs_b in per_b]
    for b, ch in units:
        qk_ref = per_b[b][0]
        rows = slice(ch * c, (ch + 1) * c)
        k = qk_ref[rows, nqk:2 * nqk]
        bcum = bcum_of[b][rows, :]
        btot = bcum[0:1, :] if reverse else bcum[c - 1:c, :]
        q_in[b, ch] = (qk_ref[rows, 0:nqk] * jnp.exp(bcum)).astype(BF16)
        k_in = k * jnp.exp(-bcum)
        k_out = (k * jnp.exp(btot - bcum)).astype(BF16)
        k_bd = jnp.where(mask_k, jnp.concatenate([k_in] * GLA_HEADS, axis=0), 0.0).astype(BF16)
        a_raw[b, ch] = _dot_nt(q_in[b, ch], k_bd)
        k_pad = jnp.concatenate([jnp.zeros((n * c, nqk), BF16) for n in (ch,) if n] + [k_out]
                                + [jnp.zeros((n * c, nqk), BF16) for n in (n_chunks - 1 - ch,) if n], axis=0)
        u_t[b, ch] = jnp.where(mask_s, _dot(vts[b], k_pad), 0.0)
        decay[b, ch] = jnp.exp(btot)
    o = {}
    for b, ch in units:
        v = per_b[b][1][ch * c:(ch + 1) * c, :]
        v_bd = jnp.where(mask_v, jnp.concatenate([v] * GLA_HEADS, axis=0), 0.0).astype(BF16)
        o[b, ch] = _dot(jnp.where(mask_a, a_raw[b, ch], 0.0).astype(BF16), v_bd)
    st = [st_ref[b] for b in range(batch)]
    for b, ch in units:
        o[b, ch] = o[b, ch] + _dot_nt(q_in[b, ch], st[b].astype(BF16))
        st[b] = st[b] * decay[b, ch] + u_t[b, ch]
    for b in range(batch):
        st_ref[b] = st[b]
        o_blk = jnp.concatenate([o[b, ch] for ch in range(n_chunks)], axis=0)
        if reverse:
            o_blk = o_blk + of_ref[b]
            hi2, lo2, _ = _split3(o_blk * o_blk)
            hr = lax.broadcasted_iota(jnp.int32, (nv, nv), 0) // GLA_DV
            hc = lax.broadcasted_iota(jnp.int32, (nv, nv), 1) // GLA_DV
            seg = jnp.where(hr == hc, 1.0, 0.0).astype(BF16)
            ms = (_dot(hi2, seg) + _dot(lo2, seg)) * (1.0 / GLA_DV)
            o_blk = o_blk * lax.rsqrt(ms + RMS_EPS) * nw_ref[...] * _silu(per_b[b][4][...])
        o_ref[b] = o_blk


def _gla(gqk, gv, gvt, glg, gg, o_f, norm_w4, *, reverse, batch, seq, ctx):
    blk = GLA_BLOCK
    nc, nl = ctx // blk, seq // blk
    ctx_base = batch * seq // blk

    def step_blk(s):
        if reverse:
            return jnp.where(s < nc, nl + (nc - 1 - s), nl - 1 - (s - nc))
        return jnp.where(s < nc, nl + s, s - nc)

    def row_blk(b):
        return lambda s: jnp.where(s < nc, ctx_base + b * nc - nl, b * nl) + step_blk(s)

    specs, args = [], []
    for b in range(batch):
        spec = pl.BlockSpec((blk, 256), lambda s, f=row_blk(b): (f(s), 0))
        spec_t = pl.BlockSpec((256, blk), lambda s, f=row_blk(b): (0, f(s)))
        specs += [spec, spec, spec_t, spec, spec]
        args += [gqk, gv, gvt, glg, gg]
    seq_spec = pl.BlockSpec((batch, blk, 256), lambda s: (0, step_blk(s), 0))
    if o_f is None:
        o_f, of_spec = norm_w4, _resident((1, 256))
    else:
        of_spec = seq_spec
    return pl.pallas_call(
        functools.partial(_gla_body, reverse=reverse, batch=batch),
        grid=(nc + nl,),
        in_specs=specs + [of_spec, _resident((1, 256))],
        out_specs=seq_spec,
        out_shape=jax.ShapeDtypeStruct((batch, seq + ctx, 256), F32),
        scratch_shapes=[pltpu.VMEM((batch, GLA_HEADS * GLA_DV, GLA_HEADS * GLA_DK), F32)],
        compiler_params=_params("arbitrary"),
        name="gla_bwd" if reverse else "gla_fwd",
    )(*args, o_f, norm_w4)


def _na_col_tables(rpb):
    col = jnp.arange(GRID_W)
    cstart = jnp.clip(col - NA_COLS // 2, 0, GRID_W - NA_COLS)
    col_ok = (col[None, :] >= cstart[:, None]) & (col[None, :] < cstart[:, None] + NA_COLS)
    col_off = jnp.clip(col[None, :] - col[:, None] + (NA_COLS - 1), 0, 2 * NA_COLS - 2)
    t = jnp.where(col_ok[None, None], rpb[:, :, col_off] * LOG2E, NEG_INF)
    dead = jnp.full((NA_HEADS, 1, GRID_W, GRID_W), NEG_INF, F32)
    t = jnp.concatenate([dead, t.astype(F32), dead], axis=1)
    return jnp.concatenate([t[:, :-1], t[:, 1:]], axis=-1)


def _na_body(q_ref, k0, k1, k2, k3, v0, v1, v2, v3, kc_ref, vc_ref, tab_ref, o_ref, *, rows, key_blocks):
    k_blk = [k0[...], k1[...], k2[...], k3[...]]
    v_blk = [v0[...], v1[...], v2[...], v3[...]]
    kc = kc_ref[...]
    vc = vc_ref[...]

    j = pl.program_id(1)
    rows_per_blk = NA_KBLK // GRID_W
    r0 = j * NA_QROWS
    kr0 = jnp.clip(2 * j - 1, 0, key_blocks - 4) * rows_per_blk
    lane_lo = lax.broadcasted_iota(jnp.int32, (1, 2 * GRID_W), 1) < GRID_W
    lane = lax.broadcasted_iota(jnp.int32, (1, NA_HEADS * NA_DIM), 1) // NA_DIM
    half = NA_QROWS // 2
    n_pairs = 3 * rows_per_blk // 2
    gq = half * GRID_W
    for g in range(2):
        first = jnp.clip(r0 + g * half - NA_ROWS // 2, 0, rows - NA_ROWS)
        shift = jnp.clip((first - kr0) // rows_per_blk, 0, 1)
        kw = jnp.concatenate([jnp.where(shift == 0, k_blk[i], k_blk[i + 1]) for i in range(3)], axis=0)
        vw = jnp.concatenate([jnp.where(shift == 0, v_blk[i], v_blk[i + 1]) for i in range(3)], axis=0)
        kg0 = kr0 + shift * rows_per_blk
        entry, ok = [], []
        for a in range(half):
            r = r0 + g * half + a
            start = jnp.clip(r - NA_ROWS // 2, 0, rows - NA_ROWS)
            for bp in range(n_pairs):
                rk = kg0 + 2 * bp
                entry.append(jnp.clip(rk - r + NA_ROWS, 0, 2 * NA_ROWS - 1))
                in0 = ((rk >= start) & (rk < start + NA_ROWS)).astype(jnp.int32)
                in1 = ((rk + 1 >= start) & (rk + 1 < start + NA_ROWS)).astype(jnp.int32)
                ok.append(jnp.where(lane_lo, in0, in1) != 0)

        def bias_of(hh):
            rows_ = []
            for a in range(half):
                tiles = [jnp.where(ok[a * n_pairs + bp], tab_ref[hh, entry[a * n_pairs + bp]], NEG_INF)
                         for bp in range(n_pairs)]
                rows_.append(jnp.concatenate(tiles, axis=1))
            return jnp.concatenate(rows_, axis=0)

        q = q_ref[g * gq:(g + 1) * gq, :]
        out = jnp.zeros((gq, NA_HEADS * NA_DIM), F32)
        for hh in range(NA_HEADS):
            mh = lane == hh
            qh = jnp.where(mh, q, jnp.zeros_like(q))
            s_w = _dot_nt(qh, kw) + bias_of(hh)
            s_c = _dot_nt(qh, kc)
            m = jnp.maximum(jnp.max(s_w, axis=-1, keepdims=True), jnp.max(s_c, axis=-1, keepdims=True))
            p_w = jnp.exp2(s_w - m)
            p_c = jnp.exp2(s_c - m)
            l = jnp.sum(p_w, axis=-1, keepdims=True) + jnp.sum(p_c, axis=-1, keepdims=True)
            o = _dot(p_w.astype(BF16), vw) + _dot(p_c.astype(BF16), vc)
            out = out + jnp.where(mh, o / l, 0.0)
        o_ref[g * gq:(g + 1) * gq, :] = out


def _na(nq, nk, nv, tab, *, batch, seq, ctx):
    r = nq.shape[0]
    tq = NA_QROWS * GRID_W
    nj = seq // tq
    kb = seq // NA_KBLK
    assert kb >= 4 and seq % tq == 0 and seq % ctx == 0
    ctx_blk0 = batch * seq // ctx

    def kspec(i):
        return pl.BlockSpec((NA_KBLK, 256),
                            lambda b, j: (b * kb + jnp.clip(2 * j - 1, 0, kb - 4) + i, 0))

    cspec = pl.BlockSpec((ctx, 256), lambda b, j: (ctx_blk0 + b, 0))
    return pl.pallas_call(
        functools.partial(_na_body, rows=seq // GRID_W, key_blocks=kb),
        grid=(batch, nj),
        in_specs=[pl.BlockSpec((tq, 256), lambda b, j: (b * nj + j, 0))]
                 + [kspec(i) for i in range(4)] + [kspec(i) for i in range(4)]
                 + [cspec, cspec, _resident(tab.shape)],
        out_specs=pl.BlockSpec((tq, 256), lambda b, j: (b * nj + j, 0)),
        out_shape=jax.ShapeDtypeStruct((r, 256), F32),
        compiler_params=_params("parallel", "arbitrary"),
        name="na",
    )(nq, nk, nk, nk, nk, nv, nv, nv, nv, nk, nv, tab)


def _diff_lambda(lq_ref, lambda_init):
    lq = lq_ref[...]
    s1 = jnp.sum(lq[0:1, :] * lq[1:2, :], axis=-1, keepdims=True)
    s2 = jnp.sum(lq[2:3, :] * lq[3:4, :], axis=-1, keepdims=True)
    return jnp.exp(s1) - jnp.exp(s2) + lambda_init


def _stack_qt(qt):
    row = lax.broadcasted_iota(jnp.int32, (2 * DIFF_DK, 1), 0)
    zero = jnp.zeros_like(qt)
    return jnp.concatenate([jnp.where(row < DIFF_DK, qt, zero), jnp.where(row >= DIFF_DK, qt, zero)], axis=1)


def _diff_finish_t(acc, tq, lam, nw_col, lambda_init):
    o = acc[0:DIFF_DV, :] / acc[DIFF_DV:DIFF_DV + 1, :]
    od = o[:, :tq] - lam * o[:, tq:]
    ms = jnp.mean(od * od, axis=0, keepdims=True)
    y = od * lax.rsqrt(ms + RMS_EPS) * nw_col * (1.0 - lambda_init)
    y = jnp.concatenate([y, jnp.zeros_like(y)], axis=0)
    return jnp.transpose(y)[:, 0:DIFF_DV]


def _diff_body(qt_ref, kc_ref, vtc_ref, k_ref, vt_ref, lq_ref, nw_ref, o_ref, s_ref, *, tq, tk, unroll, lambda_init):
    def tile(t, carry):
        q0 = pl.multiple_of(t * tq, tq)
        out = _diff_tile(qt_ref[:, pl.ds(q0, tq)], kc_ref, vtc_ref, k_ref, vt_ref, lq_ref, nw_ref, s_ref,
                         tk=tk, unroll=unroll, lambda_init=lambda_init)
        o_ref[pl.ds(q0, tq), :] = out
        return carry

    lax.fori_loop(0, qt_ref.shape[1] // tq, tile, 0)


def _diff_tile(qt, kc_ref, vtc_ref, k_ref, vt_ref, lq_ref, nw_ref, s_ref, *, tk, unroll, lambda_init):
    tq = qt.shape[1]
    n_chunks = k_ref.shape[0] // tk
    ring = s_ref.shape[0]
    qs = _stack_qt(qt)

    def chunk(j):
        return pl.ds(j * tk if isinstance(j, int) else pl.multiple_of(j * tk, tk), tk)

    def issue_scores(j, slot):
        s = _dot(k_ref[chunk(j), :], qs)
        s_ref[slot] = s
        return jnp.max(s, axis=0, keepdims=True)

    s_ctx = _dot(kc_ref[...], qs)
    m0 = jnp.max(s_ctx, axis=0, keepdims=True)
    cmax0 = tuple(issue_scores(min(a, n_chunks - 1), a % ring) for a in range(DIFF_AHEAD))
    acc0 = _dot(vtc_ref[...], jnp.exp2(s_ctx - m0).astype(BF16))

    def group(g, carry, tail=False):
        cmax, m, acc = carry
        for u in range(unroll):
            j = g * unroll + u
            if tail and j + DIFF_AHEAD >= n_chunks:
                c_new = cmax[0]
            else:
                c_new = issue_scores(j + DIFF_AHEAD, (u + DIFF_AHEAD) % ring)
            m_new = jnp.maximum(m, cmax[0])
            p = jnp.exp2(s_ref[u % ring] - m_new).astype(BF16)
            vt = vt_ref[:, chunk(j)]
            acc = jnp.exp2(m - m_new) * acc + _dot(vt, p)
            m, cmax = m_new, cmax[1:] + (c_new,)
        return cmax, m, acc

    n_groups = n_chunks // unroll
    trips = jnp.minimum(pl.program_id(2) + 1, 1) * (n_groups - 1)
    carry = lax.fori_loop(0, trips, group, (cmax0, m0, acc0))
    _, _, acc = group(n_groups - 1, carry, tail=True)
    return _diff_finish_t(acc, tq, _diff_lambda(lq_ref, lambda_init), nw_ref[...], lambda_init)


def _diff(dqt, dk, dvt, lq, norm_w, *, batch, seq, ctx, lambda_init):
    h, r, _ = dk.shape
    tq = DIFF_TQ
    tqs = tq * min(DIFF_QTILES, seq // tq)
    nq = seq // tqs
    tk = min(DIFF_TK, seq)
    unroll = min(DIFF_UNROLL, seq // tk)
    assert (seq // tk) % unroll == 0 and unroll % DIFF_RING == 0 and DIFF_RING > DIFF_AHEAD and seq % tqs == 0
    ctx_blk0 = batch * seq // ctx
    return pl.pallas_call(
        functools.partial(_diff_body, tq=tq, tk=tk, unroll=unroll, lambda_init=lambda_init),
        grid=(batch, h, nq),
        in_specs=[pl.BlockSpec((None, 64, tqs), lambda b, hh, i: (hh, 0, b * nq + i)),
                  pl.BlockSpec((None, ctx, 64), lambda b, hh, i: (hh, ctx_blk0 + b, 0)),
                  pl.BlockSpec((None, DIFF_VROWS, ctx), lambda b, hh, i: (hh, 0, ctx_blk0 + b)),
                  pl.BlockSpec((None, seq, 64), lambda b, hh, i: (hh, b, 0)),
                  pl.BlockSpec((None, DIFF_VROWS, seq), lambda b, hh, i: (hh, 0, b)),
                  _resident(lq.shape),
                  _resident((DIFF_DV, 1))],
        out_specs=pl.BlockSpec((None, tqs, 64), lambda b, hh, i: (hh, b * nq + i, 0)),
        out_shape=jax.ShapeDtypeStruct((h, r, 64), F32),
        scratch_shapes=[pltpu.VMEM((DIFF_RING, tk, 2 * tq), F32)],
        compiler_params=_params("parallel", "parallel", "arbitrary"),
        name="diff",
    )(dqt, dk, dvt, dk, dvt, lq, norm_w.reshape(DIFF_DV, 1))


def _ctx_attn_body(nq_ref, nk_ref, nv_ref, dqt_ref, dk_ref, dvt_ref, lq_ref, nw_ref, na_in, df_in,
                   na_out, df_out, *, lambda_init):
    del na_in, df_in
    q = nq_ref[...]
    k = nk_ref[...]
    v = nv_ref[...]
    lane = lax.broadcasted_iota(jnp.int32, (1, NA_HEADS * NA_DIM), 1) // NA_DIM
    out = jnp.zeros(na_out.shape, F32)
    for hh in range(NA_HEADS):
        mh = lane == hh
        s = _dot_nt(jnp.where(mh, q, jnp.zeros_like(q)), k)
        p = jnp.exp2(s - jnp.max(s, axis=-1, keepdims=True))
        o = _dot(p.astype(BF16), v) / jnp.sum(p, axis=-1, keepdims=True)
        out = out + jnp.where(mh, o, 0.0)
    na_out[...] = out

    lam = _diff_lambda(lq_ref, lambda_init)
    tq = dqt_ref.shape[2]
    for hh in range(DIFF_HEADS):
        s = _dot(dk_ref[hh], _stack_qt(dqt_ref[hh]))
        p = jnp.exp2(s - jnp.max(s, axis=0, keepdims=True))
        acc = _dot(dvt_ref[hh], p.astype(BF16))
        df_out[hh] = _diff_finish_t(acc, tq, lam, nw_ref[...], lambda_init)


def _ctx_attn(nq, nk, nv, dqt, dk, dvt, lq, norm_w, na_o, df_o, *, batch, seq, ctx, lambda_init):
    blk0 = batch * seq // ctx
    s256 = pl.BlockSpec((ctx, 256), lambda b: (blk0 + b, 0))
    sh64 = pl.BlockSpec((DIFF_HEADS, ctx, 64), lambda b: (0, blk0 + b, 0))
    sq_t = pl.BlockSpec((DIFF_HEADS, 64, ctx), lambda b: (0, 0, blk0 + b))
    sv_t = pl.BlockSpec((DIFF_HEADS, DIFF_VROWS, ctx), lambda b: (0, 0, blk0 + b))
    return pl.pallas_call(
        functools.partial(_ctx_attn_body, lambda_init=lambda_init),
        grid=(batch,),
        in_specs=[s256, s256, s256, sq_t, sh64, sv_t, _resident(lq.shape), _resident((DIFF_DV, 1)),
                  pl.BlockSpec(memory_space=pl.ANY), pl.BlockSpec(memory_space=pl.ANY)],
        out_specs=[s256, sh64],
        out_shape=[jax.ShapeDtypeStruct(na_o.shape, F32), jax.ShapeDtypeStruct(df_o.shape, F32)],
        input_output_aliases={8: 0, 9: 1},
        compiler_params=_params("arbitrary"),
        name="ctx_attn",
    )(nq, nk, nv, dqt, dk, dvt, lq, norm_w.reshape(DIFF_DV, 1), na_o, df_o)


def _conv_body(u_ref, prev_ref, next_ref, dw_ref, dwb_ref, lng_ref, lnb_ref, pw_ref, pwb_ref,
               o_ref, pad_ref, sh_ref, *, tiles_per_seq, n_latent_tiles):
    t = u_ref.shape[0]
    i = pl.program_id(0)
    in_ctx = i >= n_latent_tiles
    first = in_ctx | (i % tiles_per_seq == 0)
    last = in_ctx | (i % tiles_per_seq == tiles_per_seq - 1)
    pad_ref[0:HALO, :] = jnp.where(first, 0.0, prev_ref[...])
    pad_ref[HALO:HALO + t, :] = u_ref[...]
    pad_ref[HALO + t:HALO + t + HALO, :] = jnp.where(last, 0.0, next_ref[...])
    base = HALO - CONV_K // 2
    first = {}
    for phase in range(8):
        taps = [k for k in range(CONV_K) if (base + k) % 8 == phase]
        first[phase] = base + taps[0]
        span = base + taps[-1] + t - first[phase]
        sh_ref[phase, 0:span, :] = pad_ref[first[phase]:first[phase] + span, :]
    sub = 64
    parts = []
    for r0 in range(0, t, sub):
        acc = jnp.zeros((sub, CONV_CH), F32)
        for k in range(CONV_K):
            phase = (base + k) % 8
            off = base + k - first[phase] + r0
            acc = acc + dw_ref[k:k + 1, :] * sh_ref[phase, off:off + sub, :]
        parts.append(acc)
    y = jnp.concatenate(parts, axis=0) + dwb_ref[...]
    mu = jnp.mean(y, axis=-1, keepdims=True)
    yc = y - mu
    var = jnp.mean(yc * yc, axis=-1, keepdims=True)
    y = _silu(yc * lax.rsqrt(var + LN_EPS) * lng_ref[...] + lnb_ref[...])
    o_ref[...] = _dot(y.astype(BF16), pw_ref[...]) + pwb_ref[...]


def _conv(cu, dw, dw_b, ln_g, ln_b, pw, pw_b, *, batch, seq, ctx):
    r = cu.shape[0]
    t = CONV_TILE
    assert ctx == t and seq % t == 0
    n_tiles = r // t
    hb = t // HALO
    vec = lambda a: a.reshape(1, CONV_CH)
    return pl.pallas_call(
        functools.partial(_conv_body, tiles_per_seq=seq // t, n_latent_tiles=batch * seq // t),
        grid=(n_tiles,),
        in_specs=[pl.BlockSpec((t, CONV_CH), lambda i: (i, 0)),
                  pl.BlockSpec((HALO, CONV_CH), lambda i: (jnp.maximum(i * hb - 1, 0), 0)),
                  pl.BlockSpec((HALO, CONV_CH), lambda i: (jnp.minimum((i + 1) * hb, n_tiles * hb - 1), 0)),
                  _resident((CONV_K, CONV_CH))] + [_resident((1, CONV_CH))] * 3
                 + [_resident((CONV_CH, CONV_CH)), _resident((1, CONV_CH))],
        out_specs=pl.BlockSpec((t, CONV_CH), lambda i: (i, 0)),
        out_shape=jax.ShapeDtypeStruct((r, CONV_CH), F32),
        scratch_shapes=[pltpu.VMEM((t + 2 * HALO, CONV_CH), F32),
                        pltpu.VMEM((8, t + 2 * HALO, CONV_CH), F32)],
        compiler_params=_params("parallel"),
        name="conv",
    )(cu, cu, cu, dw, vec(dw_b), vec(ln_g), vec(ln_b), pw.astype(BF16), vec(pw_b))


def kernel(x, c, ctx, c_ctx, ada_w, ada_b, norm_ffn1, ffn1_w13, ffn1_w2, norm_mix, w_in, gla_wa_f, gla_ba_f, gla_wa_b, gla_ba_b, gla_norm, na_rpb, diff_lq1, diff_lk1, diff_lq2, diff_lk2, diff_norm, conv_dw, conv_dw_b, conv_ln_g, conv_ln_b, conv_pw, conv_pw_b, w_out, norm_ffn2, ffn2_w13, ffn2_w2, final_norm):
    batch, seq, d = x.shape
    n_ctx = ctx.shape[1]
    depth = ada_w.shape[0]
    tm = ROW_TILE
    assert seq % tm == 0 and (batch * n_ctx) % tm == 0 and batch + 1 <= 8
    lat_tiles = batch * seq // tm
    all_tiles = lat_tiles + batch * n_ctx // tm
    tiles_per_batch = seq // tm

    def group_of(i):
        return jnp.minimum(i // tiles_per_batch, batch)

    def pos_of(i):
        return jnp.where(i < lat_tiles, i % tiles_per_batch, tiles_per_batch)

    c_rows = jnp.concatenate([c, c_ctx[None, :], jnp.zeros((8 - batch - 1, d), F32)], axis=0)
    mods_all = _ada(c_rows, ada_w, ada_b)[:, :batch + 1].reshape(depth, batch + 1, N_MOD, d)
    rope = _rope_tables(seq, tm)
    h = x.reshape(batch * seq, d)
    h_ctx = ctx.reshape(batch * n_ctx, d)

    for i in range(depth):
        last = i == depth - 1
        lambda_init = 0.8 - 0.6 * math.exp(-0.3 * i)
        mods = mods_all[i]
        tok = dict(n_tiles=all_tiles, group_of=group_of)
        geo = dict(batch=batch, seq=seq, ctx=n_ctx)

        h = _ffn(h, mods, norm_ffn1[i], ffn1_w13[i], ffn1_w2[i], final_norm, k0=0, final=False,
                 h_ctx=h_ctx if i == 0 else None, **tok)

        wa, ba = _gate_weights(gla_wa_f[i], gla_ba_f[i], gla_wa_b[i], gla_ba_b[i])
        (gqk, gv, gvt, gg, glg, nq, nk, nv, dq, dk, dv, cu) = _proj(
            h, mods, norm_mix[i], _permute_w_in(w_in[i]), wa, ba, rope, pos_of=pos_of, **tok)

        gnorm = jnp.tile(gla_norm[i], GLA_HEADS).reshape(1, GLA_HEADS * GLA_DV)
        o_f = _gla(gqk, gv, gvt, glg, gg, None, gnorm, reverse=False, **geo)
        gx = _gla(gqk, gv, gvt, glg, gg, o_f, gnorm, reverse=True, **geo)

        nx = _na(nq, nk, nv, _na_col_tables(na_rpb[i]), **geo)

        lq = jnp.stack([diff_lq1[i], diff_lk1[i], diff_lq2[i], diff_lk2[i]])
        dx = _diff(dq, dk, dv, lq, diff_norm[i], lambda_init=lambda_init, **geo)
        if not last:
            nx, dx = _ctx_attn(nq, nk, nv, dq, dk, dv, lq, diff_norm[i], nx, dx,
                               lambda_init=lambda_init, **geo)

        cx = _conv(cu, conv_dw[i], conv_dw_b[i], conv_ln_g[i], conv_ln_b[i], conv_pw[i], conv_pw_b[i], **geo)

        if last:
            tok = dict(n_tiles=lat_tiles, group_of=group_of)
        h = _ffn(h, mods, norm_ffn2[i], ffn2_w13[i], ffn2_w2[i], final_norm, k0=6, final=last,
                 mixers=(gx, nx, dx, cx, w_out[i], (batch, seq, n_ctx)), **tok)

    return h.reshape(batch, seq, d)
```

```python
import functools
import math

import jax
import jax.numpy as jnp
from jax import lax
from jax.experimental import pallas as pl
from jax.experimental.pallas import tpu as pltpu

F32 = jnp.float32
BF16 = jnp.bfloat16

GRID_W = 64
N_MOD = 9
RMS_EPS = 1e-6
LN_EPS = 1e-5
NEG_INF = -1e30
GLA_HEADS, GLA_DK, GLA_DV, GLA_RANK, GLA_TAU, GLA_CHUNK = 4, 32, 64, 16, 16.0, 64
NA_HEADS, NA_DIM, NA_ROWS, NA_COLS = 4, 64, 8, 16
DIFF_HEADS, DIFF_DK, DIFF_DV = 4, 32, 64
DIFF_VROWS = 128
CONV_CH, CONV_K = 256, 31
ROPE_BASE = 10000.0
LOG2E = 1.4426950408889634

LANES = 128
VMEM_LIMIT = 56 * 1024 * 1024

ROW_TILE = 512
FF_CHUNK = 256
GLA_BLOCK = 256
NA_QROWS = 8
NA_KBLK = 256
CONV_TILE = 256
DIFF_TQ = 256
DIFF_QTILES = 8
DIFF_TK = 256
DIFF_UNROLL = 32
DIFF_AHEAD = 3
DIFF_RING = 4
HALO = 16

C_GLA, C_NA, C_DIFF, C_CONV, C_AUX, C_END = 0, 768, 1536, 2304, 2816, 2944


def _dot(a, b):
    return jnp.dot(a, b, preferred_element_type=F32)


def _dot_nt(a, b):
    return lax.dot_general(a, b, (((1,), (1,)), ((), ())), preferred_element_type=F32)


def _params(*sem):
    return pltpu.CompilerParams(dimension_semantics=sem, vmem_limit_bytes=VMEM_LIMIT)


def _resident(shape):
    nd = len(shape)
    return pl.BlockSpec(shape, lambda *_: (0,) * nd, pipeline_mode=pl.Buffered(1))


def _silu(x):
    return x * jax.nn.sigmoid(x)


def _rms(x, w):
    return x * lax.rsqrt(jnp.mean(x * x, axis=-1, keepdims=True) + RMS_EPS) * w


def _ada_body(c_ref, w_ref, b_ref, o_ref):
    s = _silu(c_ref[...])
    o_ref[...] = jnp.dot(s, w_ref[...], precision=lax.Precision.HIGHEST,
                         preferred_element_type=F32) + b_ref[...]


def _ada(c_rows, ada_w, ada_b):
    depth, d, _ = ada_w.shape
    return pl.pallas_call(
        _ada_body,
        grid=(depth, N_MOD),
        in_specs=[pl.BlockSpec((8, d), lambda l, n: (0, 0)),
                  pl.BlockSpec((None, d, d), lambda l, n: (l, 0, n)),
                  pl.BlockSpec((None, 1, d), lambda l, n: (l, 0, n))],
        out_specs=pl.BlockSpec((None, 8, d), lambda l, n: (l, 0, n)),
        out_shape=jax.ShapeDtypeStruct((depth, 8, N_MOD * d), F32),
        compiler_params=_params("arbitrary", "arbitrary"),
        name="ada",
    )(c_rows, ada_w, ada_b.reshape(depth, 1, N_MOD * d))


def _ffn_body(*refs, k0, final, source, n_first):
    h_ref = refs[0]
    n_extra = {"plain": 0, "two_arrays": 1, "mixers": 6}[source]
    extra = refs[1:1 + n_extra]
    mod_ref, nw_ref, w13_ref, w2_ref, fw_ref, o_ref, xb_ref, g_ref, h13_ref = refs[1 + n_extra:]
    ff = w2_ref.shape[0]
    tf = h13_ref.shape[2] // 2
    n_chunks = ff // tf
    x = h_ref[...]
    if source == "two_arrays":
        x = jnp.where(pl.program_id(0) < n_first, x, extra[0][...])
    elif source == "mixers":
        gx_ref, gxc_ref, nx_ref, dx_ref, cx_ref, wo_ref = extra
        gx = jnp.where(pl.program_id(0) < n_first, gx_ref[...], gxc_ref[...].reshape(gx_ref.shape))
        mix = jnp.concatenate([gx, nx_ref[...]] + [dx_ref[hh] for hh in range(DIFF_HEADS)]
                              + [cx_ref[...]], axis=-1).astype(BF16)
        x = x + mod_ref[5:6, :] * _dot(mix, wo_ref[...])
    xm = _rms(x, nw_ref[...]) * (1.0 + mod_ref[k0 + 1:k0 + 2, :]) + mod_ref[k0:k0 + 1, :]
    xb_ref[...] = xm.astype(BF16)

    def cols(c):
        return c * tf if isinstance(c, int) else pl.multiple_of(c * tf, tf)

    def up(c, slot):
        h13_ref[slot, :, :tf] = _dot(xb_ref[...], w13_ref[:, pl.ds(cols(c), tf)])
        h13_ref[slot, :, tf:] = _dot(xb_ref[...], w13_ref[:, pl.ds(ff + cols(c), tf)])

    def gate(c, slot):
        a = h13_ref[slot, :, :tf]
        u = h13_ref[slot, :, tf:]
        g_ref[:, pl.ds(cols(c), tf)] = (_silu(a) * u).astype(BF16)

    def pair(t, carry):
        up(2 * t + 1, 1)
        gate(2 * t, 0)
        up(2 * t + 2, 0)
        gate(2 * t + 1, 1)
        return carry

    up(0, 0)
    for t in range((n_chunks - 1) // 2):
        pair(t, 0)
    gate(n_chunks - 1, 0)
    out = x + (0.5 * mod_ref[k0 + 2:k0 + 3, :]) * _dot(g_ref[...], w2_ref[...])
    if final:
        out = _rms(out, fw_ref[...])
    o_ref[...] = out


def _ffn(h, mods, norm_w, w13, w2, final_w, *, k0, n_tiles, group_of, final, h_ctx=None, mixers=None):
    d = h.shape[1]
    tm = ROW_TILE
    ff = w2.shape[0]
    n_chunks = ff // FF_CHUNK
    assert ff % FF_CHUNK == 0 and n_chunks % 2 == 1
    row = lambda i: (i, 0)
    source, n_first, extra, extra_specs, h_spec = "plain", None, [], [], pl.BlockSpec((tm, d), row)
    if h_ctx is not None:
        source, n_first, extra = "two_arrays", h.shape[0] // tm, [h_ctx]
        h_spec = pl.BlockSpec((tm, d), lambda i: (jnp.minimum(i, n_first - 1), 0))
        extra_specs = [pl.BlockSpec((tm, d), lambda i: (jnp.maximum(i - n_first, 0), 0))]
    elif mixers is not None:
        gx, nx, dx, cx, w_out, (batch, seq, n_ctx) = mixers
        assert batch * n_ctx == tm and seq % tm == 0 and seq % n_ctx == 0
        source, n_first, extra = "mixers", batch * seq // tm, [gx, gx, nx, dx, cx, w_out.astype(BF16)]
        per_seq = seq // tm
        extra_specs = [pl.BlockSpec((None, tm, 256),
                                    lambda i: (jnp.minimum(i // per_seq, batch - 1), i % per_seq, 0)),
                       pl.BlockSpec((batch, n_ctx, 256), lambda i: (0, seq // n_ctx, 0)),
                       pl.BlockSpec((tm, 256), row),
                       pl.BlockSpec((DIFF_HEADS, tm, 64), lambda i: (0, i, 0)),
                       pl.BlockSpec((tm, 256), row), _resident(w_out.shape)]
    return pl.pallas_call(
        functools.partial(_ffn_body, k0=k0, final=final, source=source, n_first=n_first),
        grid=(n_tiles,),
        in_specs=[h_spec] + extra_specs + [
            pl.BlockSpec((None, N_MOD, d), lambda i: (group_of(i), 0, 0)),
            _resident((1, d)),
            _resident(w13.shape),
            _resident(w2.shape),
            _resident((1, d))],
        out_specs=pl.BlockSpec((tm, d), row),
        out_shape=jax.ShapeDtypeStruct((n_tiles * tm, d), F32),
        scratch_shapes=[pltpu.VMEM((tm, d), BF16), pltpu.VMEM((tm, ff), BF16),
                        pltpu.VMEM((2, tm, 2 * FF_CHUNK), F32)],
        compiler_params=_params("parallel"),
        name="ffn",
    )(h, *extra, mods, norm_w.reshape(1, d), w13.astype(BF16), w2.astype(BF16), final_w.reshape(1, d))


def _log_sigmoid(x):
    return jnp.minimum(x, 0.0) - jnp.log1p(jnp.exp(-jnp.abs(x)))


def _rope_rotate(x):
    n = x.shape[-1]
    lane = lax.broadcasted_iota(jnp.int32, x.shape, 1)
    up = pltpu.roll(x, n - 8, 1)
    dn = pltpu.roll(x, 8, 1)
    return jnp.where((lane & 15) < 8, -up, dn)


def _rope_rotate_rows(x):
    n = x.shape[0]
    row = lax.broadcasted_iota(jnp.int32, x.shape, 0)
    up = pltpu.roll(x, n - 8, 0)
    dn = pltpu.roll(x, 8, 0)
    return jnp.where((row & 15) < 8, -up, dn)


def _proj_body(h_ref, mod_ref, nw_ref, w_ref, wvt_ref, wdqt_ref, wdvt_ref, wa_ref, ba_ref,
               cos_ref, sin_ref, cost_ref, sint_ref,
               gqk_ref, gv_ref, gvt_ref, gg_ref, glg_ref, nq_ref, nk_ref, nv_ref,
               dqt_ref, dk_ref, dvt_ref, cu_ref, xb_ref):
    x = h_ref[...]
    tm = x.shape[0]
    xm = _rms(x, nw_ref[...]) * (1.0 + mod_ref[4:5, :]) + mod_ref[3:4, :]
    xb_ref[...] = xm.astype(BF16)

    z = _dot(xb_ref[...], w_ref[:, C_GLA:C_GLA + 256])
    lane = lax.broadcasted_iota(jnp.int32, (1, 256), 1)
    gqk_ref[...] = z * jnp.where(lane < 128, GLA_DK ** -0.5, 1.0)
    gv_ref[...] = _dot(xb_ref[...], w_ref[:, C_GLA + 256:C_GLA + 512]).astype(BF16)
    gvt_ref[...] = _dot_nt(wvt_ref[...], xb_ref[...]).astype(BF16)
    gg_ref[...] = _dot(xb_ref[...], w_ref[:, C_GLA + 512:C_GLA + 768]).astype(BF16)
    aux = _dot(xb_ref[...], w_ref[:, C_AUX:C_END])
    pre = _dot(aux.astype(BF16), wa_ref[...]) + ba_ref[...]
    glg_ref[...] = _log_sigmoid(pre) * (1.0 / GLA_TAU)

    nq_ref[...] = (_dot(xb_ref[...], w_ref[:, C_NA:C_NA + 256]) * (NA_DIM ** -0.5 * LOG2E)).astype(BF16)
    nk_ref[...] = _dot(xb_ref[...], w_ref[:, C_NA + 256:C_NA + 512]).astype(BF16)
    nv_ref[...] = _dot(xb_ref[...], w_ref[:, C_NA + 512:C_NA + 768]).astype(BF16)

    cos = cos_ref[...]
    sin = sin_ref[...]
    cos2 = jnp.concatenate([cos, cos], axis=1)
    sin2 = jnp.concatenate([sin, sin], axis=1)
    zk = _dot(xb_ref[...], w_ref[:, C_DIFF + 256:C_DIFF + 512])
    zk = zk * cos2 + _rope_rotate(zk) * sin2
    n_rep = 2 * DIFF_HEADS
    cos_t = jnp.concatenate([cost_ref[...]] * n_rep, axis=0)
    sin_t = jnp.concatenate([sint_ref[...]] * n_rep, axis=0)
    zqt = _dot_nt(wdqt_ref[...], xb_ref[...])
    zqt = (zqt * cos_t + _rope_rotate_rows(zqt) * sin_t) * (DIFF_DK ** -0.5 * LOG2E)
    zvt = _dot_nt(wdvt_ref[...], xb_ref[...])
    pad_rows = DIFF_VROWS - DIFF_DV
    one_row = jnp.where(lax.broadcasted_iota(jnp.int32, (pad_rows, tm), 0) == 0, 1.0, 0.0).astype(BF16)
    for hh in range(DIFF_HEADS):
        sl = slice(64 * hh, 64 * hh + 64)
        dk_ref[hh] = zk[:, sl].astype(BF16)
        dqt_ref[hh] = zqt[sl, :].astype(BF16)
        dvt_ref[hh, 0:64, :] = zvt[sl, :].astype(BF16)
        dvt_ref[hh, DIFF_DV:DIFF_VROWS, :] = one_row

    za = _dot(xb_ref[...], w_ref[:, C_CONV:C_CONV + 256])
    zg = _dot(xb_ref[...], w_ref[:, C_CONV + 256:C_CONV + 512])
    cu_ref[...] = za * jax.nn.sigmoid(zg)


def _proj(h, mods, norm_w, w_p, wa, ba, rope, *, n_tiles, group_of, pos_of):
    r, d = h.shape
    tm = ROW_TILE
    row = lambda i: (i, 0)
    hrow = lambda i: (0, i, 0)
    hcol = lambda i: (0, 0, i)
    f32_256 = jax.ShapeDtypeStruct((r, 256), F32)
    bf_256 = jax.ShapeDtypeStruct((r, 256), BF16)
    wvt = w_p[:, C_GLA + 256:C_GLA + 512].T
    wdqt = w_p[:, C_DIFF:C_DIFF + 256].T
    wdvt = w_p[:, C_DIFF + 512:C_DIFF + 768].T
    cos_r, sin_r, cos_c, sin_c = rope
    out_shape = [f32_256, bf_256, jax.ShapeDtypeStruct((256, r), BF16), bf_256, f32_256,
                 bf_256, bf_256, bf_256,
                 jax.ShapeDtypeStruct((DIFF_HEADS, 64, r), BF16),
                 jax.ShapeDtypeStruct((DIFF_HEADS, r, 64), BF16),
                 jax.ShapeDtypeStruct((DIFF_HEADS, DIFF_VROWS, r), BF16),
                 f32_256]
    out_specs = [pl.BlockSpec((tm, 256), row)] * 2 + [pl.BlockSpec((256, tm), lambda i: (0, i))] + [
        pl.BlockSpec((tm, 256), row)] * 5 + [
        pl.BlockSpec((DIFF_HEADS, 64, tm), hcol),
        pl.BlockSpec((DIFF_HEADS, tm, 64), hrow),
        pl.BlockSpec((DIFF_HEADS, DIFF_VROWS, tm), hcol),
        pl.BlockSpec((tm, 256), row)]
    return pl.pallas_call(
        _proj_body,
        grid=(n_tiles,),
        in_specs=[pl.BlockSpec((tm, d), row),
                  pl.BlockSpec((None, N_MOD, d), lambda i: (group_of(i), 0, 0)),
                  _resident((1, d)),
                  _resident(w_p.shape),
                  _resident(wvt.shape),
                  _resident(wdqt.shape),
                  _resident(wdvt.shape),
                  _resident(wa.shape),
                  _resident(ba.shape),
                  pl.BlockSpec((tm, LANES), lambda i: (pos_of(i), 0)),
                  pl.BlockSpec((tm, LANES), lambda i: (pos_of(i), 0)),
                  pl.BlockSpec((DIFF_DK, tm), lambda i: (0, pos_of(i))),
                  pl.BlockSpec((DIFF_DK, tm), lambda i: (0, pos_of(i)))],
        out_specs=out_specs,
        out_shape=out_shape,
        scratch_shapes=[pltpu.VMEM((tm, d), BF16)],
        compiler_params=_params("parallel"),
        name="proj",
    )(h, mods, norm_w.reshape(1, d), w_p, wvt, wdqt, wdvt, wa, ba, cos_r, sin_r, cos_c, sin_c)


def _permute_w_in(w_in):
    d = w_in.shape[0]
    g0 = 2 * GLA_HEADS * GLA_DK + 2 * GLA_HEADS * GLA_DV
    aux = w_in[:, g0:g0 + 2 * GLA_RANK]
    rest = w_in[:, g0 + 2 * GLA_RANK:]
    pad = jnp.zeros((d, C_END - C_AUX - 2 * GLA_RANK), w_in.dtype)
    return jnp.concatenate([w_in[:, :g0], rest, aux, pad], axis=1).astype(BF16)


def _gate_weights(wa_f, ba_f, wa_b, ba_b):
    n = GLA_HEADS * GLA_DK
    wa = jnp.zeros((C_END - C_AUX, 2 * n), F32)
    wa = wa.at[:GLA_RANK, :n].set(wa_f).at[GLA_RANK:2 * GLA_RANK, n:].set(wa_b)
    return wa.astype(BF16), jnp.concatenate([ba_f, ba_b]).reshape(1, 2 * n)


def _rope_tables(seq, tile):
    t = jnp.arange(seq)
    row = (t // GRID_W).astype(F32)
    col = (t % GRID_W).astype(F32)
    half = DIFF_DK // 2
    inv = 1.0 / (ROPE_BASE ** (jnp.arange(0, half, 2, dtype=F32) / half))
    ang_r = row[:, None] * inv
    ang_c = col[:, None] * inv
    ang = jnp.concatenate([ang_r, ang_r, ang_c, ang_c], axis=-1)
    cos = jnp.concatenate([jnp.cos(ang), jnp.ones((tile, DIFF_DK), F32)], axis=0)
    sin = jnp.concatenate([jnp.sin(ang), jnp.zeros((tile, DIFF_DK), F32)], axis=0)
    rep = (1, LANES // DIFF_DK)
    return jnp.tile(cos, rep), jnp.tile(sin, rep), cos.T, sin.T


def _split3(x):
    hi = x.astype(BF16)
    r1 = x - hi.astype(F32)
    mid = r1.astype(BF16)
    lo = (r1 - mid.astype(F32)).astype(BF16)
    return hi, mid, lo


def _gla_body(*refs, reverse, batch):
    per_b = [refs[5 * b:5 * b + 5] for b in range(batch)]
    of_ref, nw_ref, o_ref, st_ref = refs[5 * batch:]
    blk = per_b[0][0].shape[0]
    n_chunks = blk // GLA_CHUNK
    nqk = GLA_HEADS * GLA_DK
    nv = GLA_HEADS * GLA_DV

    @pl.when(pl.program_id(0) == 0)
    def _():
        st_ref[...] = jnp.zeros_like(st_ref)

    ri = lax.broadcasted_iota(jnp.int32, (blk, blk), 0)
    ci = lax.broadcasted_iota(jnp.int32, (blk, blk), 1)
    same = (ri // GLA_CHUNK) == (ci // GLA_CHUNK)
    tri = jnp.where(same & ((ci >= ri) if reverse else (ci <= ri)), 1.0, 0.0).astype(BF16)
    bcum_of = []
    for qk_ref, v_ref, vt_ref, lg_ref, g_ref in per_b:
        lg = lg_ref[:, nqk:2 * nqk] if reverse else lg_ref[:, 0:nqk]
        hi, mid, lo = _split3(lg)
        bcum_of.append(_dot(tri, hi) + _dot(tri, mid) + _dot(tri, lo))

    c = GLA_CHUNK
    rk = lax.broadcasted_iota(jnp.int32, (GLA_HEADS * c, nqk), 0) // c
    ck = lax.broadcasted_iota(jnp.int32, (GLA_HEADS * c, nqk), 1) // GLA_DK
    mask_k = rk == ck
    rv = lax.broadcasted_iota(jnp.int32, (GLA_HEADS * c, nv), 0) // c
    cv = lax.broadcasted_iota(jnp.int32, (GLA_HEADS * c, nv), 1) // GLA_DV
    mask_v = rv == cv
    rs = lax.broadcasted_iota(jnp.int32, (nv, nqk), 0) // GLA_DV
    cs = lax.broadcasted_iota(jnp.int32, (nv, nqk), 1) // GLA_DK
    mask_s = rs == cs
    ai = lax.broadcasted_iota(jnp.int32, (c, GLA_HEADS * c), 0)
    aj = lax.broadcasted_iota(jnp.int32, (c, GLA_HEADS * c), 1) % c
    mask_a = (aj >= ai) if reverse else (aj <= ai)

    order = range(n_chunks - 1, -1, -1) if reverse else range(n_chunks)
    units = [(b, ch) for ch in order for b in range(batch)]
    q_in, a_raw, u_t, decay = {}, {}, {}, {}
    vts = [refs_b[2][...].astype(BF16) for refs_b in per_b]
    for b, ch in units:
        qk_ref = per_b[b][0]
        rows = slice(ch * c, (ch + 1) * c)
        k = qk_ref[rows, nqk:2 * nqk]
        bcum = bcum_of[b][rows, :]
        btot = bcum[0:1, :] if reverse else bcum[c - 1:c, :]
        q_in[b, ch] = (qk_ref[rows, 0:nqk] * jnp.exp(bcum)).astype(BF16)
        k_in = k * jnp.exp(-bcum)
        k_out = (k * jnp.exp(btot - bcum)).astype(BF16)
        k_bd = jnp.where(mask_k, jnp.concatenate([k_in] * GLA_HEADS, axis=0), 0.0).astype(BF16)
        a_raw[b, ch] = _dot_nt(q_in[b, ch], k_bd)
        k_pad = jnp.concatenate([jnp.zeros((n * c, nqk), BF16) for n in (ch,) if n] + [k_out]
                                + [jnp.zeros((n * c, nqk), BF16) for n in (n_chunks - 1 - ch,) if n], axis=0)
        u_t[b, ch] = jnp.where(mask_s, _dot(vts[b], k_pad), 0.0)
        decay[b, ch] = jnp.exp(btot)
    o = {}
    for b, ch in units:
        v = per_b[b][1][ch * c:(ch + 1) * c, :]
        v4 = jnp.concatenate([v] * GLA_HEADS, axis=0)
        v_bd = jnp.where(mask_v, v4, jnp.zeros_like(v4))
        o[b, ch] = _dot(jnp.where(mask_a, a_raw[b, ch], 0.0).astype(BF16), v_bd)
    st = [st_ref[b] for b in range(batch)]
    for b, ch in units:
        o[b, ch] = o[b, ch] + _dot_nt(q_in[b, ch], st[b].astype(BF16))
        st[b] = st[b] * decay[b, ch] + u_t[b, ch]
    for b in range(batch):
        st_ref[b] = st[b]
        o_blk = jnp.concatenate([o[b, ch] for ch in range(n_chunks)], axis=0)
        if reverse:
            o_blk = o_blk + of_ref[b]
            hi2, lo2, _ = _split3(o_blk * o_blk)
            hr = lax.broadcasted_iota(jnp.int32, (nv, nv), 0) // GLA_DV
            hc = lax.broadcasted_iota(jnp.int32, (nv, nv), 1) // GLA_DV
            seg = jnp.where(hr == hc, 1.0, 0.0).astype(BF16)
            ms = (_dot(hi2, seg) + _dot(lo2, seg)) * (1.0 / GLA_DV)
            o_blk = o_blk * lax.rsqrt(ms + RMS_EPS) * nw_ref[...] * _silu(per_b[b][4][...].astype(F32))
        o_ref[b] = o_blk


def _gla(gqk, gv, gvt, glg, gg, o_f, norm_w4, *, reverse, batch, seq, ctx):
    blk = GLA_BLOCK
    nc, nl = ctx // blk, seq // blk
    ctx_base = batch * seq // blk

    def step_blk(s):
        if reverse:
            return jnp.where(s < nc, nl + (nc - 1 - s), nl - 1 - (s - nc))
        return jnp.where(s < nc, nl + s, s - nc)

    def row_blk(b):
        return lambda s: jnp.where(s < nc, ctx_base + b * nc - nl, b * nl) + step_blk(s)

    specs, args = [], []
    for b in range(batch):
        spec = pl.BlockSpec((blk, 256), lambda s, f=row_blk(b): (f(s), 0))
        spec_t = pl.BlockSpec((256, blk), lambda s, f=row_blk(b): (0, f(s)))
        if reverse:
            specs += [spec, spec, spec_t, spec, spec]
            args += [gqk, gv, gvt, glg, gg]
        else:
            specs += [spec, spec, spec_t, spec, _resident((1, 256))]
            args += [gqk, gv, gvt, glg, norm_w4]
    seq_spec = pl.BlockSpec((batch, blk, 256), lambda s: (0, step_blk(s), 0))
    if o_f is None:
        o_f, of_spec = norm_w4, _resident((1, 256))
    else:
        of_spec = seq_spec
    return pl.pallas_call(
        functools.partial(_gla_body, reverse=reverse, batch=batch),
        grid=(nc + nl,),
        in_specs=specs + [of_spec, _resident((1, 256))],
        out_specs=seq_spec,
        out_shape=jax.ShapeDtypeStruct((batch, seq + ctx, 256), F32),
        scratch_shapes=[pltpu.VMEM((batch, GLA_HEADS * GLA_DV, GLA_HEADS * GLA_DK), F32)],
        compiler_params=_params("arbitrary"),
        name="gla_bwd" if reverse else "gla_fwd",
    )(*args, o_f, norm_w4)


def _na_col_tables(rpb):
    col = jnp.arange(GRID_W)
    cstart = jnp.clip(col - NA_COLS // 2, 0, GRID_W - NA_COLS)
    col_ok = (col[None, :] >= cstart[:, None]) & (col[None, :] < cstart[:, None] + NA_COLS)
    col_off = jnp.clip(col[None, :] - col[:, None] + (NA_COLS - 1), 0, 2 * NA_COLS - 2)
    t = jnp.where(col_ok[None, None], rpb[:, :, col_off] * LOG2E, NEG_INF)
    dead = jnp.full((NA_HEADS, 1, GRID_W, GRID_W), NEG_INF, F32)
    t = jnp.concatenate([dead, t.astype(F32), dead], axis=1)
    return jnp.concatenate([t[:, :-1], t[:, 1:]], axis=-1)


def _na_body(q_ref, k0, k1, k2, k3, v0, v1, v2, v3, kc_ref, vc_ref, tab_ref, o_ref, *, rows, key_blocks):
    k_blk = [k0[...], k1[...], k2[...], k3[...]]
    v_blk = [v0[...], v1[...], v2[...], v3[...]]
    kc = kc_ref[...]
    vc = vc_ref[...]

    j = pl.program_id(1)
    rows_per_blk = NA_KBLK // GRID_W
    r0 = j * NA_QROWS
    kr0 = jnp.clip(2 * j - 1, 0, key_blocks - 4) * rows_per_blk
    lane_lo = lax.broadcasted_iota(jnp.int32, (1, 2 * GRID_W), 1) < GRID_W
    lane = lax.broadcasted_iota(jnp.int32, (1, NA_HEADS * NA_DIM), 1) // NA_DIM
    half = NA_QROWS // 2
    n_pairs = 3 * rows_per_blk // 2
    gq = half * GRID_W
    for g in range(2):
        first = jnp.clip(r0 + g * half - NA_ROWS // 2, 0, rows - NA_ROWS)
        shift = jnp.clip((first - kr0) // rows_per_blk, 0, 1)
        kw = jnp.concatenate([jnp.where(shift == 0, k_blk[i], k_blk[i + 1]) for i in range(3)], axis=0)
        vw = jnp.concatenate([jnp.where(shift == 0, v_blk[i], v_blk[i + 1]) for i in range(3)], axis=0)
        kg0 = kr0 + shift * rows_per_blk
        entry, ok = [], []
        for a in range(half):
            r = r0 + g * half + a
            start = jnp.clip(r - NA_ROWS // 2, 0, rows - NA_ROWS)
            for bp in range(n_pairs):
                rk = kg0 + 2 * bp
                entry.append(jnp.clip(rk - r + NA_ROWS, 0, 2 * NA_ROWS - 1))
                in0 = ((rk >= start) & (rk < start + NA_ROWS)).astype(jnp.int32)
                in1 = ((rk + 1 >= start) & (rk + 1 < start + NA_ROWS)).astype(jnp.int32)
                ok.append(jnp.where(lane_lo, in0, in1) != 0)

        def bias_of(hh):
            rows_ = []
            for a in range(half):
                tiles = [jnp.where(ok[a * n_pairs + bp], tab_ref[hh, entry[a * n_pairs + bp]], NEG_INF)
                         for bp in range(n_pairs)]
                rows_.append(jnp.concatenate(tiles, axis=1))
            return jnp.concatenate(rows_, axis=0)

        q = q_ref[g * gq:(g + 1) * gq, :]
        out = jnp.zeros((gq, NA_HEADS * NA_DIM), F32)
        for hh in range(NA_HEADS):
            mh = lane == hh
            qh = jnp.where(mh, q, jnp.zeros_like(q))
            s_w = _dot_nt(qh, kw) + bias_of(hh)
            s_c = _dot_nt(qh, kc)
            m = jnp.maximum(jnp.max(s_w, axis=-1, keepdims=True), jnp.max(s_c, axis=-1, keepdims=True))
            p_w = jnp.exp2(s_w - m)
            p_c = jnp.exp2(s_c - m)
            l = jnp.sum(p_w, axis=-1, keepdims=True) + jnp.sum(p_c, axis=-1, keepdims=True)
            o = _dot(p_w.astype(BF16), vw) + _dot(p_c.astype(BF16), vc)
            out = out + jnp.where(mh, o / l, 0.0)
        o_ref[g * gq:(g + 1) * gq, :] = out


def _na(nq, nk, nv, tab, *, batch, seq, ctx):
    r = nq.shape[0]
    tq = NA_QROWS * GRID_W
    nj = seq // tq
    kb = seq // NA_KBLK
    assert kb >= 4 and seq % tq == 0 and seq % ctx == 0
    ctx_blk0 = batch * seq // ctx

    def kspec(i):
        return pl.BlockSpec((NA_KBLK, 256),
                            lambda b, j: (b * kb + jnp.clip(2 * j - 1, 0, kb - 4) + i, 0))

    cspec = pl.BlockSpec((ctx, 256), lambda b, j: (ctx_blk0 + b, 0))
    return pl.pallas_call(
        functools.partial(_na_body, rows=seq // GRID_W, key_blocks=kb),
        grid=(batch, nj),
        in_specs=[pl.BlockSpec((tq, 256), lambda b, j: (b * nj + j, 0))]
                 + [kspec(i) for i in range(4)] + [kspec(i) for i in range(4)]
                 + [cspec, cspec, _resident(tab.shape)],
        out_specs=pl.BlockSpec((tq, 256), lambda b, j: (b * nj + j, 0)),
        out_shape=jax.ShapeDtypeStruct((r, 256), F32),
        compiler_params=_params("parallel", "arbitrary"),
        name="na",
    )(nq, nk, nk, nk, nk, nv, nv, nv, nv, nk, nv, tab)


def _diff_lambda(lq_ref, lambda_init):
    lq = lq_ref[...]
    s1 = jnp.sum(lq[0:1, :] * lq[1:2, :], axis=-1, keepdims=True)
    s2 = jnp.sum(lq[2:3, :] * lq[3:4, :], axis=-1, keepdims=True)
    return jnp.exp(s1) - jnp.exp(s2) + lambda_init


def _stack_qt(qt):
    row = lax.broadcasted_iota(jnp.int32, (2 * DIFF_DK, 1), 0)
    zero = jnp.zeros_like(qt)
    return jnp.concatenate([jnp.where(row < DIFF_DK, qt, zero), jnp.where(row >= DIFF_DK, qt, zero)], axis=1)


def _diff_finish_t(acc, tq, lam, nw_col, lambda_init):
    o = acc[0:DIFF_DV, :] / acc[DIFF_DV:DIFF_DV + 1, :]
    od = o[:, :tq] - lam * o[:, tq:]
    ms = jnp.mean(od * od, axis=0, keepdims=True)
    y = od * lax.rsqrt(ms + RMS_EPS) * nw_col * (1.0 - lambda_init)
    y = jnp.concatenate([y, jnp.zeros_like(y)], axis=0)
    return jnp.transpose(y)[:, 0:DIFF_DV]


def _diff_body(qt_ref, kc_ref, vtc_ref, k_ref, vt_ref, lq_ref, nw_ref, o_ref, s_ref, *, tq, tk, unroll, lambda_init):
    def tile(t, carry):
        q0 = pl.multiple_of(t * tq, tq)
        out = _diff_tile(qt_ref[:, pl.ds(q0, tq)], kc_ref, vtc_ref, k_ref, vt_ref, lq_ref, nw_ref, s_ref,
                         tk=tk, unroll=unroll, lambda_init=lambda_init)
        o_ref[pl.ds(q0, tq), :] = out
        return carry

    lax.fori_loop(0, qt_ref.shape[1] // tq, tile, 0)


def _diff_tile(qt, kc_ref, vtc_ref, k_ref, vt_ref, lq_ref, nw_ref, s_ref, *, tk, unroll, lambda_init):
    tq = qt.shape[1]
    n_chunks = k_ref.shape[0] // tk
    ring = s_ref.shape[0]
    qs = _stack_qt(qt)

    def chunk(j):
        return pl.ds(j * tk if isinstance(j, int) else pl.multiple_of(j * tk, tk), tk)

    def issue_scores(j, slot):
        s = _dot(k_ref[chunk(j), :], qs)
        s_ref[slot] = s
        return jnp.max(s, axis=0, keepdims=True)

    s_ctx = _dot(kc_ref[...], qs)
    m0 = jnp.max(s_ctx, axis=0, keepdims=True)
    cmax0 = tuple(issue_scores(min(a, n_chunks - 1), a % ring) for a in range(DIFF_AHEAD))
    acc0 = _dot(vtc_ref[...], jnp.exp2(s_ctx - m0).astype(BF16))

    def group(g, carry, tail=False):
        cmax, m, acc = carry
        for u in range(unroll):
            j = g * unroll + u
            if tail and j + DIFF_AHEAD >= n_chunks:
                c_new = cmax[0]
            else:
                c_new = issue_scores(j + DIFF_AHEAD, (u + DIFF_AHEAD) % ring)
            m_new = jnp.maximum(m, cmax[0])
            p = jnp.exp2(s_ref[u % ring] - m_new).astype(BF16)
            vt = vt_ref[:, chunk(j)]
            acc = jnp.exp2(m - m_new) * acc + _dot(vt, p)
            m, cmax = m_new, cmax[1:] + (c_new,)
        return cmax, m, acc

    n_groups = n_chunks // unroll
    trips = jnp.minimum(pl.program_id(2) + 1, 1) * (n_groups - 1)
    carry = lax.fori_loop(0, trips, group, (cmax0, m0, acc0))
    _, _, acc = group(n_groups - 1, carry, tail=True)
    return _diff_finish_t(acc, tq, _diff_lambda(lq_ref, lambda_init), nw_ref[...], lambda_init)


def _diff(dqt, dk, dvt, lq, norm_w, *, batch, seq, ctx, lambda_init):
    h, r, _ = dk.shape
    tq = DIFF_TQ
    tqs = tq * min(DIFF_QTILES, seq // tq)
    nq = seq // tqs
    tk = min(DIFF_TK, seq)
    unroll = min(DIFF_UNROLL, seq // tk)
    assert (seq // tk) % unroll == 0 and unroll % DIFF_RING == 0 and DIFF_RING > DIFF_AHEAD and seq % tqs == 0
    ctx_blk0 = batch * seq // ctx
    return pl.pallas_call(
        functools.partial(_diff_body, tq=tq, tk=tk, unroll=unroll, lambda_init=lambda_init),
        grid=(batch, h, nq),
        in_specs=[pl.BlockSpec((None, 64, tqs), lambda b, hh, i: (hh, 0, b * nq + i)),
                  pl.BlockSpec((None, ctx, 64), lambda b, hh, i: (hh, ctx_blk0 + b, 0)),
                  pl.BlockSpec((None, DIFF_VROWS, ctx), lambda b, hh, i: (hh, 0, ctx_blk0 + b)),
                  pl.BlockSpec((None, seq, 64), lambda b, hh, i: (hh, b, 0)),
                  pl.BlockSpec((None, DIFF_VROWS, seq), lambda b, hh, i: (hh, 0, b)),
                  _resident(lq.shape),
                  _resident((DIFF_DV, 1))],
        out_specs=pl.BlockSpec((None, tqs, 64), lambda b, hh, i: (hh, b * nq + i, 0)),
        out_shape=jax.ShapeDtypeStruct((h, r, 64), F32),
        scratch_shapes=[pltpu.VMEM((DIFF_RING, tk, 2 * tq), F32)],
        compiler_params=_params("parallel", "parallel", "arbitrary"),
        name="diff",
    )(dqt, dk, dvt, dk, dvt, lq, norm_w.reshape(DIFF_DV, 1))


def _ctx_attn_body(nq_ref, nk_ref, nv_ref, dqt_ref, dk_ref, dvt_ref, lq_ref, nw_ref, na_in, df_in,
                   na_out, df_out, *, lambda_init):
    del na_in, df_in
    q = nq_ref[...]
    k = nk_ref[...]
    v = nv_ref[...]
    lane = lax.broadcasted_iota(jnp.int32, (1, NA_HEADS * NA_DIM), 1) // NA_DIM
    out = jnp.zeros(na_out.shape, F32)
    for hh in range(NA_HEADS):
        mh = lane == hh
        s = _dot_nt(jnp.where(mh, q, jnp.zeros_like(q)), k)
        p = jnp.exp2(s - jnp.max(s, axis=-1, keepdims=True))
        o = _dot(p.astype(BF16), v) / jnp.sum(p, axis=-1, keepdims=True)
        out = out + jnp.where(mh, o, 0.0)
    na_out[...] = out

    lam = _diff_lambda(lq_ref, lambda_init)
    tq = dqt_ref.shape[2]
    for hh in range(DIFF_HEADS):
        s = _dot(dk_ref[hh], _stack_qt(dqt_ref[hh]))
        p = jnp.exp2(s - jnp.max(s, axis=0, keepdims=True))
        acc = _dot(dvt_ref[hh], p.astype(BF16))
        df_out[hh] = _diff_finish_t(acc, tq, lam, nw_ref[...], lambda_init)


def _ctx_attn(nq, nk, nv, dqt, dk, dvt, lq, norm_w, na_o, df_o, *, batch, seq, ctx, lambda_init):
    blk0 = batch * seq // ctx
    s256 = pl.BlockSpec((ctx, 256), lambda b: (blk0 + b, 0))
    sh64 = pl.BlockSpec((DIFF_HEADS, ctx, 64), lambda b: (0, blk0 + b, 0))
    sq_t = pl.BlockSpec((DIFF_HEADS, 64, ctx), lambda b: (0, 0, blk0 + b))
    sv_t = pl.BlockSpec((DIFF_HEADS, DIFF_VROWS, ctx), lambda b: (0, 0, blk0 + b))
    return pl.pallas_call(
        functools.partial(_ctx_attn_body, lambda_init=lambda_init),
        grid=(batch,),
        in_specs=[s256, s256, s256, sq_t, sh64, sv_t, _resident(lq.shape), _resident((DIFF_DV, 1)),
                  pl.BlockSpec(memory_space=pl.ANY), pl.BlockSpec(memory_space=pl.ANY)],
        out_specs=[s256, sh64],
        out_shape=[jax.ShapeDtypeStruct(na_o.shape, F32), jax.ShapeDtypeStruct(df_o.shape, F32)],
        input_output_aliases={8: 0, 9: 1},
        compiler_params=_params("arbitrary"),
        name="ctx_attn",
    )(nq, nk, nv, dqt, dk, dvt, lq, norm_w.reshape(DIFF_DV, 1), na_o, df_o)


def _conv_body(u_ref, prev_ref, next_ref, dw_ref, dwb_ref, lng_ref, lnb_ref, pw_ref, pwb_ref,
               o_ref, pad_ref, sh_ref, *, tiles_per_seq, n_latent_tiles):
    t = u_ref.shape[0]
    i = pl.program_id(0)
    in_ctx = i >= n_latent_tiles
    first = in_ctx | (i % tiles_per_seq == 0)
    last = in_ctx | (i % tiles_per_seq == tiles_per_seq - 1)
    pad_ref[0:HALO, :] = jnp.where(first, 0.0, prev_ref[...])
    pad_ref[HALO:HALO + t, :] = u_ref[...]
    pad_ref[HALO + t:HALO + t + HALO, :] = jnp.where(last, 0.0, next_ref[...])
    base = HALO - CONV_K // 2
    first = {}
    for phase in range(8):
        taps = [k for k in range(CONV_K) if (base + k) % 8 == phase]
        first[phase] = base + taps[0]
        span = base + taps[-1] + t - first[phase]
        sh_ref[phase, 0:span, :] = pad_ref[first[phase]:first[phase] + span, :]
    sub = 64
    parts = []
    for r0 in range(0, t, sub):
        acc = jnp.zeros((sub, CONV_CH), F32)
        for k in range(CONV_K):
            phase = (base + k) % 8
            off = base + k - first[phase] + r0
            acc = acc + dw_ref[k:k + 1, :] * sh_ref[phase, off:off + sub, :]
        parts.append(acc)
    y = jnp.concatenate(parts, axis=0) + dwb_ref[...]
    mu = jnp.mean(y, axis=-1, keepdims=True)
    yc = y - mu
    var = jnp.mean(yc * yc, axis=-1, keepdims=True)
    y = _silu(yc * lax.rsqrt(var + LN_EPS) * lng_ref[...] + lnb_ref[...])
    o_ref[...] = _dot(y.astype(BF16), pw_ref[...]) + pwb_ref[...]


def _conv(cu, dw, dw_b, ln_g, ln_b, pw, pw_b, *, batch, seq, ctx):
    r = cu.shape[0]
    t = CONV_TILE
    assert ctx == t and seq % t == 0
    n_tiles = r // t
    hb = t // HALO
    vec = lambda a: a.reshape(1, CONV_CH)
    return pl.pallas_call(
        functools.partial(_conv_body, tiles_per_seq=seq // t, n_latent_tiles=batch * seq // t),
        grid=(n_tiles,),
        in_specs=[pl.BlockSpec((t, CONV_CH), lambda i: (i, 0)),
                  pl.BlockSpec((HALO, CONV_CH), lambda i: (jnp.maximum(i * hb - 1, 0), 0)),
                  pl.BlockSpec((HALO, CONV_CH), lambda i: (jnp.minimum((i + 1) * hb, n_tiles * hb - 1), 0)),
                  _resident((CONV_K, CONV_CH))] + [_resident((1, CONV_CH))] * 3
                 + [_resident((CONV_CH, CONV_CH)), _resident((1, CONV_CH))],
        out_specs=pl.BlockSpec((t, CONV_CH), lambda i: (i, 0)),
        out_shape=jax.ShapeDtypeStruct((r, CONV_CH), F32),
        scratch_shapes=[pltpu.VMEM((t + 2 * HALO, CONV_CH), F32),
                        pltpu.VMEM((8, t + 2 * HALO, CONV_CH), F32)],
        compiler_params=_params("parallel"),
        name="conv",
    )(cu, cu, cu, dw, vec(dw_b), vec(ln_g), vec(ln_b), pw.astype(BF16), vec(pw_b))


def kernel(x, c, ctx, c_ctx, ada_w, ada_b, norm_ffn1, ffn1_w13, ffn1_w2, norm_mix, w_in, gla_wa_f, gla_ba_f, gla_wa_b, gla_ba_b, gla_norm, na_rpb, diff_lq1, diff_lk1, diff_lq2, diff_lk2, diff_norm, conv_dw, conv_dw_b, conv_ln_g, conv_ln_b, conv_pw, conv_pw_b, w_out, norm_ffn2, ffn2_w13, ffn2_w2, final_norm):
    batch, seq, d = x.shape
    n_ctx = ctx.shape[1]
    depth = ada_w.shape[0]
    tm = ROW_TILE
    assert seq % tm == 0 and (batch * n_ctx) % tm == 0 and batch + 1 <= 8
    lat_tiles = batch * seq // tm
    all_tiles = lat_tiles + batch * n_ctx // tm
    tiles_per_batch = seq // tm

    def group_of(i):
        return jnp.minimum(i // tiles_per_batch, batch)

    def pos_of(i):
        return jnp.where(i < lat_tiles, i % tiles_per_batch, tiles_per_batch)

    c_rows = jnp.concatenate([c, c_ctx[None, :], jnp.zeros((8 - batch - 1, d), F32)], axis=0)
    mods_all = _ada(c_rows, ada_w, ada_b)[:, :batch + 1].reshape(depth, batch + 1, N_MOD, d)
    rope = _rope_tables(seq, tm)
    h = x.reshape(batch * seq, d)
    h_ctx = ctx.reshape(batch * n_ctx, d)

    for i in range(depth):
        last = i == depth - 1
        lambda_init = 0.8 - 0.6 * math.exp(-0.3 * i)
        mods = mods_all[i]
        tok = dict(n_tiles=all_tiles, group_of=group_of)
        geo = dict(batch=batch, seq=seq, ctx=n_ctx)

        h = _ffn(h, mods, norm_ffn1[i], ffn1_w13[i], ffn1_w2[i], final_norm, k0=0, final=False,
                 h_ctx=h_ctx if i == 0 else None, **tok)

        wa, ba = _gate_weights(gla_wa_f[i], gla_ba_f[i], gla_wa_b[i], gla_ba_b[i])
        (gqk, gv, gvt, gg, glg, nq, nk, nv, dq, dk, dv, cu) = _proj(
            h, mods, norm_mix[i], _permute_w_in(w_in[i]), wa, ba, rope, pos_of=pos_of, **tok)

        gnorm = jnp.tile(gla_norm[i], GLA_HEADS).reshape(1, GLA_HEADS * GLA_DV)
        o_f = _gla(gqk, gv, gvt, glg, gg, None, gnorm, reverse=False, **geo)
        gx = _gla(gqk, gv, gvt, glg, gg, o_f, gnorm, reverse=True, **geo)

        nx = _na(nq, nk, nv, _na_col_tables(na_rpb[i]), **geo)

        lq = jnp.stack([diff_lq1[i], diff_lk1[i], diff_lq2[i], diff_lk2[i]])
        dx = _diff(dq, dk, dv, lq, diff_norm[i], lambda_init=lambda_init, **geo)
        if not last:
            nx, dx = _ctx_attn(nq, nk, nv, dq, dk, dv, lq, diff_norm[i], nx, dx,
                               lambda_init=lambda_init, **geo)

        cx = _conv(cu, conv_dw[i], conv_dw_b[i], conv_ln_g[i], conv_ln_b[i], conv_pw[i], conv_pw_b[i], **geo)

        if last:
            tok = dict(n_tiles=lat_tiles, group_of=group_of)
        h = _ffn(h, mods, norm_ffn2[i], ffn2_w13[i], ffn2_w2[i], final_norm, k0=6, final=last,
                 mixers=(gx, nx, dx, cx, w_out[i], (batch, seq, n_ctx)), **tok)

    return h.reshape(batch, seq, d)
```

```python
import functools
import math

import jax
import jax.numpy as jnp
from jax import lax
from jax.experimental import pallas as pl
from jax.experimental.pallas import tpu as pltpu

F32 = jnp.float32
BF16 = jnp.bfloat16

GRID_W = 64
N_MOD = 9
RMS_EPS = 1e-6
LN_EPS = 1e-5
NEG_INF = -1e30
GLA_HEADS, GLA_DK, GLA_DV, GLA_RANK, GLA_TAU, GLA_CHUNK = 4, 32, 64, 16, 16.0, 64
NA_HEADS, NA_DIM, NA_ROWS, NA_COLS = 4, 64, 8, 16
DIFF_HEADS, DIFF_DK, DIFF_DV = 4, 32, 64
DIFF_VROWS = 80
CONV_CH, CONV_K = 256, 31
ROPE_BASE = 10000.0
LOG2E = 1.4426950408889634

LANES = 128
VMEM_LIMIT = 56 * 1024 * 1024

ROW_TILE = 512
FF_CHUNK = 256
GLA_BLOCK = 256
NA_QROWS = 8
NA_KBLK = 256
CONV_TILE = 256
DIFF_TQ = 256
DIFF_QTILES = 8
DIFF_TK = 256
DIFF_UNROLL = 32
DIFF_AHEAD = 3
DIFF_RING = 4
HALO = 16

C_GLA, C_NA, C_DIFF, C_CONV, C_AUX, C_END = 0, 768, 1536, 2304, 2816, 2944


def _dot(a, b):
    return jnp.dot(a, b, preferred_element_type=F32)


def _dot_nt(a, b):
    return lax.dot_general(a, b, (((1,), (1,)), ((), ())), preferred_element_type=F32)


def _params(*sem):
    return pltpu.CompilerParams(dimension_semantics=sem, vmem_limit_bytes=VMEM_LIMIT)


def _resident(shape):
    nd = len(shape)
    return pl.BlockSpec(shape, lambda *_: (0,) * nd, pipeline_mode=pl.Buffered(1))


def _silu(x):
    return x * jax.nn.sigmoid(x)


def _rms(x, w):
    return x * lax.rsqrt(jnp.mean(x * x, axis=-1, keepdims=True) + RMS_EPS) * w


def _ada_body(c_ref, w_ref, b_ref, o_ref):
    s = _silu(c_ref[...])
    o_ref[...] = jnp.dot(s, w_ref[...], precision=lax.Precision.HIGHEST,
                         preferred_element_type=F32) + b_ref[...]


def _ada(c_rows, ada_w, ada_b):
    depth, d, _ = ada_w.shape
    return pl.pallas_call(
        _ada_body,
        grid=(depth, N_MOD),
        in_specs=[pl.BlockSpec((8, d), lambda l, n: (0, 0)),
                  pl.BlockSpec((None, d, d), lambda l, n: (l, 0, n)),
                  pl.BlockSpec((None, 1, d), lambda l, n: (l, 0, n))],
        out_specs=pl.BlockSpec((None, 8, d), lambda l, n: (l, 0, n)),
        out_shape=jax.ShapeDtypeStruct((depth, 8, N_MOD * d), F32),
        compiler_params=_params("arbitrary", "arbitrary"),
        name="ada",
    )(c_rows, ada_w, ada_b.reshape(depth, 1, N_MOD * d))


def _ffn_body(*refs, k0, final, source, n_first):
    h_ref = refs[0]
    n_extra = {"plain": 0, "two_arrays": 1, "mixers": 6}[source]
    extra = refs[1:1 + n_extra]
    mod_ref, nw_ref, w13_ref, w2_ref, fw_ref, o_ref, xb_ref, g_ref, h13_ref = refs[1 + n_extra:]
    ff = w2_ref.shape[0]
    tf = h13_ref.shape[2] // 2
    n_chunks = ff // tf
    x = h_ref[...]
    if source == "two_arrays":
        x = jnp.where(pl.program_id(0) < n_first, x, extra[0][...])
    elif source == "mixers":
        gx_ref, gxc_ref, nx_ref, dx_ref, cx_ref, wo_ref = extra
        gx = jnp.where(pl.program_id(0) < n_first, gx_ref[...], gxc_ref[...].reshape(gx_ref.shape))
        mix = jnp.concatenate([gx, nx_ref[...]] + [dx_ref[hh] for hh in range(DIFF_HEADS)]
                              + [cx_ref[...]], axis=-1)
        x = x + mod_ref[5:6, :] * _dot(mix, wo_ref[...])
    xm = _rms(x, nw_ref[...]) * (1.0 + mod_ref[k0 + 1:k0 + 2, :]) + mod_ref[k0:k0 + 1, :]
    xb_ref[...] = xm.astype(BF16)

    def cols(c):
        return c * tf if isinstance(c, int) else pl.multiple_of(c * tf, tf)

    def up(c, slot):
        h13_ref[slot, :, :tf] = _dot(xb_ref[...], w13_ref[:, pl.ds(cols(c), tf)])
        h13_ref[slot, :, tf:] = _dot(xb_ref[...], w13_ref[:, pl.ds(ff + cols(c), tf)])

    def gate(c, slot):
        a = h13_ref[slot, :, :tf]
        u = h13_ref[slot, :, tf:]
        g_ref[:, pl.ds(cols(c), tf)] = (_silu(a) * u).astype(BF16)

    def pair(t, carry):
        up(2 * t + 1, 1)
        gate(2 * t, 0)
        up(2 * t + 2, 0)
        gate(2 * t + 1, 1)
        return carry

    up(0, 0)
    for t in range((n_chunks - 1) // 2):
        pair(t, 0)
    gate(n_chunks - 1, 0)
    out = x + (0.5 * mod_ref[k0 + 2:k0 + 3, :]) * _dot(g_ref[...], w2_ref[...])
    if final:
        out = _rms(out, fw_ref[...])
    o_ref[...] = out


def _ffn(h, mods, norm_w, w13, w2, final_w, *, k0, n_tiles, group_of, final, h_ctx=None, mixers=None):
    d = h.shape[1]
    tm = ROW_TILE
    ff = w2.shape[0]
    n_chunks = ff // FF_CHUNK
    assert ff % FF_CHUNK == 0 and n_chunks % 2 == 1
    row = lambda i: (i, 0)
    source, n_first, extra, extra_specs, h_spec = "plain", None, [], [], pl.BlockSpec((tm, d), row)
    if h_ctx is not None:
        source, n_first, extra = "two_arrays", h.shape[0] // tm, [h_ctx]
        h_spec = pl.BlockSpec((tm, d), lambda i: (jnp.minimum(i, n_first - 1), 0))
        extra_specs = [pl.BlockSpec((tm, d), lambda i: (jnp.maximum(i - n_first, 0), 0))]
    elif mixers is not None:
        gx, nx, dx, cx, w_out, (batch, seq, n_ctx) = mixers
        assert batch * n_ctx == tm and seq % tm == 0 and seq % n_ctx == 0
        source, n_first, extra = "mixers", batch * seq // tm, [gx, gx, nx, dx, cx, w_out.astype(BF16)]
        per_seq = seq // tm
        extra_specs = [pl.BlockSpec((None, tm, 256),
                                    lambda i: (jnp.minimum(i // per_seq, batch - 1), i % per_seq, 0)),
                       pl.BlockSpec((batch, n_ctx, 256), lambda i: (0, seq // n_ctx, 0)),
                       pl.BlockSpec((tm, 256), row),
                       pl.BlockSpec((DIFF_HEADS, tm, 64), lambda i: (0, i, 0)),
                       pl.BlockSpec((tm, 256), row), _resident(w_out.shape)]
    return pl.pallas_call(
        functools.partial(_ffn_body, k0=k0, final=final, source=source, n_first=n_first),
        grid=(n_tiles,),
        in_specs=[h_spec] + extra_specs + [
            pl.BlockSpec((None, N_MOD, d), lambda i: (group_of(i), 0, 0)),
            _resident((1, d)),
            _resident(w13.shape),
            _resident(w2.shape),
            _resident((1, d))],
        out_specs=pl.BlockSpec((tm, d), row),
        out_shape=jax.ShapeDtypeStruct((n_tiles * tm, d), F32),
        scratch_shapes=[pltpu.VMEM((tm, d), BF16), pltpu.VMEM((tm, ff), BF16),
                        pltpu.VMEM((2, tm, 2 * FF_CHUNK), F32)],
        compiler_params=_params("parallel"),
        name="ffn",
    )(h, *extra, mods, norm_w.reshape(1, d), w13.astype(BF16), w2.astype(BF16), final_w.reshape(1, d))


def _log_sigmoid(x):
    return jnp.minimum(x, 0.0) - jnp.log1p(jnp.exp(-jnp.abs(x)))


def _rope_rotate(x):
    n = x.shape[-1]
    lane = lax.broadcasted_iota(jnp.int32, x.shape, 1)
    up = pltpu.roll(x, n - 8, 1)
    dn = pltpu.roll(x, 8, 1)
    return jnp.where((lane & 15) < 8, -up, dn)


def _rope_rotate_rows(x):
    n = x.shape[0]
    row = lax.broadcasted_iota(jnp.int32, x.shape, 0)
    up = pltpu.roll(x, n - 8, 0)
    dn = pltpu.roll(x, 8, 0)
    return jnp.where((row & 15) < 8, -up, dn)


def _proj_body(h_ref, mod_ref, nw_ref, w_ref, wt_ref, wa_ref, ba_ref,
               cos_ref, sin_ref, cost_ref, sint_ref,
               gqk_ref, gv_ref, gvt_ref, gg_ref, glg_ref, nq_ref, nk_ref, nv_ref,
               dqt_ref, dk_ref, dvt_ref, cu_ref, xb_ref):
    x = h_ref[...]
    tm = x.shape[0]
    xm = _rms(x, nw_ref[...]) * (1.0 + mod_ref[4:5, :]) + mod_ref[3:4, :]
    xb_ref[...] = xm.astype(BF16)

    z = _dot(xb_ref[...], w_ref[:, C_GLA:C_GLA + 256])
    lane = lax.broadcasted_iota(jnp.int32, (1, 256), 1)
    gqk_ref[...] = z * jnp.where(lane < 128, GLA_DK ** -0.5, 1.0)
    gv_ref[...] = _dot(xb_ref[...], w_ref[:, C_GLA + 256:C_GLA + 512]).astype(BF16)
    zt = _dot_nt(wt_ref[...], xb_ref[...])
    gvt_ref[...] = zt[0:256, :].astype(BF16)
    gg_ref[...] = _dot(xb_ref[...], w_ref[:, C_GLA + 512:C_GLA + 768]).astype(BF16)
    aux = _dot(xb_ref[...], w_ref[:, C_AUX:C_END])
    pre = _dot(aux.astype(BF16), wa_ref[...]) + ba_ref[...]
    glg_ref[...] = _log_sigmoid(pre) * (1.0 / GLA_TAU)

    nq_ref[...] = (_dot(xb_ref[...], w_ref[:, C_NA:C_NA + 256]) * (NA_DIM ** -0.5 * LOG2E)).astype(BF16)
    nk_ref[...] = _dot(xb_ref[...], w_ref[:, C_NA + 256:C_NA + 512]).astype(BF16)
    nv_ref[...] = _dot(xb_ref[...], w_ref[:, C_NA + 512:C_NA + 768]).astype(BF16)

    cos = cos_ref[...]
    sin = sin_ref[...]
    cos2 = jnp.concatenate([cos, cos], axis=1)
    sin2 = jnp.concatenate([sin, sin], axis=1)
    zk = _dot(xb_ref[...], w_ref[:, C_DIFF + 256:C_DIFF + 512])
    zk = zk * cos2 + _rope_rotate(zk) * sin2
    n_rep = 2 * DIFF_HEADS
    cos_t = jnp.concatenate([cost_ref[...]] * n_rep, axis=0)
    sin_t = jnp.concatenate([sint_ref[...]] * n_rep, axis=0)
    zqt = zt[256:512, :]
    zqt = (zqt * cos_t + _rope_rotate_rows(zqt) * sin_t) * (DIFF_DK ** -0.5 * LOG2E)
    zvt = zt[512:768, :]
    pad_rows = DIFF_VROWS - DIFF_DV
    one_row = jnp.where(lax.broadcasted_iota(jnp.int32, (pad_rows, tm), 0) == 0, 1.0, 0.0).astype(BF16)
    for hh in range(DIFF_HEADS):
        sl = slice(64 * hh, 64 * hh + 64)
        dk_ref[hh] = zk[:, sl].astype(BF16)
        dqt_ref[hh] = zqt[sl, :].astype(BF16)
        dvt_ref[hh, 0:64, :] = zvt[sl, :].astype(BF16)
        dvt_ref[hh, DIFF_DV:DIFF_VROWS, :] = one_row

    za = _dot(xb_ref[...], w_ref[:, C_CONV:C_CONV + 256])
    zg = _dot(xb_ref[...], w_ref[:, C_CONV + 256:C_CONV + 512])
    cu_ref[...] = za * jax.nn.sigmoid(zg)


def _proj(h, mods, norm_w, w_p, wa, ba, rope, *, n_tiles, group_of, pos_of):
    r, d = h.shape
    tm = ROW_TILE
    row = lambda i: (i, 0)
    hrow = lambda i: (0, i, 0)
    hcol = lambda i: (0, 0, i)
    f32_256 = jax.ShapeDtypeStruct((r, 256), F32)
    bf_256 = jax.ShapeDtypeStruct((r, 256), BF16)
    w_t = jnp.concatenate([w_p[:, C_GLA + 256:C_GLA + 512], w_p[:, C_DIFF:C_DIFF + 256],
                           w_p[:, C_DIFF + 512:C_DIFF + 768]], axis=1).T
    cos_r, sin_r, cos_c, sin_c = rope
    out_shape = [f32_256, bf_256, jax.ShapeDtypeStruct((256, r), BF16), bf_256, f32_256,
                 bf_256, bf_256, bf_256,
                 jax.ShapeDtypeStruct((DIFF_HEADS, 64, r), BF16),
                 jax.ShapeDtypeStruct((DIFF_HEADS, r, 64), BF16),
                 jax.ShapeDtypeStruct((DIFF_HEADS, DIFF_VROWS, r), BF16),
                 f32_256]
    out_specs = [pl.BlockSpec((tm, 256), row)] * 2 + [pl.BlockSpec((256, tm), lambda i: (0, i))] + [
        pl.BlockSpec((tm, 256), row)] * 5 + [
        pl.BlockSpec((DIFF_HEADS, 64, tm), hcol),
        pl.BlockSpec((DIFF_HEADS, tm, 64), hrow),
        pl.BlockSpec((DIFF_HEADS, DIFF_VROWS, tm), hcol),
        pl.BlockSpec((tm, 256), row)]
    return pl.pallas_call(
        _proj_body,
        grid=(n_tiles,),
        in_specs=[pl.BlockSpec((tm, d), row),
                  pl.BlockSpec((None, N_MOD, d), lambda i: (group_of(i), 0, 0)),
                  _resident((1, d)),
                  _resident(w_p.shape),
                  _resident(w_t.shape),
                  _resident(wa.shape),
                  _resident(ba.shape),
                  pl.BlockSpec((tm, LANES), lambda i: (pos_of(i), 0)),
                  pl.BlockSpec((tm, LANES), lambda i: (pos_of(i), 0)),
                  pl.BlockSpec((DIFF_DK, tm), lambda i: (0, pos_of(i))),
                  pl.BlockSpec((DIFF_DK, tm), lambda i: (0, pos_of(i)))],
        out_specs=out_specs,
        out_shape=out_shape,
        scratch_shapes=[pltpu.VMEM((tm, d), BF16)],
        compiler_params=_params("parallel"),
        name="proj",
    )(h, mods, norm_w.reshape(1, d), w_p, w_t, wa, ba, cos_r, sin_r, cos_c, sin_c)


def _permute_w_in(w_in):
    d = w_in.shape[0]
    g0 = 2 * GLA_HEADS * GLA_DK + 2 * GLA_HEADS * GLA_DV
    aux = w_in[:, g0:g0 + 2 * GLA_RANK]
    rest = w_in[:, g0 + 2 * GLA_RANK:]
    pad = jnp.zeros((d, C_END - C_AUX - 2 * GLA_RANK), w_in.dtype)
    return jnp.concatenate([w_in[:, :g0], rest, aux, pad], axis=1).astype(BF16)


def _gate_weights(wa_f, ba_f, wa_b, ba_b):
    n = GLA_HEADS * GLA_DK
    wa = jnp.zeros((C_END - C_AUX, 2 * n), F32)
    wa = wa.at[:GLA_RANK, :n].set(wa_f).at[GLA_RANK:2 * GLA_RANK, n:].set(wa_b)
    return wa.astype(BF16), jnp.concatenate([ba_f, ba_b]).reshape(1, 2 * n)


def _rope_tables(seq, tile):
    t = jnp.arange(seq)
    row = (t // GRID_W).astype(F32)
    col = (t % GRID_W).astype(F32)
    half = DIFF_DK // 2
    inv = 1.0 / (ROPE_BASE ** (jnp.arange(0, half, 2, dtype=F32) / half))
    ang_r = row[:, None] * inv
    ang_c = col[:, None] * inv
    ang = jnp.concatenate([ang_r, ang_r, ang_c, ang_c], axis=-1)
    cos = jnp.concatenate([jnp.cos(ang), jnp.ones((tile, DIFF_DK), F32)], axis=0)
    sin = jnp.concatenate([jnp.sin(ang), jnp.zeros((tile, DIFF_DK), F32)], axis=0)
    rep = (1, LANES // DIFF_DK)
    return jnp.tile(cos, rep), jnp.tile(sin, rep), cos.T, sin.T


def _split3(x):
    hi = x.astype(BF16)
    r1 = x - hi.astype(F32)
    mid = r1.astype(BF16)
    lo = (r1 - mid.astype(F32)).astype(BF16)
    return hi, mid, lo


def _gla_body(*refs, reverse, batch):
    per_b = [refs[5 * b:5 * b + 5] for b in range(batch)]
    of_ref, nw_ref, o_ref, st_ref = refs[5 * batch:]
    blk = per_b[0][0].shape[0]
    n_chunks = blk // GLA_CHUNK
    nqk = GLA_HEADS * GLA_DK
    nv = GLA_HEADS * GLA_DV

    @pl.when(pl.program_id(0) == 0)
    def _():
        st_ref[...] = jnp.zeros_like(st_ref)

    ri = lax.broadcasted_iota(jnp.int32, (blk, blk), 0)
    ci = lax.broadcasted_iota(jnp.int32, (blk, blk), 1)
    same = (ri // GLA_CHUNK) == (ci // GLA_CHUNK)
    tri = jnp.where(same & ((ci >= ri) if reverse else (ci <= ri)), 1.0, 0.0).astype(BF16)
    bcum_of = []
    for qk_ref, v_ref, vt_ref, lg_ref, g_ref in per_b:
        lg = lg_ref[:, nqk:2 * nqk] if reverse else lg_ref[:, 0:nqk]
        hi, mid, lo = _split3(lg)
        bcum_of.append(_dot(tri, hi) + _dot(tri, mid) + _dot(tri, lo))

    c = GLA_CHUNK
    rk = lax.broadcasted_iota(jnp.int32, (GLA_HEADS * c, nqk), 0) // c
    ck = lax.broadcasted_iota(jnp.int32, (GLA_HEADS * c, nqk), 1) // GLA_DK
    mask_k = rk == ck
    rv = lax.broadcasted_iota(jnp.int32, (GLA_HEADS * c, nv), 0) // c
    cv = lax.broadcasted_iota(jnp.int32, (GLA_HEADS * c, nv), 1) // GLA_DV
    mask_v = rv == cv
    rs = lax.broadcasted_iota(jnp.int32, (nv, nqk), 0) // GLA_DV
    cs = lax.broadcasted_iota(jnp.int32, (nv, nqk), 1) // GLA_DK
    mask_s = rs == cs
    ai = lax.broadcasted_iota(jnp.int32, (c, GLA_HEADS * c), 0)
    aj = lax.broadcasted_iota(jnp.int32, (c, GLA_HEADS * c), 1) % c
    mask_a = (aj >= ai) if reverse else (aj <= ai)

    order = range(n_chunks - 1, -1, -1) if reverse else range(n_chunks)
    units = [(b, ch) for ch in order for b in range(batch)]
    q_in, a_raw, u_t, decay = {}, {}, {}, {}
    vts = [refs_b[2][...].astype(BF16) for refs_b in per_b]
    for b, ch in units:
        qk_ref = per_b[b][0]
        rows = slice(ch * c, (ch + 1) * c)
        k = qk_ref[rows, nqk:2 * nqk]
        bcum = bcum_of[b][rows, :]
        btot = bcum[0:1, :] if reverse else bcum[c - 1:c, :]
        q_in[b, ch] = (qk_ref[rows, 0:nqk] * jnp.exp(bcum)).astype(BF16)
        k_in = k * jnp.exp(-bcum)
        k_out = (k * jnp.exp(btot - bcum)).astype(BF16)
        k_bd = jnp.where(mask_k, jnp.concatenate([k_in] * GLA_HEADS, axis=0), 0.0).astype(BF16)
        a_raw[b, ch] = _dot_nt(q_in[b, ch], k_bd)
        k_pad = jnp.concatenate([jnp.zeros((n * c, nqk), BF16) for n in (ch,) if n] + [k_out]
                                + [jnp.zeros((n * c, nqk), BF16) for n in (n_chunks - 1 - ch,) if n], axis=0)
        u_t[b, ch] = jnp.where(mask_s, _dot(vts[b], k_pad), 0.0)
        decay[b, ch] = jnp.exp(btot)
    o = {}
    for b, ch in units:
        v = per_b[b][1][ch * c:(ch + 1) * c, :]
        v4 = jnp.concatenate([v] * GLA_HEADS, axis=0)
        v_bd = jnp.where(mask_v, v4, jnp.zeros_like(v4))
        o[b, ch] = _dot(jnp.where(mask_a, a_raw[b, ch], 0.0).astype(BF16), v_bd)
    st = [st_ref[b] for b in range(batch)]
    for b, ch in units:
        o[b, ch] = o[b, ch] + _dot_nt(q_in[b, ch], st[b].astype(BF16))
        st[b] = st[b] * decay[b, ch] + u_t[b, ch]
    for b in range(batch):
        st_ref[b] = st[b]
        o_blk = jnp.concatenate([o[b, ch] for ch in range(n_chunks)], axis=0)
        if reverse:
            o_blk = o_blk + of_ref[b]
            hi2, lo2, _ = _split3(o_blk * o_blk)
            hr = lax.broadcasted_iota(jnp.int32, (nv, nv), 0) // GLA_DV
            hc = lax.broadcasted_iota(jnp.int32, (nv, nv), 1) // GLA_DV
            seg = jnp.where(hr == hc, 1.0, 0.0).astype(BF16)
            ms = (_dot(hi2, seg) + _dot(lo2, seg)) * (1.0 / GLA_DV)
            o_blk = o_blk * lax.rsqrt(ms + RMS_EPS) * nw_ref[...] * _silu(per_b[b][4][...].astype(F32))
        o_ref[b] = o_blk.astype(o_ref.dtype)


def _gla(gqk, gv, gvt, glg, gg, o_f, norm_w4, *, reverse, batch, seq, ctx):
    blk = GLA_BLOCK
    nc, nl = ctx // blk, seq // blk
    ctx_base = batch * seq // blk

    def step_blk(s):
        if reverse:
            return jnp.where(s < nc, nl + (nc - 1 - s), nl - 1 - (s - nc))
        return jnp.where(s < nc, nl + s, s - nc)

    def row_blk(b):
        return lambda s: jnp.where(s < nc, ctx_base + b * nc - nl, b * nl) + step_blk(s)

    specs, args = [], []
    for b in range(batch):
        spec = pl.BlockSpec((blk, 256), lambda s, f=row_blk(b): (f(s), 0))
        spec_t = pl.BlockSpec((256, blk), lambda s, f=row_blk(b): (0, f(s)))
        if reverse:
            specs += [spec, spec, spec_t, spec, spec]
            args += [gqk, gv, gvt, glg, gg]
        else:
            specs += [spec, spec, spec_t, spec, _resident((1, 256))]
            args += [gqk, gv, gvt, glg, norm_w4]
    seq_spec = pl.BlockSpec((batch, blk, 256), lambda s: (0, step_blk(s), 0))
    if o_f is None:
        o_f, of_spec = norm_w4, _resident((1, 256))
    else:
        of_spec = seq_spec
    return pl.pallas_call(
        functools.partial(_gla_body, reverse=reverse, batch=batch),
        grid=(nc + nl,),
        in_specs=specs + [of_spec, _resident((1, 256))],
        out_specs=seq_spec,
        out_shape=jax.ShapeDtypeStruct((batch, seq + ctx, 256), BF16 if reverse else F32),
        scratch_shapes=[pltpu.VMEM((batch, GLA_HEADS * GLA_DV, GLA_HEADS * GLA_DK), F32)],
        compiler_params=_params("arbitrary"),
        name="gla_bwd" if reverse else "gla_fwd",
    )(*args, o_f, norm_w4)


def _na_col_tables(rpb):
    col = jnp.arange(GRID_W)
    cstart = jnp.clip(col - NA_COLS // 2, 0, GRID_W - NA_COLS)
    col_ok = (col[None, :] >= cstart[:, None]) & (col[None, :] < cstart[:, None] + NA_COLS)
    col_off = jnp.clip(col[None, :] - col[:, None] + (NA_COLS - 1), 0, 2 * NA_COLS - 2)
    t = jnp.where(col_ok[None, None], rpb[:, :, col_off] * LOG2E, NEG_INF)
    dead = jnp.full((NA_HEADS, 1, GRID_W, GRID_W), NEG_INF, F32)
    t = jnp.concatenate([dead, t.astype(F32), dead], axis=1)
    return jnp.concatenate([t[:, :-1], t[:, 1:]], axis=-1)


def _na_body(q_ref, k0, k1, k2, k3, v0, v1, v2, v3, kc_ref, vc_ref, tab_ref, o_ref, *, rows, key_blocks):
    k_blk = [k0[...], k1[...], k2[...], k3[...]]
    v_blk = [v0[...], v1[...], v2[...], v3[...]]
    kc = kc_ref[...]
    vc = vc_ref[...]

    j = pl.program_id(1)
    rows_per_blk = NA_KBLK // GRID_W
    r0 = j * NA_QROWS
    kr0 = jnp.clip(2 * j - 1, 0, key_blocks - 4) * rows_per_blk
    lane_lo = lax.broadcasted_iota(jnp.int32, (1, 2 * GRID_W), 1) < GRID_W
    lane = lax.broadcasted_iota(jnp.int32, (1, NA_HEADS * NA_DIM), 1) // NA_DIM
    half = NA_QROWS // 2
    n_pairs = 3 * rows_per_blk // 2
    gq = half * GRID_W
    for g in range(2):
        first = jnp.clip(r0 + g * half - NA_ROWS // 2, 0, rows - NA_ROWS)
        shift = jnp.clip((first - kr0) // rows_per_blk, 0, 1)
        kw = jnp.concatenate([jnp.where(shift == 0, k_blk[i], k_blk[i + 1]) for i in range(3)], axis=0)
        vw = jnp.concatenate([jnp.where(shift == 0, v_blk[i], v_blk[i + 1]) for i in range(3)], axis=0)
        kg0 = kr0 + shift * rows_per_blk
        entry, ok = [], []
        for a in range(half):
            r = r0 + g * half + a
            start = jnp.clip(r - NA_ROWS // 2, 0, rows - NA_ROWS)
            for bp in range(n_pairs):
                rk = kg0 + 2 * bp
                entry.append(jnp.clip(rk - r + NA_ROWS, 0, 2 * NA_ROWS - 1))
                in0 = ((rk >= start) & (rk < start + NA_ROWS)).astype(jnp.int32)
                in1 = ((rk + 1 >= start) & (rk + 1 < start + NA_ROWS)).astype(jnp.int32)
                ok.append(jnp.where(lane_lo, in0, in1) != 0)

        def bias_of(hh):
            rows_ = []
            for a in range(half):
                tiles = [jnp.where(ok[a * n_pairs + bp], tab_ref[hh, entry[a * n_pairs + bp]], NEG_INF)
                         for bp in range(n_pairs)]
                rows_.append(jnp.concatenate(tiles, axis=1))
            return jnp.concatenate(rows_, axis=0)

        q = q_ref[g * gq:(g + 1) * gq, :]
        out = jnp.zeros((gq, NA_HEADS * NA_DIM), F32)
        for hh in range(NA_HEADS):
            mh = lane == hh
            qh = jnp.where(mh, q, jnp.zeros_like(q))
            s_w = _dot_nt(qh, kw) + bias_of(hh)
            s_c = _dot_nt(qh, kc)
            m = jnp.maximum(jnp.max(s_w, axis=-1, keepdims=True), jnp.max(s_c, axis=-1, keepdims=True))
            p_w = jnp.exp2(s_w - m)
            p_c = jnp.exp2(s_c - m)
            l = jnp.sum(p_w, axis=-1, keepdims=True) + jnp.sum(p_c, axis=-1, keepdims=True)
            o = _dot(p_w.astype(BF16), vw) + _dot(p_c.astype(BF16), vc)
            out = out + jnp.where(mh, o / l, 0.0)
        o_ref[g * gq:(g + 1) * gq, :] = out.astype(o_ref.dtype)


def _na(nq, nk, nv, tab, *, batch, seq, ctx):
    r = nq.shape[0]
    tq = NA_QROWS * GRID_W
    nj = seq // tq
    kb = seq // NA_KBLK
    assert kb >= 4 and seq % tq == 0 and seq % ctx == 0
    ctx_blk0 = batch * seq // ctx

    def kspec(i):
        return pl.BlockSpec((NA_KBLK, 256),
                            lambda b, j: (b * kb + jnp.clip(2 * j - 1, 0, kb - 4) + i, 0))

    cspec = pl.BlockSpec((ctx, 256), lambda b, j: (ctx_blk0 + b, 0))
    return pl.pallas_call(
        functools.partial(_na_body, rows=seq // GRID_W, key_blocks=kb),
        grid=(batch, nj),
        in_specs=[pl.BlockSpec((tq, 256), lambda b, j: (b * nj + j, 0))]
                 + [kspec(i) for i in range(4)] + [kspec(i) for i in range(4)]
                 + [cspec, cspec, _resident(tab.shape)],
        out_specs=pl.BlockSpec((tq, 256), lambda b, j: (b * nj + j, 0)),
        out_shape=jax.ShapeDtypeStruct((r, 256), BF16),
        compiler_params=_params("parallel", "arbitrary"),
        name="na",
    )(nq, nk, nk, nk, nk, nv, nv, nv, nv, nk, nv, tab)


def _diff_lambda(lq_ref, lambda_init):
    lq = lq_ref[...]
    s1 = jnp.sum(lq[0:1, :] * lq[1:2, :], axis=-1, keepdims=True)
    s2 = jnp.sum(lq[2:3, :] * lq[3:4, :], axis=-1, keepdims=True)
    return jnp.exp(s1) - jnp.exp(s2) + lambda_init


def _stack_qt(qt):
    row = lax.broadcasted_iota(jnp.int32, (2 * DIFF_DK, 1), 0)
    zero = jnp.zeros_like(qt)
    return jnp.concatenate([jnp.where(row < DIFF_DK, qt, zero), jnp.where(row >= DIFF_DK, qt, zero)], axis=1)


def _diff_finish_t(acc, tq, lam, nw_col, lambda_init):
    o = acc[0:DIFF_DV, :] / acc[DIFF_DV:DIFF_DV + 1, :]
    od = o[:, :tq] - lam * o[:, tq:]
    ms = jnp.mean(od * od, axis=0, keepdims=True)
    y = od * lax.rsqrt(ms + RMS_EPS) * nw_col * (1.0 - lambda_init)
    y = jnp.concatenate([y, jnp.zeros_like(y)], axis=0)
    return jnp.transpose(y)[:, 0:DIFF_DV]


def _diff_body(qt_ref, kc_ref, vtc_ref, k_ref, vt_ref, lq_ref, nw_ref, o_ref, s_ref, *, tq, tk, unroll, lambda_init):
    def tile(t, carry):
        q0 = pl.multiple_of(t * tq, tq)
        out = _diff_tile(qt_ref[:, pl.ds(q0, tq)], kc_ref, vtc_ref, k_ref, vt_ref, lq_ref, nw_ref, s_ref,
                         tk=tk, unroll=unroll, lambda_init=lambda_init)
        o_ref[pl.ds(q0, tq), :] = out.astype(o_ref.dtype)
        return carry

    lax.fori_loop(0, qt_ref.shape[1] // tq, tile, 0)


def _diff_tile(qt, kc_ref, vtc_ref, k_ref, vt_ref, lq_ref, nw_ref, s_ref, *, tk, unroll, lambda_init):
    tq = qt.shape[1]
    n_chunks = k_ref.shape[0] // tk
    ring = s_ref.shape[0]
    qs = _stack_qt(qt)

    def chunk(j):
        return pl.ds(j * tk if isinstance(j, int) else pl.multiple_of(j * tk, tk), tk)

    def issue_scores(j, slot):
        s = _dot(k_ref[chunk(j), :], qs)
        s_ref[slot] = s
        return jnp.max(s, axis=0, keepdims=True)

    s_ctx = _dot(kc_ref[...], qs)
    m0 = jnp.max(s_ctx, axis=0, keepdims=True)
    cmax0 = tuple(issue_scores(min(a, n_chunks - 1), a % ring) for a in range(DIFF_AHEAD))
    acc0 = _dot(vtc_ref[...], jnp.exp2(s_ctx - m0).astype(BF16))

    def group(g, carry, tail=False):
        cmax, m, acc = carry
        for u in range(unroll):
            j = g * unroll + u
            if tail and j + DIFF_AHEAD >= n_chunks:
                c_new = cmax[0]
            else:
                c_new = issue_scores(j + DIFF_AHEAD, (u + DIFF_AHEAD) % ring)
            m_new = jnp.maximum(m, cmax[0])
            p = jnp.exp2(s_ref[u % ring] - m_new).astype(BF16)
            vt = vt_ref[:, chunk(j)]
            acc = jnp.exp2(m - m_new) * acc + _dot(vt, p)
            m, cmax = m_new, cmax[1:] + (c_new,)
        return cmax, m, acc

    n_groups = n_chunks // unroll
    trips = jnp.minimum(pl.program_id(2) + 1, 1) * (n_groups - 1)
    carry = lax.fori_loop(0, trips, group, (cmax0, m0, acc0))
    _, _, acc = group(n_groups - 1, carry, tail=True)
    return _diff_finish_t(acc, tq, _diff_lambda(lq_ref, lambda_init), nw_ref[...], lambda_init)


def _diff(dqt, dk, dvt, lq, norm_w, *, batch, seq, ctx, lambda_init):
    h, r, _ = dk.shape
    tq = DIFF_TQ
    tqs = tq * min(DIFF_QTILES, seq // tq)
    nq = seq // tqs
    tk = min(DIFF_TK, seq)
    unroll = min(DIFF_UNROLL, seq // tk)
    assert (seq // tk) % unroll == 0 and unroll % DIFF_RING == 0 and DIFF_RING > DIFF_AHEAD and seq % tqs == 0
    ctx_blk0 = batch * seq // ctx
    return pl.pallas_call(
        functools.partial(_diff_body, tq=tq, tk=tk, unroll=unroll, lambda_init=lambda_init),
        grid=(batch, h, nq),
        in_specs=[pl.BlockSpec((None, 64, tqs), lambda b, hh, i: (hh, 0, b * nq + i)),
                  pl.BlockSpec((None, ctx, 64), lambda b, hh, i: (hh, ctx_blk0 + b, 0)),
                  pl.BlockSpec((None, DIFF_VROWS, ctx), lambda b, hh, i: (hh, 0, ctx_blk0 + b)),
                  pl.BlockSpec((None, seq, 64), lambda b, hh, i: (hh, b, 0)),
                  pl.BlockSpec((None, DIFF_VROWS, seq), lambda b, hh, i: (hh, 0, b)),
                  _resident(lq.shape),
                  _resident((DIFF_DV, 1))],
        out_specs=pl.BlockSpec((None, tqs, 64), lambda b, hh, i: (hh, b * nq + i, 0)),
        out_shape=jax.ShapeDtypeStruct((h, r, 64), BF16),
        scratch_shapes=[pltpu.VMEM((DIFF_RING, tk, 2 * tq), F32)],
        compiler_params=_params("parallel", "parallel", "arbitrary"),
        name="diff",
    )(dqt, dk, dvt, dk, dvt, lq, norm_w.reshape(DIFF_DV, 1))


def _ctx_attn_body(nq_ref, nk_ref, nv_ref, dqt_ref, dk_ref, dvt_ref, lq_ref, nw_ref, na_in, df_in,
                   na_out, df_out, *, lambda_init):
    del na_in, df_in
    q = nq_ref[...]
    k = nk_ref[...]
    v = nv_ref[...]
    lane = lax.broadcasted_iota(jnp.int32, (1, NA_HEADS * NA_DIM), 1) // NA_DIM
    out = jnp.zeros(na_out.shape, F32)
    for hh in range(NA_HEADS):
        mh = lane == hh
        s = _dot_nt(jnp.where(mh, q, jnp.zeros_like(q)), k)
        p = jnp.exp2(s - jnp.max(s, axis=-1, keepdims=True))
        o = _dot(p.astype(BF16), v) / jnp.sum(p, axis=-1, keepdims=True)
        out = out + jnp.where(mh, o, 0.0)
    na_out[...] = out.astype(na_out.dtype)

    lam = _diff_lambda(lq_ref, lambda_init)
    tq = dqt_ref.shape[2]
    for hh in range(DIFF_HEADS):
        s = _dot(dk_ref[hh], _stack_qt(dqt_ref[hh]))
        p = jnp.exp2(s - jnp.max(s, axis=0, keepdims=True))
        acc = _dot(dvt_ref[hh], p.astype(BF16))
        df_out[hh] = _diff_finish_t(acc, tq, lam, nw_ref[...], lambda_init).astype(df_out.dtype)


def _ctx_attn(nq, nk, nv, dqt, dk, dvt, lq, norm_w, na_o, df_o, *, batch, seq, ctx, lambda_init):
    blk0 = batch * seq // ctx
    s256 = pl.BlockSpec((ctx, 256), lambda b: (blk0 + b, 0))
    sh64 = pl.BlockSpec((DIFF_HEADS, ctx, 64), lambda b: (0, blk0 + b, 0))
    sq_t = pl.BlockSpec((DIFF_HEADS, 64, ctx), lambda b: (0, 0, blk0 + b))
    sv_t = pl.BlockSpec((DIFF_HEADS, DIFF_VROWS, ctx), lambda b: (0, 0, blk0 + b))
    return pl.pallas_call(
        functools.partial(_ctx_attn_body, lambda_init=lambda_init),
        grid=(batch,),
        in_specs=[s256, s256, s256, sq_t, sh64, sv_t, _resident(lq.shape), _resident((DIFF_DV, 1)),
                  pl.BlockSpec(memory_space=pl.ANY), pl.BlockSpec(memory_space=pl.ANY)],
        out_specs=[s256, sh64],
        out_shape=[jax.ShapeDtypeStruct(na_o.shape, na_o.dtype), jax.ShapeDtypeStruct(df_o.shape, df_o.dtype)],
        input_output_aliases={8: 0, 9: 1},
        compiler_params=_params("arbitrary"),
        name="ctx_attn",
    )(nq, nk, nv, dqt, dk, dvt, lq, norm_w.reshape(DIFF_DV, 1), na_o, df_o)


def _conv_body(u_ref, prev_ref, next_ref, dw_ref, dwb_ref, lng_ref, lnb_ref, pw_ref, pwb_ref,
               o_ref, pad_ref, sh_ref, *, tiles_per_seq, n_latent_tiles):
    t = u_ref.shape[0]
    i = pl.program_id(0)
    in_ctx = i >= n_latent_tiles
    first = in_ctx | (i % tiles_per_seq == 0)
    last = in_ctx | (i % tiles_per_seq == tiles_per_seq - 1)
    pad_ref[0:HALO, :] = jnp.where(first, 0.0, prev_ref[...])
    pad_ref[HALO:HALO + t, :] = u_ref[...]
    pad_ref[HALO + t:HALO + t + HALO, :] = jnp.where(last, 0.0, next_ref[...])
    base = HALO - CONV_K // 2
    first = {}
    for phase in range(8):
        taps = [k for k in range(CONV_K) if (base + k) % 8 == phase]
        first[phase] = base + taps[0]
        span = base + taps[-1] + t - first[phase]
        sh_ref[phase, 0:span, :] = pad_ref[first[phase]:first[phase] + span, :]
    sub = 64
    parts = []
    for r0 in range(0, t, sub):
        acc = jnp.zeros((sub, CONV_CH), F32)
        for k in range(CONV_K):
            phase = (base + k) % 8
            off = base + k - first[phase] + r0
            acc = acc + dw_ref[k:k + 1, :] * sh_ref[phase, off:off + sub, :]
        parts.append(acc)
    y = jnp.concatenate(parts, axis=0) + dwb_ref[...]
    mu = jnp.mean(y, axis=-1, keepdims=True)
    yc = y - mu
    var = jnp.mean(yc * yc, axis=-1, keepdims=True)
    y = _silu(yc * lax.rsqrt(var + LN_EPS) * lng_ref[...] + lnb_ref[...])
    o_ref[...] = (_dot(y.astype(BF16), pw_ref[...]) + pwb_ref[...]).astype(o_ref.dtype)


def _conv(cu, dw, dw_b, ln_g, ln_b, pw, pw_b, *, batch, seq, ctx):
    r = cu.shape[0]
    t = CONV_TILE
    assert ctx == t and seq % t == 0
    n_tiles = r // t
    hb = t // HALO
    vec = lambda a: a.reshape(1, CONV_CH)
    return pl.pallas_call(
        functools.partial(_conv_body, tiles_per_seq=seq // t, n_latent_tiles=batch * seq // t),
        grid=(n_tiles,),
        in_specs=[pl.BlockSpec((t, CONV_CH), lambda i: (i, 0)),
                  pl.BlockSpec((HALO, CONV_CH), lambda i: (jnp.maximum(i * hb - 1, 0), 0)),
                  pl.BlockSpec((HALO, CONV_CH), lambda i: (jnp.minimum((i + 1) * hb, n_tiles * hb - 1), 0)),
                  _resident((CONV_K, CONV_CH))] + [_resident((1, CONV_CH))] * 3
                 + [_resident((CONV_CH, CONV_CH)), _resident((1, CONV_CH))],
        out_specs=pl.BlockSpec((t, CONV_CH), lambda i: (i, 0)),
        out_shape=jax.ShapeDtypeStruct((r, CONV_CH), BF16),
        scratch_shapes=[pltpu.VMEM((t + 2 * HALO, CONV_CH), F32),
                        pltpu.VMEM((8, t + 2 * HALO, CONV_CH), F32)],
        compiler_params=_params("parallel"),
        name="conv",
    )(cu, cu, cu, dw, vec(dw_b), vec(ln_g), vec(ln_b), pw.astype(BF16), vec(pw_b))


def kernel(x, c, ctx, c_ctx, ada_w, ada_b, norm_ffn1, ffn1_w13, ffn1_w2, norm_mix, w_in, gla_wa_f, gla_ba_f, gla_wa_b, gla_ba_b, gla_norm, na_rpb, diff_lq1, diff_lk1, diff_lq2, diff_lk2, diff_norm, conv_dw, conv_dw_b, conv_ln_g, conv_ln_b, conv_pw, conv_pw_b, w_out, norm_ffn2, ffn2_w13, ffn2_w2, final_norm):
    batch, seq, d = x.shape
    n_ctx = ctx.shape[1]
    depth = ada_w.shape[0]
    tm = ROW_TILE
    assert seq % tm == 0 and (batch * n_ctx) % tm == 0 and batch + 1 <= 8
    lat_tiles = batch * seq // tm
    all_tiles = lat_tiles + batch * n_ctx // tm
    tiles_per_batch = seq // tm

    def group_of(i):
        return jnp.minimum(i // tiles_per_batch, batch)

    def pos_of(i):
        return jnp.where(i < lat_tiles, i % tiles_per_batch, tiles_per_batch)

    c_rows = jnp.concatenate([c, c_ctx[None, :], jnp.zeros((8 - batch - 1, d), F32)], axis=0)
    mods_all = _ada(c_rows, ada_w, ada_b)[:, :batch + 1].reshape(depth, batch + 1, N_MOD, d)
    rope = _rope_tables(seq, tm)
    h = x.reshape(batch * seq, d)
    h_ctx = ctx.reshape(batch * n_ctx, d)

    for i in range(depth):
        last = i == depth - 1
        lambda_init = 0.8 - 0.6 * math.exp(-0.3 * i)
        mods = mods_all[i]
        tok = dict(n_tiles=all_tiles, group_of=group_of)
        geo = dict(batch=batch, seq=seq, ctx=n_ctx)

        h = _ffn(h, mods, norm_ffn1[i], ffn1_w13[i], ffn1_w2[i], final_norm, k0=0, final=False,
                 h_ctx=h_ctx if i == 0 else None, **tok)

        wa, ba = _gate_weights(gla_wa_f[i], gla_ba_f[i], gla_wa_b[i], gla_ba_b[i])
        (gqk, gv, gvt, gg, glg, nq, nk, nv, dq, dk, dv, cu) = _proj(
            h, mods, norm_mix[i], _permute_w_in(w_in[i]), wa, ba, rope, pos_of=pos_of, **tok)

        gnorm = jnp.tile(gla_norm[i], GLA_HEADS).reshape(1, GLA_HEADS * GLA_DV)
        o_f = _gla(gqk, gv, gvt, glg, gg, None, gnorm, reverse=False, **geo)
        gx = _gla(gqk, gv, gvt, glg, gg, o_f, gnorm, reverse=True, **geo)

        nx = _na(nq, nk, nv, _na_col_tables(na_rpb[i]), **geo)

        lq = jnp.stack([diff_lq1[i], diff_lk1[i], diff_lq2[i], diff_lk2[i]])
        dx = _diff(dq, dk, dv, lq, diff_norm[i], lambda_init=lambda_init, **geo)
        if not last:
            nx, dx = _ctx_attn(nq, nk, nv, dq, dk, dv, lq, diff_norm[i], nx, dx,
                               lambda_init=lambda_init, **geo)

        cx = _conv(cu, conv_dw[i], conv_dw_b[i], conv_ln_g[i], conv_ln_b[i], conv_pw[i], conv_pw_b[i], **geo)

        if last:
            tok = dict(n_tiles=lat_tiles, group_of=group_of)
        h = _ffn(h, mods, norm_ffn2[i], ffn2_w13[i], ffn2_w2[i], final_norm, k0=6, final=last,
                 mixers=(gx, nx, dx, cx, w_out[i], (batch, seq, n_ctx)), **tok)

    return h.reshape(batch, seq, d)
```

```python
import functools
import math

import jax
import jax.numpy as jnp
from jax import lax
from jax.experimental import pallas as pl
from jax.experimental.pallas import tpu as pltpu

F32 = jnp.float32
BF16 = jnp.bfloat16

GRID_W = 64
N_MOD = 9
RMS_EPS = 1e-6
LN_EPS = 1e-5
NEG_INF = -1e30
GLA_HEADS, GLA_DK, GLA_DV, GLA_RANK, GLA_TAU, GLA_CHUNK = 4, 32, 64, 16, 16.0, 64
NA_HEADS, NA_DIM, NA_ROWS, NA_COLS = 4, 64, 8, 16
DIFF_HEADS, DIFF_DK, DIFF_DV = 4, 32, 64
DIFF_VROWS = 80
CONV_CH, CONV_K = 256, 31
ROPE_BASE = 10000.0
LOG2E = 1.4426950408889634

LANES = 128
VMEM_LIMIT = 56 * 1024 * 1024

ROW_TILE = 512
FF_CHUNK = 256
GLA_BLOCK = 256
NA_QROWS = 8
NA_KBLK = 256
CONV_TILE = 256
DIFF_TQ = 256
DIFF_QTILES = 8
DIFF_TK = 256
DIFF_UNROLL = 32
DIFF_AHEAD = 3
DIFF_RING = 4
HALO = 16

C_GLA, C_NA, C_DIFF, C_CONV, C_AUX, C_END = 0, 768, 1536, 2304, 2816, 2944


def _dot(a, b):
    return jnp.dot(a, b, preferred_element_type=F32)


def _dot_nt(a, b):
    return lax.dot_general(a, b, (((1,), (1,)), ((), ())), preferred_element_type=F32)


def _params(*sem):
    return pltpu.CompilerParams(dimension_semantics=sem, vmem_limit_bytes=VMEM_LIMIT)


def _resident(shape):
    nd = len(shape)
    return pl.BlockSpec(shape, lambda *_: (0,) * nd, pipeline_mode=pl.Buffered(1))


def _silu(x):
    return x * jax.nn.sigmoid(x)


def _rms(x, w):
    return x * lax.rsqrt(jnp.mean(x * x, axis=-1, keepdims=True) + RMS_EPS) * w


def _ada_body(c_ref, w_ref, b_ref, o_ref):
    s = _silu(c_ref[...])
    o_ref[...] = jnp.dot(s, w_ref[...], precision=lax.Precision.HIGHEST,
                         preferred_element_type=F32) + b_ref[...]


def _ada(c_rows, ada_w, ada_b):
    depth, d, _ = ada_w.shape
    return pl.pallas_call(
        _ada_body,
        grid=(depth, N_MOD),
        in_specs=[pl.BlockSpec((8, d), lambda l, n: (0, 0)),
                  pl.BlockSpec((None, d, d), lambda l, n: (l, 0, n)),
                  pl.BlockSpec((None, 1, d), lambda l, n: (l, 0, n))],
        out_specs=pl.BlockSpec((None, 8, d), lambda l, n: (l, 0, n)),
        out_shape=jax.ShapeDtypeStruct((depth, 8, N_MOD * d), F32),
        compiler_params=_params("arbitrary", "arbitrary"),
        name="ada",
    )(c_rows, ada_w, ada_b.reshape(depth, 1, N_MOD * d))


def _ffn_body(*refs, k0, final, source, n_first):
    h_ref = refs[0]
    n_extra = {"plain": 0, "two_arrays": 1, "mixers": 6}[source]
    extra = refs[1:1 + n_extra]
    mod_ref, nw_ref, w13_ref, w2_ref, fw_ref, o_ref, xb_ref, g_ref, h13_ref = refs[1 + n_extra:]
    ff = w2_ref.shape[0]
    tf = h13_ref.shape[2] // 2
    n_chunks = ff // tf
    x = h_ref[...]
    if source == "two_arrays":
        x = jnp.where(pl.program_id(0) < n_first, x, extra[0][...])
    elif source == "mixers":
        gx_ref, gxc_ref, nx_ref, dx_ref, cx_ref, wo_ref = extra
        gx = jnp.where(pl.program_id(0) < n_first, gx_ref[...], gxc_ref[...].reshape(gx_ref.shape))
        mix = jnp.concatenate([gx, nx_ref[...]] + [dx_ref[hh] for hh in range(DIFF_HEADS)]
                              + [cx_ref[...]], axis=-1)
        x = x + mod_ref[5:6, :] * _dot(mix, wo_ref[...])
    xm = _rms(x, nw_ref[...]) * (1.0 + mod_ref[k0 + 1:k0 + 2, :]) + mod_ref[k0:k0 + 1, :]
    xb_ref[...] = xm.astype(BF16)

    def cols(c):
        return c * tf if isinstance(c, int) else pl.multiple_of(c * tf, tf)

    def up(c, slot):
        h13_ref[slot, :, :tf] = _dot(xb_ref[...], w13_ref[:, pl.ds(cols(c), tf)])
        h13_ref[slot, :, tf:] = _dot(xb_ref[...], w13_ref[:, pl.ds(ff + cols(c), tf)])

    def gate(c, slot):
        a = h13_ref[slot, :, :tf]
        u = h13_ref[slot, :, tf:]
        g_ref[:, pl.ds(cols(c), tf)] = (_silu(a) * u).astype(BF16)

    def pair(t, carry):
        up(2 * t + 1, 1)
        gate(2 * t, 0)
        up(2 * t + 2, 0)
        gate(2 * t + 1, 1)
        return carry

    up(0, 0)
    for t in range((n_chunks - 1) // 2):
        pair(t, 0)
    gate(n_chunks - 1, 0)
    out = x + (0.5 * mod_ref[k0 + 2:k0 + 3, :]) * _dot(g_ref[...], w2_ref[...])
    if final:
        out = _rms(out, fw_ref[...])
    o_ref[...] = out


def _ffn(h, mods, norm_w, w13, w2, final_w, *, k0, n_tiles, group_of, final, h_ctx=None, mixers=None):
    d = h.shape[1]
    tm = ROW_TILE
    ff = w2.shape[0]
    n_chunks = ff // FF_CHUNK
    assert ff % FF_CHUNK == 0 and n_chunks % 2 == 1
    row = lambda i: (i, 0)
    source, n_first, extra, extra_specs, h_spec = "plain", None, [], [], pl.BlockSpec((tm, d), row)
    if h_ctx is not None:
        source, n_first, extra = "two_arrays", h.shape[0] // tm, [h_ctx]
        h_spec = pl.BlockSpec((tm, d), lambda i: (jnp.minimum(i, n_first - 1), 0))
        extra_specs = [pl.BlockSpec((tm, d), lambda i: (jnp.maximum(i - n_first, 0), 0))]
    elif mixers is not None:
        gx, nx, dx, cx, w_out, (batch, seq, n_ctx) = mixers
        assert batch * n_ctx == tm and seq % tm == 0 and seq % n_ctx == 0
        source, n_first, extra = "mixers", batch * seq // tm, [gx, gx, nx, dx, cx, w_out.astype(BF16)]
        per_seq = seq // tm
        extra_specs = [pl.BlockSpec((None, tm, 256),
                                    lambda i: (jnp.minimum(i // per_seq, batch - 1), i % per_seq, 0)),
                       pl.BlockSpec((batch, n_ctx, 256), lambda i: (0, seq // n_ctx, 0)),
                       pl.BlockSpec((tm, 256), row),
                       pl.BlockSpec((DIFF_HEADS, tm, 64), lambda i: (0, i, 0)),
                       pl.BlockSpec((tm, 256), row), _resident(w_out.shape)]
    return pl.pallas_call(
        functools.partial(_ffn_body, k0=k0, final=final, source=source, n_first=n_first),
        grid=(n_tiles,),
        in_specs=[h_spec] + extra_specs + [
            pl.BlockSpec((None, N_MOD, d), lambda i: (group_of(i), 0, 0)),
            _resident((1, d)),
            _resident(w13.shape),
            _resident(w2.shape),
            _resident((1, d))],
        out_specs=pl.BlockSpec((tm, d), row),
        out_shape=jax.ShapeDtypeStruct((n_tiles * tm, d), F32),
        scratch_shapes=[pltpu.VMEM((tm, d), BF16), pltpu.VMEM((tm, ff), BF16),
                        pltpu.VMEM((2, tm, 2 * FF_CHUNK), F32)],
        compiler_params=_params("parallel"),
        name="ffn",
    )(h, *extra, mods, norm_w.reshape(1, d), w13.astype(BF16), w2.astype(BF16), final_w.reshape(1, d))


def _log_sigmoid(x):
    return jnp.minimum(x, 0.0) - jnp.log1p(jnp.exp(-jnp.abs(x)))


def _rope_rotate(x):
    n = x.shape[-1]
    lane = lax.broadcasted_iota(jnp.int32, x.shape, 1)
    up = pltpu.roll(x, n - 8, 1)
    dn = pltpu.roll(x, 8, 1)
    return jnp.where((lane & 15) < 8, -up, dn)


def _rope_rotate_rows(x):
    n = x.shape[0]
    row = lax.broadcasted_iota(jnp.int32, x.shape, 0)
    up = pltpu.roll(x, n - 8, 0)
    dn = pltpu.roll(x, 8, 0)
    return jnp.where((row & 15) < 8, -up, dn)


def _proj_body(h_ref, mod_ref, nw_ref, w_ref, wt_ref, wa_ref, ba_ref,
               cos_ref, sin_ref, cost_ref, sint_ref,
               gqk_ref, gv_ref, gvt_ref, gg_ref, glg_ref, nq_ref, nk_ref, nv_ref,
               dqt_ref, dk_ref, dvt_ref, cu_ref, xb_ref):
    x = h_ref[...]
    tm = x.shape[0]
    xm = _rms(x, nw_ref[...]) * (1.0 + mod_ref[4:5, :]) + mod_ref[3:4, :]
    xb_ref[...] = xm.astype(BF16)

    z = _dot(xb_ref[...], w_ref[:, C_GLA:C_GLA + 256])
    lane = lax.broadcasted_iota(jnp.int32, (1, 256), 1)
    gqk_ref[...] = z * jnp.where(lane < 128, GLA_DK ** -0.5, 1.0)
    gv_ref[...] = _dot(xb_ref[...], w_ref[:, C_GLA + 256:C_GLA + 512]).astype(BF16)
    zt = _dot_nt(wt_ref[...], xb_ref[...])
    gvt_ref[...] = zt[0:256, :].astype(BF16)
    gg_ref[...] = _dot(xb_ref[...], w_ref[:, C_GLA + 512:C_GLA + 768]).astype(BF16)
    aux = _dot(xb_ref[...], w_ref[:, C_AUX:C_END])
    pre = _dot(aux.astype(BF16), wa_ref[...]) + ba_ref[...]
    glg_ref[...] = _log_sigmoid(pre) * (1.0 / GLA_TAU)

    nq_ref[...] = (_dot(xb_ref[...], w_ref[:, C_NA:C_NA + 256]) * (NA_DIM ** -0.5 * LOG2E)).astype(BF16)
    nk_ref[...] = _dot(xb_ref[...], w_ref[:, C_NA + 256:C_NA + 512]).astype(BF16)
    nv_ref[...] = _dot(xb_ref[...], w_ref[:, C_NA + 512:C_NA + 768]).astype(BF16)

    cos = cos_ref[...]
    sin = sin_ref[...]
    cos2 = jnp.concatenate([cos, cos], axis=1)
    sin2 = jnp.concatenate([sin, sin], axis=1)
    zk = _dot(xb_ref[...], w_ref[:, C_DIFF + 256:C_DIFF + 512])
    zk = zk * cos2 + _rope_rotate(zk) * sin2
    n_rep = 2 * DIFF_HEADS
    cos_t = jnp.concatenate([cost_ref[...]] * n_rep, axis=0)
    sin_t = jnp.concatenate([sint_ref[...]] * n_rep, axis=0)
    zqt = zt[256:512, :]
    zqt = (zqt * cos_t + _rope_rotate_rows(zqt) * sin_t) * (DIFF_DK ** -0.5 * LOG2E)
    zvt = zt[512:768, :]
    pad_rows = DIFF_VROWS - DIFF_DV
    one_row = jnp.where(lax.broadcasted_iota(jnp.int32, (pad_rows, tm), 0) == 0, 1.0, 0.0).astype(BF16)
    for hh in range(DIFF_HEADS):
        sl = slice(64 * hh, 64 * hh + 64)
        dk_ref[hh] = zk[:, sl].astype(BF16)
        dqt_ref[hh] = zqt[sl, :].astype(BF16)
        dvt_ref[hh, 0:64, :] = zvt[sl, :].astype(BF16)
        dvt_ref[hh, DIFF_DV:DIFF_VROWS, :] = one_row

    za = _dot(xb_ref[...], w_ref[:, C_CONV:C_CONV + 256])
    zg = _dot(xb_ref[...], w_ref[:, C_CONV + 256:C_CONV + 512])
    cu_ref[...] = za * jax.nn.sigmoid(zg)


def _proj(h, mods, norm_w, w_p, wa, ba, rope, *, n_tiles, group_of, pos_of):
    r, d = h.shape
    tm = ROW_TILE
    row = lambda i: (i, 0)
    hrow = lambda i: (0, i, 0)
    hcol = lambda i: (0, 0, i)
    f32_256 = jax.ShapeDtypeStruct((r, 256), F32)
    bf_256 = jax.ShapeDtypeStruct((r, 256), BF16)
    w_t = jnp.concatenate([w_p[:, C_GLA + 256:C_GLA + 512], w_p[:, C_DIFF:C_DIFF + 256],
                           w_p[:, C_DIFF + 512:C_DIFF + 768]], axis=1).T
    cos_r, sin_r, cos_c, sin_c = rope
    out_shape = [f32_256, bf_256, jax.ShapeDtypeStruct((256, r), BF16), bf_256, f32_256,
                 bf_256, bf_256, bf_256,
                 jax.ShapeDtypeStruct((DIFF_HEADS, 64, r), BF16),
                 jax.ShapeDtypeStruct((DIFF_HEADS, r, 64), BF16),
                 jax.ShapeDtypeStruct((DIFF_HEADS, DIFF_VROWS, r), BF16),
                 f32_256]
    out_specs = [pl.BlockSpec((tm, 256), row)] * 2 + [pl.BlockSpec((256, tm), lambda i: (0, i))] + [
        pl.BlockSpec((tm, 256), row)] * 5 + [
        pl.BlockSpec((DIFF_HEADS, 64, tm), hcol),
        pl.BlockSpec((DIFF_HEADS, tm, 64), hrow),
        pl.BlockSpec((DIFF_HEADS, DIFF_VROWS, tm), hcol),
        pl.BlockSpec((tm, 256), row)]
    return pl.pallas_call(
        _proj_body,
        grid=(n_tiles,),
        in_specs=[pl.BlockSpec((tm, d), row),
                  pl.BlockSpec((None, N_MOD, d), lambda i: (group_of(i), 0, 0)),
                  _resident((1, d)),
                  _resident(w_p.shape),
                  _resident(w_t.shape),
                  _resident(wa.shape),
                  _resident(ba.shape),
                  pl.BlockSpec((tm, LANES), lambda i: (pos_of(i), 0)),
                  pl.BlockSpec((tm, LANES), lambda i: (pos_of(i), 0)),
                  pl.BlockSpec((DIFF_DK, tm), lambda i: (0, pos_of(i))),
                  pl.BlockSpec((DIFF_DK, tm), lambda i: (0, pos_of(i)))],
        out_specs=out_specs,
        out_shape=out_shape,
        scratch_shapes=[pltpu.VMEM((tm, d), BF16)],
        compiler_params=_params("parallel"),
        name="proj",
    )(h, mods, norm_w.reshape(1, d), w_p, w_t, wa, ba, cos_r, sin_r, cos_c, sin_c)


def _permute_w_in(w_in):
    d = w_in.shape[0]
    g0 = 2 * GLA_HEADS * GLA_DK + 2 * GLA_HEADS * GLA_DV
    aux = w_in[:, g0:g0 + 2 * GLA_RANK]
    rest = w_in[:, g0 + 2 * GLA_RANK:]
    pad = jnp.zeros((d, C_END - C_AUX - 2 * GLA_RANK), w_in.dtype)
    return jnp.concatenate([w_in[:, :g0], rest, aux, pad], axis=1).astype(BF16)


def _gate_weights(wa_f, ba_f, wa_b, ba_b):
    n = GLA_HEADS * GLA_DK
    wa = jnp.zeros((C_END - C_AUX, 2 * n), F32)
    wa = wa.at[:GLA_RANK, :n].set(wa_f).at[GLA_RANK:2 * GLA_RANK, n:].set(wa_b)
    return wa.astype(BF16), jnp.concatenate([ba_f, ba_b]).reshape(1, 2 * n)


def _rope_tables(seq, tile):
    t = jnp.arange(seq)
    row = (t // GRID_W).astype(F32)
    col = (t % GRID_W).astype(F32)
    half = DIFF_DK // 2
    inv = 1.0 / (ROPE_BASE ** (jnp.arange(0, half, 2, dtype=F32) / half))
    ang_r = row[:, None] * inv
    ang_c = col[:, None] * inv
    ang = jnp.concatenate([ang_r, ang_r, ang_c, ang_c], axis=-1)
    cos = jnp.concatenate([jnp.cos(ang), jnp.ones((tile, DIFF_DK), F32)], axis=0)
    sin = jnp.concatenate([jnp.sin(ang), jnp.zeros((tile, DIFF_DK), F32)], axis=0)
    rep = (1, LANES // DIFF_DK)
    return jnp.tile(cos, rep), jnp.tile(sin, rep), cos.T, sin.T


def _split3(x):
    hi = x.astype(BF16)
    r1 = x - hi.astype(F32)
    mid = r1.astype(BF16)
    lo = (r1 - mid.astype(F32)).astype(BF16)
    return hi, mid, lo


def _gla_body(*refs, reverse, batch):
    per_b = [refs[5 * b:5 * b + 5] for b in range(batch)]
    of_ref, nw_ref, o_ref, st_ref = refs[5 * batch:]
    blk = per_b[0][0].shape[0]
    n_chunks = blk // GLA_CHUNK
    nqk = GLA_HEADS * GLA_DK
    nv = GLA_HEADS * GLA_DV

    @pl.when(pl.program_id(0) == 0)
    def _():
        st_ref[...] = jnp.zeros_like(st_ref)

    ri = lax.broadcasted_iota(jnp.int32, (blk, blk), 0)
    ci = lax.broadcasted_iota(jnp.int32, (blk, blk), 1)
    same = (ri // GLA_CHUNK) == (ci // GLA_CHUNK)
    tri = jnp.where(same & ((ci >= ri) if reverse else (ci <= ri)), 1.0, 0.0).astype(BF16)
    bcum_of = []
    for qk_ref, v_ref, vt_ref, lg_ref, g_ref in per_b:
        lg = lg_ref[:, nqk:2 * nqk] if reverse else lg_ref[:, 0:nqk]
        hi, mid, lo = _split3(lg)
        bcum_of.append(_dot(tri, hi) + _dot(tri, mid) + _dot(tri, lo))

    c = GLA_CHUNK
    rk = lax.broadcasted_iota(jnp.int32, (GLA_HEADS * c, nqk), 0) // c
    ck = lax.broadcasted_iota(jnp.int32, (GLA_HEADS * c, nqk), 1) // GLA_DK
    mask_k = rk == ck
    rv = lax.broadcasted_iota(jnp.int32, (GLA_HEADS * c, nv), 0) // c
    cv = lax.broadcasted_iota(jnp.int32, (GLA_HEADS * c, nv), 1) // GLA_DV
    mask_v = rv == cv
    rs = lax.broadcasted_iota(jnp.int32, (nv, nqk), 0) // GLA_DV
    cs = lax.broadcasted_iota(jnp.int32, (nv, nqk), 1) // GLA_DK
    mask_s = rs == cs
    ai = lax.broadcasted_iota(jnp.int32, (c, GLA_HEADS * c), 0)
    aj = lax.broadcasted_iota(jnp.int32, (c, GLA_HEADS * c), 1) % c
    mask_a = (aj >= ai) if reverse else (aj <= ai)

    order = range(n_chunks - 1, -1, -1) if reverse else range(n_chunks)
    units = [(b, ch) for ch in order for b in range(batch)]
    q_in, a_raw, u_t, decay = {}, {}, {}, {}
    vts = [refs_b[2][...].astype(BF16) for refs_b in per_b]
    for b, ch in units:
        qk_ref = per_b[b][0]
        rows = slice(ch * c, (ch + 1) * c)
        k = qk_ref[rows, nqk:2 * nqk]
        bcum = bcum_of[b][rows, :]
        btot = bcum[0:1, :] if reverse else bcum[c - 1:c, :]
        q_in[b, ch] = (qk_ref[rows, 0:nqk] * jnp.exp(bcum)).astype(BF16)
        k_in = k * jnp.exp(-bcum)
        k_out = (k * jnp.exp(btot - bcum)).astype(BF16)
        k_bd = jnp.where(mask_k, jnp.concatenate([k_in] * GLA_HEADS, axis=0), 0.0).astype(BF16)
        a_raw[b, ch] = _dot_nt(q_in[b, ch], k_bd)
        k_pad = jnp.concatenate([jnp.zeros((n * c, nqk), BF16) for n in (ch,) if n] + [k_out]
                                + [jnp.zeros((n * c, nqk), BF16) for n in (n_chunks - 1 - ch,) if n], axis=0)
        u_t[b, ch] = jnp.where(mask_s, _dot(vts[b], k_pad), 0.0)
        decay[b, ch] = jnp.exp(btot)
    o = {}
    for b, ch in units:
        v = per_b[b][1][ch * c:(ch + 1) * c, :]
        v4 = jnp.concatenate([v] * GLA_HEADS, axis=0)
        v_bd = jnp.where(mask_v, v4, jnp.zeros_like(v4))
        o[b, ch] = _dot(jnp.where(mask_a, a_raw[b, ch], 0.0).astype(BF16), v_bd)
    st = [st_ref[b] for b in range(batch)]
    for b, ch in units:
        o[b, ch] = o[b, ch] + _dot_nt(q_in[b, ch], st[b].astype(BF16))
        st[b] = st[b] * decay[b, ch] + u_t[b, ch]
    for b in range(batch):
        st_ref[b] = st[b]
        o_blk = jnp.concatenate([o[b, ch] for ch in range(n_chunks)], axis=0)
        if reverse:
            o_blk = o_blk + of_ref[b]
            hi2, lo2, _ = _split3(o_blk * o_blk)
            hr = lax.broadcasted_iota(jnp.int32, (nv, nv), 0) // GLA_DV
            hc = lax.broadcasted_iota(jnp.int32, (nv, nv), 1) // GLA_DV
            seg = jnp.where(hr == hc, 1.0, 0.0).astype(BF16)
            ms = (_dot(hi2, seg) + _dot(lo2, seg)) * (1.0 / GLA_DV)
            o_blk = o_blk * lax.rsqrt(ms + RMS_EPS) * nw_ref[...] * _silu(per_b[b][4][...].astype(F32))
        o_ref[b] = o_blk.astype(o_ref.dtype)


def _gla(gqk, gv, gvt, glg, gg, o_f, norm_w4, *, reverse, batch, seq, ctx):
    blk = GLA_BLOCK
    nc, nl = ctx // blk, seq // blk
    ctx_base = batch * seq // blk

    def step_blk(s):
        if reverse:
            return jnp.where(s < nc, nl + (nc - 1 - s), nl - 1 - (s - nc))
        return jnp.where(s < nc, nl + s, s - nc)

    def row_blk(b):
        return lambda s: jnp.where(s < nc, ctx_base + b * nc - nl, b * nl) + step_blk(s)

    specs, args = [], []
    for b in range(batch):
        spec = pl.BlockSpec((blk, 256), lambda s, f=row_blk(b): (f(s), 0))
        spec_t = pl.BlockSpec((256, blk), lambda s, f=row_blk(b): (0, f(s)))
        if reverse:
            specs += [spec, spec, spec_t, spec, spec]
            args += [gqk, gv, gvt, glg, gg]
        else:
            specs += [spec, spec, spec_t, spec, _resident((1, 256))]
            args += [gqk, gv, gvt, glg, norm_w4]
    seq_spec = pl.BlockSpec((batch, blk, 256), lambda s: (0, step_blk(s), 0))
    if o_f is None:
        o_f, of_spec = norm_w4, _resident((1, 256))
    else:
        of_spec = seq_spec
    return pl.pallas_call(
        functools.partial(_gla_body, reverse=reverse, batch=batch),
        grid=(nc + nl,),
        in_specs=specs + [of_spec, _resident((1, 256))],
        out_specs=seq_spec,
        out_shape=jax.ShapeDtypeStruct((batch, seq + ctx, 256), BF16 if reverse else F32),
        scratch_shapes=[pltpu.VMEM((batch, GLA_HEADS * GLA_DV, GLA_HEADS * GLA_DK), F32)],
        compiler_params=_params("arbitrary"),
        name="gla_bwd" if reverse else "gla_fwd",
    )(*args, o_f, norm_w4)


def _na_col_tables(rpb):
    col = jnp.arange(GRID_W)
    cstart = jnp.clip(col - NA_COLS // 2, 0, GRID_W - NA_COLS)
    col_ok = (col[None, :] >= cstart[:, None]) & (col[None, :] < cstart[:, None] + NA_COLS)
    col_off = jnp.clip(col[None, :] - col[:, None] + (NA_COLS - 1), 0, 2 * NA_COLS - 2)
    t = jnp.where(col_ok[None, None], rpb[:, :, col_off] * LOG2E, NEG_INF)
    dead = jnp.full((NA_HEADS, 1, GRID_W, GRID_W), NEG_INF, F32)
    t = jnp.concatenate([dead, t.astype(F32), dead], axis=1)
    return jnp.concatenate([t[:, :-1], t[:, 1:]], axis=-1)


def _na_body(q_ref, k0, k1, k2, k3, v0, v1, v2, v3, kc_ref, vc_ref, tab_ref, o_ref, *, rows, key_blocks):
    k_blk = [k0[...], k1[...], k2[...], k3[...]]
    v_blk = [v0[...], v1[...], v2[...], v3[...]]
    kc = kc_ref[...]
    vc = vc_ref[...]

    j = pl.program_id(1)
    rows_per_blk = NA_KBLK // GRID_W
    r0 = j * NA_QROWS
    kr0 = jnp.clip(2 * j - 1, 0, key_blocks - 4) * rows_per_blk
    lane_lo = lax.broadcasted_iota(jnp.int32, (1, 2 * GRID_W), 1) < GRID_W
    lane = lax.broadcasted_iota(jnp.int32, (1, NA_HEADS * NA_DIM), 1) // NA_DIM
    half = NA_QROWS // 2
    n_pairs = 3 * rows_per_blk // 2
    gq = half * GRID_W
    for g in range(2):
        first = jnp.clip(r0 + g * half - NA_ROWS // 2, 0, rows - NA_ROWS)
        shift = jnp.clip((first - kr0) // rows_per_blk, 0, 1)
        kw = jnp.concatenate([jnp.where(shift == 0, k_blk[i], k_blk[i + 1]) for i in range(3)], axis=0)
        vw = jnp.concatenate([jnp.where(shift == 0, v_blk[i], v_blk[i + 1]) for i in range(3)], axis=0)
        kg0 = kr0 + shift * rows_per_blk
        entry, ok = [], []
        for a in range(half):
            r = r0 + g * half + a
            start = jnp.clip(r - NA_ROWS // 2, 0, rows - NA_ROWS)
            for bp in range(n_pairs):
                rk = kg0 + 2 * bp
                entry.append(jnp.clip(rk - r + NA_ROWS, 0, 2 * NA_ROWS - 1))
                in0 = ((rk >= start) & (rk < start + NA_ROWS)).astype(jnp.int32)
                in1 = ((rk + 1 >= start) & (rk + 1 < start + NA_ROWS)).astype(jnp.int32)
                ok.append(jnp.where(lane_lo, in0, in1) != 0)

        def bias_of(hh):
            rows_ = []
            for a in range(half):
                tiles = [jnp.where(ok[a * n_pairs + bp], tab_ref[hh, entry[a * n_pairs + bp]], NEG_INF)
                         for bp in range(n_pairs)]
                rows_.append(jnp.concatenate(tiles, axis=1))
            return jnp.concatenate(rows_, axis=0)

        q = q_ref[g * gq:(g + 1) * gq, :]
        out = jnp.zeros((gq, NA_HEADS * NA_DIM), F32)
        for hh in range(NA_HEADS):
            mh = lane == hh
            qh = jnp.where(mh, q, jnp.zeros_like(q))
            s_w = _dot_nt(qh, kw) + bias_of(hh)
            s_c = _dot_nt(qh, kc)
            m = jnp.maximum(jnp.max(s_w, axis=-1, keepdims=True), jnp.max(s_c, axis=-1, keepdims=True))
            p_w = jnp.exp2(s_w - m)
            p_c = jnp.exp2(s_c - m)
            l = jnp.sum(p_w, axis=-1, keepdims=True) + jnp.sum(p_c, axis=-1, keepdims=True)
            o = _dot(p_w.astype(BF16), vw) + _dot(p_c.astype(BF16), vc)
            out = out + jnp.where(mh, o / l, 0.0)
        o_ref[g * gq:(g + 1) * gq, :] = out.astype(o_ref.dtype)


def _na(nq, nk, nv, tab, *, batch, seq, ctx):
    r = nq.shape[0]
    tq = NA_QROWS * GRID_W
    nj = seq // tq
    kb = seq // NA_KBLK
    assert kb >= 4 and seq % tq == 0 and seq % ctx == 0
    ctx_blk0 = batch * seq // ctx

    def kspec(i):
        return pl.BlockSpec((NA_KBLK, 256),
                            lambda b, j: (b * kb + jnp.clip(2 * j - 1, 0, kb - 4) + i, 0))

    cspec = pl.BlockSpec((ctx, 256), lambda b, j: (ctx_blk0 + b, 0))
    return pl.pallas_call(
        functools.partial(_na_body, rows=seq // GRID_W, key_blocks=kb),
        grid=(batch, nj),
        in_specs=[pl.BlockSpec((tq, 256), lambda b, j: (b * nj + j, 0))]
                 + [kspec(i) for i in range(4)] + [kspec(i) for i in range(4)]
                 + [cspec, cspec, _resident(tab.shape)],
        out_specs=pl.BlockSpec((tq, 256), lambda b, j: (b * nj + j, 0)),
        out_shape=jax.ShapeDtypeStruct((r, 256), BF16),
        compiler_params=_params("parallel", "arbitrary"),
        name="na",
    )(nq, nk, nk, nk, nk, nv, nv, nv, nv, nk, nv, tab)


def _diff_lambda(lq_ref, lambda_init):
    lq = lq_ref[...]
    s1 = jnp.sum(lq[0:1, :] * lq[1:2, :], axis=-1, keepdims=True)
    s2 = jnp.sum(lq[2:3, :] * lq[3:4, :], axis=-1, keepdims=True)
    return jnp.exp(s1) - jnp.exp(s2) + lambda_init


def _stack_qt(qt):
    row = lax.broadcasted_iota(jnp.int32, (2 * DIFF_DK, 1), 0)
    zero = jnp.zeros_like(qt)
    return jnp.concatenate([jnp.where(row < DIFF_DK, qt, zero), jnp.where(row >= DIFF_DK, qt, zero)], axis=1)


def _diff_finish_t(acc, tq, lam, nw_col, lambda_init):
    o = acc[0:DIFF_DV, :] / acc[DIFF_DV:DIFF_DV + 1, :]
    od = o[:, :tq] - lam * o[:, tq:]
    ms = jnp.mean(od * od, axis=0, keepdims=True)
    y = od * lax.rsqrt(ms + RMS_EPS) * nw_col * (1.0 - lambda_init)
    y = jnp.concatenate([y, jnp.zeros_like(y)], axis=0)
    return jnp.transpose(y)[:, 0:DIFF_DV]


def _diff_body(qt_ref, kc_ref, vtc_ref, k_ref, vt_ref, lq_ref, nw_ref, o_ref, s_ref, *, tq, tk, unroll, lambda_init):
    n_tiles = qt_ref.shape[1] // tq
    lam = _diff_lambda(lq_ref, lambda_init)

    def finish(acc, t):
        q0 = t * tq if isinstance(t, int) else pl.multiple_of(t * tq, tq)
        o_ref[pl.ds(q0, tq), :] = _diff_finish_t(acc, tq, lam, nw_ref[...], lambda_init).astype(o_ref.dtype)

    def tile(t, acc_prev):
        q0 = pl.multiple_of(t * tq, tq)
        return _diff_tile(qt_ref[:, pl.ds(q0, tq)], kc_ref, vtc_ref, k_ref, vt_ref, s_ref, tk=tk, unroll=unroll,
                          after_issue=lambda: finish(acc_prev, jnp.maximum(t - 1, 0)))

    placeholder = jnp.ones((vt_ref.shape[0], 2 * tq), F32)
    finish(lax.fori_loop(0, n_tiles, tile, placeholder), n_tiles - 1)


def _diff_tile(qt, kc_ref, vtc_ref, k_ref, vt_ref, s_ref, *, tk, unroll, after_issue):
    tq = qt.shape[1]
    n_chunks = k_ref.shape[0] // tk
    ring = s_ref.shape[0]
    qs = _stack_qt(qt)

    def chunk(j):
        return pl.ds(j * tk if isinstance(j, int) else pl.multiple_of(j * tk, tk), tk)

    def issue_scores(j, slot):
        s = _dot(k_ref[chunk(j), :], qs)
        s_ref[slot] = s
        return jnp.max(s, axis=0, keepdims=True)

    s_ctx = _dot(kc_ref[...], qs)
    m0 = jnp.max(s_ctx, axis=0, keepdims=True)
    cmax0 = tuple(issue_scores(min(a, n_chunks - 1), a % ring) for a in range(DIFF_AHEAD))
    after_issue()
    acc0 = _dot(vtc_ref[...], jnp.exp2(s_ctx - m0).astype(BF16))

    def group(g, carry, tail=False):
        cmax, m, acc = carry
        for u in range(unroll):
            j = g * unroll + u
            if tail and j + DIFF_AHEAD >= n_chunks:
                c_new = cmax[0]
            else:
                c_new = issue_scores(j + DIFF_AHEAD, (u + DIFF_AHEAD) % ring)
            m_new = jnp.maximum(m, cmax[0])
            p = jnp.exp2(s_ref[u % ring] - m_new).astype(BF16)
            vt = vt_ref[:, chunk(j)]
            acc = jnp.exp2(m - m_new) * acc + _dot(vt, p)
            m, cmax = m_new, cmax[1:] + (c_new,)
        return cmax, m, acc

    n_groups = n_chunks // unroll
    trips = jnp.minimum(pl.program_id(2) + 1, 1) * (n_groups - 1)
    carry = lax.fori_loop(0, trips, group, (cmax0, m0, acc0))
    _, _, acc = group(n_groups - 1, carry, tail=True)
    return acc


def _diff(dqt, dk, dvt, lq, norm_w, *, batch, seq, ctx, lambda_init):
    h, r, _ = dk.shape
    tq = DIFF_TQ
    tqs = tq * min(DIFF_QTILES, seq // tq)
    nq = seq // tqs
    tk = min(DIFF_TK, seq)
    unroll = min(DIFF_UNROLL, seq // tk)
    assert (seq // tk) % unroll == 0 and unroll % DIFF_RING == 0 and DIFF_RING > DIFF_AHEAD and seq % tqs == 0
    ctx_blk0 = batch * seq // ctx
    return pl.pallas_call(
        functools.partial(_diff_body, tq=tq, tk=tk, unroll=unroll, lambda_init=lambda_init),
        grid=(batch, h, nq),
        in_specs=[pl.BlockSpec((None, 64, tqs), lambda b, hh, i: (hh, 0, b * nq + i)),
                  pl.BlockSpec((None, ctx, 64), lambda b, hh, i: (hh, ctx_blk0 + b, 0)),
                  pl.BlockSpec((None, DIFF_VROWS, ctx), lambda b, hh, i: (hh, 0, ctx_blk0 + b)),
                  pl.BlockSpec((None, seq, 64), lambda b, hh, i: (hh, b, 0)),
                  pl.BlockSpec((None, DIFF_VROWS, seq), lambda b, hh, i: (hh, 0, b)),
                  _resident(lq.shape),
                  _resident((DIFF_DV, 1))],
        out_specs=pl.BlockSpec((None, tqs, 64), lambda b, hh, i: (hh, b * nq + i, 0)),
        out_shape=jax.ShapeDtypeStruct((h, r, 64), BF16),
        scratch_shapes=[pltpu.VMEM((DIFF_RING, tk, 2 * tq), F32)],
        compiler_params=_params("parallel", "parallel", "arbitrary"),
        name="diff",
    )(dqt, dk, dvt, dk, dvt, lq, norm_w.reshape(DIFF_DV, 1))


def _ctx_attn_body(nq_ref, nk_ref, nv_ref, dqt_ref, dk_ref, dvt_ref, lq_ref, nw_ref, na_in, df_in,
                   na_out, df_out, *, lambda_init):
    del na_in, df_in
    q = nq_ref[...]
    k = nk_ref[...]
    v = nv_ref[...]
    lane = lax.broadcasted_iota(jnp.int32, (1, NA_HEADS * NA_DIM), 1) // NA_DIM
    out = jnp.zeros(na_out.shape, F32)
    for hh in range(NA_HEADS):
        mh = lane == hh
        s = _dot_nt(jnp.where(mh, q, jnp.zeros_like(q)), k)
        p = jnp.exp2(s - jnp.max(s, axis=-1, keepdims=True))
        o = _dot(p.astype(BF16), v) / jnp.sum(p, axis=-1, keepdims=True)
        out = out + jnp.where(mh, o, 0.0)
    na_out[...] = out.astype(na_out.dtype)

    lam = _diff_lambda(lq_ref, lambda_init)
    tq = dqt_ref.shape[2]
    for hh in range(DIFF_HEADS):
        s = _dot(dk_ref[hh], _stack_qt(dqt_ref[hh]))
        p = jnp.exp2(s - jnp.max(s, axis=0, keepdims=True))
        acc = _dot(dvt_ref[hh], p.astype(BF16))
        df_out[hh] = _diff_finish_t(acc, tq, lam, nw_ref[...], lambda_init).astype(df_out.dtype)


def _ctx_attn(nq, nk, nv, dqt, dk, dvt, lq, norm_w, na_o, df_o, *, batch, seq, ctx, lambda_init):
    blk0 = batch * seq // ctx
    s256 = pl.BlockSpec((ctx, 256), lambda b: (blk0 + b, 0))
    sh64 = pl.BlockSpec((DIFF_HEADS, ctx, 64), lambda b: (0, blk0 + b, 0))
    sq_t = pl.BlockSpec((DIFF_HEADS, 64, ctx), lambda b: (0, 0, blk0 + b))
    sv_t = pl.BlockSpec((DIFF_HEADS, DIFF_VROWS, ctx), lambda b: (0, 0, blk0 + b))
    return pl.pallas_call(
        functools.partial(_ctx_attn_body, lambda_init=lambda_init),
        grid=(batch,),
        in_specs=[s256, s256, s256, sq_t, sh64, sv_t, _resident(lq.shape), _resident((DIFF_DV, 1)),
                  pl.BlockSpec(memory_space=pl.ANY), pl.BlockSpec(memory_space=pl.ANY)],
        out_specs=[s256, sh64],
        out_shape=[jax.ShapeDtypeStruct(na_o.shape, na_o.dtype), jax.ShapeDtypeStruct(df_o.shape, df_o.dtype)],
        input_output_aliases={8: 0, 9: 1},
        compiler_params=_params("arbitrary"),
        name="ctx_attn",
    )(nq, nk, nv, dqt, dk, dvt, lq, norm_w.reshape(DIFF_DV, 1), na_o, df_o)


def _conv_body(u_ref, prev_ref, next_ref, dw_ref, dwb_ref, lng_ref, lnb_ref, pw_ref, pwb_ref,
               o_ref, pad_ref, sh_ref, *, tiles_per_seq, n_latent_tiles):
    t = u_ref.shape[0]
    i = pl.program_id(0)
    in_ctx = i >= n_latent_tiles
    first = in_ctx | (i % tiles_per_seq == 0)
    last = in_ctx | (i % tiles_per_seq == tiles_per_seq - 1)
    pad_ref[0:HALO, :] = jnp.where(first, 0.0, prev_ref[...])
    pad_ref[HALO:HALO + t, :] = u_ref[...]
    pad_ref[HALO + t:HALO + t + HALO, :] = jnp.where(last, 0.0, next_ref[...])
    base = HALO - CONV_K // 2
    first = {}
    for phase in range(8):
        taps = [k for k in range(CONV_K) if (base + k) % 8 == phase]
        first[phase] = base + taps[0]
        span = base + taps[-1] + t - first[phase]
        sh_ref[phase, 0:span, :] = pad_ref[first[phase]:first[phase] + span, :]
    sub = 64
    parts = []
    for r0 in range(0, t, sub):
        acc = jnp.zeros((sub, CONV_CH), F32)
        for k in range(CONV_K):
            phase = (base + k) % 8
            off = base + k - first[phase] + r0
            acc = acc + dw_ref[k:k + 1, :] * sh_ref[phase, off:off + sub, :]
        parts.append(acc)
    y = jnp.concatenate(parts, axis=0) + dwb_ref[...]
    mu = jnp.mean(y, axis=-1, keepdims=True)
    yc = y - mu
    var = jnp.mean(yc * yc, axis=-1, keepdims=True)
    y = _silu(yc * lax.rsqrt(var + LN_EPS) * lng_ref[...] + lnb_ref[...])
    o_ref[...] = (_dot(y.astype(BF16), pw_ref[...]) + pwb_ref[...]).astype(o_ref.dtype)


def _conv(cu, dw, dw_b, ln_g, ln_b, pw, pw_b, *, batch, seq, ctx):
    r = cu.shape[0]
    t = CONV_TILE
    assert ctx == t and seq % t == 0
    n_tiles = r // t
    hb = t // HALO
    vec = lambda a: a.reshape(1, CONV_CH)
    return pl.pallas_call(
        functools.partial(_conv_body, tiles_per_seq=seq // t, n_latent_tiles=batch * seq // t),
        grid=(n_tiles,),
        in_specs=[pl.BlockSpec((t, CONV_CH), lambda i: (i, 0)),
                  pl.BlockSpec((HALO, CONV_CH), lambda i: (jnp.maximum(i * hb - 1, 0), 0)),
                  pl.BlockSpec((HALO, CONV_CH), lambda i: (jnp.minimum((i + 1) * hb, n_tiles * hb - 1), 0)),
                  _resident((CONV_K, CONV_CH))] + [_resident((1, CONV_CH))] * 3
                 + [_resident((CONV_CH, CONV_CH)), _resident((1, CONV_CH))],
        out_specs=pl.BlockSpec((t, CONV_CH), lambda i: (i, 0)),
        out_shape=jax.ShapeDtypeStruct((r, CONV_CH), BF16),
        scratch_shapes=[pltpu.VMEM((t + 2 * HALO, CONV_CH), F32),
                        pltpu.VMEM((8, t + 2 * HALO, CONV_CH), F32)],
        compiler_params=_params("parallel"),
        name="conv",
    )(cu, cu, cu, dw, vec(dw_b), vec(ln_g), vec(ln_b), pw.astype(BF16), vec(pw_b))


def kernel(x, c, ctx, c_ctx, ada_w, ada_b, norm_ffn1, ffn1_w13, ffn1_w2, norm_mix, w_in, gla_wa_f, gla_ba_f, gla_wa_b, gla_ba_b, gla_norm, na_rpb, diff_lq1, diff_lk1, diff_lq2, diff_lk2, diff_norm, conv_dw, conv_dw_b, conv_ln_g, conv_ln_b, conv_pw, conv_pw_b, w_out, norm_ffn2, ffn2_w13, ffn2_w2, final_norm):
    batch, seq, d = x.shape
    n_ctx = ctx.shape[1]
    depth = ada_w.shape[0]
    tm = ROW_TILE
    assert seq % tm == 0 and (batch * n_ctx) % tm == 0 and batch + 1 <= 8
    lat_tiles = batch * seq // tm
    all_tiles = lat_tiles + batch * n_ctx // tm
    tiles_per_batch = seq // tm

    def group_of(i):
        return jnp.minimum(i // tiles_per_batch, batch)

    def pos_of(i):
        return jnp.where(i < lat_tiles, i % tiles_per_batch, tiles_per_batch)

    c_rows = jnp.concatenate([c, c_ctx[None, :], jnp.zeros((8 - batch - 1, d), F32)], axis=0)
    mods_all = _ada(c_rows, ada_w, ada_b)[:, :batch + 1].reshape(depth, batch + 1, N_MOD, d)
    rope = _rope_tables(seq, tm)
    h = x.reshape(batch * seq, d)
    h_ctx = ctx.reshape(batch * n_ctx, d)

    for i in range(depth):
        last = i == depth - 1
        lambda_init = 0.8 - 0.6 * math.exp(-0.3 * i)
        mods = mods_all[i]
        tok = dict(n_tiles=all_tiles, group_of=group_of)
        geo = dict(batch=batch, seq=seq, ctx=n_ctx)

        h = _ffn(h, mods, norm_ffn1[i], ffn1_w13[i], ffn1_w2[i], final_norm, k0=0, final=False,
                 h_ctx=h_ctx if i == 0 else None, **tok)

        wa, ba = _gate_weights(gla_wa_f[i], gla_ba_f[i], gla_wa_b[i], gla_ba_b[i])
        (gqk, gv, gvt, gg, glg, nq, nk, nv, dq, dk, dv, cu) = _proj(
            h, mods, norm_mix[i], _permute_w_in(w_in[i]), wa, ba, rope, pos_of=pos_of, **tok)

        gnorm = jnp.tile(gla_norm[i], GLA_HEADS).reshape(1, GLA_HEADS * GLA_DV)
        o_f = _gla(gqk, gv, gvt, glg, gg, None, gnorm, reverse=False, **geo)
        gx = _gla(gqk, gv, gvt, glg, gg, o_f, gnorm, reverse=True, **geo)

        nx = _na(nq, nk, nv, _na_col_tables(na_rpb[i]), **geo)

        lq = jnp.stack([diff_lq1[i], diff_lk1[i], diff_lq2[i], diff_lk2[i]])
        dx = _diff(dq, dk, dv, lq, diff_norm[i], lambda_init=lambda_init, **geo)
        if not last:
            nx, dx = _ctx_attn(nq, nk, nv, dq, dk, dv, lq, diff_norm[i], nx, dx,
                               lambda_init=lambda_init, **geo)

        cx = _conv(cu, conv_dw[i], conv_dw_b[i], conv_ln_g[i], conv_ln_b[i], conv_pw[i], conv_pw_b[i], **geo)

        if last:
            tok = dict(n_tiles=lat_tiles, group_of=group_of)
        h = _ffn(h, mods, norm_ffn2[i], ffn2_w13[i], ffn2_w2[i], final_norm, k0=6, final=last,
                 mixers=(gx, nx, dx, cx, w_out[i], (batch, seq, n_ctx)), **tok)

    return h.reshape(batch, seq, d)
```

```python
import functools
import math

import jax
import jax.numpy as jnp
from jax import lax
from jax.experimental import pallas as pl
from jax.experimental.pallas import tpu as pltpu

F32 = jnp.float32
BF16 = jnp.bfloat16

GRID_W = 64
N_MOD = 9
RMS_EPS = 1e-6
LN_EPS = 1e-5
NEG_INF = -1e30
GLA_HEADS, GLA_DK, GLA_DV, GLA_RANK, GLA_TAU, GLA_CHUNK = 4, 32, 64, 16, 16.0, 64
NA_HEADS, NA_DIM, NA_ROWS, NA_COLS = 4, 64, 8, 16
DIFF_HEADS, DIFF_DK, DIFF_DV = 4, 32, 64
DIFF_VROWS = 80
CONV_CH, CONV_K = 256, 31
ROPE_BASE = 10000.0
LOG2E = 1.4426950408889634

LANES = 128
VMEM_LIMIT = 56 * 1024 * 1024

ROW_TILE = 512
FF_CHUNK = 256
GLA_BLOCK = 256
NA_QROWS = 8
NA_KBLK = 256
CONV_TILE = 256
DIFF_TQ = 256
DIFF_QTILES = 8
DIFF_TK = 256
DIFF_UNROLL = 32
DIFF_AHEAD = 2
DIFF_RING = 4
HALO = 16

C_GLA, C_NA, C_DIFF, C_CONV, C_AUX, C_END = 0, 768, 1536, 2304, 2816, 2944


def _dot(a, b):
    return jnp.dot(a, b, preferred_element_type=F32)


def _dot_nt(a, b):
    return lax.dot_general(a, b, (((1,), (1,)), ((), ())), preferred_element_type=F32)


def _params(*sem):
    return pltpu.CompilerParams(dimension_semantics=sem, vmem_limit_bytes=VMEM_LIMIT)


def _resident(shape):
    nd = len(shape)
    return pl.BlockSpec(shape, lambda *_: (0,) * nd, pipeline_mode=pl.Buffered(1))


def _silu(x):
    return x * jax.nn.sigmoid(x)


def _rms(x, w):
    return x * lax.rsqrt(jnp.mean(x * x, axis=-1, keepdims=True) + RMS_EPS) * w


def _ada_body(c_ref, w_ref, b_ref, o_ref):
    s = _silu(c_ref[...])
    o_ref[...] = jnp.dot(s, w_ref[...], precision=lax.Precision.HIGHEST,
                         preferred_element_type=F32) + b_ref[...]


def _ada(c_rows, ada_w, ada_b):
    depth, d, _ = ada_w.shape
    return pl.pallas_call(
        _ada_body,
        grid=(depth, N_MOD),
        in_specs=[pl.BlockSpec((8, d), lambda l, n: (0, 0)),
                  pl.BlockSpec((None, d, d), lambda l, n: (l, 0, n)),
                  pl.BlockSpec((None, 1, d), lambda l, n: (l, 0, n))],
        out_specs=pl.BlockSpec((None, 8, d), lambda l, n: (l, 0, n)),
        out_shape=jax.ShapeDtypeStruct((depth, 8, N_MOD * d), F32),
        compiler_params=_params("arbitrary", "arbitrary"),
        name="ada",
    )(c_rows, ada_w, ada_b.reshape(depth, 1, N_MOD * d))


def _ffn_body(*refs, k0, final, source, n_first):
    h_ref = refs[0]
    n_extra = {"plain": 0, "two_arrays": 1, "mixers": 6}[source]
    extra = refs[1:1 + n_extra]
    mod_ref, nw_ref, w13_ref, w2_ref, fw_ref, o_ref, xb_ref, g_ref, h13_ref = refs[1 + n_extra:]
    ff = w2_ref.shape[0]
    tf = h13_ref.shape[2] // 2
    n_chunks = ff // tf
    x = h_ref[...]
    if source == "two_arrays":
        x = jnp.where(pl.program_id(0) < n_first, x, extra[0][...])
    elif source == "mixers":
        gx_ref, gxc_ref, nx_ref, dx_ref, cx_ref, wo_ref = extra
        gx = jnp.where(pl.program_id(0) < n_first, gx_ref[...], gxc_ref[...].reshape(gx_ref.shape))
        mix = jnp.concatenate([gx, nx_ref[...]] + [dx_ref[hh] for hh in range(DIFF_HEADS)]
                              + [cx_ref[...]], axis=-1)
        x = x + mod_ref[5:6, :] * _dot(mix, wo_ref[...])
    xm = _rms(x, nw_ref[...]) * (1.0 + mod_ref[k0 + 1:k0 + 2, :]) + mod_ref[k0:k0 + 1, :]
    xb_ref[...] = xm.astype(BF16)

    def cols(c):
        return c * tf if isinstance(c, int) else pl.multiple_of(c * tf, tf)

    def up(c, slot):
        h13_ref[slot, :, :tf] = _dot(xb_ref[...], w13_ref[:, pl.ds(cols(c), tf)])
        h13_ref[slot, :, tf:] = _dot(xb_ref[...], w13_ref[:, pl.ds(ff + cols(c), tf)])

    def gate(c, slot):
        a = h13_ref[slot, :, :tf]
        u = h13_ref[slot, :, tf:]
        g_ref[:, pl.ds(cols(c), tf)] = (_silu(a) * u).astype(BF16)

    def pair(t, carry):
        up(2 * t + 1, 1)
        gate(2 * t, 0)
        up(2 * t + 2, 0)
        gate(2 * t + 1, 1)
        return carry

    up(0, 0)
    for t in range((n_chunks - 1) // 2):
        pair(t, 0)
    gate(n_chunks - 1, 0)
    out = x + (0.5 * mod_ref[k0 + 2:k0 + 3, :]) * _dot(g_ref[...], w2_ref[...])
    if final:
        out = _rms(out, fw_ref[...])
    o_ref[...] = out


def _ffn(h, mods, norm_w, w13, w2, final_w, *, k0, n_tiles, group_of, final, h_ctx=None, mixers=None):
    d = h.shape[1]
    tm = ROW_TILE
    ff = w2.shape[0]
    n_chunks = ff // FF_CHUNK
    assert ff % FF_CHUNK == 0 and n_chunks % 2 == 1
    row = lambda i: (i, 0)
    source, n_first, extra, extra_specs, h_spec = "plain", None, [], [], pl.BlockSpec((tm, d), row)
    if h_ctx is not None:
        source, n_first, extra = "two_arrays", h.shape[0] // tm, [h_ctx]
        h_spec = pl.BlockSpec((tm, d), lambda i: (jnp.minimum(i, n_first - 1), 0))
        extra_specs = [pl.BlockSpec((tm, d), lambda i: (jnp.maximum(i - n_first, 0), 0))]
    elif mixers is not None:
        gx, nx, dx, cx, w_out, (batch, seq, n_ctx) = mixers
        assert batch * n_ctx == tm and seq % tm == 0 and seq % n_ctx == 0
        source, n_first, extra = "mixers", batch * seq // tm, [gx, gx, nx, dx, cx, w_out.astype(BF16)]
        per_seq = seq // tm
        extra_specs = [pl.BlockSpec((None, tm, 256),
                                    lambda i: (jnp.minimum(i // per_seq, batch - 1), i % per_seq, 0)),
                       pl.BlockSpec((batch, n_ctx, 256), lambda i: (0, seq // n_ctx, 0)),
                       pl.BlockSpec((tm, 256), row),
                       pl.BlockSpec((DIFF_HEADS, tm, 64), lambda i: (0, i, 0)),
                       pl.BlockSpec((tm, 256), row), _resident(w_out.shape)]
    return pl.pallas_call(
        functools.partial(_ffn_body, k0=k0, final=final, source=source, n_first=n_first),
        grid=(n_tiles,),
        in_specs=[h_spec] + extra_specs + [
            pl.BlockSpec((None, N_MOD, d), lambda i: (group_of(i), 0, 0)),
            _resident((1, d)),
            _resident(w13.shape),
            _resident(w2.shape),
            _resident((1, d))],
        out_specs=pl.BlockSpec((tm, d), row),
        out_shape=jax.ShapeDtypeStruct((n_tiles * tm, d), F32),
        scratch_shapes=[pltpu.VMEM((tm, d), BF16), pltpu.VMEM((tm, ff), BF16),
                        pltpu.VMEM((2, tm, 2 * FF_CHUNK), F32)],
        compiler_params=_params("parallel"),
        name="ffn",
    )(h, *extra, mods, norm_w.reshape(1, d), w13.astype(BF16), w2.astype(BF16), final_w.reshape(1, d))


def _log_sigmoid(x):
    return jnp.minimum(x, 0.0) - jnp.log1p(jnp.exp(-jnp.abs(x)))


def _rope_rotate(x):
    n = x.shape[-1]
    lane = lax.broadcasted_iota(jnp.int32, x.shape, 1)
    up = pltpu.roll(x, n - 8, 1)
    dn = pltpu.roll(x, 8, 1)
    return jnp.where((lane & 15) < 8, -up, dn)


def _rope_rotate_rows(x):
    n = x.shape[0]
    row = lax.broadcasted_iota(jnp.int32, x.shape, 0)
    up = pltpu.roll(x, n - 8, 0)
    dn = pltpu.roll(x, 8, 0)
    return jnp.where((row & 15) < 8, -up, dn)


def _proj_body(h_ref, mod_ref, nw_ref, w_ref, wt_ref, wa_ref, ba_ref,
               cos_ref, sin_ref, cost_ref, sint_ref,
               gqk_ref, gv_ref, gvt_ref, gg_ref, glg_ref, nq_ref, nk_ref, nv_ref,
               dqt_ref, dk_ref, dvt_ref, cu_ref, xb_ref):
    x = h_ref[...]
    tm = x.shape[0]
    xm = _rms(x, nw_ref[...]) * (1.0 + mod_ref[4:5, :]) + mod_ref[3:4, :]
    xb_ref[...] = xm.astype(BF16)

    z = _dot(xb_ref[...], w_ref[:, C_GLA:C_GLA + 256])
    lane = lax.broadcasted_iota(jnp.int32, (1, 256), 1)
    gqk_ref[...] = z * jnp.where(lane < 128, GLA_DK ** -0.5, 1.0)
    gv_ref[...] = _dot(xb_ref[...], w_ref[:, C_GLA + 256:C_GLA + 512]).astype(BF16)
    zt = _dot_nt(wt_ref[...], xb_ref[...])
    gvt_ref[...] = zt[0:256, :].astype(BF16)
    gg_ref[...] = _dot(xb_ref[...], w_ref[:, C_GLA + 512:C_GLA + 768]).astype(BF16)
    aux = _dot(xb_ref[...], w_ref[:, C_AUX:C_END])
    pre = _dot(aux.astype(BF16), wa_ref[...]) + ba_ref[...]
    glg_ref[...] = _log_sigmoid(pre) * (1.0 / GLA_TAU)

    nq_ref[...] = (_dot(xb_ref[...], w_ref[:, C_NA:C_NA + 256]) * (NA_DIM ** -0.5 * LOG2E)).astype(BF16)
    nk_ref[...] = _dot(xb_ref[...], w_ref[:, C_NA + 256:C_NA + 512]).astype(BF16)
    nv_ref[...] = _dot(xb_ref[...], w_ref[:, C_NA + 512:C_NA + 768]).astype(BF16)

    cos = cos_ref[...]
    sin = sin_ref[...]
    cos2 = jnp.concatenate([cos, cos], axis=1)
    sin2 = jnp.concatenate([sin, sin], axis=1)
    zk = _dot(xb_ref[...], w_ref[:, C_DIFF + 256:C_DIFF + 512])
    zk = zk * cos2 + _rope_rotate(zk) * sin2
    n_rep = 2 * DIFF_HEADS
    cos_t = jnp.concatenate([cost_ref[...]] * n_rep, axis=0)
    sin_t = jnp.concatenate([sint_ref[...]] * n_rep, axis=0)
    zqt = zt[256:512, :]
    zqt = (zqt * cos_t + _rope_rotate_rows(zqt) * sin_t) * (DIFF_DK ** -0.5 * LOG2E)
    zvt = zt[512:768, :]
    pad_rows = DIFF_VROWS - DIFF_DV
    one_row = jnp.where(lax.broadcasted_iota(jnp.int32, (pad_rows, tm), 0) == 0, 1.0, 0.0).astype(BF16)
    for hh in range(DIFF_HEADS):
        sl = slice(64 * hh, 64 * hh + 64)
        dk_ref[hh] = zk[:, sl].astype(BF16)
        dqt_ref[hh] = zqt[sl, :].astype(BF16)
        dvt_ref[hh, 0:64, :] = zvt[sl, :].astype(BF16)
        dvt_ref[hh, DIFF_DV:DIFF_VROWS, :] = one_row

    za = _dot(xb_ref[...], w_ref[:, C_CONV:C_CONV + 256])
    zg = _dot(xb_ref[...], w_ref[:, C_CONV + 256:C_CONV + 512])
    cu_ref[...] = za * jax.nn.sigmoid(zg)


def _proj(h, mods, norm_w, w_p, wa, ba, rope, *, n_tiles, group_of, pos_of):
    r, d = h.shape
    tm = ROW_TILE
    row = lambda i: (i, 0)
    hrow = lambda i: (0, i, 0)
    hcol = lambda i: (0, 0, i)
    f32_256 = jax.ShapeDtypeStruct((r, 256), F32)
    bf_256 = jax.ShapeDtypeStruct((r, 256), BF16)
    w_t = jnp.concatenate([w_p[:, C_GLA + 256:C_GLA + 512], w_p[:, C_DIFF:C_DIFF + 256],
                           w_p[:, C_DIFF + 512:C_DIFF + 768]], axis=1).T
    cos_r, sin_r, cos_c, sin_c = rope
    out_shape = [f32_256, bf_256, jax.ShapeDtypeStruct((256, r), BF16), bf_256, f32_256,
                 bf_256, bf_256, bf_256,
                 jax.ShapeDtypeStruct((DIFF_HEADS, 64, r), BF16),
                 jax.ShapeDtypeStruct((DIFF_HEADS, r, 64), BF16),
                 jax.ShapeDtypeStruct((DIFF_HEADS, DIFF_VROWS, r), BF16),
                 f32_256]
    out_specs = [pl.BlockSpec((tm, 256), row)] * 2 + [pl.BlockSpec((256, tm), lambda i: (0, i))] + [
        pl.BlockSpec((tm, 256), row)] * 5 + [
        pl.BlockSpec((DIFF_HEADS, 64, tm), hcol),
        pl.BlockSpec((DIFF_HEADS, tm, 64), hrow),
        pl.BlockSpec((DIFF_HEADS, DIFF_VROWS, tm), hcol),
        pl.BlockSpec((tm, 256), row)]
    return pl.pallas_call(
        _proj_body,
        grid=(n_tiles,),
        in_specs=[pl.BlockSpec((tm, d), row),
                  pl.BlockSpec((None, N_MOD, d), lambda i: (group_of(i), 0, 0)),
                  _resident((1, d)),
                  _resident(w_p.shape),
                  _resident(w_t.shape),
                  _resident(wa.shape),
                  _resident(ba.shape),
                  pl.BlockSpec((tm, LANES), lambda i: (pos_of(i), 0)),
                  pl.BlockSpec((tm, LANES), lambda i: (pos_of(i), 0)),
                  pl.BlockSpec((DIFF_DK, tm), lambda i: (0, pos_of(i))),
                  pl.BlockSpec((DIFF_DK, tm), lambda i: (0, pos_of(i)))],
        out_specs=out_specs,
        out_shape=out_shape,
        scratch_shapes=[pltpu.VMEM((tm, d), BF16)],
        compiler_params=_params("parallel"),
        name="proj",
    )(h, mods, norm_w.reshape(1, d), w_p, w_t, wa, ba, cos_r, sin_r, cos_c, sin_c)


def _permute_w_in(w_in):
    d = w_in.shape[0]
    g0 = 2 * GLA_HEADS * GLA_DK + 2 * GLA_HEADS * GLA_DV
    aux = w_in[:, g0:g0 + 2 * GLA_RANK]
    rest = w_in[:, g0 + 2 * GLA_RANK:]
    pad = jnp.zeros((d, C_END - C_AUX - 2 * GLA_RANK), w_in.dtype)
    return jnp.concatenate([w_in[:, :g0], rest, aux, pad], axis=1).astype(BF16)


def _gate_weights(wa_f, ba_f, wa_b, ba_b):
    n = GLA_HEADS * GLA_DK
    wa = jnp.zeros((C_END - C_AUX, 2 * n), F32)
    wa = wa.at[:GLA_RANK, :n].set(wa_f).at[GLA_RANK:2 * GLA_RANK, n:].set(wa_b)
    return wa.astype(BF16), jnp.concatenate([ba_f, ba_b]).reshape(1, 2 * n)


def _rope_tables(seq, tile):
    t = jnp.arange(seq)
    row = (t // GRID_W).astype(F32)
    col = (t % GRID_W).astype(F32)
    half = DIFF_DK // 2
    inv = 1.0 / (ROPE_BASE ** (jnp.arange(0, half, 2, dtype=F32) / half))
    ang_r = row[:, None] * inv
    ang_c = col[:, None] * inv
    ang = jnp.concatenate([ang_r, ang_r, ang_c, ang_c], axis=-1)
    cos = jnp.concatenate([jnp.cos(ang), jnp.ones((tile, DIFF_DK), F32)], axis=0)
    sin = jnp.concatenate([jnp.sin(ang), jnp.zeros((tile, DIFF_DK), F32)], axis=0)
    rep = (1, LANES // DIFF_DK)
    return jnp.tile(cos, rep), jnp.tile(sin, rep), cos.T, sin.T


def _split3(x):
    hi = x.astype(BF16)
    r1 = x - hi.astype(F32)
    mid = r1.astype(BF16)
    lo = (r1 - mid.astype(F32)).astype(BF16)
    return hi, mid, lo


def _gla_body(*refs, reverse, batch):
    per_b = [refs[5 * b:5 * b + 5] for b in range(batch)]
    of_ref, nw_ref, o_ref, st_ref = refs[5 * batch:]
    blk = per_b[0][0].shape[0]
    n_chunks = blk // GLA_CHUNK
    nqk = GLA_HEADS * GLA_DK
    nv = GLA_HEADS * GLA_DV

    @pl.when(pl.program_id(0) == 0)
    def _():
        st_ref[...] = jnp.zeros_like(st_ref)

    ri = lax.broadcasted_iota(jnp.int32, (blk, blk), 0)
    ci = lax.broadcasted_iota(jnp.int32, (blk, blk), 1)
    same = (ri // GLA_CHUNK) == (ci // GLA_CHUNK)
    tri = jnp.where(same & ((ci >= ri) if reverse else (ci <= ri)), 1.0, 0.0).astype(BF16)
    bcum_of = []
    for qk_ref, v_ref, vt_ref, lg_ref, g_ref in per_b:
        lg = lg_ref[:, nqk:2 * nqk] if reverse else lg_ref[:, 0:nqk]
        hi, mid, lo = _split3(lg)
        bcum_of.append(_dot(tri, hi) + _dot(tri, mid) + _dot(tri, lo))

    c = GLA_CHUNK
    rk = lax.broadcasted_iota(jnp.int32, (GLA_HEADS * c, nqk), 0) // c
    ck = lax.broadcasted_iota(jnp.int32, (GLA_HEADS * c, nqk), 1) // GLA_DK
    mask_k = rk == ck
    rv = lax.broadcasted_iota(jnp.int32, (GLA_HEADS * c, nv), 0) // c
    cv = lax.broadcasted_iota(jnp.int32, (GLA_HEADS * c, nv), 1) // GLA_DV
    mask_v = rv == cv
    rs = lax.broadcasted_iota(jnp.int32, (nv, nqk), 0) // GLA_DV
    cs = lax.broadcasted_iota(jnp.int32, (nv, nqk), 1) // GLA_DK
    mask_s = rs == cs
    ai = lax.broadcasted_iota(jnp.int32, (c, GLA_HEADS * c), 0)
    aj = lax.broadcasted_iota(jnp.int32, (c, GLA_HEADS * c), 1) % c
    mask_a = (aj >= ai) if reverse else (aj <= ai)

    order = range(n_chunks - 1, -1, -1) if reverse else range(n_chunks)
    units = [(b, ch) for ch in order for b in range(batch)]
    q_in, a_raw, u_t, decay = {}, {}, {}, {}
    vts = [refs_b[2][...].astype(BF16) for refs_b in per_b]
    for b, ch in units:
        qk_ref = per_b[b][0]
        rows = slice(ch * c, (ch + 1) * c)
        k = qk_ref[rows, nqk:2 * nqk]
        bcum = bcum_of[b][rows, :]
        btot = bcum[0:1, :] if reverse else bcum[c - 1:c, :]
        q_in[b, ch] = (qk_ref[rows, 0:nqk] * jnp.exp(bcum)).astype(BF16)
        k_in = k * jnp.exp(-bcum)
        k_out = (k * jnp.exp(btot - bcum)).astype(BF16)
        k_bd = jnp.where(mask_k, jnp.concatenate([k_in] * GLA_HEADS, axis=0), 0.0).astype(BF16)
        a_raw[b, ch] = _dot_nt(q_in[b, ch], k_bd)
        k_pad = jnp.concatenate([jnp.zeros((n * c, nqk), BF16) for n in (ch,) if n] + [k_out]
                                + [jnp.zeros((n * c, nqk), BF16) for n in (n_chunks - 1 - ch,) if n], axis=0)
        u_t[b, ch] = jnp.where(mask_s, _dot(vts[b], k_pad), 0.0)
        decay[b, ch] = jnp.exp(btot)
    o = {}
    for b, ch in units:
        v = per_b[b][1][ch * c:(ch + 1) * c, :]
        v4 = jnp.concatenate([v] * GLA_HEADS, axis=0)
        v_bd = jnp.where(mask_v, v4, jnp.zeros_like(v4))
        o[b, ch] = _dot(jnp.where(mask_a, a_raw[b, ch], 0.0).astype(BF16), v_bd)
    st = [st_ref[b] for b in range(batch)]
    for b, ch in units:
        o[b, ch] = o[b, ch] + _dot_nt(q_in[b, ch], st[b].astype(BF16))
        st[b] = st[b] * decay[b, ch] + u_t[b, ch]
    for b in range(batch):
        st_ref[b] = st[b]
        o_blk = jnp.concatenate([o[b, ch] for ch in range(n_chunks)], axis=0)
        if reverse:
            o_blk = o_blk + of_ref[b]
            hi2, lo2, _ = _split3(o_blk * o_blk)
            hr = lax.broadcasted_iota(jnp.int32, (nv, nv), 0) // GLA_DV
            hc = lax.broadcasted_iota(jnp.int32, (nv, nv), 1) // GLA_DV
            seg = jnp.where(hr == hc, 1.0, 0.0).astype(BF16)
            ms = (_dot(hi2, seg) + _dot(lo2, seg)) * (1.0 / GLA_DV)
            o_blk = o_blk * lax.rsqrt(ms + RMS_EPS) * nw_ref[...] * _silu(per_b[b][4][...].astype(F32))
        o_ref[b] = o_blk.astype(o_ref.dtype)


def _gla(gqk, gv, gvt, glg, gg, o_f, norm_w4, *, reverse, batch, seq, ctx):
    blk = GLA_BLOCK
    nc, nl = ctx // blk, seq // blk
    ctx_base = batch * seq // blk

    def step_blk(s):
        if reverse:
            return jnp.where(s < nc, nl + (nc - 1 - s), nl - 1 - (s - nc))
        return jnp.where(s < nc, nl + s, s - nc)

    def row_blk(b):
        return lambda s: jnp.where(s < nc, ctx_base + b * nc - nl, b * nl) + step_blk(s)

    specs, args = [], []
    for b in range(batch):
        spec = pl.BlockSpec((blk, 256), lambda s, f=row_blk(b): (f(s), 0))
        spec_t = pl.BlockSpec((256, blk), lambda s, f=row_blk(b): (0, f(s)))
        if reverse:
            specs += [spec, spec, spec_t, spec, spec]
            args += [gqk, gv, gvt, glg, gg]
        else:
            specs += [spec, spec, spec_t, spec, _resident((1, 256))]
            args += [gqk, gv, gvt, glg, norm_w4]
    seq_spec = pl.BlockSpec((batch, blk, 256), lambda s: (0, step_blk(s), 0))
    if o_f is None:
        o_f, of_spec = norm_w4, _resident((1, 256))
    else:
        of_spec = seq_spec
    return pl.pallas_call(
        functools.partial(_gla_body, reverse=reverse, batch=batch),
        grid=(nc + nl,),
        in_specs=specs + [of_spec, _resident((1, 256))],
        out_specs=seq_spec,
        out_shape=jax.ShapeDtypeStruct((batch, seq + ctx, 256), BF16 if reverse else F32),
        scratch_shapes=[pltpu.VMEM((batch, GLA_HEADS * GLA_DV, GLA_HEADS * GLA_DK), F32)],
        compiler_params=_params("arbitrary"),
        name="gla_bwd" if reverse else "gla_fwd",
    )(*args, o_f, norm_w4)


def _na_col_tables(rpb):
    col = jnp.arange(GRID_W)
    cstart = jnp.clip(col - NA_COLS // 2, 0, GRID_W - NA_COLS)
    col_ok = (col[None, :] >= cstart[:, None]) & (col[None, :] < cstart[:, None] + NA_COLS)
    col_off = jnp.clip(col[None, :] - col[:, None] + (NA_COLS - 1), 0, 2 * NA_COLS - 2)
    t = jnp.where(col_ok[None, None], rpb[:, :, col_off] * LOG2E, NEG_INF)
    dead = jnp.full((NA_HEADS, 1, GRID_W, GRID_W), NEG_INF, F32)
    t = jnp.concatenate([dead, t.astype(F32), dead], axis=1)
    return jnp.concatenate([t[:, :-1], t[:, 1:]], axis=-1)


def _na_body(q_ref, k0, k1, k2, k3, v0, v1, v2, v3, kc_ref, vc_ref, tab_ref, o_ref, *, rows, key_blocks):
    k_blk = [k0[...], k1[...], k2[...], k3[...]]
    v_blk = [v0[...], v1[...], v2[...], v3[...]]
    kc = kc_ref[...]
    vc = vc_ref[...]

    j = pl.program_id(1)
    rows_per_blk = NA_KBLK // GRID_W
    r0 = j * NA_QROWS
    kr0 = jnp.clip(2 * j - 1, 0, key_blocks - 4) * rows_per_blk
    lane_lo = lax.broadcasted_iota(jnp.int32, (1, 2 * GRID_W), 1) < GRID_W
    lane = lax.broadcasted_iota(jnp.int32, (1, NA_HEADS * NA_DIM), 1) // NA_DIM
    half = NA_QROWS // 2
    n_pairs = 3 * rows_per_blk // 2
    gq = half * GRID_W
    for g in range(2):
        first = jnp.clip(r0 + g * half - NA_ROWS // 2, 0, rows - NA_ROWS)
        shift = jnp.clip((first - kr0) // rows_per_blk, 0, 1)
        kw = jnp.concatenate([jnp.where(shift == 0, k_blk[i], k_blk[i + 1]) for i in range(3)], axis=0)
        vw = jnp.concatenate([jnp.where(shift == 0, v_blk[i], v_blk[i + 1]) for i in range(3)], axis=0)
        kg0 = kr0 + shift * rows_per_blk
        entry, ok = [], []
        for a in range(half):
            r = r0 + g * half + a
            start = jnp.clip(r - NA_ROWS // 2, 0, rows - NA_ROWS)
            for bp in range(n_pairs):
                rk = kg0 + 2 * bp
                entry.append(jnp.clip(rk - r + NA_ROWS, 0, 2 * NA_ROWS - 1))
                in0 = ((rk >= start) & (rk < start + NA_ROWS)).astype(jnp.int32)
                in1 = ((rk + 1 >= start) & (rk + 1 < start + NA_ROWS)).astype(jnp.int32)
                ok.append(jnp.where(lane_lo, in0, in1) != 0)

        def bias_of(hh):
            rows_ = []
            for a in range(half):
                tiles = [jnp.where(ok[a * n_pairs + bp], tab_ref[hh, entry[a * n_pairs + bp]], NEG_INF)
                         for bp in range(n_pairs)]
                rows_.append(jnp.concatenate(tiles, axis=1))
            return jnp.concatenate(rows_, axis=0)

        q = q_ref[g * gq:(g + 1) * gq, :]
        out = jnp.zeros((gq, NA_HEADS * NA_DIM), F32)
        for hh in range(NA_HEADS):
            mh = lane == hh
            qh = jnp.where(mh, q, jnp.zeros_like(q))
            s_w = _dot_nt(qh, kw) + bias_of(hh)
            s_c = _dot_nt(qh, kc)
            m = jnp.maximum(jnp.max(s_w, axis=-1, keepdims=True), jnp.max(s_c, axis=-1, keepdims=True))
            p_w = jnp.exp2(s_w - m)
            p_c = jnp.exp2(s_c - m)
            l = jnp.sum(p_w, axis=-1, keepdims=True) + jnp.sum(p_c, axis=-1, keepdims=True)
            o = _dot(p_w.astype(BF16), vw) + _dot(p_c.astype(BF16), vc)
            out = out + jnp.where(mh, o / l, 0.0)
        o_ref[g * gq:(g + 1) * gq, :] = out.astype(o_ref.dtype)


def _na(nq, nk, nv, tab, *, batch, seq, ctx):
    r = nq.shape[0]
    tq = NA_QROWS * GRID_W
    nj = seq // tq
    kb = seq // NA_KBLK
    assert kb >= 4 and seq % tq == 0 and seq % ctx == 0
    ctx_blk0 = batch * seq // ctx

    def kspec(i):
        return pl.BlockSpec((NA_KBLK, 256),
                            lambda b, j: (b * kb + jnp.clip(2 * j - 1, 0, kb - 4) + i, 0))

    cspec = pl.BlockSpec((ctx, 256), lambda b, j: (ctx_blk0 + b, 0))
    return pl.pallas_call(
        functools.partial(_na_body, rows=seq // GRID_W, key_blocks=kb),
        grid=(batch, nj),
        in_specs=[pl.BlockSpec((tq, 256), lambda b, j: (b * nj + j, 0))]
                 + [kspec(i) for i in range(4)] + [kspec(i) for i in range(4)]
                 + [cspec, cspec, _resident(tab.shape)],
        out_specs=pl.BlockSpec((tq, 256), lambda b, j: (b * nj + j, 0)),
        out_shape=jax.ShapeDtypeStruct((r, 256), BF16),
        compiler_params=_params("parallel", "arbitrary"),
        name="na",
    )(nq, nk, nk, nk, nk, nv, nv, nv, nv, nk, nv, tab)


def _diff_lambda(lq_ref, lambda_init):
    lq = lq_ref[...]
    s1 = jnp.sum(lq[0:1, :] * lq[1:2, :], axis=-1, keepdims=True)
    s2 = jnp.sum(lq[2:3, :] * lq[3:4, :], axis=-1, keepdims=True)
    return jnp.exp(s1) - jnp.exp(s2) + lambda_init


def _stack_qt(qt):
    row = lax.broadcasted_iota(jnp.int32, (2 * DIFF_DK, 1), 0)
    zero = jnp.zeros_like(qt)
    return jnp.concatenate([jnp.where(row < DIFF_DK, qt, zero), jnp.where(row >= DIFF_DK, qt, zero)], axis=1)


def _diff_finish_t(acc, tq, lam, nw_col, lambda_init):
    o = acc[0:DIFF_DV, :] / acc[DIFF_DV:DIFF_DV + 1, :]
    od = o[:, :tq] - lam * o[:, tq:]
    ms = jnp.mean(od * od, axis=0, keepdims=True)
    y = od * lax.rsqrt(ms + RMS_EPS) * nw_col * (1.0 - lambda_init)
    y = jnp.concatenate([y, jnp.zeros_like(y)], axis=0)
    return jnp.transpose(y)[:, 0:DIFF_DV]


def _diff_body(qt_ref, kc_ref, vtc_ref, k_ref, vt_ref, lq_ref, nw_ref, o_ref, s_ref, *, tq, tk, unroll, lambda_init):
    n_tiles = qt_ref.shape[1] // tq
    lam = _diff_lambda(lq_ref, lambda_init)

    def finish(acc, t):
        q0 = t * tq if isinstance(t, int) else pl.multiple_of(t * tq, tq)
        o_ref[pl.ds(q0, tq), :] = _diff_finish_t(acc, tq, lam, nw_ref[...], lambda_init).astype(o_ref.dtype)

    def tile(t, acc_prev):
        q0 = pl.multiple_of(t * tq, tq)
        return _diff_tile(qt_ref[:, pl.ds(q0, tq)], kc_ref, vtc_ref, k_ref, vt_ref, s_ref, tk=tk, unroll=unroll,
                          after_issue=lambda: finish(acc_prev, jnp.maximum(t - 1, 0)))

    placeholder = jnp.ones((vt_ref.shape[0], 2 * tq), F32)
    finish(lax.fori_loop(0, n_tiles, tile, placeholder), n_tiles - 1)


def _diff_tile(qt, kc_ref, vtc_ref, k_ref, vt_ref, s_ref, *, tk, unroll, after_issue):
    tq = qt.shape[1]
    n_chunks = k_ref.shape[0] // tk
    ring = s_ref.shape[0]
    qs = _stack_qt(qt)

    def chunk(j):
        return pl.ds(j * tk if isinstance(j, int) else pl.multiple_of(j * tk, tk), tk)

    def issue_scores(j, slot):
        s = _dot(k_ref[chunk(j), :], qs)
        s_ref[slot] = s
        return jnp.max(s, axis=0, keepdims=True)

    s_ctx = _dot(kc_ref[...], qs)
    m0 = jnp.max(s_ctx, axis=0, keepdims=True)
    cmax0 = tuple(issue_scores(min(a, n_chunks - 1), a % ring) for a in range(DIFF_AHEAD))
    after_issue()
    acc0 = _dot(vtc_ref[...], jnp.exp2(s_ctx - m0).astype(BF16))

    def group(g, carry, tail=False):
        cmax, m, acc = carry
        for u in range(unroll):
            j = g * unroll + u
            if tail and j + DIFF_AHEAD >= n_chunks:
                c_new = cmax[0]
            else:
                c_new = issue_scores(j + DIFF_AHEAD, (u + DIFF_AHEAD) % ring)
            m_new = jnp.maximum(m, cmax[0])
            p = jnp.exp2(s_ref[u % ring] - m_new).astype(BF16)
            vt = vt_ref[:, chunk(j)]
            acc = jnp.exp2(m - m_new) * acc + _dot(vt, p)
            m, cmax = m_new, cmax[1:] + (c_new,)
        return cmax, m, acc

    n_groups = n_chunks // unroll
    trips = jnp.minimum(pl.program_id(2) + 1, 1) * (n_groups - 1)
    carry = lax.fori_loop(0, trips, group, (cmax0, m0, acc0))
    _, _, acc = group(n_groups - 1, carry, tail=True)
    return acc


def _diff(dqt, dk, dvt, lq, norm_w, *, batch, seq, ctx, lambda_init):
    h, r, _ = dk.shape
    tq = DIFF_TQ
    tqs = tq * min(DIFF_QTILES, seq // tq)
    nq = seq // tqs
    tk = min(DIFF_TK, seq)
    unroll = min(DIFF_UNROLL, seq // tk)
    assert (seq // tk) % unroll == 0 and unroll % DIFF_RING == 0 and DIFF_RING > DIFF_AHEAD and seq % tqs == 0
    ctx_blk0 = batch * seq // ctx
    return pl.pallas_call(
        functools.partial(_diff_body, tq=tq, tk=tk, unroll=unroll, lambda_init=lambda_init),
        grid=(batch, h, nq),
        in_specs=[pl.BlockSpec((None, 64, tqs), lambda b, hh, i: (hh, 0, b * nq + i)),
                  pl.BlockSpec((None, ctx, 64), lambda b, hh, i: (hh, ctx_blk0 + b, 0)),
                  pl.BlockSpec((None, DIFF_VROWS, ctx), lambda b, hh, i: (hh, 0, ctx_blk0 + b)),
                  pl.BlockSpec((None, seq, 64), lambda b, hh, i: (hh, b, 0)),
                  pl.BlockSpec((None, DIFF_VROWS, seq), lambda b, hh, i: (hh, 0, b)),
                  _resident(lq.shape),
                  _resident((DIFF_DV, 1))],
        out_specs=pl.BlockSpec((None, tqs, 64), lambda b, hh, i: (hh, b * nq + i, 0)),
        out_shape=jax.ShapeDtypeStruct((h, r, 64), BF16),
        scratch_shapes=[pltpu.VMEM((DIFF_RING, tk, 2 * tq), F32)],
        compiler_params=_params("parallel", "parallel", "arbitrary"),
        name="diff",
    )(dqt, dk, dvt, dk, dvt, lq, norm_w.reshape(DIFF_DV, 1))


def _ctx_attn_body(nq_ref, nk_ref, nv_ref, dqt_ref, dk_ref, dvt_ref, lq_ref, nw_ref, na_in, df_in,
                   na_out, df_out, *, lambda_init):
    del na_in, df_in
    q = nq_ref[...]
    k = nk_ref[...]
    v = nv_ref[...]
    lane = lax.broadcasted_iota(jnp.int32, (1, NA_HEADS * NA_DIM), 1) // NA_DIM
    out = jnp.zeros(na_out.shape, F32)
    for hh in range(NA_HEADS):
        mh = lane == hh
        s = _dot_nt(jnp.where(mh, q, jnp.zeros_like(q)), k)
        p = jnp.exp2(s - jnp.max(s, axis=-1, keepdims=True))
        o = _dot(p.astype(BF16), v) / jnp.sum(p, axis=-1, keepdims=True)
        out = out + jnp.where(mh, o, 0.0)
    na_out[...] = out.astype(na_out.dtype)

    lam = _diff_lambda(lq_ref, lambda_init)
    tq = dqt_ref.shape[2]
    for hh in range(DIFF_HEADS):
        s = _dot(dk_ref[hh], _stack_qt(dqt_ref[hh]))
        p = jnp.exp2(s - jnp.max(s, axis=0, keepdims=True))
        acc = _dot(dvt_ref[hh], p.astype(BF16))
        df_out[hh] = _diff_finish_t(acc, tq, lam, nw_ref[...], lambda_init).astype(df_out.dtype)


def _ctx_attn(nq, nk, nv, dqt, dk, dvt, lq, norm_w, na_o, df_o, *, batch, seq, ctx, lambda_init):
    blk0 = batch * seq // ctx
    s256 = pl.BlockSpec((ctx, 256), lambda b: (blk0 + b, 0))
    sh64 = pl.BlockSpec((DIFF_HEADS, ctx, 64), lambda b: (0, blk0 + b, 0))
    sq_t = pl.BlockSpec((DIFF_HEADS, 64, ctx), lambda b: (0, 0, blk0 + b))
    sv_t = pl.BlockSpec((DIFF_HEADS, DIFF_VROWS, ctx), lambda b: (0, 0, blk0 + b))
    return pl.pallas_call(
        functools.partial(_ctx_attn_body, lambda_init=lambda_init),
        grid=(batch,),
        in_specs=[s256, s256, s256, sq_t, sh64, sv_t, _resident(lq.shape), _resident((DIFF_DV, 1)),
                  pl.BlockSpec(memory_space=pl.ANY), pl.BlockSpec(memory_space=pl.ANY)],
        out_specs=[s256, sh64],
        out_shape=[jax.ShapeDtypeStruct(na_o.shape, na_o.dtype), jax.ShapeDtypeStruct(df_o.shape, df_o.dtype)],
        input_output_aliases={8: 0, 9: 1},
        compiler_params=_params("arbitrary"),
        name="ctx_attn",
    )(nq, nk, nv, dqt, dk, dvt, lq, norm_w.reshape(DIFF_DV, 1), na_o, df_o)


def _conv_body(u_ref, prev_ref, next_ref, dw_ref, dwb_ref, lng_ref, lnb_ref, pw_ref, pwb_ref,
               o_ref, pad_ref, sh_ref, *, tiles_per_seq, n_latent_tiles):
    t = u_ref.shape[0]
    i = pl.program_id(0)
    in_ctx = i >= n_latent_tiles
    first = in_ctx | (i % tiles_per_seq == 0)
    last = in_ctx | (i % tiles_per_seq == tiles_per_seq - 1)
    pad_ref[0:HALO, :] = jnp.where(first, 0.0, prev_ref[...])
    pad_ref[HALO:HALO + t, :] = u_ref[...]
    pad_ref[HALO + t:HALO + t + HALO, :] = jnp.where(last, 0.0, next_ref[...])
    base = HALO - CONV_K // 2
    first = {}
    for phase in range(8):
        taps = [k for k in range(CONV_K) if (base + k) % 8 == phase]
        first[phase] = base + taps[0]
        span = base + taps[-1] + t - first[phase]
        sh_ref[phase, 0:span, :] = pad_ref[first[phase]:first[phase] + span, :]
    sub = 64
    parts = []
    for r0 in range(0, t, sub):
        acc = jnp.zeros((sub, CONV_CH), F32)
        for k in range(CONV_K):
            phase = (base + k) % 8
            off = base + k - first[phase] + r0
            acc = acc + dw_ref[k:k + 1, :] * sh_ref[phase, off:off + sub, :]
        parts.append(acc)
    y = jnp.concatenate(parts, axis=0) + dwb_ref[...]
    mu = jnp.mean(y, axis=-1, keepdims=True)
    yc = y - mu
    var = jnp.mean(yc * yc, axis=-1, keepdims=True)
    y = _silu(yc * lax.rsqrt(var + LN_EPS) * lng_ref[...] + lnb_ref[...])
    o_ref[...] = (_dot(y.astype(BF16), pw_ref[...]) + pwb_ref[...]).astype(o_ref.dtype)


def _conv(cu, dw, dw_b, ln_g, ln_b, pw, pw_b, *, batch, seq, ctx):
    r = cu.shape[0]
    t = CONV_TILE
    assert ctx == t and seq % t == 0
    n_tiles = r // t
    hb = t // HALO
    vec = lambda a: a.reshape(1, CONV_CH)
    return pl.pallas_call(
        functools.partial(_conv_body, tiles_per_seq=seq // t, n_latent_tiles=batch * seq // t),
        grid=(n_tiles,),
        in_specs=[pl.BlockSpec((t, CONV_CH), lambda i: (i, 0)),
                  pl.BlockSpec((HALO, CONV_CH), lambda i: (jnp.maximum(i * hb - 1, 0), 0)),
                  pl.BlockSpec((HALO, CONV_CH), lambda i: (jnp.minimum((i + 1) * hb, n_tiles * hb - 1), 0)),
                  _resident((CONV_K, CONV_CH))] + [_resident((1, CONV_CH))] * 3
                 + [_resident((CONV_CH, CONV_CH)), _resident((1, CONV_CH))],
        out_specs=pl.BlockSpec((t, CONV_CH), lambda i: (i, 0)),
        out_shape=jax.ShapeDtypeStruct((r, CONV_CH), BF16),
        scratch_shapes=[pltpu.VMEM((t + 2 * HALO, CONV_CH), F32),
                        pltpu.VMEM((8, t + 2 * HALO, CONV_CH), F32)],
        compiler_params=_params("parallel"),
        name="conv",
    )(cu, cu, cu, dw, vec(dw_b), vec(ln_g), vec(ln_b), pw.astype(BF16), vec(pw_b))


def kernel(x, c, ctx, c_ctx, ada_w, ada_b, norm_ffn1, ffn1_w13, ffn1_w2, norm_mix, w_in, gla_wa_f, gla_ba_f, gla_wa_b, gla_ba_b, gla_norm, na_rpb, diff_lq1, diff_lk1, diff_lq2, diff_lk2, diff_norm, conv_dw, conv_dw_b, conv_ln_g, conv_ln_b, conv_pw, conv_pw_b, w_out, norm_ffn2, ffn2_w13, ffn2_w2, final_norm):
    batch, seq, d = x.shape
    n_ctx = ctx.shape[1]
    depth = ada_w.shape[0]
    tm = ROW_TILE
    assert seq % tm == 0 and (batch * n_ctx) % tm == 0 and batch + 1 <= 8
    lat_tiles = batch * seq // tm
    all_tiles = lat_tiles + batch * n_ctx // tm
    tiles_per_batch = seq // tm

    def group_of(i):
        return jnp.minimum(i // tiles_per_batch, batch)

    def pos_of(i):
        return jnp.where(i < lat_tiles, i % tiles_per_batch, tiles_per_batch)

    c_rows = jnp.concatenate([c, c_ctx[None, :], jnp.zeros((8 - batch - 1, d), F32)], axis=0)
    mods_all = _ada(c_rows, ada_w, ada_b)[:, :batch + 1].reshape(depth, batch + 1, N_MOD, d)
    rope = _rope_tables(seq, tm)
    h = x.reshape(batch * seq, d)
    h_ctx = ctx.reshape(batch * n_ctx, d)

    for i in range(depth):
        last = i == depth - 1
        lambda_init = 0.8 - 0.6 * math.exp(-0.3 * i)
        mods = mods_all[i]
        tok = dict(n_tiles=all_tiles, group_of=group_of)
        geo = dict(batch=batch, seq=seq, ctx=n_ctx)

        h = _ffn(h, mods, norm_ffn1[i], ffn1_w13[i], ffn1_w2[i], final_norm, k0=0, final=False,
                 h_ctx=h_ctx if i == 0 else None, **tok)

        wa, ba = _gate_weights(gla_wa_f[i], gla_ba_f[i], gla_wa_b[i], gla_ba_b[i])
        (gqk, gv, gvt, gg, glg, nq, nk, nv, dq, dk, dv, cu) = _proj(
            h, mods, norm_mix[i], _permute_w_in(w_in[i]), wa, ba, rope, pos_of=pos_of, **tok)

        gnorm = jnp.tile(gla_norm[i], GLA_HEADS).reshape(1, GLA_HEADS * GLA_DV)
        o_f = _gla(gqk, gv, gvt, glg, gg, None, gnorm, reverse=False, **geo)
        gx = _gla(gqk, gv, gvt, glg, gg, o_f, gnorm, reverse=True, **geo)

        nx = _na(nq, nk, nv, _na_col_tables(na_rpb[i]), **geo)

        lq = jnp.stack([diff_lq1[i], diff_lk1[i], diff_lq2[i], diff_lk2[i]])
        dx = _diff(dq, dk, dv, lq, diff_norm[i], lambda_init=lambda_init, **geo)
        if not last:
            nx, dx = _ctx_attn(nq, nk, nv, dq, dk, dv, lq, diff_norm[i], nx, dx,
                               lambda_init=lambda_init, **geo)

        cx = _conv(cu, conv_dw[i], conv_dw_b[i], conv_ln_g[i], conv_ln_b[i], conv_pw[i], conv_pw_b[i], **geo)

        if last:
            tok = dict(n_tiles=lat_tiles, group_of=group_of)
        h = _ffn(h, mods, norm_ffn2[i], ffn2_w13[i], ffn2_w2[i], final_norm, k0=6, final=last,
                 mixers=(gx, nx, dx, cx, w_out[i], (batch, seq, n_ctx)), **tok)

    return h.reshape(batch, seq, d)
```

```python
import functools
import math

import jax
import jax.numpy as jnp
from jax import lax
from jax.experimental import pallas as pl
from jax.experimental.pallas import tpu as pltpu

F32 = jnp.float32
BF16 = jnp.bfloat16

GRID_W = 64
N_MOD = 9
RMS_EPS = 1e-6
LN_EPS = 1e-5
NEG_INF = -1e30
GLA_HEADS, GLA_DK, GLA_DV, GLA_RANK, GLA_TAU, GLA_CHUNK = 4, 32, 64, 16, 16.0, 64
NA_HEADS, NA_DIM, NA_ROWS, NA_COLS = 4, 64, 8, 16
DIFF_HEADS, DIFF_DK, DIFF_DV = 4, 32, 64
DIFF_VROWS = 80
CONV_CH, CONV_K = 256, 31
ROPE_BASE = 10000.0
LOG2E = 1.4426950408889634

LANES = 128
VMEM_LIMIT = 56 * 1024 * 1024

ROW_TILE = 512
FF_CHUNK = 256
GLA_BLOCK = 256
NA_QROWS = 8
NA_KBLK = 256
CONV_TILE = 256
DIFF_TQ = 256
DIFF_QTILES = 8
DIFF_TK = 256
DIFF_UNROLL = 32
DIFF_AHEAD = 4
DIFF_RING = 8
HALO = 16

C_GLA, C_NA, C_DIFF, C_CONV, C_AUX, C_END = 0, 768, 1536, 2304, 2816, 2944


def _dot(a, b):
    return jnp.dot(a, b, preferred_element_type=F32)


def _dot_nt(a, b):
    return lax.dot_general(a, b, (((1,), (1,)), ((), ())), preferred_element_type=F32)


def _params(*sem):
    return pltpu.CompilerParams(dimension_semantics=sem, vmem_limit_bytes=VMEM_LIMIT)


def _resident(shape):
    nd = len(shape)
    return pl.BlockSpec(shape, lambda *_: (0,) * nd, pipeline_mode=pl.Buffered(1))


def _silu(x):
    return x * jax.nn.sigmoid(x)


def _rms(x, w):
    return x * lax.rsqrt(jnp.mean(x * x, axis=-1, keepdims=True) + RMS_EPS) * w


def _ada_body(c_ref, w_ref, b_ref, o_ref):
    s = _silu(c_ref[...])
    o_ref[...] = jnp.dot(s, w_ref[...], precision=lax.Precision.HIGHEST,
                         preferred_element_type=F32) + b_ref[...]


def _ada(c_rows, ada_w, ada_b):
    depth, d, _ = ada_w.shape
    return pl.pallas_call(
        _ada_body,
        grid=(depth, N_MOD),
        in_specs=[pl.BlockSpec((8, d), lambda l, n: (0, 0)),
                  pl.BlockSpec((None, d, d), lambda l, n: (l, 0, n)),
                  pl.BlockSpec((None, 1, d), lambda l, n: (l, 0, n))],
        out_specs=pl.BlockSpec((None, 8, d), lambda l, n: (l, 0, n)),
        out_shape=jax.ShapeDtypeStruct((depth, 8, N_MOD * d), F32),
        compiler_params=_params("arbitrary", "arbitrary"),
        name="ada",
    )(c_rows, ada_w, ada_b.reshape(depth, 1, N_MOD * d))


def _ffn_body(*refs, k0, final, source, n_first):
    h_ref = refs[0]
    n_extra = {"plain": 0, "two_arrays": 1, "mixers": 6}[source]
    extra = refs[1:1 + n_extra]
    mod_ref, nw_ref, w13_ref, w2_ref, fw_ref, o_ref, xb_ref, g_ref, h13_ref = refs[1 + n_extra:]
    ff = w2_ref.shape[0]
    tf = h13_ref.shape[2] // 2
    n_chunks = ff // tf
    x = h_ref[...]
    if source == "two_arrays":
        x = jnp.where(pl.program_id(0) < n_first, x, extra[0][...])
    elif source == "mixers":
        gx_ref, gxc_ref, nx_ref, dx_ref, cx_ref, wo_ref = extra
        gx = jnp.where(pl.program_id(0) < n_first, gx_ref[...], gxc_ref[...].reshape(gx_ref.shape))
        mix = jnp.concatenate([gx, nx_ref[...]] + [dx_ref[hh] for hh in range(DIFF_HEADS)]
                              + [cx_ref[...]], axis=-1)
        x = x + mod_ref[5:6, :] * _dot(mix, wo_ref[...])
    xm = _rms(x, nw_ref[...]) * (1.0 + mod_ref[k0 + 1:k0 + 2, :]) + mod_ref[k0:k0 + 1, :]
    xb_ref[...] = xm.astype(BF16)

    def cols(c):
        return c * tf if isinstance(c, int) else pl.multiple_of(c * tf, tf)

    def up(c, slot):
        h13_ref[slot, :, :tf] = _dot(xb_ref[...], w13_ref[:, pl.ds(cols(c), tf)])
        h13_ref[slot, :, tf:] = _dot(xb_ref[...], w13_ref[:, pl.ds(ff + cols(c), tf)])

    def gate(c, slot):
        a = h13_ref[slot, :, :tf]
        u = h13_ref[slot, :, tf:]
        g_ref[:, pl.ds(cols(c), tf)] = (_silu(a) * u).astype(BF16)

    def pair(t, carry):
        up(2 * t + 1, 1)
        gate(2 * t, 0)
        up(2 * t + 2, 0)
        gate(2 * t + 1, 1)
        return carry

    up(0, 0)
    for t in range((n_chunks - 1) // 2):
        pair(t, 0)
    gate(n_chunks - 1, 0)
    out = x + (0.5 * mod_ref[k0 + 2:k0 + 3, :]) * _dot(g_ref[...], w2_ref[...])
    if final:
        out = _rms(out, fw_ref[...])
    o_ref[...] = out


def _ffn(h, mods, norm_w, w13, w2, final_w, *, k0, n_tiles, group_of, final, h_ctx=None, mixers=None):
    d = h.shape[1]
    tm = ROW_TILE
    ff = w2.shape[0]
    n_chunks = ff // FF_CHUNK
    assert ff % FF_CHUNK == 0 and n_chunks % 2 == 1
    row = lambda i: (i, 0)
    source, n_first, extra, extra_specs, h_spec = "plain", None, [], [], pl.BlockSpec((tm, d), row)
    if h_ctx is not None:
        source, n_first, extra = "two_arrays", h.shape[0] // tm, [h_ctx]
        h_spec = pl.BlockSpec((tm, d), lambda i: (jnp.minimum(i, n_first - 1), 0))
        extra_specs = [pl.BlockSpec((tm, d), lambda i: (jnp.maximum(i - n_first, 0), 0))]
    elif mixers is not None:
        gx, nx, dx, cx, w_out, (batch, seq, n_ctx) = mixers
        assert batch * n_ctx == tm and seq % tm == 0 and seq % n_ctx == 0
        source, n_first, extra = "mixers", batch * seq // tm, [gx, gx, nx, dx, cx, w_out.astype(BF16)]
        per_seq = seq // tm
        extra_specs = [pl.BlockSpec((None, tm, 256),
                                    lambda i: (jnp.minimum(i // per_seq, batch - 1), i % per_seq, 0)),
                       pl.BlockSpec((batch, n_ctx, 256), lambda i: (0, seq // n_ctx, 0)),
                       pl.BlockSpec((tm, 256), row),
                       pl.BlockSpec((DIFF_HEADS, tm, 64), lambda i: (0, i, 0)),
                       pl.BlockSpec((tm, 256), row), _resident(w_out.shape)]
    return pl.pallas_call(
        functools.partial(_ffn_body, k0=k0, final=final, source=source, n_first=n_first),
        grid=(n_tiles,),
        in_specs=[h_spec] + extra_specs + [
            pl.BlockSpec((None, N_MOD, d), lambda i: (group_of(i), 0, 0)),
            _resident((1, d)),
            _resident(w13.shape),
            _resident(w2.shape),
            _resident((1, d))],
        out_specs=pl.BlockSpec((tm, d), row),
        out_shape=jax.ShapeDtypeStruct((n_tiles * tm, d), F32),
        scratch_shapes=[pltpu.VMEM((tm, d), BF16), pltpu.VMEM((tm, ff), BF16),
                        pltpu.VMEM((2, tm, 2 * FF_CHUNK), F32)],
        compiler_params=_params("parallel"),
        name="ffn",
    )(h, *extra, mods, norm_w.reshape(1, d), w13.astype(BF16), w2.astype(BF16), final_w.reshape(1, d))


def _log_sigmoid(x):
    return jnp.minimum(x, 0.0) - jnp.log1p(jnp.exp(-jnp.abs(x)))


def _rope_rotate(x):
    n = x.shape[-1]
    lane = lax.broadcasted_iota(jnp.int32, x.shape, 1)
    up = pltpu.roll(x, n - 8, 1)
    dn = pltpu.roll(x, 8, 1)
    return jnp.where((lane & 15) < 8, -up, dn)


def _rope_rotate_rows(x):
    n = x.shape[0]
    row = lax.broadcasted_iota(jnp.int32, x.shape, 0)
    up = pltpu.roll(x, n - 8, 0)
    dn = pltpu.roll(x, 8, 0)
    return jnp.where((row & 15) < 8, -up, dn)


def _proj_body(h_ref, mod_ref, nw_ref, w_ref, wt_ref, wa_ref, ba_ref,
               cos_ref, sin_ref, cost_ref, sint_ref,
               gqk_ref, gv_ref, gvt_ref, gg_ref, glg_ref, nq_ref, nk_ref, nv_ref,
               dqt_ref, dk_ref, dvt_ref, cu_ref, xb_ref):
    x = h_ref[...]
    tm = x.shape[0]
    xm = _rms(x, nw_ref[...]) * (1.0 + mod_ref[4:5, :]) + mod_ref[3:4, :]
    xb_ref[...] = xm.astype(BF16)

    z = _dot(xb_ref[...], w_ref[:, C_GLA:C_GLA + 256])
    lane = lax.broadcasted_iota(jnp.int32, (1, 256), 1)
    gqk_ref[...] = z * jnp.where(lane < 128, GLA_DK ** -0.5, 1.0)
    gv_ref[...] = _dot(xb_ref[...], w_ref[:, C_GLA + 256:C_GLA + 512]).astype(BF16)
    zt = _dot_nt(wt_ref[...], xb_ref[...])
    gvt_ref[...] = zt[0:256, :].astype(BF16)
    gg_ref[...] = _dot(xb_ref[...], w_ref[:, C_GLA + 512:C_GLA + 768]).astype(BF16)
    aux = _dot(xb_ref[...], w_ref[:, C_AUX:C_END])
    pre = _dot(aux.astype(BF16), wa_ref[...]) + ba_ref[...]
    glg_ref[...] = _log_sigmoid(pre) * (1.0 / GLA_TAU)

    nq_ref[...] = (_dot(xb_ref[...], w_ref[:, C_NA:C_NA + 256]) * (NA_DIM ** -0.5 * LOG2E)).astype(BF16)
    nk_ref[...] = _dot(xb_ref[...], w_ref[:, C_NA + 256:C_NA + 512]).astype(BF16)
    nv_ref[...] = _dot(xb_ref[...], w_ref[:, C_NA + 512:C_NA + 768]).astype(BF16)

    cos = cos_ref[...]
    sin = sin_ref[...]
    cos2 = jnp.concatenate([cos, cos], axis=1)
    sin2 = jnp.concatenate([sin, sin], axis=1)
    zk = _dot(xb_ref[...], w_ref[:, C_DIFF + 256:C_DIFF + 512])
    zk = zk * cos2 + _rope_rotate(zk) * sin2
    n_rep = 2 * DIFF_HEADS
    cos_t = jnp.concatenate([cost_ref[...]] * n_rep, axis=0)
    sin_t = jnp.concatenate([sint_ref[...]] * n_rep, axis=0)
    zqt = zt[256:512, :]
    zqt = (zqt * cos_t + _rope_rotate_rows(zqt) * sin_t) * (DIFF_DK ** -0.5 * LOG2E)
    zvt = zt[512:768, :]
    pad_rows = DIFF_VROWS - DIFF_DV
    one_row = jnp.where(lax.broadcasted_iota(jnp.int32, (pad_rows, tm), 0) == 0, 1.0, 0.0).astype(BF16)
    for hh in range(DIFF_HEADS):
        sl = slice(64 * hh, 64 * hh + 64)
        dk_ref[hh] = zk[:, sl].astype(BF16)
        dqt_ref[hh] = zqt[sl, :].astype(BF16)
        dvt_ref[hh, 0:64, :] = zvt[sl, :].astype(BF16)
        dvt_ref[hh, DIFF_DV:DIFF_VROWS, :] = one_row

    za = _dot(xb_ref[...], w_ref[:, C_CONV:C_CONV + 256])
    zg = _dot(xb_ref[...], w_ref[:, C_CONV + 256:C_CONV + 512])
    cu_ref[...] = za * jax.nn.sigmoid(zg)


def _proj(h, mods, norm_w, w_p, wa, ba, rope, *, n_tiles, group_of, pos_of):
    r, d = h.shape
    tm = ROW_TILE
    row = lambda i: (i, 0)
    hrow = lambda i: (0, i, 0)
    hcol = lambda i: (0, 0, i)
    f32_256 = jax.ShapeDtypeStruct((r, 256), F32)
    bf_256 = jax.ShapeDtypeStruct((r, 256), BF16)
    w_t = jnp.concatenate([w_p[:, C_GLA + 256:C_GLA + 512], w_p[:, C_DIFF:C_DIFF + 256],
                           w_p[:, C_DIFF + 512:C_DIFF + 768]], axis=1).T
    cos_r, sin_r, cos_c, sin_c = rope
    out_shape = [f32_256, bf_256, jax.ShapeDtypeStruct((256, r), BF16), bf_256, f32_256,
                 bf_256, bf_256, bf_256,
                 jax.ShapeDtypeStruct((DIFF_HEADS, 64, r), BF16),
                 jax.ShapeDtypeStruct((DIFF_HEADS, r, 64), BF16),
                 jax.ShapeDtypeStruct((DIFF_HEADS, DIFF_VROWS, r), BF16),
                 f32_256]
    out_specs = [pl.BlockSpec((tm, 256), row)] * 2 + [pl.BlockSpec((256, tm), lambda i: (0, i))] + [
        pl.BlockSpec((tm, 256), row)] * 5 + [
        pl.BlockSpec((DIFF_HEADS, 64, tm), hcol),
        pl.BlockSpec((DIFF_HEADS, tm, 64), hrow),
        pl.BlockSpec((DIFF_HEADS, DIFF_VROWS, tm), hcol),
        pl.BlockSpec((tm, 256), row)]
    return pl.pallas_call(
        _proj_body,
        grid=(n_tiles,),
        in_specs=[pl.BlockSpec((tm, d), row),
                  pl.BlockSpec((None, N_MOD, d), lambda i: (group_of(i), 0, 0)),
                  _resident((1, d)),
                  _resident(w_p.shape),
                  _resident(w_t.shape),
                  _resident(wa.shape),
                  _resident(ba.shape),
                  pl.BlockSpec((tm, LANES), lambda i: (pos_of(i), 0)),
                  pl.BlockSpec((tm, LANES), lambda i: (pos_of(i), 0)),
                  pl.BlockSpec((DIFF_DK, tm), lambda i: (0, pos_of(i))),
                  pl.BlockSpec((DIFF_DK, tm), lambda i: (0, pos_of(i)))],
        out_specs=out_specs,
        out_shape=out_shape,
        scratch_shapes=[pltpu.VMEM((tm, d), BF16)],
        compiler_params=_params("parallel"),
        name="proj",
    )(h, mods, norm_w.reshape(1, d), w_p, w_t, wa, ba, cos_r, sin_r, cos_c, sin_c)


def _permute_w_in(w_in):
    d = w_in.shape[0]
    g0 = 2 * GLA_HEADS * GLA_DK + 2 * GLA_HEADS * GLA_DV
    aux = w_in[:, g0:g0 + 2 * GLA_RANK]
    rest = w_in[:, g0 + 2 * GLA_RANK:]
    pad = jnp.zeros((d, C_END - C_AUX - 2 * GLA_RANK), w_in.dtype)
    return jnp.concatenate([w_in[:, :g0], rest, aux, pad], axis=1).astype(BF16)


def _gate_weights(wa_f, ba_f, wa_b, ba_b):
    n = GLA_HEADS * GLA_DK
    wa = jnp.zeros((C_END - C_AUX, 2 * n), F32)
    wa = wa.at[:GLA_RANK, :n].set(wa_f).at[GLA_RANK:2 * GLA_RANK, n:].set(wa_b)
    return wa.astype(BF16), jnp.concatenate([ba_f, ba_b]).reshape(1, 2 * n)


def _rope_tables(seq, tile):
    t = jnp.arange(seq)
    row = (t // GRID_W).astype(F32)
    col = (t % GRID_W).astype(F32)
    half = DIFF_DK // 2
    inv = 1.0 / (ROPE_BASE ** (jnp.arange(0, half, 2, dtype=F32) / half))
    ang_r = row[:, None] * inv
    ang_c = col[:, None] * inv
    ang = jnp.concatenate([ang_r, ang_r, ang_c, ang_c], axis=-1)
    cos = jnp.concatenate([jnp.cos(ang), jnp.ones((tile, DIFF_DK), F32)], axis=0)
    sin = jnp.concatenate([jnp.sin(ang), jnp.zeros((tile, DIFF_DK), F32)], axis=0)
    rep = (1, LANES // DIFF_DK)
    return jnp.tile(cos, rep), jnp.tile(sin, rep), cos.T, sin.T


def _split3(x):
    hi = x.astype(BF16)
    r1 = x - hi.astype(F32)
    mid = r1.astype(BF16)
    lo = (r1 - mid.astype(F32)).astype(BF16)
    return hi, mid, lo


def _gla_body(*refs, reverse, batch):
    per_b = [refs[5 * b:5 * b + 5] for b in range(batch)]
    of_ref, nw_ref, o_ref, st_ref = refs[5 * batch:]
    blk = per_b[0][0].shape[0]
    n_chunks = blk // GLA_CHUNK
    nqk = GLA_HEADS * GLA_DK
    nv = GLA_HEADS * GLA_DV

    @pl.when(pl.program_id(0) == 0)
    def _():
        st_ref[...] = jnp.zeros_like(st_ref)

    ri = lax.broadcasted_iota(jnp.int32, (blk, blk), 0)
    ci = lax.broadcasted_iota(jnp.int32, (blk, blk), 1)
    same = (ri // GLA_CHUNK) == (ci // GLA_CHUNK)
    tri = jnp.where(same & ((ci >= ri) if reverse else (ci <= ri)), 1.0, 0.0).astype(BF16)
    bcum_of = []
    for qk_ref, v_ref, vt_ref, lg_ref, g_ref in per_b:
        lg = lg_ref[:, nqk:2 * nqk] if reverse else lg_ref[:, 0:nqk]
        hi, mid, lo = _split3(lg)
        bcum_of.append(_dot(tri, hi) + _dot(tri, mid) + _dot(tri, lo))

    c = GLA_CHUNK
    rk = lax.broadcasted_iota(jnp.int32, (GLA_HEADS * c, nqk), 0) // c
    ck = lax.broadcasted_iota(jnp.int32, (GLA_HEADS * c, nqk), 1) // GLA_DK
    mask_k = rk == ck
    rv = lax.broadcasted_iota(jnp.int32, (GLA_HEADS * c, nv), 0) // c
    cv = lax.broadcasted_iota(jnp.int32, (GLA_HEADS * c, nv), 1) // GLA_DV
    mask_v = rv == cv
    rs = lax.broadcasted_iota(jnp.int32, (nv, nqk), 0) // GLA_DV
    cs = lax.broadcasted_iota(jnp.int32, (nv, nqk), 1) // GLA_DK
    mask_s = rs == cs
    ai = lax.broadcasted_iota(jnp.int32, (c, GLA_HEADS * c), 0)
    aj = lax.broadcasted_iota(jnp.int32, (c, GLA_HEADS * c), 1) % c
    mask_a = (aj >= ai) if reverse else (aj <= ai)

    order = range(n_chunks - 1, -1, -1) if reverse else range(n_chunks)
    units = [(b, ch) for ch in order for b in range(batch)]
    q_in, a_raw, u_t, decay = {}, {}, {}, {}
    vts = [refs_b[2][...].astype(BF16) for refs_b in per_b]
    for b, ch in units:
        qk_ref = per_b[b][0]
        rows = slice(ch * c, (ch + 1) * c)
        k = qk_ref[rows, nqk:2 * nqk]
        bcum = bcum_of[b][rows, :]
        btot = bcum[0:1, :] if reverse else bcum[c - 1:c, :]
        q_in[b, ch] = (qk_ref[rows, 0:nqk] * jnp.exp(bcum)).astype(BF16)
        k_in = k * jnp.exp(-bcum)
        k_out = (k * jnp.exp(btot - bcum)).astype(BF16)
        k_bd = jnp.where(mask_k, jnp.concatenate([k_in] * GLA_HEADS, axis=0), 0.0).astype(BF16)
        a_raw[b, ch] = _dot_nt(q_in[b, ch], k_bd)
        k_pad = jnp.concatenate([jnp.zeros((n * c, nqk), BF16) for n in (ch,) if n] + [k_out]
                                + [jnp.zeros((n * c, nqk), BF16) for n in (n_chunks - 1 - ch,) if n], axis=0)
        u_t[b, ch] = jnp.where(mask_s, _dot(vts[b], k_pad), 0.0)
        decay[b, ch] = jnp.exp(btot)
    o = {}
    for b, ch in units:
        v = per_b[b][1][ch * c:(ch + 1) * c, :]
        v4 = jnp.concatenate([v] * GLA_HEADS, axis=0)
        v_bd = jnp.where(mask_v, v4, jnp.zeros_like(v4))
        o[b, ch] = _dot(jnp.where(mask_a, a_raw[b, ch], 0.0).astype(BF16), v_bd)
    st = [st_ref[b] for b in range(batch)]
    for b, ch in units:
        o[b, ch] = o[b, ch] + _dot_nt(q_in[b, ch], st[b].astype(BF16))
        st[b] = st[b] * decay[b, ch] + u_t[b, ch]
    for b in range(batch):
        st_ref[b] = st[b]
        o_blk = jnp.concatenate([o[b, ch] for ch in range(n_chunks)], axis=0)
        if reverse:
            o_blk = o_blk + of_ref[b]
            hi2, lo2, _ = _split3(o_blk * o_blk)
            hr = lax.broadcasted_iota(jnp.int32, (nv, nv), 0) // GLA_DV
            hc = lax.broadcasted_iota(jnp.int32, (nv, nv), 1) // GLA_DV
            seg = jnp.where(hr == hc, 1.0, 0.0).astype(BF16)
            ms = (_dot(hi2, seg) + _dot(lo2, seg)) * (1.0 / GLA_DV)
            o_blk = o_blk * lax.rsqrt(ms + RMS_EPS) * nw_ref[...] * _silu(per_b[b][4][...].astype(F32))
        o_ref[b] = o_blk.astype(o_ref.dtype)


def _gla(gqk, gv, gvt, glg, gg, o_f, norm_w4, *, reverse, batch, seq, ctx):
    blk = GLA_BLOCK
    nc, nl = ctx // blk, seq // blk
    ctx_base = batch * seq // blk

    def step_blk(s):
        if reverse:
            return jnp.where(s < nc, nl + (nc - 1 - s), nl - 1 - (s - nc))
        return jnp.where(s < nc, nl + s, s - nc)

    def row_blk(b):
        return lambda s: jnp.where(s < nc, ctx_base + b * nc - nl, b * nl) + step_blk(s)

    specs, args = [], []
    for b in range(batch):
        spec = pl.BlockSpec((blk, 256), lambda s, f=row_blk(b): (f(s), 0))
        spec_t = pl.BlockSpec((256, blk), lambda s, f=row_blk(b): (0, f(s)))
        if reverse:
            specs += [spec, spec, spec_t, spec, spec]
            args += [gqk, gv, gvt, glg, gg]
        else:
            specs += [spec, spec, spec_t, spec, _resident((1, 256))]
            args += [gqk, gv, gvt, glg, norm_w4]
    seq_spec = pl.BlockSpec((batch, blk, 256), lambda s: (0, step_blk(s), 0))
    if o_f is None:
        o_f, of_spec = norm_w4, _resident((1, 256))
    else:
        of_spec = seq_spec
    return pl.pallas_call(
        functools.partial(_gla_body, reverse=reverse, batch=batch),
        grid=(nc + nl,),
        in_specs=specs + [of_spec, _resident((1, 256))],
        out_specs=seq_spec,
        out_shape=jax.ShapeDtypeStruct((batch, seq + ctx, 256), BF16 if reverse else F32),
        scratch_shapes=[pltpu.VMEM((batch, GLA_HEADS * GLA_DV, GLA_HEADS * GLA_DK), F32)],
        compiler_params=_params("arbitrary"),
        name="gla_bwd" if reverse else "gla_fwd",
    )(*args, o_f, norm_w4)


def _na_col_tables(rpb):
    col = jnp.arange(GRID_W)
    cstart = jnp.clip(col - NA_COLS // 2, 0, GRID_W - NA_COLS)
    col_ok = (col[None, :] >= cstart[:, None]) & (col[None, :] < cstart[:, None] + NA_COLS)
    col_off = jnp.clip(col[None, :] - col[:, None] + (NA_COLS - 1), 0, 2 * NA_COLS - 2)
    t = jnp.where(col_ok[None, None], rpb[:, :, col_off] * LOG2E, NEG_INF)
    dead = jnp.full((NA_HEADS, 1, GRID_W, GRID_W), NEG_INF, F32)
    t = jnp.concatenate([dead, t.astype(F32), dead], axis=1)
    return jnp.concatenate([t[:, :-1], t[:, 1:]], axis=-1)


def _na_body(q_ref, k0, k1, k2, k3, v0, v1, v2, v3, kc_ref, vc_ref, tab_ref, o_ref, *, rows, key_blocks):
    k_blk = [k0[...], k1[...], k2[...], k3[...]]
    v_blk = [v0[...], v1[...], v2[...], v3[...]]
    kc = kc_ref[...]
    vc = vc_ref[...]

    j = pl.program_id(1)
    rows_per_blk = NA_KBLK // GRID_W
    r0 = j * NA_QROWS
    kr0 = jnp.clip(2 * j - 1, 0, key_blocks - 4) * rows_per_blk
    lane_lo = lax.broadcasted_iota(jnp.int32, (1, 2 * GRID_W), 1) < GRID_W
    lane = lax.broadcasted_iota(jnp.int32, (1, NA_HEADS * NA_DIM), 1) // NA_DIM
    half = NA_QROWS // 2
    n_pairs = 3 * rows_per_blk // 2
    gq = half * GRID_W
    for g in range(2):
        first = jnp.clip(r0 + g * half - NA_ROWS // 2, 0, rows - NA_ROWS)
        shift = jnp.clip((first - kr0) // rows_per_blk, 0, 1)
        kw = jnp.concatenate([jnp.where(shift == 0, k_blk[i], k_blk[i + 1]) for i in range(3)], axis=0)
        vw = jnp.concatenate([jnp.where(shift == 0, v_blk[i], v_blk[i + 1]) for i in range(3)], axis=0)
        kg0 = kr0 + shift * rows_per_blk
        entry, ok = [], []
        for a in range(half):
            r = r0 + g * half + a
            start = jnp.clip(r - NA_ROWS // 2, 0, rows - NA_ROWS)
            for bp in range(n_pairs):
                rk = kg0 + 2 * bp
                entry.append(jnp.clip(rk - r + NA_ROWS, 0, 2 * NA_ROWS - 1))
                in0 = ((rk >= start) & (rk < start + NA_ROWS)).astype(jnp.int32)
                in1 = ((rk + 1 >= start) & (rk + 1 < start + NA_ROWS)).astype(jnp.int32)
                ok.append(jnp.where(lane_lo, in0, in1) != 0)

        def bias_of(hh):
            rows_ = []
            for a in range(half):
                tiles = [jnp.where(ok[a * n_pairs + bp], tab_ref[hh, entry[a * n_pairs + bp]], NEG_INF)
                         for bp in range(n_pairs)]
                rows_.append(jnp.concatenate(tiles, axis=1))
            return jnp.concatenate(rows_, axis=0)

        q = q_ref[g * gq:(g + 1) * gq, :]
        out = jnp.zeros((gq, NA_HEADS * NA_DIM), F32)
        for hh in range(NA_HEADS):
            mh = lane == hh
            qh = jnp.where(mh, q, jnp.zeros_like(q))
            s_w = _dot_nt(qh, kw) + bias_of(hh)
            s_c = _dot_nt(qh, kc)
            m = jnp.maximum(jnp.max(s_w, axis=-1, keepdims=True), jnp.max(s_c, axis=-1, keepdims=True))
            p_w = jnp.exp2(s_w - m)
            p_c = jnp.exp2(s_c - m)
            l = jnp.sum(p_w, axis=-1, keepdims=True) + jnp.sum(p_c, axis=-1, keepdims=True)
            o = _dot(p_w.astype(BF16), vw) + _dot(p_c.astype(BF16), vc)
            out = out + jnp.where(mh, o / l, 0.0)
        o_ref[g * gq:(g + 1) * gq, :] = out.astype(o_ref.dtype)


def _na(nq, nk, nv, tab, *, batch, seq, ctx):
    r = nq.shape[0]
    tq = NA_QROWS * GRID_W
    nj = seq // tq
    kb = seq // NA_KBLK
    assert kb >= 4 and seq % tq == 0 and seq % ctx == 0
    ctx_blk0 = batch * seq // ctx

    def kspec(i):
        return pl.BlockSpec((NA_KBLK, 256),
                            lambda b, j: (b * kb + jnp.clip(2 * j - 1, 0, kb - 4) + i, 0))

    cspec = pl.BlockSpec((ctx, 256), lambda b, j: (ctx_blk0 + b, 0))
    return pl.pallas_call(
        functools.partial(_na_body, rows=seq // GRID_W, key_blocks=kb),
        grid=(batch, nj),
        in_specs=[pl.BlockSpec((tq, 256), lambda b, j: (b * nj + j, 0))]
                 + [kspec(i) for i in range(4)] + [kspec(i) for i in range(4)]
                 + [cspec, cspec, _resident(tab.shape)],
        out_specs=pl.BlockSpec((tq, 256), lambda b, j: (b * nj + j, 0)),
        out_shape=jax.ShapeDtypeStruct((r, 256), BF16),
        compiler_params=_params("parallel", "arbitrary"),
        name="na",
    )(nq, nk, nk, nk, nk, nv, nv, nv, nv, nk, nv, tab)


def _diff_lambda(lq_ref, lambda_init):
    lq = lq_ref[...]
    s1 = jnp.sum(lq[0:1, :] * lq[1:2, :], axis=-1, keepdims=True)
    s2 = jnp.sum(lq[2:3, :] * lq[3:4, :], axis=-1, keepdims=True)
    return jnp.exp(s1) - jnp.exp(s2) + lambda_init


def _stack_qt(qt):
    row = lax.broadcasted_iota(jnp.int32, (2 * DIFF_DK, 1), 0)
    zero = jnp.zeros_like(qt)
    return jnp.concatenate([jnp.where(row < DIFF_DK, qt, zero), jnp.where(row >= DIFF_DK, qt, zero)], axis=1)


def _diff_finish_t(acc, tq, lam, nw_col, lambda_init):
    o = acc[0:DIFF_DV, :] / acc[DIFF_DV:DIFF_DV + 1, :]
    od = o[:, :tq] - lam * o[:, tq:]
    ms = jnp.mean(od * od, axis=0, keepdims=True)
    y = od * lax.rsqrt(ms + RMS_EPS) * nw_col * (1.0 - lambda_init)
    y = jnp.concatenate([y, jnp.zeros_like(y)], axis=0)
    return jnp.transpose(y)[:, 0:DIFF_DV]


def _diff_body(qt_ref, kc_ref, vtc_ref, k_ref, vt_ref, lq_ref, nw_ref, o_ref, s_ref, *, tq, tk, unroll, lambda_init):
    n_tiles = qt_ref.shape[1] // tq
    lam = _diff_lambda(lq_ref, lambda_init)

    def finish(acc, t):
        q0 = t * tq if isinstance(t, int) else pl.multiple_of(t * tq, tq)
        o_ref[pl.ds(q0, tq), :] = _diff_finish_t(acc, tq, lam, nw_ref[...], lambda_init).astype(o_ref.dtype)

    def tile(t, acc_prev):
        q0 = pl.multiple_of(t * tq, tq)
        return _diff_tile(qt_ref[:, pl.ds(q0, tq)], kc_ref, vtc_ref, k_ref, vt_ref, s_ref, tk=tk, unroll=unroll,
                          after_issue=lambda: finish(acc_prev, jnp.maximum(t - 1, 0)))

    placeholder = jnp.ones((vt_ref.shape[0], 2 * tq), F32)
    finish(lax.fori_loop(0, n_tiles, tile, placeholder), n_tiles - 1)


def _diff_tile(qt, kc_ref, vtc_ref, k_ref, vt_ref, s_ref, *, tk, unroll, after_issue):
    tq = qt.shape[1]
    n_chunks = k_ref.shape[0] // tk
    ring = s_ref.shape[0]
    qs = _stack_qt(qt)

    def chunk(j):
        return pl.ds(j * tk if isinstance(j, int) else pl.multiple_of(j * tk, tk), tk)

    def issue_scores(j, slot):
        s = _dot(k_ref[chunk(j), :], qs)
        s_ref[slot] = s
        return jnp.max(s, axis=0, keepdims=True)

    s_ctx = _dot(kc_ref[...], qs)
    m0 = jnp.max(s_ctx, axis=0, keepdims=True)
    cmax0 = tuple(issue_scores(min(a, n_chunks - 1), a % ring) for a in range(DIFF_AHEAD))
    after_issue()
    acc0 = _dot(vtc_ref[...], jnp.exp2(s_ctx - m0).astype(BF16))

    def group(g, carry, tail=False):
        cmax, m, acc = carry
        for u in range(unroll):
            j = g * unroll + u
            if tail and j + DIFF_AHEAD >= n_chunks:
                c_new = cmax[0]
            else:
                c_new = issue_scores(j + DIFF_AHEAD, (u + DIFF_AHEAD) % ring)
            m_new = jnp.maximum(m, cmax[0])
            p = jnp.exp2(s_ref[u % ring] - m_new).astype(BF16)
            vt = vt_ref[:, chunk(j)]
            acc = jnp.exp2(m - m_new) * acc + _dot(vt, p)
            m, cmax = m_new, cmax[1:] + (c_new,)
        return cmax, m, acc

    n_groups = n_chunks // unroll
    trips = jnp.minimum(pl.program_id(2) + 1, 1) * (n_groups - 1)
    carry = lax.fori_loop(0, trips, group, (cmax0, m0, acc0))
    _, _, acc = group(n_groups - 1, carry, tail=True)
    return acc


def _diff(dqt, dk, dvt, lq, norm_w, *, batch, seq, ctx, lambda_init):
    h, r, _ = dk.shape
    tq = DIFF_TQ
    tqs = tq * min(DIFF_QTILES, seq // tq)
    nq = seq // tqs
    tk = min(DIFF_TK, seq)
    unroll = min(DIFF_UNROLL, seq // tk)
    assert (seq // tk) % unroll == 0 and unroll % DIFF_RING == 0 and DIFF_RING > DIFF_AHEAD and seq % tqs == 0
    ctx_blk0 = batch * seq // ctx
    return pl.pallas_call(
        functools.partial(_diff_body, tq=tq, tk=tk, unroll=unroll, lambda_init=lambda_init),
        grid=(batch, h, nq),
        in_specs=[pl.BlockSpec((None, 64, tqs), lambda b, hh, i: (hh, 0, b * nq + i)),
                  pl.BlockSpec((None, ctx, 64), lambda b, hh, i: (hh, ctx_blk0 + b, 0)),
                  pl.BlockSpec((None, DIFF_VROWS, ctx), lambda b, hh, i: (hh, 0, ctx_blk0 + b)),
                  pl.BlockSpec((None, seq, 64), lambda b, hh, i: (hh, b, 0)),
                  pl.BlockSpec((None, DIFF_VROWS, seq), lambda b, hh, i: (hh, 0, b)),
                  _resident(lq.shape),
                  _resident((DIFF_DV, 1))],
        out_specs=pl.BlockSpec((None, tqs, 64), lambda b, hh, i: (hh, b * nq + i, 0)),
        out_shape=jax.ShapeDtypeStruct((h, r, 64), BF16),
        scratch_shapes=[pltpu.VMEM((DIFF_RING, tk, 2 * tq), F32)],
        compiler_params=_params("parallel", "parallel", "arbitrary"),
        name="diff",
    )(dqt, dk, dvt, dk, dvt, lq, norm_w.reshape(DIFF_DV, 1))


def _ctx_attn_body(nq_ref, nk_ref, nv_ref, dqt_ref, dk_ref, dvt_ref, lq_ref, nw_ref, na_in, df_in,
                   na_out, df_out, *, lambda_init):
    del na_in, df_in
    q = nq_ref[...]
    k = nk_ref[...]
    v = nv_ref[...]
    lane = lax.broadcasted_iota(jnp.int32, (1, NA_HEADS * NA_DIM), 1) // NA_DIM
    out = jnp.zeros(na_out.shape, F32)
    for hh in range(NA_HEADS):
        mh = lane == hh
        s = _dot_nt(jnp.where(mh, q, jnp.zeros_like(q)), k)
        p = jnp.exp2(s - jnp.max(s, axis=-1, keepdims=True))
        o = _dot(p.astype(BF16), v) / jnp.sum(p, axis=-1, keepdims=True)
        out = out + jnp.where(mh, o, 0.0)
    na_out[...] = out.astype(na_out.dtype)

    lam = _diff_lambda(lq_ref, lambda_init)
    tq = dqt_ref.shape[2]
    for hh in range(DIFF_HEADS):
        s = _dot(dk_ref[hh], _stack_qt(dqt_ref[hh]))
        p = jnp.exp2(s - jnp.max(s, axis=0, keepdims=True))
        acc = _dot(dvt_ref[hh], p.astype(BF16))
        df_out[hh] = _diff_finish_t(acc, tq, lam, nw_ref[...], lambda_init).astype(df_out.dtype)


def _ctx_attn(nq, nk, nv, dqt, dk, dvt, lq, norm_w, na_o, df_o, *, batch, seq, ctx, lambda_init):
    blk0 = batch * seq // ctx
    s256 = pl.BlockSpec((ctx, 256), lambda b: (blk0 + b, 0))
    sh64 = pl.BlockSpec((DIFF_HEADS, ctx, 64), lambda b: (0, blk0 + b, 0))
    sq_t = pl.BlockSpec((DIFF_HEADS, 64, ctx), lambda b: (0, 0, blk0 + b))
    sv_t = pl.BlockSpec((DIFF_HEADS, DIFF_VROWS, ctx), lambda b: (0, 0, blk0 + b))
    return pl.pallas_call(
        functools.partial(_ctx_attn_body, lambda_init=lambda_init),
        grid=(batch,),
        in_specs=[s256, s256, s256, sq_t, sh64, sv_t, _resident(lq.shape), _resident((DIFF_DV, 1)),
                  pl.BlockSpec(memory_space=pl.ANY), pl.BlockSpec(memory_space=pl.ANY)],
        out_specs=[s256, sh64],
        out_shape=[jax.ShapeDtypeStruct(na_o.shape, na_o.dtype), jax.ShapeDtypeStruct(df_o.shape, df_o.dtype)],
        input_output_aliases={8: 0, 9: 1},
        compiler_params=_params("arbitrary"),
        name="ctx_attn",
    )(nq, nk, nv, dqt, dk, dvt, lq, norm_w.reshape(DIFF_DV, 1), na_o, df_o)


def _conv_body(u_ref, prev_ref, next_ref, dw_ref, dwb_ref, lng_ref, lnb_ref, pw_ref, pwb_ref,
               o_ref, pad_ref, sh_ref, *, tiles_per_seq, n_latent_tiles):
    t = u_ref.shape[0]
    i = pl.program_id(0)
    in_ctx = i >= n_latent_tiles
    first = in_ctx | (i % tiles_per_seq == 0)
    last = in_ctx | (i % tiles_per_seq == tiles_per_seq - 1)
    pad_ref[0:HALO, :] = jnp.where(first, 0.0, prev_ref[...])
    pad_ref[HALO:HALO + t, :] = u_ref[...]
    pad_ref[HALO + t:HALO + t + HALO, :] = jnp.where(last, 0.0, next_ref[...])
    base = HALO - CONV_K // 2
    first = {}
    for phase in range(8):
        taps = [k for k in range(CONV_K) if (base + k) % 8 == phase]
        first[phase] = base + taps[0]
        span = base + taps[-1] + t - first[phase]
        sh_ref[phase, 0:span, :] = pad_ref[first[phase]:first[phase] + span, :]
    sub = 64
    parts = []
    for r0 in range(0, t, sub):
        acc = jnp.zeros((sub, CONV_CH), F32)
        for k in range(CONV_K):
            phase = (base + k) % 8
            off = base + k - first[phase] + r0
            acc = acc + dw_ref[k:k + 1, :] * sh_ref[phase, off:off + sub, :]
        parts.append(acc)
    y = jnp.concatenate(parts, axis=0) + dwb_ref[...]
    mu = jnp.mean(y, axis=-1, keepdims=True)
    yc = y - mu
    var = jnp.mean(yc * yc, axis=-1, keepdims=True)
    y = _silu(yc * lax.rsqrt(var + LN_EPS) * lng_ref[...] + lnb_ref[...])
    o_ref[...] = (_dot(y.astype(BF16), pw_ref[...]) + pwb_ref[...]).astype(o_ref.dtype)


def _conv(cu, dw, dw_b, ln_g, ln_b, pw, pw_b, *, batch, seq, ctx):
    r = cu.shape[0]
    t = CONV_TILE
    assert ctx == t and seq % t == 0
    n_tiles = r // t
    hb = t // HALO
    vec = lambda a: a.reshape(1, CONV_CH)
    return pl.pallas_call(
        functools.partial(_conv_body, tiles_per_seq=seq // t, n_latent_tiles=batch * seq // t),
        grid=(n_tiles,),
        in_specs=[pl.BlockSpec((t, CONV_CH), lambda i: (i, 0)),
                  pl.BlockSpec((HALO, CONV_CH), lambda i: (jnp.maximum(i * hb - 1, 0), 0)),
                  pl.BlockSpec((HALO, CONV_CH), lambda i: (jnp.minimum((i + 1) * hb, n_tiles * hb - 1), 0)),
                  _resident((CONV_K, CONV_CH))] + [_resident((1, CONV_CH))] * 3
                 + [_resident((CONV_CH, CONV_CH)), _resident((1, CONV_CH))],
        out_specs=pl.BlockSpec((t, CONV_CH), lambda i: (i, 0)),
        out_shape=jax.ShapeDtypeStruct((r, CONV_CH), BF16),
        scratch_shapes=[pltpu.VMEM((t + 2 * HALO, CONV_CH), F32),
                        pltpu.VMEM((8, t + 2 * HALO, CONV_CH), F32)],
        compiler_params=_params("parallel"),
        name="conv",
    )(cu, cu, cu, dw, vec(dw_b), vec(ln_g), vec(ln_b), pw.astype(BF16), vec(pw_b))


def kernel(x, c, ctx, c_ctx, ada_w, ada_b, norm_ffn1, ffn1_w13, ffn1_w2, norm_mix, w_in, gla_wa_f, gla_ba_f, gla_wa_b, gla_ba_b, gla_norm, na_rpb, diff_lq1, diff_lk1, diff_lq2, diff_lk2, diff_norm, conv_dw, conv_dw_b, conv_ln_g, conv_ln_b, conv_pw, conv_pw_b, w_out, norm_ffn2, ffn2_w13, ffn2_w2, final_norm):
    batch, seq, d = x.shape
    n_ctx = ctx.shape[1]
    depth = ada_w.shape[0]
    tm = ROW_TILE
    assert seq % tm == 0 and (batch * n_ctx) % tm == 0 and batch + 1 <= 8
    lat_tiles = batch * seq // tm
    all_tiles = lat_tiles + batch * n_ctx // tm
    tiles_per_batch = seq // tm

    def group_of(i):
        return jnp.minimum(i // tiles_per_batch, batch)

    def pos_of(i):
        return jnp.where(i < lat_tiles, i % tiles_per_batch, tiles_per_batch)

    c_rows = jnp.concatenate([c, c_ctx[None, :], jnp.zeros((8 - batch - 1, d), F32)], axis=0)
    mods_all = _ada(c_rows, ada_w, ada_b)[:, :batch + 1].reshape(depth, batch + 1, N_MOD, d)
    rope = _rope_tables(seq, tm)
    h = x.reshape(batch * seq, d)
    h_ctx = ctx.reshape(batch * n_ctx, d)

    for i in range(depth):
        last = i == depth - 1
        lambda_init = 0.8 - 0.6 * math.exp(-0.3 * i)
        mods = mods_all[i]
        tok = dict(n_tiles=all_tiles, group_of=group_of)
        geo = dict(batch=batch, seq=seq, ctx=n_ctx)

        h = _ffn(h, mods, norm_ffn1[i], ffn1_w13[i], ffn1_w2[i], final_norm, k0=0, final=False,
                 h_ctx=h_ctx if i == 0 else None, **tok)

        wa, ba = _gate_weights(gla_wa_f[i], gla_ba_f[i], gla_wa_b[i], gla_ba_b[i])
        (gqk, gv, gvt, gg, glg, nq, nk, nv, dq, dk, dv, cu) = _proj(
            h, mods, norm_mix[i], _permute_w_in(w_in[i]), wa, ba, rope, pos_of=pos_of, **tok)

        gnorm = jnp.tile(gla_norm[i], GLA_HEADS).reshape(1, GLA_HEADS * GLA_DV)
        o_f = _gla(gqk, gv, gvt, glg, gg, None, gnorm, reverse=False, **geo)
        gx = _gla(gqk, gv, gvt, glg, gg, o_f, gnorm, reverse=True, **geo)

        nx = _na(nq, nk, nv, _na_col_tables(na_rpb[i]), **geo)

        lq = jnp.stack([diff_lq1[i], diff_lk1[i], diff_lq2[i], diff_lk2[i]])
        dx = _diff(dq, dk, dv, lq, diff_norm[i], lambda_init=lambda_init, **geo)
        if not last:
            nx, dx = _ctx_attn(nq, nk, nv, dq, dk, dv, lq, diff_norm[i], nx, dx,
                               lambda_init=lambda_init, **geo)

        cx = _conv(cu, conv_dw[i], conv_dw_b[i], conv_ln_g[i], conv_ln_b[i], conv_pw[i], conv_pw_b[i], **geo)

        if last:
            tok = dict(n_tiles=lat_tiles, group_of=group_of)
        h = _ffn(h, mods, norm_ffn2[i], ffn2_w13[i], ffn2_w2[i], final_norm, k0=6, final=last,
                 mixers=(gx, nx, dx, cx, w_out[i], (batch, seq, n_ctx)), **tok)

    return h.reshape(batch, seq, d)
```

```python
import functools
import math

import jax
import jax.numpy as jnp
from jax import lax
from jax.experimental import pallas as pl
from jax.experimental.pallas import tpu as pltpu

F32 = jnp.float32
BF16 = jnp.bfloat16

GRID_W = 64
N_MOD = 9
RMS_EPS = 1e-6
LN_EPS = 1e-5
NEG_INF = -1e30
GLA_HEADS, GLA_DK, GLA_DV, GLA_RANK, GLA_TAU, GLA_CHUNK = 4, 32, 64, 16, 16.0, 64
NA_HEADS, NA_DIM, NA_ROWS, NA_COLS = 4, 64, 8, 16
DIFF_HEADS, DIFF_DK, DIFF_DV = 4, 32, 64
DIFF_VROWS = 80
CONV_CH, CONV_K = 256, 31
ROPE_BASE = 10000.0
LOG2E = 1.4426950408889634

LANES = 128
VMEM_LIMIT = 56 * 1024 * 1024

ROW_TILE = 512
FF_CHUNK = 256
GLA_BLOCK = 256
NA_QROWS = 8
NA_KBLK = 256
CONV_TILE = 256
DIFF_TQ = 256
DIFF_QTILES = 8
DIFF_TK = 256
DIFF_UNROLL = 32
DIFF_AHEAD = 6
DIFF_RING = 8
HALO = 16

C_GLA, C_NA, C_DIFF, C_CONV, C_AUX, C_END = 0, 768, 1536, 2304, 2816, 2944


def _dot(a, b):
    return jnp.dot(a, b, preferred_element_type=F32)


def _dot_nt(a, b):
    return lax.dot_general(a, b, (((1,), (1,)), ((), ())), preferred_element_type=F32)


def _params(*sem):
    return pltpu.CompilerParams(dimension_semantics=sem, vmem_limit_bytes=VMEM_LIMIT)


def _resident(shape):
    nd = len(shape)
    return pl.BlockSpec(shape, lambda *_: (0,) * nd, pipeline_mode=pl.Buffered(1))


def _silu(x):
    return x * jax.nn.sigmoid(x)


def _rms(x, w):
    return x * lax.rsqrt(jnp.mean(x * x, axis=-1, keepdims=True) + RMS_EPS) * w


def _ada_body(c_ref, w_ref, b_ref, o_ref):
    s = _silu(c_ref[...])
    o_ref[...] = jnp.dot(s, w_ref[...], precision=lax.Precision.HIGHEST,
                         preferred_element_type=F32) + b_ref[...]


def _ada(c_rows, ada_w, ada_b):
    depth, d, _ = ada_w.shape
    return pl.pallas_call(
        _ada_body,
        grid=(depth, N_MOD),
        in_specs=[pl.BlockSpec((8, d), lambda l, n: (0, 0)),
                  pl.BlockSpec((None, d, d), lambda l, n: (l, 0, n)),
                  pl.BlockSpec((None, 1, d), lambda l, n: (l, 0, n))],
        out_specs=pl.BlockSpec((None, 8, d), lambda l, n: (l, 0, n)),
        out_shape=jax.ShapeDtypeStruct((depth, 8, N_MOD * d), F32),
        compiler_params=_params("arbitrary", "arbitrary"),
        name="ada",
    )(c_rows, ada_w, ada_b.reshape(depth, 1, N_MOD * d))


def _ffn_body(*refs, k0, final, source, n_first):
    h_ref = refs[0]
    n_extra = {"plain": 0, "two_arrays": 1, "mixers": 6}[source]
    extra = refs[1:1 + n_extra]
    mod_ref, nw_ref, w13_ref, w2_ref, fw_ref, o_ref, xb_ref, g_ref, h13_ref = refs[1 + n_extra:]
    ff = w2_ref.shape[0]
    tf = h13_ref.shape[2] // 2
    n_chunks = ff // tf
    x = h_ref[...]
    if source == "two_arrays":
        x = jnp.where(pl.program_id(0) < n_first, x, extra[0][...])
    elif source == "mixers":
        gx_ref, gxc_ref, nx_ref, dx_ref, cx_ref, wo_ref = extra
        gx = jnp.where(pl.program_id(0) < n_first, gx_ref[...], gxc_ref[...].reshape(gx_ref.shape))
        mix = jnp.concatenate([gx, nx_ref[...]] + [dx_ref[hh] for hh in range(DIFF_HEADS)]
                              + [cx_ref[...]], axis=-1)
        x = x + mod_ref[5:6, :] * _dot(mix, wo_ref[...])
    xm = _rms(x, nw_ref[...]) * (1.0 + mod_ref[k0 + 1:k0 + 2, :]) + mod_ref[k0:k0 + 1, :]
    xb_ref[...] = xm.astype(BF16)

    def cols(c):
        return c * tf if isinstance(c, int) else pl.multiple_of(c * tf, tf)

    def up(c, slot):
        h13_ref[slot, :, :tf] = _dot(xb_ref[...], w13_ref[:, pl.ds(cols(c), tf)])
        h13_ref[slot, :, tf:] = _dot(xb_ref[...], w13_ref[:, pl.ds(ff + cols(c), tf)])

    def gate(c, slot):
        a = h13_ref[slot, :, :tf]
        u = h13_ref[slot, :, tf:]
        g_ref[:, pl.ds(cols(c), tf)] = (_silu(a) * u).astype(BF16)

    def pair(t, carry):
        up(2 * t + 1, 1)
        gate(2 * t, 0)
        up(2 * t + 2, 0)
        gate(2 * t + 1, 1)
        return carry

    up(0, 0)
    for t in range((n_chunks - 1) // 2):
        pair(t, 0)
    gate(n_chunks - 1, 0)
    out = x + (0.5 * mod_ref[k0 + 2:k0 + 3, :]) * _dot(g_ref[...], w2_ref[...])
    if final:
        out = _rms(out, fw_ref[...])
    o_ref[...] = out


def _ffn(h, mods, norm_w, w13, w2, final_w, *, k0, n_tiles, group_of, final, h_ctx=None, mixers=None):
    d = h.shape[1]
    tm = ROW_TILE
    ff = w2.shape[0]
    n_chunks = ff // FF_CHUNK
    assert ff % FF_CHUNK == 0 and n_chunks % 2 == 1
    row = lambda i: (i, 0)
    source, n_first, extra, extra_specs, h_spec = "plain", None, [], [], pl.BlockSpec((tm, d), row)
    if h_ctx is not None:
        source, n_first, extra = "two_arrays", h.shape[0] // tm, [h_ctx]
        h_spec = pl.BlockSpec((tm, d), lambda i: (jnp.minimum(i, n_first - 1), 0))
        extra_specs = [pl.BlockSpec((tm, d), lambda i: (jnp.maximum(i - n_first, 0), 0))]
    elif mixers is not None:
        gx, nx, dx, cx, w_out, (batch, seq, n_ctx) = mixers
        assert batch * n_ctx == tm and seq % tm == 0 and seq % n_ctx == 0
        source, n_first, extra = "mixers", batch * seq // tm, [gx, gx, nx, dx, cx, w_out.astype(BF16)]
        per_seq = seq // tm
        extra_specs = [pl.BlockSpec((None, tm, 256),
                                    lambda i: (jnp.minimum(i // per_seq, batch - 1), i % per_seq, 0)),
                       pl.BlockSpec((batch, n_ctx, 256), lambda i: (0, seq // n_ctx, 0)),
                       pl.BlockSpec((tm, 256), row),
                       pl.BlockSpec((DIFF_HEADS, tm, 64), lambda i: (0, i, 0)),
                       pl.BlockSpec((tm, 256), row), _resident(w_out.shape)]
    return pl.pallas_call(
        functools.partial(_ffn_body, k0=k0, final=final, source=source, n_first=n_first),
        grid=(n_tiles,),
        in_specs=[h_spec] + extra_specs + [
            pl.BlockSpec((None, N_MOD, d), lambda i: (group_of(i), 0, 0)),
            _resident((1, d)),
            _resident(w13.shape),
            _resident(w2.shape),
            _resident((1, d))],
        out_specs=pl.BlockSpec((tm, d), row),
        out_shape=jax.ShapeDtypeStruct((n_tiles * tm, d), F32),
        scratch_shapes=[pltpu.VMEM((tm, d), BF16), pltpu.VMEM((tm, ff), BF16),
                        pltpu.VMEM((2, tm, 2 * FF_CHUNK), F32)],
        compiler_params=_params("parallel"),
        name="ffn",
    )(h, *extra, mods, norm_w.reshape(1, d), w13.astype(BF16), w2.astype(BF16), final_w.reshape(1, d))


def _log_sigmoid(x):
    return jnp.minimum(x, 0.0) - jnp.log1p(jnp.exp(-jnp.abs(x)))


def _rope_rotate(x):
    n = x.shape[-1]
    lane = lax.broadcasted_iota(jnp.int32, x.shape, 1)
    up = pltpu.roll(x, n - 8, 1)
    dn = pltpu.roll(x, 8, 1)
    return jnp.where((lane & 15) < 8, -up, dn)


def _rope_rotate_rows(x):
    n = x.shape[0]
    row = lax.broadcasted_iota(jnp.int32, x.shape, 0)
    up = pltpu.roll(x, n - 8, 0)
    dn = pltpu.roll(x, 8, 0)
    return jnp.where((row & 15) < 8, -up, dn)


def _proj_body(h_ref, mod_ref, nw_ref, w_ref, wt_ref, wa_ref, ba_ref,
               cos_ref, sin_ref, cost_ref, sint_ref,
               gqk_ref, gv_ref, gvt_ref, gg_ref, glg_ref, nq_ref, nk_ref, nv_ref,
               dqt_ref, dk_ref, dvt_ref, cu_ref, xb_ref):
    x = h_ref[...]
    tm = x.shape[0]
    xm = _rms(x, nw_ref[...]) * (1.0 + mod_ref[4:5, :]) + mod_ref[3:4, :]
    xb_ref[...] = xm.astype(BF16)

    z = _dot(xb_ref[...], w_ref[:, C_GLA:C_GLA + 256])
    lane = lax.broadcasted_iota(jnp.int32, (1, 256), 1)
    gqk_ref[...] = z * jnp.where(lane < 128, GLA_DK ** -0.5, 1.0)
    gv_ref[...] = _dot(xb_ref[...], w_ref[:, C_GLA + 256:C_GLA + 512]).astype(BF16)
    zt = _dot_nt(wt_ref[...], xb_ref[...])
    gvt_ref[...] = zt[0:256, :].astype(BF16)
    gg_ref[...] = _dot(xb_ref[...], w_ref[:, C_GLA + 512:C_GLA + 768]).astype(BF16)
    aux = _dot(xb_ref[...], w_ref[:, C_AUX:C_END])
    pre = _dot(aux.astype(BF16), wa_ref[...]) + ba_ref[...]
    glg_ref[...] = _log_sigmoid(pre) * (1.0 / GLA_TAU)

    nq_ref[...] = (_dot(xb_ref[...], w_ref[:, C_NA:C_NA + 256]) * (NA_DIM ** -0.5 * LOG2E)).astype(BF16)
    nk_ref[...] = _dot(xb_ref[...], w_ref[:, C_NA + 256:C_NA + 512]).astype(BF16)
    nv_ref[...] = _dot(xb_ref[...], w_ref[:, C_NA + 512:C_NA + 768]).astype(BF16)

    cos = cos_ref[...]
    sin = sin_ref[...]
    cos2 = jnp.concatenate([cos, cos], axis=1)
    sin2 = jnp.concatenate([sin, sin], axis=1)
    zk = _dot(xb_ref[...], w_ref[:, C_DIFF + 256:C_DIFF + 512])
    zk = zk * cos2 + _rope_rotate(zk) * sin2
    n_rep = 2 * DIFF_HEADS
    cos_t = jnp.concatenate([cost_ref[...]] * n_rep, axis=0)
    sin_t = jnp.concatenate([sint_ref[...]] * n_rep, axis=0)
    zqt = zt[256:512, :]
    zqt = (zqt * cos_t + _rope_rotate_rows(zqt) * sin_t) * (DIFF_DK ** -0.5 * LOG2E)
    zvt = zt[512:768, :]
    pad_rows = DIFF_VROWS - DIFF_DV
    one_row = jnp.where(lax.broadcasted_iota(jnp.int32, (pad_rows, tm), 0) == 0, 1.0, 0.0).astype(BF16)
    for hh in range(DIFF_HEADS):
        sl = slice(64 * hh, 64 * hh + 64)
        dk_ref[hh] = zk[:, sl].astype(BF16)
        dqt_ref[hh] = zqt[sl, :].astype(BF16)
        dvt_ref[hh, 0:64, :] = zvt[sl, :].astype(BF16)
        dvt_ref[hh, DIFF_DV:DIFF_VROWS, :] = one_row

    za = _dot(xb_ref[...], w_ref[:, C_CONV:C_CONV + 256])
    zg = _dot(xb_ref[...], w_ref[:, C_CONV + 256:C_CONV + 512])
    cu_ref[...] = za * jax.nn.sigmoid(zg)


def _proj(h, mods, norm_w, w_p, wa, ba, rope, *, n_tiles, group_of, pos_of):
    r, d = h.shape
    tm = ROW_TILE
    row = lambda i: (i, 0)
    hrow = lambda i: (0, i, 0)
    hcol = lambda i: (0, 0, i)
    f32_256 = jax.ShapeDtypeStruct((r, 256), F32)
    bf_256 = jax.ShapeDtypeStruct((r, 256), BF16)
    w_t = jnp.concatenate([w_p[:, C_GLA + 256:C_GLA + 512], w_p[:, C_DIFF:C_DIFF + 256],
                           w_p[:, C_DIFF + 512:C_DIFF + 768]], axis=1).T
    cos_r, sin_r, cos_c, sin_c = rope
    out_shape = [f32_256, bf_256, jax.ShapeDtypeStruct((256, r), BF16), bf_256, f32_256,
                 bf_256, bf_256, bf_256,
                 jax.ShapeDtypeStruct((DIFF_HEADS, 64, r), BF16),
                 jax.ShapeDtypeStruct((DIFF_HEADS, r, 64), BF16),
                 jax.ShapeDtypeStruct((DIFF_HEADS, DIFF_VROWS, r), BF16),
                 f32_256]
    out_specs = [pl.BlockSpec((tm, 256), row)] * 2 + [pl.BlockSpec((256, tm), lambda i: (0, i))] + [
        pl.BlockSpec((tm, 256), row)] * 5 + [
        pl.BlockSpec((DIFF_HEADS, 64, tm), hcol),
        pl.BlockSpec((DIFF_HEADS, tm, 64), hrow),
        pl.BlockSpec((DIFF_HEADS, DIFF_VROWS, tm), hcol),
        pl.BlockSpec((tm, 256), row)]
    return pl.pallas_call(
        _proj_body,
        grid=(n_tiles,),
        in_specs=[pl.BlockSpec((tm, d), row),
                  pl.BlockSpec((None, N_MOD, d), lambda i: (group_of(i), 0, 0)),
                  _resident((1, d)),
                  _resident(w_p.shape),
                  _resident(w_t.shape),
                  _resident(wa.shape),
                  _resident(ba.shape),
                  pl.BlockSpec((tm, LANES), lambda i: (pos_of(i), 0)),
                  pl.BlockSpec((tm, LANES), lambda i: (pos_of(i), 0)),
                  pl.BlockSpec((DIFF_DK, tm), lambda i: (0, pos_of(i))),
                  pl.BlockSpec((DIFF_DK, tm), lambda i: (0, pos_of(i)))],
        out_specs=out_specs,
        out_shape=out_shape,
        scratch_shapes=[pltpu.VMEM((tm, d), BF16)],
        compiler_params=_params("parallel"),
        name="proj",
    )(h, mods, norm_w.reshape(1, d), w_p, w_t, wa, ba, cos_r, sin_r, cos_c, sin_c)


def _permute_w_in(w_in):
    d = w_in.shape[0]
    g0 = 2 * GLA_HEADS * GLA_DK + 2 * GLA_HEADS * GLA_DV
    aux = w_in[:, g0:g0 + 2 * GLA_RANK]
    rest = w_in[:, g0 + 2 * GLA_RANK:]
    pad = jnp.zeros((d, C_END - C_AUX - 2 * GLA_RANK), w_in.dtype)
    return jnp.concatenate([w_in[:, :g0], rest, aux, pad], axis=1).astype(BF16)


def _gate_weights(wa_f, ba_f, wa_b, ba_b):
    n = GLA_HEADS * GLA_DK
    wa = jnp.zeros((C_END - C_AUX, 2 * n), F32)
    wa = wa.at[:GLA_RANK, :n].set(wa_f).at[GLA_RANK:2 * GLA_RANK, n:].set(wa_b)
    return wa.astype(BF16), jnp.concatenate([ba_f, ba_b]).reshape(1, 2 * n)


def _rope_tables(seq, tile):
    t = jnp.arange(seq)
    row = (t // GRID_W).astype(F32)
    col = (t % GRID_W).astype(F32)
    half = DIFF_DK // 2
    inv = 1.0 / (ROPE_BASE ** (jnp.arange(0, half, 2, dtype=F32) / half))
    ang_r = row[:, None] * inv
    ang_c = col[:, None] * inv
    ang = jnp.concatenate([ang_r, ang_r, ang_c, ang_c], axis=-1)
    cos = jnp.concatenate([jnp.cos(ang), jnp.ones((tile, DIFF_DK), F32)], axis=0)
    sin = jnp.concatenate([jnp.sin(ang), jnp.zeros((tile, DIFF_DK), F32)], axis=0)
    rep = (1, LANES // DIFF_DK)
    return jnp.tile(cos, rep), jnp.tile(sin, rep), cos.T, sin.T


def _split3(x):
    hi = x.astype(BF16)
    r1 = x - hi.astype(F32)
    mid = r1.astype(BF16)
    lo = (r1 - mid.astype(F32)).astype(BF16)
    return hi, mid, lo


def _gla_body(*refs, reverse, batch):
    per_b = [refs[5 * b:5 * b + 5] for b in range(batch)]
    of_ref, nw_ref, o_ref, st_ref = refs[5 * batch:]
    blk = per_b[0][0].shape[0]
    n_chunks = blk // GLA_CHUNK
    nqk = GLA_HEADS * GLA_DK
    nv = GLA_HEADS * GLA_DV

    @pl.when(pl.program_id(0) == 0)
    def _():
        st_ref[...] = jnp.zeros_like(st_ref)

    ri = lax.broadcasted_iota(jnp.int32, (blk, blk), 0)
    ci = lax.broadcasted_iota(jnp.int32, (blk, blk), 1)
    same = (ri // GLA_CHUNK) == (ci // GLA_CHUNK)
    tri = jnp.where(same & ((ci >= ri) if reverse else (ci <= ri)), 1.0, 0.0).astype(BF16)
    bcum_of = []
    for qk_ref, v_ref, vt_ref, lg_ref, g_ref in per_b:
        lg = lg_ref[:, nqk:2 * nqk] if reverse else lg_ref[:, 0:nqk]
        hi, mid, lo = _split3(lg)
        bcum_of.append(_dot(tri, hi) + _dot(tri, mid) + _dot(tri, lo))

    c = GLA_CHUNK
    rk = lax.broadcasted_iota(jnp.int32, (GLA_HEADS * c, nqk), 0) // c
    ck = lax.broadcasted_iota(jnp.int32, (GLA_HEADS * c, nqk), 1) // GLA_DK
    mask_k = rk == ck
    rv = lax.broadcasted_iota(jnp.int32, (GLA_HEADS * c, nv), 0) // c
    cv = lax.broadcasted_iota(jnp.int32, (GLA_HEADS * c, nv), 1) // GLA_DV
    mask_v = rv == cv
    rs = lax.broadcasted_iota(jnp.int32, (nv, nqk), 0) // GLA_DV
    cs = lax.broadcasted_iota(jnp.int32, (nv, nqk), 1) // GLA_DK
    mask_s = rs == cs
    ai = lax.broadcasted_iota(jnp.int32, (c, GLA_HEADS * c), 0)
    aj = lax.broadcasted_iota(jnp.int32, (c, GLA_HEADS * c), 1) % c
    mask_a = (aj >= ai) if reverse else (aj <= ai)

    order = range(n_chunks - 1, -1, -1) if reverse else range(n_chunks)
    units = [(b, ch) for ch in order for b in range(batch)]
    q_in, a_raw, u_t, decay = {}, {}, {}, {}
    vts = [refs_b[2][...].astype(BF16) for refs_b in per_b]
    for b, ch in units:
        qk_ref = per_b[b][0]
        rows = slice(ch * c, (ch + 1) * c)
        k = qk_ref[rows, nqk:2 * nqk]
        bcum = bcum_of[b][rows, :]
        btot = bcum[0:1, :] if reverse else bcum[c - 1:c, :]
        q_in[b, ch] = (qk_ref[rows, 0:nqk] * jnp.exp(bcum)).astype(BF16)
        k_in = k * jnp.exp(-bcum)
        k_out = (k * jnp.exp(btot - bcum)).astype(BF16)
        k_bd = jnp.where(mask_k, jnp.concatenate([k_in] * GLA_HEADS, axis=0), 0.0).astype(BF16)
        a_raw[b, ch] = _dot_nt(q_in[b, ch], k_bd)
        k_pad = jnp.concatenate([jnp.zeros((n * c, nqk), BF16) for n in (ch,) if n] + [k_out]
                                + [jnp.zeros((n * c, nqk), BF16) for n in (n_chunks - 1 - ch,) if n], axis=0)
        u_t[b, ch] = jnp.where(mask_s, _dot(vts[b], k_pad), 0.0)
        decay[b, ch] = jnp.exp(btot)
    o = {}
    for b, ch in units:
        v = per_b[b][1][ch * c:(ch + 1) * c, :]
        v4 = jnp.concatenate([v] * GLA_HEADS, axis=0)
        v_bd = jnp.where(mask_v, v4, jnp.zeros_like(v4))
        o[b, ch] = _dot(jnp.where(mask_a, a_raw[b, ch], 0.0).astype(BF16), v_bd)
    st = [st_ref[b] for b in range(batch)]
    for b, ch in units:
        o[b, ch] = o[b, ch] + _dot_nt(q_in[b, ch], st[b].astype(BF16))
        st[b] = st[b] * decay[b, ch] + u_t[b, ch]
    for b in range(batch):
        st_ref[b] = st[b]
        o_blk = jnp.concatenate([o[b, ch] for ch in range(n_chunks)], axis=0)
        if reverse:
            o_blk = o_blk + of_ref[b]
            hi2, lo2, _ = _split3(o_blk * o_blk)
            hr = lax.broadcasted_iota(jnp.int32, (nv, nv), 0) // GLA_DV
            hc = lax.broadcasted_iota(jnp.int32, (nv, nv), 1) // GLA_DV
            seg = jnp.where(hr == hc, 1.0, 0.0).astype(BF16)
            ms = (_dot(hi2, seg) + _dot(lo2, seg)) * (1.0 / GLA_DV)
            o_blk = o_blk * lax.rsqrt(ms + RMS_EPS) * nw_ref[...] * _silu(per_b[b][4][...].astype(F32))
        o_ref[b] = o_blk.astype(o_ref.dtype)


def _gla(gqk, gv, gvt, glg, gg, o_f, norm_w4, *, reverse, batch, seq, ctx):
    blk = GLA_BLOCK
    nc, nl = ctx // blk, seq // blk
    ctx_base = batch * seq // blk

    def step_blk(s):
        if reverse:
            return jnp.where(s < nc, nl + (nc - 1 - s), nl - 1 - (s - nc))
        return jnp.where(s < nc, nl + s, s - nc)

    def row_blk(b):
        return lambda s: jnp.where(s < nc, ctx_base + b * nc - nl, b * nl) + step_blk(s)

    specs, args = [], []
    for b in range(batch):
        spec = pl.BlockSpec((blk, 256), lambda s, f=row_blk(b): (f(s), 0))
        spec_t = pl.BlockSpec((256, blk), lambda s, f=row_blk(b): (0, f(s)))
        if reverse:
            specs += [spec, spec, spec_t, spec, spec]
            args += [gqk, gv, gvt, glg, gg]
        else:
            specs += [spec, spec, spec_t, spec, _resident((1, 256))]
            args += [gqk, gv, gvt, glg, norm_w4]
    seq_spec = pl.BlockSpec((batch, blk, 256), lambda s: (0, step_blk(s), 0))
    if o_f is None:
        o_f, of_spec = norm_w4, _resident((1, 256))
    else:
        of_spec = seq_spec
    return pl.pallas_call(
        functools.partial(_gla_body, reverse=reverse, batch=batch),
        grid=(nc + nl,),
        in_specs=specs + [of_spec, _resident((1, 256))],
        out_specs=seq_spec,
        out_shape=jax.ShapeDtypeStruct((batch, seq + ctx, 256), BF16 if reverse else F32),
        scratch_shapes=[pltpu.VMEM((batch, GLA_HEADS * GLA_DV, GLA_HEADS * GLA_DK), F32)],
        compiler_params=_params("arbitrary"),
        name="gla_bwd" if reverse else "gla_fwd",
    )(*args, o_f, norm_w4)


def _na_col_tables(rpb):
    col = jnp.arange(GRID_W)
    cstart = jnp.clip(col - NA_COLS // 2, 0, GRID_W - NA_COLS)
    col_ok = (col[None, :] >= cstart[:, None]) & (col[None, :] < cstart[:, None] + NA_COLS)
    col_off = jnp.clip(col[None, :] - col[:, None] + (NA_COLS - 1), 0, 2 * NA_COLS - 2)
    t = jnp.where(col_ok[None, None], rpb[:, :, col_off] * LOG2E, NEG_INF)
    dead = jnp.full((NA_HEADS, 1, GRID_W, GRID_W), NEG_INF, F32)
    t = jnp.concatenate([dead, t.astype(F32), dead], axis=1)
    return jnp.concatenate([t[:, :-1], t[:, 1:]], axis=-1)


def _na_body(q_ref, k0, k1, k2, k3, v0, v1, v2, v3, kc_ref, vc_ref, tab_ref, o_ref, *, rows, key_blocks):
    k_blk = [k0[...], k1[...], k2[...], k3[...]]
    v_blk = [v0[...], v1[...], v2[...], v3[...]]
    kc = kc_ref[...]
    vc = vc_ref[...]

    j = pl.program_id(1)
    rows_per_blk = NA_KBLK // GRID_W
    r0 = j * NA_QROWS
    kr0 = jnp.clip(2 * j - 1, 0, key_blocks - 4) * rows_per_blk
    lane_lo = lax.broadcasted_iota(jnp.int32, (1, 2 * GRID_W), 1) < GRID_W
    lane = lax.broadcasted_iota(jnp.int32, (1, NA_HEADS * NA_DIM), 1) // NA_DIM
    half = NA_QROWS // 2
    n_pairs = 3 * rows_per_blk // 2
    gq = half * GRID_W
    for g in range(2):
        first = jnp.clip(r0 + g * half - NA_ROWS // 2, 0, rows - NA_ROWS)
        shift = jnp.clip((first - kr0) // rows_per_blk, 0, 1)
        kw = jnp.concatenate([jnp.where(shift == 0, k_blk[i], k_blk[i + 1]) for i in range(3)], axis=0)
        vw = jnp.concatenate([jnp.where(shift == 0, v_blk[i], v_blk[i + 1]) for i in range(3)], axis=0)
        kg0 = kr0 + shift * rows_per_blk
        entry, ok = [], []
        for a in range(half):
            r = r0 + g * half + a
            start = jnp.clip(r - NA_ROWS // 2, 0, rows - NA_ROWS)
            for bp in range(n_pairs):
                rk = kg0 + 2 * bp
                entry.append(jnp.clip(rk - r + NA_ROWS, 0, 2 * NA_ROWS - 1))
                in0 = ((rk >= start) & (rk < start + NA_ROWS)).astype(jnp.int32)
                in1 = ((rk + 1 >= start) & (rk + 1 < start + NA_ROWS)).astype(jnp.int32)
                ok.append(jnp.where(lane_lo, in0, in1) != 0)

        def bias_of(hh):
            rows_ = []
            for a in range(half):
                tiles = [jnp.where(ok[a * n_pairs + bp], tab_ref[hh, entry[a * n_pairs + bp]], NEG_INF)
                         for bp in range(n_pairs)]
                rows_.append(jnp.concatenate(tiles, axis=1))
            return jnp.concatenate(rows_, axis=0)

        q = q_ref[g * gq:(g + 1) * gq, :]
        out = jnp.zeros((gq, NA_HEADS * NA_DIM), F32)
        for hh in range(NA_HEADS):
            mh = lane == hh
            qh = jnp.where(mh, q, jnp.zeros_like(q))
            s_w = _dot_nt(qh, kw) + bias_of(hh)
            s_c = _dot_nt(qh, kc)
            m = jnp.maximum(jnp.max(s_w, axis=-1, keepdims=True), jnp.max(s_c, axis=-1, keepdims=True))
            p_w = jnp.exp2(s_w - m)
            p_c = jnp.exp2(s_c - m)
            l = jnp.sum(p_w, axis=-1, keepdims=True) + jnp.sum(p_c, axis=-1, keepdims=True)
            o = _dot(p_w.astype(BF16), vw) + _dot(p_c.astype(BF16), vc)
            out = out + jnp.where(mh, o / l, 0.0)
        o_ref[g * gq:(g + 1) * gq, :] = out.astype(o_ref.dtype)


def _na(nq, nk, nv, tab, *, batch, seq, ctx):
    r = nq.shape[0]
    tq = NA_QROWS * GRID_W
    nj = seq // tq
    kb = seq // NA_KBLK
    assert kb >= 4 and seq % tq == 0 and seq % ctx == 0
    ctx_blk0 = batch * seq // ctx

    def kspec(i):
        return pl.BlockSpec((NA_KBLK, 256),
                            lambda b, j: (b * kb + jnp.clip(2 * j - 1, 0, kb - 4) + i, 0))

    cspec = pl.BlockSpec((ctx, 256), lambda b, j: (ctx_blk0 + b, 0))
    return pl.pallas_call(
        functools.partial(_na_body, rows=seq // GRID_W, key_blocks=kb),
        grid=(batch, nj),
        in_specs=[pl.BlockSpec((tq, 256), lambda b, j: (b * nj + j, 0))]
                 + [kspec(i) for i in range(4)] + [kspec(i) for i in range(4)]
                 + [cspec, cspec, _resident(tab.shape)],
        out_specs=pl.BlockSpec((tq, 256), lambda b, j: (b * nj + j, 0)),
        out_shape=jax.ShapeDtypeStruct((r, 256), BF16),
        compiler_params=_params("parallel", "arbitrary"),
        name="na",
    )(nq, nk, nk, nk, nk, nv, nv, nv, nv, nk, nv, tab)


def _diff_lambda(lq_ref, lambda_init):
    lq = lq_ref[...]
    s1 = jnp.sum(lq[0:1, :] * lq[1:2, :], axis=-1, keepdims=True)
    s2 = jnp.sum(lq[2:3, :] * lq[3:4, :], axis=-1, keepdims=True)
    return jnp.exp(s1) - jnp.exp(s2) + lambda_init


def _stack_qt(qt):
    row = lax.broadcasted_iota(jnp.int32, (2 * DIFF_DK, 1), 0)
    zero = jnp.zeros_like(qt)
    return jnp.concatenate([jnp.where(row < DIFF_DK, qt, zero), jnp.where(row >= DIFF_DK, qt, zero)], axis=1)


def _diff_finish_t(acc, tq, lam, nw_col, lambda_init):
    o = acc[0:DIFF_DV, :] / acc[DIFF_DV:DIFF_DV + 1, :]
    od = o[:, :tq] - lam * o[:, tq:]
    ms = jnp.mean(od * od, axis=0, keepdims=True)
    y = od * lax.rsqrt(ms + RMS_EPS) * nw_col * (1.0 - lambda_init)
    y = jnp.concatenate([y, jnp.zeros_like(y)], axis=0)
    return jnp.transpose(y)[:, 0:DIFF_DV]


def _diff_body(qt_ref, kc_ref, vtc_ref, k_ref, vt_ref, lq_ref, nw_ref, o_ref, s_ref, *, tq, tk, unroll, lambda_init):
    n_tiles = qt_ref.shape[1] // tq
    lam = _diff_lambda(lq_ref, lambda_init)

    def finish(acc, t):
        q0 = t * tq if isinstance(t, int) else pl.multiple_of(t * tq, tq)
        o_ref[pl.ds(q0, tq), :] = _diff_finish_t(acc, tq, lam, nw_ref[...], lambda_init).astype(o_ref.dtype)

    def tile(t, acc_prev):
        q0 = pl.multiple_of(t * tq, tq)
        return _diff_tile(qt_ref[:, pl.ds(q0, tq)], kc_ref, vtc_ref, k_ref, vt_ref, s_ref, tk=tk, unroll=unroll,
                          after_issue=lambda: finish(acc_prev, jnp.maximum(t - 1, 0)))

    placeholder = jnp.ones((vt_ref.shape[0], 2 * tq), F32)
    finish(lax.fori_loop(0, n_tiles, tile, placeholder), n_tiles - 1)


def _diff_tile(qt, kc_ref, vtc_ref, k_ref, vt_ref, s_ref, *, tk, unroll, after_issue):
    tq = qt.shape[1]
    n_chunks = k_ref.shape[0] // tk
    ring = s_ref.shape[0]
    qs = _stack_qt(qt)

    def chunk(j):
        return pl.ds(j * tk if isinstance(j, int) else pl.multiple_of(j * tk, tk), tk)

    def issue_scores(j, slot):
        s = _dot(k_ref[chunk(j), :], qs)
        s_ref[slot] = s
        return jnp.max(s, axis=0, keepdims=True)

    s_ctx = _dot(kc_ref[...], qs)
    m0 = jnp.max(s_ctx, axis=0, keepdims=True)
    cmax0 = tuple(issue_scores(min(a, n_chunks - 1), a % ring) for a in range(DIFF_AHEAD))
    after_issue()
    acc0 = _dot(vtc_ref[...], jnp.exp2(s_ctx - m0).astype(BF16))

    def group(g, carry, tail=False):
        cmax, m, acc = carry
        for u in range(unroll):
            j = g * unroll + u
            if tail and j + DIFF_AHEAD >= n_chunks:
                c_new = cmax[0]
            else:
                c_new = issue_scores(j + DIFF_AHEAD, (u + DIFF_AHEAD) % ring)
            m_new = jnp.maximum(m, cmax[0])
            p = jnp.exp2(s_ref[u % ring] - m_new).astype(BF16)
            vt = vt_ref[:, chunk(j)]
            acc = jnp.exp2(m - m_new) * acc + _dot(vt, p)
            m, cmax = m_new, cmax[1:] + (c_new,)
        return cmax, m, acc

    n_groups = n_chunks // unroll
    trips = jnp.minimum(pl.program_id(2) + 1, 1) * (n_groups - 1)
    carry = lax.fori_loop(0, trips, group, (cmax0, m0, acc0))
    _, _, acc = group(n_groups - 1, carry, tail=True)
    return acc


def _diff(dqt, dk, dvt, lq, norm_w, *, batch, seq, ctx, lambda_init):
    h, r, _ = dk.shape
    tq = DIFF_TQ
    tqs = tq * min(DIFF_QTILES, seq // tq)
    nq = seq // tqs
    tk = min(DIFF_TK, seq)
    unroll = min(DIFF_UNROLL, seq // tk)
    assert (seq // tk) % unroll == 0 and unroll % DIFF_RING == 0 and DIFF_RING > DIFF_AHEAD and seq % tqs == 0
    ctx_blk0 = batch * seq // ctx
    return pl.pallas_call(
        functools.partial(_diff_body, tq=tq, tk=tk, unroll=unroll, lambda_init=lambda_init),
        grid=(batch, h, nq),
        in_specs=[pl.BlockSpec((None, 64, tqs), lambda b, hh, i: (hh, 0, b * nq + i)),
                  pl.BlockSpec((None, ctx, 64), lambda b, hh, i: (hh, ctx_blk0 + b, 0)),
                  pl.BlockSpec((None, DIFF_VROWS, ctx), lambda b, hh, i: (hh, 0, ctx_blk0 + b)),
                  pl.BlockSpec((None, seq, 64), lambda b, hh, i: (hh, b, 0)),
                  pl.BlockSpec((None, DIFF_VROWS, seq), lambda b, hh, i: (hh, 0, b)),
                  _resident(lq.shape),
                  _resident((DIFF_DV, 1))],
        out_specs=pl.BlockSpec((None, tqs, 64), lambda b, hh, i: (hh, b * nq + i, 0)),
        out_shape=jax.ShapeDtypeStruct((h, r, 64), BF16),
        scratch_shapes=[pltpu.VMEM((DIFF_RING, tk, 2 * tq), F32)],
        compiler_params=_params("parallel", "parallel", "arbitrary"),
        name="diff",
    )(dqt, dk, dvt, dk, dvt, lq, norm_w.reshape(DIFF_DV, 1))


def _ctx_attn_body(nq_ref, nk_ref, nv_ref, dqt_ref, dk_ref, dvt_ref, lq_ref, nw_ref, na_in, df_in,
                   na_out, df_out, *, lambda_init):
    del na_in, df_in
    q = nq_ref[...]
    k = nk_ref[...]
    v = nv_ref[...]
    lane = lax.broadcasted_iota(jnp.int32, (1, NA_HEADS * NA_DIM), 1) // NA_DIM
    out = jnp.zeros(na_out.shape, F32)
    for hh in range(NA_HEADS):
        mh = lane == hh
        s = _dot_nt(jnp.where(mh, q, jnp.zeros_like(q)), k)
        p = jnp.exp2(s - jnp.max(s, axis=-1, keepdims=True))
        o = _dot(p.astype(BF16), v) / jnp.sum(p, axis=-1, keepdims=True)
        out = out + jnp.where(mh, o, 0.0)
    na_out[...] = out.astype(na_out.dtype)

    lam = _diff_lambda(lq_ref, lambda_init)
    tq = dqt_ref.shape[2]
    for hh in range(DIFF_HEADS):
        s = _dot(dk_ref[hh], _stack_qt(dqt_ref[hh]))
        p = jnp.exp2(s - jnp.max(s, axis=0, keepdims=True))
        acc = _dot(dvt_ref[hh], p.astype(BF16))
        df_out[hh] = _diff_finish_t(acc, tq, lam, nw_ref[...], lambda_init).astype(df_out.dtype)


def _ctx_attn(nq, nk, nv, dqt, dk, dvt, lq, norm_w, na_o, df_o, *, batch, seq, ctx, lambda_init):
    blk0 = batch * seq // ctx
    s256 = pl.BlockSpec((ctx, 256), lambda b: (blk0 + b, 0))
    sh64 = pl.BlockSpec((DIFF_HEADS, ctx, 64), lambda b: (0, blk0 + b, 0))
    sq_t = pl.BlockSpec((DIFF_HEADS, 64, ctx), lambda b: (0, 0, blk0 + b))
    sv_t = pl.BlockSpec((DIFF_HEADS, DIFF_VROWS, ctx), lambda b: (0, 0, blk0 + b))
    return pl.pallas_call(
        functools.partial(_ctx_attn_body, lambda_init=lambda_init),
        grid=(batch,),
        in_specs=[s256, s256, s256, sq_t, sh64, sv_t, _resident(lq.shape), _resident((DIFF_DV, 1)),
                  pl.BlockSpec(memory_space=pl.ANY), pl.BlockSpec(memory_space=pl.ANY)],
        out_specs=[s256, sh64],
        out_shape=[jax.ShapeDtypeStruct(na_o.shape, na_o.dtype), jax.ShapeDtypeStruct(df_o.shape, df_o.dtype)],
        input_output_aliases={8: 0, 9: 1},
        compiler_params=_params("arbitrary"),
        name="ctx_attn",
    )(nq, nk, nv, dqt, dk, dvt, lq, norm_w.reshape(DIFF_DV, 1), na_o, df_o)


def _conv_body(u_ref, prev_ref, next_ref, dw_ref, dwb_ref, lng_ref, lnb_ref, pw_ref, pwb_ref,
               o_ref, pad_ref, sh_ref, *, tiles_per_seq, n_latent_tiles):
    t = u_ref.shape[0]
    i = pl.program_id(0)
    in_ctx = i >= n_latent_tiles
    first = in_ctx | (i % tiles_per_seq == 0)
    last = in_ctx | (i % tiles_per_seq == tiles_per_seq - 1)
    pad_ref[0:HALO, :] = jnp.where(first, 0.0, prev_ref[...])
    pad_ref[HALO:HALO + t, :] = u_ref[...]
    pad_ref[HALO + t:HALO + t + HALO, :] = jnp.where(last, 0.0, next_ref[...])
    base = HALO - CONV_K // 2
    first = {}
    for phase in range(8):
        taps = [k for k in range(CONV_K) if (base + k) % 8 == phase]
        first[phase] = base + taps[0]
        span = base + taps[-1] + t - first[phase]
        sh_ref[phase, 0:span, :] = pad_ref[first[phase]:first[phase] + span, :]
    sub = 64
    parts = []
    for r0 in range(0, t, sub):
        acc = jnp.zeros((sub, CONV_CH), F32)
        for k in range(CONV_K):
            phase = (base + k) % 8
            off = base + k - first[phase] + r0
            acc = acc + dw_ref[k:k + 1, :] * sh_ref[phase, off:off + sub, :]
        parts.append(acc)
    y = jnp.concatenate(parts, axis=0) + dwb_ref[...]
    mu = jnp.mean(y, axis=-1, keepdims=True)
    yc = y - mu
    var = jnp.mean(yc * yc, axis=-1, keepdims=True)
    y = _silu(yc * lax.rsqrt(var + LN_EPS) * lng_ref[...] + lnb_ref[...])
    o_ref[...] = (_dot(y.astype(BF16), pw_ref[...]) + pwb_ref[...]).astype(o_ref.dtype)


def _conv(cu, dw, dw_b, ln_g, ln_b, pw, pw_b, *, batch, seq, ctx):
    r = cu.shape[0]
    t = CONV_TILE
    assert ctx == t and seq % t == 0
    n_tiles = r // t
    hb = t // HALO
    vec = lambda a: a.reshape(1, CONV_CH)
    return pl.pallas_call(
        functools.partial(_conv_body, tiles_per_seq=seq // t, n_latent_tiles=batch * seq // t),
        grid=(n_tiles,),
        in_specs=[pl.BlockSpec((t, CONV_CH), lambda i: (i, 0)),
                  pl.BlockSpec((HALO, CONV_CH), lambda i: (jnp.maximum(i * hb - 1, 0), 0)),
                  pl.BlockSpec((HALO, CONV_CH), lambda i: (jnp.minimum((i + 1) * hb, n_tiles * hb - 1), 0)),
                  _resident((CONV_K, CONV_CH))] + [_resident((1, CONV_CH))] * 3
                 + [_resident((CONV_CH, CONV_CH)), _resident((1, CONV_CH))],
        out_specs=pl.BlockSpec((t, CONV_CH), lambda i: (i, 0)),
        out_shape=jax.ShapeDtypeStruct((r, CONV_CH), BF16),
        scratch_shapes=[pltpu.VMEM((t + 2 * HALO, CONV_CH), F32),
                        pltpu.VMEM((8, t + 2 * HALO, CONV_CH), F32)],
        compiler_params=_params("parallel"),
        name="conv",
    )(cu, cu, cu, dw, vec(dw_b), vec(ln_g), vec(ln_b), pw.astype(BF16), vec(pw_b))


def kernel(x, c, ctx, c_ctx, ada_w, ada_b, norm_ffn1, ffn1_w13, ffn1_w2, norm_mix, w_in, gla_wa_f, gla_ba_f, gla_wa_b, gla_ba_b, gla_norm, na_rpb, diff_lq1, diff_lk1, diff_lq2, diff_lk2, diff_norm, conv_dw, conv_dw_b, conv_ln_g, conv_ln_b, conv_pw, conv_pw_b, w_out, norm_ffn2, ffn2_w13, ffn2_w2, final_norm):
    batch, seq, d = x.shape
    n_ctx = ctx.shape[1]
    depth = ada_w.shape[0]
    tm = ROW_TILE
    assert seq % tm == 0 and (batch * n_ctx) % tm == 0 and batch + 1 <= 8
    lat_tiles = batch * seq // tm
    all_tiles = lat_tiles + batch * n_ctx // tm
    tiles_per_batch = seq // tm

    def group_of(i):
        return jnp.minimum(i // tiles_per_batch, batch)

    def pos_of(i):
        return jnp.where(i < lat_tiles, i % tiles_per_batch, tiles_per_batch)

    c_rows = jnp.concatenate([c, c_ctx[None, :], jnp.zeros((8 - batch - 1, d), F32)], axis=0)
    mods_all = _ada(c_rows, ada_w, ada_b)[:, :batch + 1].reshape(depth, batch + 1, N_MOD, d)
    rope = _rope_tables(seq, tm)
    h = x.reshape(batch * seq, d)
    h_ctx = ctx.reshape(batch * n_ctx, d)

    for i in range(depth):
        last = i == depth - 1
        lambda_init = 0.8 - 0.6 * math.exp(-0.3 * i)
        mods = mods_all[i]
        tok = dict(n_tiles=all_tiles, group_of=group_of)
        geo = dict(batch=batch, seq=seq, ctx=n_ctx)

        h = _ffn(h, mods, norm_ffn1[i], ffn1_w13[i], ffn1_w2[i], final_norm, k0=0, final=False,
                 h_ctx=h_ctx if i == 0 else None, **tok)

        wa, ba = _gate_weights(gla_wa_f[i], gla_ba_f[i], gla_wa_b[i], gla_ba_b[i])
        (gqk, gv, gvt, gg, glg, nq, nk, nv, dq, dk, dv, cu) = _proj(
            h, mods, norm_mix[i], _permute_w_in(w_in[i]), wa, ba, rope, pos_of=pos_of, **tok)

        gnorm = jnp.tile(gla_norm[i], GLA_HEADS).reshape(1, GLA_HEADS * GLA_DV)
        o_f = _gla(gqk, gv, gvt, glg, gg, None, gnorm, reverse=False, **geo)
        gx = _gla(gqk, gv, gvt, glg, gg, o_f, gnorm, reverse=True, **geo)

        nx = _na(nq, nk, nv, _na_col_tables(na_rpb[i]), **geo)

        lq = jnp.stack([diff_lq1[i], diff_lk1[i], diff_lq2[i], diff_lk2[i]])
        dx = _diff(dq, dk, dv, lq, diff_norm[i], lambda_init=lambda_init, **geo)
        if not last:
            nx, dx = _ctx_attn(nq, nk, nv, dq, dk, dv, lq, diff_norm[i], nx, dx,
                               lambda_init=lambda_init, **geo)

        cx = _conv(cu, conv_dw[i], conv_dw_b[i], conv_ln_g[i], conv_ln_b[i], conv_pw[i], conv_pw_b[i], **geo)

        if last:
            tok = dict(n_tiles=lat_tiles, group_of=group_of)
        h = _ffn(h, mods, norm_ffn2[i], ffn2_w13[i], ffn2_w2[i], final_norm, k0=6, final=last,
                 mixers=(gx, nx, dx, cx, w_out[i], (batch, seq, n_ctx)), **tok)

    return h.reshape(batch, seq, d)
```

```python
import functools
import math

import jax
import jax.numpy as jnp
from jax import lax
from jax.experimental import pallas as pl
from jax.experimental.pallas import tpu as pltpu

F32 = jnp.float32
BF16 = jnp.bfloat16

GRID_W = 64
N_MOD = 9
RMS_EPS = 1e-6
LN_EPS = 1e-5
NEG_INF = -1e30
GLA_HEADS, GLA_DK, GLA_DV, GLA_RANK, GLA_TAU, GLA_CHUNK = 4, 32, 64, 16, 16.0, 64
NA_HEADS, NA_DIM, NA_ROWS, NA_COLS = 4, 64, 8, 16
DIFF_HEADS, DIFF_DK, DIFF_DV = 4, 32, 64
DIFF_VROWS = 80
CONV_CH, CONV_K = 256, 31
ROPE_BASE = 10000.0
LOG2E = 1.4426950408889634

LANES = 128
VMEM_LIMIT = 56 * 1024 * 1024

ROW_TILE = 512
FF_CHUNK = 256
GLA_BLOCK = 256
NA_QROWS = 8
NA_KBLK = 256
CONV_TILE = 256
DIFF_TQ = 256
DIFF_QTILES = 16
DIFF_TK = 256
DIFF_UNROLL = 32
DIFF_AHEAD = 4
DIFF_RING = 8
HALO = 16

C_GLA, C_NA, C_DIFF, C_CONV, C_AUX, C_END = 0, 768, 1536, 2304, 2816, 2944


def _dot(a, b):
    return jnp.dot(a, b, preferred_element_type=F32)


def _dot_nt(a, b):
    return lax.dot_general(a, b, (((1,), (1,)), ((), ())), preferred_element_type=F32)


def _params(*sem):
    return pltpu.CompilerParams(dimension_semantics=sem, vmem_limit_bytes=VMEM_LIMIT)


def _resident(shape):
    nd = len(shape)
    return pl.BlockSpec(shape, lambda *_: (0,) * nd, pipeline_mode=pl.Buffered(1))


def _silu(x):
    return x * jax.nn.sigmoid(x)


def _rms(x, w):
    return x * lax.rsqrt(jnp.mean(x * x, axis=-1, keepdims=True) + RMS_EPS) * w


def _ada_body(c_ref, w_ref, b_ref, o_ref):
    s = _silu(c_ref[...])
    o_ref[...] = jnp.dot(s, w_ref[...], precision=lax.Precision.HIGHEST,
                         preferred_element_type=F32) + b_ref[...]


def _ada(c_rows, ada_w, ada_b):
    depth, d, _ = ada_w.shape
    return pl.pallas_call(
        _ada_body,
        grid=(depth, N_MOD),
        in_specs=[pl.BlockSpec((8, d), lambda l, n: (0, 0)),
                  pl.BlockSpec((None, d, d), lambda l, n: (l, 0, n)),
                  pl.BlockSpec((None, 1, d), lambda l, n: (l, 0, n))],
        out_specs=pl.BlockSpec((None, 8, d), lambda l, n: (l, 0, n)),
        out_shape=jax.ShapeDtypeStruct((depth, 8, N_MOD * d), F32),
        compiler_params=_params("arbitrary", "arbitrary"),
        name="ada",
    )(c_rows, ada_w, ada_b.reshape(depth, 1, N_MOD * d))


def _ffn_body(*refs, k0, final, source, n_first):
    h_ref = refs[0]
    n_extra = {"plain": 0, "two_arrays": 1, "mixers": 6}[source]
    extra = refs[1:1 + n_extra]
    mod_ref, nw_ref, w13_ref, w2_ref, fw_ref, o_ref, xb_ref, g_ref, h13_ref = refs[1 + n_extra:]
    ff = w2_ref.shape[0]
    tf = h13_ref.shape[2] // 2
    n_chunks = ff // tf
    x = h_ref[...]
    if source == "two_arrays":
        x = jnp.where(pl.program_id(0) < n_first, x, extra[0][...])
    elif source == "mixers":
        gx_ref, gxc_ref, nx_ref, dx_ref, cx_ref, wo_ref = extra
        gx = jnp.where(pl.program_id(0) < n_first, gx_ref[...], gxc_ref[...].reshape(gx_ref.shape))
        mix = jnp.concatenate([gx, nx_ref[...]] + [dx_ref[hh] for hh in range(DIFF_HEADS)]
                              + [cx_ref[...]], axis=-1)
        x = x + mod_ref[5:6, :] * _dot(mix, wo_ref[...])
    xm = _rms(x, nw_ref[...]) * (1.0 + mod_ref[k0 + 1:k0 + 2, :]) + mod_ref[k0:k0 + 1, :]
    xb_ref[...] = xm.astype(BF16)

    def cols(c):
        return c * tf if isinstance(c, int) else pl.multiple_of(c * tf, tf)

    def up(c, slot):
        h13_ref[slot, :, :tf] = _dot(xb_ref[...], w13_ref[:, pl.ds(cols(c), tf)])
        h13_ref[slot, :, tf:] = _dot(xb_ref[...], w13_ref[:, pl.ds(ff + cols(c), tf)])

    def gate(c, slot):
        a = h13_ref[slot, :, :tf]
        u = h13_ref[slot, :, tf:]
        g_ref[:, pl.ds(cols(c), tf)] = (_silu(a) * u).astype(BF16)

    def pair(t, carry):
        up(2 * t + 1, 1)
        gate(2 * t, 0)
        up(2 * t + 2, 0)
        gate(2 * t + 1, 1)
        return carry

    up(0, 0)
    for t in range((n_chunks - 1) // 2):
        pair(t, 0)
    gate(n_chunks - 1, 0)
    out = x + (0.5 * mod_ref[k0 + 2:k0 + 3, :]) * _dot(g_ref[...], w2_ref[...])
    if final:
        out = _rms(out, fw_ref[...])
    o_ref[...] = out


def _ffn(h, mods, norm_w, w13, w2, final_w, *, k0, n_tiles, group_of, final, h_ctx=None, mixers=None):
    d = h.shape[1]
    tm = ROW_TILE
    ff = w2.shape[0]
    n_chunks = ff // FF_CHUNK
    assert ff % FF_CHUNK == 0 and n_chunks % 2 == 1
    row = lambda i: (i, 0)
    source, n_first, extra, extra_specs, h_spec = "plain", None, [], [], pl.BlockSpec((tm, d), row)
    if h_ctx is not None:
        source, n_first, extra = "two_arrays", h.shape[0] // tm, [h_ctx]
        h_spec = pl.BlockSpec((tm, d), lambda i: (jnp.minimum(i, n_first - 1), 0))
        extra_specs = [pl.BlockSpec((tm, d), lambda i: (jnp.maximum(i - n_first, 0), 0))]
    elif mixers is not None:
        gx, nx, dx, cx, w_out, (batch, seq, n_ctx) = mixers
        assert batch * n_ctx == tm and seq % tm == 0 and seq % n_ctx == 0
        source, n_first, extra = "mixers", batch * seq // tm, [gx, gx, nx, dx, cx, w_out.astype(BF16)]
        per_seq = seq // tm
        extra_specs = [pl.BlockSpec((None, tm, 256),
                                    lambda i: (jnp.minimum(i // per_seq, batch - 1), i % per_seq, 0)),
                       pl.BlockSpec((batch, n_ctx, 256), lambda i: (0, seq // n_ctx, 0)),
                       pl.BlockSpec((tm, 256), row),
                       pl.BlockSpec((DIFF_HEADS, tm, 64), lambda i: (0, i, 0)),
                       pl.BlockSpec((tm, 256), row), _resident(w_out.shape)]
    return pl.pallas_call(
        functools.partial(_ffn_body, k0=k0, final=final, source=source, n_first=n_first),
        grid=(n_tiles,),
        in_specs=[h_spec] + extra_specs + [
            pl.BlockSpec((None, N_MOD, d), lambda i: (group_of(i), 0, 0)),
            _resident((1, d)),
            _resident(w13.shape),
            _resident(w2.shape),
            _resident((1, d))],
        out_specs=pl.BlockSpec((tm, d), row),
        out_shape=jax.ShapeDtypeStruct((n_tiles * tm, d), F32),
        scratch_shapes=[pltpu.VMEM((tm, d), BF16), pltpu.VMEM((tm, ff), BF16),
                        pltpu.VMEM((2, tm, 2 * FF_CHUNK), F32)],
        compiler_params=_params("parallel"),
        name="ffn",
    )(h, *extra, mods, norm_w.reshape(1, d), w13.astype(BF16), w2.astype(BF16), final_w.reshape(1, d))


def _log_sigmoid(x):
    return jnp.minimum(x, 0.0) - jnp.log1p(jnp.exp(-jnp.abs(x)))


def _rope_rotate(x):
    n = x.shape[-1]
    lane = lax.broadcasted_iota(jnp.int32, x.shape, 1)
    up = pltpu.roll(x, n - 8, 1)
    dn = pltpu.roll(x, 8, 1)
    return jnp.where((lane & 15) < 8, -up, dn)


def _rope_rotate_rows(x):
    n = x.shape[0]
    row = lax.broadcasted_iota(jnp.int32, x.shape, 0)
    up = pltpu.roll(x, n - 8, 0)
    dn = pltpu.roll(x, 8, 0)
    return jnp.where((row & 15) < 8, -up, dn)


def _proj_body(h_ref, mod_ref, nw_ref, w_ref, wt_ref, wa_ref, ba_ref,
               cos_ref, sin_ref, cost_ref, sint_ref,
               gqk_ref, gv_ref, gvt_ref, gg_ref, glg_ref, nq_ref, nk_ref, nv_ref,
               dqt_ref, dk_ref, dvt_ref, cu_ref, xb_ref):
    x = h_ref[...]
    tm = x.shape[0]
    xm = _rms(x, nw_ref[...]) * (1.0 + mod_ref[4:5, :]) + mod_ref[3:4, :]
    xb_ref[...] = xm.astype(BF16)

    z = _dot(xb_ref[...], w_ref[:, C_GLA:C_GLA + 256])
    lane = lax.broadcasted_iota(jnp.int32, (1, 256), 1)
    gqk_ref[...] = z * jnp.where(lane < 128, GLA_DK ** -0.5, 1.0)
    gv_ref[...] = _dot(xb_ref[...], w_ref[:, C_GLA + 256:C_GLA + 512]).astype(BF16)
    zt = _dot_nt(wt_ref[...], xb_ref[...])
    gvt_ref[...] = zt[0:256, :].astype(BF16)
    gg_ref[...] = _dot(xb_ref[...], w_ref[:, C_GLA + 512:C_GLA + 768]).astype(BF16)
    aux = _dot(xb_ref[...], w_ref[:, C_AUX:C_END])
    pre = _dot(aux.astype(BF16), wa_ref[...]) + ba_ref[...]
    glg_ref[...] = _log_sigmoid(pre) * (1.0 / GLA_TAU)

    nq_ref[...] = (_dot(xb_ref[...], w_ref[:, C_NA:C_NA + 256]) * (NA_DIM ** -0.5 * LOG2E)).astype(BF16)
    nk_ref[...] = _dot(xb_ref[...], w_ref[:, C_NA + 256:C_NA + 512]).astype(BF16)
    nv_ref[...] = _dot(xb_ref[...], w_ref[:, C_NA + 512:C_NA + 768]).astype(BF16)

    cos = cos_ref[...]
    sin = sin_ref[...]
    cos2 = jnp.concatenate([cos, cos], axis=1)
    sin2 = jnp.concatenate([sin, sin], axis=1)
    zk = _dot(xb_ref[...], w_ref[:, C_DIFF + 256:C_DIFF + 512])
    zk = zk * cos2 + _rope_rotate(zk) * sin2
    n_rep = 2 * DIFF_HEADS
    cos_t = jnp.concatenate([cost_ref[...]] * n_rep, axis=0)
    sin_t = jnp.concatenate([sint_ref[...]] * n_rep, axis=0)
    zqt = zt[256:512, :]
    zqt = (zqt * cos_t + _rope_rotate_rows(zqt) * sin_t) * (DIFF_DK ** -0.5 * LOG2E)
    zvt = zt[512:768, :]
    pad_rows = DIFF_VROWS - DIFF_DV
    one_row = jnp.where(lax.broadcasted_iota(jnp.int32, (pad_rows, tm), 0) == 0, 1.0, 0.0).astype(BF16)
    for hh in range(DIFF_HEADS):
        sl = slice(64 * hh, 64 * hh + 64)
        dk_ref[hh] = zk[:, sl].astype(BF16)
        dqt_ref[hh] = zqt[sl, :].astype(BF16)
        dvt_ref[hh, 0:64, :] = zvt[sl, :].astype(BF16)
        dvt_ref[hh, DIFF_DV:DIFF_VROWS, :] = one_row

    za = _dot(xb_ref[...], w_ref[:, C_CONV:C_CONV + 256])
    zg = _dot(xb_ref[...], w_ref[:, C_CONV + 256:C_CONV + 512])
    cu_ref[...] = za * jax.nn.sigmoid(zg)


def _proj(h, mods, norm_w, w_p, wa, ba, rope, *, n_tiles, group_of, pos_of):
    r, d = h.shape
    tm = ROW_TILE
    row = lambda i: (i, 0)
    hrow = lambda i: (0, i, 0)
    hcol = lambda i: (0, 0, i)
    f32_256 = jax.ShapeDtypeStruct((r, 256), F32)
    bf_256 = jax.ShapeDtypeStruct((r, 256), BF16)
    w_t = jnp.concatenate([w_p[:, C_GLA + 256:C_GLA + 512], w_p[:, C_DIFF:C_DIFF + 256],
                           w_p[:, C_DIFF + 512:C_DIFF + 768]], axis=1).T
    cos_r, sin_r, cos_c, sin_c = rope
    out_shape = [f32_256, bf_256, jax.ShapeDtypeStruct((256, r), BF16), bf_256, f32_256,
                 bf_256, bf_256, bf_256,
                 jax.ShapeDtypeStruct((DIFF_HEADS, 64, r), BF16),
                 jax.ShapeDtypeStruct((DIFF_HEADS, r, 64), BF16),
                 jax.ShapeDtypeStruct((DIFF_HEADS, DIFF_VROWS, r), BF16),
                 f32_256]
    out_specs = [pl.BlockSpec((tm, 256), row)] * 2 + [pl.BlockSpec((256, tm), lambda i: (0, i))] + [
        pl.BlockSpec((tm, 256), row)] * 5 + [
        pl.BlockSpec((DIFF_HEADS, 64, tm), hcol),
        pl.BlockSpec((DIFF_HEADS, tm, 64), hrow),
        pl.BlockSpec((DIFF_HEADS, DIFF_VROWS, tm), hcol),
        pl.BlockSpec((tm, 256), row)]
    return pl.pallas_call(
        _proj_body,
        grid=(n_tiles,),
        in_specs=[pl.BlockSpec((tm, d), row),
                  pl.BlockSpec((None, N_MOD, d), lambda i: (group_of(i), 0, 0)),
                  _resident((1, d)),
                  _resident(w_p.shape),
                  _resident(w_t.shape),
                  _resident(wa.shape),
                  _resident(ba.shape),
                  pl.BlockSpec((tm, LANES), lambda i: (pos_of(i), 0)),
                  pl.BlockSpec((tm, LANES), lambda i: (pos_of(i), 0)),
                  pl.BlockSpec((DIFF_DK, tm), lambda i: (0, pos_of(i))),
                  pl.BlockSpec((DIFF_DK, tm), lambda i: (0, pos_of(i)))],
        out_specs=out_specs,
        out_shape=out_shape,
        scratch_shapes=[pltpu.VMEM((tm, d), BF16)],
        compiler_params=_params("parallel"),
        name="proj",
    )(h, mods, norm_w.reshape(1, d), w_p, w_t, wa, ba, cos_r, sin_r, cos_c, sin_c)


def _permute_w_in(w_in):
    d = w_in.shape[0]
    g0 = 2 * GLA_HEADS * GLA_DK + 2 * GLA_HEADS * GLA_DV
    aux = w_in[:, g0:g0 + 2 * GLA_RANK]
    rest = w_in[:, g0 + 2 * GLA_RANK:]
    pad = jnp.zeros((d, C_END - C_AUX - 2 * GLA_RANK), w_in.dtype)
    return jnp.concatenate([w_in[:, :g0], rest, aux, pad], axis=1).astype(BF16)


def _gate_weights(wa_f, ba_f, wa_b, ba_b):
    n = GLA_HEADS * GLA_DK
    wa = jnp.zeros((C_END - C_AUX, 2 * n), F32)
    wa = wa.at[:GLA_RANK, :n].set(wa_f).at[GLA_RANK:2 * GLA_RANK, n:].set(wa_b)
    return wa.astype(BF16), jnp.concatenate([ba_f, ba_b]).reshape(1, 2 * n)


def _rope_tables(seq, tile):
    t = jnp.arange(seq)
    row = (t // GRID_W).astype(F32)
    col = (t % GRID_W).astype(F32)
    half = DIFF_DK // 2
    inv = 1.0 / (ROPE_BASE ** (jnp.arange(0, half, 2, dtype=F32) / half))
    ang_r = row[:, None] * inv
    ang_c = col[:, None] * inv
    ang = jnp.concatenate([ang_r, ang_r, ang_c, ang_c], axis=-1)
    cos = jnp.concatenate([jnp.cos(ang), jnp.ones((tile, DIFF_DK), F32)], axis=0)
    sin = jnp.concatenate([jnp.sin(ang), jnp.zeros((tile, DIFF_DK), F32)], axis=0)
    rep = (1, LANES // DIFF_DK)
    return jnp.tile(cos, rep), jnp.tile(sin, rep), cos.T, sin.T


def _split3(x):
    hi = x.astype(BF16)
    r1 = x - hi.astype(F32)
    mid = r1.astype(BF16)
    lo = (r1 - mid.astype(F32)).astype(BF16)
    return hi, mid, lo


def _gla_body(*refs, reverse, batch):
    per_b = [refs[5 * b:5 * b + 5] for b in range(batch)]
    of_ref, nw_ref, o_ref, st_ref = refs[5 * batch:]
    blk = per_b[0][0].shape[0]
    n_chunks = blk // GLA_CHUNK
    nqk = GLA_HEADS * GLA_DK
    nv = GLA_HEADS * GLA_DV

    @pl.when(pl.program_id(0) == 0)
    def _():
        st_ref[...] = jnp.zeros_like(st_ref)

    ri = lax.broadcasted_iota(jnp.int32, (blk, blk), 0)
    ci = lax.broadcasted_iota(jnp.int32, (blk, blk), 1)
    same = (ri // GLA_CHUNK) == (ci // GLA_CHUNK)
    tri = jnp.where(same & ((ci >= ri) if reverse else (ci <= ri)), 1.0, 0.0).astype(BF16)
    bcum_of = []
    for qk_ref, v_ref, vt_ref, lg_ref, g_ref in per_b:
        lg = lg_ref[:, nqk:2 * nqk] if reverse else lg_ref[:, 0:nqk]
        hi, mid, lo = _split3(lg)
        bcum_of.append(_dot(tri, hi) + _dot(tri, mid) + _dot(tri, lo))

    c = GLA_CHUNK
    rk = lax.broadcasted_iota(jnp.int32, (GLA_HEADS * c, nqk), 0) // c
    ck = lax.broadcasted_iota(jnp.int32, (GLA_HEADS * c, nqk), 1) // GLA_DK
    mask_k = rk == ck
    rv = lax.broadcasted_iota(jnp.int32, (GLA_HEADS * c, nv), 0) // c
    cv = lax.broadcasted_iota(jnp.int32, (GLA_HEADS * c, nv), 1) // GLA_DV
    mask_v = rv == cv
    rs = lax.broadcasted_iota(jnp.int32, (nv, nqk), 0) // GLA_DV
    cs = lax.broadcasted_iota(jnp.int32, (nv, nqk), 1) // GLA_DK
    mask_s = rs == cs
    ai = lax.broadcasted_iota(jnp.int32, (c, GLA_HEADS * c), 0)
    aj = lax.broadcasted_iota(jnp.int32, (c, GLA_HEADS * c), 1) % c
    mask_a = (aj >= ai) if reverse else (aj <= ai)

    order = range(n_chunks - 1, -1, -1) if reverse else range(n_chunks)
    units = [(b, ch) for ch in order for b in range(batch)]
    q_in, a_raw, u_t, decay = {}, {}, {}, {}
    vts = [refs_b[2][...].astype(BF16) for refs_b in per_b]
    for b, ch in units:
        qk_ref = per_b[b][0]
        rows = slice(ch * c, (ch + 1) * c)
        k = qk_ref[rows, nqk:2 * nqk]
        bcum = bcum_of[b][rows, :]
        btot = bcum[0:1, :] if reverse else bcum[c - 1:c, :]
        q_in[b, ch] = (qk_ref[rows, 0:nqk] * jnp.exp(bcum)).astype(BF16)
        k_in = k * jnp.exp(-bcum)
        k_out = (k * jnp.exp(btot - bcum)).astype(BF16)
        k_bd = jnp.where(mask_k, jnp.concatenate([k_in] * GLA_HEADS, axis=0), 0.0).astype(BF16)
        a_raw[b, ch] = _dot_nt(q_in[b, ch], k_bd)
        k_pad = jnp.concatenate([jnp.zeros((n * c, nqk), BF16) for n in (ch,) if n] + [k_out]
                                + [jnp.zeros((n * c, nqk), BF16) for n in (n_chunks - 1 - ch,) if n], axis=0)
        u_t[b, ch] = jnp.where(mask_s, _dot(vts[b], k_pad), 0.0)
        decay[b, ch] = jnp.exp(btot)
    o = {}
    for b, ch in units:
        v = per_b[b][1][ch * c:(ch + 1) * c, :]
        v4 = jnp.concatenate([v] * GLA_HEADS, axis=0)
        v_bd = jnp.where(mask_v, v4, jnp.zeros_like(v4))
        o[b, ch] = _dot(jnp.where(mask_a, a_raw[b, ch], 0.0).astype(BF16), v_bd)
    st = [st_ref[b] for b in range(batch)]
    for b, ch in units:
        o[b, ch] = o[b, ch] + _dot_nt(q_in[b, ch], st[b].astype(BF16))
        st[b] = st[b] * decay[b, ch] + u_t[b, ch]
    for b in range(batch):
        st_ref[b] = st[b]
        o_blk = jnp.concatenate([o[b, ch] for ch in range(n_chunks)], axis=0)
        if reverse:
            o_blk = o_blk + of_ref[b]
            hi2, lo2, _ = _split3(o_blk * o_blk)
            hr = lax.broadcasted_iota(jnp.int32, (nv, nv), 0) // GLA_DV
            hc = lax.broadcasted_iota(jnp.int32, (nv, nv), 1) // GLA_DV
            seg = jnp.where(hr == hc, 1.0, 0.0).astype(BF16)
            ms = (_dot(hi2, seg) + _dot(lo2, seg)) * (1.0 / GLA_DV)
            o_blk = o_blk * lax.rsqrt(ms + RMS_EPS) * nw_ref[...] * _silu(per_b[b][4][...].astype(F32))
        o_ref[b] = o_blk.astype(o_ref.dtype)


def _gla(gqk, gv, gvt, glg, gg, o_f, norm_w4, *, reverse, batch, seq, ctx):
    blk = GLA_BLOCK
    nc, nl = ctx // blk, seq // blk
    ctx_base = batch * seq // blk

    def step_blk(s):
        if reverse:
            return jnp.where(s < nc, nl + (nc - 1 - s), nl - 1 - (s - nc))
        return jnp.where(s < nc, nl + s, s - nc)

    def row_blk(b):
        return lambda s: jnp.where(s < nc, ctx_base + b * nc - nl, b * nl) + step_blk(s)

    specs, args = [], []
    for b in range(batch):
        spec = pl.BlockSpec((blk, 256), lambda s, f=row_blk(b): (f(s), 0))
        spec_t = pl.BlockSpec((256, blk), lambda s, f=row_blk(b): (0, f(s)))
        if reverse:
            specs += [spec, spec, spec_t, spec, spec]
            args += [gqk, gv, gvt, glg, gg]
        else:
            specs += [spec, spec, spec_t, spec, _resident((1, 256))]
            args += [gqk, gv, gvt, glg, norm_w4]
    seq_spec = pl.BlockSpec((batch, blk, 256), lambda s: (0, step_blk(s), 0))
    if o_f is None:
        o_f, of_spec = norm_w4, _resident((1, 256))
    else:
        of_spec = seq_spec
    return pl.pallas_call(
        functools.partial(_gla_body, reverse=reverse, batch=batch),
        grid=(nc + nl,),
        in_specs=specs + [of_spec, _resident((1, 256))],
        out_specs=seq_spec,
        out_shape=jax.ShapeDtypeStruct((batch, seq + ctx, 256), BF16 if reverse else F32),
        scratch_shapes=[pltpu.VMEM((batch, GLA_HEADS * GLA_DV, GLA_HEADS * GLA_DK), F32)],
        compiler_params=_params("arbitrary"),
        name="gla_bwd" if reverse else "gla_fwd",
    )(*args, o_f, norm_w4)


def _na_col_tables(rpb):
    col = jnp.arange(GRID_W)
    cstart = jnp.clip(col - NA_COLS // 2, 0, GRID_W - NA_COLS)
    col_ok = (col[None, :] >= cstart[:, None]) & (col[None, :] < cstart[:, None] + NA_COLS)
    col_off = jnp.clip(col[None, :] - col[:, None] + (NA_COLS - 1), 0, 2 * NA_COLS - 2)
    t = jnp.where(col_ok[None, None], rpb[:, :, col_off] * LOG2E, NEG_INF)
    dead = jnp.full((NA_HEADS, 1, GRID_W, GRID_W), NEG_INF, F32)
    t = jnp.concatenate([dead, t.astype(F32), dead], axis=1)
    return jnp.concatenate([t[:, :-1], t[:, 1:]], axis=-1)


def _na_body(q_ref, k0, k1, k2, k3, v0, v1, v2, v3, kc_ref, vc_ref, tab_ref, o_ref, *, rows, key_blocks):
    k_blk = [k0[...], k1[...], k2[...], k3[...]]
    v_blk = [v0[...], v1[...], v2[...], v3[...]]
    kc = kc_ref[...]
    vc = vc_ref[...]

    j = pl.program_id(1)
    rows_per_blk = NA_KBLK // GRID_W
    r0 = j * NA_QROWS
    kr0 = jnp.clip(2 * j - 1, 0, key_blocks - 4) * rows_per_blk
    lane_lo = lax.broadcasted_iota(jnp.int32, (1, 2 * GRID_W), 1) < GRID_W
    lane = lax.broadcasted_iota(jnp.int32, (1, NA_HEADS * NA_DIM), 1) // NA_DIM
    half = NA_QROWS // 2
    n_pairs = 3 * rows_per_blk // 2
    gq = half * GRID_W
    for g in range(2):
        first = jnp.clip(r0 + g * half - NA_ROWS // 2, 0, rows - NA_ROWS)
        shift = jnp.clip((first - kr0) // rows_per_blk, 0, 1)
        kw = jnp.concatenate([jnp.where(shift == 0, k_blk[i], k_blk[i + 1]) for i in range(3)], axis=0)
        vw = jnp.concatenate([jnp.where(shift == 0, v_blk[i], v_blk[i + 1]) for i in range(3)], axis=0)
        kg0 = kr0 + shift * rows_per_blk
        entry, ok = [], []
        for a in range(half):
            r = r0 + g * half + a
            start = jnp.clip(r - NA_ROWS // 2, 0, rows - NA_ROWS)
            for bp in range(n_pairs):
                rk = kg0 + 2 * bp
                entry.append(jnp.clip(rk - r + NA_ROWS, 0, 2 * NA_ROWS - 1))
                in0 = ((rk >= start) & (rk < start + NA_ROWS)).astype(jnp.int32)
                in1 = ((rk + 1 >= start) & (rk + 1 < start + NA_ROWS)).astype(jnp.int32)
                ok.append(jnp.where(lane_lo, in0, in1) != 0)

        def bias_of(hh):
            rows_ = []
            for a in range(half):
                tiles = [jnp.where(ok[a * n_pairs + bp], tab_ref[hh, entry[a * n_pairs + bp]], NEG_INF)
                         for bp in range(n_pairs)]
                rows_.append(jnp.concatenate(tiles, axis=1))
            return jnp.concatenate(rows_, axis=0)

        q = q_ref[g * gq:(g + 1) * gq, :]
        out = jnp.zeros((gq, NA_HEADS * NA_DIM), F32)
        for hh in range(NA_HEADS):
            mh = lane == hh
            qh = jnp.where(mh, q, jnp.zeros_like(q))
            s_w = _dot_nt(qh, kw) + bias_of(hh)
            s_c = _dot_nt(qh, kc)
            m = jnp.maximum(jnp.max(s_w, axis=-1, keepdims=True), jnp.max(s_c, axis=-1, keepdims=True))
            p_w = jnp.exp2(s_w - m)
            p_c = jnp.exp2(s_c - m)
            l = jnp.sum(p_w, axis=-1, keepdims=True) + jnp.sum(p_c, axis=-1, keepdims=True)
            o = _dot(p_w.astype(BF16), vw) + _dot(p_c.astype(BF16), vc)
            out = out + jnp.where(mh, o / l, 0.0)
        o_ref[g * gq:(g + 1) * gq, :] = out.astype(o_ref.dtype)


def _na(nq, nk, nv, tab, *, batch, seq, ctx):
    r = nq.shape[0]
    tq = NA_QROWS * GRID_W
    nj = seq // tq
    kb = seq // NA_KBLK
    assert kb >= 4 and seq % tq == 0 and seq % ctx == 0
    ctx_blk0 = batch * seq // ctx

    def kspec(i):
        return pl.BlockSpec((NA_KBLK, 256),
                            lambda b, j: (b * kb + jnp.clip(2 * j - 1, 0, kb - 4) + i, 0))

    cspec = pl.BlockSpec((ctx, 256), lambda b, j: (ctx_blk0 + b, 0))
    return pl.pallas_call(
        functools.partial(_na_body, rows=seq // GRID_W, key_blocks=kb),
        grid=(batch, nj),
        in_specs=[pl.BlockSpec((tq, 256), lambda b, j: (b * nj + j, 0))]
                 + [kspec(i) for i in range(4)] + [kspec(i) for i in range(4)]
                 + [cspec, cspec, _resident(tab.shape)],
        out_specs=pl.BlockSpec((tq, 256), lambda b, j: (b * nj + j, 0)),
        out_shape=jax.ShapeDtypeStruct((r, 256), BF16),
        compiler_params=_params("parallel", "arbitrary"),
        name="na",
    )(nq, nk, nk, nk, nk, nv, nv, nv, nv, nk, nv, tab)


def _diff_lambda(lq_ref, lambda_init):
    lq = lq_ref[...]
    s1 = jnp.sum(lq[0:1, :] * lq[1:2, :], axis=-1, keepdims=True)
    s2 = jnp.sum(lq[2:3, :] * lq[3:4, :], axis=-1, keepdims=True)
    return jnp.exp(s1) - jnp.exp(s2) + lambda_init


def _stack_qt(qt):
    row = lax.broadcasted_iota(jnp.int32, (2 * DIFF_DK, 1), 0)
    zero = jnp.zeros_like(qt)
    return jnp.concatenate([jnp.where(row < DIFF_DK, qt, zero), jnp.where(row >= DIFF_DK, qt, zero)], axis=1)


def _diff_finish_t(acc, tq, lam, nw_col, lambda_init):
    o = acc[0:DIFF_DV, :] / acc[DIFF_DV:DIFF_DV + 1, :]
    od = o[:, :tq] - lam * o[:, tq:]
    ms = jnp.mean(od * od, axis=0, keepdims=True)
    y = od * lax.rsqrt(ms + RMS_EPS) * nw_col * (1.0 - lambda_init)
    y = jnp.concatenate([y, jnp.zeros_like(y)], axis=0)
    return jnp.transpose(y)[:, 0:DIFF_DV]


def _diff_body(qt_ref, kc_ref, vtc_ref, k_ref, vt_ref, lq_ref, nw_ref, o_ref, s_ref, *, tq, tk, unroll, lambda_init):
    n_tiles = qt_ref.shape[1] // tq
    lam = _diff_lambda(lq_ref, lambda_init)

    def finish(acc, t):
        q0 = t * tq if isinstance(t, int) else pl.multiple_of(t * tq, tq)
        o_ref[pl.ds(q0, tq), :] = _diff_finish_t(acc, tq, lam, nw_ref[...], lambda_init).astype(o_ref.dtype)

    def tile(t, acc_prev):
        q0 = pl.multiple_of(t * tq, tq)
        return _diff_tile(qt_ref[:, pl.ds(q0, tq)], kc_ref, vtc_ref, k_ref, vt_ref, s_ref, tk=tk, unroll=unroll,
                          after_issue=lambda: finish(acc_prev, jnp.maximum(t - 1, 0)))

    placeholder = jnp.ones((vt_ref.shape[0], 2 * tq), F32)
    finish(lax.fori_loop(0, n_tiles, tile, placeholder), n_tiles - 1)


def _diff_tile(qt, kc_ref, vtc_ref, k_ref, vt_ref, s_ref, *, tk, unroll, after_issue):
    tq = qt.shape[1]
    n_chunks = k_ref.shape[0] // tk
    ring = s_ref.shape[0]
    qs = _stack_qt(qt)

    def chunk(j):
        return pl.ds(j * tk if isinstance(j, int) else pl.multiple_of(j * tk, tk), tk)

    def issue_scores(j, slot):
        s = _dot(k_ref[chunk(j), :], qs)
        s_ref[slot] = s
        return jnp.max(s, axis=0, keepdims=True)

    s_ctx = _dot(kc_ref[...], qs)
    m0 = jnp.max(s_ctx, axis=0, keepdims=True)
    cmax0 = tuple(issue_scores(min(a, n_chunks - 1), a % ring) for a in range(DIFF_AHEAD))
    after_issue()
    acc0 = _dot(vtc_ref[...], jnp.exp2(s_ctx - m0).astype(BF16))

    def group(g, carry, tail=False):
        cmax, m, acc = carry
        for u in range(unroll):
            j = g * unroll + u
            if tail and j + DIFF_AHEAD >= n_chunks:
                c_new = cmax[0]
            else:
                c_new = issue_scores(j + DIFF_AHEAD, (u + DIFF_AHEAD) % ring)
            m_new = jnp.maximum(m, cmax[0])
            p = jnp.exp2(s_ref[u % ring] - m_new).astype(BF16)
            vt = vt_ref[:, chunk(j)]
            acc = jnp.exp2(m - m_new) * acc + _dot(vt, p)
            m, cmax = m_new, cmax[1:] + (c_new,)
        return cmax, m, acc

    n_groups = n_chunks // unroll
    trips = jnp.minimum(pl.program_id(2) + 1, 1) * (n_groups - 1)
    carry = lax.fori_loop(0, trips, group, (cmax0, m0, acc0))
    _, _, acc = group(n_groups - 1, carry, tail=True)
    return acc


def _diff(dqt, dk, dvt, lq, norm_w, *, batch, seq, ctx, lambda_init):
    h, r, _ = dk.shape
    tq = DIFF_TQ
    tqs = tq * min(DIFF_QTILES, seq // tq)
    nq = seq // tqs
    tk = min(DIFF_TK, seq)
    unroll = min(DIFF_UNROLL, seq // tk)
    assert (seq // tk) % unroll == 0 and unroll % DIFF_RING == 0 and DIFF_RING > DIFF_AHEAD and seq % tqs == 0
    ctx_blk0 = batch * seq // ctx
    return pl.pallas_call(
        functools.partial(_diff_body, tq=tq, tk=tk, unroll=unroll, lambda_init=lambda_init),
        grid=(batch, h, nq),
        in_specs=[pl.BlockSpec((None, 64, tqs), lambda b, hh, i: (hh, 0, b * nq + i)),
                  pl.BlockSpec((None, ctx, 64), lambda b, hh, i: (hh, ctx_blk0 + b, 0)),
                  pl.BlockSpec((None, DIFF_VROWS, ctx), lambda b, hh, i: (hh, 0, ctx_blk0 + b)),
                  pl.BlockSpec((None, seq, 64), lambda b, hh, i: (hh, b, 0)),
                  pl.BlockSpec((None, DIFF_VROWS, seq), lambda b, hh, i: (hh, 0, b)),
                  _resident(lq.shape),
                  _resident((DIFF_DV, 1))],
        out_specs=pl.BlockSpec((None, tqs, 64), lambda b, hh, i: (hh, b * nq + i, 0)),
        out_shape=jax.ShapeDtypeStruct((h, r, 64), BF16),
        scratch_shapes=[pltpu.VMEM((DIFF_RING, tk, 2 * tq), F32)],
        compiler_params=_params("parallel", "parallel", "arbitrary"),
        name="diff",
    )(dqt, dk, dvt, dk, dvt, lq, norm_w.reshape(DIFF_DV, 1))


def _ctx_attn_body(nq_ref, nk_ref, nv_ref, dqt_ref, dk_ref, dvt_ref, lq_ref, nw_ref, na_in, df_in,
                   na_out, df_out, *, lambda_init):
    del na_in, df_in
    q = nq_ref[...]
    k = nk_ref[...]
    v = nv_ref[...]
    lane = lax.broadcasted_iota(jnp.int32, (1, NA_HEADS * NA_DIM), 1) // NA_DIM
    out = jnp.zeros(na_out.shape, F32)
    for hh in range(NA_HEADS):
        mh = lane == hh
        s = _dot_nt(jnp.where(mh, q, jnp.zeros_like(q)), k)
        p = jnp.exp2(s - jnp.max(s, axis=-1, keepdims=True))
        o = _dot(p.astype(BF16), v) / jnp.sum(p, axis=-1, keepdims=True)
        out = out + jnp.where(mh, o, 0.0)
    na_out[...] = out.astype(na_out.dtype)

    lam = _diff_lambda(lq_ref, lambda_init)
    tq = dqt_ref.shape[2]
    for hh in range(DIFF_HEADS):
        s = _dot(dk_ref[hh], _stack_qt(dqt_ref[hh]))
        p = jnp.exp2(s - jnp.max(s, axis=0, keepdims=True))
        acc = _dot(dvt_ref[hh], p.astype(BF16))
        df_out[hh] = _diff_finish_t(acc, tq, lam, nw_ref[...], lambda_init).astype(df_out.dtype)


def _ctx_attn(nq, nk, nv, dqt, dk, dvt, lq, norm_w, na_o, df_o, *, batch, seq, ctx, lambda_init):
    blk0 = batch * seq // ctx
    s256 = pl.BlockSpec((ctx, 256), lambda b: (blk0 + b, 0))
    sh64 = pl.BlockSpec((DIFF_HEADS, ctx, 64), lambda b: (0, blk0 + b, 0))
    sq_t = pl.BlockSpec((DIFF_HEADS, 64, ctx), lambda b: (0, 0, blk0 + b))
    sv_t = pl.BlockSpec((DIFF_HEADS, DIFF_VROWS, ctx), lambda b: (0, 0, blk0 + b))
    return pl.pallas_call(
        functools.partial(_ctx_attn_body, lambda_init=lambda_init),
        grid=(batch,),
        in_specs=[s256, s256, s256, sq_t, sh64, sv_t, _resident(lq.shape), _resident((DIFF_DV, 1)),
                  pl.BlockSpec(memory_space=pl.ANY), pl.BlockSpec(memory_space=pl.ANY)],
        out_specs=[s256, sh64],
        out_shape=[jax.ShapeDtypeStruct(na_o.shape, na_o.dtype), jax.ShapeDtypeStruct(df_o.shape, df_o.dtype)],
        input_output_aliases={8: 0, 9: 1},
        compiler_params=_params("arbitrary"),
        name="ctx_attn",
    )(nq, nk, nv, dqt, dk, dvt, lq, norm_w.reshape(DIFF_DV, 1), na_o, df_o)


def _conv_body(u_ref, prev_ref, next_ref, dw_ref, dwb_ref, lng_ref, lnb_ref, pw_ref, pwb_ref,
               o_ref, pad_ref, sh_ref, *, tiles_per_seq, n_latent_tiles):
    t = u_ref.shape[0]
    i = pl.program_id(0)
    in_ctx = i >= n_latent_tiles
    first = in_ctx | (i % tiles_per_seq == 0)
    last = in_ctx | (i % tiles_per_seq == tiles_per_seq - 1)
    pad_ref[0:HALO, :] = jnp.where(first, 0.0, prev_ref[...])
    pad_ref[HALO:HALO + t, :] = u_ref[...]
    pad_ref[HALO + t:HALO + t + HALO, :] = jnp.where(last, 0.0, next_ref[...])
    base = HALO - CONV_K // 2
    first = {}
    for phase in range(8):
        taps = [k for k in range(CONV_K) if (base + k) % 8 == phase]
        first[phase] = base + taps[0]
        span = base + taps[-1] + t - first[phase]
        sh_ref[phase, 0:span, :] = pad_ref[first[phase]:first[phase] + span, :]
    sub = 64
    parts = []
    for r0 in range(0, t, sub):
        acc = jnp.zeros((sub, CONV_CH), F32)
        for k in range(CONV_K):
            phase = (base + k) % 8
            off = base + k - first[phase] + r0
            acc = acc + dw_ref[k:k + 1, :] * sh_ref[phase, off:off + sub, :]
        parts.append(acc)
    y = jnp.concatenate(parts, axis=0) + dwb_ref[...]
    mu = jnp.mean(y, axis=-1, keepdims=True)
    yc = y - mu
    var = jnp.mean(yc * yc, axis=-1, keepdims=True)
    y = _silu(yc * lax.rsqrt(var + LN_EPS) * lng_ref[...] + lnb_ref[...])
    o_ref[...] = (_dot(y.astype(BF16), pw_ref[...]) + pwb_ref[...]).astype(o_ref.dtype)


def _conv(cu, dw, dw_b, ln_g, ln_b, pw, pw_b, *, batch, seq, ctx):
    r = cu.shape[0]
    t = CONV_TILE
    assert ctx == t and seq % t == 0
    n_tiles = r // t
    hb = t // HALO
    vec = lambda a: a.reshape(1, CONV_CH)
    return pl.pallas_call(
        functools.partial(_conv_body, tiles_per_seq=seq // t, n_latent_tiles=batch * seq // t),
        grid=(n_tiles,),
        in_specs=[pl.BlockSpec((t, CONV_CH), lambda i: (i, 0)),
                  pl.BlockSpec((HALO, CONV_CH), lambda i: (jnp.maximum(i * hb - 1, 0), 0)),
                  pl.BlockSpec((HALO, CONV_CH), lambda i: (jnp.minimum((i + 1) * hb, n_tiles * hb - 1), 0)),
                  _resident((CONV_K, CONV_CH))] + [_resident((1, CONV_CH))] * 3
                 + [_resident((CONV_CH, CONV_CH)), _resident((1, CONV_CH))],
        out_specs=pl.BlockSpec((t, CONV_CH), lambda i: (i, 0)),
        out_shape=jax.ShapeDtypeStruct((r, CONV_CH), BF16),
        scratch_shapes=[pltpu.VMEM((t + 2 * HALO, CONV_CH), F32),
                        pltpu.VMEM((8, t + 2 * HALO, CONV_CH), F32)],
        compiler_params=_params("parallel"),
        name="conv",
    )(cu, cu, cu, dw, vec(dw_b), vec(ln_g), vec(ln_b), pw.astype(BF16), vec(pw_b))


def kernel(x, c, ctx, c_ctx, ada_w, ada_b, norm_ffn1, ffn1_w13, ffn1_w2, norm_mix, w_in, gla_wa_f, gla_ba_f, gla_wa_b, gla_ba_b, gla_norm, na_rpb, diff_lq1, diff_lk1, diff_lq2, diff_lk2, diff_norm, conv_dw, conv_dw_b, conv_ln_g, conv_ln_b, conv_pw, conv_pw_b, w_out, norm_ffn2, ffn2_w13, ffn2_w2, final_norm):
    batch, seq, d = x.shape
    n_ctx = ctx.shape[1]
    depth = ada_w.shape[0]
    tm = ROW_TILE
    assert seq % tm == 0 and (batch * n_ctx) % tm == 0 and batch + 1 <= 8
    lat_tiles = batch * seq // tm
    all_tiles = lat_tiles + batch * n_ctx // tm
    tiles_per_batch = seq // tm

    def group_of(i):
        return jnp.minimum(i // tiles_per_batch, batch)

    def pos_of(i):
        return jnp.where(i < lat_tiles, i % tiles_per_batch, tiles_per_batch)

    c_rows = jnp.concatenate([c, c_ctx[None, :], jnp.zeros((8 - batch - 1, d), F32)], axis=0)
    mods_all = _ada(c_rows, ada_w, ada_b)[:, :batch + 1].reshape(depth, batch + 1, N_MOD, d)
    rope = _rope_tables(seq, tm)
    h = x.reshape(batch * seq, d)
    h_ctx = ctx.reshape(batch * n_ctx, d)

    for i in range(depth):
        last = i == depth - 1
        lambda_init = 0.8 - 0.6 * math.exp(-0.3 * i)
        mods = mods_all[i]
        tok = dict(n_tiles=all_tiles, group_of=group_of)
        geo = dict(batch=batch, seq=seq, ctx=n_ctx)

        h = _ffn(h, mods, norm_ffn1[i], ffn1_w13[i], ffn1_w2[i], final_norm, k0=0, final=False,
                 h_ctx=h_ctx if i == 0 else None, **tok)

        wa, ba = _gate_weights(gla_wa_f[i], gla_ba_f[i], gla_wa_b[i], gla_ba_b[i])
        (gqk, gv, gvt, gg, glg, nq, nk, nv, dq, dk, dv, cu) = _proj(
            h, mods, norm_mix[i], _permute_w_in(w_in[i]), wa, ba, rope, pos_of=pos_of, **tok)

        gnorm = jnp.tile(gla_norm[i], GLA_HEADS).reshape(1, GLA_HEADS * GLA_DV)
        o_f = _gla(gqk, gv, gvt, glg, gg, None, gnorm, reverse=False, **geo)
        gx = _gla(gqk, gv, gvt, glg, gg, o_f, gnorm, reverse=True, **geo)

        nx = _na(nq, nk, nv, _na_col_tables(na_rpb[i]), **geo)

        lq = jnp.stack([diff_lq1[i], diff_lk1[i], diff_lq2[i], diff_lk2[i]])
        dx = _diff(dq, dk, dv, lq, diff_norm[i], lambda_init=lambda_init, **geo)
        if not last:
            nx, dx = _ctx_attn(nq, nk, nv, dq, dk, dv, lq, diff_norm[i], nx, dx,
                               lambda_init=lambda_init, **geo)

        cx = _conv(cu, conv_dw[i], conv_dw_b[i], conv_ln_g[i], conv_ln_b[i], conv_pw[i], conv_pw_b[i], **geo)

        if last:
            tok = dict(n_tiles=lat_tiles, group_of=group_of)
        h = _ffn(h, mods, norm_ffn2[i], ffn2_w13[i], ffn2_w2[i], final_norm, k0=6, final=last,
                 mixers=(gx, nx, dx, cx, w_out[i], (batch, seq, n_ctx)), **tok)

    return h.reshape(batch, seq, d)
```

```python
import functools
import math

import jax
import jax.numpy as jnp
from jax import lax
from jax.experimental import pallas as pl
from jax.experimental.pallas import tpu as pltpu

F32 = jnp.float32
BF16 = jnp.bfloat16

GRID_W = 64
N_MOD = 9
RMS_EPS = 1e-6
LN_EPS = 1e-5
NEG_INF = -1e30
GLA_HEADS, GLA_DK, GLA_DV, GLA_RANK, GLA_TAU, GLA_CHUNK = 4, 32, 64, 16, 16.0, 64
NA_HEADS, NA_DIM, NA_ROWS, NA_COLS = 4, 64, 8, 16
DIFF_HEADS, DIFF_DK, DIFF_DV = 4, 32, 64
DIFF_VROWS = 80
CONV_CH, CONV_K = 256, 31
ROPE_BASE = 10000.0
LOG2E = 1.4426950408889634

LANES = 128
VMEM_LIMIT = 56 * 1024 * 1024

ROW_TILE = 512
FF_CHUNK = 256
GLA_BLOCK = 256
NA_QROWS = 8
NA_GROUPS = 4
NA_KBLK = 256
CONV_TILE = 256
DIFF_TQ = 256
DIFF_QTILES = 8
DIFF_TK = 256
DIFF_UNROLL = 32
DIFF_AHEAD = 4
DIFF_RING = 8
HALO = 16

C_GLA, C_NA, C_DIFF, C_CONV, C_AUX, C_END = 0, 768, 1536, 2304, 2816, 2944


def _dot(a, b):
    return jnp.dot(a, b, preferred_element_type=F32)


def _dot_nt(a, b):
    return lax.dot_general(a, b, (((1,), (1,)), ((), ())), preferred_element_type=F32)


def _params(*sem):
    return pltpu.CompilerParams(dimension_semantics=sem, vmem_limit_bytes=VMEM_LIMIT)


def _resident(shape):
    nd = len(shape)
    return pl.BlockSpec(shape, lambda *_: (0,) * nd, pipeline_mode=pl.Buffered(1))


def _silu(x):
    return x * jax.nn.sigmoid(x)


def _rms(x, w):
    return x * lax.rsqrt(jnp.mean(x * x, axis=-1, keepdims=True) + RMS_EPS) * w


def _ada_body(c_ref, w_ref, b_ref, o_ref):
    s = _silu(c_ref[...])
    o_ref[...] = jnp.dot(s, w_ref[...], precision=lax.Precision.HIGHEST,
                         preferred_element_type=F32) + b_ref[...]


def _ada(c_rows, ada_w, ada_b):
    depth, d, _ = ada_w.shape
    return pl.pallas_call(
        _ada_body,
        grid=(depth, N_MOD),
        in_specs=[pl.BlockSpec((8, d), lambda l, n: (0, 0)),
                  pl.BlockSpec((None, d, d), lambda l, n: (l, 0, n)),
                  pl.BlockSpec((None, 1, d), lambda l, n: (l, 0, n))],
        out_specs=pl.BlockSpec((None, 8, d), lambda l, n: (l, 0, n)),
        out_shape=jax.ShapeDtypeStruct((depth, 8, N_MOD * d), F32),
        compiler_params=_params("arbitrary", "arbitrary"),
        name="ada",
    )(c_rows, ada_w, ada_b.reshape(depth, 1, N_MOD * d))


def _ffn_body(*refs, k0, final, source, n_first):
    h_ref = refs[0]
    n_extra = {"plain": 0, "two_arrays": 1, "mixers": 6}[source]
    extra = refs[1:1 + n_extra]
    mod_ref, nw_ref, w13_ref, w2_ref, fw_ref, o_ref, xb_ref, g_ref, h13_ref = refs[1 + n_extra:]
    ff = w2_ref.shape[0]
    tf = h13_ref.shape[2] // 2
    n_chunks = ff // tf
    x = h_ref[...]
    if source == "two_arrays":
        x = jnp.where(pl.program_id(0) < n_first, x, extra[0][...])
    elif source == "mixers":
        gx_ref, gxc_ref, nx_ref, dx_ref, cx_ref, wo_ref = extra
        gx = jnp.where(pl.program_id(0) < n_first, gx_ref[...], gxc_ref[...].reshape(gx_ref.shape))
        mix = jnp.concatenate([gx, nx_ref[...]] + [dx_ref[hh] for hh in range(DIFF_HEADS)]
                              + [cx_ref[...]], axis=-1)
        x = x + mod_ref[5:6, :] * _dot(mix, wo_ref[...])
    xm = _rms(x, nw_ref[...]) * (1.0 + mod_ref[k0 + 1:k0 + 2, :]) + mod_ref[k0:k0 + 1, :]
    xb_ref[...] = xm.astype(BF16)

    def cols(c):
        return c * tf if isinstance(c, int) else pl.multiple_of(c * tf, tf)

    def up(c, slot):
        h13_ref[slot, :, :tf] = _dot(xb_ref[...], w13_ref[:, pl.ds(cols(c), tf)])
        h13_ref[slot, :, tf:] = _dot(xb_ref[...], w13_ref[:, pl.ds(ff + cols(c), tf)])

    def gate(c, slot):
        a = h13_ref[slot, :, :tf]
        u = h13_ref[slot, :, tf:]
        g_ref[:, pl.ds(cols(c), tf)] = (_silu(a) * u).astype(BF16)

    def pair(t, carry):
        up(2 * t + 1, 1)
        gate(2 * t, 0)
        up(2 * t + 2, 0)
        gate(2 * t + 1, 1)
        return carry

    up(0, 0)
    for t in range((n_chunks - 1) // 2):
        pair(t, 0)
    gate(n_chunks - 1, 0)
    out = x + (0.5 * mod_ref[k0 + 2:k0 + 3, :]) * _dot(g_ref[...], w2_ref[...])
    if final:
        out = _rms(out, fw_ref[...])
    o_ref[...] = out


def _ffn(h, mods, norm_w, w13, w2, final_w, *, k0, n_tiles, group_of, final, h_ctx=None, mixers=None):
    d = h.shape[1]
    tm = ROW_TILE
    ff = w2.shape[0]
    n_chunks = ff // FF_CHUNK
    assert ff % FF_CHUNK == 0 and n_chunks % 2 == 1
    row = lambda i: (i, 0)
    source, n_first, extra, extra_specs, h_spec = "plain", None, [], [], pl.BlockSpec((tm, d), row)
    if h_ctx is not None:
        source, n_first, extra = "two_arrays", h.shape[0] // tm, [h_ctx]
        h_spec = pl.BlockSpec((tm, d), lambda i: (jnp.minimum(i, n_first - 1), 0))
        extra_specs = [pl.BlockSpec((tm, d), lambda i: (jnp.maximum(i - n_first, 0), 0))]
    elif mixers is not None:
        gx, nx, dx, cx, w_out, (batch, seq, n_ctx) = mixers
        assert batch * n_ctx == tm and seq % tm == 0 and seq % n_ctx == 0
        source, n_first, extra = "mixers", batch * seq // tm, [gx, gx, nx, dx, cx, w_out.astype(BF16)]
        per_seq = seq // tm
        extra_specs = [pl.BlockSpec((None, tm, 256),
                                    lambda i: (jnp.minimum(i // per_seq, batch - 1), i % per_seq, 0)),
                       pl.BlockSpec((batch, n_ctx, 256), lambda i: (0, seq // n_ctx, 0)),
                       pl.BlockSpec((tm, 256), row),
                       pl.BlockSpec((DIFF_HEADS, tm, 64), lambda i: (0, i, 0)),
                       pl.BlockSpec((tm, 256), row), _resident(w_out.shape)]
    return pl.pallas_call(
        functools.partial(_ffn_body, k0=k0, final=final, source=source, n_first=n_first),
        grid=(n_tiles,),
        in_specs=[h_spec] + extra_specs + [
            pl.BlockSpec((None, N_MOD, d), lambda i: (group_of(i), 0, 0)),
            _resident((1, d)),
            _resident(w13.shape),
            _resident(w2.shape),
            _resident((1, d))],
        out_specs=pl.BlockSpec((tm, d), row),
        out_shape=jax.ShapeDtypeStruct((n_tiles * tm, d), F32),
        scratch_shapes=[pltpu.VMEM((tm, d), BF16), pltpu.VMEM((tm, ff), BF16),
                        pltpu.VMEM((2, tm, 2 * FF_CHUNK), F32)],
        compiler_params=_params("parallel"),
        name="ffn",
    )(h, *extra, mods, norm_w.reshape(1, d), w13.astype(BF16), w2.astype(BF16), final_w.reshape(1, d))


def _log_sigmoid(x):
    return jnp.minimum(x, 0.0) - jnp.log1p(jnp.exp(-jnp.abs(x)))


def _rope_rotate(x):
    n = x.shape[-1]
    lane = lax.broadcasted_iota(jnp.int32, x.shape, 1)
    up = pltpu.roll(x, n - 8, 1)
    dn = pltpu.roll(x, 8, 1)
    return jnp.where((lane & 15) < 8, -up, dn)


def _rope_rotate_rows(x):
    n = x.shape[0]
    row = lax.broadcasted_iota(jnp.int32, x.shape, 0)
    up = pltpu.roll(x, n - 8, 0)
    dn = pltpu.roll(x, 8, 0)
    return jnp.where((row & 15) < 8, -up, dn)


def _proj_body(h_ref, mod_ref, nw_ref, w_ref, wt_ref, wa_ref, ba_ref,
               cos_ref, sin_ref, cost_ref, sint_ref,
               gqk_ref, gv_ref, gvt_ref, gg_ref, glg_ref, nq_ref, nk_ref, nv_ref,
               dqt_ref, dk_ref, dvt_ref, cu_ref, xb_ref):
    x = h_ref[...]
    tm = x.shape[0]
    xm = _rms(x, nw_ref[...]) * (1.0 + mod_ref[4:5, :]) + mod_ref[3:4, :]
    xb_ref[...] = xm.astype(BF16)

    z = _dot(xb_ref[...], w_ref[:, C_GLA:C_GLA + 256])
    lane = lax.broadcasted_iota(jnp.int32, (1, 256), 1)
    gqk_ref[...] = z * jnp.where(lane < 128, GLA_DK ** -0.5, 1.0)
    gv_ref[...] = _dot(xb_ref[...], w_ref[:, C_GLA + 256:C_GLA + 512]).astype(BF16)
    zt = _dot_nt(wt_ref[...], xb_ref[...])
    gvt_ref[...] = zt[0:256, :].astype(BF16)
    gg_ref[...] = _dot(xb_ref[...], w_ref[:, C_GLA + 512:C_GLA + 768]).astype(BF16)
    aux = _dot(xb_ref[...], w_ref[:, C_AUX:C_END])
    pre = _dot(aux.astype(BF16), wa_ref[...]) + ba_ref[...]
    glg_ref[...] = _log_sigmoid(pre) * (1.0 / GLA_TAU)

    nq_ref[...] = (_dot(xb_ref[...], w_ref[:, C_NA:C_NA + 256]) * (NA_DIM ** -0.5 * LOG2E)).astype(BF16)
    nk_ref[...] = _dot(xb_ref[...], w_ref[:, C_NA + 256:C_NA + 512]).astype(BF16)
    nv_ref[...] = _dot(xb_ref[...], w_ref[:, C_NA + 512:C_NA + 768]).astype(BF16)

    cos = cos_ref[...]
    sin = sin_ref[...]
    cos2 = jnp.concatenate([cos, cos], axis=1)
    sin2 = jnp.concatenate([sin, sin], axis=1)
    zk = _dot(xb_ref[...], w_ref[:, C_DIFF + 256:C_DIFF + 512])
    zk = zk * cos2 + _rope_rotate(zk) * sin2
    n_rep = 2 * DIFF_HEADS
    cos_t = jnp.concatenate([cost_ref[...]] * n_rep, axis=0)
    sin_t = jnp.concatenate([sint_ref[...]] * n_rep, axis=0)
    zqt = zt[256:512, :]
    zqt = (zqt * cos_t + _rope_rotate_rows(zqt) * sin_t) * (DIFF_DK ** -0.5 * LOG2E)
    zvt = zt[512:768, :]
    pad_rows = DIFF_VROWS - DIFF_DV
    one_row = jnp.where(lax.broadcasted_iota(jnp.int32, (pad_rows, tm), 0) == 0, 1.0, 0.0).astype(BF16)
    for hh in range(DIFF_HEADS):
        sl = slice(64 * hh, 64 * hh + 64)
        dk_ref[hh] = zk[:, sl].astype(BF16)
        dqt_ref[hh] = zqt[sl, :].astype(BF16)
        dvt_ref[hh, 0:64, :] = zvt[sl, :].astype(BF16)
        dvt_ref[hh, DIFF_DV:DIFF_VROWS, :] = one_row

    za = _dot(xb_ref[...], w_ref[:, C_CONV:C_CONV + 256])
    zg = _dot(xb_ref[...], w_ref[:, C_CONV + 256:C_CONV + 512])
    cu_ref[...] = za * jax.nn.sigmoid(zg)


def _proj(h, mods, norm_w, w_p, wa, ba, rope, *, n_tiles, group_of, pos_of):
    r, d = h.shape
    tm = ROW_TILE
    row = lambda i: (i, 0)
    hrow = lambda i: (0, i, 0)
    hcol = lambda i: (0, 0, i)
    f32_256 = jax.ShapeDtypeStruct((r, 256), F32)
    bf_256 = jax.ShapeDtypeStruct((r, 256), BF16)
    w_t = jnp.concatenate([w_p[:, C_GLA + 256:C_GLA + 512], w_p[:, C_DIFF:C_DIFF + 256],
                           w_p[:, C_DIFF + 512:C_DIFF + 768]], axis=1).T
    cos_r, sin_r, cos_c, sin_c = rope
    out_shape = [f32_256, bf_256, jax.ShapeDtypeStruct((256, r), BF16), bf_256, f32_256,
                 bf_256, bf_256, bf_256,
                 jax.ShapeDtypeStruct((DIFF_HEADS, 64, r), BF16),
                 jax.ShapeDtypeStruct((DIFF_HEADS, r, 64), BF16),
                 jax.ShapeDtypeStruct((DIFF_HEADS, DIFF_VROWS, r), BF16),
                 f32_256]
    out_specs = [pl.BlockSpec((tm, 256), row)] * 2 + [pl.BlockSpec((256, tm), lambda i: (0, i))] + [
        pl.BlockSpec((tm, 256), row)] * 5 + [
        pl.BlockSpec((DIFF_HEADS, 64, tm), hcol),
        pl.BlockSpec((DIFF_HEADS, tm, 64), hrow),
        pl.BlockSpec((DIFF_HEADS, DIFF_VROWS, tm), hcol),
        pl.BlockSpec((tm, 256), row)]
    return pl.pallas_call(
        _proj_body,
        grid=(n_tiles,),
        in_specs=[pl.BlockSpec((tm, d), row),
                  pl.BlockSpec((None, N_MOD, d), lambda i: (group_of(i), 0, 0)),
                  _resident((1, d)),
                  _resident(w_p.shape),
                  _resident(w_t.shape),
                  _resident(wa.shape),
                  _resident(ba.shape),
                  pl.BlockSpec((tm, LANES), lambda i: (pos_of(i), 0)),
                  pl.BlockSpec((tm, LANES), lambda i: (pos_of(i), 0)),
                  pl.BlockSpec((DIFF_DK, tm), lambda i: (0, pos_of(i))),
                  pl.BlockSpec((DIFF_DK, tm), lambda i: (0, pos_of(i)))],
        out_specs=out_specs,
        out_shape=out_shape,
        scratch_shapes=[pltpu.VMEM((tm, d), BF16)],
        compiler_params=_params("parallel"),
        name="proj",
    )(h, mods, norm_w.reshape(1, d), w_p, w_t, wa, ba, cos_r, sin_r, cos_c, sin_c)


def _permute_w_in(w_in):
    d = w_in.shape[0]
    g0 = 2 * GLA_HEADS * GLA_DK + 2 * GLA_HEADS * GLA_DV
    aux = w_in[:, g0:g0 + 2 * GLA_RANK]
    rest = w_in[:, g0 + 2 * GLA_RANK:]
    pad = jnp.zeros((d, C_END - C_AUX - 2 * GLA_RANK), w_in.dtype)
    return jnp.concatenate([w_in[:, :g0], rest, aux, pad], axis=1).astype(BF16)


def _gate_weights(wa_f, ba_f, wa_b, ba_b):
    n = GLA_HEADS * GLA_DK
    wa = jnp.zeros((C_END - C_AUX, 2 * n), F32)
    wa = wa.at[:GLA_RANK, :n].set(wa_f).at[GLA_RANK:2 * GLA_RANK, n:].set(wa_b)
    return wa.astype(BF16), jnp.concatenate([ba_f, ba_b]).reshape(1, 2 * n)


def _rope_tables(seq, tile):
    t = jnp.arange(seq)
    row = (t // GRID_W).astype(F32)
    col = (t % GRID_W).astype(F32)
    half = DIFF_DK // 2
    inv = 1.0 / (ROPE_BASE ** (jnp.arange(0, half, 2, dtype=F32) / half))
    ang_r = row[:, None] * inv
    ang_c = col[:, None] * inv
    ang = jnp.concatenate([ang_r, ang_r, ang_c, ang_c], axis=-1)
    cos = jnp.concatenate([jnp.cos(ang), jnp.ones((tile, DIFF_DK), F32)], axis=0)
    sin = jnp.concatenate([jnp.sin(ang), jnp.zeros((tile, DIFF_DK), F32)], axis=0)
    rep = (1, LANES // DIFF_DK)
    return jnp.tile(cos, rep), jnp.tile(sin, rep), cos.T, sin.T


def _split3(x):
    hi = x.astype(BF16)
    r1 = x - hi.astype(F32)
    mid = r1.astype(BF16)
    lo = (r1 - mid.astype(F32)).astype(BF16)
    return hi, mid, lo


def _gla_body(*refs, reverse, batch):
    per_b = [refs[5 * b:5 * b + 5] for b in range(batch)]
    of_ref, nw_ref, o_ref, st_ref = refs[5 * batch:]
    blk = per_b[0][0].shape[0]
    n_chunks = blk // GLA_CHUNK
    nqk = GLA_HEADS * GLA_DK
    nv = GLA_HEADS * GLA_DV

    @pl.when(pl.program_id(0) == 0)
    def _():
        st_ref[...] = jnp.zeros_like(st_ref)

    ri = lax.broadcasted_iota(jnp.int32, (blk, blk), 0)
    ci = lax.broadcasted_iota(jnp.int32, (blk, blk), 1)
    same = (ri // GLA_CHUNK) == (ci // GLA_CHUNK)
    tri = jnp.where(same & ((ci >= ri) if reverse else (ci <= ri)), 1.0, 0.0).astype(BF16)
    bcum_of = []
    for qk_ref, v_ref, vt_ref, lg_ref, g_ref in per_b:
        lg = lg_ref[:, nqk:2 * nqk] if reverse else lg_ref[:, 0:nqk]
        hi, mid, lo = _split3(lg)
        bcum_of.append(_dot(tri, hi) + _dot(tri, mid) + _dot(tri, lo))

    c = GLA_CHUNK
    rk = lax.broadcasted_iota(jnp.int32, (GLA_HEADS * c, nqk), 0) // c
    ck = lax.broadcasted_iota(jnp.int32, (GLA_HEADS * c, nqk), 1) // GLA_DK
    mask_k = rk == ck
    rv = lax.broadcasted_iota(jnp.int32, (GLA_HEADS * c, nv), 0) // c
    cv = lax.broadcasted_iota(jnp.int32, (GLA_HEADS * c, nv), 1) // GLA_DV
    mask_v = rv == cv
    rs = lax.broadcasted_iota(jnp.int32, (nv, nqk), 0) // GLA_DV
    cs = lax.broadcasted_iota(jnp.int32, (nv, nqk), 1) // GLA_DK
    mask_s = rs == cs
    ai = lax.broadcasted_iota(jnp.int32, (c, GLA_HEADS * c), 0)
    aj = lax.broadcasted_iota(jnp.int32, (c, GLA_HEADS * c), 1) % c
    mask_a = (aj >= ai) if reverse else (aj <= ai)

    order = range(n_chunks - 1, -1, -1) if reverse else range(n_chunks)
    units = [(b, ch) for ch in order for b in range(batch)]
    q_in, a_raw, u_t, decay = {}, {}, {}, {}
    vts = [refs_b[2][...].astype(BF16) for refs_b in per_b]
    for b, ch in units:
        qk_ref = per_b[b][0]
        rows = slice(ch * c, (ch + 1) * c)
        k = qk_ref[rows, nqk:2 * nqk]
        bcum = bcum_of[b][rows, :]
        btot = bcum[0:1, :] if reverse else bcum[c - 1:c, :]
        q_in[b, ch] = (qk_ref[rows, 0:nqk] * jnp.exp(bcum)).astype(BF16)
        k_in = k * jnp.exp(-bcum)
        k_out = (k * jnp.exp(btot - bcum)).astype(BF16)
        k_bd = jnp.where(mask_k, jnp.concatenate([k_in] * GLA_HEADS, axis=0), 0.0).astype(BF16)
        a_raw[b, ch] = _dot_nt(q_in[b, ch], k_bd)
        k_pad = jnp.concatenate([jnp.zeros((n * c, nqk), BF16) for n in (ch,) if n] + [k_out]
                                + [jnp.zeros((n * c, nqk), BF16) for n in (n_chunks - 1 - ch,) if n], axis=0)
        u_t[b, ch] = jnp.where(mask_s, _dot(vts[b], k_pad), 0.0)
        decay[b, ch] = jnp.exp(btot)
    o = {}
    for b, ch in units:
        v = per_b[b][1][ch * c:(ch + 1) * c, :]
        v4 = jnp.concatenate([v] * GLA_HEADS, axis=0)
        v_bd = jnp.where(mask_v, v4, jnp.zeros_like(v4))
        o[b, ch] = _dot(jnp.where(mask_a, a_raw[b, ch], 0.0).astype(BF16), v_bd)
    st = [st_ref[b] for b in range(batch)]
    for b, ch in units:
        o[b, ch] = o[b, ch] + _dot_nt(q_in[b, ch], st[b].astype(BF16))
        st[b] = st[b] * decay[b, ch] + u_t[b, ch]
    for b in range(batch):
        st_ref[b] = st[b]
        o_blk = jnp.concatenate([o[b, ch] for ch in range(n_chunks)], axis=0)
        if reverse:
            o_blk = o_blk + of_ref[b]
            hi2, lo2, _ = _split3(o_blk * o_blk)
            hr = lax.broadcasted_iota(jnp.int32, (nv, nv), 0) // GLA_DV
            hc = lax.broadcasted_iota(jnp.int32, (nv, nv), 1) // GLA_DV
            seg = jnp.where(hr == hc, 1.0, 0.0).astype(BF16)
            ms = (_dot(hi2, seg) + _dot(lo2, seg)) * (1.0 / GLA_DV)
            o_blk = o_blk * lax.rsqrt(ms + RMS_EPS) * nw_ref[...] * _silu(per_b[b][4][...].astype(F32))
        o_ref[b] = o_blk.astype(o_ref.dtype)


def _gla(gqk, gv, gvt, glg, gg, o_f, norm_w4, *, reverse, batch, seq, ctx):
    blk = GLA_BLOCK
    nc, nl = ctx // blk, seq // blk
    ctx_base = batch * seq // blk

    def step_blk(s):
        if reverse:
            return jnp.where(s < nc, nl + (nc - 1 - s), nl - 1 - (s - nc))
        return jnp.where(s < nc, nl + s, s - nc)

    def row_blk(b):
        return lambda s: jnp.where(s < nc, ctx_base + b * nc - nl, b * nl) + step_blk(s)

    specs, args = [], []
    for b in range(batch):
        spec = pl.BlockSpec((blk, 256), lambda s, f=row_blk(b): (f(s), 0))
        spec_t = pl.BlockSpec((256, blk), lambda s, f=row_blk(b): (0, f(s)))
        if reverse:
            specs += [spec, spec, spec_t, spec, spec]
            args += [gqk, gv, gvt, glg, gg]
        else:
            specs += [spec, spec, spec_t, spec, _resident((1, 256))]
            args += [gqk, gv, gvt, glg, norm_w4]
    seq_spec = pl.BlockSpec((batch, blk, 256), lambda s: (0, step_blk(s), 0))
    if o_f is None:
        o_f, of_spec = norm_w4, _resident((1, 256))
    else:
        of_spec = seq_spec
    return pl.pallas_call(
        functools.partial(_gla_body, reverse=reverse, batch=batch),
        grid=(nc + nl,),
        in_specs=specs + [of_spec, _resident((1, 256))],
        out_specs=seq_spec,
        out_shape=jax.ShapeDtypeStruct((batch, seq + ctx, 256), BF16 if reverse else F32),
        scratch_shapes=[pltpu.VMEM((batch, GLA_HEADS * GLA_DV, GLA_HEADS * GLA_DK), F32)],
        compiler_params=_params("arbitrary"),
        name="gla_bwd" if reverse else "gla_fwd",
    )(*args, o_f, norm_w4)


def _na_col_tables(rpb):
    col = jnp.arange(GRID_W)
    cstart = jnp.clip(col - NA_COLS // 2, 0, GRID_W - NA_COLS)
    col_ok = (col[None, :] >= cstart[:, None]) & (col[None, :] < cstart[:, None] + NA_COLS)
    col_off = jnp.clip(col[None, :] - col[:, None] + (NA_COLS - 1), 0, 2 * NA_COLS - 2)
    t = jnp.where(col_ok[None, None], rpb[:, :, col_off] * LOG2E, NEG_INF)
    dead = jnp.full((NA_HEADS, 1, GRID_W, GRID_W), NEG_INF, F32)
    t = jnp.concatenate([dead, t.astype(F32), dead], axis=1)
    return jnp.concatenate([t[:, :-1], t[:, 1:]], axis=-1)


def _na_body(q_ref, k0, k1, k2, k3, v0, v1, v2, v3, kc_ref, vc_ref, tab_ref, o_ref, *, rows, key_blocks):
    k_blk = [k0[...], k1[...], k2[...], k3[...]]
    v_blk = [v0[...], v1[...], v2[...], v3[...]]
    kc = kc_ref[...]
    vc = vc_ref[...]

    j = pl.program_id(1)
    rows_per_blk = NA_KBLK // GRID_W
    r0 = j * NA_QROWS
    kr0 = jnp.clip(2 * j - 1, 0, key_blocks - 4) * rows_per_blk
    lane_lo = lax.broadcasted_iota(jnp.int32, (1, 2 * GRID_W), 1) < GRID_W
    lane = lax.broadcasted_iota(jnp.int32, (1, NA_HEADS * NA_DIM), 1) // NA_DIM
    half = NA_QROWS // NA_GROUPS
    n_pairs = 3 * rows_per_blk // 2
    gq = half * GRID_W
    for g in range(NA_GROUPS):
        first = jnp.clip(r0 + g * half - NA_ROWS // 2, 0, rows - NA_ROWS)
        shift = jnp.clip((first - kr0) // rows_per_blk, 0, 1)
        kw = jnp.concatenate([jnp.where(shift == 0, k_blk[i], k_blk[i + 1]) for i in range(3)], axis=0)
        vw = jnp.concatenate([jnp.where(shift == 0, v_blk[i], v_blk[i + 1]) for i in range(3)], axis=0)
        kg0 = kr0 + shift * rows_per_blk
        entry, ok = [], []
        for a in range(half):
            r = r0 + g * half + a
            start = jnp.clip(r - NA_ROWS // 2, 0, rows - NA_ROWS)
            for bp in range(n_pairs):
                rk = kg0 + 2 * bp
                entry.append(jnp.clip(rk - r + NA_ROWS, 0, 2 * NA_ROWS - 1))
                in0 = ((rk >= start) & (rk < start + NA_ROWS)).astype(jnp.int32)
                in1 = ((rk + 1 >= start) & (rk + 1 < start + NA_ROWS)).astype(jnp.int32)
                ok.append(jnp.where(lane_lo, in0, in1) != 0)

        def bias_of(hh):
            rows_ = []
            for a in range(half):
                tiles = [jnp.where(ok[a * n_pairs + bp], tab_ref[hh, entry[a * n_pairs + bp]], NEG_INF)
                         for bp in range(n_pairs)]
                rows_.append(jnp.concatenate(tiles, axis=1))
            return jnp.concatenate(rows_, axis=0)

        q = q_ref[g * gq:(g + 1) * gq, :]
        out = jnp.zeros((gq, NA_HEADS * NA_DIM), F32)
        for hh in range(NA_HEADS):
            mh = lane == hh
            qh = jnp.where(mh, q, jnp.zeros_like(q))
            s_w = _dot_nt(qh, kw) + bias_of(hh)
            s_c = _dot_nt(qh, kc)
            m = jnp.maximum(jnp.max(s_w, axis=-1, keepdims=True), jnp.max(s_c, axis=-1, keepdims=True))
            p_w = jnp.exp2(s_w - m)
            p_c = jnp.exp2(s_c - m)
            l = jnp.sum(p_w, axis=-1, keepdims=True) + jnp.sum(p_c, axis=-1, keepdims=True)
            o = _dot(p_w.astype(BF16), vw) + _dot(p_c.astype(BF16), vc)
            out = out + jnp.where(mh, o / l, 0.0)
        o_ref[g * gq:(g + 1) * gq, :] = out.astype(o_ref.dtype)


def _na(nq, nk, nv, tab, *, batch, seq, ctx):
    r = nq.shape[0]
    tq = NA_QROWS * GRID_W
    nj = seq // tq
    kb = seq // NA_KBLK
    assert kb >= 4 and seq % tq == 0 and seq % ctx == 0
    ctx_blk0 = batch * seq // ctx

    def kspec(i):
        return pl.BlockSpec((NA_KBLK, 256),
                            lambda b, j: (b * kb + jnp.clip(2 * j - 1, 0, kb - 4) + i, 0))

    cspec = pl.BlockSpec((ctx, 256), lambda b, j: (ctx_blk0 + b, 0))
    return pl.pallas_call(
        functools.partial(_na_body, rows=seq // GRID_W, key_blocks=kb),
        grid=(batch, nj),
        in_specs=[pl.BlockSpec((tq, 256), lambda b, j: (b * nj + j, 0))]
                 + [kspec(i) for i in range(4)] + [kspec(i) for i in range(4)]
                 + [cspec, cspec, _resident(tab.shape)],
        out_specs=pl.BlockSpec((tq, 256), lambda b, j: (b * nj + j, 0)),
        out_shape=jax.ShapeDtypeStruct((r, 256), BF16),
        compiler_params=_params("parallel", "arbitrary"),
        name="na",
    )(nq, nk, nk, nk, nk, nv, nv, nv, nv, nk, nv, tab)


def _diff_lambda(lq_ref, lambda_init):
    lq = lq_ref[...]
    s1 = jnp.sum(lq[0:1, :] * lq[1:2, :], axis=-1, keepdims=True)
    s2 = jnp.sum(lq[2:3, :] * lq[3:4, :], axis=-1, keepdims=True)
    return jnp.exp(s1) - jnp.exp(s2) + lambda_init


def _stack_qt(qt):
    row = lax.broadcasted_iota(jnp.int32, (2 * DIFF_DK, 1), 0)
    zero = jnp.zeros_like(qt)
    return jnp.concatenate([jnp.where(row < DIFF_DK, qt, zero), jnp.where(row >= DIFF_DK, qt, zero)], axis=1)


def _diff_finish_t(acc, tq, lam, nw_col, lambda_init):
    o = acc[0:DIFF_DV, :] / acc[DIFF_DV:DIFF_DV + 1, :]
    od = o[:, :tq] - lam * o[:, tq:]
    ms = jnp.mean(od * od, axis=0, keepdims=True)
    y = od * lax.rsqrt(ms + RMS_EPS) * nw_col * (1.0 - lambda_init)
    y = jnp.concatenate([y, jnp.zeros_like(y)], axis=0)
    return jnp.transpose(y)[:, 0:DIFF_DV]


def _diff_body(qt_ref, kc_ref, vtc_ref, k_ref, vt_ref, lq_ref, nw_ref, o_ref, s_ref, *, tq, tk, unroll, lambda_init):
    n_tiles = qt_ref.shape[1] // tq
    lam = _diff_lambda(lq_ref, lambda_init)

    def finish(acc, t):
        q0 = t * tq if isinstance(t, int) else pl.multiple_of(t * tq, tq)
        o_ref[pl.ds(q0, tq), :] = _diff_finish_t(acc, tq, lam, nw_ref[...], lambda_init).astype(o_ref.dtype)

    def tile(t, acc_prev):
        q0 = pl.multiple_of(t * tq, tq)
        return _diff_tile(qt_ref[:, pl.ds(q0, tq)], kc_ref, vtc_ref, k_ref, vt_ref, s_ref, tk=tk, unroll=unroll,
                          after_issue=lambda: finish(acc_prev, jnp.maximum(t - 1, 0)))

    placeholder = jnp.ones((vt_ref.shape[0], 2 * tq), F32)
    finish(lax.fori_loop(0, n_tiles, tile, placeholder), n_tiles - 1)


def _diff_tile(qt, kc_ref, vtc_ref, k_ref, vt_ref, s_ref, *, tk, unroll, after_issue):
    tq = qt.shape[1]
    n_chunks = k_ref.shape[0] // tk
    ring = s_ref.shape[0]
    qs = _stack_qt(qt)

    def chunk(j):
        return pl.ds(j * tk if isinstance(j, int) else pl.multiple_of(j * tk, tk), tk)

    def issue_scores(j, slot):
        s = _dot(k_ref[chunk(j), :], qs)
        s_ref[slot] = s
        return jnp.max(s, axis=0, keepdims=True)

    s_ctx = _dot(kc_ref[...], qs)
    m0 = jnp.max(s_ctx, axis=0, keepdims=True)
    cmax0 = tuple(issue_scores(min(a, n_chunks - 1), a % ring) for a in range(DIFF_AHEAD))
    after_issue()
    acc0 = _dot(vtc_ref[...], jnp.exp2(s_ctx - m0).astype(BF16))

    def group(g, carry, tail=False):
        cmax, m, acc = carry
        for u in range(unroll):
            j = g * unroll + u
            if tail and j + DIFF_AHEAD >= n_chunks:
                c_new = cmax[0]
            else:
                c_new = issue_scores(j + DIFF_AHEAD, (u + DIFF_AHEAD) % ring)
            m_new = jnp.maximum(m, cmax[0])
            p = jnp.exp2(s_ref[u % ring] - m_new).astype(BF16)
            vt = vt_ref[:, chunk(j)]
            acc = jnp.exp2(m - m_new) * acc + _dot(vt, p)
            m, cmax = m_new, cmax[1:] + (c_new,)
        return cmax, m, acc

    n_groups = n_chunks // unroll
    trips = jnp.minimum(pl.program_id(2) + 1, 1) * (n_groups - 1)
    carry = lax.fori_loop(0, trips, group, (cmax0, m0, acc0))
    _, _, acc = group(n_groups - 1, carry, tail=True)
    return acc


def _diff(dqt, dk, dvt, lq, norm_w, *, batch, seq, ctx, lambda_init):
    h, r, _ = dk.shape
    tq = DIFF_TQ
    tqs = tq * min(DIFF_QTILES, seq // tq)
    nq = seq // tqs
    tk = min(DIFF_TK, seq)
    unroll = min(DIFF_UNROLL, seq // tk)
    assert (seq // tk) % unroll == 0 and unroll % DIFF_RING == 0 and DIFF_RING > DIFF_AHEAD and seq % tqs == 0
    ctx_blk0 = batch * seq // ctx
    return pl.pallas_call(
        functools.partial(_diff_body, tq=tq, tk=tk, unroll=unroll, lambda_init=lambda_init),
        grid=(batch, h, nq),
        in_specs=[pl.BlockSpec((None, 64, tqs), lambda b, hh, i: (hh, 0, b * nq + i)),
                  pl.BlockSpec((None, ctx, 64), lambda b, hh, i: (hh, ctx_blk0 + b, 0)),
                  pl.BlockSpec((None, DIFF_VROWS, ctx), lambda b, hh, i: (hh, 0, ctx_blk0 + b)),
                  pl.BlockSpec((None, seq, 64), lambda b, hh, i: (hh, b, 0)),
                  pl.BlockSpec((None, DIFF_VROWS, seq), lambda b, hh, i: (hh, 0, b)),
                  _resident(lq.shape),
                  _resident((DIFF_DV, 1))],
        out_specs=pl.BlockSpec((None, tqs, 64), lambda b, hh, i: (hh, b * nq + i, 0)),
        out_shape=jax.ShapeDtypeStruct((h, r, 64), BF16),
        scratch_shapes=[pltpu.VMEM((DIFF_RING, tk, 2 * tq), F32)],
        compiler_params=_params("parallel", "parallel", "arbitrary"),
        name="diff",
    )(dqt, dk, dvt, dk, dvt, lq, norm_w.reshape(DIFF_DV, 1))


def _ctx_attn_body(nq_ref, nk_ref, nv_ref, dqt_ref, dk_ref, dvt_ref, lq_ref, nw_ref, na_in, df_in,
                   na_out, df_out, *, lambda_init):
    del na_in, df_in
    q = nq_ref[...]
    k = nk_ref[...]
    v = nv_ref[...]
    lane = lax.broadcasted_iota(jnp.int32, (1, NA_HEADS * NA_DIM), 1) // NA_DIM
    out = jnp.zeros(na_out.shape, F32)
    for hh in range(NA_HEADS):
        mh = lane == hh
        s = _dot_nt(jnp.where(mh, q, jnp.zeros_like(q)), k)
        p = jnp.exp2(s - jnp.max(s, axis=-1, keepdims=True))
        o = _dot(p.astype(BF16), v) / jnp.sum(p, axis=-1, keepdims=True)
        out = out + jnp.where(mh, o, 0.0)
    na_out[...] = out.astype(na_out.dtype)

    lam = _diff_lambda(lq_ref, lambda_init)
    tq = dqt_ref.shape[2]
    for hh in range(DIFF_HEADS):
        s = _dot(dk_ref[hh], _stack_qt(dqt_ref[hh]))
        p = jnp.exp2(s - jnp.max(s, axis=0, keepdims=True))
        acc = _dot(dvt_ref[hh], p.astype(BF16))
        df_out[hh] = _diff_finish_t(acc, tq, lam, nw_ref[...], lambda_init).astype(df_out.dtype)


def _ctx_attn(nq, nk, nv, dqt, dk, dvt, lq, norm_w, na_o, df_o, *, batch, seq, ctx, lambda_init):
    blk0 = batch * seq // ctx
    s256 = pl.BlockSpec((ctx, 256), lambda b: (blk0 + b, 0))
    sh64 = pl.BlockSpec((DIFF_HEADS, ctx, 64), lambda b: (0, blk0 + b, 0))
    sq_t = pl.BlockSpec((DIFF_HEADS, 64, ctx), lambda b: (0, 0, blk0 + b))
    sv_t = pl.BlockSpec((DIFF_HEADS, DIFF_VROWS, ctx), lambda b: (0, 0, blk0 + b))
    return pl.pallas_call(
        functools.partial(_ctx_attn_body, lambda_init=lambda_init),
        grid=(batch,),
        in_specs=[s256, s256, s256, sq_t, sh64, sv_t, _resident(lq.shape), _resident((DIFF_DV, 1)),
                  pl.BlockSpec(memory_space=pl.ANY), pl.BlockSpec(memory_space=pl.ANY)],
        out_specs=[s256, sh64],
        out_shape=[jax.ShapeDtypeStruct(na_o.shape, na_o.dtype), jax.ShapeDtypeStruct(df_o.shape, df_o.dtype)],
        input_output_aliases={8: 0, 9: 1},
        compiler_params=_params("arbitrary"),
        name="ctx_attn",
    )(nq, nk, nv, dqt, dk, dvt, lq, norm_w.reshape(DIFF_DV, 1), na_o, df_o)


def _conv_body(u_ref, prev_ref, next_ref, dw_ref, dwb_ref, lng_ref, lnb_ref, pw_ref, pwb_ref,
               o_ref, pad_ref, sh_ref, *, tiles_per_seq, n_latent_tiles):
    t = u_ref.shape[0]
    i = pl.program_id(0)
    in_ctx = i >= n_latent_tiles
    first = in_ctx | (i % tiles_per_seq == 0)
    last = in_ctx | (i % tiles_per_seq == tiles_per_seq - 1)
    pad_ref[0:HALO, :] = jnp.where(first, 0.0, prev_ref[...])
    pad_ref[HALO:HALO + t, :] = u_ref[...]
    pad_ref[HALO + t:HALO + t + HALO, :] = jnp.where(last, 0.0, next_ref[...])
    base = HALO - CONV_K // 2
    first = {}
    for phase in range(8):
        taps = [k for k in range(CONV_K) if (base + k) % 8 == phase]
        first[phase] = base + taps[0]
        span = base + taps[-1] + t - first[phase]
        sh_ref[phase, 0:span, :] = pad_ref[first[phase]:first[phase] + span, :]
    sub = 64
    parts = []
    for r0 in range(0, t, sub):
        acc = jnp.zeros((sub, CONV_CH), F32)
        for k in range(CONV_K):
            phase = (base + k) % 8
            off = base + k - first[phase] + r0
            acc = acc + dw_ref[k:k + 1, :] * sh_ref[phase, off:off + sub, :]
        parts.append(acc)
    y = jnp.concatenate(parts, axis=0) + dwb_ref[...]
    mu = jnp.mean(y, axis=-1, keepdims=True)
    yc = y - mu
    var = jnp.mean(yc * yc, axis=-1, keepdims=True)
    y = _silu(yc * lax.rsqrt(var + LN_EPS) * lng_ref[...] + lnb_ref[...])
    o_ref[...] = (_dot(y.astype(BF16), pw_ref[...]) + pwb_ref[...]).astype(o_ref.dtype)


def _conv(cu, dw, dw_b, ln_g, ln_b, pw, pw_b, *, batch, seq, ctx):
    r = cu.shape[0]
    t = CONV_TILE
    assert ctx == t and seq % t == 0
    n_tiles = r // t
    hb = t // HALO
    vec = lambda a: a.reshape(1, CONV_CH)
    return pl.pallas_call(
        functools.partial(_conv_body, tiles_per_seq=seq // t, n_latent_tiles=batch * seq // t),
        grid=(n_tiles,),
        in_specs=[pl.BlockSpec((t, CONV_CH), lambda i: (i, 0)),
                  pl.BlockSpec((HALO, CONV_CH), lambda i: (jnp.maximum(i * hb - 1, 0), 0)),
                  pl.BlockSpec((HALO, CONV_CH), lambda i: (jnp.minimum((i + 1) * hb, n_tiles * hb - 1), 0)),
                  _resident((CONV_K, CONV_CH))] + [_resident((1, CONV_CH))] * 3
                 + [_resident((CONV_CH, CONV_CH)), _resident((1, CONV_CH))],
        out_specs=pl.BlockSpec((t, CONV_CH), lambda i: (i, 0)),
        out_shape=jax.ShapeDtypeStruct((r, CONV_CH), BF16),
        scratch_shapes=[pltpu.VMEM((t + 2 * HALO, CONV_CH), F32),
                        pltpu.VMEM((8, t + 2 * HALO, CONV_CH), F32)],
        compiler_params=_params("parallel"),
        name="conv",
    )(cu, cu, cu, dw, vec(dw_b), vec(ln_g), vec(ln_b), pw.astype(BF16), vec(pw_b))


def kernel(x, c, ctx, c_ctx, ada_w, ada_b, norm_ffn1, ffn1_w13, ffn1_w2, norm_mix, w_in, gla_wa_f, gla_ba_f, gla_wa_b, gla_ba_b, gla_norm, na_rpb, diff_lq1, diff_lk1, diff_lq2, diff_lk2, diff_norm, conv_dw, conv_dw_b, conv_ln_g, conv_ln_b, conv_pw, conv_pw_b, w_out, norm_ffn2, ffn2_w13, ffn2_w2, final_norm):
    batch, seq, d = x.shape
    n_ctx = ctx.shape[1]
    depth = ada_w.shape[0]
    tm = ROW_TILE
    assert seq % tm == 0 and (batch * n_ctx) % tm == 0 and batch + 1 <= 8
    lat_tiles = batch * seq // tm
    all_tiles = lat_tiles + batch * n_ctx // tm
    tiles_per_batch = seq // tm

    def group_of(i):
        return jnp.minimum(i // tiles_per_batch, batch)

    def pos_of(i):
        return jnp.where(i < lat_tiles, i % tiles_per_batch, tiles_per_batch)

    c_rows = jnp.concatenate([c, c_ctx[None, :], jnp.zeros((8 - batch - 1, d), F32)], axis=0)
    mods_all = _ada(c_rows, ada_w, ada_b)[:, :batch + 1].reshape(depth, batch + 1, N_MOD, d)
    rope = _rope_tables(seq, tm)
    h = x.reshape(batch * seq, d)
    h_ctx = ctx.reshape(batch * n_ctx, d)

    for i in range(depth):
        last = i == depth - 1
        lambda_init = 0.8 - 0.6 * math.exp(-0.3 * i)
        mods = mods_all[i]
        tok = dict(n_tiles=all_tiles, group_of=group_of)
        geo = dict(batch=batch, seq=seq, ctx=n_ctx)

        h = _ffn(h, mods, norm_ffn1[i], ffn1_w13[i], ffn1_w2[i], final_norm, k0=0, final=False,
                 h_ctx=h_ctx if i == 0 else None, **tok)

        wa, ba = _gate_weights(gla_wa_f[i], gla_ba_f[i], gla_wa_b[i], gla_ba_b[i])
        (gqk, gv, gvt, gg, glg, nq, nk, nv, dq, dk, dv, cu) = _proj(
            h, mods, norm_mix[i], _permute_w_in(w_in[i]), wa, ba, rope, pos_of=pos_of, **tok)

        gnorm = jnp.tile(gla_norm[i], GLA_HEADS).reshape(1, GLA_HEADS * GLA_DV)
        o_f = _gla(gqk, gv, gvt, glg, gg, None, gnorm, reverse=False, **geo)
        gx = _gla(gqk, gv, gvt, glg, gg, o_f, gnorm, reverse=True, **geo)

        nx = _na(nq, nk, nv, _na_col_tables(na_rpb[i]), **geo)

        lq = jnp.stack([diff_lq1[i], diff_lk1[i], diff_lq2[i], diff_lk2[i]])
        dx = _diff(dq, dk, dv, lq, diff_norm[i], lambda_init=lambda_init, **geo)
        if not last:
            nx, dx = _ctx_attn(nq, nk, nv, dq, dk, dv, lq, diff_norm[i], nx, dx,
                               lambda_init=lambda_init, **geo)

        cx = _conv(cu, conv_dw[i], conv_dw_b[i], conv_ln_g[i], conv_ln_b[i], conv_pw[i], conv_pw_b[i], **geo)

        if last:
            tok = dict(n_tiles=lat_tiles, group_of=group_of)
        h = _ffn(h, mods, norm_ffn2[i], ffn2_w13[i], ffn2_w2[i], final_norm, k0=6, final=last,
                 mixers=(gx, nx, dx, cx, w_out[i], (batch, seq, n_ctx)), **tok)

    return h.reshape(batch, seq, d)
```
